```python
import math, functools
import jax, jax.numpy as jnp
from jax import lax
import numpy as np

D_MODEL = 1024
BATCH = 4
SEQ = 4096
DEPTH = 1

MIX_WIDTH = D_MODEL
HEAD_DIM = 64
ATTN_WIDTH = D_MODEL // 2
N_HEADS_ATTN = ATTN_WIDTH // HEAD_DIM
POOL_WIDTH = D_MODEL // 4
POOL_WINDOWS = (2, 4, 8, 16)
POOL_GROUP = POOL_WIDTH // len(POOL_WINDOWS)
MEM_WIDTH = D_MODEL // 4
MEM_HEADS = 4
MEM_HEAD_DIM = MEM_WIDTH // MEM_HEADS
N_MEM = 256
IN_WIDTH = 3 * ATTN_WIDTH + POOL_WIDTH + MEM_WIDTH

DILATED_PATTERNS = ((128, 1), (512, 4), (2048, 16))
BLOCK = 128

N_GROUPS = 8
EXPERTS_PER_GROUP = 8
N_EXPERTS = N_GROUPS * EXPERTS_PER_GROUP
TOP_K_INNER = 2
EXPERT_HIDDEN = D_MODEL // 2
ROUTE_BLOCK = 128

EPS = 1e-6
NEG_INF = -1e30

kernel_name = "hymba_dilated_pool_memory_hiermoe"


def _rms_norm(x, g):
    xf = x.astype(jnp.float32)
    y = xf * lax.rsqrt(jnp.mean(xf * xf, axis=-1, keepdims=True) + EPS)
    return (y * g.astype(jnp.float32)).astype(x.dtype)


def _banded_window_attention(q, k, v, steps):
    N, L, H, hd = q.shape
    nb = -(-L // BLOCK)
    pad = nb * BLOCK - L

    def blocks(t):
        return jnp.pad(t, ((0, 0), (0, pad), (0, 0), (0, 0))).reshape(N, nb, BLOCK, H, hd)

    def band(t):
        prev = jnp.concatenate([jnp.zeros_like(t[:, :1]), t[:, :-1]], axis=1)
        return jnp.concatenate([prev, t], axis=2)

    qb = blocks(q)
    kk = band(blocks(k))
    vv = band(blocks(v))
    s = jnp.einsum('nbqhd,nbkhd->nbhqk', qb, kk).astype(jnp.float32) * (hd ** -0.5)
    qi = jnp.arange(BLOCK)[:, None]
    kj = jnp.arange(2 * BLOCK)[None, :]
    dist = qi + BLOCK - kj
    in_band = (dist >= 0) & (dist <= steps)
    has_prev = (jnp.arange(nb) > 0)[:, None, None] | (kj >= BLOCK)[None]
    mask = in_band[None] & has_prev
    s = jnp.where(mask[None, :, None], s, NEG_INF)
    m = jnp.max(s, axis=-1, keepdims=True)
    p = jnp.exp(s - m)
    den = jnp.sum(p, axis=-1, keepdims=True)
    o = jnp.einsum('nbhqk,nbkhd->nbqhd', p, vv.astype(jnp.float32))
    o = o / jnp.transpose(den[..., 0], (0, 1, 3, 2))[..., None]
    lse = jnp.transpose((m + jnp.log(den))[..., 0], (0, 1, 3, 2))
    o = o.reshape(N, nb * BLOCK, H, hd)[:, :L]
    lse = lse.reshape(N, nb * BLOCK, H)[:, :L]
    return o, lse


def _dilated_attention_mixture(q, k, v):
    B, S, H, hd = q.shape
    outs, lses = [], []
    for window, dil in DILATED_PATTERNS:
        L = S // dil
        qr = q.reshape(B, L, dil, H, hd).transpose(0, 2, 1, 3, 4).reshape(B * dil, L, H, hd)
        kr = k.reshape(B, L, dil, H, hd).transpose(0, 2, 1, 3, 4).reshape(B * dil, L, H, hd)
        vr = v.reshape(B, L, dil, H, hd).transpose(0, 2, 1, 3, 4).reshape(B * dil, L, H, hd)
        o, lse = _banded_window_attention(qr, kr, vr, window // dil)
        outs.append(o.reshape(B, dil, L, H, hd).transpose(0, 2, 1, 3, 4).reshape(B, S, H, hd))
        lses.append(lse.reshape(B, dil, L, H).transpose(0, 2, 1, 3).reshape(B, S, H))
    alpha = jax.nn.softmax(jnp.stack(lses, axis=0), axis=0)
    return jnp.sum(alpha[..., None] * jnp.stack(outs, axis=0), axis=0)


def _multiscale_pool(u, proj, scale):
    B, S, C = u.shape
    uf = u.astype(jnp.float32)
    c0 = jnp.concatenate([jnp.zeros((B, 1, C), jnp.float32), jnp.cumsum(uf, axis=1)], axis=1)
    groups = []
    for gi, w in enumerate(POOL_WINDOWS):
        sl = slice(gi * POOL_GROUP, (gi + 1) * POOL_GROUP)
        cg = c0[:, :, sl]
        upper = cg[:, 1:]
        lower = jnp.concatenate([jnp.zeros((B, w - 1, POOL_GROUP), jnp.float32), cg[:, :S + 1 - w]], axis=1)
        cnt = jnp.minimum(jnp.arange(1, S + 1), w).astype(jnp.float32)[None, :, None]
        groups.append((upper - lower) / cnt - uf[:, :, sl])
    pooled = jnp.stack(groups, axis=2)
    y = jnp.einsum('bsgc,gce->bsge', pooled, proj.astype(jnp.float32)).reshape(B, S, C)
    return y * scale.astype(jnp.float32)


def _memory_attention(qm, mem, mem_norm, w_mem_kv, mq_norm, mk_norm):
    B, S, _ = qm.shape
    mn = _rms_norm(mem, mem_norm)
    kv = jnp.einsum('bmd,de->bme', mn, w_mem_kv)
    km = kv[..., :MEM_WIDTH].reshape(B, -1, MEM_HEADS, MEM_HEAD_DIM)
    vm = kv[..., MEM_WIDTH:].reshape(B, -1, MEM_HEADS, MEM_HEAD_DIM)
    q = _rms_norm(qm.reshape(B, S, MEM_HEADS, MEM_HEAD_DIM), mq_norm)
    km = _rms_norm(km, mk_norm)
    s = jnp.einsum('bshd,bmhd->bhsm', q, km).astype(jnp.float32) * (MEM_HEAD_DIM ** -0.5)
    p = jax.nn.softmax(s, axis=-1)
    return jnp.einsum('bhsm,bmhd->bshd', p, vm.astype(jnp.float32)).reshape(B, S, MEM_WIDTH)


def _hier_moe(h, w_group, b_group, w_router, b_router, w1, w3, w2):
    T, D = h.shape
    lg = (jnp.einsum('td,dg->tg', h, w_group) + b_group).astype(jnp.float32)
    pg = jax.nn.softmax(lg, axis=-1)
    g_sel = jnp.argmax(lg, axis=-1)
    w_g = jnp.take_along_axis(pg, g_sel[:, None], axis=-1)[:, 0]
    le = (jnp.einsum('td,gde->tge', h, w_router) + b_router).astype(jnp.float32)
    le = jnp.take_along_axis(le, g_sel[:, None, None], axis=1)[:, 0]
    top_v, top_i = lax.top_k(le, TOP_K_INNER)
    gates = w_g[:, None] * jax.nn.softmax(top_v, axis=-1)
    eid = (g_sel[:, None] * EXPERTS_PER_GROUP + top_i).reshape(-1).astype(jnp.int32)
    tok = jnp.repeat(jnp.arange(T, dtype=jnp.int32), TOP_K_INNER)
    gate = gates.reshape(-1)
    order = jnp.argsort(eid)
    eid_s, tok_s, gate_s = eid[order], tok[order], gate[order]
    counts = jnp.bincount(eid, length=N_EXPERTS)
    starts = jnp.cumsum(counts) - counts
    padded = ((counts + ROUTE_BLOCK - 1) // ROUTE_BLOCK) * ROUTE_BLOCK
    pends = jnp.cumsum(padded)
    pstarts = pends - padded
    n_assign = T * TOP_K_INNER
    rank = jnp.arange(n_assign, dtype=jnp.int32) - starts[eid_s]
    dest = pstarts[eid_s] + rank
    R = n_assign + N_EXPERTS * ROUTE_BLOCK
    n_blk = R // ROUTE_BLOCK
    row_tok = jnp.full((R,), T, jnp.int32).at[dest].set(tok_s)
    row_gate = jnp.zeros((R,), jnp.float32).at[dest].set(gate_s)
    blk_exp = jnp.minimum(jnp.searchsorted(pends, jnp.arange(n_blk) * ROUTE_BLOCK, side='right'),
                          N_EXPERTS - 1).astype(jnp.int32)
    h_pad = jnp.concatenate([h, jnp.zeros((1, D), h.dtype)], axis=0)
    xr = h_pad[row_tok].reshape(n_blk, ROUTE_BLOCK, D)

    def expert_block(args):
        xb, e = args
        a = jnp.einsum('rd,df->rf', xb, w1[e])
        b = jnp.einsum('rd,df->rf', xb, w3[e])
        return jnp.einsum('rf,fd->rd', jax.nn.silu(a) * b, w2[e])

    yb = lax.map(expert_block, (xr, blk_exp)).reshape(R, D).astype(jnp.float32)
    y = jax.ops.segment_sum(yb * row_gate[:, None], row_tok, num_segments=T + 1)[:T]
    return y


def _layer(x, mem, attn_norm, w_in, q_norm, k_norm, pool_proj, pool_scale, mem_norm, w_mem_kv,
           mq_norm, mk_norm, w_out, ffn_norm, w_group, b_group, w_router, b_router, w1, w3, w2):
    B, S, D = x.shape
    h = _rms_norm(x, attn_norm)
    z = jnp.einsum('bsd,de->bse', h, w_in)
    a0, a1, a2, a3 = ATTN_WIDTH, 2 * ATTN_WIDTH, 3 * ATTN_WIDTH, 3 * ATTN_WIDTH + POOL_WIDTH
    q = _rms_norm(z[..., :a0].reshape(B, S, N_HEADS_ATTN, HEAD_DIM), q_norm)
    k = _rms_norm(z[..., a0:a1].reshape(B, S, N_HEADS_ATTN, HEAD_DIM), k_norm)
    v = z[..., a1:a2].reshape(B, S, N_HEADS_ATTN, HEAD_DIM)
    u = z[..., a2:a3]
    qm = z[..., a3:]
    y_attn = _dilated_attention_mixture(q, k, v).reshape(B, S, ATTN_WIDTH)
    y_pool = _multiscale_pool(u, pool_proj, pool_scale)
    y_mem = _memory_attention(qm, mem, mem_norm, w_mem_kv, mq_norm, mk_norm)
    mix = jnp.concatenate([y_attn, y_pool, y_mem], axis=-1).astype(x.dtype)
    x = x + jnp.einsum('bse,ed->bsd', mix, w_out)
    h2 = _rms_norm(x, ffn_norm).reshape(B * S, D)
    y = _hier_moe(h2, w_group, b_group, w_router, b_router, w1, w3, w2)
    return x + y.reshape(B, S, D).astype(x.dtype)


def setup_inputs(seed: int = 0) -> dict:
    key = jax.random.key(seed)
    ks = jax.random.split(key, 24)
    f32 = jnp.float32
    L = DEPTH

    def nrm(k, shape, scale):
        return jax.random.normal(k, shape, f32) * scale

    def gain(k, n):
        return 1.0 + 0.02 * jax.random.normal(k, (L, n), f32)

    return {
        "x": jax.random.normal(ks[0], (BATCH, SEQ, D_MODEL), f32),
        "mem": jax.random.normal(ks[1], (BATCH, N_MEM, D_MODEL), f32),
        "attn_norm": gain(ks[2], D_MODEL),
        "w_in": nrm(ks[3], (L, D_MODEL, IN_WIDTH), D_MODEL ** -0.5),
        "q_norm": gain(ks[4], HEAD_DIM),
        "k_norm": gain(ks[5], HEAD_DIM),
        "pool_proj": nrm(ks[6], (L, len(POOL_WINDOWS), POOL_GROUP, POOL_GROUP), POOL_GROUP ** -0.5),
        "pool_scale": gain(ks[7], POOL_WIDTH),
        "mem_norm": gain(ks[8], D_MODEL),
        "w_mem_kv": nrm(ks[9], (L, D_MODEL, 2 * MEM_WIDTH), D_MODEL ** -0.5),
        "mq_norm": gain(ks[10], MEM_HEAD_DIM),
        "mk_norm": gain(ks[11], MEM_HEAD_DIM),
        "w_out": nrm(ks[12], (L, MIX_WIDTH, D_MODEL), MIX_WIDTH ** -0.5),
        "ffn_norm": gain(ks[13], D_MODEL),
        "w_group": nrm(ks[14], (L, D_MODEL, N_GROUPS), D_MODEL ** -0.5),
        "b_group": nrm(ks[15], (L, N_GROUPS), 0.01),
        "w_router": nrm(ks[16], (L, N_GROUPS, D_MODEL, EXPERTS_PER_GROUP), D_MODEL ** -0.5),
        "b_router": nrm(ks[17], (L, N_GROUPS, EXPERTS_PER_GROUP), 0.01),
        "w1": nrm(ks[18], (L, N_EXPERTS, D_MODEL, EXPERT_HIDDEN), D_MODEL ** -0.5),
        "w3": nrm(ks[19], (L, N_EXPERTS, D_MODEL, EXPERT_HIDDEN), D_MODEL ** -0.5),
        "w2": nrm(ks[20], (L, N_EXPERTS, EXPERT_HIDDEN, D_MODEL), EXPERT_HIDDEN ** -0.5),
    }


def reference(x, mem, attn_norm, w_in, q_norm, k_norm, pool_proj, pool_scale, mem_norm, w_mem_kv,
              mq_norm, mk_norm, w_out, ffn_norm, w_group, b_group, w_router, b_router, w1, w3, w2):
    for l in range(DEPTH):
        x = _layer(x, mem, attn_norm[l], w_in[l], q_norm[l], k_norm[l], pool_proj[l], pool_scale[l],
                   mem_norm[l], w_mem_kv[l], mq_norm[l], mk_norm[l], w_out[l], ffn_norm[l],
                   w_group[l], b_group[l], w_router[l], b_router[l], w1[l], w3[l], w2[l])
    return x
```

```python
import functools

import jax
import jax.numpy as jnp
from jax import lax
from jax.experimental import pallas as pl
from jax.experimental.pallas import tpu as pltpu

F32 = jnp.float32
BF16 = jnp.bfloat16
I32 = jnp.int32
U32 = jnp.uint32

D_MODEL = 1024
HEAD_DIM = 64
ATTN_WIDTH = 512
POOL_WIDTH = 256
MEM_WIDTH = 256
N_MEM = 256
IN_WIDTH = 3 * ATTN_WIDTH + POOL_WIDTH + MEM_WIDTH
N_GROUPS = 8
EXPERTS_PER_GROUP = 8
N_EXPERTS = 64
EXPERT_HIDDEN = 512
EPS = 1e-6
NEG_INF = -1e30

LANES = 128
N_PAIRS = ATTN_WIDTH // LANES
BLOCK = 128
SUPER = 16 * BLOCK
HALO = 16

TM_PROJ = 512
TM_MIX = 256
TD_DISPATCH = 256
TR_MOE = 256
TC_COMBINE = 256
EXPERT_LANE0 = 64

VMEM_LIMIT = 48 * 1024 * 1024


def _cparams(sem):
    return pltpu.CompilerParams(dimension_semantics=sem, vmem_limit_bytes=VMEM_LIMIT)


def _group_ones(n):
    g = jnp.arange(n) // HEAD_DIM
    return (g[:, None] == g[None, :]).astype(BF16)


def _head_norm(z, ones_ref, gain_ref):
    ss = jnp.dot((z * z).astype(BF16), ones_ref[...], preferred_element_type=F32)
    return z * lax.rsqrt(ss * (1.0 / HEAD_DIM) + EPS) * gain_ref[...]


def _proj_kernel(x_ref, an_ref, win_ref, gq_ref, gk_ref, gm_ref, o512_ref, o256_ref,
                 q1_ref, k1_ref, v1_ref, q4_ref, k4_ref, v4_ref, q16_ref, k16_ref, v16_ref,
                 u_ref, qm_ref, zs_ref):
    tm = x_ref.shape[0]
    x = x_ref[...]
    ms = jnp.mean(x * x, axis=-1, keepdims=True)
    h = (x * lax.rsqrt(ms + EPS) * an_ref[...]).astype(BF16)
    z = jnp.dot(h, win_ref[...], preferred_element_type=F32)
    a0, a1, a2, a3 = ATTN_WIDTH, 2 * ATTN_WIDTH, 3 * ATTN_WIDTH, 3 * ATTN_WIDTH + POOL_WIDTH
    qn = _head_norm(z[:, :a0], o512_ref, gq_ref)
    kn = _head_norm(z[:, a0:a1], o512_ref, gk_ref)
    v = z[:, a1:a2]
    u_ref[...] = z[:, a2:a3]
    qm_ref[...] = _head_norm(z[:, a3:], o256_ref, gm_ref).astype(BF16)
    for val, o1, o4, o16 in ((qn, q1_ref, q4_ref, q16_ref), (kn, k1_ref, k4_ref, k16_ref),
                             (v, v1_ref, v4_ref, v16_ref)):
        for hp in range(N_PAIRS):
            pair = val[:, hp * LANES:(hp + 1) * LANES]
            zs_ref[hp] = pair
            o1[hp] = pair.astype(BF16)
        for d, o in ((4, o4), (16, o16)):
            for r in range(d):
                for hp in range(N_PAIRS):
                    rows = zs_ref[hp, pl.ds(r, tm // d, stride=d), :]
                    o[hp, :, r * LANES:(r + 1) * LANES] = rows.astype(BF16)


def _proj(x2d, attn_norm, w_in, gq, gk, gm, B, S):
    T = B * S
    tm = TM_PROJ
    nj = S // tm
    const = lambda i: (0, 0)

    def lay(d):
        return jax.ShapeDtypeStruct((B, N_PAIRS, S // d, d * LANES), BF16)

    def lay_spec(d):
        return pl.BlockSpec((None, N_PAIRS, tm // d, d * LANES), lambda i: (i // nj, 0, i % nj, 0))

    out_shape = [lay(1)] * 3 + [lay(4)] * 3 + [lay(16)] * 3 + [
        jax.ShapeDtypeStruct((T, POOL_WIDTH), F32), jax.ShapeDtypeStruct((T, MEM_WIDTH), BF16)]
    out_specs = [lay_spec(1)] * 3 + [lay_spec(4)] * 3 + [lay_spec(16)] * 3 + [
        pl.BlockSpec((tm, POOL_WIDTH), lambda i: (i, 0)), pl.BlockSpec((tm, MEM_WIDTH), lambda i: (i, 0))]
    return pl.pallas_call(
        _proj_kernel,
        grid=(T // tm,),
        in_specs=[pl.BlockSpec((tm, D_MODEL), lambda i: (i, 0)),
                  pl.BlockSpec((1, D_MODEL), const),
                  pl.BlockSpec((D_MODEL, IN_WIDTH), const),
                  pl.BlockSpec((1, ATTN_WIDTH), const),
                  pl.BlockSpec((1, ATTN_WIDTH), const),
                  pl.BlockSpec((1, MEM_WIDTH), const),
                  pl.BlockSpec((ATTN_WIDTH, ATTN_WIDTH), const),
                  pl.BlockSpec((MEM_WIDTH, MEM_WIDTH), const)],
        out_specs=out_specs,
        out_shape=out_shape,
        scratch_shapes=[pltpu.VMEM((N_PAIRS, tm, LANES), F32)],
        compiler_params=_cparams(("parallel",)),
        name="proj",
    )(x2d, attn_norm, w_in, gq, gk, gm, _group_ones(ATTN_WIDTH), _group_ones(MEM_WIDTH))


def _attn_kernel(q1, k1c, k1p, v1c, v1p, q4, k4c, k4p, v4c, v4p, q16, k16c, k16p, v16c, v16p,
                 bias_ref, o_ref, acc, mrun, lrun):
    c = pl.program_id(2)
    lane = lax.broadcasted_iota(I32, (BLOCK, LANES), 1)
    is_a = lane < HEAD_DIM
    lane_row = lax.broadcasted_iota(I32, (1, LANES), 1)
    mask_a = jnp.where(lane_row < HEAD_DIM, 1.0, 0.0).astype(BF16)
    mask_b = jnp.where(lane_row < HEAD_DIM, 0.0, 1.0).astype(BF16)
    bias_full = bias_ref[0]
    bias_first = jnp.where(c > 0, bias_full, bias_ref[1])

    def tile(q_t, kp_t, kc_t, vp_t, vc_t, bias):
        lhs = jnp.concatenate([q_t * mask_a, q_t * mask_b], axis=0)
        keys = jnp.concatenate([kp_t, kc_t], axis=0)
        s = lax.dot_general(lhs, keys, (((1,), (1,)), ((), ())), preferred_element_type=F32) + bias
        m = jnp.max(s, axis=-1, keepdims=True)
        p = jnp.exp(s - m)
        l = jnp.sum(p, axis=-1, keepdims=True)
        vals = jnp.concatenate([vp_t, vc_t], axis=0)
        pv = jnp.dot(p.astype(BF16), vals, preferred_element_type=F32)
        o_t = jnp.where(is_a, pv[:BLOCK], pv[BLOCK:])
        m_t = jnp.where(is_a, m[:BLOCK], m[BLOCK:])
        l_t = jnp.where(is_a, l[:BLOCK], l[BLOCK:])
        return o_t, m_t, l_t

    def merge(rows, o_t, m_t, l_t):
        m_o = mrun[rows, :]
        m_n = jnp.maximum(m_o, m_t)
        a = jnp.exp(m_o - m_n)
        b = jnp.exp(m_t - m_n)
        acc[rows, :] = acc[rows, :] * a + o_t * b
        lrun[rows, :] = lrun[rows, :] * a + l_t * b
        mrun[rows, :] = m_n

    for jb in range(SUPER // BLOCK):
        cur = pl.ds(jb * BLOCK, BLOCK)
        if jb == 0:
            kp_t, vp_t, bias = k1p[...], v1p[...], bias_first
        else:
            prev = pl.ds((jb - 1) * BLOCK, BLOCK)
            kp_t, vp_t, bias = k1c[prev, :], v1c[prev, :], bias_full
        o_t, m_t, l_t = tile(q1[cur, :], kp_t, k1c[cur, :], vp_t, v1c[cur, :], bias)
        acc[cur, :] = o_t
        mrun[cur, :] = m_t
        lrun[cur, :] = l_t
    for d, q, kc, kp, vc, vp in ((4, q4, k4c, k4p, v4c, v4p), (16, q16, k16c, k16p, v16c, v16p)):
        nblk = SUPER // (BLOCK * d)
        for r in range(d):
            cols = pl.ds(r * LANES, LANES)
            for jb in range(nblk):
                cur = pl.ds(jb * BLOCK, BLOCK)
                if jb == 0:
                    kp_t, vp_t, bias = kp[:, cols], vp[:, cols], bias_first
                else:
                    prev = pl.ds((jb - 1) * BLOCK, BLOCK)
                    kp_t, vp_t, bias = kc[prev, cols], vc[prev, cols], bias_full
                o_t, m_t, l_t = tile(q[cur, cols], kp_t, kc[cur, cols], vp_t, vc[cur, cols], bias)
                merge(pl.ds(jb * BLOCK * d + r, BLOCK, stride=d), o_t, m_t, l_t)
    o_ref[...] = (acc[...] / lrun[...]).astype(BF16)


def _band_bias():
    qi = jnp.arange(BLOCK)[:, None]
    kj = jnp.arange(2 * BLOCK)[None, :]
    dist = qi + BLOCK - kj
    in_band = (dist >= 0) & (dist <= BLOCK)
    full = jnp.where(in_band, 0.0, NEG_INF).astype(F32)
    first = jnp.where(in_band & (kj >= BLOCK), 0.0, NEG_INF).astype(F32)
    return jnp.stack([jnp.tile(full, (2, 1)), jnp.tile(first, (2, 1))])


def _attn(q1, k1, v1, q4, k4, v4, q16, k16, v16, B, S):
    nsup = S // SUPER

    def specs(d):
        rows = SUPER // d
        per = rows // BLOCK
        cur = pl.BlockSpec((None, None, rows, d * LANES), lambda b, hp, c: (b, hp, c, 0))
        prev = pl.BlockSpec((None, None, BLOCK, d * LANES),
                            lambda b, hp, c: (b, hp, jnp.maximum(per * c - 1, 0), 0))
        return [cur, cur, prev, cur, prev]

    return pl.pallas_call(
        _attn_kernel,
        grid=(B, N_PAIRS, nsup),
        in_specs=specs(1) + specs(4) + specs(16) + [
            pl.BlockSpec((2, 2 * BLOCK, 2 * BLOCK), lambda b, hp, c: (0, 0, 0))],
        out_specs=pl.BlockSpec((None, SUPER, LANES), lambda b, hp, c: (b, c, hp)),
        out_shape=jax.ShapeDtypeStruct((B, S, ATTN_WIDTH), BF16),
        scratch_shapes=[pltpu.VMEM((SUPER, LANES), F32)] * 3,
        compiler_params=_cparams(("parallel", "parallel", "parallel")),
        name="dilated_attn",
    )(q1, k1, k1, v1, v1, q4, k4, k4, v4, v4, q16, k16, k16, v16, v16, _band_bias())


def _memkv_kernel(mem_ref, mn_ref, wkv_ref, gk_ref, o256_ref, km_ref, vm_ref):
    m = mem_ref[...]
    ms = jnp.mean(m * m, axis=-1, keepdims=True)
    mn = (m * lax.rsqrt(ms + EPS) * mn_ref[...]).astype(BF16)
    kv = jnp.dot(mn, wkv_ref[...], preferred_element_type=F32)
    km_ref[...] = _head_norm(kv[:, :MEM_WIDTH], o256_ref, gk_ref).astype(BF16)
    vm_ref[...] = kv[:, MEM_WIDTH:].astype(BF16)


def _memkv(mem, mem_norm, w_mem_kv, gmk, B):
    const = lambda b: (0, 0)
    return pl.pallas_call(
        _memkv_kernel,
        grid=(B,),
        in_specs=[pl.BlockSpec((None, N_MEM, D_MODEL), lambda b: (b, 0, 0)),
                  pl.BlockSpec((1, D_MODEL), const),
                  pl.BlockSpec((D_MODEL, 2 * MEM_WIDTH), const),
                  pl.BlockSpec((1, MEM_WIDTH), const),
                  pl.BlockSpec((MEM_WIDTH, MEM_WIDTH), const)],
        out_specs=[pl.BlockSpec((None, N_MEM, MEM_WIDTH), lambda b: (b, 0, 0))] * 2,
        out_shape=[jax.ShapeDtypeStruct((B, N_MEM, MEM_WIDTH), BF16)] * 2,
        compiler_params=_cparams(("parallel",)),
        name="memkv",
    )(mem, mem_norm, w_mem_kv, gmk, _group_ones(MEM_WIDTH))


def _mix_kernel(tiles_per_seq, ya_ref, u_ref, uh_ref, qm_ref, km_ref, vm_ref, x_ref, pp_ref, ps_ref,
                wo_ref, fn_ref, wr_ref, br_ref, ltri_ref,
                x2_ref, h2p_ref, rinfo_ref, rinfot_ref, cnt_ref, base_ref):
    i = pl.program_id(0)
    tm = x_ref.shape[0]
    seq_tile = i % tiles_per_seq

    @pl.when(i == 0)
    def _():
        base_ref[...] = jnp.zeros_like(base_ref)

    u = u_ref[...]
    halo = jnp.where(seq_tile == 0, 0.0, uh_ref[...])
    uu = jnp.concatenate([halo, u], axis=0)
    a1 = uu[1:] + uu[:-1]
    a2 = a1[2:] + a1[:-2]
    a3 = a2[4:] + a2[:-4]
    a4 = a3[8:] + a3[:-8]
    lane_p = lax.broadcasted_iota(I32, (tm, POOL_WIDTH), 1)
    g0, g1, g2 = lane_p < 64, lane_p < 128, lane_p < 192
    wsum = jnp.where(g0, a1[15:], jnp.where(g1, a2[13:], jnp.where(g2, a3[9:], a4[1:])))
    wlen = jnp.where(g0, 2.0, jnp.where(g1, 4.0, jnp.where(g2, 8.0, 16.0)))
    tpos = seq_tile * tm + lax.broadcasted_iota(I32, (tm, POOL_WIDTH), 0) + 1
    cnt = jnp.minimum(tpos.astype(F32), wlen)
    pooled = wsum / cnt - u
    y_pool = jnp.dot(pooled.astype(BF16), pp_ref[...], preferred_element_type=F32) * ps_ref[...]

    lane = lax.broadcasted_iota(I32, (tm, LANES), 1)
    is_a = lane < HEAD_DIM
    lane_row = lax.broadcasted_iota(I32, (1, LANES), 1)
    mask_a = jnp.where(lane_row < HEAD_DIM, 1.0, 0.0).astype(BF16)
    mask_b = jnp.where(lane_row < HEAD_DIM, 0.0, 1.0).astype(BF16)
    y_mem = []
    for pr in range(MEM_WIDTH // LANES):
        cols = slice(pr * LANES, (pr + 1) * LANES)
        qp, kp, vp = qm_ref[:, cols], km_ref[:, cols], vm_ref[:, cols]
        outs = []
        for msk in (mask_a, mask_b):
            s = lax.dot_general(qp * msk, kp, (((1,), (1,)), ((), ())), preferred_element_type=F32)
            m = jnp.max(s, axis=-1, keepdims=True)
            p = jnp.exp(s - m)
            p = p / jnp.sum(p, axis=-1, keepdims=True)
            outs.append(jnp.dot(p.astype(BF16), vp, preferred_element_type=F32))
        y_mem.append(jnp.where(is_a, outs[0], outs[1]))

    a0, a1w = ATTN_WIDTH, ATTN_WIDTH + POOL_WIDTH
    proj = jnp.dot(ya_ref[...], wo_ref[:a0, :], preferred_element_type=F32)
    proj += jnp.dot(y_pool.astype(BF16), wo_ref[a0:a1w, :], preferred_element_type=F32)
    for pr in range(MEM_WIDTH // LANES):
        lo = a1w + pr * LANES
        proj += jnp.dot(y_mem[pr].astype(BF16), wo_ref[lo:lo + LANES, :], preferred_element_type=F32)
    x2 = x_ref[...] + proj
    x2_ref[...] = x2

    ms = jnp.mean(x2 * x2, axis=-1, keepdims=True)
    h2 = (x2 * lax.rsqrt(ms + EPS) * fn_ref[...]).astype(BF16)
    bits = lax.bitcast_convert_type(h2.astype(F32), U32)
    half = D_MODEL // 2
    h2p_ref[...] = bits[:, :half] | (bits[:, half:] >> 16)

    logits = jnp.dot(h2, wr_ref[...], preferred_element_type=F32) + br_ref[...]
    ninf = -jnp.inf
    is_g = lane < N_GROUPS
    lg = jnp.where(is_g, logits, ninf)
    mg = jnp.max(lg, axis=-1, keepdims=True)
    g_sel = jnp.min(jnp.where(lg == mg, lane, LANES), axis=-1, keepdims=True)
    w_g = 1.0 / jnp.sum(jnp.where(is_g, jnp.exp(logits - mg), 0.0), axis=-1, keepdims=True)
    in_grp = (lane >= EXPERT_LANE0) & (((lane - EXPERT_LANE0) >> 3) == g_sel)
    le = jnp.where(in_grp, logits, ninf)
    v1 = jnp.max(le, axis=-1, keepdims=True)
    i1 = jnp.min(jnp.where(le == v1, lane, 2 * LANES), axis=-1, keepdims=True)
    le2 = jnp.where(lane == i1, ninf, le)
    v2 = jnp.max(le2, axis=-1, keepdims=True)
    i2 = jnp.min(jnp.where(le2 == v2, lane, 2 * LANES), axis=-1, keepdims=True)
    e21 = jnp.exp(v2 - v1)
    gate1 = w_g / (1.0 + e21)
    gate2 = w_g * e21 / (1.0 + e21)

    hot1 = lane == i1
    hot2 = lane == i2
    onehot = jnp.where(hot1 | hot2, 1.0, 0.0)
    prefix = jnp.dot(ltri_ref[...], onehot.astype(BF16), preferred_element_type=F32)
    tot = prefix + base_ref[...]
    rank1 = jnp.sum(jnp.where(hot1, tot, 0.0), axis=-1, keepdims=True)
    rank2 = jnp.sum(jnp.where(hot2, tot, 0.0), axis=-1, keepdims=True)
    base_new = base_ref[...] + jnp.sum(onehot, axis=0, keepdims=True)
    base_ref[...] = base_new
    cnt_ref[...] = base_new

    e1 = (i1 - EXPERT_LANE0).astype(F32)
    e2 = (i2 - EXPERT_LANE0).astype(F32)
    rinfo = jnp.where(lane == 0, e1, jnp.where(lane == 1, e2, jnp.where(lane == 2, rank1, jnp.where(
        lane == 3, rank2, jnp.where(lane == 4, gate1, jnp.where(lane == 5, gate2, 0.0))))))
    rinfo_ref[...] = rinfo
    rinfot_ref[...] = jnp.transpose(rinfo)[:8, :]


def _mix(ya, u, qm, km, vm, x2d, pool_bd, pool_scale, w_out, ffn_norm, w_r, b_r, B, S):
    T = B * S
    tm = TM_MIX
    tps = S // tm
    hb = tm // HALO
    const = lambda i: (0, 0)
    ltri = (jnp.arange(tm)[:, None] > jnp.arange(tm)[None, :]).astype(BF16)
    return pl.pallas_call(
        functools.partial(_mix_kernel, tps),
        grid=(T // tm,),
        in_specs=[pl.BlockSpec((tm, ATTN_WIDTH), lambda i: (i, 0)),
                  pl.BlockSpec((tm, POOL_WIDTH), lambda i: (i, 0)),
                  pl.BlockSpec((HALO, POOL_WIDTH), lambda i: (jnp.maximum(i * hb - 1, 0), 0)),
                  pl.BlockSpec((tm, MEM_WIDTH), lambda i: (i, 0)),
                  pl.BlockSpec((None, N_MEM, MEM_WIDTH), lambda i: (i // tps, 0, 0)),
                  pl.BlockSpec((None, N_MEM, MEM_WIDTH), lambda i: (i // tps, 0, 0)),
                  pl.BlockSpec((tm, D_MODEL), lambda i: (i, 0)),
                  pl.BlockSpec((POOL_WIDTH, POOL_WIDTH), const),
                  pl.BlockSpec((1, POOL_WIDTH), const),
                  pl.BlockSpec((D_MODEL, D_MODEL), const),
                  pl.BlockSpec((1, D_MODEL), const),
                  pl.BlockSpec((D_MODEL, LANES), const),
                  pl.BlockSpec((1, LANES), const),
                  pl.BlockSpec((tm, tm), const)],
        out_specs=[pl.BlockSpec((tm, D_MODEL), lambda i: (i, 0)),
                   pl.BlockSpec((tm, D_MODEL // 2), lambda i: (i, 0)),
                   pl.BlockSpec((tm, LANES), lambda i: (i, 0)),
                   pl.BlockSpec((8, tm), lambda i: (0, i)),
                   pl.BlockSpec((1, LANES), const)],
        out_shape=[jax.ShapeDtypeStruct((T, D_MODEL), F32),
                   jax.ShapeDtypeStruct((T, D_MODEL // 2), U32),
                   jax.ShapeDtypeStruct((T, LANES), F32),
                   jax.ShapeDtypeStruct((8, T), F32),
                   jax.ShapeDtypeStruct((1, LANES), F32)],
        scratch_shapes=[pltpu.VMEM((1, LANES), F32)],
        compiler_params=_cparams(("arbitrary",)),
        name="mix_router",
    )(ya, u, u, qm, km, vm, x2d, pool_bd, pool_scale, w_out, ffn_norm, w_r, b_r, ltri)


def _dest_kernel(rt_ref, ps_ref, d_ref):
    tn = rt_ref.shape[1]
    sub = lax.broadcasted_iota(I32, (N_EXPERTS, tn), 0).astype(F32)
    ps = ps_ref[...]
    rows = []
    for k in range(2):
        e = rt_ref[k:k + 1, :]
        start = jnp.sum(jnp.where(sub == e, ps, 0.0), axis=0, keepdims=True)
        rows.append(start + rt_ref[2 + k:3 + k, :])
    d = jnp.concatenate(rows + [jnp.zeros((6, tn), F32)], axis=0)
    d_ref[...] = d.astype(I32)


def _dest(rinfot, pstart_col, T):
    tn = 2048
    return pl.pallas_call(
        _dest_kernel,
        grid=(T // tn,),
        in_specs=[pl.BlockSpec((8, tn), lambda i: (0, i)),
                  pl.BlockSpec((N_EXPERTS, 1), lambda i: (0, 0))],
        out_specs=pl.BlockSpec((8, tn), lambda i: (0, i)),
        out_shape=jax.ShapeDtypeStruct((8, T), I32),
        compiler_params=_cparams(("parallel",)),
        name="dest_rows",
    )(rinfot, pstart_col)


def _dispatch_kernel(d1_ref, d2_ref, ps_ref, pl_ref, nu_ref, h_hbm, xs_hbm, zbuf, sem, zsem):
    base = pl.program_id(0) * TD_DISPATCH

    @pl.when(pl.program_id(0) == 0)
    def _():
        zbuf[...] = jnp.zeros_like(zbuf)
        n_bits = TR_MOE.bit_length() - 1

        def pad_copies(e, wait):
            start, n = ps_ref[e], pl_ref[e]
            head = jnp.minimum((-start) & 7, n)
            for k in range(7):
                @pl.when(k < head)
                def _():
                    cp = pltpu.make_async_copy(zbuf.at[pl.ds(0, 1)], xs_hbm.at[pl.ds(start + k, 1)], zsem)
                    cp.wait() if wait else cp.start()
            body0, nb = start + head, n - head
            for k in range(3, n_bits):
                size = 1 << k

                @pl.when(((nb >> k) & 1) == 1)
                def _():
                    off = pl.multiple_of(body0 + (nb & (size - 1)), 8)
                    cp = pltpu.make_async_copy(zbuf.at[pl.ds(0, size)], xs_hbm.at[pl.ds(off, size)], zsem)
                    cp.wait() if wait else cp.start()

        def fill(e, carry):
            pad_copies(e, False)
            return carry

        def fill_wait(e, carry):
            pad_copies(e, True)
            return carry

        def tail_copies(j, wait):
            @pl.when(j >= nu_ref[0])
            def _():
                off = pl.multiple_of(j * TR_MOE, TR_MOE)
                cp = pltpu.make_async_copy(zbuf, xs_hbm.at[pl.ds(off, TR_MOE)], zsem)
                cp.wait() if wait else cp.start()

        def tail(j, carry):
            tail_copies(j, False)
            return carry

        def tail_wait(j, carry):
            tail_copies(j, True)
            return carry

        n_tiles = xs_hbm.shape[0] // TR_MOE
        lax.fori_loop(0, N_EXPERTS, fill, 0)
        lax.fori_loop(0, n_tiles, tail, 0)
        lax.fori_loop(0, N_EXPERTS, fill_wait, 0)
        lax.fori_loop(0, n_tiles, tail_wait, 0)

    def row_copy(tok, dst):
        return pltpu.make_async_copy(h_hbm.at[pl.ds(tok, 1)], xs_hbm.at[pl.ds(dst, 1)], sem)

    def issue(t, carry):
        tok = base + t
        row_copy(tok, d1_ref[tok]).start()
        row_copy(tok, d2_ref[tok]).start()
        return carry

    lax.fori_loop(0, TD_DISPATCH, issue, 0)

    def drain(t, carry):
        row_copy(0, 0).wait()
        row_copy(0, 0).wait()
        return carry

    lax.fori_loop(0, TD_DISPATCH, drain, 0)


def _dispatch(d1, d2, pad_start, pad_len, n_used, h2p, R):
    T = h2p.shape[0]
    return pl.pallas_call(
        _dispatch_kernel,
        grid_spec=pltpu.PrefetchScalarGridSpec(
            num_scalar_prefetch=5,
            grid=(T // TD_DISPATCH,),
            in_specs=[pl.BlockSpec(memory_space=pl.ANY)],
            out_specs=pl.BlockSpec(memory_space=pl.ANY),
            scratch_shapes=[pltpu.VMEM((TR_MOE, D_MODEL // 2), U32),
                            pltpu.SemaphoreType.DMA(()),
                            pltpu.SemaphoreType.DMA(())]),
        out_shape=jax.ShapeDtypeStruct((R, D_MODEL // 2), U32),
        compiler_params=_cparams(("arbitrary",)),
        name="dispatch",
    )(d1, d2, pad_start, pad_len, n_used, h2p)


def _moe_kernel(be_ref, nu_ref, xs_ref, w1_ref, w3_ref, w2_ref, yb_ref, w1b, w3b, w2b):
    j = pl.program_id(0)

    @pl.when(j >= nu_ref[0])
    def _():
        yb_ref[...] = jnp.zeros_like(yb_ref)

    @pl.when(j < nu_ref[0])
    def _():
        e = be_ref[j]
        e_prev = be_ref[jnp.maximum(j - 1, 0)]

        @pl.when((j == 0) | (e != e_prev))
        def _():
            w1b[...] = w1_ref[...].astype(BF16)
            w3b[...] = w3_ref[...].astype(BF16)
            w2b[...] = w2_ref[...].astype(BF16)

        w = xs_ref[...]
        half = D_MODEL // 2
        lo = lax.bitcast_convert_type(w & jnp.uint32(0xFFFF0000), F32).astype(BF16)
        hi = lax.bitcast_convert_type(w << 16, F32).astype(BF16)
        a = jnp.dot(lo, w1b[:half, :], preferred_element_type=F32)
        a += jnp.dot(hi, w1b[half:, :], preferred_element_type=F32)
        b = jnp.dot(lo, w3b[:half, :], preferred_element_type=F32)
        b += jnp.dot(hi, w3b[half:, :], preferred_element_type=F32)
        hmid = (a / (1.0 + jnp.exp(-a)) * b).astype(BF16)
        yb_ref[...] = jnp.dot(hmid, w2b[...], preferred_element_type=F32)


def _moe(blk_exp, n_used, xs, w1, w3, w2):
    R = xs.shape[0]
    tr = TR_MOE

    def row_map(j, be, nu):
        return (jnp.minimum(j, nu[0] - 1), 0)

    def w_map(j, be, nu):
        return (be[jnp.minimum(j, nu[0] - 1)], 0, 0)

    return pl.pallas_call(
        _moe_kernel,
        grid_spec=pltpu.PrefetchScalarGridSpec(
            num_scalar_prefetch=2,
            grid=(R // tr,),
            in_specs=[pl.BlockSpec((tr, D_MODEL // 2), row_map),
                      pl.BlockSpec((None, D_MODEL, EXPERT_HIDDEN), w_map),
                      pl.BlockSpec((None, D_MODEL, EXPERT_HIDDEN), w_map),
                      pl.BlockSpec((None, EXPERT_HIDDEN, D_MODEL), w_map)],
            out_specs=pl.BlockSpec((tr, D_MODEL), lambda j, be, nu: (j, 0)),
            scratch_shapes=[pltpu.VMEM((D_MODEL, EXPERT_HIDDEN), BF16),
                            pltpu.VMEM((D_MODEL, EXPERT_HIDDEN), BF16),
                            pltpu.VMEM((EXPERT_HIDDEN, D_MODEL), BF16)]),
        out_shape=jax.ShapeDtypeStruct((R, D_MODEL), F32),
        compiler_params=_cparams(("arbitrary",)),
        name="moe_experts",
    )(blk_exp, n_used, xs, w1, w3, w2)


def _combine_kernel(d1_ref, d2_ref, x2_ref, rinfo_ref, yb_hbm, o_ref, buf, sem):
    tc = x2_ref.shape[0]
    base = pl.program_id(0) * tc

    def row_copy(src, k, t):
        return pltpu.make_async_copy(yb_hbm.at[pl.ds(src, 1)], buf.at[k, pl.ds(t, 1)], sem)

    def issue(t, carry):
        row_copy(d1_ref[base + t], 0, t).start()
        row_copy(d2_ref[base + t], 1, t).start()
        return carry

    lax.fori_loop(0, tc, issue, 0)

    def drain(t, carry):
        row_copy(0, 0, 0).wait()
        row_copy(0, 1, 0).wait()
        return carry

    lax.fori_loop(0, tc, drain, 0)
    g1 = rinfo_ref[:, 4:5]
    g2 = rinfo_ref[:, 5:6]
    o_ref[...] = x2_ref[...] + (g1 * buf[0] + g2 * buf[1])


def _combine(d1, d2, x2, rinfo, yb):
    T = x2.shape[0]
    tc = TC_COMBINE
    return pl.pallas_call(
        _combine_kernel,
        grid_spec=pltpu.PrefetchScalarGridSpec(
            num_scalar_prefetch=2,
            grid=(T // tc,),
            in_specs=[pl.BlockSpec((tc, D_MODEL), lambda i, a, b: (i, 0)),
                      pl.BlockSpec((tc, LANES), lambda i, a, b: (i, 0)),
                      pl.BlockSpec(memory_space=pl.ANY)],
            out_specs=pl.BlockSpec((tc, D_MODEL), lambda i, a, b: (i, 0)),
            scratch_shapes=[pltpu.VMEM((2, tc, D_MODEL), F32),
                            pltpu.SemaphoreType.DMA(())]),
        out_shape=jax.ShapeDtypeStruct((T, D_MODEL), F32),
        compiler_params=_cparams(("arbitrary",)),
        name="combine",
    )(d1, d2, x2, rinfo, yb)


def _layer(x, mem, attn_norm, w_in, q_norm, k_norm, pool_proj, pool_scale, mem_norm, w_mem_kv,
           mq_norm, mk_norm, w_out, ffn_norm, w_group, b_group, w_router, b_router, w1, w3, w2):
    B, S, D = x.shape
    T = B * S
    assert D == D_MODEL and S % SUPER == 0 and T % TM_PROJ == 0
    x2d = x.reshape(T, D)
    row = lambda v: v.reshape(1, -1).astype(F32)
    scale = HEAD_DIM ** -0.5
    gq = row(jnp.tile(q_norm, ATTN_WIDTH // HEAD_DIM) * scale)
    gk = row(jnp.tile(k_norm, ATTN_WIDTH // HEAD_DIM))
    gmq = row(jnp.tile(mq_norm, MEM_WIDTH // HEAD_DIM) * scale)
    gmk = row(jnp.tile(mk_norm, MEM_WIDTH // HEAD_DIM))

    (q1, k1, v1, q4, k4, v4, q16, k16, v16, u, qm) = _proj(
        x2d, row(attn_norm), w_in.astype(BF16), gq, gk, gmq, B, S)
    ya = _attn(q1, k1, v1, q4, k4, v4, q16, k16, v16, B, S).reshape(T, ATTN_WIDTH)
    km, vm = _memkv(mem, row(mem_norm), w_mem_kv.astype(BF16), gmk, B)

    pool_bd = jax.scipy.linalg.block_diag(*[pool_proj[g] for g in range(pool_proj.shape[0])]).astype(BF16)
    w_r = jnp.zeros((D, LANES), F32)
    w_r = w_r.at[:, :N_GROUPS].set(w_group)
    w_r = w_r.at[:, EXPERT_LANE0:].set(jnp.transpose(w_router, (1, 0, 2)).reshape(D, N_EXPERTS))
    b_r = jnp.zeros((1, LANES), F32).at[0, :N_GROUPS].set(b_group).at[0, EXPERT_LANE0:].set(b_router.reshape(-1))
    x2, h2p, rinfo, rinfot, counts = _mix(ya, u, qm, km, vm, x2d, pool_bd, row(pool_scale),
                                          w_out.astype(BF16), row(ffn_norm), w_r.astype(BF16), b_r, B, S)

    tr = TR_MOE
    R = 2 * T + N_EXPERTS * tr
    n_tiles = R // tr
    cnt = counts[0, EXPERT_LANE0:].astype(I32)
    padded = ((cnt + tr - 1) // tr) * tr
    pend = jnp.cumsum(padded)
    pstart = pend - padded
    n_used = (pend[-1] // tr).astype(I32).reshape(1)
    tile_row = jnp.minimum(jnp.arange(n_tiles, dtype=I32), n_used[0] - 1) * tr
    blk_exp = jnp.minimum(jnp.searchsorted(pend, tile_row, side="right"), N_EXPERTS - 1).astype(I32)

    dest = _dest(rinfot, pstart.astype(F32).reshape(N_EXPERTS, 1), T)
    d1, d2 = dest[0], dest[1]
    xs = _dispatch(d1, d2, pstart + cnt, padded - cnt, n_used, h2p, R)
    yb = _moe(blk_exp, n_used, xs, w1, w3, w2)
    out = _combine(d1, d2, x2, rinfo, yb)
    return out.reshape(B, S, D)


def kernel(x, mem, attn_norm, w_in, q_norm, k_norm, pool_proj, pool_scale, mem_norm, w_mem_kv, mq_norm, mk_norm,
           w_out, ffn_norm, w_group, b_group, w_router, b_router, w1, w3, w2):
    for l in range(attn_norm.shape[0]):
        x = _layer(x, mem, attn_norm[l], w_in[l], q_norm[l], k_norm[l], pool_proj[l], pool_scale[l],
                   mem_norm[l], w_mem_kv[l], mq_norm[l], mk_norm[l], w_out[l], ffn_norm[l],
                   w_group[l], b_group[l], w_router[l], b_router[l], w1[l], w3[l], w2[l])
    return x
```

```python
import functools

import jax
import jax.numpy as jnp
from jax import lax
from jax.experimental import pallas as pl
from jax.experimental.pallas import tpu as pltpu

F32 = jnp.float32
BF16 = jnp.bfloat16
I32 = jnp.int32
U32 = jnp.uint32

D_MODEL = 1024
HEAD_DIM = 64
ATTN_WIDTH = 512
POOL_WIDTH = 256
MEM_WIDTH = 256
N_MEM = 256
IN_WIDTH = 3 * ATTN_WIDTH + POOL_WIDTH + MEM_WIDTH
N_GROUPS = 8
EXPERTS_PER_GROUP = 8
N_EXPERTS = 64
EXPERT_HIDDEN = 512
EPS = 1e-6
NEG_INF = -1e30

LANES = 128
N_PAIRS = ATTN_WIDTH // LANES
BLOCK = 128
SUPER = 16 * BLOCK
HALO = 16

TM_PROJ = 512
TM_MIX = 256
TD_DISPATCH = 256
TR_MOE = 256
TC_COMBINE = 256
EXPERT_LANE0 = 64

VMEM_LIMIT = 48 * 1024 * 1024


def _cparams(sem):
    return pltpu.CompilerParams(dimension_semantics=sem, vmem_limit_bytes=VMEM_LIMIT)


def _group_ones(n):
    g = jnp.arange(n) // HEAD_DIM
    return (g[:, None] == g[None, :]).astype(BF16)


def _head_norm(z, ones_ref, gain_ref):
    ss = jnp.dot((z * z).astype(BF16), ones_ref[...], preferred_element_type=F32)
    return z * lax.rsqrt(ss * (1.0 / HEAD_DIM) + EPS) * gain_ref[...]


def _proj_kernel(x_ref, an_ref, win_ref, gq_ref, gk_ref, gm_ref, o512_ref, o256_ref,
                 q1_ref, k1_ref, v1_ref, q4_ref, k4_ref, v4_ref, q16_ref, k16_ref, v16_ref,
                 u_ref, qm_ref, zs_ref):
    tm = x_ref.shape[0]
    x = x_ref[...]
    ms = jnp.mean(x * x, axis=-1, keepdims=True)
    h = (x * lax.rsqrt(ms + EPS) * an_ref[...]).astype(BF16)
    z = jnp.dot(h, win_ref[...], preferred_element_type=F32)
    a0, a1, a2, a3 = ATTN_WIDTH, 2 * ATTN_WIDTH, 3 * ATTN_WIDTH, 3 * ATTN_WIDTH + POOL_WIDTH
    qn = _head_norm(z[:, :a0], o512_ref, gq_ref)
    kn = _head_norm(z[:, a0:a1], o512_ref, gk_ref)
    v = z[:, a1:a2]
    u_ref[...] = z[:, a2:a3]
    qm_ref[...] = _head_norm(z[:, a3:], o256_ref, gm_ref).astype(BF16)
    for val, o1, o4, o16 in ((qn, q1_ref, q4_ref, q16_ref), (kn, k1_ref, k4_ref, k16_ref),
                             (v, v1_ref, v4_ref, v16_ref)):
        for hp in range(N_PAIRS):
            pair = val[:, hp * LANES:(hp + 1) * LANES]
            zs_ref[hp] = pair
            o1[hp] = pair.astype(BF16)
        for d, o in ((4, o4), (16, o16)):
            for r in range(d):
                for hp in range(N_PAIRS):
                    rows = zs_ref[hp, pl.ds(r, tm // d, stride=d), :]
                    o[hp, :, r * LANES:(r + 1) * LANES] = rows.astype(BF16)


def _proj(x2d, attn_norm, w_in, gq, gk, gm, B, S):
    T = B * S
    tm = TM_PROJ
    nj = S // tm
    const = lambda i: (0, 0)

    def lay(d):
        return jax.ShapeDtypeStruct((B, N_PAIRS, S // d, d * LANES), BF16)

    def lay_spec(d):
        return pl.BlockSpec((None, N_PAIRS, tm // d, d * LANES), lambda i: (i // nj, 0, i % nj, 0))

    out_shape = [lay(1)] * 3 + [lay(4)] * 3 + [lay(16)] * 3 + [
        jax.ShapeDtypeStruct((T, POOL_WIDTH), F32), jax.ShapeDtypeStruct((T, MEM_WIDTH), BF16)]
    out_specs = [lay_spec(1)] * 3 + [lay_spec(4)] * 3 + [lay_spec(16)] * 3 + [
        pl.BlockSpec((tm, POOL_WIDTH), lambda i: (i, 0)), pl.BlockSpec((tm, MEM_WIDTH), lambda i: (i, 0))]
    return pl.pallas_call(
        _proj_kernel,
        grid=(T // tm,),
        in_specs=[pl.BlockSpec((tm, D_MODEL), lambda i: (i, 0)),
                  pl.BlockSpec((1, D_MODEL), const),
                  pl.BlockSpec((D_MODEL, IN_WIDTH), const),
                  pl.BlockSpec((1, ATTN_WIDTH), const),
                  pl.BlockSpec((1, ATTN_WIDTH), const),
                  pl.BlockSpec((1, MEM_WIDTH), const),
                  pl.BlockSpec((ATTN_WIDTH, ATTN_WIDTH), const),
                  pl.BlockSpec((MEM_WIDTH, MEM_WIDTH), const)],
        out_specs=out_specs,
        out_shape=out_shape,
        scratch_shapes=[pltpu.VMEM((N_PAIRS, tm, LANES), F32)],
        compiler_params=_cparams(("parallel",)),
        name="proj",
    )(x2d, attn_norm, w_in, gq, gk, gm, _group_ones(ATTN_WIDTH), _group_ones(MEM_WIDTH))


def _attn_kernel(q1, k1c, k1p, v1c, v1p, q4, k4c, k4p, v4c, v4p, q16, k16c, k16p, v16c, v16p,
                 bias_ref, o_ref, acc, mrun, lrun):
    c = pl.program_id(2)
    lane = lax.broadcasted_iota(I32, (BLOCK, LANES), 1)
    is_a = lane < HEAD_DIM
    lane_row = lax.broadcasted_iota(I32, (1, LANES), 1)
    mask_a = jnp.where(lane_row < HEAD_DIM, 1.0, 0.0).astype(BF16)
    mask_b = jnp.where(lane_row < HEAD_DIM, 0.0, 1.0).astype(BF16)
    bias_full = bias_ref[0]
    bias_first = jnp.where(c > 0, bias_full, bias_ref[1])

    def tile(q_t, kp_t, kc_t, vp_t, vc_t, bias):
        lhs = jnp.concatenate([q_t * mask_a, q_t * mask_b], axis=0)
        keys = jnp.concatenate([kp_t, kc_t], axis=0)
        s = lax.dot_general(lhs, keys, (((1,), (1,)), ((), ())), preferred_element_type=F32) + bias
        m = jnp.max(s, axis=-1, keepdims=True)
        p = jnp.exp(s - m)
        l = jnp.sum(p, axis=-1, keepdims=True)
        vals = jnp.concatenate([vp_t, vc_t], axis=0)
        pv = jnp.dot(p.astype(BF16), vals, preferred_element_type=F32)
        o_t = jnp.where(is_a, pv[:BLOCK], pv[BLOCK:])
        m_t = jnp.where(is_a, m[:BLOCK], m[BLOCK:])
        l_t = jnp.where(is_a, l[:BLOCK], l[BLOCK:])
        return o_t, m_t, l_t

    def merge(rows, o_t, m_t, l_t):
        m_o = mrun[rows, :]
        m_n = jnp.maximum(m_o, m_t)
        a = jnp.exp(m_o - m_n)
        b = jnp.exp(m_t - m_n)
        acc[rows, :] = acc[rows, :] * a + o_t * b
        lrun[rows, :] = lrun[rows, :] * a + l_t * b
        mrun[rows, :] = m_n

    for jb in range(SUPER // BLOCK):
        cur = pl.ds(jb * BLOCK, BLOCK)
        if jb == 0:
            kp_t, vp_t, bias = k1p[...], v1p[...], bias_first
        else:
            prev = pl.ds((jb - 1) * BLOCK, BLOCK)
            kp_t, vp_t, bias = k1c[prev, :], v1c[prev, :], bias_full
        o_t, m_t, l_t = tile(q1[cur, :], kp_t, k1c[cur, :], vp_t, v1c[cur, :], bias)
        acc[cur, :] = o_t
        mrun[cur, :] = m_t
        lrun[cur, :] = l_t
    for d, q, kc, kp, vc, vp in ((4, q4, k4c, k4p, v4c, v4p), (16, q16, k16c, k16p, v16c, v16p)):
        nblk = SUPER // (BLOCK * d)
        for r in range(d):
            cols = pl.ds(r * LANES, LANES)
            for jb in range(nblk):
                cur = pl.ds(jb * BLOCK, BLOCK)
                if jb == 0:
                    kp_t, vp_t, bias = kp[:, cols], vp[:, cols], bias_first
                else:
                    prev = pl.ds((jb - 1) * BLOCK, BLOCK)
                    kp_t, vp_t, bias = kc[prev, cols], vc[prev, cols], bias_full
                o_t, m_t, l_t = tile(q[cur, cols], kp_t, kc[cur, cols], vp_t, vc[cur, cols], bias)
                merge(pl.ds(jb * BLOCK * d + r, BLOCK, stride=d), o_t, m_t, l_t)
    o_ref[...] = (acc[...] / lrun[...]).astype(BF16)


def _band_bias():
    qi = jnp.arange(BLOCK)[:, None]
    kj = jnp.arange(2 * BLOCK)[None, :]
    dist = qi + BLOCK - kj
    in_band = (dist >= 0) & (dist <= BLOCK)
    full = jnp.where(in_band, 0.0, NEG_INF).astype(F32)
    first = jnp.where(in_band & (kj >= BLOCK), 0.0, NEG_INF).astype(F32)
    return jnp.stack([jnp.tile(full, (2, 1)), jnp.tile(first, (2, 1))])


def _attn(q1, k1, v1, q4, k4, v4, q16, k16, v16, B, S):
    nsup = S // SUPER

    def specs(d):
        rows = SUPER // d
        per = rows // BLOCK
        cur = pl.BlockSpec((None, None, rows, d * LANES), lambda b, hp, c: (b, hp, c, 0))
        prev = pl.BlockSpec((None, None, BLOCK, d * LANES),
                            lambda b, hp, c: (b, hp, jnp.maximum(per * c - 1, 0), 0))
        return [cur, cur, prev, cur, prev]

    return pl.pallas_call(
        _attn_kernel,
        grid=(B, N_PAIRS, nsup),
        in_specs=specs(1) + specs(4) + specs(16) + [
            pl.BlockSpec((2, 2 * BLOCK, 2 * BLOCK), lambda b, hp, c: (0, 0, 0))],
        out_specs=pl.BlockSpec((None, SUPER, LANES), lambda b, hp, c: (b, c, hp)),
        out_shape=jax.ShapeDtypeStruct((B, S, ATTN_WIDTH), BF16),
        scratch_shapes=[pltpu.VMEM((SUPER, LANES), F32)] * 3,
        compiler_params=_cparams(("parallel", "parallel", "parallel")),
        name="dilated_attn",
    )(q1, k1, k1, v1, v1, q4, k4, k4, v4, v4, q16, k16, k16, v16, v16, _band_bias())


def _memkv_kernel(mem_ref, mn_ref, wkv_ref, gk_ref, o256_ref, km_ref, vm_ref):
    m = mem_ref[...]
    ms = jnp.mean(m * m, axis=-1, keepdims=True)
    mn = (m * lax.rsqrt(ms + EPS) * mn_ref[...]).astype(BF16)
    kv = jnp.dot(mn, wkv_ref[...], preferred_element_type=F32)
    km_ref[...] = _head_norm(kv[:, :MEM_WIDTH], o256_ref, gk_ref).astype(BF16)
    vm_ref[...] = kv[:, MEM_WIDTH:].astype(BF16)


def _memkv(mem, mem_norm, w_mem_kv, gmk, B):
    const = lambda b: (0, 0)
    return pl.pallas_call(
        _memkv_kernel,
        grid=(B,),
        in_specs=[pl.BlockSpec((None, N_MEM, D_MODEL), lambda b: (b, 0, 0)),
                  pl.BlockSpec((1, D_MODEL), const),
                  pl.BlockSpec((D_MODEL, 2 * MEM_WIDTH), const),
                  pl.BlockSpec((1, MEM_WIDTH), const),
                  pl.BlockSpec((MEM_WIDTH, MEM_WIDTH), const)],
        out_specs=[pl.BlockSpec((None, N_MEM, MEM_WIDTH), lambda b: (b, 0, 0))] * 2,
        out_shape=[jax.ShapeDtypeStruct((B, N_MEM, MEM_WIDTH), BF16)] * 2,
        compiler_params=_cparams(("parallel",)),
        name="memkv",
    )(mem, mem_norm, w_mem_kv, gmk, _group_ones(MEM_WIDTH))


def _mix_kernel(tiles_per_seq, ya_ref, u_ref, uh_ref, qm_ref, km_ref, vm_ref, x_ref, pp_ref, ps_ref,
                wo_ref, fn_ref, wr_ref, br_ref, ltri_ref,
                x2_ref, h2p_ref, rinfo_ref, rinfot_ref, cnt_ref, base_ref):
    i = pl.program_id(0)
    tm = x_ref.shape[0]
    seq_tile = i % tiles_per_seq

    @pl.when(i == 0)
    def _():
        base_ref[...] = jnp.zeros_like(base_ref)

    u = u_ref[...]
    halo = jnp.where(seq_tile == 0, 0.0, uh_ref[...])
    uu = jnp.concatenate([halo, u], axis=0)
    a1 = uu[1:] + uu[:-1]
    a2 = a1[2:] + a1[:-2]
    a3 = a2[4:] + a2[:-4]
    a4 = a3[8:] + a3[:-8]
    lane_p = lax.broadcasted_iota(I32, (tm, POOL_WIDTH), 1)
    g0, g1, g2 = lane_p < 64, lane_p < 128, lane_p < 192
    wsum = jnp.where(g0, a1[15:], jnp.where(g1, a2[13:], jnp.where(g2, a3[9:], a4[1:])))
    wlen = jnp.where(g0, 2.0, jnp.where(g1, 4.0, jnp.where(g2, 8.0, 16.0)))
    tpos = seq_tile * tm + lax.broadcasted_iota(I32, (tm, POOL_WIDTH), 0) + 1
    cnt = jnp.minimum(tpos.astype(F32), wlen)
    pooled = wsum / cnt - u
    y_pool = jnp.dot(pooled.astype(BF16), pp_ref[...], preferred_element_type=F32) * ps_ref[...]

    lane = lax.broadcasted_iota(I32, (tm, LANES), 1)
    is_a = lane < HEAD_DIM
    lane_row = lax.broadcasted_iota(I32, (1, LANES), 1)
    mask_a = jnp.where(lane_row < HEAD_DIM, 1.0, 0.0).astype(BF16)
    mask_b = jnp.where(lane_row < HEAD_DIM, 0.0, 1.0).astype(BF16)
    y_mem = []
    for pr in range(MEM_WIDTH // LANES):
        cols = slice(pr * LANES, (pr + 1) * LANES)
        qp, kp, vp = qm_ref[:, cols], km_ref[:, cols], vm_ref[:, cols]
        outs = []
        for msk in (mask_a, mask_b):
            s = lax.dot_general(qp * msk, kp, (((1,), (1,)), ((), ())), preferred_element_type=F32)
            m = jnp.max(s, axis=-1, keepdims=True)
            p = jnp.exp(s - m)
            p = p / jnp.sum(p, axis=-1, keepdims=True)
            outs.append(jnp.dot(p.astype(BF16), vp, preferred_element_type=F32))
        y_mem.append(jnp.where(is_a, outs[0], outs[1]))

    a0, a1w = ATTN_WIDTH, ATTN_WIDTH + POOL_WIDTH
    proj = jnp.dot(ya_ref[...], wo_ref[:a0, :], preferred_element_type=F32)
    proj += jnp.dot(y_pool.astype(BF16), wo_ref[a0:a1w, :], preferred_element_type=F32)
    for pr in range(MEM_WIDTH // LANES):
        lo = a1w + pr * LANES
        proj += jnp.dot(y_mem[pr].astype(BF16), wo_ref[lo:lo + LANES, :], preferred_element_type=F32)
    x2 = x_ref[...] + proj
    x2_ref[...] = x2

    ms = jnp.mean(x2 * x2, axis=-1, keepdims=True)
    h2 = (x2 * lax.rsqrt(ms + EPS) * fn_ref[...]).astype(BF16)
    bits = lax.bitcast_convert_type(h2.astype(F32), U32)
    half = D_MODEL // 2
    h2p_ref[...] = bits[:, :half] | (bits[:, half:] >> 16)

    logits = jnp.dot(h2, wr_ref[...], preferred_element_type=F32) + br_ref[...]
    ninf = -jnp.inf
    is_g = lane < N_GROUPS
    lg = jnp.where(is_g, logits, ninf)
    mg = jnp.max(lg, axis=-1, keepdims=True)
    g_sel = jnp.min(jnp.where(lg == mg, lane, LANES), axis=-1, keepdims=True)
    w_g = 1.0 / jnp.sum(jnp.where(is_g, jnp.exp(logits - mg), 0.0), axis=-1, keepdims=True)
    in_grp = (lane >= EXPERT_LANE0) & (((lane - EXPERT_LANE0) >> 3) == g_sel)
    le = jnp.where(in_grp, logits, ninf)
    v1 = jnp.max(le, axis=-1, keepdims=True)
    i1 = jnp.min(jnp.where(le == v1, lane, 2 * LANES), axis=-1, keepdims=True)
    le2 = jnp.where(lane == i1, ninf, le)
    v2 = jnp.max(le2, axis=-1, keepdims=True)
    i2 = jnp.min(jnp.where(le2 == v2, lane, 2 * LANES), axis=-1, keepdims=True)
    e21 = jnp.exp(v2 - v1)
    gate1 = w_g / (1.0 + e21)
    gate2 = w_g * e21 / (1.0 + e21)

    hot1 = lane == i1
    hot2 = lane == i2
    onehot = jnp.where(hot1 | hot2, 1.0, 0.0)
    prefix = jnp.dot(ltri_ref[...], onehot.astype(BF16), preferred_element_type=F32)
    tot = prefix + base_ref[...]
    rank1 = jnp.sum(jnp.where(hot1, tot, 0.0), axis=-1, keepdims=True)
    rank2 = jnp.sum(jnp.where(hot2, tot, 0.0), axis=-1, keepdims=True)
    base_new = base_ref[...] + jnp.sum(onehot, axis=0, keepdims=True)
    base_ref[...] = base_new
    cnt_ref[...] = base_new

    e1 = (i1 - EXPERT_LANE0).astype(F32)
    e2 = (i2 - EXPERT_LANE0).astype(F32)
    rinfo = jnp.where(lane == 0, e1, jnp.where(lane == 1, e2, jnp.where(lane == 2, rank1, jnp.where(
        lane == 3, rank2, jnp.where(lane == 4, gate1, jnp.where(lane == 5, gate2, 0.0))))))
    rinfo_ref[...] = rinfo
    rinfot_ref[...] = jnp.transpose(rinfo)[:8, :]


def _mix(ya, u, qm, km, vm, x2d, pool_bd, pool_scale, w_out, ffn_norm, w_r, b_r, B, S):
    T = B * S
    tm = TM_MIX
    tps = S // tm
    hb = tm // HALO
    const = lambda i: (0, 0)
    ltri = (jnp.arange(tm)[:, None] > jnp.arange(tm)[None, :]).astype(BF16)
    return pl.pallas_call(
        functools.partial(_mix_kernel, tps),
        grid=(T // tm,),
        in_specs=[pl.BlockSpec((tm, ATTN_WIDTH), lambda i: (i, 0)),
                  pl.BlockSpec((tm, POOL_WIDTH), lambda i: (i, 0)),
                  pl.BlockSpec((HALO, POOL_WIDTH), lambda i: (jnp.maximum(i * hb - 1, 0), 0)),
                  pl.BlockSpec((tm, MEM_WIDTH), lambda i: (i, 0)),
                  pl.BlockSpec((None, N_MEM, MEM_WIDTH), lambda i: (i // tps, 0, 0)),
                  pl.BlockSpec((None, N_MEM, MEM_WIDTH), lambda i: (i // tps, 0, 0)),
                  pl.BlockSpec((tm, D_MODEL), lambda i: (i, 0)),
                  pl.BlockSpec((POOL_WIDTH, POOL_WIDTH), const),
                  pl.BlockSpec((1, POOL_WIDTH), const),
                  pl.BlockSpec((D_MODEL, D_MODEL), const),
                  pl.BlockSpec((1, D_MODEL), const),
                  pl.BlockSpec((D_MODEL, LANES), const),
                  pl.BlockSpec((1, LANES), const),
                  pl.BlockSpec((tm, tm), const)],
        out_specs=[pl.BlockSpec((tm, D_MODEL), lambda i: (i, 0)),
                   pl.BlockSpec((tm, D_MODEL // 2), lambda i: (i, 0)),
                   pl.BlockSpec((tm, LANES), lambda i: (i, 0)),
                   pl.BlockSpec((8, tm), lambda i: (0, i)),
                   pl.BlockSpec((1, LANES), const)],
        out_shape=[jax.ShapeDtypeStruct((T, D_MODEL), F32),
                   jax.ShapeDtypeStruct((T, D_MODEL // 2), U32),
                   jax.ShapeDtypeStruct((T, LANES), F32),
                   jax.ShapeDtypeStruct((8, T), F32),
                   jax.ShapeDtypeStruct((1, LANES), F32)],
        scratch_shapes=[pltpu.VMEM((1, LANES), F32)],
        compiler_params=_cparams(("arbitrary",)),
        name="mix_router",
    )(ya, u, u, qm, km, vm, x2d, pool_bd, pool_scale, w_out, ffn_norm, w_r, b_r, ltri)


def _dest_kernel(rt_ref, ps_ref, d_ref):
    tn = rt_ref.shape[1]
    sub = lax.broadcasted_iota(I32, (N_EXPERTS, tn), 0).astype(F32)
    ps = ps_ref[...]
    rows = []
    for k in range(2):
        e = rt_ref[k:k + 1, :]
        start = jnp.sum(jnp.where(sub == e, ps, 0.0), axis=0, keepdims=True)
        rows.append(start + rt_ref[2 + k:3 + k, :])
    d = jnp.concatenate(rows + [jnp.zeros((6, tn), F32)], axis=0)
    d_ref[...] = d.astype(I32)


def _dest(rinfot, pstart_col, T):
    tn = 2048
    return pl.pallas_call(
        _dest_kernel,
        grid=(T // tn,),
        in_specs=[pl.BlockSpec((8, tn), lambda i: (0, i)),
                  pl.BlockSpec((N_EXPERTS, 1), lambda i: (0, 0))],
        out_specs=pl.BlockSpec((8, tn), lambda i: (0, i)),
        out_shape=jax.ShapeDtypeStruct((8, T), I32),
        compiler_params=_cparams(("parallel",)),
        name="dest_rows",
    )(rinfot, pstart_col)


def _dispatch_kernel(d1_ref, d2_ref, ps_ref, pl_ref, nu_ref, h_ref, xs_hbm, zbuf, sem, zsem):
    base = pl.program_id(0) * TD_DISPATCH

    @pl.when(pl.program_id(0) == 0)
    def _():
        zbuf[...] = jnp.zeros_like(zbuf)
        n_bits = TR_MOE.bit_length() - 1

        def pad_copies(e, wait):
            start, n = ps_ref[e], pl_ref[e]
            head = jnp.minimum((-start) & 7, n)
            for k in range(7):
                @pl.when(k < head)
                def _():
                    cp = pltpu.make_async_copy(zbuf.at[pl.ds(0, 1)], xs_hbm.at[pl.ds(start + k, 1)], zsem)
                    cp.wait() if wait else cp.start()
            body0, nb = start + head, n - head
            for k in range(3, n_bits):
                size = 1 << k

                @pl.when(((nb >> k) & 1) == 1)
                def _():
                    off = pl.multiple_of(body0 + (nb & (size - 1)), 8)
                    cp = pltpu.make_async_copy(zbuf.at[pl.ds(0, size)], xs_hbm.at[pl.ds(off, size)], zsem)
                    cp.wait() if wait else cp.start()

        def fill(e, carry):
            pad_copies(e, False)
            return carry

        def fill_wait(e, carry):
            pad_copies(e, True)
            return carry

        def tail_copies(j, wait):
            @pl.when(j >= nu_ref[0])
            def _():
                off = pl.multiple_of(j * TR_MOE, TR_MOE)
                cp = pltpu.make_async_copy(zbuf, xs_hbm.at[pl.ds(off, TR_MOE)], zsem)
                cp.wait() if wait else cp.start()

        def tail(j, carry):
            tail_copies(j, False)
            return carry

        def tail_wait(j, carry):
            tail_copies(j, True)
            return carry

        n_tiles = xs_hbm.shape[0] // TR_MOE
        lax.fori_loop(0, N_EXPERTS, fill, 0)
        lax.fori_loop(0, n_tiles, tail, 0)
        lax.fori_loop(0, N_EXPERTS, fill_wait, 0)
        lax.fori_loop(0, n_tiles, tail_wait, 0)

    def row_copy(t, dst):
        return pltpu.make_async_copy(h_ref.at[pl.ds(t, 1)], xs_hbm.at[pl.ds(dst, 1)], sem)

    def issue(t, carry):
        row_copy(t, d1_ref[base + t]).start()
        row_copy(t, d2_ref[base + t]).start()
        return carry

    lax.fori_loop(0, TD_DISPATCH, issue, 0)

    def drain(t, carry):
        row_copy(0, 0).wait()
        row_copy(0, 0).wait()
        return carry

    lax.fori_loop(0, TD_DISPATCH, drain, 0)


def _dispatch(d1, d2, pad_start, pad_len, n_used, h2p, R):
    T = h2p.shape[0]
    return pl.pallas_call(
        _dispatch_kernel,
        grid_spec=pltpu.PrefetchScalarGridSpec(
            num_scalar_prefetch=5,
            grid=(T // TD_DISPATCH,),
            in_specs=[pl.BlockSpec((TD_DISPATCH, D_MODEL // 2), lambda i, *_: (i, 0))],
            out_specs=pl.BlockSpec(memory_space=pl.ANY),
            scratch_shapes=[pltpu.VMEM((TR_MOE, D_MODEL // 2), U32),
                            pltpu.SemaphoreType.DMA(()),
                            pltpu.SemaphoreType.DMA(())]),
        out_shape=jax.ShapeDtypeStruct((R, D_MODEL // 2), U32),
        compiler_params=_cparams(("arbitrary",)),
        name="dispatch",
    )(d1, d2, pad_start, pad_len, n_used, h2p)


def _moe_kernel(be_ref, nu_ref, xs_ref, w1_ref, w3_ref, w2_ref, yb_ref, w1b, w3b, w2b):
    j = pl.program_id(0)

    @pl.when(j >= nu_ref[0])
    def _():
        yb_ref[...] = jnp.zeros_like(yb_ref)

    @pl.when(j < nu_ref[0])
    def _():
        e = be_ref[j]
        e_prev = be_ref[jnp.maximum(j - 1, 0)]

        @pl.when((j == 0) | (e != e_prev))
        def _():
            w1b[...] = w1_ref[...].astype(BF16)
            w3b[...] = w3_ref[...].astype(BF16)
            w2b[...] = w2_ref[...].astype(BF16)

        w = xs_ref[...]
        half = D_MODEL // 2
        lo = lax.bitcast_convert_type(w & jnp.uint32(0xFFFF0000), F32).astype(BF16)
        hi = lax.bitcast_convert_type(w << 16, F32).astype(BF16)
        a = jnp.dot(lo, w1b[:half, :], preferred_element_type=F32)
        a += jnp.dot(hi, w1b[half:, :], preferred_element_type=F32)
        b = jnp.dot(lo, w3b[:half, :], preferred_element_type=F32)
        b += jnp.dot(hi, w3b[half:, :], preferred_element_type=F32)
        hmid = (a / (1.0 + jnp.exp(-a)) * b).astype(BF16)
        yb_ref[...] = jnp.dot(hmid, w2b[...], preferred_element_type=F32)


def _moe(blk_exp, n_used, xs, w1, w3, w2):
    R = xs.shape[0]
    tr = TR_MOE

    def row_map(j, be, nu):
        return (jnp.minimum(j, nu[0] - 1), 0)

    def w_map(j, be, nu):
        return (be[jnp.minimum(j, nu[0] - 1)], 0, 0)

    return pl.pallas_call(
        _moe_kernel,
        grid_spec=pltpu.PrefetchScalarGridSpec(
            num_scalar_prefetch=2,
            grid=(R // tr,),
            in_specs=[pl.BlockSpec((tr, D_MODEL // 2), row_map),
                      pl.BlockSpec((None, D_MODEL, EXPERT_HIDDEN), w_map),
                      pl.BlockSpec((None, D_MODEL, EXPERT_HIDDEN), w_map),
                      pl.BlockSpec((None, EXPERT_HIDDEN, D_MODEL), w_map)],
            out_specs=pl.BlockSpec((tr, D_MODEL), lambda j, be, nu: (j, 0)),
            scratch_shapes=[pltpu.VMEM((D_MODEL, EXPERT_HIDDEN), BF16),
                            pltpu.VMEM((D_MODEL, EXPERT_HIDDEN), BF16),
                            pltpu.VMEM((EXPERT_HIDDEN, D_MODEL), BF16)]),
        out_shape=jax.ShapeDtypeStruct((R, D_MODEL), F32),
        compiler_params=_cparams(("arbitrary",)),
        name="moe_experts",
    )(blk_exp, n_used, xs, w1, w3, w2)


def _combine_kernel(d1_ref, d2_ref, x2_ref, rinfo_ref, yb_hbm, o_ref, buf, sem):
    tc = x2_ref.shape[0]
    base = pl.program_id(0) * tc

    def row_copy(src, k, t):
        return pltpu.make_async_copy(yb_hbm.at[pl.ds(src, 1)], buf.at[k, pl.ds(t, 1)], sem)

    def issue(t, carry):
        row_copy(d1_ref[base + t], 0, t).start()
        row_copy(d2_ref[base + t], 1, t).start()
        return carry

    lax.fori_loop(0, tc, issue, 0)

    def drain(t, carry):
        row_copy(0, 0, 0).wait()
        row_copy(0, 1, 0).wait()
        return carry

    lax.fori_loop(0, tc, drain, 0)
    g1 = rinfo_ref[:, 4:5]
    g2 = rinfo_ref[:, 5:6]
    o_ref[...] = x2_ref[...] + (g1 * buf[0] + g2 * buf[1])


def _combine(d1, d2, x2, rinfo, yb):
    T = x2.shape[0]
    tc = TC_COMBINE
    return pl.pallas_call(
        _combine_kernel,
        grid_spec=pltpu.PrefetchScalarGridSpec(
            num_scalar_prefetch=2,
            grid=(T // tc,),
            in_specs=[pl.BlockSpec((tc, D_MODEL), lambda i, a, b: (i, 0)),
                      pl.BlockSpec((tc, LANES), lambda i, a, b: (i, 0)),
                      pl.BlockSpec(memory_space=pl.ANY)],
            out_specs=pl.BlockSpec((tc, D_MODEL), lambda i, a, b: (i, 0)),
            scratch_shapes=[pltpu.VMEM((2, tc, D_MODEL), F32),
                            pltpu.SemaphoreType.DMA(())]),
        out_shape=jax.ShapeDtypeStruct((T, D_MODEL), F32),
        compiler_params=_cparams(("arbitrary",)),
        name="combine",
    )(d1, d2, x2, rinfo, yb)


def _layer(x, mem, attn_norm, w_in, q_norm, k_norm, pool_proj, pool_scale, mem_norm, w_mem_kv,
           mq_norm, mk_norm, w_out, ffn_norm, w_group, b_group, w_router, b_router, w1, w3, w2):
    B, S, D = x.shape
    T = B * S
    assert D == D_MODEL and S % SUPER == 0 and T % TM_PROJ == 0
    x2d = x.reshape(T, D)
    row = lambda v: v.reshape(1, -1).astype(F32)
    scale = HEAD_DIM ** -0.5
    gq = row(jnp.tile(q_norm, ATTN_WIDTH // HEAD_DIM) * scale)
    gk = row(jnp.tile(k_norm, ATTN_WIDTH // HEAD_DIM))
    gmq = row(jnp.tile(mq_norm, MEM_WIDTH // HEAD_DIM) * scale)
    gmk = row(jnp.tile(mk_norm, MEM_WIDTH // HEAD_DIM))

    (q1, k1, v1, q4, k4, v4, q16, k16, v16, u, qm) = _proj(
        x2d, row(attn_norm), w_in.astype(BF16), gq, gk, gmq, B, S)
    ya = _attn(q1, k1, v1, q4, k4, v4, q16, k16, v16, B, S).reshape(T, ATTN_WIDTH)
    km, vm = _memkv(mem, row(mem_norm), w_mem_kv.astype(BF16), gmk, B)

    pool_bd = jax.scipy.linalg.block_diag(*[pool_proj[g] for g in range(pool_proj.shape[0])]).astype(BF16)
    w_r = jnp.zeros((D, LANES), F32)
    w_r = w_r.at[:, :N_GROUPS].set(w_group)
    w_r = w_r.at[:, EXPERT_LANE0:].set(jnp.transpose(w_router, (1, 0, 2)).reshape(D, N_EXPERTS))
    b_r = jnp.zeros((1, LANES), F32).at[0, :N_GROUPS].set(b_group).at[0, EXPERT_LANE0:].set(b_router.reshape(-1))
    x2, h2p, rinfo, rinfot, counts = _mix(ya, u, qm, km, vm, x2d, pool_bd, row(pool_scale),
                                          w_out.astype(BF16), row(ffn_norm), w_r.astype(BF16), b_r, B, S)

    tr = TR_MOE
    R = 2 * T + N_EXPERTS * tr
    n_tiles = R // tr
    cnt = counts[0, EXPERT_LANE0:].astype(I32)
    padded = ((cnt + tr - 1) // tr) * tr
    pend = jnp.cumsum(padded)
    pstart = pend - padded
    n_used = (pend[-1] // tr).astype(I32).reshape(1)
    tile_row = jnp.minimum(jnp.arange(n_tiles, dtype=I32), n_used[0] - 1) * tr
    blk_exp = jnp.minimum(jnp.searchsorted(pend, tile_row, side="right"), N_EXPERTS - 1).astype(I32)

    dest = _dest(rinfot, pstart.astype(F32).reshape(N_EXPERTS, 1), T)
    d1, d2 = dest[0], dest[1]
    xs = _dispatch(d1, d2, pstart + cnt, padded - cnt, n_used, h2p, R)
    yb = _moe(blk_exp, n_used, xs, w1, w3, w2)
    out = _combine(d1, d2, x2, rinfo, yb)
    return out.reshape(B, S, D)


def kernel(x, mem, attn_norm, w_in, q_norm, k_norm, pool_proj, pool_scale, mem_norm, w_mem_kv, mq_norm, mk_norm,
           w_out, ffn_norm, w_group, b_group, w_router, b_router, w1, w3, w2):
    for l in range(attn_norm.shape[0]):
        x = _layer(x, mem, attn_norm[l], w_in[l], q_norm[l], k_norm[l], pool_proj[l], pool_scale[l],
                   mem_norm[l], w_mem_kv[l], mq_norm[l], mk_norm[l], w_out[l], ffn_norm[l],
                   w_group[l], b_group[l], w_router[l], b_router[l], w1[l], w3[l], w2[l])
    return x
```

```python
import functools

import jax
import jax.numpy as jnp
from jax import lax
from jax.experimental import pallas as pl
from jax.experimental.pallas import tpu as pltpu

F32 = jnp.float32
BF16 = jnp.bfloat16
I32 = jnp.int32
U32 = jnp.uint32

D_MODEL = 1024
HEAD_DIM = 64
ATTN_WIDTH = 512
POOL_WIDTH = 256
MEM_WIDTH = 256
N_MEM = 256
IN_WIDTH = 3 * ATTN_WIDTH + POOL_WIDTH + MEM_WIDTH
N_GROUPS = 8
EXPERTS_PER_GROUP = 8
N_EXPERTS = 64
EXPERT_HIDDEN = 512
EPS = 1e-6
NEG_INF = -1e30

LANES = 128
N_PAIRS = ATTN_WIDTH // LANES
BLOCK = 128
SUPER = 16 * BLOCK
HALO = 16

TM_PROJ = 512
TM_MIX = 256
TD_DISPATCH = 256
TR_MOE = 256
TC_COMBINE = 256
EXPERT_LANE0 = 64

VMEM_LIMIT = 48 * 1024 * 1024


def _cparams(sem):
    return pltpu.CompilerParams(dimension_semantics=sem, vmem_limit_bytes=VMEM_LIMIT)


def _group_ones(n):
    g = jnp.arange(n) // HEAD_DIM
    return (g[:, None] == g[None, :]).astype(BF16)


def _head_norm(z, ones_ref, gain_ref):
    ss = jnp.dot((z * z).astype(BF16), ones_ref[...], preferred_element_type=F32)
    return z * lax.rsqrt(ss * (1.0 / HEAD_DIM) + EPS) * gain_ref[...]


def _proj_kernel(x_ref, an_ref, win_ref, gq_ref, gk_ref, gm_ref, o512_ref, o256_ref,
                 q1_ref, k1_ref, v1_ref, q4_ref, k4_ref, v4_ref, q16_ref, k16_ref, v16_ref,
                 u_ref, qm_ref, zs_ref):
    tm = x_ref.shape[0]
    x = x_ref[...]
    ms = jnp.mean(x * x, axis=-1, keepdims=True)
    h = (x * lax.rsqrt(ms + EPS) * an_ref[...]).astype(BF16)
    z = jnp.dot(h, win_ref[...], preferred_element_type=F32)
    a0, a1, a2, a3 = ATTN_WIDTH, 2 * ATTN_WIDTH, 3 * ATTN_WIDTH, 3 * ATTN_WIDTH + POOL_WIDTH
    qn = _head_norm(z[:, :a0], o512_ref, gq_ref)
    kn = _head_norm(z[:, a0:a1], o512_ref, gk_ref)
    v = z[:, a1:a2]
    u_ref[...] = z[:, a2:a3]
    qm_ref[...] = _head_norm(z[:, a3:], o256_ref, gm_ref).astype(BF16)
    for val, o1, o4, o16 in ((qn, q1_ref, q4_ref, q16_ref), (kn, k1_ref, k4_ref, k16_ref),
                             (v, v1_ref, v4_ref, v16_ref)):
        for hp in range(N_PAIRS):
            pair = val[:, hp * LANES:(hp + 1) * LANES]
            zs_ref[hp] = pair
            o1[hp] = pair.astype(BF16)
        for d, o in ((4, o4), (16, o16)):
            for r in range(d):
                for hp in range(N_PAIRS):
                    rows = zs_ref[hp, pl.ds(r, tm // d, stride=d), :]
                    o[hp, :, r * LANES:(r + 1) * LANES] = rows.astype(BF16)


def _proj(x2d, attn_norm, w_in, gq, gk, gm, B, S):
    T = B * S
    tm = TM_PROJ
    nj = S // tm
    const = lambda i: (0, 0)

    def lay(d):
        return jax.ShapeDtypeStruct((B, N_PAIRS, S // d, d * LANES), BF16)

    def lay_spec(d):
        return pl.BlockSpec((None, N_PAIRS, tm // d, d * LANES), lambda i: (i // nj, 0, i % nj, 0))

    out_shape = [lay(1)] * 3 + [lay(4)] * 3 + [lay(16)] * 3 + [
        jax.ShapeDtypeStruct((T, POOL_WIDTH), F32), jax.ShapeDtypeStruct((T, MEM_WIDTH), BF16)]
    out_specs = [lay_spec(1)] * 3 + [lay_spec(4)] * 3 + [lay_spec(16)] * 3 + [
        pl.BlockSpec((tm, POOL_WIDTH), lambda i: (i, 0)), pl.BlockSpec((tm, MEM_WIDTH), lambda i: (i, 0))]
    return pl.pallas_call(
        _proj_kernel,
        grid=(T // tm,),
        in_specs=[pl.BlockSpec((tm, D_MODEL), lambda i: (i, 0)),
                  pl.BlockSpec((1, D_MODEL), const),
                  pl.BlockSpec((D_MODEL, IN_WIDTH), const),
                  pl.BlockSpec((1, ATTN_WIDTH), const),
                  pl.BlockSpec((1, ATTN_WIDTH), const),
                  pl.BlockSpec((1, MEM_WIDTH), const),
                  pl.BlockSpec((ATTN_WIDTH, ATTN_WIDTH), const),
                  pl.BlockSpec((MEM_WIDTH, MEM_WIDTH), const)],
        out_specs=out_specs,
        out_shape=out_shape,
        scratch_shapes=[pltpu.VMEM((N_PAIRS, tm, LANES), F32)],
        compiler_params=_cparams(("parallel",)),
        name="proj",
    )(x2d, attn_norm, w_in, gq, gk, gm, _group_ones(ATTN_WIDTH), _group_ones(MEM_WIDTH))


def _attn_kernel(q1, k1c, k1p, v1c, v1p, q4, k4c, k4p, v4c, v4p, q16, k16c, k16p, v16c, v16p,
                 bias_ref, o_ref, acc, mrun, lrun):
    c = pl.program_id(2)
    lane = lax.broadcasted_iota(I32, (BLOCK, LANES), 1)
    is_a = lane < HEAD_DIM
    lane_row = lax.broadcasted_iota(I32, (1, LANES), 1)
    mask_a = jnp.where(lane_row < HEAD_DIM, 1.0, 0.0).astype(BF16)
    mask_b = jnp.where(lane_row < HEAD_DIM, 0.0, 1.0).astype(BF16)
    bias_full = bias_ref[0]
    bias_first = jnp.where(c > 0, bias_full, bias_ref[1])

    def tile(q_t, kp_t, kc_t, vp_t, vc_t, bias):
        lhs = jnp.concatenate([q_t * mask_a, q_t * mask_b], axis=0)
        keys = jnp.concatenate([kp_t, kc_t], axis=0)
        s = lax.dot_general(lhs, keys, (((1,), (1,)), ((), ())), preferred_element_type=F32) + bias
        m = jnp.max(s, axis=-1, keepdims=True)
        p = jnp.exp(s - m)
        l = jnp.sum(p, axis=-1, keepdims=True)
        vals = jnp.concatenate([vp_t, vc_t], axis=0)
        pv = jnp.dot(p.astype(BF16), vals, preferred_element_type=F32)
        o_t = jnp.where(is_a, pv[:BLOCK], pv[BLOCK:])
        m_t = jnp.where(is_a, m[:BLOCK], m[BLOCK:])
        l_t = jnp.where(is_a, l[:BLOCK], l[BLOCK:])
        return o_t, m_t, l_t

    def merge(rows, o_t, m_t, l_t):
        m_o = mrun[rows, :]
        m_n = jnp.maximum(m_o, m_t)
        a = jnp.exp(m_o - m_n)
        b = jnp.exp(m_t - m_n)
        acc[rows, :] = acc[rows, :] * a + o_t * b
        lrun[rows, :] = lrun[rows, :] * a + l_t * b
        mrun[rows, :] = m_n

    for jb in range(SUPER // BLOCK):
        cur = pl.ds(jb * BLOCK, BLOCK)
        if jb == 0:
            kp_t, vp_t, bias = k1p[...], v1p[...], bias_first
        else:
            prev = pl.ds((jb - 1) * BLOCK, BLOCK)
            kp_t, vp_t, bias = k1c[prev, :], v1c[prev, :], bias_full
        o_t, m_t, l_t = tile(q1[cur, :], kp_t, k1c[cur, :], vp_t, v1c[cur, :], bias)
        acc[cur, :] = o_t
        mrun[cur, :] = m_t
        lrun[cur, :] = l_t
    for d, q, kc, kp, vc, vp in ((4, q4, k4c, k4p, v4c, v4p), (16, q16, k16c, k16p, v16c, v16p)):
        nblk = SUPER // (BLOCK * d)
        for r in range(d):
            cols = pl.ds(r * LANES, LANES)
            for jb in range(nblk):
                cur = pl.ds(jb * BLOCK, BLOCK)
                if jb == 0:
                    kp_t, vp_t, bias = kp[:, cols], vp[:, cols], bias_first
                else:
                    prev = pl.ds((jb - 1) * BLOCK, BLOCK)
                    kp_t, vp_t, bias = kc[prev, cols], vc[prev, cols], bias_full
                o_t, m_t, l_t = tile(q[cur, cols], kp_t, kc[cur, cols], vp_t, vc[cur, cols], bias)
                merge(pl.ds(jb * BLOCK * d + r, BLOCK, stride=d), o_t, m_t, l_t)
    o_ref[...] = (acc[...] / lrun[...]).astype(BF16)


def _band_bias():
    qi = jnp.arange(BLOCK)[:, None]
    kj = jnp.arange(2 * BLOCK)[None, :]
    dist = qi + BLOCK - kj
    in_band = (dist >= 0) & (dist <= BLOCK)
    full = jnp.where(in_band, 0.0, NEG_INF).astype(F32)
    first = jnp.where(in_band & (kj >= BLOCK), 0.0, NEG_INF).astype(F32)
    return jnp.stack([jnp.tile(full, (2, 1)), jnp.tile(first, (2, 1))])


def _attn(q1, k1, v1, q4, k4, v4, q16, k16, v16, B, S):
    nsup = S // SUPER

    def specs(d):
        rows = SUPER // d
        per = rows // BLOCK
        cur = pl.BlockSpec((None, None, rows, d * LANES), lambda b, hp, c: (b, hp, c, 0))
        prev = pl.BlockSpec((None, None, BLOCK, d * LANES),
                            lambda b, hp, c: (b, hp, jnp.maximum(per * c - 1, 0), 0))
        return [cur, cur, prev, cur, prev]

    return pl.pallas_call(
        _attn_kernel,
        grid=(B, N_PAIRS, nsup),
        in_specs=specs(1) + specs(4) + specs(16) + [
            pl.BlockSpec((2, 2 * BLOCK, 2 * BLOCK), lambda b, hp, c: (0, 0, 0))],
        out_specs=pl.BlockSpec((None, SUPER, LANES), lambda b, hp, c: (b, c, hp)),
        out_shape=jax.ShapeDtypeStruct((B, S, ATTN_WIDTH), BF16),
        scratch_shapes=[pltpu.VMEM((SUPER, LANES), F32)] * 3,
        compiler_params=_cparams(("parallel", "parallel", "parallel")),
        name="dilated_attn",
    )(q1, k1, k1, v1, v1, q4, k4, k4, v4, v4, q16, k16, k16, v16, v16, _band_bias())


def _memkv_kernel(mem_ref, mn_ref, wkv_ref, gk_ref, o256_ref, km_ref, vm_ref):
    m = mem_ref[...]
    ms = jnp.mean(m * m, axis=-1, keepdims=True)
    mn = (m * lax.rsqrt(ms + EPS) * mn_ref[...]).astype(BF16)
    kv = jnp.dot(mn, wkv_ref[...], preferred_element_type=F32)
    km_ref[...] = _head_norm(kv[:, :MEM_WIDTH], o256_ref, gk_ref).astype(BF16)
    vm_ref[...] = kv[:, MEM_WIDTH:].astype(BF16)


def _memkv(mem, mem_norm, w_mem_kv, gmk, B):
    const = lambda b: (0, 0)
    return pl.pallas_call(
        _memkv_kernel,
        grid=(B,),
        in_specs=[pl.BlockSpec((None, N_MEM, D_MODEL), lambda b: (b, 0, 0)),
                  pl.BlockSpec((1, D_MODEL), const),
                  pl.BlockSpec((D_MODEL, 2 * MEM_WIDTH), const),
                  pl.BlockSpec((1, MEM_WIDTH), const),
                  pl.BlockSpec((MEM_WIDTH, MEM_WIDTH), const)],
        out_specs=[pl.BlockSpec((None, N_MEM, MEM_WIDTH), lambda b: (b, 0, 0))] * 2,
        out_shape=[jax.ShapeDtypeStruct((B, N_MEM, MEM_WIDTH), BF16)] * 2,
        compiler_params=_cparams(("parallel",)),
        name="memkv",
    )(mem, mem_norm, w_mem_kv, gmk, _group_ones(MEM_WIDTH))


def _mix_kernel(tiles_per_seq, ya_ref, u_ref, uh_ref, qm_ref, km_ref, vm_ref, x_ref, pp_ref, ps_ref,
                wo_ref, fn_ref, wr_ref, br_ref, ltri_ref,
                x2_ref, h2p_ref, rinfo_ref, rinfot_ref, cnt_ref, base_ref):
    i = pl.program_id(0)
    tm = x_ref.shape[0]
    seq_tile = i % tiles_per_seq

    @pl.when(i == 0)
    def _():
        base_ref[...] = jnp.zeros_like(base_ref)

    u = u_ref[...]
    halo = jnp.where(seq_tile == 0, 0.0, uh_ref[...])
    uu = jnp.concatenate([halo, u], axis=0)
    a1 = uu[1:] + uu[:-1]
    a2 = a1[2:] + a1[:-2]
    a3 = a2[4:] + a2[:-4]
    a4 = a3[8:] + a3[:-8]
    lane_p = lax.broadcasted_iota(I32, (tm, POOL_WIDTH), 1)
    g0, g1, g2 = lane_p < 64, lane_p < 128, lane_p < 192
    wsum = jnp.where(g0, a1[15:], jnp.where(g1, a2[13:], jnp.where(g2, a3[9:], a4[1:])))
    wlen = jnp.where(g0, 2.0, jnp.where(g1, 4.0, jnp.where(g2, 8.0, 16.0)))
    tpos = seq_tile * tm + lax.broadcasted_iota(I32, (tm, POOL_WIDTH), 0) + 1
    cnt = jnp.minimum(tpos.astype(F32), wlen)
    pooled = wsum / cnt - u
    y_pool = jnp.dot(pooled.astype(BF16), pp_ref[...], preferred_element_type=F32) * ps_ref[...]

    lane = lax.broadcasted_iota(I32, (tm, LANES), 1)
    is_a = lane < HEAD_DIM
    lane_row = lax.broadcasted_iota(I32, (1, LANES), 1)
    mask_a = jnp.where(lane_row < HEAD_DIM, 1.0, 0.0).astype(BF16)
    mask_b = jnp.where(lane_row < HEAD_DIM, 0.0, 1.0).astype(BF16)
    y_mem = []
    for pr in range(MEM_WIDTH // LANES):
        cols = slice(pr * LANES, (pr + 1) * LANES)
        qp, kp, vp = qm_ref[:, cols], km_ref[:, cols], vm_ref[:, cols]
        outs = []
        for msk in (mask_a, mask_b):
            s = lax.dot_general(qp * msk, kp, (((1,), (1,)), ((), ())), preferred_element_type=F32)
            m = jnp.max(s, axis=-1, keepdims=True)
            p = jnp.exp(s - m)
            p = p / jnp.sum(p, axis=-1, keepdims=True)
            outs.append(jnp.dot(p.astype(BF16), vp, preferred_element_type=F32))
        y_mem.append(jnp.where(is_a, outs[0], outs[1]))

    a0, a1w = ATTN_WIDTH, ATTN_WIDTH + POOL_WIDTH
    proj = jnp.dot(ya_ref[...], wo_ref[:a0, :], preferred_element_type=F32)
    proj += jnp.dot(y_pool.astype(BF16), wo_ref[a0:a1w, :], preferred_element_type=F32)
    for pr in range(MEM_WIDTH // LANES):
        lo = a1w + pr * LANES
        proj += jnp.dot(y_mem[pr].astype(BF16), wo_ref[lo:lo + LANES, :], preferred_element_type=F32)
    x2 = x_ref[...] + proj
    x2_ref[...] = x2

    ms = jnp.mean(x2 * x2, axis=-1, keepdims=True)
    h2 = (x2 * lax.rsqrt(ms + EPS) * fn_ref[...]).astype(BF16)
    bits = lax.bitcast_convert_type(h2.astype(F32), U32)
    half = D_MODEL // 2
    h2p_ref[...] = bits[:, :half] | (bits[:, half:] >> 16)

    logits = jnp.dot(h2, wr_ref[...], preferred_element_type=F32) + br_ref[...]
    ninf = -jnp.inf
    is_g = lane < N_GROUPS
    lg = jnp.where(is_g, logits, ninf)
    mg = jnp.max(lg, axis=-1, keepdims=True)
    g_sel = jnp.min(jnp.where(lg == mg, lane, LANES), axis=-1, keepdims=True)
    w_g = 1.0 / jnp.sum(jnp.where(is_g, jnp.exp(logits - mg), 0.0), axis=-1, keepdims=True)
    in_grp = (lane >= EXPERT_LANE0) & (((lane - EXPERT_LANE0) >> 3) == g_sel)
    le = jnp.where(in_grp, logits, ninf)
    v1 = jnp.max(le, axis=-1, keepdims=True)
    i1 = jnp.min(jnp.where(le == v1, lane, 2 * LANES), axis=-1, keepdims=True)
    le2 = jnp.where(lane == i1, ninf, le)
    v2 = jnp.max(le2, axis=-1, keepdims=True)
    i2 = jnp.min(jnp.where(le2 == v2, lane, 2 * LANES), axis=-1, keepdims=True)
    e21 = jnp.exp(v2 - v1)
    gate1 = w_g / (1.0 + e21)
    gate2 = w_g * e21 / (1.0 + e21)

    hot1 = lane == i1
    hot2 = lane == i2
    onehot = jnp.where(hot1 | hot2, 1.0, 0.0)
    prefix = jnp.dot(ltri_ref[...], onehot.astype(BF16), preferred_element_type=F32)
    tot = prefix + base_ref[...]
    rank1 = jnp.sum(jnp.where(hot1, tot, 0.0), axis=-1, keepdims=True)
    rank2 = jnp.sum(jnp.where(hot2, tot, 0.0), axis=-1, keepdims=True)
    base_new = base_ref[...] + jnp.sum(onehot, axis=0, keepdims=True)
    base_ref[...] = base_new
    cnt_ref[...] = base_new

    e1 = (i1 - EXPERT_LANE0).astype(F32)
    e2 = (i2 - EXPERT_LANE0).astype(F32)
    rinfo = jnp.where(lane == 0, e1, jnp.where(lane == 1, e2, jnp.where(lane == 2, rank1, jnp.where(
        lane == 3, rank2, jnp.where(lane == 4, gate1, jnp.where(lane == 5, gate2, 0.0))))))
    rinfo_ref[...] = rinfo
    rinfot_ref[...] = jnp.transpose(rinfo)[:8, :]


def _mix(ya, u, qm, km, vm, x2d, pool_bd, pool_scale, w_out, ffn_norm, w_r, b_r, B, S):
    T = B * S
    tm = TM_MIX
    tps = S // tm
    hb = tm // HALO
    const = lambda i: (0, 0)
    ltri = (jnp.arange(tm)[:, None] > jnp.arange(tm)[None, :]).astype(BF16)
    return pl.pallas_call(
        functools.partial(_mix_kernel, tps),
        grid=(T // tm,),
        in_specs=[pl.BlockSpec((tm, ATTN_WIDTH), lambda i: (i, 0)),
                  pl.BlockSpec((tm, POOL_WIDTH), lambda i: (i, 0)),
                  pl.BlockSpec((HALO, POOL_WIDTH), lambda i: (jnp.maximum(i * hb - 1, 0), 0)),
                  pl.BlockSpec((tm, MEM_WIDTH), lambda i: (i, 0)),
                  pl.BlockSpec((None, N_MEM, MEM_WIDTH), lambda i: (i // tps, 0, 0)),
                  pl.BlockSpec((None, N_MEM, MEM_WIDTH), lambda i: (i // tps, 0, 0)),
                  pl.BlockSpec((tm, D_MODEL), lambda i: (i, 0)),
                  pl.BlockSpec((POOL_WIDTH, POOL_WIDTH), const),
                  pl.BlockSpec((1, POOL_WIDTH), const),
                  pl.BlockSpec((D_MODEL, D_MODEL), const),
                  pl.BlockSpec((1, D_MODEL), const),
                  pl.BlockSpec((D_MODEL, LANES), const),
                  pl.BlockSpec((1, LANES), const),
                  pl.BlockSpec((tm, tm), const)],
        out_specs=[pl.BlockSpec((tm, D_MODEL), lambda i: (i, 0)),
                   pl.BlockSpec((tm, D_MODEL // 2), lambda i: (i, 0)),
                   pl.BlockSpec((tm, LANES), lambda i: (i, 0)),
                   pl.BlockSpec((8, tm), lambda i: (0, i)),
                   pl.BlockSpec((1, LANES), const)],
        out_shape=[jax.ShapeDtypeStruct((T, D_MODEL), F32),
                   jax.ShapeDtypeStruct((T, D_MODEL // 2), U32),
                   jax.ShapeDtypeStruct((T, LANES), F32),
                   jax.ShapeDtypeStruct((8, T), F32),
                   jax.ShapeDtypeStruct((1, LANES), F32)],
        scratch_shapes=[pltpu.VMEM((1, LANES), F32)],
        compiler_params=_cparams(("arbitrary",)),
        name="mix_router",
    )(ya, u, u, qm, km, vm, x2d, pool_bd, pool_scale, w_out, ffn_norm, w_r, b_r, ltri)


def _dest_kernel(rt_ref, ps_ref, d_ref):
    tn = rt_ref.shape[1]
    sub = lax.broadcasted_iota(I32, (N_EXPERTS, tn), 0).astype(F32)
    ps = ps_ref[...]
    rows = []
    for k in range(2):
        e = rt_ref[k:k + 1, :]
        start = jnp.sum(jnp.where(sub == e, ps, 0.0), axis=0, keepdims=True)
        rows.append(start + rt_ref[2 + k:3 + k, :])
    d = jnp.concatenate(rows + [jnp.zeros((6, tn), F32)], axis=0)
    d_ref[...] = d.astype(I32)


def _dest(rinfot, pstart_col, T):
    tn = 2048
    return pl.pallas_call(
        _dest_kernel,
        grid=(T // tn,),
        in_specs=[pl.BlockSpec((8, tn), lambda i: (0, i)),
                  pl.BlockSpec((N_EXPERTS, 1), lambda i: (0, 0))],
        out_specs=pl.BlockSpec((8, tn), lambda i: (0, i)),
        out_shape=jax.ShapeDtypeStruct((8, T), I32),
        compiler_params=_cparams(("parallel",)),
        name="dest_rows",
    )(rinfot, pstart_col)


def _rowmap_kernel(d1_ref, d2_ref, fill_hbm, inv_ref, sem):
    T = d1_ref.shape[0]
    cp = pltpu.make_async_copy(fill_hbm, inv_ref, sem)
    cp.start()
    cp.wait()

    def body(t, carry):
        inv_ref[d1_ref[t]] = t
        inv_ref[d2_ref[t]] = T + t
        return carry

    lax.fori_loop(0, T, body, 0, unroll=8)


def _rowmap(d1, d2, R):
    return pl.pallas_call(
        _rowmap_kernel,
        grid_spec=pltpu.PrefetchScalarGridSpec(
            num_scalar_prefetch=2,
            grid=(1,),
            in_specs=[pl.BlockSpec(memory_space=pl.ANY)],
            out_specs=pl.BlockSpec(memory_space=pltpu.SMEM),
            scratch_shapes=[pltpu.SemaphoreType.DMA(())]),
        out_shape=jax.ShapeDtypeStruct((R,), I32),
        compiler_params=_cparams(("arbitrary",)),
        name="rowmap",
    )(d1, d2, jnp.zeros((R,), I32))


def _row_loop(n, body):
    def group(g, carry):
        for u in range(8):
            body(g * 8 + u)
        return carry

    def rest(i, carry):
        body(((n >> 3) << 3) + i)
        return carry

    lax.fori_loop(0, n >> 3, group, 0)
    lax.fori_loop(0, n & 7, rest, 0)


def _moe_kernel(be_ref, nu_ref, first_ref, nxt_ref, ws_ref, valid_ref, inv_ref,
                h_hbm, w1_hbm, w3_hbm, w2_hbm, o_hbm,
                xbuf, ybuf, w1s, w3s, w2s, w1b, w3b, w2b, gsem, ssem, wsem):
    j = pl.program_id(0)
    nu = nu_ref[0]
    T = h_hbm.shape[0]
    tr = TR_MOE

    def gather_row(tile, slot, i):
        v = inv_ref[tile * tr + i]
        tok = jnp.where(v >= T, v - T, v)
        return pltpu.make_async_copy(h_hbm.at[pl.ds(tok, 1)], xbuf.at[slot, pl.ds(i, 1)], gsem.at[slot])

    def gather_done(slot):
        return pltpu.make_async_copy(h_hbm.at[pl.ds(0, 1)], xbuf.at[slot, pl.ds(0, 1)], gsem.at[slot])

    def scatter_row(tile, slot, i):
        v = inv_ref[tile * tr + i]
        return pltpu.make_async_copy(ybuf.at[slot, pl.ds(i, 1)], o_hbm.at[pl.ds(v, 1)], ssem.at[slot])

    def scatter_done(slot):
        return pltpu.make_async_copy(ybuf.at[slot, pl.ds(0, 1)], o_hbm.at[pl.ds(0, 1)], ssem.at[slot])

    def weight_copies(e, slot):
        return [pltpu.make_async_copy(src.at[e], dst.at[slot], wsem.at[slot])
                for src, dst in ((w1_hbm, w1s), (w3_hbm, w3s), (w2_hbm, w2s))]

    @pl.when(j < nu)
    def _():
        s = j & 1
        e = be_ref[j]

        @pl.when(j == 0)
        def _():
            xbuf[...] = jnp.zeros_like(xbuf)
            _row_loop(valid_ref[0], lambda i: gather_row(0, 0, i).start())
            for cp in weight_copies(e, 0):
                cp.start()

        _row_loop(valid_ref[j], lambda i: gather_done(s).wait())

        @pl.when(j + 1 < nu)
        def _():
            _row_loop(valid_ref[j + 1], lambda i: gather_row(j + 1, 1 - s, i).start())

        @pl.when(first_ref[j] == 1)
        def _():
            wslot = ws_ref[j]
            for cp in weight_copies(e, wslot):
                cp.wait()
            e_next = nxt_ref[j]

            @pl.when(e_next >= 0)
            def _():
                for cp in weight_copies(e_next, 1 - wslot):
                    cp.start()

            w1b[...] = w1s[wslot].astype(BF16)
            w3b[...] = w3s[wslot].astype(BF16)
            w2b[...] = w2s[wslot].astype(BF16)

        @pl.when(j >= 2)
        def _():
            _row_loop(valid_ref[j - 2], lambda i: scatter_done(s).wait())

        w = xbuf[s]
        half = D_MODEL // 2
        lo = lax.bitcast_convert_type(w & jnp.uint32(0xFFFF0000), F32).astype(BF16)
        hi = lax.bitcast_convert_type(w << 16, F32).astype(BF16)
        a = jnp.dot(lo, w1b[:half, :], preferred_element_type=F32)
        a += jnp.dot(hi, w1b[half:, :], preferred_element_type=F32)
        b = jnp.dot(lo, w3b[:half, :], preferred_element_type=F32)
        b += jnp.dot(hi, w3b[half:, :], preferred_element_type=F32)
        hmid = (a / (1.0 + jnp.exp(-a)) * b).astype(BF16)
        ybuf[s] = jnp.dot(hmid, w2b[...], preferred_element_type=F32)

        _row_loop(valid_ref[j], lambda i: scatter_row(j, s, i).start())

        @pl.when(j == nu - 1)
        def _():
            _row_loop(valid_ref[j], lambda i: scatter_done(s).wait())

            @pl.when(j >= 1)
            def _():
                _row_loop(valid_ref[j - 1], lambda i: scatter_done(1 - s).wait())


def _moe(blk_exp, n_used, first, nxt, wslot, valid, inv, h2p, w1, w3, w2):
    T = h2p.shape[0]
    tr = TR_MOE
    n_tiles = blk_exp.shape[0]
    any_spec = pl.BlockSpec(memory_space=pl.ANY)
    return pl.pallas_call(
        _moe_kernel,
        grid_spec=pltpu.PrefetchScalarGridSpec(
            num_scalar_prefetch=7,
            grid=(n_tiles,),
            in_specs=[any_spec] * 4,
            out_specs=any_spec,
            scratch_shapes=[pltpu.VMEM((2, tr, D_MODEL // 2), U32),
                            pltpu.VMEM((2, tr, D_MODEL), F32),
                            pltpu.VMEM((2, D_MODEL, EXPERT_HIDDEN), F32),
                            pltpu.VMEM((2, D_MODEL, EXPERT_HIDDEN), F32),
                            pltpu.VMEM((2, EXPERT_HIDDEN, D_MODEL), F32),
                            pltpu.VMEM((D_MODEL, EXPERT_HIDDEN), BF16),
                            pltpu.VMEM((D_MODEL, EXPERT_HIDDEN), BF16),
                            pltpu.VMEM((EXPERT_HIDDEN, D_MODEL), BF16),
                            pltpu.SemaphoreType.DMA((2,)),
                            pltpu.SemaphoreType.DMA((2,)),
                            pltpu.SemaphoreType.DMA((2,))]),
        out_shape=jax.ShapeDtypeStruct((2 * T, D_MODEL), F32),
        compiler_params=_cparams(("arbitrary",)),
        name="moe_experts",
    )(blk_exp, n_used, first, nxt, wslot, valid, inv, h2p, w1, w3, w2)


def _combine_kernel(x2_ref, rinfo_ref, y1_ref, y2_ref, o_ref):
    g1 = rinfo_ref[:, 4:5]
    g2 = rinfo_ref[:, 5:6]
    o_ref[...] = x2_ref[...] + (g1 * y1_ref[...] + g2 * y2_ref[...])


def _combine(x2, rinfo, y2slot):
    T = x2.shape[0]
    tc = TC_COMBINE
    return pl.pallas_call(
        _combine_kernel,
        grid=(T // tc,),
        in_specs=[pl.BlockSpec((tc, D_MODEL), lambda i: (i, 0)),
                  pl.BlockSpec((tc, LANES), lambda i: (i, 0)),
                  pl.BlockSpec((None, tc, D_MODEL), lambda i: (0, i, 0)),
                  pl.BlockSpec((None, tc, D_MODEL), lambda i: (1, i, 0))],
        out_specs=pl.BlockSpec((tc, D_MODEL), lambda i: (i, 0)),
        out_shape=jax.ShapeDtypeStruct((T, D_MODEL), F32),
        compiler_params=_cparams(("parallel",)),
        name="combine",
    )(x2, rinfo, y2slot, y2slot)


def _layer(x, mem, attn_norm, w_in, q_norm, k_norm, pool_proj, pool_scale, mem_norm, w_mem_kv,
           mq_norm, mk_norm, w_out, ffn_norm, w_group, b_group, w_router, b_router, w1, w3, w2):
    B, S, D = x.shape
    T = B * S
    assert D == D_MODEL and S % SUPER == 0 and T % TM_PROJ == 0
    x2d = x.reshape(T, D)
    row = lambda v: v.reshape(1, -1).astype(F32)
    scale = HEAD_DIM ** -0.5
    gq = row(jnp.tile(q_norm, ATTN_WIDTH // HEAD_DIM) * scale)
    gk = row(jnp.tile(k_norm, ATTN_WIDTH // HEAD_DIM))
    gmq = row(jnp.tile(mq_norm, MEM_WIDTH // HEAD_DIM) * scale)
    gmk = row(jnp.tile(mk_norm, MEM_WIDTH // HEAD_DIM))

    (q1, k1, v1, q4, k4, v4, q16, k16, v16, u, qm) = _proj(
        x2d, row(attn_norm), w_in.astype(BF16), gq, gk, gmq, B, S)
    ya = _attn(q1, k1, v1, q4, k4, v4, q16, k16, v16, B, S).reshape(T, ATTN_WIDTH)
    km, vm = _memkv(mem, row(mem_norm), w_mem_kv.astype(BF16), gmk, B)

    pool_bd = jax.scipy.linalg.block_diag(*[pool_proj[g] for g in range(pool_proj.shape[0])]).astype(BF16)
    w_r = jnp.zeros((D, LANES), F32)
    w_r = w_r.at[:, :N_GROUPS].set(w_group)
    w_r = w_r.at[:, EXPERT_LANE0:].set(jnp.transpose(w_router, (1, 0, 2)).reshape(D, N_EXPERTS))
    b_r = jnp.zeros((1, LANES), F32).at[0, :N_GROUPS].set(b_group).at[0, EXPERT_LANE0:].set(b_router.reshape(-1))
    x2, h2p, rinfo, rinfot, counts = _mix(ya, u, qm, km, vm, x2d, pool_bd, row(pool_scale),
                                          w_out.astype(BF16), row(ffn_norm), w_r.astype(BF16), b_r, B, S)

    R, pstart, sched = _tile_schedule(counts, T)
    dest = _dest(rinfot, pstart.astype(F32).reshape(N_EXPERTS, 1), T)
    inv = _rowmap(dest[0], dest[1], R)
    y2slot = _moe(*sched, inv, h2p, w1, w3, w2)
    out = _combine(x2, rinfo, y2slot.reshape(2, T, D))
    return out.reshape(B, S, D)


def _tile_schedule(counts, T):
    tr = TR_MOE
    R = 2 * T + N_EXPERTS * tr
    n_tiles = R // tr
    cnt = counts[0, EXPERT_LANE0:].astype(I32)
    padded = ((cnt + tr - 1) // tr) * tr
    pend = jnp.cumsum(padded)
    pstart = pend - padded
    n_used = (pend[-1] // tr).astype(I32).reshape(1)
    tiles = jnp.arange(n_tiles, dtype=I32)
    tile_row = jnp.minimum(tiles, n_used[0] - 1) * tr
    blk_exp = jnp.minimum(jnp.searchsorted(pend, tile_row, side="right"), N_EXPERTS - 1).astype(I32)
    in_use = tiles < n_used[0]
    first = in_use & ((tiles == 0) | (blk_exp != jnp.roll(blk_exp, 1)))
    run_idx = jnp.cumsum(first.astype(I32)) - 1
    used = cnt > 0
    exp_of_run = jnp.argsort(jnp.logical_not(used), stable=True).astype(I32)
    nxt = jnp.where(run_idx + 1 < jnp.sum(used), exp_of_run[jnp.clip(run_idx + 1, 0, N_EXPERTS - 1)], -1)
    valid = jnp.where(in_use, jnp.clip(cnt[blk_exp] - (tiles * tr - pstart[blk_exp]), 0, tr), 0)
    sched = (blk_exp, n_used, first.astype(I32), nxt.astype(I32), (run_idx & 1).astype(I32), valid.astype(I32))
    return R, pstart, sched


def kernel(x, mem, attn_norm, w_in, q_norm, k_norm, pool_proj, pool_scale, mem_norm, w_mem_kv, mq_norm, mk_norm,
           w_out, ffn_norm, w_group, b_group, w_router, b_router, w1, w3, w2):
    for l in range(attn_norm.shape[0]):
        x = _layer(x, mem, attn_norm[l], w_in[l], q_norm[l], k_norm[l], pool_proj[l], pool_scale[l],
                   mem_norm[l], w_mem_kv[l], mq_norm[l], mk_norm[l], w_out[l], ffn_norm[l],
                   w_group[l], b_group[l], w_router[l], b_router[l], w1[l], w3[l], w2[l])
    return x
```

```python
import functools

import jax
import jax.numpy as jnp
from jax import lax
from jax.experimental import pallas as pl
from jax.experimental.pallas import tpu as pltpu

F32 = jnp.float32
BF16 = jnp.bfloat16
I32 = jnp.int32
U32 = jnp.uint32

D_MODEL = 1024
HEAD_DIM = 64
ATTN_WIDTH = 512
POOL_WIDTH = 256
MEM_WIDTH = 256
N_MEM = 256
IN_WIDTH = 3 * ATTN_WIDTH + POOL_WIDTH + MEM_WIDTH
N_GROUPS = 8
EXPERTS_PER_GROUP = 8
N_EXPERTS = 64
EXPERT_HIDDEN = 512
EPS = 1e-6
NEG_INF = -1e30

LANES = 128
N_PAIRS = ATTN_WIDTH // LANES
BLOCK = 128
SUPER = 16 * BLOCK
HALO = 16

ROW_TILE = (D_MODEL // LANES, LANES)

TM_PROJ = 512
TM_MIX = 256
TR_MOE = 256
TC_COMBINE = 256
EXPERT_LANE0 = 64

VMEM_LIMIT = 48 * 1024 * 1024


def _cparams(sem):
    return pltpu.CompilerParams(dimension_semantics=sem, vmem_limit_bytes=VMEM_LIMIT)


def _group_ones(n):
    g = jnp.arange(n) // HEAD_DIM
    return (g[:, None] == g[None, :]).astype(BF16)


def _head_norm(z, ones_ref, gain_ref):
    ss = jnp.dot((z * z).astype(BF16), ones_ref[...], preferred_element_type=F32)
    return z * lax.rsqrt(ss * (1.0 / HEAD_DIM) + EPS) * gain_ref[...]


def _proj_kernel(x_ref, an_ref, win_ref, gq_ref, gk_ref, gm_ref, o512_ref, o256_ref,
                 q1_ref, k1_ref, v1_ref, q4_ref, k4_ref, v4_ref, q16_ref, k16_ref, v16_ref,
                 u_ref, qm_ref, zs_ref):
    tm = x_ref.shape[0]
    x = x_ref[...]
    ms = jnp.mean(x * x, axis=-1, keepdims=True)
    h = (x * lax.rsqrt(ms + EPS) * an_ref[...]).astype(BF16)
    z = jnp.dot(h, win_ref[...], preferred_element_type=F32)
    a0, a1, a2, a3 = ATTN_WIDTH, 2 * ATTN_WIDTH, 3 * ATTN_WIDTH, 3 * ATTN_WIDTH + POOL_WIDTH
    qn = _head_norm(z[:, :a0], o512_ref, gq_ref)
    kn = _head_norm(z[:, a0:a1], o512_ref, gk_ref)
    v = z[:, a1:a2]
    u_ref[...] = z[:, a2:a3]
    qm_ref[...] = _head_norm(z[:, a3:], o256_ref, gm_ref).astype(BF16)
    for val, o1, o4, o16 in ((qn, q1_ref, q4_ref, q16_ref), (kn, k1_ref, k4_ref, k16_ref),
                             (v, v1_ref, v4_ref, v16_ref)):
        for hp in range(N_PAIRS):
            pair = val[:, hp * LANES:(hp + 1) * LANES]
            zs_ref[hp] = pair
            o1[hp] = pair.astype(BF16)
        for d, o in ((4, o4), (16, o16)):
            for r in range(d):
                for hp in range(N_PAIRS):
                    rows = zs_ref[hp, pl.ds(r, tm // d, stride=d), :]
                    o[hp, :, r * LANES:(r + 1) * LANES] = rows.astype(BF16)


def _proj(x2d, attn_norm, w_in, gq, gk, gm, B, S):
    T = B * S
    tm = TM_PROJ
    nj = S // tm
    const = lambda i: (0, 0)

    def lay(d):
        return jax.ShapeDtypeStruct((B, N_PAIRS, S // d, d * LANES), BF16)

    def lay_spec(d):
        return pl.BlockSpec((None, N_PAIRS, tm // d, d * LANES), lambda i: (i // nj, 0, i % nj, 0))

    out_shape = [lay(1)] * 3 + [lay(4)] * 3 + [lay(16)] * 3 + [
        jax.ShapeDtypeStruct((T, POOL_WIDTH), F32), jax.ShapeDtypeStruct((T, MEM_WIDTH), BF16)]
    out_specs = [lay_spec(1)] * 3 + [lay_spec(4)] * 3 + [lay_spec(16)] * 3 + [
        pl.BlockSpec((tm, POOL_WIDTH), lambda i: (i, 0)), pl.BlockSpec((tm, MEM_WIDTH), lambda i: (i, 0))]
    return pl.pallas_call(
        _proj_kernel,
        grid=(T // tm,),
        in_specs=[pl.BlockSpec((tm, D_MODEL), lambda i: (i, 0)),
                  pl.BlockSpec((1, D_MODEL), const),
                  pl.BlockSpec((D_MODEL, IN_WIDTH), const),
                  pl.BlockSpec((1, ATTN_WIDTH), const),
                  pl.BlockSpec((1, ATTN_WIDTH), const),
                  pl.BlockSpec((1, MEM_WIDTH), const),
                  pl.BlockSpec((ATTN_WIDTH, ATTN_WIDTH), const),
                  pl.BlockSpec((MEM_WIDTH, MEM_WIDTH), const)],
        out_specs=out_specs,
        out_shape=out_shape,
        scratch_shapes=[pltpu.VMEM((N_PAIRS, tm, LANES), F32)],
        compiler_params=_cparams(("parallel",)),
        name="proj",
    )(x2d, attn_norm, w_in, gq, gk, gm, _group_ones(ATTN_WIDTH), _group_ones(MEM_WIDTH))


def _attn_kernel(q1, k1c, k1p, v1c, v1p, q4, k4c, k4p, v4c, v4p, q16, k16c, k16p, v16c, v16p,
                 bias_ref, o_ref, acc, mrun, lrun):
    c = pl.program_id(2)
    lane = lax.broadcasted_iota(I32, (BLOCK, LANES), 1)
    is_a = lane < HEAD_DIM
    lane_row = lax.broadcasted_iota(I32, (1, LANES), 1)
    mask_a = jnp.where(lane_row < HEAD_DIM, 1.0, 0.0).astype(BF16)
    mask_b = jnp.where(lane_row < HEAD_DIM, 0.0, 1.0).astype(BF16)
    bias_full = bias_ref[0]
    bias_first = jnp.where(c > 0, bias_full, bias_ref[1])

    def tile(q_t, kp_t, kc_t, vp_t, vc_t, bias):
        lhs = jnp.concatenate([q_t * mask_a, q_t * mask_b], axis=0)
        keys = jnp.concatenate([kp_t, kc_t], axis=0)
        s = lax.dot_general(lhs, keys, (((1,), (1,)), ((), ())), preferred_element_type=F32) + bias
        m = jnp.max(s, axis=-1, keepdims=True)
        p = jnp.exp(s - m)
        l = jnp.sum(p, axis=-1, keepdims=True)
        vals = jnp.concatenate([vp_t, vc_t], axis=0)
        pv = jnp.dot(p.astype(BF16), vals, preferred_element_type=F32)
        o_t = jnp.where(is_a, pv[:BLOCK], pv[BLOCK:])
        m_t = jnp.where(is_a, m[:BLOCK], m[BLOCK:])
        l_t = jnp.where(is_a, l[:BLOCK], l[BLOCK:])
        return o_t, m_t, l_t

    def merge(rows, o_t, m_t, l_t):
        m_o = mrun[rows, :]
        m_n = jnp.maximum(m_o, m_t)
        a = jnp.exp(m_o - m_n)
        b = jnp.exp(m_t - m_n)
        acc[rows, :] = acc[rows, :] * a + o_t * b
        lrun[rows, :] = lrun[rows, :] * a + l_t * b
        mrun[rows, :] = m_n

    for jb in range(SUPER // BLOCK):
        cur = pl.ds(jb * BLOCK, BLOCK)
        if jb == 0:
            kp_t, vp_t, bias = k1p[...], v1p[...], bias_first
        else:
            prev = pl.ds((jb - 1) * BLOCK, BLOCK)
            kp_t, vp_t, bias = k1c[prev, :], v1c[prev, :], bias_full
        o_t, m_t, l_t = tile(q1[cur, :], kp_t, k1c[cur, :], vp_t, v1c[cur, :], bias)
        acc[cur, :] = o_t
        mrun[cur, :] = m_t
        lrun[cur, :] = l_t
    for d, q, kc, kp, vc, vp in ((4, q4, k4c, k4p, v4c, v4p), (16, q16, k16c, k16p, v16c, v16p)):
        nblk = SUPER // (BLOCK * d)
        for r in range(d):
            cols = pl.ds(r * LANES, LANES)
            for jb in range(nblk):
                cur = pl.ds(jb * BLOCK, BLOCK)
                if jb == 0:
                    kp_t, vp_t, bias = kp[:, cols], vp[:, cols], bias_first
                else:
                    prev = pl.ds((jb - 1) * BLOCK, BLOCK)
                    kp_t, vp_t, bias = kc[prev, cols], vc[prev, cols], bias_full
                o_t, m_t, l_t = tile(q[cur, cols], kp_t, kc[cur, cols], vp_t, vc[cur, cols], bias)
                merge(pl.ds(jb * BLOCK * d + r, BLOCK, stride=d), o_t, m_t, l_t)
    o_ref[...] = (acc[...] / lrun[...]).astype(BF16)


def _band_bias():
    qi = jnp.arange(BLOCK)[:, None]
    kj = jnp.arange(2 * BLOCK)[None, :]
    dist = qi + BLOCK - kj
    in_band = (dist >= 0) & (dist <= BLOCK)
    full = jnp.where(in_band, 0.0, NEG_INF).astype(F32)
    first = jnp.where(in_band & (kj >= BLOCK), 0.0, NEG_INF).astype(F32)
    return jnp.stack([jnp.tile(full, (2, 1)), jnp.tile(first, (2, 1))])


def _attn(q1, k1, v1, q4, k4, v4, q16, k16, v16, B, S):
    nsup = S // SUPER

    def specs(d):
        rows = SUPER // d
        per = rows // BLOCK
        cur = pl.BlockSpec((None, None, rows, d * LANES), lambda b, hp, c: (b, hp, c, 0))
        prev = pl.BlockSpec((None, None, BLOCK, d * LANES),
                            lambda b, hp, c: (b, hp, jnp.maximum(per * c - 1, 0), 0))
        return [cur, cur, prev, cur, prev]

    return pl.pallas_call(
        _attn_kernel,
        grid=(B, N_PAIRS, nsup),
        in_specs=specs(1) + specs(4) + specs(16) + [
            pl.BlockSpec((2, 2 * BLOCK, 2 * BLOCK), lambda b, hp, c: (0, 0, 0))],
        out_specs=pl.BlockSpec((None, SUPER, LANES), lambda b, hp, c: (b, c, hp)),
        out_shape=jax.ShapeDtypeStruct((B, S, ATTN_WIDTH), BF16),
        scratch_shapes=[pltpu.VMEM((SUPER, LANES), F32)] * 3,
        compiler_params=_cparams(("parallel", "parallel", "parallel")),
        name="dilated_attn",
    )(q1, k1, k1, v1, v1, q4, k4, k4, v4, v4, q16, k16, k16, v16, v16, _band_bias())


def _memkv_kernel(mem_ref, mn_ref, wkv_ref, gk_ref, o256_ref, km_ref, vm_ref):
    m = mem_ref[...]
    ms = jnp.mean(m * m, axis=-1, keepdims=True)
    mn = (m * lax.rsqrt(ms + EPS) * mn_ref[...]).astype(BF16)
    kv = jnp.dot(mn, wkv_ref[...], preferred_element_type=F32)
    km_ref[...] = _head_norm(kv[:, :MEM_WIDTH], o256_ref, gk_ref).astype(BF16)
    vm_ref[...] = kv[:, MEM_WIDTH:].astype(BF16)


def _memkv(mem, mem_norm, w_mem_kv, gmk, B):
    const = lambda b: (0, 0)
    return pl.pallas_call(
        _memkv_kernel,
        grid=(B,),
        in_specs=[pl.BlockSpec((None, N_MEM, D_MODEL), lambda b: (b, 0, 0)),
                  pl.BlockSpec((1, D_MODEL), const),
                  pl.BlockSpec((D_MODEL, 2 * MEM_WIDTH), const),
                  pl.BlockSpec((1, MEM_WIDTH), const),
                  pl.BlockSpec((MEM_WIDTH, MEM_WIDTH), const)],
        out_specs=[pl.BlockSpec((None, N_MEM, MEM_WIDTH), lambda b: (b, 0, 0))] * 2,
        out_shape=[jax.ShapeDtypeStruct((B, N_MEM, MEM_WIDTH), BF16)] * 2,
        compiler_params=_cparams(("parallel",)),
        name="memkv",
    )(mem, mem_norm, w_mem_kv, gmk, _group_ones(MEM_WIDTH))


def _mix_kernel(tiles_per_seq, ya_ref, u_ref, uh_ref, qm_ref, km_ref, vm_ref, x_ref, pp_ref, ps_ref,
                wo_ref, fn_ref, wr_ref, br_ref, ltri_ref,
                x2_ref, h2t_ref, rinfo_ref, rinfot_ref, cnt_ref, base_ref):
    i = pl.program_id(0)
    tm = x_ref.shape[0]
    seq_tile = i % tiles_per_seq

    @pl.when(i == 0)
    def _():
        base_ref[...] = jnp.zeros_like(base_ref)

    u = u_ref[...]
    halo = jnp.where(seq_tile == 0, 0.0, uh_ref[...])
    uu = jnp.concatenate([halo, u], axis=0)
    a1 = uu[1:] + uu[:-1]
    a2 = a1[2:] + a1[:-2]
    a3 = a2[4:] + a2[:-4]
    a4 = a3[8:] + a3[:-8]
    lane_p = lax.broadcasted_iota(I32, (tm, POOL_WIDTH), 1)
    g0, g1, g2 = lane_p < 64, lane_p < 128, lane_p < 192
    wsum = jnp.where(g0, a1[15:], jnp.where(g1, a2[13:], jnp.where(g2, a3[9:], a4[1:])))
    wlen = jnp.where(g0, 2.0, jnp.where(g1, 4.0, jnp.where(g2, 8.0, 16.0)))
    tpos = seq_tile * tm + lax.broadcasted_iota(I32, (tm, POOL_WIDTH), 0) + 1
    cnt = jnp.minimum(tpos.astype(F32), wlen)
    pooled = wsum / cnt - u
    y_pool = jnp.dot(pooled.astype(BF16), pp_ref[...], preferred_element_type=F32) * ps_ref[...]

    lane = lax.broadcasted_iota(I32, (tm, LANES), 1)
    is_a = lane < HEAD_DIM
    lane_row = lax.broadcasted_iota(I32, (1, LANES), 1)
    mask_a = jnp.where(lane_row < HEAD_DIM, 1.0, 0.0).astype(BF16)
    mask_b = jnp.where(lane_row < HEAD_DIM, 0.0, 1.0).astype(BF16)
    y_mem = []
    for pr in range(MEM_WIDTH // LANES):
        cols = slice(pr * LANES, (pr + 1) * LANES)
        qp, kp, vp = qm_ref[:, cols], km_ref[:, cols], vm_ref[:, cols]
        outs = []
        for msk in (mask_a, mask_b):
            s = lax.dot_general(qp * msk, kp, (((1,), (1,)), ((), ())), preferred_element_type=F32)
            m = jnp.max(s, axis=-1, keepdims=True)
            p = jnp.exp(s - m)
            p = p / jnp.sum(p, axis=-1, keepdims=True)
            outs.append(jnp.dot(p.astype(BF16), vp, preferred_element_type=F32))
        y_mem.append(jnp.where(is_a, outs[0], outs[1]))

    a0, a1w = ATTN_WIDTH, ATTN_WIDTH + POOL_WIDTH
    proj = jnp.dot(ya_ref[...], wo_ref[:a0, :], preferred_element_type=F32)
    proj += jnp.dot(y_pool.astype(BF16), wo_ref[a0:a1w, :], preferred_element_type=F32)
    for pr in range(MEM_WIDTH // LANES):
        lo = a1w + pr * LANES
        proj += jnp.dot(y_mem[pr].astype(BF16), wo_ref[lo:lo + LANES, :], preferred_element_type=F32)
    x2 = x_ref[...] + proj
    x2_ref[...] = x2

    ms = jnp.mean(x2 * x2, axis=-1, keepdims=True)
    h2f = x2 * lax.rsqrt(ms + EPS) * fn_ref[...]
    for a in range(D_MODEL // LANES):
        h2t_ref[:, a, :] = h2f[:, a * LANES:(a + 1) * LANES]
    h2 = h2f.astype(BF16)

    logits = jnp.dot(h2, wr_ref[...], preferred_element_type=F32) + br_ref[...]
    ninf = -jnp.inf
    is_g = lane < N_GROUPS
    lg = jnp.where(is_g, logits, ninf)
    mg = jnp.max(lg, axis=-1, keepdims=True)
    g_sel = jnp.min(jnp.where(lg == mg, lane, LANES), axis=-1, keepdims=True)
    w_g = 1.0 / jnp.sum(jnp.where(is_g, jnp.exp(logits - mg), 0.0), axis=-1, keepdims=True)
    in_grp = (lane >= EXPERT_LANE0) & (((lane - EXPERT_LANE0) >> 3) == g_sel)
    le = jnp.where(in_grp, logits, ninf)
    v1 = jnp.max(le, axis=-1, keepdims=True)
    i1 = jnp.min(jnp.where(le == v1, lane, 2 * LANES), axis=-1, keepdims=True)
    le2 = jnp.where(lane == i1, ninf, le)
    v2 = jnp.max(le2, axis=-1, keepdims=True)
    i2 = jnp.min(jnp.where(le2 == v2, lane, 2 * LANES), axis=-1, keepdims=True)
    e21 = jnp.exp(v2 - v1)
    gate1 = w_g / (1.0 + e21)
    gate2 = w_g * e21 / (1.0 + e21)

    hot1 = lane == i1
    hot2 = lane == i2
    onehot = jnp.where(hot1 | hot2, 1.0, 0.0)
    prefix = jnp.dot(ltri_ref[...], onehot.astype(BF16), preferred_element_type=F32)
    tot = prefix + base_ref[...]
    rank1 = jnp.sum(jnp.where(hot1, tot, 0.0), axis=-1, keepdims=True)
    rank2 = jnp.sum(jnp.where(hot2, tot, 0.0), axis=-1, keepdims=True)
    base_new = base_ref[...] + jnp.sum(onehot, axis=0, keepdims=True)
    base_ref[...] = base_new
    cnt_ref[...] = base_new

    e1 = (i1 - EXPERT_LANE0).astype(F32)
    e2 = (i2 - EXPERT_LANE0).astype(F32)
    rinfo = jnp.where(lane == 0, e1, jnp.where(lane == 1, e2, jnp.where(lane == 2, rank1, jnp.where(
        lane == 3, rank2, jnp.where(lane == 4, gate1, jnp.where(lane == 5, gate2, 0.0))))))
    rinfo_ref[...] = rinfo
    rinfot_ref[...] = jnp.transpose(rinfo)[:8, :]


def _mix(ya, u, qm, km, vm, x2d, pool_bd, pool_scale, w_out, ffn_norm, w_r, b_r, B, S):
    T = B * S
    tm = TM_MIX
    tps = S // tm
    hb = tm // HALO
    const = lambda i: (0, 0)
    ltri = (jnp.arange(tm)[:, None] > jnp.arange(tm)[None, :]).astype(BF16)
    return pl.pallas_call(
        functools.partial(_mix_kernel, tps),
        grid=(T // tm,),
        in_specs=[pl.BlockSpec((tm, ATTN_WIDTH), lambda i: (i, 0)),
                  pl.BlockSpec((tm, POOL_WIDTH), lambda i: (i, 0)),
                  pl.BlockSpec((HALO, POOL_WIDTH), lambda i: (jnp.maximum(i * hb - 1, 0), 0)),
                  pl.BlockSpec((tm, MEM_WIDTH), lambda i: (i, 0)),
                  pl.BlockSpec((None, N_MEM, MEM_WIDTH), lambda i: (i // tps, 0, 0)),
                  pl.BlockSpec((None, N_MEM, MEM_WIDTH), lambda i: (i // tps, 0, 0)),
                  pl.BlockSpec((tm, D_MODEL), lambda i: (i, 0)),
                  pl.BlockSpec((POOL_WIDTH, POOL_WIDTH), const),
                  pl.BlockSpec((1, POOL_WIDTH), const),
                  pl.BlockSpec((D_MODEL, D_MODEL), const),
                  pl.BlockSpec((1, D_MODEL), const),
                  pl.BlockSpec((D_MODEL, LANES), const),
                  pl.BlockSpec((1, LANES), const),
                  pl.BlockSpec((tm, tm), const)],
        out_specs=[pl.BlockSpec((tm, D_MODEL), lambda i: (i, 0)),
                   pl.BlockSpec((tm, ROW_TILE[0], ROW_TILE[1]), lambda i: (i, 0, 0)),
                   pl.BlockSpec((tm, LANES), lambda i: (i, 0)),
                   pl.BlockSpec((8, tm), lambda i: (0, i)),
                   pl.BlockSpec((1, LANES), const)],
        out_shape=[jax.ShapeDtypeStruct((T, D_MODEL), F32),
                   jax.ShapeDtypeStruct((T,) + ROW_TILE, F32),
                   jax.ShapeDtypeStruct((T, LANES), F32),
                   jax.ShapeDtypeStruct((8, T), F32),
                   jax.ShapeDtypeStruct((1, LANES), F32)],
        scratch_shapes=[pltpu.VMEM((1, LANES), F32)],
        compiler_params=_cparams(("arbitrary",)),
        name="mix_router",
    )(ya, u, u, qm, km, vm, x2d, pool_bd, pool_scale, w_out, ffn_norm, w_r, b_r, ltri)


def _dest_kernel(rt_ref, ps_ref, d_ref):
    tn = rt_ref.shape[1]
    sub = lax.broadcasted_iota(I32, (N_EXPERTS, tn), 0).astype(F32)
    ps = ps_ref[...]
    rows = []
    for k in range(2):
        e = rt_ref[k:k + 1, :]
        start = jnp.sum(jnp.where(sub == e, ps, 0.0), axis=0, keepdims=True)
        rows.append(start + rt_ref[2 + k:3 + k, :])
    d = jnp.concatenate(rows + [jnp.zeros((6, tn), F32)], axis=0)
    d_ref[...] = d.astype(I32)


def _dest(rinfot, pstart_col, T):
    tn = 2048
    return pl.pallas_call(
        _dest_kernel,
        grid=(T // tn,),
        in_specs=[pl.BlockSpec((8, tn), lambda i: (0, i)),
                  pl.BlockSpec((N_EXPERTS, 1), lambda i: (0, 0))],
        out_specs=pl.BlockSpec((8, tn), lambda i: (0, i)),
        out_shape=jax.ShapeDtypeStruct((8, T), I32),
        compiler_params=_cparams(("parallel",)),
        name="dest_rows",
    )(rinfot, pstart_col)


def _rowmap_kernel(d1_ref, d2_ref, fill_hbm, inv_ref, sem):
    T = d1_ref.shape[0]
    cp = pltpu.make_async_copy(fill_hbm, inv_ref, sem)
    cp.start()
    cp.wait()

    def body(t, carry):
        inv_ref[d1_ref[t]] = t
        inv_ref[d2_ref[t]] = T + t
        return carry

    lax.fori_loop(0, T, body, 0, unroll=8)


def _rowmap(d1, d2, R):
    return pl.pallas_call(
        _rowmap_kernel,
        grid_spec=pltpu.PrefetchScalarGridSpec(
            num_scalar_prefetch=2,
            grid=(1,),
            in_specs=[pl.BlockSpec(memory_space=pl.ANY)],
            out_specs=pl.BlockSpec(memory_space=pltpu.SMEM),
            scratch_shapes=[pltpu.SemaphoreType.DMA(())]),
        out_shape=jax.ShapeDtypeStruct((R,), I32),
        compiler_params=_cparams(("arbitrary",)),
        name="rowmap",
    )(d1, d2, jnp.zeros((R,), I32))


def _row_loop(n, body):
    def group(g, carry):
        for u in range(8):
            body(g * 8 + u)
        return carry

    def rest(i, carry):
        body(((n >> 3) << 3) + i)
        return carry

    lax.fori_loop(0, n >> 3, group, 0)
    lax.fori_loop(0, n & 7, rest, 0)


def _moe_kernel(be_ref, nu_ref, first_ref, nxt_ref, ws_ref, valid_ref, inv_ref,
                h_hbm, w1_hbm, w3_hbm, w2_hbm, o_hbm,
                xbuf, ybuf, w1s, w3s, w2s, w1b, w3b, w2b, gsem, ssem, wsem):
    j = pl.program_id(0)
    nu = nu_ref[0]
    T = h_hbm.shape[0]
    tr = TR_MOE

    def gather_row(tile, slot, i):
        v = inv_ref[tile * tr + i]
        tok = jnp.where(v >= T, v - T, v)
        return pltpu.make_async_copy(h_hbm.at[pl.ds(tok, 1)], xbuf.at[slot, pl.ds(i, 1)], gsem.at[slot])

    def gather_wait(slot, n):
        def done(rows):
            return pltpu.make_async_copy(h_hbm.at[pl.ds(0, rows)], xbuf.at[slot, pl.ds(0, rows)], gsem.at[slot])
        lax.fori_loop(0, n >> 3, lambda g, c: (done(8).wait(), c)[1], 0)
        lax.fori_loop(0, n & 7, lambda g, c: (done(1).wait(), c)[1], 0)

    def scatter_row(tile, slot, i):
        v = inv_ref[tile * tr + i]
        return pltpu.make_async_copy(ybuf.at[slot, pl.ds(i, 1)], o_hbm.at[pl.ds(v, 1)], ssem.at[slot])

    def scatter_wait(slot, n):
        def done(rows):
            return pltpu.make_async_copy(ybuf.at[slot, pl.ds(0, rows)], o_hbm.at[pl.ds(0, rows)], ssem.at[slot])
        lax.fori_loop(0, n >> 3, lambda g, c: (done(8).wait(), c)[1], 0)
        lax.fori_loop(0, n & 7, lambda g, c: (done(1).wait(), c)[1], 0)

    def weight_copies(e, slot):
        return [pltpu.make_async_copy(src.at[e], dst.at[slot], wsem.at[slot])
                for src, dst in ((w1_hbm, w1s), (w3_hbm, w3s), (w2_hbm, w2s))]

    @pl.when(j < nu)
    def _():
        s = j & 1
        e = be_ref[j]

        @pl.when(j == 0)
        def _():
            xbuf[...] = jnp.zeros_like(xbuf)
            _row_loop(valid_ref[0], lambda i: gather_row(0, 0, i).start())
            for cp in weight_copies(e, 0):
                cp.start()

        gather_wait(s, valid_ref[j])

        @pl.when(j + 1 < nu)
        def _():
            _row_loop(valid_ref[j + 1], lambda i: gather_row(j + 1, 1 - s, i).start())

        @pl.when(first_ref[j] == 1)
        def _():
            wslot = ws_ref[j]
            for cp in weight_copies(e, wslot):
                cp.wait()
            e_next = nxt_ref[j]

            @pl.when(e_next >= 0)
            def _():
                for cp in weight_copies(e_next, 1 - wslot):
                    cp.start()

            w1b[...] = w1s[wslot].astype(BF16)
            w3b[...] = w3s[wslot].astype(BF16)
            w2b[...] = w2s[wslot].astype(BF16)

        @pl.when(j >= 2)
        def _():
            scatter_wait(s, valid_ref[j - 2])

        n_chunks = ROW_TILE[0]
        x = jnp.concatenate([xbuf[s, :, c, :] for c in range(n_chunks)], axis=1).astype(BF16)
        a = jnp.dot(x, w1b[...], preferred_element_type=F32)
        b = jnp.dot(x, w3b[...], preferred_element_type=F32)
        hmid = (a / (1.0 + jnp.exp(-a)) * b).astype(BF16)
        y = jnp.dot(hmid, w2b[...], preferred_element_type=F32)
        for c in range(n_chunks):
            ybuf[s, :, c, :] = y[:, c * LANES:(c + 1) * LANES]

        _row_loop(valid_ref[j], lambda i: scatter_row(j, s, i).start())

        @pl.when(j == nu - 1)
        def _():
            scatter_wait(s, valid_ref[j])

            @pl.when(j >= 1)
            def _():
                scatter_wait(1 - s, valid_ref[j - 1])


def _moe(blk_exp, n_used, first, nxt, wslot, valid, inv, h2t, w1, w3, w2):
    T = h2t.shape[0]
    tr = TR_MOE
    n_tiles = blk_exp.shape[0]
    any_spec = pl.BlockSpec(memory_space=pl.ANY)
    return pl.pallas_call(
        _moe_kernel,
        grid_spec=pltpu.PrefetchScalarGridSpec(
            num_scalar_prefetch=7,
            grid=(n_tiles,),
            in_specs=[any_spec] * 4,
            out_specs=any_spec,
            scratch_shapes=[pltpu.VMEM((2, tr) + ROW_TILE, F32),
                            pltpu.VMEM((2, tr) + ROW_TILE, F32),
                            pltpu.VMEM((2, D_MODEL, EXPERT_HIDDEN), F32),
                            pltpu.VMEM((2, D_MODEL, EXPERT_HIDDEN), F32),
                            pltpu.VMEM((2, EXPERT_HIDDEN, D_MODEL), F32),
                            pltpu.VMEM((D_MODEL, EXPERT_HIDDEN), BF16),
                            pltpu.VMEM((D_MODEL, EXPERT_HIDDEN), BF16),
                            pltpu.VMEM((EXPERT_HIDDEN, D_MODEL), BF16),
                            pltpu.SemaphoreType.DMA((2,)),
                            pltpu.SemaphoreType.DMA((2,)),
                            pltpu.SemaphoreType.DMA((2,))]),
        out_shape=jax.ShapeDtypeStruct((2 * T,) + ROW_TILE, F32),
        compiler_params=_cparams(("arbitrary",)),
        name="moe_experts",
    )(blk_exp, n_used, first, nxt, wslot, valid, inv, h2t, w1, w3, w2)


def _combine_kernel(x2_ref, rinfo_ref, y1_ref, y2_ref, o_ref):
    g1 = rinfo_ref[:, 4:5]
    g2 = rinfo_ref[:, 5:6]
    for c in range(ROW_TILE[0]):
        cols = slice(c * LANES, (c + 1) * LANES)
        o_ref[:, cols] = x2_ref[:, cols] + (g1 * y1_ref[:, c, :] + g2 * y2_ref[:, c, :])


def _combine(x2, rinfo, y2slot):
    T = x2.shape[0]
    tc = TC_COMBINE
    return pl.pallas_call(
        _combine_kernel,
        grid=(T // tc,),
        in_specs=[pl.BlockSpec((tc, D_MODEL), lambda i: (i, 0)),
                  pl.BlockSpec((tc, LANES), lambda i: (i, 0)),
                  pl.BlockSpec((None, tc) + ROW_TILE, lambda i: (0, i, 0, 0)),
                  pl.BlockSpec((None, tc) + ROW_TILE, lambda i: (1, i, 0, 0))],
        out_specs=pl.BlockSpec((tc, D_MODEL), lambda i: (i, 0)),
        out_shape=jax.ShapeDtypeStruct((T, D_MODEL), F32),
        compiler_params=_cparams(("parallel",)),
        name="combine",
    )(x2, rinfo, y2slot, y2slot)


def _layer(x, mem, attn_norm, w_in, q_norm, k_norm, pool_proj, pool_scale, mem_norm, w_mem_kv,
           mq_norm, mk_norm, w_out, ffn_norm, w_group, b_group, w_router, b_router, w1, w3, w2):
    B, S, D = x.shape
    T = B * S
    assert D == D_MODEL and S % SUPER == 0 and T % TM_PROJ == 0
    x2d = x.reshape(T, D)
    row = lambda v: v.reshape(1, -1).astype(F32)
    scale = HEAD_DIM ** -0.5
    gq = row(jnp.tile(q_norm, ATTN_WIDTH // HEAD_DIM) * scale)
    gk = row(jnp.tile(k_norm, ATTN_WIDTH // HEAD_DIM))
    gmq = row(jnp.tile(mq_norm, MEM_WIDTH // HEAD_DIM) * scale)
    gmk = row(jnp.tile(mk_norm, MEM_WIDTH // HEAD_DIM))

    (q1, k1, v1, q4, k4, v4, q16, k16, v16, u, qm) = _proj(
        x2d, row(attn_norm), w_in.astype(BF16), gq, gk, gmq, B, S)
    ya = _attn(q1, k1, v1, q4, k4, v4, q16, k16, v16, B, S).reshape(T, ATTN_WIDTH)
    km, vm = _memkv(mem, row(mem_norm), w_mem_kv.astype(BF16), gmk, B)

    pool_bd = jax.scipy.linalg.block_diag(*[pool_proj[g] for g in range(pool_proj.shape[0])]).astype(BF16)
    w_r = jnp.zeros((D, LANES), F32)
    w_r = w_r.at[:, :N_GROUPS].set(w_group)
    w_r = w_r.at[:, EXPERT_LANE0:].set(jnp.transpose(w_router, (1, 0, 2)).reshape(D, N_EXPERTS))
    b_r = jnp.zeros((1, LANES), F32).at[0, :N_GROUPS].set(b_group).at[0, EXPERT_LANE0:].set(b_router.reshape(-1))
    x2, h2t, rinfo, rinfot, counts = _mix(ya, u, qm, km, vm, x2d, pool_bd, row(pool_scale),
                                          w_out.astype(BF16), row(ffn_norm), w_r.astype(BF16), b_r, B, S)

    R, pstart, sched = _tile_schedule(counts, T)
    dest = _dest(rinfot, pstart.astype(F32).reshape(N_EXPERTS, 1), T)
    inv = _rowmap(dest[0], dest[1], R)
    y2slot = _moe(*sched, inv, h2t, w1, w3, w2)
    out = _combine(x2, rinfo, y2slot.reshape((2, T) + ROW_TILE))
    return out.reshape(B, S, D)


def _tile_schedule(counts, T):
    tr = TR_MOE
    R = 2 * T + N_EXPERTS * tr
    n_tiles = R // tr
    cnt = counts[0, EXPERT_LANE0:].astype(I32)
    padded = ((cnt + tr - 1) // tr) * tr
    pend = jnp.cumsum(padded)
    pstart = pend - padded
    n_used = (pend[-1] // tr).astype(I32).reshape(1)
    tiles = jnp.arange(n_tiles, dtype=I32)
    tile_row = jnp.minimum(tiles, n_used[0] - 1) * tr
    blk_exp = jnp.minimum(jnp.searchsorted(pend, tile_row, side="right"), N_EXPERTS - 1).astype(I32)
    in_use = tiles < n_used[0]
    first = in_use & ((tiles == 0) | (blk_exp != jnp.roll(blk_exp, 1)))
    run_idx = jnp.cumsum(first.astype(I32)) - 1
    used = cnt > 0
    exp_of_run = jnp.argsort(jnp.logical_not(used), stable=True).astype(I32)
    nxt = jnp.where(run_idx + 1 < jnp.sum(used), exp_of_run[jnp.clip(run_idx + 1, 0, N_EXPERTS - 1)], -1)
    valid = jnp.where(in_use, jnp.clip(cnt[blk_exp] - (tiles * tr - pstart[blk_exp]), 0, tr), 0)
    sched = (blk_exp, n_used, first.astype(I32), nxt.astype(I32), (run_idx & 1).astype(I32), valid.astype(I32))
    return R, pstart, sched


def kernel(x, mem, attn_norm, w_in, q_norm, k_norm, pool_proj, pool_scale, mem_norm, w_mem_kv, mq_norm, mk_norm,
           w_out, ffn_norm, w_group, b_group, w_router, b_router, w1, w3, w2):
    for l in range(attn_norm.shape[0]):
        x = _layer(x, mem, attn_norm[l], w_in[l], q_norm[l], k_norm[l], pool_proj[l], pool_scale[l],
                   mem_norm[l], w_mem_kv[l], mq_norm[l], mk_norm[l], w_out[l], ffn_norm[l],
                   w_group[l], b_group[l], w_router[l], b_router[l], w1[l], w3[l], w2[l])
    return x
```

```python
import functools

import jax
import jax.numpy as jnp
from jax import lax
from jax.experimental import pallas as pl
from jax.experimental.pallas import tpu as pltpu

F32 = jnp.float32
BF16 = jnp.bfloat16
I32 = jnp.int32
U32 = jnp.uint32

D_MODEL = 1024
HEAD_DIM = 64
ATTN_WIDTH = 512
POOL_WIDTH = 256
MEM_WIDTH = 256
N_MEM = 256
IN_WIDTH = 3 * ATTN_WIDTH + POOL_WIDTH + MEM_WIDTH
N_GROUPS = 8
EXPERTS_PER_GROUP = 8
N_EXPERTS = 64
EXPERT_HIDDEN = 512
EPS = 1e-6
NEG_INF = -1e30

LANES = 128
N_PAIRS = ATTN_WIDTH // LANES
BLOCK = 128
SUPER = 16 * BLOCK
HALO = 16

ROW_SUB = D_MODEL // LANES

TM_PROJ = 512
TM_MIX = 256
TR_MOE = 256
TC_COMBINE = 256
EXPERT_LANE0 = 64

VMEM_LIMIT = 48 * 1024 * 1024


def _cparams(sem):
    return pltpu.CompilerParams(dimension_semantics=sem, vmem_limit_bytes=VMEM_LIMIT)


def _group_ones(n):
    g = jnp.arange(n) // HEAD_DIM
    return (g[:, None] == g[None, :]).astype(BF16)


def _head_norm(z, ones_ref, gain_ref):
    ss = jnp.dot((z * z).astype(BF16), ones_ref[...], preferred_element_type=F32)
    return z * lax.rsqrt(ss * (1.0 / HEAD_DIM) + EPS) * gain_ref[...]


def _proj_kernel(x_ref, an_ref, win_ref, gq_ref, gk_ref, gm_ref, o512_ref, o256_ref,
                 q1_ref, k1_ref, v1_ref, q4_ref, k4_ref, v4_ref, q16_ref, k16_ref, v16_ref,
                 u_ref, qm_ref, zs_ref):
    tm = x_ref.shape[0]
    x = x_ref[...]
    ms = jnp.mean(x * x, axis=-1, keepdims=True)
    h = (x * lax.rsqrt(ms + EPS) * an_ref[...]).astype(BF16)
    z = jnp.dot(h, win_ref[...], preferred_element_type=F32)
    a0, a1, a2, a3 = ATTN_WIDTH, 2 * ATTN_WIDTH, 3 * ATTN_WIDTH, 3 * ATTN_WIDTH + POOL_WIDTH
    qn = _head_norm(z[:, :a0], o512_ref, gq_ref)
    kn = _head_norm(z[:, a0:a1], o512_ref, gk_ref)
    v = z[:, a1:a2]
    u_ref[...] = z[:, a2:a3]
    qm_ref[...] = _head_norm(z[:, a3:], o256_ref, gm_ref).astype(BF16)
    for val, o1, o4, o16 in ((qn, q1_ref, q4_ref, q16_ref), (kn, k1_ref, k4_ref, k16_ref),
                             (v, v1_ref, v4_ref, v16_ref)):
        for hp in range(N_PAIRS):
            pair = val[:, hp * LANES:(hp + 1) * LANES]
            zs_ref[hp] = pair
            o1[hp] = pair.astype(BF16)
        for d, o in ((4, o4), (16, o16)):
            for r in range(d):
                for hp in range(N_PAIRS):
                    rows = zs_ref[hp, pl.ds(r, tm // d, stride=d), :]
                    o[hp, :, r * LANES:(r + 1) * LANES] = rows.astype(BF16)


def _proj(x2d, attn_norm, w_in, gq, gk, gm, B, S):
    T = B * S
    tm = TM_PROJ
    nj = S // tm
    const = lambda i: (0, 0)

    def lay(d):
        return jax.ShapeDtypeStruct((B, N_PAIRS, S // d, d * LANES), BF16)

    def lay_spec(d):
        return pl.BlockSpec((None, N_PAIRS, tm // d, d * LANES), lambda i: (i // nj, 0, i % nj, 0))

    out_shape = [lay(1)] * 3 + [lay(4)] * 3 + [lay(16)] * 3 + [
        jax.ShapeDtypeStruct((T, POOL_WIDTH), F32), jax.ShapeDtypeStruct((T, MEM_WIDTH), BF16)]
    out_specs = [lay_spec(1)] * 3 + [lay_spec(4)] * 3 + [lay_spec(16)] * 3 + [
        pl.BlockSpec((tm, POOL_WIDTH), lambda i: (i, 0)), pl.BlockSpec((tm, MEM_WIDTH), lambda i: (i, 0))]
    return pl.pallas_call(
        _proj_kernel,
        grid=(T // tm,),
        in_specs=[pl.BlockSpec((tm, D_MODEL), lambda i: (i, 0)),
                  pl.BlockSpec((1, D_MODEL), const),
                  pl.BlockSpec((D_MODEL, IN_WIDTH), const),
                  pl.BlockSpec((1, ATTN_WIDTH), const),
                  pl.BlockSpec((1, ATTN_WIDTH), const),
                  pl.BlockSpec((1, MEM_WIDTH), const),
                  pl.BlockSpec((ATTN_WIDTH, ATTN_WIDTH), const),
                  pl.BlockSpec((MEM_WIDTH, MEM_WIDTH), const)],
        out_specs=out_specs,
        out_shape=out_shape,
        scratch_shapes=[pltpu.VMEM((N_PAIRS, tm, LANES), F32)],
        compiler_params=_cparams(("parallel",)),
        name="proj",
    )(x2d, attn_norm, w_in, gq, gk, gm, _group_ones(ATTN_WIDTH), _group_ones(MEM_WIDTH))


def _attn_kernel(q1, k1c, k1p, v1c, v1p, q4, k4c, k4p, v4c, v4p, q16, k16c, k16p, v16c, v16p,
                 bias_ref, o_ref, acc, mrun, lrun):
    c = pl.program_id(2)
    lane = lax.broadcasted_iota(I32, (BLOCK, LANES), 1)
    is_a = lane < HEAD_DIM
    lane_row = lax.broadcasted_iota(I32, (1, LANES), 1)
    mask_a = jnp.where(lane_row < HEAD_DIM, 1.0, 0.0).astype(BF16)
    mask_b = jnp.where(lane_row < HEAD_DIM, 0.0, 1.0).astype(BF16)
    bias_full = bias_ref[0]
    bias_first = jnp.where(c > 0, bias_full, bias_ref[1])

    def tile(q_t, kp_t, kc_t, vp_t, vc_t, bias):
        lhs = jnp.concatenate([q_t * mask_a, q_t * mask_b], axis=0)
        keys = jnp.concatenate([kp_t, kc_t], axis=0)
        s = lax.dot_general(lhs, keys, (((1,), (1,)), ((), ())), preferred_element_type=F32) + bias
        m = jnp.max(s, axis=-1, keepdims=True)
        p = jnp.exp(s - m)
        l = jnp.sum(p, axis=-1, keepdims=True)
        vals = jnp.concatenate([vp_t, vc_t], axis=0)
        pv = jnp.dot(p.astype(BF16), vals, preferred_element_type=F32)
        o_t = jnp.where(is_a, pv[:BLOCK], pv[BLOCK:])
        m_t = jnp.where(is_a, m[:BLOCK], m[BLOCK:])
        l_t = jnp.where(is_a, l[:BLOCK], l[BLOCK:])
        return o_t, m_t, l_t

    def merge(rows, o_t, m_t, l_t):
        m_o = mrun[rows, :]
        m_n = jnp.maximum(m_o, m_t)
        a = jnp.exp(m_o - m_n)
        b = jnp.exp(m_t - m_n)
        acc[rows, :] = acc[rows, :] * a + o_t * b
        lrun[rows, :] = lrun[rows, :] * a + l_t * b
        mrun[rows, :] = m_n

    for jb in range(SUPER // BLOCK):
        cur = pl.ds(jb * BLOCK, BLOCK)
        if jb == 0:
            kp_t, vp_t, bias = k1p[...], v1p[...], bias_first
        else:
            prev = pl.ds((jb - 1) * BLOCK, BLOCK)
            kp_t, vp_t, bias = k1c[prev, :], v1c[prev, :], bias_full
        o_t, m_t, l_t = tile(q1[cur, :], kp_t, k1c[cur, :], vp_t, v1c[cur, :], bias)
        acc[cur, :] = o_t
        mrun[cur, :] = m_t
        lrun[cur, :] = l_t
    for d, q, kc, kp, vc, vp in ((4, q4, k4c, k4p, v4c, v4p), (16, q16, k16c, k16p, v16c, v16p)):
        nblk = SUPER // (BLOCK * d)
        for r in range(d):
            cols = pl.ds(r * LANES, LANES)
            for jb in range(nblk):
                cur = pl.ds(jb * BLOCK, BLOCK)
                if jb == 0:
                    kp_t, vp_t, bias = kp[:, cols], vp[:, cols], bias_first
                else:
                    prev = pl.ds((jb - 1) * BLOCK, BLOCK)
                    kp_t, vp_t, bias = kc[prev, cols], vc[prev, cols], bias_full
                o_t, m_t, l_t = tile(q[cur, cols], kp_t, kc[cur, cols], vp_t, vc[cur, cols], bias)
                merge(pl.ds(jb * BLOCK * d + r, BLOCK, stride=d), o_t, m_t, l_t)
    o_ref[...] = (acc[...] / lrun[...]).astype(BF16)


def _band_bias():
    qi = jnp.arange(BLOCK)[:, None]
    kj = jnp.arange(2 * BLOCK)[None, :]
    dist = qi + BLOCK - kj
    in_band = (dist >= 0) & (dist <= BLOCK)
    full = jnp.where(in_band, 0.0, NEG_INF).astype(F32)
    first = jnp.where(in_band & (kj >= BLOCK), 0.0, NEG_INF).astype(F32)
    return jnp.stack([jnp.tile(full, (2, 1)), jnp.tile(first, (2, 1))])


def _attn(q1, k1, v1, q4, k4, v4, q16, k16, v16, B, S):
    nsup = S // SUPER

    def specs(d):
        rows = SUPER // d
        per = rows // BLOCK
        cur = pl.BlockSpec((None, None, rows, d * LANES), lambda b, hp, c: (b, hp, c, 0))
        prev = pl.BlockSpec((None, None, BLOCK, d * LANES),
                            lambda b, hp, c: (b, hp, jnp.maximum(per * c - 1, 0), 0))
        return [cur, cur, prev, cur, prev]

    return pl.pallas_call(
        _attn_kernel,
        grid=(B, N_PAIRS, nsup),
        in_specs=specs(1) + specs(4) + specs(16) + [
            pl.BlockSpec((2, 2 * BLOCK, 2 * BLOCK), lambda b, hp, c: (0, 0, 0))],
        out_specs=pl.BlockSpec((None, SUPER, LANES), lambda b, hp, c: (b, c, hp)),
        out_shape=jax.ShapeDtypeStruct((B, S, ATTN_WIDTH), BF16),
        scratch_shapes=[pltpu.VMEM((SUPER, LANES), F32)] * 3,
        compiler_params=_cparams(("parallel", "parallel", "parallel")),
        name="dilated_attn",
    )(q1, k1, k1, v1, v1, q4, k4, k4, v4, v4, q16, k16, k16, v16, v16, _band_bias())


def _memkv_kernel(mem_ref, mn_ref, wkv_ref, gk_ref, o256_ref, km_ref, vm_ref):
    m = mem_ref[...]
    ms = jnp.mean(m * m, axis=-1, keepdims=True)
    mn = (m * lax.rsqrt(ms + EPS) * mn_ref[...]).astype(BF16)
    kv = jnp.dot(mn, wkv_ref[...], preferred_element_type=F32)
    km_ref[...] = _head_norm(kv[:, :MEM_WIDTH], o256_ref, gk_ref).astype(BF16)
    vm_ref[...] = kv[:, MEM_WIDTH:].astype(BF16)


def _memkv(mem, mem_norm, w_mem_kv, gmk, B):
    const = lambda b: (0, 0)
    return pl.pallas_call(
        _memkv_kernel,
        grid=(B,),
        in_specs=[pl.BlockSpec((None, N_MEM, D_MODEL), lambda b: (b, 0, 0)),
                  pl.BlockSpec((1, D_MODEL), const),
                  pl.BlockSpec((D_MODEL, 2 * MEM_WIDTH), const),
                  pl.BlockSpec((1, MEM_WIDTH), const),
                  pl.BlockSpec((MEM_WIDTH, MEM_WIDTH), const)],
        out_specs=[pl.BlockSpec((None, N_MEM, MEM_WIDTH), lambda b: (b, 0, 0))] * 2,
        out_shape=[jax.ShapeDtypeStruct((B, N_MEM, MEM_WIDTH), BF16)] * 2,
        compiler_params=_cparams(("parallel",)),
        name="memkv",
    )(mem, mem_norm, w_mem_kv, gmk, _group_ones(MEM_WIDTH))


def _mix_kernel(tiles_per_seq, ya_ref, u_ref, uh_ref, qm_ref, km_ref, vm_ref, x_ref, pp_ref, ps_ref,
                wo_ref, fn_ref, wr_ref, br_ref, ltri_ref,
                x2_ref, h2t_ref, rinfo_ref, rinfot_ref, cnt_ref, base_ref):
    i = pl.program_id(0)
    tm = x_ref.shape[0]
    seq_tile = i % tiles_per_seq

    @pl.when(i == 0)
    def _():
        base_ref[...] = jnp.zeros_like(base_ref)

    u = u_ref[...]
    halo = jnp.where(seq_tile == 0, 0.0, uh_ref[...])
    uu = jnp.concatenate([halo, u], axis=0)
    a1 = uu[1:] + uu[:-1]
    a2 = a1[2:] + a1[:-2]
    a3 = a2[4:] + a2[:-4]
    a4 = a3[8:] + a3[:-8]
    lane_p = lax.broadcasted_iota(I32, (tm, POOL_WIDTH), 1)
    g0, g1, g2 = lane_p < 64, lane_p < 128, lane_p < 192
    wsum = jnp.where(g0, a1[15:], jnp.where(g1, a2[13:], jnp.where(g2, a3[9:], a4[1:])))
    wlen = jnp.where(g0, 2.0, jnp.where(g1, 4.0, jnp.where(g2, 8.0, 16.0)))
    tpos = seq_tile * tm + lax.broadcasted_iota(I32, (tm, POOL_WIDTH), 0) + 1
    cnt = jnp.minimum(tpos.astype(F32), wlen)
    pooled = wsum / cnt - u
    y_pool = jnp.dot(pooled.astype(BF16), pp_ref[...], preferred_element_type=F32) * ps_ref[...]

    lane = lax.broadcasted_iota(I32, (tm, LANES), 1)
    is_a = lane < HEAD_DIM
    lane_row = lax.broadcasted_iota(I32, (1, LANES), 1)
    mask_a = jnp.where(lane_row < HEAD_DIM, 1.0, 0.0).astype(BF16)
    mask_b = jnp.where(lane_row < HEAD_DIM, 0.0, 1.0).astype(BF16)
    y_mem = []
    for pr in range(MEM_WIDTH // LANES):
        cols = slice(pr * LANES, (pr + 1) * LANES)
        qp, kp, vp = qm_ref[:, cols], km_ref[:, cols], vm_ref[:, cols]
        outs = []
        for msk in (mask_a, mask_b):
            s = lax.dot_general(qp * msk, kp, (((1,), (1,)), ((), ())), preferred_element_type=F32)
            m = jnp.max(s, axis=-1, keepdims=True)
            p = jnp.exp(s - m)
            p = p / jnp.sum(p, axis=-1, keepdims=True)
            outs.append(jnp.dot(p.astype(BF16), vp, preferred_element_type=F32))
        y_mem.append(jnp.where(is_a, outs[0], outs[1]))

    a0, a1w = ATTN_WIDTH, ATTN_WIDTH + POOL_WIDTH
    proj = jnp.dot(ya_ref[...], wo_ref[:a0, :], preferred_element_type=F32)
    proj += jnp.dot(y_pool.astype(BF16), wo_ref[a0:a1w, :], preferred_element_type=F32)
    for pr in range(MEM_WIDTH // LANES):
        lo = a1w + pr * LANES
        proj += jnp.dot(y_mem[pr].astype(BF16), wo_ref[lo:lo + LANES, :], preferred_element_type=F32)
    x2 = x_ref[...] + proj
    x2_ref[...] = x2

    ms = jnp.mean(x2 * x2, axis=-1, keepdims=True)
    h2f = x2 * lax.rsqrt(ms + EPS) * fn_ref[...]
    for c in range(ROW_SUB):
        h2t_ref[pl.ds(c, tm, stride=ROW_SUB), :] = h2f[:, c * LANES:(c + 1) * LANES]
    h2 = h2f.astype(BF16)

    logits = jnp.dot(h2, wr_ref[...], preferred_element_type=F32) + br_ref[...]
    ninf = -jnp.inf
    is_g = lane < N_GROUPS
    lg = jnp.where(is_g, logits, ninf)
    mg = jnp.max(lg, axis=-1, keepdims=True)
    g_sel = jnp.min(jnp.where(lg == mg, lane, LANES), axis=-1, keepdims=True)
    w_g = 1.0 / jnp.sum(jnp.where(is_g, jnp.exp(logits - mg), 0.0), axis=-1, keepdims=True)
    in_grp = (lane >= EXPERT_LANE0) & (((lane - EXPERT_LANE0) >> 3) == g_sel)
    le = jnp.where(in_grp, logits, ninf)
    v1 = jnp.max(le, axis=-1, keepdims=True)
    i1 = jnp.min(jnp.where(le == v1, lane, 2 * LANES), axis=-1, keepdims=True)
    le2 = jnp.where(lane == i1, ninf, le)
    v2 = jnp.max(le2, axis=-1, keepdims=True)
    i2 = jnp.min(jnp.where(le2 == v2, lane, 2 * LANES), axis=-1, keepdims=True)
    e21 = jnp.exp(v2 - v1)
    gate1 = w_g / (1.0 + e21)
    gate2 = w_g * e21 / (1.0 + e21)

    hot1 = lane == i1
    hot2 = lane == i2
    onehot = jnp.where(hot1 | hot2, 1.0, 0.0)
    prefix = jnp.dot(ltri_ref[...], onehot.astype(BF16), preferred_element_type=F32)
    tot = prefix + base_ref[...]
    rank1 = jnp.sum(jnp.where(hot1, tot, 0.0), axis=-1, keepdims=True)
    rank2 = jnp.sum(jnp.where(hot2, tot, 0.0), axis=-1, keepdims=True)
    base_new = base_ref[...] + jnp.sum(onehot, axis=0, keepdims=True)
    base_ref[...] = base_new
    cnt_ref[...] = base_new

    e1 = (i1 - EXPERT_LANE0).astype(F32)
    e2 = (i2 - EXPERT_LANE0).astype(F32)
    rinfo = jnp.where(lane == 0, e1, jnp.where(lane == 1, e2, jnp.where(lane == 2, rank1, jnp.where(
        lane == 3, rank2, jnp.where(lane == 4, gate1, jnp.where(lane == 5, gate2, 0.0))))))
    rinfo_ref[...] = rinfo
    rinfot_ref[...] = jnp.transpose(rinfo)[:8, :]


def _mix(ya, u, qm, km, vm, x2d, pool_bd, pool_scale, w_out, ffn_norm, w_r, b_r, B, S):
    T = B * S
    tm = TM_MIX
    tps = S // tm
    hb = tm // HALO
    const = lambda i: (0, 0)
    ltri = (jnp.arange(tm)[:, None] > jnp.arange(tm)[None, :]).astype(BF16)
    return pl.pallas_call(
        functools.partial(_mix_kernel, tps),
        grid=(T // tm,),
        in_specs=[pl.BlockSpec((tm, ATTN_WIDTH), lambda i: (i, 0)),
                  pl.BlockSpec((tm, POOL_WIDTH), lambda i: (i, 0)),
                  pl.BlockSpec((HALO, POOL_WIDTH), lambda i: (jnp.maximum(i * hb - 1, 0), 0)),
                  pl.BlockSpec((tm, MEM_WIDTH), lambda i: (i, 0)),
                  pl.BlockSpec((None, N_MEM, MEM_WIDTH), lambda i: (i // tps, 0, 0)),
                  pl.BlockSpec((None, N_MEM, MEM_WIDTH), lambda i: (i // tps, 0, 0)),
                  pl.BlockSpec((tm, D_MODEL), lambda i: (i, 0)),
                  pl.BlockSpec((POOL_WIDTH, POOL_WIDTH), const),
                  pl.BlockSpec((1, POOL_WIDTH), const),
                  pl.BlockSpec((D_MODEL, D_MODEL), const),
                  pl.BlockSpec((1, D_MODEL), const),
                  pl.BlockSpec((D_MODEL, LANES), const),
                  pl.BlockSpec((1, LANES), const),
                  pl.BlockSpec((tm, tm), const)],
        out_specs=[pl.BlockSpec((tm, D_MODEL), lambda i: (i, 0)),
                   pl.BlockSpec((tm * ROW_SUB, LANES), lambda i: (i, 0)),
                   pl.BlockSpec((tm, LANES), lambda i: (i, 0)),
                   pl.BlockSpec((8, tm), lambda i: (0, i)),
                   pl.BlockSpec((1, LANES), const)],
        out_shape=[jax.ShapeDtypeStruct((T, D_MODEL), F32),
                   jax.ShapeDtypeStruct((T * ROW_SUB, LANES), F32),
                   jax.ShapeDtypeStruct((T, LANES), F32),
                   jax.ShapeDtypeStruct((8, T), F32),
                   jax.ShapeDtypeStruct((1, LANES), F32)],
        scratch_shapes=[pltpu.VMEM((1, LANES), F32)],
        compiler_params=_cparams(("arbitrary",)),
        name="mix_router",
    )(ya, u, u, qm, km, vm, x2d, pool_bd, pool_scale, w_out, ffn_norm, w_r, b_r, ltri)


def _dest_kernel(rt_ref, ps_ref, d_ref):
    tn = rt_ref.shape[1]
    sub = lax.broadcasted_iota(I32, (N_EXPERTS, tn), 0).astype(F32)
    ps = ps_ref[...]
    rows = []
    for k in range(2):
        e = rt_ref[k:k + 1, :]
        start = jnp.sum(jnp.where(sub == e, ps, 0.0), axis=0, keepdims=True)
        rows.append(start + rt_ref[2 + k:3 + k, :])
    d = jnp.concatenate(rows + [jnp.zeros((6, tn), F32)], axis=0)
    d_ref[...] = d.astype(I32)


def _dest(rinfot, pstart_col, T):
    tn = 2048
    return pl.pallas_call(
        _dest_kernel,
        grid=(T // tn,),
        in_specs=[pl.BlockSpec((8, tn), lambda i: (0, i)),
                  pl.BlockSpec((N_EXPERTS, 1), lambda i: (0, 0))],
        out_specs=pl.BlockSpec((8, tn), lambda i: (0, i)),
        out_shape=jax.ShapeDtypeStruct((8, T), I32),
        compiler_params=_cparams(("parallel",)),
        name="dest_rows",
    )(rinfot, pstart_col)


def _rowmap_kernel(d1_ref, d2_ref, fill_hbm, inv_ref, sem):
    T = d1_ref.shape[0]
    cp = pltpu.make_async_copy(fill_hbm, inv_ref, sem)
    cp.start()
    cp.wait()

    def body(t, carry):
        inv_ref[d1_ref[t]] = t
        inv_ref[d2_ref[t]] = T + t
        return carry

    lax.fori_loop(0, T, body, 0, unroll=8)


def _rowmap(d1, d2, R):
    return pl.pallas_call(
        _rowmap_kernel,
        grid_spec=pltpu.PrefetchScalarGridSpec(
            num_scalar_prefetch=2,
            grid=(1,),
            in_specs=[pl.BlockSpec(memory_space=pl.ANY)],
            out_specs=pl.BlockSpec(memory_space=pltpu.SMEM),
            scratch_shapes=[pltpu.SemaphoreType.DMA(())]),
        out_shape=jax.ShapeDtypeStruct((R,), I32),
        compiler_params=_cparams(("arbitrary",)),
        name="rowmap",
    )(d1, d2, jnp.zeros((R,), I32))


def _row_loop(n, body):
    def group(g, carry):
        for u in range(8):
            body(g * 8 + u)
        return carry

    def rest(i, carry):
        body(((n >> 3) << 3) + i)
        return carry

    lax.fori_loop(0, n >> 3, group, 0)
    lax.fori_loop(0, n & 7, rest, 0)


def _moe_kernel(be_ref, nu_ref, first_ref, nxt_ref, ws_ref, valid_ref, inv_ref,
                h_hbm, w1_hbm, w3_hbm, w2_hbm, o_hbm,
                xbuf, ybuf, w1s, w3s, w2s, w1b, w3b, w2b, gsem, ssem, wsem):
    j = pl.program_id(0)
    nu = nu_ref[0]
    T = h_hbm.shape[0] // ROW_SUB
    tr = TR_MOE

    def tile_rows(ref, row, n=1):
        return ref.at[pl.ds(pl.multiple_of(row * ROW_SUB, ROW_SUB), n * ROW_SUB)]

    def gather_row(tile, slot, i):
        v = inv_ref[tile * tr + i]
        tok = (v & (T - 1)) if T & (T - 1) == 0 else jnp.where(v >= T, v - T, v)
        return pltpu.make_async_copy(tile_rows(h_hbm, tok), tile_rows(xbuf, slot * tr + i), gsem.at[slot])

    def gather_wait(slot, n):
        def done(rows):
            return pltpu.make_async_copy(tile_rows(h_hbm, 0, rows), tile_rows(xbuf, slot * tr, rows), gsem.at[slot])
        lax.fori_loop(0, n >> 3, lambda g, c: (done(8).wait(), c)[1], 0)
        lax.fori_loop(0, n & 7, lambda g, c: (done(1).wait(), c)[1], 0)

    def scatter_row(tile, slot, i):
        v = inv_ref[tile * tr + i]
        return pltpu.make_async_copy(tile_rows(ybuf, slot * tr + i), tile_rows(o_hbm, v), ssem.at[slot])

    def scatter_wait(slot, n):
        def done(rows):
            return pltpu.make_async_copy(tile_rows(ybuf, slot * tr, rows), tile_rows(o_hbm, 0, rows), ssem.at[slot])
        lax.fori_loop(0, n >> 3, lambda g, c: (done(8).wait(), c)[1], 0)
        lax.fori_loop(0, n & 7, lambda g, c: (done(1).wait(), c)[1], 0)

    def weight_copies(e, slot):
        return [pltpu.make_async_copy(src.at[e], dst.at[slot], wsem.at[slot])
                for src, dst in ((w1_hbm, w1s), (w3_hbm, w3s), (w2_hbm, w2s))]

    @pl.when(j < nu)
    def _():
        s = j & 1
        e = be_ref[j]

        @pl.when(j == 0)
        def _():
            xbuf[...] = jnp.zeros_like(xbuf)
            _row_loop(valid_ref[0], lambda i: gather_row(0, 0, i).start())
            for cp in weight_copies(e, 0):
                cp.start()

        gather_wait(s, valid_ref[j])

        @pl.when(j + 1 < nu)
        def _():
            _row_loop(valid_ref[j + 1], lambda i: gather_row(j + 1, 1 - s, i).start())

        @pl.when(first_ref[j] == 1)
        def _():
            wslot = ws_ref[j]
            for cp in weight_copies(e, wslot):
                cp.wait()
            e_next = nxt_ref[j]

            @pl.when(e_next >= 0)
            def _():
                for cp in weight_copies(e_next, 1 - wslot):
                    cp.start()

            w1b[...] = w1s[wslot].astype(BF16)
            w3b[...] = w3s[wslot].astype(BF16)
            w2b[...] = w2s[wslot].astype(BF16)

        @pl.when(j >= 2)
        def _():
            scatter_wait(s, valid_ref[j - 2])

        base = pl.multiple_of(s * (tr * ROW_SUB), tr * ROW_SUB)
        x = jnp.concatenate([xbuf[pl.ds(base + c, tr, stride=ROW_SUB), :] for c in range(ROW_SUB)],
                            axis=1).astype(BF16)
        a = jnp.dot(x, w1b[...], preferred_element_type=F32)
        b = jnp.dot(x, w3b[...], preferred_element_type=F32)
        hmid = (a / (1.0 + jnp.exp(-a)) * b).astype(BF16)
        y = jnp.dot(hmid, w2b[...], preferred_element_type=F32)
        for c in range(ROW_SUB):
            ybuf[pl.ds(base + c, tr, stride=ROW_SUB), :] = y[:, c * LANES:(c + 1) * LANES]

        _row_loop(valid_ref[j], lambda i: scatter_row(j, s, i).start())

        @pl.when(j == nu - 1)
        def _():
            scatter_wait(s, valid_ref[j])

            @pl.when(j >= 1)
            def _():
                scatter_wait(1 - s, valid_ref[j - 1])


def _moe(blk_exp, n_used, first, nxt, wslot, valid, inv, h2t, w1, w3, w2):
    T = h2t.shape[0] // ROW_SUB
    tr = TR_MOE
    n_tiles = blk_exp.shape[0]
    any_spec = pl.BlockSpec(memory_space=pl.ANY)
    return pl.pallas_call(
        _moe_kernel,
        grid_spec=pltpu.PrefetchScalarGridSpec(
            num_scalar_prefetch=7,
            grid=(n_tiles,),
            in_specs=[any_spec] * 4,
            out_specs=any_spec,
            scratch_shapes=[pltpu.VMEM((2 * tr * ROW_SUB, LANES), F32),
                            pltpu.VMEM((2 * tr * ROW_SUB, LANES), F32),
                            pltpu.VMEM((2, D_MODEL, EXPERT_HIDDEN), F32),
                            pltpu.VMEM((2, D_MODEL, EXPERT_HIDDEN), F32),
                            pltpu.VMEM((2, EXPERT_HIDDEN, D_MODEL), F32),
                            pltpu.VMEM((D_MODEL, EXPERT_HIDDEN), BF16),
                            pltpu.VMEM((D_MODEL, EXPERT_HIDDEN), BF16),
                            pltpu.VMEM((EXPERT_HIDDEN, D_MODEL), BF16),
                            pltpu.SemaphoreType.DMA((2,)),
                            pltpu.SemaphoreType.DMA((2,)),
                            pltpu.SemaphoreType.DMA((2,))]),
        out_shape=jax.ShapeDtypeStruct((2 * T * ROW_SUB, LANES), F32),
        compiler_params=_cparams(("arbitrary",)),
        name="moe_experts",
    )(blk_exp, n_used, first, nxt, wslot, valid, inv, h2t, w1, w3, w2)


def _combine_kernel(x2_ref, rinfo_ref, y1_ref, y2_ref, o_ref):
    g1 = rinfo_ref[:, 4:5]
    g2 = rinfo_ref[:, 5:6]
    tc = x2_ref.shape[0]
    for c in range(ROW_SUB):
        cols = slice(c * LANES, (c + 1) * LANES)
        rows = pl.ds(c, tc, stride=ROW_SUB)
        o_ref[:, cols] = x2_ref[:, cols] + (g1 * y1_ref[rows, :] + g2 * y2_ref[rows, :])


def _combine(x2, rinfo, y2slot):
    T = x2.shape[0]
    tc = TC_COMBINE
    return pl.pallas_call(
        _combine_kernel,
        grid=(T // tc,),
        in_specs=[pl.BlockSpec((tc, D_MODEL), lambda i: (i, 0)),
                  pl.BlockSpec((tc, LANES), lambda i: (i, 0)),
                  pl.BlockSpec((tc * ROW_SUB, LANES), lambda i: (i, 0)),
                  pl.BlockSpec((tc * ROW_SUB, LANES), lambda i: (T // tc + i, 0))],
        out_specs=pl.BlockSpec((tc, D_MODEL), lambda i: (i, 0)),
        out_shape=jax.ShapeDtypeStruct((T, D_MODEL), F32),
        compiler_params=_cparams(("parallel",)),
        name="combine",
    )(x2, rinfo, y2slot, y2slot)


def _layer(x, mem, attn_norm, w_in, q_norm, k_norm, pool_proj, pool_scale, mem_norm, w_mem_kv,
           mq_norm, mk_norm, w_out, ffn_norm, w_group, b_group, w_router, b_router, w1, w3, w2):
    B, S, D = x.shape
    T = B * S
    assert D == D_MODEL and S % SUPER == 0 and T % TM_PROJ == 0
    x2d = x.reshape(T, D)
    row = lambda v: v.reshape(1, -1).astype(F32)
    scale = HEAD_DIM ** -0.5
    gq = row(jnp.tile(q_norm, ATTN_WIDTH // HEAD_DIM) * scale)
    gk = row(jnp.tile(k_norm, ATTN_WIDTH // HEAD_DIM))
    gmq = row(jnp.tile(mq_norm, MEM_WIDTH // HEAD_DIM) * scale)
    gmk = row(jnp.tile(mk_norm, MEM_WIDTH // HEAD_DIM))

    (q1, k1, v1, q4, k4, v4, q16, k16, v16, u, qm) = _proj(
        x2d, row(attn_norm), w_in.astype(BF16), gq, gk, gmq, B, S)
    ya = _attn(q1, k1, v1, q4, k4, v4, q16, k16, v16, B, S).reshape(T, ATTN_WIDTH)
    km, vm = _memkv(mem, row(mem_norm), w_mem_kv.astype(BF16), gmk, B)

    pool_bd = jax.scipy.linalg.block_diag(*[pool_proj[g] for g in range(pool_proj.shape[0])]).astype(BF16)
    w_r = jnp.zeros((D, LANES), F32)
    w_r = w_r.at[:, :N_GROUPS].set(w_group)
    w_r = w_r.at[:, EXPERT_LANE0:].set(jnp.transpose(w_router, (1, 0, 2)).reshape(D, N_EXPERTS))
    b_r = jnp.zeros((1, LANES), F32).at[0, :N_GROUPS].set(b_group).at[0, EXPERT_LANE0:].set(b_router.reshape(-1))
    x2, h2t, rinfo, rinfot, counts = _mix(ya, u, qm, km, vm, x2d, pool_bd, row(pool_scale),
                                          w_out.astype(BF16), row(ffn_norm), w_r.astype(BF16), b_r, B, S)

    R, pstart, sched = _tile_schedule(counts, T)
    dest = _dest(rinfot, pstart.astype(F32).reshape(N_EXPERTS, 1), T)
    inv = _rowmap(dest[0], dest[1], R)
    y2slot = _moe(*sched, inv, h2t, w1, w3, w2)
    out = _combine(x2, rinfo, y2slot)
    return out.reshape(B, S, D)


def _tile_schedule(counts, T):
    tr = TR_MOE
    R = 2 * T + N_EXPERTS * tr
    n_tiles = R // tr
    cnt = counts[0, EXPERT_LANE0:].astype(I32)
    padded = ((cnt + tr - 1) // tr) * tr
    pend = jnp.cumsum(padded)
    pstart = pend - padded
    n_used = (pend[-1] // tr).astype(I32).reshape(1)
    tiles = jnp.arange(n_tiles, dtype=I32)
    tile_row = jnp.minimum(tiles, n_used[0] - 1) * tr
    blk_exp = jnp.minimum(jnp.sum(tile_row[:, None] >= pend[None, :], axis=1), N_EXPERTS - 1).astype(I32)
    in_use = tiles < n_used[0]
    first = in_use & ((tiles == 0) | (blk_exp != jnp.roll(blk_exp, 1)))
    run_idx = jnp.cumsum(first.astype(I32)) - 1
    used = cnt > 0
    exp_of_run = jnp.argsort(jnp.logical_not(used), stable=True).astype(I32)
    nxt = jnp.where(run_idx + 1 < jnp.sum(used), exp_of_run[jnp.clip(run_idx + 1, 0, N_EXPERTS - 1)], -1)
    valid = jnp.where(in_use, jnp.clip(cnt[blk_exp] - (tiles * tr - pstart[blk_exp]), 0, tr), 0)
    sched = (blk_exp, n_used, first.astype(I32), nxt.astype(I32), (run_idx & 1).astype(I32), valid.astype(I32))
    return R, pstart, sched


def kernel(x, mem, attn_norm, w_in, q_norm, k_norm, pool_proj, pool_scale, mem_norm, w_mem_kv, mq_norm, mk_norm,
           w_out, ffn_norm, w_group, b_group, w_router, b_router, w1, w3, w2):
    for l in range(attn_norm.shape[0]):
        x = _layer(x, mem, attn_norm[l], w_in[l], q_norm[l], k_norm[l], pool_proj[l], pool_scale[l],
                   mem_norm[l], w_mem_kv[l], mq_norm[l], mk_norm[l], w_out[l], ffn_norm[l],
                   w_group[l], b_group[l], w_router[l], b_router[l], w1[l], w3[l], w2[l])
    return x
```

```python
import functools

import jax
import jax.numpy as jnp
from jax import lax
from jax.experimental import pallas as pl
from jax.experimental.pallas import tpu as pltpu

F32 = jnp.float32
BF16 = jnp.bfloat16
I32 = jnp.int32
U32 = jnp.uint32

D_MODEL = 1024
HEAD_DIM = 64
ATTN_WIDTH = 512
POOL_WIDTH = 256
MEM_WIDTH = 256
N_MEM = 256
IN_WIDTH = 3 * ATTN_WIDTH + POOL_WIDTH + MEM_WIDTH
N_GROUPS = 8
EXPERTS_PER_GROUP = 8
N_EXPERTS = 64
EXPERT_HIDDEN = 512
EPS = 1e-6
NEG_INF = -1e30

LANES = 128
N_PAIRS = ATTN_WIDTH // LANES
BLOCK = 128
SUPER = 16 * BLOCK
HALO = 16

ROW_SUB = D_MODEL // LANES

TM_PROJ = 512
TM_MIX = 512
TR_MOE = 256
ROW_UNROLL = 16
TC_COMBINE = 256
EXPERT_LANE0 = 64

VMEM_LIMIT = 48 * 1024 * 1024


def _cparams(sem):
    return pltpu.CompilerParams(dimension_semantics=sem, vmem_limit_bytes=VMEM_LIMIT)


def _group_ones(n):
    g = jnp.arange(n) // HEAD_DIM
    return (g[:, None] == g[None, :]).astype(BF16)


def _head_norm(z, ones_ref, gain_ref):
    ss = jnp.dot((z * z).astype(BF16), ones_ref[...], preferred_element_type=F32)
    return z * lax.rsqrt(ss * (1.0 / HEAD_DIM) + EPS) * gain_ref[...]


def _proj_kernel(x_ref, an_ref, win_ref, gq_ref, gk_ref, gm_ref, o512_ref, o256_ref,
                 q1_ref, k1_ref, v1_ref, q4_ref, k4_ref, v4_ref, q16_ref, k16_ref, v16_ref,
                 u_ref, qm_ref, zs_ref):
    tm = x_ref.shape[0]
    x = x_ref[...]
    ms = jnp.mean(x * x, axis=-1, keepdims=True)
    h = (x * lax.rsqrt(ms + EPS) * an_ref[...]).astype(BF16)
    z = jnp.dot(h, win_ref[...], preferred_element_type=F32)
    a0, a1, a2, a3 = ATTN_WIDTH, 2 * ATTN_WIDTH, 3 * ATTN_WIDTH, 3 * ATTN_WIDTH + POOL_WIDTH
    qn = _head_norm(z[:, :a0], o512_ref, gq_ref)
    kn = _head_norm(z[:, a0:a1], o512_ref, gk_ref)
    v = z[:, a1:a2]
    u_ref[...] = z[:, a2:a3]
    qm_ref[...] = _head_norm(z[:, a3:], o256_ref, gm_ref).astype(BF16)
    for val, o1, o4, o16 in ((qn, q1_ref, q4_ref, q16_ref), (kn, k1_ref, k4_ref, k16_ref),
                             (v, v1_ref, v4_ref, v16_ref)):
        for hp in range(N_PAIRS):
            pair = val[:, hp * LANES:(hp + 1) * LANES]
            zs_ref[hp] = pair
            o1[hp] = pair.astype(BF16)
        for d, o in ((4, o4), (16, o16)):
            for r in range(d):
                for hp in range(N_PAIRS):
                    rows = zs_ref[hp, pl.ds(r, tm // d, stride=d), :]
                    o[hp, :, r * LANES:(r + 1) * LANES] = rows.astype(BF16)


def _proj(x2d, attn_norm, w_in, gq, gk, gm, B, S):
    T = B * S
    tm = TM_PROJ
    nj = S // tm
    const = lambda i: (0, 0)

    def lay(d):
        return jax.ShapeDtypeStruct((B, N_PAIRS, S // d, d * LANES), BF16)

    def lay_spec(d):
        return pl.BlockSpec((None, N_PAIRS, tm // d, d * LANES), lambda i: (i // nj, 0, i % nj, 0))

    out_shape = [lay(1)] * 3 + [lay(4)] * 3 + [lay(16)] * 3 + [
        jax.ShapeDtypeStruct((T, POOL_WIDTH), F32), jax.ShapeDtypeStruct((T, MEM_WIDTH), BF16)]
    out_specs = [lay_spec(1)] * 3 + [lay_spec(4)] * 3 + [lay_spec(16)] * 3 + [
        pl.BlockSpec((tm, POOL_WIDTH), lambda i: (i, 0)), pl.BlockSpec((tm, MEM_WIDTH), lambda i: (i, 0))]
    return pl.pallas_call(
        _proj_kernel,
        grid=(T // tm,),
        in_specs=[pl.BlockSpec((tm, D_MODEL), lambda i: (i, 0)),
                  pl.BlockSpec((1, D_MODEL), const),
                  pl.BlockSpec((D_MODEL, IN_WIDTH), const),
                  pl.BlockSpec((1, ATTN_WIDTH), const),
                  pl.BlockSpec((1, ATTN_WIDTH), const),
                  pl.BlockSpec((1, MEM_WIDTH), const),
                  pl.BlockSpec((ATTN_WIDTH, ATTN_WIDTH), const),
                  pl.BlockSpec((MEM_WIDTH, MEM_WIDTH), const)],
        out_specs=out_specs,
        out_shape=out_shape,
        scratch_shapes=[pltpu.VMEM((N_PAIRS, tm, LANES), F32)],
        compiler_params=_cparams(("parallel",)),
        name="proj",
    )(x2d, attn_norm, w_in, gq, gk, gm, _group_ones(ATTN_WIDTH), _group_ones(MEM_WIDTH))


def _attn_kernel(q1, k1c, k1p, v1c, v1p, q4, k4c, k4p, v4c, v4p, q16, k16c, k16p, v16c, v16p,
                 bias_ref, o_ref, acc, mrun, lrun):
    c = pl.program_id(2)
    lane = lax.broadcasted_iota(I32, (BLOCK, LANES), 1)
    is_a = lane < HEAD_DIM
    lane_row = lax.broadcasted_iota(I32, (1, LANES), 1)
    mask_a = jnp.where(lane_row < HEAD_DIM, 1.0, 0.0).astype(BF16)
    mask_b = jnp.where(lane_row < HEAD_DIM, 0.0, 1.0).astype(BF16)
    bias_full = bias_ref[0]
    bias_first = jnp.where(c > 0, bias_full, bias_ref[1])

    def tile(q_t, kp_t, kc_t, vp_t, vc_t, bias):
        lhs = jnp.concatenate([q_t * mask_a, q_t * mask_b], axis=0)
        keys = jnp.concatenate([kp_t, kc_t], axis=0)
        s = lax.dot_general(lhs, keys, (((1,), (1,)), ((), ())), preferred_element_type=F32) + bias
        m = jnp.max(s, axis=-1, keepdims=True)
        p = jnp.exp(s - m)
        l = jnp.sum(p, axis=-1, keepdims=True)
        vals = jnp.concatenate([vp_t, vc_t], axis=0)
        pv = jnp.dot(p.astype(BF16), vals, preferred_element_type=F32)
        o_t = jnp.where(is_a, pv[:BLOCK], pv[BLOCK:])
        m_t = jnp.where(is_a, m[:BLOCK], m[BLOCK:])
        l_t = jnp.where(is_a, l[:BLOCK], l[BLOCK:])
        return o_t, m_t, l_t

    def merge(rows, o_t, m_t, l_t):
        m_o = mrun[rows, :]
        m_n = jnp.maximum(m_o, m_t)
        a = jnp.exp(m_o - m_n)
        b = jnp.exp(m_t - m_n)
        acc[rows, :] = acc[rows, :] * a + o_t * b
        lrun[rows, :] = lrun[rows, :] * a + l_t * b
        mrun[rows, :] = m_n

    for jb in range(SUPER // BLOCK):
        cur = pl.ds(jb * BLOCK, BLOCK)
        if jb == 0:
            kp_t, vp_t, bias = k1p[...], v1p[...], bias_first
        else:
            prev = pl.ds((jb - 1) * BLOCK, BLOCK)
            kp_t, vp_t, bias = k1c[prev, :], v1c[prev, :], bias_full
        o_t, m_t, l_t = tile(q1[cur, :], kp_t, k1c[cur, :], vp_t, v1c[cur, :], bias)
        acc[cur, :] = o_t
        mrun[cur, :] = m_t
        lrun[cur, :] = l_t
    for d, q, kc, kp, vc, vp in ((4, q4, k4c, k4p, v4c, v4p), (16, q16, k16c, k16p, v16c, v16p)):
        nblk = SUPER // (BLOCK * d)
        for r in range(d):
            cols = pl.ds(r * LANES, LANES)
            for jb in range(nblk):
                cur = pl.ds(jb * BLOCK, BLOCK)
                if jb == 0:
                    kp_t, vp_t, bias = kp[:, cols], vp[:, cols], bias_first
                else:
                    prev = pl.ds((jb - 1) * BLOCK, BLOCK)
                    kp_t, vp_t, bias = kc[prev, cols], vc[prev, cols], bias_full
                o_t, m_t, l_t = tile(q[cur, cols], kp_t, kc[cur, cols], vp_t, vc[cur, cols], bias)
                merge(pl.ds(jb * BLOCK * d + r, BLOCK, stride=d), o_t, m_t, l_t)
    o_ref[...] = (acc[...] / lrun[...]).astype(BF16)


def _band_bias():
    qi = jnp.arange(BLOCK)[:, None]
    kj = jnp.arange(2 * BLOCK)[None, :]
    dist = qi + BLOCK - kj
    in_band = (dist >= 0) & (dist <= BLOCK)
    full = jnp.where(in_band, 0.0, NEG_INF).astype(F32)
    first = jnp.where(in_band & (kj >= BLOCK), 0.0, NEG_INF).astype(F32)
    return jnp.stack([jnp.tile(full, (2, 1)), jnp.tile(first, (2, 1))])


def _attn(q1, k1, v1, q4, k4, v4, q16, k16, v16, B, S):
    nsup = S // SUPER

    def specs(d):
        rows = SUPER // d
        per = rows // BLOCK
        cur = pl.BlockSpec((None, None, rows, d * LANES), lambda b, hp, c: (b, hp, c, 0))
        prev = pl.BlockSpec((None, None, BLOCK, d * LANES),
                            lambda b, hp, c: (b, hp, jnp.maximum(per * c - 1, 0), 0))
        return [cur, cur, prev, cur, prev]

    return pl.pallas_call(
        _attn_kernel,
        grid=(B, N_PAIRS, nsup),
        in_specs=specs(1) + specs(4) + specs(16) + [
            pl.BlockSpec((2, 2 * BLOCK, 2 * BLOCK), lambda b, hp, c: (0, 0, 0))],
        out_specs=pl.BlockSpec((None, SUPER, LANES), lambda b, hp, c: (b, c, hp)),
        out_shape=jax.ShapeDtypeStruct((B, S, ATTN_WIDTH), BF16),
        scratch_shapes=[pltpu.VMEM((SUPER, LANES), F32)] * 3,
        compiler_params=_cparams(("parallel", "parallel", "parallel")),
        name="dilated_attn",
    )(q1, k1, k1, v1, v1, q4, k4, k4, v4, v4, q16, k16, k16, v16, v16, _band_bias())


def _memkv_kernel(mem_ref, mn_ref, wkv_ref, gk_ref, o256_ref, km_ref, vm_ref):
    m = mem_ref[...]
    ms = jnp.mean(m * m, axis=-1, keepdims=True)
    mn = (m * lax.rsqrt(ms + EPS) * mn_ref[...]).astype(BF16)
    kv = jnp.dot(mn, wkv_ref[...], preferred_element_type=F32)
    km_ref[...] = _head_norm(kv[:, :MEM_WIDTH], o256_ref, gk_ref).astype(BF16)
    vm_ref[...] = kv[:, MEM_WIDTH:].astype(BF16)


def _memkv(mem, mem_norm, w_mem_kv, gmk, B):
    const = lambda b: (0, 0)
    return pl.pallas_call(
        _memkv_kernel,
        grid=(B,),
        in_specs=[pl.BlockSpec((None, N_MEM, D_MODEL), lambda b: (b, 0, 0)),
                  pl.BlockSpec((1, D_MODEL), const),
                  pl.BlockSpec((D_MODEL, 2 * MEM_WIDTH), const),
                  pl.BlockSpec((1, MEM_WIDTH), const),
                  pl.BlockSpec((MEM_WIDTH, MEM_WIDTH), const)],
        out_specs=[pl.BlockSpec((None, N_MEM, MEM_WIDTH), lambda b: (b, 0, 0))] * 2,
        out_shape=[jax.ShapeDtypeStruct((B, N_MEM, MEM_WIDTH), BF16)] * 2,
        compiler_params=_cparams(("parallel",)),
        name="memkv",
    )(mem, mem_norm, w_mem_kv, gmk, _group_ones(MEM_WIDTH))


def _mix_kernel(tiles_per_seq, ya_ref, u_ref, uh_ref, qm_ref, km_ref, vm_ref, x_ref, pp_ref, ps_ref,
                wo_ref, fn_ref, wr_ref, br_ref, ltri_ref,
                x2_ref, h2t_ref, rinfo_ref, rinfot_ref, cnt_ref, base_ref):
    i = pl.program_id(0)
    tm = x_ref.shape[0]
    seq_tile = i % tiles_per_seq

    @pl.when(i == 0)
    def _():
        base_ref[...] = jnp.zeros_like(base_ref)

    u = u_ref[...]
    halo = jnp.where(seq_tile == 0, 0.0, uh_ref[...])
    uu = jnp.concatenate([halo, u], axis=0)
    a1 = uu[1:] + uu[:-1]
    a2 = a1[2:] + a1[:-2]
    a3 = a2[4:] + a2[:-4]
    a4 = a3[8:] + a3[:-8]
    lane_p = lax.broadcasted_iota(I32, (tm, POOL_WIDTH), 1)
    g0, g1, g2 = lane_p < 64, lane_p < 128, lane_p < 192
    wsum = jnp.where(g0, a1[15:], jnp.where(g1, a2[13:], jnp.where(g2, a3[9:], a4[1:])))
    wlen = jnp.where(g0, 2.0, jnp.where(g1, 4.0, jnp.where(g2, 8.0, 16.0)))
    tpos = seq_tile * tm + lax.broadcasted_iota(I32, (tm, POOL_WIDTH), 0) + 1
    cnt = jnp.minimum(tpos.astype(F32), wlen)
    pooled = wsum / cnt - u
    y_pool = jnp.dot(pooled.astype(BF16), pp_ref[...], preferred_element_type=F32) * ps_ref[...]

    lane = lax.broadcasted_iota(I32, (tm, LANES), 1)
    is_a = lane < HEAD_DIM
    lane_row = lax.broadcasted_iota(I32, (1, LANES), 1)
    mask_a = jnp.where(lane_row < HEAD_DIM, 1.0, 0.0).astype(BF16)
    mask_b = jnp.where(lane_row < HEAD_DIM, 0.0, 1.0).astype(BF16)
    y_mem = []
    for pr in range(MEM_WIDTH // LANES):
        cols = slice(pr * LANES, (pr + 1) * LANES)
        qp, kp, vp = qm_ref[:, cols], km_ref[:, cols], vm_ref[:, cols]
        outs = []
        for msk in (mask_a, mask_b):
            s = lax.dot_general(qp * msk, kp, (((1,), (1,)), ((), ())), preferred_element_type=F32)
            m = jnp.max(s, axis=-1, keepdims=True)
            p = jnp.exp(s - m)
            p = p / jnp.sum(p, axis=-1, keepdims=True)
            outs.append(jnp.dot(p.astype(BF16), vp, preferred_element_type=F32))
        y_mem.append(jnp.where(is_a, outs[0], outs[1]))

    a0, a1w = ATTN_WIDTH, ATTN_WIDTH + POOL_WIDTH
    proj = jnp.dot(ya_ref[...], wo_ref[:a0, :], preferred_element_type=F32)
    proj += jnp.dot(y_pool.astype(BF16), wo_ref[a0:a1w, :], preferred_element_type=F32)
    for pr in range(MEM_WIDTH // LANES):
        lo = a1w + pr * LANES
        proj += jnp.dot(y_mem[pr].astype(BF16), wo_ref[lo:lo + LANES, :], preferred_element_type=F32)
    x2 = x_ref[...] + proj
    x2_ref[...] = x2

    ms = jnp.mean(x2 * x2, axis=-1, keepdims=True)
    h2f = x2 * lax.rsqrt(ms + EPS) * fn_ref[...]
    for c in range(ROW_SUB):
        h2t_ref[pl.ds(c, tm, stride=ROW_SUB), :] = h2f[:, c * LANES:(c + 1) * LANES]
    h2 = h2f.astype(BF16)

    logits = jnp.dot(h2, wr_ref[...], preferred_element_type=F32) + br_ref[...]
    ninf = -jnp.inf
    is_g = lane < N_GROUPS
    lg = jnp.where(is_g, logits, ninf)
    mg = jnp.max(lg, axis=-1, keepdims=True)
    g_sel = jnp.min(jnp.where(lg == mg, lane, LANES), axis=-1, keepdims=True)
    w_g = 1.0 / jnp.sum(jnp.where(is_g, jnp.exp(logits - mg), 0.0), axis=-1, keepdims=True)
    in_grp = (lane >= EXPERT_LANE0) & (((lane - EXPERT_LANE0) >> 3) == g_sel)
    le = jnp.where(in_grp, logits, ninf)
    v1 = jnp.max(le, axis=-1, keepdims=True)
    i1 = jnp.min(jnp.where(le == v1, lane, 2 * LANES), axis=-1, keepdims=True)
    le2 = jnp.where(lane == i1, ninf, le)
    v2 = jnp.max(le2, axis=-1, keepdims=True)
    i2 = jnp.min(jnp.where(le2 == v2, lane, 2 * LANES), axis=-1, keepdims=True)
    e21 = jnp.exp(v2 - v1)
    gate1 = w_g / (1.0 + e21)
    gate2 = w_g * e21 / (1.0 + e21)

    hot1 = lane == i1
    hot2 = lane == i2
    onehot = jnp.where(hot1 | hot2, 1.0, 0.0)
    prefix = jnp.dot(ltri_ref[...], onehot.astype(BF16), preferred_element_type=F32)
    tot = prefix + base_ref[...]
    rank1 = jnp.sum(jnp.where(hot1, tot, 0.0), axis=-1, keepdims=True)
    rank2 = jnp.sum(jnp.where(hot2, tot, 0.0), axis=-1, keepdims=True)
    base_new = base_ref[...] + jnp.sum(onehot, axis=0, keepdims=True)
    base_ref[...] = base_new
    cnt_ref[...] = base_new

    e1 = (i1 - EXPERT_LANE0).astype(F32)
    e2 = (i2 - EXPERT_LANE0).astype(F32)
    rinfo = jnp.where(lane == 0, e1, jnp.where(lane == 1, e2, jnp.where(lane == 2, rank1, jnp.where(
        lane == 3, rank2, jnp.where(lane == 4, gate1, jnp.where(lane == 5, gate2, 0.0))))))
    rinfo_ref[...] = rinfo
    rinfot_ref[...] = jnp.transpose(rinfo)[:8, :]


def _mix(ya, u, qm, km, vm, x2d, pool_bd, pool_scale, w_out, ffn_norm, w_r, b_r, B, S):
    T = B * S
    tm = TM_MIX
    tps = S // tm
    hb = tm // HALO
    const = lambda i: (0, 0)
    ltri = (jnp.arange(tm)[:, None] > jnp.arange(tm)[None, :]).astype(BF16)
    return pl.pallas_call(
        functools.partial(_mix_kernel, tps),
        grid=(T // tm,),
        in_specs=[pl.BlockSpec((tm, ATTN_WIDTH), lambda i: (i, 0)),
                  pl.BlockSpec((tm, POOL_WIDTH), lambda i: (i, 0)),
                  pl.BlockSpec((HALO, POOL_WIDTH), lambda i: (jnp.maximum(i * hb - 1, 0), 0)),
                  pl.BlockSpec((tm, MEM_WIDTH), lambda i: (i, 0)),
                  pl.BlockSpec((None, N_MEM, MEM_WIDTH), lambda i: (i // tps, 0, 0)),
                  pl.BlockSpec((None, N_MEM, MEM_WIDTH), lambda i: (i // tps, 0, 0)),
                  pl.BlockSpec((tm, D_MODEL), lambda i: (i, 0)),
                  pl.BlockSpec((POOL_WIDTH, POOL_WIDTH), const),
                  pl.BlockSpec((1, POOL_WIDTH), const),
                  pl.BlockSpec((D_MODEL, D_MODEL), const),
                  pl.BlockSpec((1, D_MODEL), const),
                  pl.BlockSpec((D_MODEL, LANES), const),
                  pl.BlockSpec((1, LANES), const),
                  pl.BlockSpec((tm, tm), const)],
        out_specs=[pl.BlockSpec((tm, D_MODEL), lambda i: (i, 0)),
                   pl.BlockSpec((tm * ROW_SUB, LANES), lambda i: (i, 0)),
                   pl.BlockSpec((tm, LANES), lambda i: (i, 0)),
                   pl.BlockSpec((8, tm), lambda i: (0, i)),
                   pl.BlockSpec((1, LANES), const)],
        out_shape=[jax.ShapeDtypeStruct((T, D_MODEL), F32),
                   jax.ShapeDtypeStruct((T * ROW_SUB, LANES), F32),
                   jax.ShapeDtypeStruct((T, LANES), F32),
                   jax.ShapeDtypeStruct((8, T), F32),
                   jax.ShapeDtypeStruct((1, LANES), F32)],
        scratch_shapes=[pltpu.VMEM((1, LANES), F32)],
        compiler_params=_cparams(("arbitrary",)),
        name="mix_router",
    )(ya, u, u, qm, km, vm, x2d, pool_bd, pool_scale, w_out, ffn_norm, w_r, b_r, ltri)


def _dest_kernel(rt_ref, ps_ref, d_ref):
    tn = rt_ref.shape[1]
    sub = lax.broadcasted_iota(I32, (N_EXPERTS, tn), 0).astype(F32)
    ps = ps_ref[...]
    rows = []
    for k in range(2):
        e = rt_ref[k:k + 1, :]
        start = jnp.sum(jnp.where(sub == e, ps, 0.0), axis=0, keepdims=True)
        rows.append(start + rt_ref[2 + k:3 + k, :])
    d = jnp.concatenate(rows + [jnp.zeros((6, tn), F32)], axis=0)
    d_ref[...] = d.astype(I32)


def _dest(rinfot, pstart_col, T):
    tn = 2048
    return pl.pallas_call(
        _dest_kernel,
        grid=(T // tn,),
        in_specs=[pl.BlockSpec((8, tn), lambda i: (0, i)),
                  pl.BlockSpec((N_EXPERTS, 1), lambda i: (0, 0))],
        out_specs=pl.BlockSpec((8, tn), lambda i: (0, i)),
        out_shape=jax.ShapeDtypeStruct((8, T), I32),
        compiler_params=_cparams(("parallel",)),
        name="dest_rows",
    )(rinfot, pstart_col)


def _rowmap_kernel(d1_ref, d2_ref, fill_hbm, inv_ref, sem):
    T = d1_ref.shape[0]
    cp = pltpu.make_async_copy(fill_hbm, inv_ref, sem)
    cp.start()
    cp.wait()

    def body(t, carry):
        inv_ref[d1_ref[t]] = t
        inv_ref[d2_ref[t]] = T + t
        return carry

    lax.fori_loop(0, T, body, 0, unroll=8)


def _rowmap(d1, d2, R):
    return pl.pallas_call(
        _rowmap_kernel,
        grid_spec=pltpu.PrefetchScalarGridSpec(
            num_scalar_prefetch=2,
            grid=(1,),
            in_specs=[pl.BlockSpec(memory_space=pl.ANY)],
            out_specs=pl.BlockSpec(memory_space=pltpu.SMEM),
            scratch_shapes=[pltpu.SemaphoreType.DMA(())]),
        out_shape=jax.ShapeDtypeStruct((R,), I32),
        compiler_params=_cparams(("arbitrary",)),
        name="rowmap",
    )(d1, d2, jnp.zeros((R,), I32))


def _row_loop(n, body):
    shift = ROW_UNROLL.bit_length() - 1

    def group(g, carry):
        for u in range(ROW_UNROLL):
            body(g * ROW_UNROLL + u)
        return carry

    def rest(i, carry):
        body(((n >> shift) << shift) + i)
        return carry

    lax.fori_loop(0, n >> shift, group, 0)
    lax.fori_loop(0, n & (ROW_UNROLL - 1), rest, 0)


def _moe_kernel(be_ref, nu_ref, first_ref, nxt_ref, ws_ref, valid_ref, inv_ref,
                h_hbm, w1_hbm, w3_hbm, w2_hbm, o_hbm,
                xbuf, ybuf, w1s, w3s, w2s, w1b, w3b, w2b, gsem, ssem, wsem):
    j = pl.program_id(0)
    nu = nu_ref[0]
    T = h_hbm.shape[0] // ROW_SUB
    tr = TR_MOE

    def tile_rows(ref, row, n=1):
        return ref.at[pl.ds(pl.multiple_of(row * ROW_SUB, ROW_SUB), n * ROW_SUB)]

    def gather_row(tile, slot, i):
        v = inv_ref[tile * tr + i]
        tok = (v & (T - 1)) if T & (T - 1) == 0 else jnp.where(v >= T, v - T, v)
        return pltpu.make_async_copy(tile_rows(h_hbm, tok), tile_rows(xbuf, slot * tr + i), gsem.at[slot])

    def wait_rows(n, done):
        @pl.when(n == tr)
        def _():
            done(tr).wait()

        @pl.when(n != tr)
        def _():
            lax.fori_loop(0, n >> 3, lambda g, c: (done(8).wait(), c)[1], 0)
            lax.fori_loop(0, n & 7, lambda g, c: (done(1).wait(), c)[1], 0)

    def gather_wait(slot, n):
        wait_rows(n, lambda rows: pltpu.make_async_copy(
            tile_rows(h_hbm, 0, rows), tile_rows(xbuf, slot * tr, rows), gsem.at[slot]))

    def scatter_row(tile, slot, i):
        v = inv_ref[tile * tr + i]
        return pltpu.make_async_copy(tile_rows(ybuf, slot * tr + i), tile_rows(o_hbm, v), ssem.at[slot])

    def scatter_wait(slot, n):
        wait_rows(n, lambda rows: pltpu.make_async_copy(
            tile_rows(ybuf, slot * tr, rows), tile_rows(o_hbm, 0, rows), ssem.at[slot]))

    def weight_copies(e, slot):
        return [pltpu.make_async_copy(src.at[e], dst.at[slot], wsem.at[slot])
                for src, dst in ((w1_hbm, w1s), (w3_hbm, w3s), (w2_hbm, w2s))]

    @pl.when(j < nu)
    def _():
        s = j & 1
        e = be_ref[j]

        @pl.when(j == 0)
        def _():
            xbuf[...] = jnp.zeros_like(xbuf)
            _row_loop(valid_ref[0], lambda i: gather_row(0, 0, i).start())
            for cp in weight_copies(e, 0):
                cp.start()

        gather_wait(s, valid_ref[j])

        @pl.when(j + 1 < nu)
        def _():
            _row_loop(valid_ref[j + 1], lambda i: gather_row(j + 1, 1 - s, i).start())

        @pl.when(first_ref[j] == 1)
        def _():
            wslot = ws_ref[j]
            for cp in weight_copies(e, wslot):
                cp.wait()
            e_next = nxt_ref[j]

            @pl.when(e_next >= 0)
            def _():
                for cp in weight_copies(e_next, 1 - wslot):
                    cp.start()

            w1b[...] = w1s[wslot].astype(BF16)
            w3b[...] = w3s[wslot].astype(BF16)
            w2b[...] = w2s[wslot].astype(BF16)

        @pl.when(j >= 2)
        def _():
            scatter_wait(s, valid_ref[j - 2])

        base = pl.multiple_of(s * (tr * ROW_SUB), tr * ROW_SUB)
        x = jnp.concatenate([xbuf[pl.ds(base + c, tr, stride=ROW_SUB), :] for c in range(ROW_SUB)],
                            axis=1).astype(BF16)
        a = jnp.dot(x, w1b[...], preferred_element_type=F32)
        b = jnp.dot(x, w3b[...], preferred_element_type=F32)
        hmid = (a / (1.0 + jnp.exp(-a)) * b).astype(BF16)
        y = jnp.dot(hmid, w2b[...], preferred_element_type=F32)
        for c in range(ROW_SUB):
            ybuf[pl.ds(base + c, tr, stride=ROW_SUB), :] = y[:, c * LANES:(c + 1) * LANES]

        _row_loop(valid_ref[j], lambda i: scatter_row(j, s, i).start())

        @pl.when(j == nu - 1)
        def _():
            scatter_wait(s, valid_ref[j])

            @pl.when(j >= 1)
            def _():
                scatter_wait(1 - s, valid_ref[j - 1])


def _moe(blk_exp, n_used, first, nxt, wslot, valid, inv, h2t, w1, w3, w2):
    T = h2t.shape[0] // ROW_SUB
    tr = TR_MOE
    n_tiles = blk_exp.shape[0]
    any_spec = pl.BlockSpec(memory_space=pl.ANY)
    return pl.pallas_call(
        _moe_kernel,
        grid_spec=pltpu.PrefetchScalarGridSpec(
            num_scalar_prefetch=7,
            grid=(n_tiles,),
            in_specs=[any_spec] * 4,
            out_specs=any_spec,
            scratch_shapes=[pltpu.VMEM((2 * tr * ROW_SUB, LANES), F32),
                            pltpu.VMEM((2 * tr * ROW_SUB, LANES), F32),
                            pltpu.VMEM((2, D_MODEL, EXPERT_HIDDEN), F32),
                            pltpu.VMEM((2, D_MODEL, EXPERT_HIDDEN), F32),
                            pltpu.VMEM((2, EXPERT_HIDDEN, D_MODEL), F32),
                            pltpu.VMEM((D_MODEL, EXPERT_HIDDEN), BF16),
                            pltpu.VMEM((D_MODEL, EXPERT_HIDDEN), BF16),
                            pltpu.VMEM((EXPERT_HIDDEN, D_MODEL), BF16),
                            pltpu.SemaphoreType.DMA((2,)),
                            pltpu.SemaphoreType.DMA((2,)),
                            pltpu.SemaphoreType.DMA((2,))]),
        out_shape=jax.ShapeDtypeStruct((2 * T * ROW_SUB, LANES), F32),
        compiler_params=_cparams(("arbitrary",)),
        name="moe_experts",
    )(blk_exp, n_used, first, nxt, wslot, valid, inv, h2t, w1, w3, w2)


def _combine_kernel(x2_ref, rinfo_ref, y1_ref, y2_ref, o_ref):
    g1 = rinfo_ref[:, 4:5]
    g2 = rinfo_ref[:, 5:6]
    tc = x2_ref.shape[0]
    for c in range(ROW_SUB):
        cols = slice(c * LANES, (c + 1) * LANES)
        rows = pl.ds(c, tc, stride=ROW_SUB)
        o_ref[:, cols] = x2_ref[:, cols] + (g1 * y1_ref[rows, :] + g2 * y2_ref[rows, :])


def _combine(x2, rinfo, y2slot):
    T = x2.shape[0]
    tc = TC_COMBINE
    return pl.pallas_call(
        _combine_kernel,
        grid=(T // tc,),
        in_specs=[pl.BlockSpec((tc, D_MODEL), lambda i: (i, 0)),
                  pl.BlockSpec((tc, LANES), lambda i: (i, 0)),
                  pl.BlockSpec((tc * ROW_SUB, LANES), lambda i: (i, 0)),
                  pl.BlockSpec((tc * ROW_SUB, LANES), lambda i: (T // tc + i, 0))],
        out_specs=pl.BlockSpec((tc, D_MODEL), lambda i: (i, 0)),
        out_shape=jax.ShapeDtypeStruct((T, D_MODEL), F32),
        compiler_params=_cparams(("parallel",)),
        name="combine",
    )(x2, rinfo, y2slot, y2slot)


def _layer(x, mem, attn_norm, w_in, q_norm, k_norm, pool_proj, pool_scale, mem_norm, w_mem_kv,
           mq_norm, mk_norm, w_out, ffn_norm, w_group, b_group, w_router, b_router, w1, w3, w2):
    B, S, D = x.shape
    T = B * S
    assert D == D_MODEL and S % SUPER == 0 and T % TM_PROJ == 0
    x2d = x.reshape(T, D)
    row = lambda v: v.reshape(1, -1).astype(F32)
    scale = HEAD_DIM ** -0.5
    gq = row(jnp.tile(q_norm, ATTN_WIDTH // HEAD_DIM) * scale)
    gk = row(jnp.tile(k_norm, ATTN_WIDTH // HEAD_DIM))
    gmq = row(jnp.tile(mq_norm, MEM_WIDTH // HEAD_DIM) * scale)
    gmk = row(jnp.tile(mk_norm, MEM_WIDTH // HEAD_DIM))

    (q1, k1, v1, q4, k4, v4, q16, k16, v16, u, qm) = _proj(
        x2d, row(attn_norm), w_in.astype(BF16), gq, gk, gmq, B, S)
    ya = _attn(q1, k1, v1, q4, k4, v4, q16, k16, v16, B, S).reshape(T, ATTN_WIDTH)
    km, vm = _memkv(mem, row(mem_norm), w_mem_kv.astype(BF16), gmk, B)

    pool_bd = jax.scipy.linalg.block_diag(*[pool_proj[g] for g in range(pool_proj.shape[0])]).astype(BF16)
    w_r = jnp.zeros((D, LANES), F32)
    w_r = w_r.at[:, :N_GROUPS].set(w_group)
    w_r = w_r.at[:, EXPERT_LANE0:].set(jnp.transpose(w_router, (1, 0, 2)).reshape(D, N_EXPERTS))
    b_r = jnp.zeros((1, LANES), F32).at[0, :N_GROUPS].set(b_group).at[0, EXPERT_LANE0:].set(b_router.reshape(-1))
    x2, h2t, rinfo, rinfot, counts = _mix(ya, u, qm, km, vm, x2d, pool_bd, row(pool_scale),
                                          w_out.astype(BF16), row(ffn_norm), w_r.astype(BF16), b_r, B, S)

    R, pstart, sched = _tile_schedule(counts, T)
    dest = _dest(rinfot, pstart.astype(F32).reshape(N_EXPERTS, 1), T)
    inv = _rowmap(dest[0], dest[1], R)
    y2slot = _moe(*sched, inv, h2t, w1, w3, w2)
    out = _combine(x2, rinfo, y2slot)
    return out.reshape(B, S, D)


def _tile_schedule(counts, T):
    tr = TR_MOE
    R = 2 * T + N_EXPERTS * tr
    n_tiles = R // tr
    cnt = counts[0, EXPERT_LANE0:].astype(I32)
    padded = ((cnt + tr - 1) // tr) * tr
    pend = jnp.cumsum(padded)
    pstart = pend - padded
    n_used = (pend[-1] // tr).astype(I32).reshape(1)
    tiles = jnp.arange(n_tiles, dtype=I32)
    tile_row = jnp.minimum(tiles, n_used[0] - 1) * tr
    blk_exp = jnp.minimum(jnp.sum(tile_row[:, None] >= pend[None, :], axis=1), N_EXPERTS - 1).astype(I32)
    in_use = tiles < n_used[0]
    first = in_use & ((tiles == 0) | (blk_exp != jnp.roll(blk_exp, 1)))
    run_idx = jnp.cumsum(first.astype(I32)) - 1
    used = cnt > 0
    exp_of_run = jnp.argsort(jnp.logical_not(used), stable=True).astype(I32)
    nxt = jnp.where(run_idx + 1 < jnp.sum(used), exp_of_run[jnp.clip(run_idx + 1, 0, N_EXPERTS - 1)], -1)
    valid = jnp.where(in_use, jnp.clip(cnt[blk_exp] - (tiles * tr - pstart[blk_exp]), 0, tr), 0)
    sched = (blk_exp, n_used, first.astype(I32), nxt.astype(I32), (run_idx & 1).astype(I32), valid.astype(I32))
    return R, pstart, sched


def kernel(x, mem, attn_norm, w_in, q_norm, k_norm, pool_proj, pool_scale, mem_norm, w_mem_kv, mq_norm, mk_norm,
           w_out, ffn_norm, w_group, b_group, w_router, b_router, w1, w3, w2):
    for l in range(attn_norm.shape[0]):
        x = _layer(x, mem, attn_norm[l], w_in[l], q_norm[l], k_norm[l], pool_proj[l], pool_scale[l],
                   mem_norm[l], w_mem_kv[l], mq_norm[l], mk_norm[l], w_out[l], ffn_norm[l],
                   w_group[l], b_group[l], w_router[l], b_router[l], w1[l], w3[l], w2[l])
    return x
```

```python
import functools

import jax
import jax.numpy as jnp
from jax import lax
from jax.experimental import pallas as pl
from jax.experimental.pallas import tpu as pltpu

F32 = jnp.float32
BF16 = jnp.bfloat16
I32 = jnp.int32
U32 = jnp.uint32

D_MODEL = 1024
HEAD_DIM = 64
ATTN_WIDTH = 512
POOL_WIDTH = 256
MEM_WIDTH = 256
N_MEM = 256
IN_WIDTH = 3 * ATTN_WIDTH + POOL_WIDTH + MEM_WIDTH
N_GROUPS = 8
EXPERTS_PER_GROUP = 8
N_EXPERTS = 64
EXPERT_HIDDEN = 512
EPS = 1e-6
NEG_INF = -1e30

LANES = 128
N_PAIRS = ATTN_WIDTH // LANES
BLOCK = 128
SUPER = 16 * BLOCK
HALO = 16

ROW_SUB = D_MODEL // LANES
PACKED = D_MODEL // 2

TM_PROJ = 512
TM_MIX = 512
TR_MOE = 256
ROW_UNROLL = 16
TC_COMBINE = 256
EXPERT_LANE0 = 64

VMEM_LIMIT = 48 * 1024 * 1024
MOE_VMEM_LIMIT = 56 * 1024 * 1024


def _cparams(sem):
    return pltpu.CompilerParams(dimension_semantics=sem, vmem_limit_bytes=VMEM_LIMIT)


def _group_ones(n):
    g = jnp.arange(n) // HEAD_DIM
    return (g[:, None] == g[None, :]).astype(BF16)


def _head_norm(z, ones_ref, gain_ref):
    ss = jnp.dot((z * z).astype(BF16), ones_ref[...], preferred_element_type=F32)
    return z * lax.rsqrt(ss * (1.0 / HEAD_DIM) + EPS) * gain_ref[...]


def _proj_kernel(x_ref, an_ref, win_ref, gq_ref, gk_ref, gm_ref, o512_ref, o256_ref,
                 q1_ref, k1_ref, v1_ref, q4_ref, k4_ref, v4_ref, q16_ref, k16_ref, v16_ref,
                 u_ref, qm_ref, zs_ref):
    tm = x_ref.shape[0]
    x = x_ref[...]
    ms = jnp.mean(x * x, axis=-1, keepdims=True)
    h = (x * lax.rsqrt(ms + EPS) * an_ref[...]).astype(BF16)
    z = jnp.dot(h, win_ref[...], preferred_element_type=F32)
    a0, a1, a2, a3 = ATTN_WIDTH, 2 * ATTN_WIDTH, 3 * ATTN_WIDTH, 3 * ATTN_WIDTH + POOL_WIDTH
    qn = _head_norm(z[:, :a0], o512_ref, gq_ref)
    kn = _head_norm(z[:, a0:a1], o512_ref, gk_ref)
    v = z[:, a1:a2]
    u_ref[...] = z[:, a2:a3]
    qm_ref[...] = _head_norm(z[:, a3:], o256_ref, gm_ref).astype(BF16)
    for val, o1, o4, o16 in ((qn, q1_ref, q4_ref, q16_ref), (kn, k1_ref, k4_ref, k16_ref),
                             (v, v1_ref, v4_ref, v16_ref)):
        for hp in range(N_PAIRS):
            pair = val[:, hp * LANES:(hp + 1) * LANES]
            zs_ref[hp] = pair
            o1[hp] = pair.astype(BF16)
        for d, o in ((4, o4), (16, o16)):
            for r in range(d):
                for hp in range(N_PAIRS):
                    rows = zs_ref[hp, pl.ds(r, tm // d, stride=d), :]
                    o[hp, :, r * LANES:(r + 1) * LANES] = rows.astype(BF16)


def _proj(x2d, attn_norm, w_in, gq, gk, gm, B, S):
    T = B * S
    tm = TM_PROJ
    nj = S // tm
    const = lambda i: (0, 0)

    def lay(d):
        return jax.ShapeDtypeStruct((B, N_PAIRS, S // d, d * LANES), BF16)

    def lay_spec(d):
        return pl.BlockSpec((None, N_PAIRS, tm // d, d * LANES), lambda i: (i // nj, 0, i % nj, 0))

    out_shape = [lay(1)] * 3 + [lay(4)] * 3 + [lay(16)] * 3 + [
        jax.ShapeDtypeStruct((T, POOL_WIDTH), F32), jax.ShapeDtypeStruct((T, MEM_WIDTH), BF16)]
    out_specs = [lay_spec(1)] * 3 + [lay_spec(4)] * 3 + [lay_spec(16)] * 3 + [
        pl.BlockSpec((tm, POOL_WIDTH), lambda i: (i, 0)), pl.BlockSpec((tm, MEM_WIDTH), lambda i: (i, 0))]
    return pl.pallas_call(
        _proj_kernel,
        grid=(T // tm,),
        in_specs=[pl.BlockSpec((tm, D_MODEL), lambda i: (i, 0)),
                  pl.BlockSpec((1, D_MODEL), const),
                  pl.BlockSpec((D_MODEL, IN_WIDTH), const),
                  pl.BlockSpec((1, ATTN_WIDTH), const),
                  pl.BlockSpec((1, ATTN_WIDTH), const),
                  pl.BlockSpec((1, MEM_WIDTH), const),
                  pl.BlockSpec((ATTN_WIDTH, ATTN_WIDTH), const),
                  pl.BlockSpec((MEM_WIDTH, MEM_WIDTH), const)],
        out_specs=out_specs,
        out_shape=out_shape,
        scratch_shapes=[pltpu.VMEM((N_PAIRS, tm, LANES), F32)],
        compiler_params=_cparams(("parallel",)),
        name="proj",
    )(x2d, attn_norm, w_in, gq, gk, gm, _group_ones(ATTN_WIDTH), _group_ones(MEM_WIDTH))


def _attn_kernel(q1, k1c, k1p, v1c, v1p, q4, k4c, k4p, v4c, v4p, q16, k16c, k16p, v16c, v16p,
                 bias_ref, o_ref, acc, mrun, lrun):
    c = pl.program_id(2)
    lane = lax.broadcasted_iota(I32, (BLOCK, LANES), 1)
    is_a = lane < HEAD_DIM
    lane_row = lax.broadcasted_iota(I32, (1, LANES), 1)
    mask_a = jnp.where(lane_row < HEAD_DIM, 1.0, 0.0).astype(BF16)
    mask_b = jnp.where(lane_row < HEAD_DIM, 0.0, 1.0).astype(BF16)
    bias_full = bias_ref[0]
    bias_first = jnp.where(c > 0, bias_full, bias_ref[1])

    def tile(q_t, kp_t, kc_t, vp_t, vc_t, bias):
        lhs = jnp.concatenate([q_t * mask_a, q_t * mask_b], axis=0)
        keys = jnp.concatenate([kp_t, kc_t], axis=0)
        s = lax.dot_general(lhs, keys, (((1,), (1,)), ((), ())), preferred_element_type=F32) + bias
        m = jnp.max(s, axis=-1, keepdims=True)
        p = jnp.exp(s - m)
        l = jnp.sum(p, axis=-1, keepdims=True)
        vals = jnp.concatenate([vp_t, vc_t], axis=0)
        pv = jnp.dot(p.astype(BF16), vals, preferred_element_type=F32)
        o_t = jnp.where(is_a, pv[:BLOCK], pv[BLOCK:])
        m_t = jnp.where(is_a, m[:BLOCK], m[BLOCK:])
        l_t = jnp.where(is_a, l[:BLOCK], l[BLOCK:])
        return o_t, m_t, l_t

    def merge(rows, o_t, m_t, l_t):
        m_o = mrun[rows, :]
        m_n = jnp.maximum(m_o, m_t)
        a = jnp.exp(m_o - m_n)
        b = jnp.exp(m_t - m_n)
        acc[rows, :] = acc[rows, :] * a + o_t * b
        lrun[rows, :] = lrun[rows, :] * a + l_t * b
        mrun[rows, :] = m_n

    for jb in range(SUPER // BLOCK):
        cur = pl.ds(jb * BLOCK, BLOCK)
        if jb == 0:
            kp_t, vp_t, bias = k1p[...], v1p[...], bias_first
        else:
            prev = pl.ds((jb - 1) * BLOCK, BLOCK)
            kp_t, vp_t, bias = k1c[prev, :], v1c[prev, :], bias_full
        o_t, m_t, l_t = tile(q1[cur, :], kp_t, k1c[cur, :], vp_t, v1c[cur, :], bias)
        acc[cur, :] = o_t
        mrun[cur, :] = m_t
        lrun[cur, :] = l_t
    for d, q, kc, kp, vc, vp in ((4, q4, k4c, k4p, v4c, v4p), (16, q16, k16c, k16p, v16c, v16p)):
        nblk = SUPER // (BLOCK * d)
        for r in range(d):
            cols = pl.ds(r * LANES, LANES)
            for jb in range(nblk):
                cur = pl.ds(jb * BLOCK, BLOCK)
                if jb == 0:
                    kp_t, vp_t, bias = kp[:, cols], vp[:, cols], bias_first
                else:
                    prev = pl.ds((jb - 1) * BLOCK, BLOCK)
                    kp_t, vp_t, bias = kc[prev, cols], vc[prev, cols], bias_full
                o_t, m_t, l_t = tile(q[cur, cols], kp_t, kc[cur, cols], vp_t, vc[cur, cols], bias)
                merge(pl.ds(jb * BLOCK * d + r, BLOCK, stride=d), o_t, m_t, l_t)
    o_ref[...] = (acc[...] / lrun[...]).astype(BF16)


def _band_bias():
    qi = jnp.arange(BLOCK)[:, None]
    kj = jnp.arange(2 * BLOCK)[None, :]
    dist = qi + BLOCK - kj
    in_band = (dist >= 0) & (dist <= BLOCK)
    full = jnp.where(in_band, 0.0, NEG_INF).astype(F32)
    first = jnp.where(in_band & (kj >= BLOCK), 0.0, NEG_INF).astype(F32)
    return jnp.stack([jnp.tile(full, (2, 1)), jnp.tile(first, (2, 1))])


def _attn(q1, k1, v1, q4, k4, v4, q16, k16, v16, B, S):
    nsup = S // SUPER

    def specs(d):
        rows = SUPER // d
        per = rows // BLOCK
        cur = pl.BlockSpec((None, None, rows, d * LANES), lambda b, hp, c: (b, hp, c, 0))
        prev = pl.BlockSpec((None, None, BLOCK, d * LANES),
                            lambda b, hp, c: (b, hp, jnp.maximum(per * c - 1, 0), 0))
        return [cur, cur, prev, cur, prev]

    return pl.pallas_call(
        _attn_kernel,
        grid=(B, N_PAIRS, nsup),
        in_specs=specs(1) + specs(4) + specs(16) + [
            pl.BlockSpec((2, 2 * BLOCK, 2 * BLOCK), lambda b, hp, c: (0, 0, 0))],
        out_specs=pl.BlockSpec((None, SUPER, LANES), lambda b, hp, c: (b, c, hp)),
        out_shape=jax.ShapeDtypeStruct((B, S, ATTN_WIDTH), BF16),
        scratch_shapes=[pltpu.VMEM((SUPER, LANES), F32)] * 3,
        compiler_params=_cparams(("parallel", "parallel", "parallel")),
        name="dilated_attn",
    )(q1, k1, k1, v1, v1, q4, k4, k4, v4, v4, q16, k16, k16, v16, v16, _band_bias())


def _memkv_kernel(mem_ref, mn_ref, wkv_ref, gk_ref, o256_ref, km_ref, vm_ref):
    m = mem_ref[...]
    ms = jnp.mean(m * m, axis=-1, keepdims=True)
    mn = (m * lax.rsqrt(ms + EPS) * mn_ref[...]).astype(BF16)
    kv = jnp.dot(mn, wkv_ref[...], preferred_element_type=F32)
    km_ref[...] = _head_norm(kv[:, :MEM_WIDTH], o256_ref, gk_ref).astype(BF16)
    vm_ref[...] = kv[:, MEM_WIDTH:].astype(BF16)


def _memkv(mem, mem_norm, w_mem_kv, gmk, B):
    const = lambda b: (0, 0)
    return pl.pallas_call(
        _memkv_kernel,
        grid=(B,),
        in_specs=[pl.BlockSpec((None, N_MEM, D_MODEL), lambda b: (b, 0, 0)),
                  pl.BlockSpec((1, D_MODEL), const),
                  pl.BlockSpec((D_MODEL, 2 * MEM_WIDTH), const),
                  pl.BlockSpec((1, MEM_WIDTH), const),
                  pl.BlockSpec((MEM_WIDTH, MEM_WIDTH), const)],
        out_specs=[pl.BlockSpec((None, N_MEM, MEM_WIDTH), lambda b: (b, 0, 0))] * 2,
        out_shape=[jax.ShapeDtypeStruct((B, N_MEM, MEM_WIDTH), BF16)] * 2,
        compiler_params=_cparams(("parallel",)),
        name="memkv",
    )(mem, mem_norm, w_mem_kv, gmk, _group_ones(MEM_WIDTH))


def _mix_kernel(tiles_per_seq, ya_ref, u_ref, uh_ref, qm_ref, km_ref, vm_ref, x_ref, pp_ref, ps_ref,
                wo_ref, fn_ref, wr_ref, br_ref, ltri_ref,
                x2_ref, h2p_ref, rinfo_ref, rinfot_ref, cnt_ref, base_ref):
    i = pl.program_id(0)
    tm = x_ref.shape[0]
    seq_tile = i % tiles_per_seq

    @pl.when(i == 0)
    def _():
        base_ref[...] = jnp.zeros_like(base_ref)

    u = u_ref[...]
    halo = jnp.where(seq_tile == 0, 0.0, uh_ref[...])
    uu = jnp.concatenate([halo, u], axis=0)
    a1 = uu[1:] + uu[:-1]
    a2 = a1[2:] + a1[:-2]
    a3 = a2[4:] + a2[:-4]
    a4 = a3[8:] + a3[:-8]
    lane_p = lax.broadcasted_iota(I32, (tm, POOL_WIDTH), 1)
    g0, g1, g2 = lane_p < 64, lane_p < 128, lane_p < 192
    wsum = jnp.where(g0, a1[15:], jnp.where(g1, a2[13:], jnp.where(g2, a3[9:], a4[1:])))
    wlen = jnp.where(g0, 2.0, jnp.where(g1, 4.0, jnp.where(g2, 8.0, 16.0)))
    tpos = seq_tile * tm + lax.broadcasted_iota(I32, (tm, POOL_WIDTH), 0) + 1
    cnt = jnp.minimum(tpos.astype(F32), wlen)
    pooled = wsum / cnt - u
    y_pool = jnp.dot(pooled.astype(BF16), pp_ref[...], preferred_element_type=F32) * ps_ref[...]

    lane = lax.broadcasted_iota(I32, (tm, LANES), 1)
    is_a = lane < HEAD_DIM
    lane_row = lax.broadcasted_iota(I32, (1, LANES), 1)
    mask_a = jnp.where(lane_row < HEAD_DIM, 1.0, 0.0).astype(BF16)
    mask_b = jnp.where(lane_row < HEAD_DIM, 0.0, 1.0).astype(BF16)
    y_mem = []
    for pr in range(MEM_WIDTH // LANES):
        cols = slice(pr * LANES, (pr + 1) * LANES)
        qp, kp, vp = qm_ref[:, cols], km_ref[:, cols], vm_ref[:, cols]
        outs = []
        for msk in (mask_a, mask_b):
            s = lax.dot_general(qp * msk, kp, (((1,), (1,)), ((), ())), preferred_element_type=F32)
            m = jnp.max(s, axis=-1, keepdims=True)
            p = jnp.exp(s - m)
            p = p / jnp.sum(p, axis=-1, keepdims=True)
            outs.append(jnp.dot(p.astype(BF16), vp, preferred_element_type=F32))
        y_mem.append(jnp.where(is_a, outs[0], outs[1]))

    a0, a1w = ATTN_WIDTH, ATTN_WIDTH + POOL_WIDTH
    proj = jnp.dot(ya_ref[...], wo_ref[:a0, :], preferred_element_type=F32)
    proj += jnp.dot(y_pool.astype(BF16), wo_ref[a0:a1w, :], preferred_element_type=F32)
    for pr in range(MEM_WIDTH // LANES):
        lo = a1w + pr * LANES
        proj += jnp.dot(y_mem[pr].astype(BF16), wo_ref[lo:lo + LANES, :], preferred_element_type=F32)
    x2 = x_ref[...] + proj
    x2_ref[...] = x2

    ms = jnp.mean(x2 * x2, axis=-1, keepdims=True)
    h2 = (x2 * lax.rsqrt(ms + EPS) * fn_ref[...]).astype(BF16)
    bits = lax.bitcast_convert_type(h2.astype(F32), U32)
    h2p_ref[...] = bits[:, :PACKED] | (bits[:, PACKED:] >> 16)

    logits = jnp.dot(h2, wr_ref[...], preferred_element_type=F32) + br_ref[...]
    ninf = -jnp.inf
    is_g = lane < N_GROUPS
    lg = jnp.where(is_g, logits, ninf)
    mg = jnp.max(lg, axis=-1, keepdims=True)
    g_sel = jnp.min(jnp.where(lg == mg, lane, LANES), axis=-1, keepdims=True)
    w_g = 1.0 / jnp.sum(jnp.where(is_g, jnp.exp(logits - mg), 0.0), axis=-1, keepdims=True)
    in_grp = (lane >= EXPERT_LANE0) & (((lane - EXPERT_LANE0) >> 3) == g_sel)
    le = jnp.where(in_grp, logits, ninf)
    v1 = jnp.max(le, axis=-1, keepdims=True)
    i1 = jnp.min(jnp.where(le == v1, lane, 2 * LANES), axis=-1, keepdims=True)
    le2 = jnp.where(lane == i1, ninf, le)
    v2 = jnp.max(le2, axis=-1, keepdims=True)
    i2 = jnp.min(jnp.where(le2 == v2, lane, 2 * LANES), axis=-1, keepdims=True)
    e21 = jnp.exp(v2 - v1)
    gate1 = w_g / (1.0 + e21)
    gate2 = w_g * e21 / (1.0 + e21)

    hot1 = lane == i1
    hot2 = lane == i2
    onehot = jnp.where(hot1 | hot2, 1.0, 0.0)
    prefix = jnp.dot(ltri_ref[...], onehot.astype(BF16), preferred_element_type=F32)
    tot = prefix + base_ref[...]
    rank1 = jnp.sum(jnp.where(hot1, tot, 0.0), axis=-1, keepdims=True)
    rank2 = jnp.sum(jnp.where(hot2, tot, 0.0), axis=-1, keepdims=True)
    base_new = base_ref[...] + jnp.sum(onehot, axis=0, keepdims=True)
    base_ref[...] = base_new
    cnt_ref[...] = base_new

    e1 = (i1 - EXPERT_LANE0).astype(F32)
    e2 = (i2 - EXPERT_LANE0).astype(F32)
    rinfo = jnp.where(lane == 0, e1, jnp.where(lane == 1, e2, jnp.where(lane == 2, rank1, jnp.where(
        lane == 3, rank2, jnp.where(lane == 4, gate1, jnp.where(lane == 5, gate2, 0.0))))))
    rinfo_ref[...] = rinfo
    rinfot_ref[...] = jnp.transpose(rinfo)[:8, :]


def _mix(ya, u, qm, km, vm, x2d, pool_bd, pool_scale, w_out, ffn_norm, w_r, b_r, B, S):
    T = B * S
    tm = TM_MIX
    tps = S // tm
    hb = tm // HALO
    const = lambda i: (0, 0)
    ltri = (jnp.arange(tm)[:, None] > jnp.arange(tm)[None, :]).astype(BF16)
    return pl.pallas_call(
        functools.partial(_mix_kernel, tps),
        grid=(T // tm,),
        in_specs=[pl.BlockSpec((tm, ATTN_WIDTH), lambda i: (i, 0)),
                  pl.BlockSpec((tm, POOL_WIDTH), lambda i: (i, 0)),
                  pl.BlockSpec((HALO, POOL_WIDTH), lambda i: (jnp.maximum(i * hb - 1, 0), 0)),
                  pl.BlockSpec((tm, MEM_WIDTH), lambda i: (i, 0)),
                  pl.BlockSpec((None, N_MEM, MEM_WIDTH), lambda i: (i // tps, 0, 0)),
                  pl.BlockSpec((None, N_MEM, MEM_WIDTH), lambda i: (i // tps, 0, 0)),
                  pl.BlockSpec((tm, D_MODEL), lambda i: (i, 0)),
                  pl.BlockSpec((POOL_WIDTH, POOL_WIDTH), const),
                  pl.BlockSpec((1, POOL_WIDTH), const),
                  pl.BlockSpec((D_MODEL, D_MODEL), const),
                  pl.BlockSpec((1, D_MODEL), const),
                  pl.BlockSpec((D_MODEL, LANES), const),
                  pl.BlockSpec((1, LANES), const),
                  pl.BlockSpec((tm, tm), const)],
        out_specs=[pl.BlockSpec((tm, D_MODEL), lambda i: (i, 0)),
                   pl.BlockSpec((tm, PACKED), lambda i: (i, 0)),
                   pl.BlockSpec((tm, LANES), lambda i: (i, 0)),
                   pl.BlockSpec((8, tm), lambda i: (0, i)),
                   pl.BlockSpec((1, LANES), const)],
        out_shape=[jax.ShapeDtypeStruct((T, D_MODEL), F32),
                   jax.ShapeDtypeStruct((T, PACKED), U32),
                   jax.ShapeDtypeStruct((T, LANES), F32),
                   jax.ShapeDtypeStruct((8, T), F32),
                   jax.ShapeDtypeStruct((1, LANES), F32)],
        scratch_shapes=[pltpu.VMEM((1, LANES), F32)],
        compiler_params=_cparams(("arbitrary",)),
        name="mix_router",
    )(ya, u, u, qm, km, vm, x2d, pool_bd, pool_scale, w_out, ffn_norm, w_r, b_r, ltri)


def _dest_kernel(rt_ref, ps_ref, d_ref):
    tn = rt_ref.shape[1]
    sub = lax.broadcasted_iota(I32, (N_EXPERTS, tn), 0).astype(F32)
    ps = ps_ref[...]
    rows = []
    for k in range(2):
        e = rt_ref[k:k + 1, :]
        start = jnp.sum(jnp.where(sub == e, ps, 0.0), axis=0, keepdims=True)
        rows.append(start + rt_ref[2 + k:3 + k, :])
    d = jnp.concatenate(rows + [jnp.zeros((6, tn), F32)], axis=0)
    d_ref[...] = d.astype(I32)


def _dest(rinfot, pstart_col, T):
    tn = 2048
    return pl.pallas_call(
        _dest_kernel,
        grid=(T // tn,),
        in_specs=[pl.BlockSpec((8, tn), lambda i: (0, i)),
                  pl.BlockSpec((N_EXPERTS, 1), lambda i: (0, 0))],
        out_specs=pl.BlockSpec((8, tn), lambda i: (0, i)),
        out_shape=jax.ShapeDtypeStruct((8, T), I32),
        compiler_params=_cparams(("parallel",)),
        name="dest_rows",
    )(rinfot, pstart_col)


def _rowmap_kernel(d1_ref, d2_ref, fill_hbm, inv_ref, sem):
    T = d1_ref.shape[0]
    cp = pltpu.make_async_copy(fill_hbm, inv_ref, sem)
    cp.start()
    cp.wait()

    def body(t, carry):
        inv_ref[d1_ref[t]] = t
        inv_ref[d2_ref[t]] = T + t
        return carry

    lax.fori_loop(0, T, body, 0, unroll=8)


def _rowmap(d1, d2, R):
    return pl.pallas_call(
        _rowmap_kernel,
        grid_spec=pltpu.PrefetchScalarGridSpec(
            num_scalar_prefetch=2,
            grid=(1,),
            in_specs=[pl.BlockSpec(memory_space=pl.ANY)],
            out_specs=pl.BlockSpec(memory_space=pltpu.SMEM),
            scratch_shapes=[pltpu.SemaphoreType.DMA(())]),
        out_shape=jax.ShapeDtypeStruct((R,), I32),
        compiler_params=_cparams(("arbitrary",)),
        name="rowmap",
    )(d1, d2, jnp.zeros((R,), I32))


def _row_loop(n, body):
    shift = ROW_UNROLL.bit_length() - 1

    def group(g, carry):
        for u in range(ROW_UNROLL):
            body(g * ROW_UNROLL + u)
        return carry

    def rest(i, carry):
        body(((n >> shift) << shift) + i)
        return carry

    lax.fori_loop(0, n >> shift, group, 0)
    lax.fori_loop(0, n & (ROW_UNROLL - 1), rest, 0)


def _moe_kernel(be_ref, nu_ref, first_ref, nxt_ref, ws_ref, valid_ref, inv_ref,
                h_hbm, w1_hbm, w3_hbm, w2_hbm, o_hbm,
                tab, xbuf, ybuf, w1s, w3s, w2s, w1b, w3b, w2b, tsem, ssem, wsem):
    j = pl.program_id(0)
    nu = nu_ref[0]
    T = h_hbm.shape[0]
    tr = TR_MOE
    n_tiles = inv_ref.shape[0] // tr

    def tile_rows(ref, row, n=1):
        return ref.at[pl.ds(pl.multiple_of(row * ROW_SUB, ROW_SUB), n * ROW_SUB)]

    def gather_tile(tile, slot):
        for i in range(tr):
            v = inv_ref[tile * tr + i]
            tok = (v & (T - 1)) if T & (T - 1) == 0 else jnp.where(v >= T, v - T, v)
            xbuf[pl.ds(slot * tr + i, 1), :] = tab[pl.ds(tok, 1), :]

    def wait_rows(n, done):
        @pl.when(n == tr)
        def _():
            done(tr).wait()

        @pl.when(n != tr)
        def _():
            lax.fori_loop(0, n >> 3, lambda g, c: (done(8).wait(), c)[1], 0)
            lax.fori_loop(0, n & 7, lambda g, c: (done(1).wait(), c)[1], 0)

    def scatter_row(tile, slot, i):
        v = inv_ref[tile * tr + i]
        return pltpu.make_async_copy(tile_rows(ybuf, slot * tr + i), tile_rows(o_hbm, v), ssem.at[slot])

    def scatter_wait(slot, n):
        wait_rows(n, lambda rows: pltpu.make_async_copy(
            tile_rows(ybuf, slot * tr, rows), tile_rows(o_hbm, 0, rows), ssem.at[slot]))

    def weight_copies(e, slot):
        return [pltpu.make_async_copy(src.at[e], dst.at[slot], wsem.at[slot])
                for src, dst in ((w1_hbm, w1s), (w3_hbm, w3s), (w2_hbm, w2s))]

    @pl.when(j < nu)
    def _():
        s = j & 1
        e = be_ref[j]

        @pl.when(j == 0)
        def _():
            table = pltpu.make_async_copy(h_hbm, tab, tsem)
            table.start()
            for cp in weight_copies(e, 0):
                cp.start()
            table.wait()
            gather_tile(0, 0)

        @pl.when(first_ref[j] == 1)
        def _():
            wslot = ws_ref[j]
            for cp in weight_copies(e, wslot):
                cp.wait()
            e_next = nxt_ref[j]

            @pl.when(e_next >= 0)
            def _():
                for cp in weight_copies(e_next, 1 - wslot):
                    cp.start()

            w1b[...] = w1s[wslot].astype(BF16)
            w3b[...] = w3s[wslot].astype(BF16)
            w2b[...] = w2s[wslot].astype(BF16)

        @pl.when(j >= 2)
        def _():
            scatter_wait(s, valid_ref[j - 2])

        w = xbuf[pl.ds(pl.multiple_of(s * tr, tr), tr), :]
        lo = lax.bitcast_convert_type(w & jnp.uint32(0xFFFF0000), F32).astype(BF16)
        hi = lax.bitcast_convert_type(w << 16, F32).astype(BF16)
        gather_tile(jnp.minimum(j + 1, n_tiles - 1), 1 - s)
        a = jnp.dot(lo, w1b[:PACKED, :], preferred_element_type=F32)
        a += jnp.dot(hi, w1b[PACKED:, :], preferred_element_type=F32)
        b = jnp.dot(lo, w3b[:PACKED, :], preferred_element_type=F32)
        b += jnp.dot(hi, w3b[PACKED:, :], preferred_element_type=F32)
        hmid = (a / (1.0 + jnp.exp(-a)) * b).astype(BF16)
        y = jnp.dot(hmid, w2b[...], preferred_element_type=F32)
        base = pl.multiple_of(s * (tr * ROW_SUB), tr * ROW_SUB)
        for c in range(ROW_SUB):
            ybuf[pl.ds(base + c, tr, stride=ROW_SUB), :] = y[:, c * LANES:(c + 1) * LANES]

        _row_loop(valid_ref[j], lambda i: scatter_row(j, s, i).start())

        @pl.when(j == nu - 1)
        def _():
            scatter_wait(s, valid_ref[j])

            @pl.when(j >= 1)
            def _():
                scatter_wait(1 - s, valid_ref[j - 1])


def _moe(blk_exp, n_used, first, nxt, wslot, valid, inv, h2p, w1, w3, w2):
    T = h2p.shape[0]
    tr = TR_MOE
    n_tiles = blk_exp.shape[0]
    any_spec = pl.BlockSpec(memory_space=pl.ANY)
    return pl.pallas_call(
        _moe_kernel,
        grid_spec=pltpu.PrefetchScalarGridSpec(
            num_scalar_prefetch=7,
            grid=(n_tiles,),
            in_specs=[any_spec] * 4,
            out_specs=any_spec,
            scratch_shapes=[pltpu.VMEM((T, PACKED), U32),
                            pltpu.VMEM((2 * tr, PACKED), U32),
                            pltpu.VMEM((2 * tr * ROW_SUB, LANES), F32),
                            pltpu.VMEM((2, D_MODEL, EXPERT_HIDDEN), F32),
                            pltpu.VMEM((2, D_MODEL, EXPERT_HIDDEN), F32),
                            pltpu.VMEM((2, EXPERT_HIDDEN, D_MODEL), F32),
                            pltpu.VMEM((D_MODEL, EXPERT_HIDDEN), BF16),
                            pltpu.VMEM((D_MODEL, EXPERT_HIDDEN), BF16),
                            pltpu.VMEM((EXPERT_HIDDEN, D_MODEL), BF16),
                            pltpu.SemaphoreType.DMA(()),
                            pltpu.SemaphoreType.DMA((2,)),
                            pltpu.SemaphoreType.DMA((2,))]),
        out_shape=jax.ShapeDtypeStruct((2 * T * ROW_SUB, LANES), F32),
        compiler_params=pltpu.CompilerParams(dimension_semantics=("arbitrary",), vmem_limit_bytes=MOE_VMEM_LIMIT),
        name="moe_experts",
    )(blk_exp, n_used, first, nxt, wslot, valid, inv, h2p, w1, w3, w2)


def _combine_kernel(x2_ref, rinfo_ref, y1_ref, y2_ref, o_ref):
    g1 = rinfo_ref[:, 4:5]
    g2 = rinfo_ref[:, 5:6]
    tc = x2_ref.shape[0]
    for c in range(ROW_SUB):
        cols = slice(c * LANES, (c + 1) * LANES)
        rows = pl.ds(c, tc, stride=ROW_SUB)
        o_ref[:, cols] = x2_ref[:, cols] + (g1 * y1_ref[rows, :] + g2 * y2_ref[rows, :])


def _combine(x2, rinfo, y2slot):
    T = x2.shape[0]
    tc = TC_COMBINE
    return pl.pallas_call(
        _combine_kernel,
        grid=(T // tc,),
        in_specs=[pl.BlockSpec((tc, D_MODEL), lambda i: (i, 0)),
                  pl.BlockSpec((tc, LANES), lambda i: (i, 0)),
                  pl.BlockSpec((tc * ROW_SUB, LANES), lambda i: (i, 0)),
                  pl.BlockSpec((tc * ROW_SUB, LANES), lambda i: (T // tc + i, 0))],
        out_specs=pl.BlockSpec((tc, D_MODEL), lambda i: (i, 0)),
        out_shape=jax.ShapeDtypeStruct((T, D_MODEL), F32),
        compiler_params=_cparams(("parallel",)),
        name="combine",
    )(x2, rinfo, y2slot, y2slot)


def _layer(x, mem, attn_norm, w_in, q_norm, k_norm, pool_proj, pool_scale, mem_norm, w_mem_kv,
           mq_norm, mk_norm, w_out, ffn_norm, w_group, b_group, w_router, b_router, w1, w3, w2):
    B, S, D = x.shape
    T = B * S
    assert D == D_MODEL and S % SUPER == 0 and T % TM_PROJ == 0
    x2d = x.reshape(T, D)
    row = lambda v: v.reshape(1, -1).astype(F32)
    scale = HEAD_DIM ** -0.5
    gq = row(jnp.tile(q_norm, ATTN_WIDTH // HEAD_DIM) * scale)
    gk = row(jnp.tile(k_norm, ATTN_WIDTH // HEAD_DIM))
    gmq = row(jnp.tile(mq_norm, MEM_WIDTH // HEAD_DIM) * scale)
    gmk = row(jnp.tile(mk_norm, MEM_WIDTH // HEAD_DIM))

    (q1, k1, v1, q4, k4, v4, q16, k16, v16, u, qm) = _proj(
        x2d, row(attn_norm), w_in.astype(BF16), gq, gk, gmq, B, S)
    ya = _attn(q1, k1, v1, q4, k4, v4, q16, k16, v16, B, S).reshape(T, ATTN_WIDTH)
    km, vm = _memkv(mem, row(mem_norm), w_mem_kv.astype(BF16), gmk, B)

    pool_bd = jax.scipy.linalg.block_diag(*[pool_proj[g] for g in range(pool_proj.shape[0])]).astype(BF16)
    w_r = jnp.zeros((D, LANES), F32)
    w_r = w_r.at[:, :N_GROUPS].set(w_group)
    w_r = w_r.at[:, EXPERT_LANE0:].set(jnp.transpose(w_router, (1, 0, 2)).reshape(D, N_EXPERTS))
    b_r = jnp.zeros((1, LANES), F32).at[0, :N_GROUPS].set(b_group).at[0, EXPERT_LANE0:].set(b_router.reshape(-1))
    x2, h2p, rinfo, rinfot, counts = _mix(ya, u, qm, km, vm, x2d, pool_bd, row(pool_scale),
                                          w_out.astype(BF16), row(ffn_norm), w_r.astype(BF16), b_r, B, S)

    R, pstart, sched = _tile_schedule(counts, T)
    dest = _dest(rinfot, pstart.astype(F32).reshape(N_EXPERTS, 1), T)
    inv = _rowmap(dest[0], dest[1], R)
    y2slot = _moe(*sched, inv, h2p, w1, w3, w2)
    out = _combine(x2, rinfo, y2slot)
    return out.reshape(B, S, D)


def _tile_schedule(counts, T):
    tr = TR_MOE
    R = 2 * T + N_EXPERTS * tr
    n_tiles = R // tr
    cnt = counts[0, EXPERT_LANE0:].astype(I32)
    padded = ((cnt + tr - 1) // tr) * tr
    pend = jnp.cumsum(padded)
    pstart = pend - padded
    n_used = (pend[-1] // tr).astype(I32).reshape(1)
    tiles = jnp.arange(n_tiles, dtype=I32)
    tile_row = jnp.minimum(tiles, n_used[0] - 1) * tr
    blk_exp = jnp.minimum(jnp.sum(tile_row[:, None] >= pend[None, :], axis=1), N_EXPERTS - 1).astype(I32)
    in_use = tiles < n_used[0]
    first = in_use & ((tiles == 0) | (blk_exp != jnp.roll(blk_exp, 1)))
    run_idx = jnp.cumsum(first.astype(I32)) - 1
    used = cnt > 0
    exp_of_run = jnp.argsort(jnp.logical_not(used), stable=True).astype(I32)
    nxt = jnp.where(run_idx + 1 < jnp.sum(used), exp_of_run[jnp.clip(run_idx + 1, 0, N_EXPERTS - 1)], -1)
    valid = jnp.where(in_use, jnp.clip(cnt[blk_exp] - (tiles * tr - pstart[blk_exp]), 0, tr), 0)
    sched = (blk_exp, n_used, first.astype(I32), nxt.astype(I32), (run_idx & 1).astype(I32), valid.astype(I32))
    return R, pstart, sched


def kernel(x, mem, attn_norm, w_in, q_norm, k_norm, pool_proj, pool_scale, mem_norm, w_mem_kv, mq_norm, mk_norm,
           w_out, ffn_norm, w_group, b_group, w_router, b_router, w1, w3, w2):
    for l in range(attn_norm.shape[0]):
        x = _layer(x, mem, attn_norm[l], w_in[l], q_norm[l], k_norm[l], pool_proj[l], pool_scale[l],
                   mem_norm[l], w_mem_kv[l], mq_norm[l], mk_norm[l], w_out[l], ffn_norm[l],
                   w_group[l], b_group[l], w_router[l], b_router[l], w1[l], w3[l], w2[l])
    return x
```

```python
import functools

import jax
import jax.numpy as jnp
from jax import lax
from jax.experimental import pallas as pl
from jax.experimental.pallas import tpu as pltpu

F32 = jnp.float32
BF16 = jnp.bfloat16
I32 = jnp.int32
U32 = jnp.uint32

D_MODEL = 1024
HEAD_DIM = 64
ATTN_WIDTH = 512
POOL_WIDTH = 256
MEM_WIDTH = 256
N_MEM = 256
IN_WIDTH = 3 * ATTN_WIDTH + POOL_WIDTH + MEM_WIDTH
N_GROUPS = 8
EXPERTS_PER_GROUP = 8
N_EXPERTS = 64
EXPERT_HIDDEN = 512
EPS = 1e-6
NEG_INF = -1e30

LANES = 128
N_PAIRS = ATTN_WIDTH // LANES
BLOCK = 128
SUPER = 16 * BLOCK
HALO = 16

ROW_SUB = D_MODEL // LANES
PACKED = D_MODEL // 2

TM_PROJ = 512
TM_MIX = 512
TR_MOE = 256
ROW_UNROLL = 16
TC_COMBINE = 256
EXPERT_LANE0 = 64

VMEM_LIMIT = 48 * 1024 * 1024
MOE_VMEM_LIMIT = 56 * 1024 * 1024


def _cparams(sem):
    return pltpu.CompilerParams(dimension_semantics=sem, vmem_limit_bytes=VMEM_LIMIT)


def _group_ones(n):
    g = jnp.arange(n) // HEAD_DIM
    return (g[:, None] == g[None, :]).astype(BF16)


def _head_norm(z, ones_ref, gain_ref):
    ss = jnp.dot((z * z).astype(BF16), ones_ref[...], preferred_element_type=F32)
    return z * lax.rsqrt(ss * (1.0 / HEAD_DIM) + EPS) * gain_ref[...]


def _proj_kernel(x_ref, an_ref, win_ref, gq_ref, gk_ref, gm_ref, o512_ref, o256_ref,
                 q1_ref, k1_ref, v1_ref, q4_ref, k4_ref, v4_ref, q16_ref, k16_ref, v16_ref,
                 u_ref, qm_ref, zs_ref):
    tm = x_ref.shape[0]
    x = x_ref[...]
    ms = jnp.mean(x * x, axis=-1, keepdims=True)
    h = (x * lax.rsqrt(ms + EPS) * an_ref[...]).astype(BF16)
    z = jnp.dot(h, win_ref[...], preferred_element_type=F32)
    a0, a1, a2, a3 = ATTN_WIDTH, 2 * ATTN_WIDTH, 3 * ATTN_WIDTH, 3 * ATTN_WIDTH + POOL_WIDTH
    qn = _head_norm(z[:, :a0], o512_ref, gq_ref)
    kn = _head_norm(z[:, a0:a1], o512_ref, gk_ref)
    v = z[:, a1:a2]
    u_ref[...] = z[:, a2:a3]
    qm_ref[...] = _head_norm(z[:, a3:], o256_ref, gm_ref).astype(BF16)
    for val, o1, o4, o16 in ((qn, q1_ref, q4_ref, q16_ref), (kn, k1_ref, k4_ref, k16_ref),
                             (v, v1_ref, v4_ref, v16_ref)):
        for hp in range(N_PAIRS):
            pair = val[:, hp * LANES:(hp + 1) * LANES]
            zs_ref[hp] = pair
            o1[hp] = pair.astype(BF16)
        for d, o in ((4, o4), (16, o16)):
            for r in range(d):
                for hp in range(N_PAIRS):
                    rows = zs_ref[hp, pl.ds(r, tm // d, stride=d), :]
                    o[hp, :, r * LANES:(r + 1) * LANES] = rows.astype(BF16)


def _proj(x2d, attn_norm, w_in, gq, gk, gm, B, S):
    T = B * S
    tm = TM_PROJ
    nj = S // tm
    const = lambda i: (0, 0)

    def lay(d):
        return jax.ShapeDtypeStruct((B, N_PAIRS, S // d, d * LANES), BF16)

    def lay_spec(d):
        return pl.BlockSpec((None, N_PAIRS, tm // d, d * LANES), lambda i: (i // nj, 0, i % nj, 0))

    out_shape = [lay(1)] * 3 + [lay(4)] * 3 + [lay(16)] * 3 + [
        jax.ShapeDtypeStruct((T, POOL_WIDTH), F32), jax.ShapeDtypeStruct((T, MEM_WIDTH), BF16)]
    out_specs = [lay_spec(1)] * 3 + [lay_spec(4)] * 3 + [lay_spec(16)] * 3 + [
        pl.BlockSpec((tm, POOL_WIDTH), lambda i: (i, 0)), pl.BlockSpec((tm, MEM_WIDTH), lambda i: (i, 0))]
    return pl.pallas_call(
        _proj_kernel,
        grid=(T // tm,),
        in_specs=[pl.BlockSpec((tm, D_MODEL), lambda i: (i, 0)),
                  pl.BlockSpec((1, D_MODEL), const),
                  pl.BlockSpec((D_MODEL, IN_WIDTH), const),
                  pl.BlockSpec((1, ATTN_WIDTH), const),
                  pl.BlockSpec((1, ATTN_WIDTH), const),
                  pl.BlockSpec((1, MEM_WIDTH), const),
                  pl.BlockSpec((ATTN_WIDTH, ATTN_WIDTH), const),
                  pl.BlockSpec((MEM_WIDTH, MEM_WIDTH), const)],
        out_specs=out_specs,
        out_shape=out_shape,
        scratch_shapes=[pltpu.VMEM((N_PAIRS, tm, LANES), F32)],
        compiler_params=_cparams(("parallel",)),
        name="proj",
    )(x2d, attn_norm, w_in, gq, gk, gm, _group_ones(ATTN_WIDTH), _group_ones(MEM_WIDTH))


def _attn_kernel(q1, k1c, k1p, v1c, v1p, q4, k4c, k4p, v4c, v4p, q16, k16c, k16p, v16c, v16p,
                 bias_ref, o_ref, obuf, lbuf):
    c = pl.program_id(2)
    lane = lax.broadcasted_iota(I32, (BLOCK, LANES), 1)
    is_a = lane < HEAD_DIM
    lane_row = lax.broadcasted_iota(I32, (1, LANES), 1)
    mask_a = jnp.where(lane_row < HEAD_DIM, 1.0, 0.0).astype(BF16)
    mask_b = jnp.where(lane_row < HEAD_DIM, 0.0, 1.0).astype(BF16)
    bias_full = bias_ref[0]
    bias_first = jnp.where(c > 0, bias_full, bias_ref[1])

    def tile(q_t, kp_t, kc_t, vp_t, vc_t, bias):
        lhs = jnp.concatenate([q_t * mask_a, q_t * mask_b], axis=0)
        keys = jnp.concatenate([kp_t, kc_t], axis=0)
        s = lax.dot_general(lhs, keys, (((1,), (1,)), ((), ())), preferred_element_type=F32) + bias
        m = jnp.max(s, axis=-1, keepdims=True)
        p = jnp.exp(s - m)
        l = jnp.sum(p, axis=-1, keepdims=True)
        vals = jnp.concatenate([vp_t, vc_t], axis=0)
        pv = jnp.dot(p.astype(BF16), vals, preferred_element_type=F32) * (1.0 / l)
        lse = m + jnp.log(l)
        return jnp.where(is_a, pv[:BLOCK], pv[BLOCK:]), jnp.where(is_a, lse[:BLOCK], lse[BLOCK:])

    for jb in range(SUPER // BLOCK):
        cur = pl.ds(jb * BLOCK, BLOCK)
        if jb == 0:
            kp_t, vp_t, bias = k1p[...], v1p[...], bias_first
        else:
            prev = pl.ds((jb - 1) * BLOCK, BLOCK)
            kp_t, vp_t, bias = k1c[prev, :], v1c[prev, :], bias_full
        o_t, lse_t = tile(q1[cur, :], kp_t, k1c[cur, :], vp_t, v1c[cur, :], bias)
        obuf[0, cur, :] = o_t
        lbuf[0, cur, :] = lse_t
    for pat, (d, q, kc, kp, vc, vp) in enumerate(((4, q4, k4c, k4p, v4c, v4p), (16, q16, k16c, k16p, v16c, v16p)), 1):
        nblk = SUPER // (BLOCK * d)
        for r in range(d):
            cols = pl.ds(r * LANES, LANES)
            for jb in range(nblk):
                cur = pl.ds(jb * BLOCK, BLOCK)
                if jb == 0:
                    kp_t, vp_t, bias = kp[:, cols], vp[:, cols], bias_first
                else:
                    prev = pl.ds((jb - 1) * BLOCK, BLOCK)
                    kp_t, vp_t, bias = kc[prev, cols], vc[prev, cols], bias_full
                o_t, lse_t = tile(q[cur, cols], kp_t, kc[cur, cols], vp_t, vc[cur, cols], bias)
                rows = pl.ds(jb * BLOCK * d + r, BLOCK, stride=d)
                obuf[pat, rows, :] = o_t
                lbuf[pat, rows, :] = lse_t
    for jb in range(SUPER // BLOCK):
        cur = pl.ds(jb * BLOCK, BLOCK)
        l0, l1, l2 = lbuf[0, cur, :], lbuf[1, cur, :], lbuf[2, cur, :]
        top = jnp.maximum(jnp.maximum(l0, l1), l2)
        w0, w1, w2 = jnp.exp(l0 - top), jnp.exp(l1 - top), jnp.exp(l2 - top)
        mixed = (w0 * obuf[0, cur, :] + w1 * obuf[1, cur, :] + w2 * obuf[2, cur, :]) * (1.0 / (w0 + w1 + w2))
        o_ref[cur, :] = mixed.astype(BF16)


def _band_bias():
    qi = jnp.arange(BLOCK)[:, None]
    kj = jnp.arange(2 * BLOCK)[None, :]
    dist = qi + BLOCK - kj
    in_band = (dist >= 0) & (dist <= BLOCK)
    full = jnp.where(in_band, 0.0, NEG_INF).astype(F32)
    first = jnp.where(in_band & (kj >= BLOCK), 0.0, NEG_INF).astype(F32)
    return jnp.stack([jnp.tile(full, (2, 1)), jnp.tile(first, (2, 1))])


def _attn(q1, k1, v1, q4, k4, v4, q16, k16, v16, B, S):
    nsup = S // SUPER

    def specs(d):
        rows = SUPER // d
        per = rows // BLOCK
        cur = pl.BlockSpec((None, None, rows, d * LANES), lambda b, hp, c: (b, hp, c, 0))
        prev = pl.BlockSpec((None, None, BLOCK, d * LANES),
                            lambda b, hp, c: (b, hp, jnp.maximum(per * c - 1, 0), 0))
        return [cur, cur, prev, cur, prev]

    return pl.pallas_call(
        _attn_kernel,
        grid=(B, N_PAIRS, nsup),
        in_specs=specs(1) + specs(4) + specs(16) + [
            pl.BlockSpec((2, 2 * BLOCK, 2 * BLOCK), lambda b, hp, c: (0, 0, 0))],
        out_specs=pl.BlockSpec((None, SUPER, LANES), lambda b, hp, c: (b, c, hp)),
        out_shape=jax.ShapeDtypeStruct((B, S, ATTN_WIDTH), BF16),
        scratch_shapes=[pltpu.VMEM((3, SUPER, LANES), F32)] * 2,
        compiler_params=_cparams(("parallel", "parallel", "parallel")),
        name="dilated_attn",
    )(q1, k1, k1, v1, v1, q4, k4, k4, v4, v4, q16, k16, k16, v16, v16, _band_bias())


def _memkv_kernel(mem_ref, mn_ref, wkv_ref, gk_ref, o256_ref, km_ref, vm_ref):
    m = mem_ref[...]
    ms = jnp.mean(m * m, axis=-1, keepdims=True)
    mn = (m * lax.rsqrt(ms + EPS) * mn_ref[...]).astype(BF16)
    kv = jnp.dot(mn, wkv_ref[...], preferred_element_type=F32)
    km_ref[...] = _head_norm(kv[:, :MEM_WIDTH], o256_ref, gk_ref).astype(BF16)
    vm_ref[...] = kv[:, MEM_WIDTH:].astype(BF16)


def _memkv(mem, mem_norm, w_mem_kv, gmk, B):
    const = lambda b: (0, 0)
    return pl.pallas_call(
        _memkv_kernel,
        grid=(B,),
        in_specs=[pl.BlockSpec((None, N_MEM, D_MODEL), lambda b: (b, 0, 0)),
                  pl.BlockSpec((1, D_MODEL), const),
                  pl.BlockSpec((D_MODEL, 2 * MEM_WIDTH), const),
                  pl.BlockSpec((1, MEM_WIDTH), const),
                  pl.BlockSpec((MEM_WIDTH, MEM_WIDTH), const)],
        out_specs=[pl.BlockSpec((None, N_MEM, MEM_WIDTH), lambda b: (b, 0, 0))] * 2,
        out_shape=[jax.ShapeDtypeStruct((B, N_MEM, MEM_WIDTH), BF16)] * 2,
        compiler_params=_cparams(("parallel",)),
        name="memkv",
    )(mem, mem_norm, w_mem_kv, gmk, _group_ones(MEM_WIDTH))


def _mix_kernel(tiles_per_seq, ya_ref, u_ref, uh_ref, qm_ref, km_ref, vm_ref, x_ref, pp_ref, ps_ref,
                wo_ref, fn_ref, wr_ref, br_ref, ltri_ref,
                x2_ref, h2p_ref, rinfo_ref, rinfot_ref, cnt_ref, base_ref):
    i = pl.program_id(0)
    tm = x_ref.shape[0]
    seq_tile = i % tiles_per_seq

    @pl.when(i == 0)
    def _():
        base_ref[...] = jnp.zeros_like(base_ref)

    u = u_ref[...]
    halo = jnp.where(seq_tile == 0, 0.0, uh_ref[...])
    uu = jnp.concatenate([halo, u], axis=0)
    a1 = uu[1:] + uu[:-1]
    a2 = a1[2:] + a1[:-2]
    a3 = a2[4:] + a2[:-4]
    a4 = a3[8:] + a3[:-8]
    lane_p = lax.broadcasted_iota(I32, (tm, POOL_WIDTH), 1)
    g0, g1, g2 = lane_p < 64, lane_p < 128, lane_p < 192
    wsum = jnp.where(g0, a1[15:], jnp.where(g1, a2[13:], jnp.where(g2, a3[9:], a4[1:])))
    wlen = jnp.where(g0, 2.0, jnp.where(g1, 4.0, jnp.where(g2, 8.0, 16.0)))
    tpos = seq_tile * tm + lax.broadcasted_iota(I32, (tm, POOL_WIDTH), 0) + 1
    cnt = jnp.minimum(tpos.astype(F32), wlen)
    pooled = wsum / cnt - u
    y_pool = jnp.dot(pooled.astype(BF16), pp_ref[...], preferred_element_type=F32) * ps_ref[...]

    lane = lax.broadcasted_iota(I32, (tm, LANES), 1)
    is_a = lane < HEAD_DIM
    lane_row = lax.broadcasted_iota(I32, (1, LANES), 1)
    mask_a = jnp.where(lane_row < HEAD_DIM, 1.0, 0.0).astype(BF16)
    mask_b = jnp.where(lane_row < HEAD_DIM, 0.0, 1.0).astype(BF16)
    y_mem = []
    for pr in range(MEM_WIDTH // LANES):
        cols = slice(pr * LANES, (pr + 1) * LANES)
        qp, kp, vp = qm_ref[:, cols], km_ref[:, cols], vm_ref[:, cols]
        outs = []
        for msk in (mask_a, mask_b):
            s = lax.dot_general(qp * msk, kp, (((1,), (1,)), ((), ())), preferred_element_type=F32)
            m = jnp.max(s, axis=-1, keepdims=True)
            p = jnp.exp(s - m)
            p = p / jnp.sum(p, axis=-1, keepdims=True)
            outs.append(jnp.dot(p.astype(BF16), vp, preferred_element_type=F32))
        y_mem.append(jnp.where(is_a, outs[0], outs[1]))

    a0, a1w = ATTN_WIDTH, ATTN_WIDTH + POOL_WIDTH
    proj = jnp.dot(ya_ref[...], wo_ref[:a0, :], preferred_element_type=F32)
    proj += jnp.dot(y_pool.astype(BF16), wo_ref[a0:a1w, :], preferred_element_type=F32)
    for pr in range(MEM_WIDTH // LANES):
        lo = a1w + pr * LANES
        proj += jnp.dot(y_mem[pr].astype(BF16), wo_ref[lo:lo + LANES, :], preferred_element_type=F32)
    x2 = x_ref[...] + proj
    x2_ref[...] = x2

    ms = jnp.mean(x2 * x2, axis=-1, keepdims=True)
    h2 = (x2 * lax.rsqrt(ms + EPS) * fn_ref[...]).astype(BF16)
    bits = lax.bitcast_convert_type(h2.astype(F32), U32)
    h2p_ref[...] = bits[:, :PACKED] | (bits[:, PACKED:] >> 16)

    logits = jnp.dot(h2, wr_ref[...], preferred_element_type=F32) + br_ref[...]
    ninf = -jnp.inf
    is_g = lane < N_GROUPS
    lg = jnp.where(is_g, logits, ninf)
    mg = jnp.max(lg, axis=-1, keepdims=True)
    g_sel = jnp.min(jnp.where(lg == mg, lane, LANES), axis=-1, keepdims=True)
    w_g = 1.0 / jnp.sum(jnp.where(is_g, jnp.exp(logits - mg), 0.0), axis=-1, keepdims=True)
    in_grp = (lane >= EXPERT_LANE0) & (((lane - EXPERT_LANE0) >> 3) == g_sel)
    le = jnp.where(in_grp, logits, ninf)
    v1 = jnp.max(le, axis=-1, keepdims=True)
    i1 = jnp.min(jnp.where(le == v1, lane, 2 * LANES), axis=-1, keepdims=True)
    le2 = jnp.where(lane == i1, ninf, le)
    v2 = jnp.max(le2, axis=-1, keepdims=True)
    i2 = jnp.min(jnp.where(le2 == v2, lane, 2 * LANES), axis=-1, keepdims=True)
    e21 = jnp.exp(v2 - v1)
    gate1 = w_g / (1.0 + e21)
    gate2 = w_g * e21 / (1.0 + e21)

    hot1 = lane == i1
    hot2 = lane == i2
    onehot = jnp.where(hot1 | hot2, 1.0, 0.0)
    prefix = jnp.dot(ltri_ref[...], onehot.astype(BF16), preferred_element_type=F32)
    tot = prefix + base_ref[...]
    rank1 = jnp.sum(jnp.where(hot1, tot, 0.0), axis=-1, keepdims=True)
    rank2 = jnp.sum(jnp.where(hot2, tot, 0.0), axis=-1, keepdims=True)
    base_new = base_ref[...] + jnp.sum(onehot, axis=0, keepdims=True)
    base_ref[...] = base_new
    cnt_ref[...] = base_new

    e1 = (i1 - EXPERT_LANE0).astype(F32)
    e2 = (i2 - EXPERT_LANE0).astype(F32)
    rinfo = jnp.where(lane == 0, e1, jnp.where(lane == 1, e2, jnp.where(lane == 2, rank1, jnp.where(
        lane == 3, rank2, jnp.where(lane == 4, gate1, jnp.where(lane == 5, gate2, 0.0))))))
    rinfo_ref[...] = rinfo
    rinfot_ref[...] = jnp.transpose(rinfo)[:8, :]


def _mix(ya, u, qm, km, vm, x2d, pool_bd, pool_scale, w_out, ffn_norm, w_r, b_r, B, S):
    T = B * S
    tm = TM_MIX
    tps = S // tm
    hb = tm // HALO
    const = lambda i: (0, 0)
    ltri = (jnp.arange(tm)[:, None] > jnp.arange(tm)[None, :]).astype(BF16)
    return pl.pallas_call(
        functools.partial(_mix_kernel, tps),
        grid=(T // tm,),
        in_specs=[pl.BlockSpec((tm, ATTN_WIDTH), lambda i: (i, 0)),
                  pl.BlockSpec((tm, POOL_WIDTH), lambda i: (i, 0)),
                  pl.BlockSpec((HALO, POOL_WIDTH), lambda i: (jnp.maximum(i * hb - 1, 0), 0)),
                  pl.BlockSpec((tm, MEM_WIDTH), lambda i: (i, 0)),
                  pl.BlockSpec((None, N_MEM, MEM_WIDTH), lambda i: (i // tps, 0, 0)),
                  pl.BlockSpec((None, N_MEM, MEM_WIDTH), lambda i: (i // tps, 0, 0)),
                  pl.BlockSpec((tm, D_MODEL), lambda i: (i, 0)),
                  pl.BlockSpec((POOL_WIDTH, POOL_WIDTH), const),
                  pl.BlockSpec((1, POOL_WIDTH), const),
                  pl.BlockSpec((D_MODEL, D_MODEL), const),
                  pl.BlockSpec((1, D_MODEL), const),
                  pl.BlockSpec((D_MODEL, LANES), const),
                  pl.BlockSpec((1, LANES), const),
                  pl.BlockSpec((tm, tm), const)],
        out_specs=[pl.BlockSpec((tm, D_MODEL), lambda i: (i, 0)),
                   pl.BlockSpec((tm, PACKED), lambda i: (i, 0)),
                   pl.BlockSpec((tm, LANES), lambda i: (i, 0)),
                   pl.BlockSpec((8, tm), lambda i: (0, i)),
                   pl.BlockSpec((1, LANES), const)],
        out_shape=[jax.ShapeDtypeStruct((T, D_MODEL), F32),
                   jax.ShapeDtypeStruct((T, PACKED), U32),
                   jax.ShapeDtypeStruct((T, LANES), F32),
                   jax.ShapeDtypeStruct((8, T), F32),
                   jax.ShapeDtypeStruct((1, LANES), F32)],
        scratch_shapes=[pltpu.VMEM((1, LANES), F32)],
        compiler_params=_cparams(("arbitrary",)),
        name="mix_router",
    )(ya, u, u, qm, km, vm, x2d, pool_bd, pool_scale, w_out, ffn_norm, w_r, b_r, ltri)


def _dest_kernel(rt_ref, ps_ref, d_ref):
    tn = rt_ref.shape[1]
    sub = lax.broadcasted_iota(I32, (N_EXPERTS, tn), 0).astype(F32)
    ps = ps_ref[...]
    rows = []
    for k in range(2):
        e = rt_ref[k:k + 1, :]
        start = jnp.sum(jnp.where(sub == e, ps, 0.0), axis=0, keepdims=True)
        rows.append(start + rt_ref[2 + k:3 + k, :])
    word = rows[0].astype(I32) | (rows[1].astype(I32) << 16)
    d_ref[...] = jnp.concatenate([word, jnp.zeros((7, tn), I32)], axis=0)


def _dest(rinfot, pstart_col, T):
    tn = 2048
    return pl.pallas_call(
        _dest_kernel,
        grid=(T // tn,),
        in_specs=[pl.BlockSpec((8, tn), lambda i: (0, i)),
                  pl.BlockSpec((N_EXPERTS, 1), lambda i: (0, 0))],
        out_specs=pl.BlockSpec((8, tn), lambda i: (0, i)),
        out_shape=jax.ShapeDtypeStruct((8, T), I32),
        compiler_params=_cparams(("parallel",)),
        name="dest_rows",
    )(rinfot, pstart_col)


def _rowmap_kernel(d_ref, fill_hbm, inv_ref, sem):
    T = d_ref.shape[0]
    cp = pltpu.make_async_copy(fill_hbm, inv_ref, sem)
    cp.start()
    cp.wait()
    unroll = 8

    def body(g, carry):
        t0 = g * unroll
        words = [d_ref[t0 + u] for u in range(unroll)]
        for u, w in enumerate(words):
            inv_ref[w & 0xFFFF] = t0 + u
            inv_ref[lax.shift_right_logical(w, 16)] = T + t0 + u
        return carry

    lax.fori_loop(0, T // unroll, body, 0)


def _rowmap(dword, R):
    assert R <= 1 << 16
    return pl.pallas_call(
        _rowmap_kernel,
        grid_spec=pltpu.PrefetchScalarGridSpec(
            num_scalar_prefetch=1,
            grid=(1,),
            in_specs=[pl.BlockSpec(memory_space=pl.ANY)],
            out_specs=pl.BlockSpec(memory_space=pltpu.SMEM),
            scratch_shapes=[pltpu.SemaphoreType.DMA(())]),
        out_shape=jax.ShapeDtypeStruct((R,), I32),
        compiler_params=_cparams(("arbitrary",)),
        name="rowmap",
    )(dword, jnp.zeros((R,), I32))


def _row_loop(n, body):
    shift = ROW_UNROLL.bit_length() - 1

    def group(g, carry):
        for u in range(ROW_UNROLL):
            body(g * ROW_UNROLL + u)
        return carry

    def rest(i, carry):
        body(((n >> shift) << shift) + i)
        return carry

    lax.fori_loop(0, n >> shift, group, 0)
    lax.fori_loop(0, n & (ROW_UNROLL - 1), rest, 0)


def _moe_kernel(be_ref, nu_ref, first_ref, nxt_ref, ws_ref, valid_ref, inv_ref,
                h_hbm, w1_hbm, w3_hbm, w2_hbm, o_hbm,
                tab, xbuf, ybuf, w1s, w3s, w2s, w1b, w3b, w2b, tsem, ssem, wsem):
    j = pl.program_id(0)
    nu = nu_ref[0]
    T = h_hbm.shape[0]
    tr = TR_MOE
    n_tiles = inv_ref.shape[0] // tr

    def tile_rows(ref, row, n=1):
        return ref.at[pl.ds(pl.multiple_of(row * ROW_SUB, ROW_SUB), n * ROW_SUB)]

    def gather_tile(tile, slot):
        for i in range(tr):
            v = inv_ref[tile * tr + i]
            tok = (v & (T - 1)) if T & (T - 1) == 0 else jnp.where(v >= T, v - T, v)
            xbuf[pl.ds(slot * tr + i, 1), :] = tab[pl.ds(tok, 1), :]

    def wait_rows(n, done):
        @pl.when(n == tr)
        def _():
            done(tr).wait()

        @pl.when(n != tr)
        def _():
            lax.fori_loop(0, n >> 3, lambda g, c: (done(8).wait(), c)[1], 0)
            lax.fori_loop(0, n & 7, lambda g, c: (done(1).wait(), c)[1], 0)

    def scatter_row(tile, slot, i):
        v = inv_ref[tile * tr + i]
        return pltpu.make_async_copy(tile_rows(ybuf, slot * tr + i), tile_rows(o_hbm, v), ssem.at[slot])

    def scatter_wait(slot, n):
        wait_rows(n, lambda rows: pltpu.make_async_copy(
            tile_rows(ybuf, slot * tr, rows), tile_rows(o_hbm, 0, rows), ssem.at[slot]))

    def weight_copies(e, slot):
        return [pltpu.make_async_copy(src.at[e], dst.at[slot], wsem.at[slot])
                for src, dst in ((w1_hbm, w1s), (w3_hbm, w3s), (w2_hbm, w2s))]

    @pl.when(j < nu)
    def _():
        s = j & 1
        e = be_ref[j]

        @pl.when(j == 0)
        def _():
            table = pltpu.make_async_copy(h_hbm, tab, tsem)
            table.start()
            for cp in weight_copies(e, 0):
                cp.start()
            table.wait()
            gather_tile(0, 0)

        @pl.when(first_ref[j] == 1)
        def _():
            wslot = ws_ref[j]
            for cp in weight_copies(e, wslot):
                cp.wait()
            e_next = nxt_ref[j]

            @pl.when(e_next >= 0)
            def _():
                for cp in weight_copies(e_next, 1 - wslot):
                    cp.start()

            w1b[...] = w1s[wslot].astype(BF16)
            w3b[...] = w3s[wslot].astype(BF16)
            w2b[...] = w2s[wslot].astype(BF16)

        @pl.when(j >= 2)
        def _():
            scatter_wait(s, valid_ref[j - 2])

        w = xbuf[pl.ds(pl.multiple_of(s * tr, tr), tr), :]
        lo = lax.bitcast_convert_type(w & jnp.uint32(0xFFFF0000), F32).astype(BF16)
        hi = lax.bitcast_convert_type(w << 16, F32).astype(BF16)
        gather_tile(jnp.minimum(j + 1, n_tiles - 1), 1 - s)
        a = jnp.dot(lo, w1b[:PACKED, :], preferred_element_type=F32)
        a += jnp.dot(hi, w1b[PACKED:, :], preferred_element_type=F32)
        b = jnp.dot(lo, w3b[:PACKED, :], preferred_element_type=F32)
        b += jnp.dot(hi, w3b[PACKED:, :], preferred_element_type=F32)
        hmid = (a / (1.0 + jnp.exp(-a)) * b).astype(BF16)
        y = jnp.dot(hmid, w2b[...], preferred_element_type=F32)
        base = pl.multiple_of(s * (tr * ROW_SUB), tr * ROW_SUB)
        for c in range(ROW_SUB):
            ybuf[pl.ds(base + c, tr, stride=ROW_SUB), :] = y[:, c * LANES:(c + 1) * LANES]

        _row_loop(valid_ref[j], lambda i: scatter_row(j, s, i).start())

        @pl.when(j == nu - 1)
        def _():
            scatter_wait(s, valid_ref[j])

            @pl.when(j >= 1)
            def _():
                scatter_wait(1 - s, valid_ref[j - 1])


def _moe(blk_exp, n_used, first, nxt, wslot, valid, inv, h2p, w1, w3, w2):
    T = h2p.shape[0]
    tr = TR_MOE
    n_tiles = blk_exp.shape[0]
    any_spec = pl.BlockSpec(memory_space=pl.ANY)
    return pl.pallas_call(
        _moe_kernel,
        grid_spec=pltpu.PrefetchScalarGridSpec(
            num_scalar_prefetch=7,
            grid=(n_tiles,),
            in_specs=[any_spec] * 4,
            out_specs=any_spec,
            scratch_shapes=[pltpu.VMEM((T, PACKED), U32),
                            pltpu.VMEM((2 * tr, PACKED), U32),
                            pltpu.VMEM((2 * tr * ROW_SUB, LANES), F32),
                            pltpu.VMEM((2, D_MODEL, EXPERT_HIDDEN), F32),
                            pltpu.VMEM((2, D_MODEL, EXPERT_HIDDEN), F32),
                            pltpu.VMEM((2, EXPERT_HIDDEN, D_MODEL), F32),
                            pltpu.VMEM((D_MODEL, EXPERT_HIDDEN), BF16),
                            pltpu.VMEM((D_MODEL, EXPERT_HIDDEN), BF16),
                            pltpu.VMEM((EXPERT_HIDDEN, D_MODEL), BF16),
                            pltpu.SemaphoreType.DMA(()),
                            pltpu.SemaphoreType.DMA((2,)),
                            pltpu.SemaphoreType.DMA((2,))]),
        out_shape=jax.ShapeDtypeStruct((2 * T * ROW_SUB, LANES), F32),
        compiler_params=pltpu.CompilerParams(dimension_semantics=("arbitrary",), vmem_limit_bytes=MOE_VMEM_LIMIT),
        name="moe_experts",
    )(blk_exp, n_used, first, nxt, wslot, valid, inv, h2p, w1, w3, w2)


def _combine_kernel(x2_ref, rinfo_ref, y1_ref, y2_ref, o_ref):
    g1 = rinfo_ref[:, 4:5]
    g2 = rinfo_ref[:, 5:6]
    tc = x2_ref.shape[0]
    for c in range(ROW_SUB):
        cols = slice(c * LANES, (c + 1) * LANES)
        rows = pl.ds(c, tc, stride=ROW_SUB)
        o_ref[:, cols] = x2_ref[:, cols] + (g1 * y1_ref[rows, :] + g2 * y2_ref[rows, :])


def _combine(x2, rinfo, y2slot):
    T = x2.shape[0]
    tc = TC_COMBINE
    return pl.pallas_call(
        _combine_kernel,
        grid=(T // tc,),
        in_specs=[pl.BlockSpec((tc, D_MODEL), lambda i: (i, 0)),
                  pl.BlockSpec((tc, LANES), lambda i: (i, 0)),
                  pl.BlockSpec((tc * ROW_SUB, LANES), lambda i: (i, 0)),
                  pl.BlockSpec((tc * ROW_SUB, LANES), lambda i: (T // tc + i, 0))],
        out_specs=pl.BlockSpec((tc, D_MODEL), lambda i: (i, 0)),
        out_shape=jax.ShapeDtypeStruct((T, D_MODEL), F32),
        compiler_params=_cparams(("parallel",)),
        name="combine",
    )(x2, rinfo, y2slot, y2slot)


def _layer(x, mem, attn_norm, w_in, q_norm, k_norm, pool_proj, pool_scale, mem_norm, w_mem_kv,
           mq_norm, mk_norm, w_out, ffn_norm, w_group, b_group, w_router, b_router, w1, w3, w2):
    B, S, D = x.shape
    T = B * S
    assert D == D_MODEL and S % SUPER == 0 and T % TM_PROJ == 0
    x2d = x.reshape(T, D)
    row = lambda v: v.reshape(1, -1).astype(F32)
    scale = HEAD_DIM ** -0.5
    gq = row(jnp.tile(q_norm, ATTN_WIDTH // HEAD_DIM) * scale)
    gk = row(jnp.tile(k_norm, ATTN_WIDTH // HEAD_DIM))
    gmq = row(jnp.tile(mq_norm, MEM_WIDTH // HEAD_DIM) * scale)
    gmk = row(jnp.tile(mk_norm, MEM_WIDTH // HEAD_DIM))

    (q1, k1, v1, q4, k4, v4, q16, k16, v16, u, qm) = _proj(
        x2d, row(attn_norm), w_in.astype(BF16), gq, gk, gmq, B, S)
    ya = _attn(q1, k1, v1, q4, k4, v4, q16, k16, v16, B, S).reshape(T, ATTN_WIDTH)
    km, vm = _memkv(mem, row(mem_norm), w_mem_kv.astype(BF16), gmk, B)

    pool_bd = jax.scipy.linalg.block_diag(*[pool_proj[g] for g in range(pool_proj.shape[0])]).astype(BF16)
    w_r = jnp.zeros((D, LANES), F32)
    w_r = w_r.at[:, :N_GROUPS].set(w_group)
    w_r = w_r.at[:, EXPERT_LANE0:].set(jnp.transpose(w_router, (1, 0, 2)).reshape(D, N_EXPERTS))
    b_r = jnp.zeros((1, LANES), F32).at[0, :N_GROUPS].set(b_group).at[0, EXPERT_LANE0:].set(b_router.reshape(-1))
    x2, h2p, rinfo, rinfot, counts = _mix(ya, u, qm, km, vm, x2d, pool_bd, row(pool_scale),
                                          w_out.astype(BF16), row(ffn_norm), w_r.astype(BF16), b_r, B, S)

    R, pstart, sched = _tile_schedule(counts, T)
    dest = _dest(rinfot, pstart.astype(F32).reshape(N_EXPERTS, 1), T)
    inv = _rowmap(dest[0], R)
    y2slot = _moe(*sched, inv, h2p, w1, w3, w2)
    out = _combine(x2, rinfo, y2slot)
    return out.reshape(B, S, D)


def _tile_schedule(counts, T):
    tr = TR_MOE
    R = 2 * T + N_EXPERTS * tr
    n_tiles = R // tr
    cnt = counts[0, EXPERT_LANE0:].astype(I32)
    padded = ((cnt + tr - 1) // tr) * tr
    pend = jnp.cumsum(padded)
    pstart = pend - padded
    n_used = (pend[-1] // tr).astype(I32).reshape(1)
    tiles = jnp.arange(n_tiles, dtype=I32)
    tile_row = jnp.minimum(tiles, n_used[0] - 1) * tr
    blk_exp = jnp.minimum(jnp.sum(tile_row[:, None] >= pend[None, :], axis=1), N_EXPERTS - 1).astype(I32)
    in_use = tiles < n_used[0]
    first = in_use & ((tiles == 0) | (blk_exp != jnp.roll(blk_exp, 1)))
    run_idx = jnp.cumsum(first.astype(I32)) - 1
    used = cnt > 0
    exp_of_run = jnp.argsort(jnp.logical_not(used), stable=True).astype(I32)
    nxt = jnp.where(run_idx + 1 < jnp.sum(used), exp_of_run[jnp.clip(run_idx + 1, 0, N_EXPERTS - 1)], -1)
    valid = jnp.where(in_use, jnp.clip(cnt[blk_exp] - (tiles * tr - pstart[blk_exp]), 0, tr), 0)
    sched = (blk_exp, n_used, first.astype(I32), nxt.astype(I32), (run_idx & 1).astype(I32), valid.astype(I32))
    return R, pstart, sched


def kernel(x, mem, attn_norm, w_in, q_norm, k_norm, pool_proj, pool_scale, mem_norm, w_mem_kv, mq_norm, mk_norm,
           w_out, ffn_norm, w_group, b_group, w_router, b_router, w1, w3, w2):
    for l in range(attn_norm.shape[0]):
        x = _layer(x, mem, attn_norm[l], w_in[l], q_norm[l], k_norm[l], pool_proj[l], pool_scale[l],
                   mem_norm[l], w_mem_kv[l], mq_norm[l], mk_norm[l], w_out[l], ffn_norm[l],
                   w_group[l], b_group[l], w_router[l], b_router[l], w1[l], w3[l], w2[l])
    return x
```

```python
import functools

import jax
import jax.numpy as jnp
from jax import lax
from jax.experimental import pallas as pl
from jax.experimental.pallas import tpu as pltpu

F32 = jnp.float32
BF16 = jnp.bfloat16
I32 = jnp.int32
U32 = jnp.uint32

D_MODEL = 1024
HEAD_DIM = 64
ATTN_WIDTH = 512
POOL_WIDTH = 256
MEM_WIDTH = 256
N_MEM = 256
IN_WIDTH = 3 * ATTN_WIDTH + POOL_WIDTH + MEM_WIDTH
N_GROUPS = 8
EXPERTS_PER_GROUP = 8
N_EXPERTS = 64
EXPERT_HIDDEN = 512
EPS = 1e-6
NEG_INF = -1e30

LANES = 128
N_PAIRS = ATTN_WIDTH // LANES
BLOCK = 128
SUPER = 16 * BLOCK
HALO = 16

ROW_SUB = D_MODEL // LANES
PACKED = D_MODEL // 2

TM_PROJ = 1024
TM_MIX = 512
TR_MOE = 256
ROW_UNROLL = 16
TC_COMBINE = 256
EXPERT_LANE0 = 64

VMEM_LIMIT = 48 * 1024 * 1024
MOE_VMEM_LIMIT = 56 * 1024 * 1024


def _cparams(sem):
    return pltpu.CompilerParams(dimension_semantics=sem, vmem_limit_bytes=VMEM_LIMIT)


def _group_ones(n):
    g = jnp.arange(n) // HEAD_DIM
    return (g[:, None] == g[None, :]).astype(BF16)


def _head_norm(z, ones_ref, gain_ref):
    ss = jnp.dot((z * z).astype(BF16), ones_ref[...], preferred_element_type=F32)
    return z * lax.rsqrt(ss * (1.0 / HEAD_DIM) + EPS) * gain_ref[...]


def _proj_kernel(x_ref, an_ref, win_ref, gq_ref, gk_ref, gm_ref, o512_ref, o256_ref,
                 q1_ref, k1_ref, v1_ref, q4_ref, k4_ref, v4_ref, q16_ref, k16_ref, v16_ref,
                 u_ref, qm_ref, zs_ref):
    tm = x_ref.shape[0]
    x = x_ref[...]
    ms = jnp.mean(x * x, axis=-1, keepdims=True)
    h = (x * lax.rsqrt(ms + EPS) * an_ref[...]).astype(BF16)
    a0, a1, a2, a3 = ATTN_WIDTH, 2 * ATTN_WIDTH, 3 * ATTN_WIDTH, 3 * ATTN_WIDTH + POOL_WIDTH

    def cols(lo, hi):
        return jnp.dot(h, win_ref[:, lo:hi], preferred_element_type=F32)

    groups = ((lambda: _head_norm(cols(0, a0), o512_ref, gq_ref), q1_ref, q4_ref, q16_ref),
              (lambda: _head_norm(cols(a0, a1), o512_ref, gk_ref), k1_ref, k4_ref, k16_ref),
              (lambda: cols(a1, a2), v1_ref, v4_ref, v16_ref))
    for make, o1, o4, o16 in groups:
        val = make()
        for hp in range(N_PAIRS):
            pair = val[:, hp * LANES:(hp + 1) * LANES]
            zs_ref[hp] = pair
            o1[hp] = pair.astype(BF16)
        for d, o in ((4, o4), (16, o16)):
            for r in range(d):
                for hp in range(N_PAIRS):
                    rows = zs_ref[hp, pl.ds(r, tm // d, stride=d), :]
                    o[hp, :, r * LANES:(r + 1) * LANES] = rows.astype(BF16)
    u_ref[...] = cols(a2, a3)
    qm_ref[...] = _head_norm(cols(a3, IN_WIDTH), o256_ref, gm_ref).astype(BF16)


def _proj(x2d, attn_norm, w_in, gq, gk, gm, B, S):
    T = B * S
    tm = TM_PROJ
    nj = S // tm
    const = lambda i: (0, 0)

    def lay(d):
        return jax.ShapeDtypeStruct((B, N_PAIRS, S // d, d * LANES), BF16)

    def lay_spec(d):
        return pl.BlockSpec((None, N_PAIRS, tm // d, d * LANES), lambda i: (i // nj, 0, i % nj, 0))

    out_shape = [lay(1)] * 3 + [lay(4)] * 3 + [lay(16)] * 3 + [
        jax.ShapeDtypeStruct((T, POOL_WIDTH), F32), jax.ShapeDtypeStruct((T, MEM_WIDTH), BF16)]
    out_specs = [lay_spec(1)] * 3 + [lay_spec(4)] * 3 + [lay_spec(16)] * 3 + [
        pl.BlockSpec((tm, POOL_WIDTH), lambda i: (i, 0)), pl.BlockSpec((tm, MEM_WIDTH), lambda i: (i, 0))]
    return pl.pallas_call(
        _proj_kernel,
        grid=(T // tm,),
        in_specs=[pl.BlockSpec((tm, D_MODEL), lambda i: (i, 0)),
                  pl.BlockSpec((1, D_MODEL), const),
                  pl.BlockSpec((D_MODEL, IN_WIDTH), const),
                  pl.BlockSpec((1, ATTN_WIDTH), const),
                  pl.BlockSpec((1, ATTN_WIDTH), const),
                  pl.BlockSpec((1, MEM_WIDTH), const),
                  pl.BlockSpec((ATTN_WIDTH, ATTN_WIDTH), const),
                  pl.BlockSpec((MEM_WIDTH, MEM_WIDTH), const)],
        out_specs=out_specs,
        out_shape=out_shape,
        scratch_shapes=[pltpu.VMEM((N_PAIRS, tm, LANES), F32)],
        compiler_params=_cparams(("parallel",)),
        name="proj",
    )(x2d, attn_norm, w_in, gq, gk, gm, _group_ones(ATTN_WIDTH), _group_ones(MEM_WIDTH))


def _attn_kernel(q1, k1c, k1p, v1c, v1p, q4, k4c, k4p, v4c, v4p, q16, k16c, k16p, v16c, v16p,
                 bias_ref, o_ref, obuf, lbuf):
    c = pl.program_id(2)
    lane = lax.broadcasted_iota(I32, (BLOCK, LANES), 1)
    is_a = lane < HEAD_DIM
    lane_row = lax.broadcasted_iota(I32, (1, LANES), 1)
    mask_a = jnp.where(lane_row < HEAD_DIM, 1.0, 0.0).astype(BF16)
    mask_b = jnp.where(lane_row < HEAD_DIM, 0.0, 1.0).astype(BF16)
    bias_full = bias_ref[0]
    bias_first = jnp.where(c > 0, bias_full, bias_ref[1])

    def tile(q_t, kp_t, kc_t, vp_t, vc_t, bias):
        lhs = jnp.concatenate([q_t * mask_a, q_t * mask_b], axis=0)
        keys = jnp.concatenate([kp_t, kc_t], axis=0)
        s = lax.dot_general(lhs, keys, (((1,), (1,)), ((), ())), preferred_element_type=F32) + bias
        m = jnp.max(s, axis=-1, keepdims=True)
        p = jnp.exp(s - m)
        l = jnp.sum(p, axis=-1, keepdims=True)
        vals = jnp.concatenate([vp_t, vc_t], axis=0)
        pv = jnp.dot(p.astype(BF16), vals, preferred_element_type=F32) * (1.0 / l)
        lse = m + jnp.log(l)
        return jnp.where(is_a, pv[:BLOCK], pv[BLOCK:]), jnp.where(is_a, lse[:BLOCK], lse[BLOCK:])

    for jb in range(SUPER // BLOCK):
        cur = pl.ds(jb * BLOCK, BLOCK)
        if jb == 0:
            kp_t, vp_t, bias = k1p[...], v1p[...], bias_first
        else:
            prev = pl.ds((jb - 1) * BLOCK, BLOCK)
            kp_t, vp_t, bias = k1c[prev, :], v1c[prev, :], bias_full
        o_t, lse_t = tile(q1[cur, :], kp_t, k1c[cur, :], vp_t, v1c[cur, :], bias)
        obuf[0, cur, :] = o_t
        lbuf[0, cur, :] = lse_t
    for pat, (d, q, kc, kp, vc, vp) in enumerate(((4, q4, k4c, k4p, v4c, v4p), (16, q16, k16c, k16p, v16c, v16p)), 1):
        nblk = SUPER // (BLOCK * d)
        for r in range(d):
            cols = pl.ds(r * LANES, LANES)
            for jb in range(nblk):
                cur = pl.ds(jb * BLOCK, BLOCK)
                if jb == 0:
                    kp_t, vp_t, bias = kp[:, cols], vp[:, cols], bias_first
                else:
                    prev = pl.ds((jb - 1) * BLOCK, BLOCK)
                    kp_t, vp_t, bias = kc[prev, cols], vc[prev, cols], bias_full
                o_t, lse_t = tile(q[cur, cols], kp_t, kc[cur, cols], vp_t, vc[cur, cols], bias)
                rows = pl.ds(jb * BLOCK * d + r, BLOCK, stride=d)
                obuf[pat, rows, :] = o_t
                lbuf[pat, rows, :] = lse_t
    for jb in range(SUPER // BLOCK):
        cur = pl.ds(jb * BLOCK, BLOCK)
        l0, l1, l2 = lbuf[0, cur, :], lbuf[1, cur, :], lbuf[2, cur, :]
        top = jnp.maximum(jnp.maximum(l0, l1), l2)
        w0, w1, w2 = jnp.exp(l0 - top), jnp.exp(l1 - top), jnp.exp(l2 - top)
        mixed = (w0 * obuf[0, cur, :] + w1 * obuf[1, cur, :] + w2 * obuf[2, cur, :]) * (1.0 / (w0 + w1 + w2))
        o_ref[cur, :] = mixed.astype(BF16)


def _band_bias():
    qi = jnp.arange(BLOCK)[:, None]
    kj = jnp.arange(2 * BLOCK)[None, :]
    dist = qi + BLOCK - kj
    in_band = (dist >= 0) & (dist <= BLOCK)
    full = jnp.where(in_band, 0.0, NEG_INF).astype(F32)
    first = jnp.where(in_band & (kj >= BLOCK), 0.0, NEG_INF).astype(F32)
    return jnp.stack([jnp.tile(full, (2, 1)), jnp.tile(first, (2, 1))])


def _attn(q1, k1, v1, q4, k4, v4, q16, k16, v16, B, S):
    nsup = S // SUPER

    def specs(d):
        rows = SUPER // d
        per = rows // BLOCK
        cur = pl.BlockSpec((None, None, rows, d * LANES), lambda b, hp, c: (b, hp, c, 0))
        prev = pl.BlockSpec((None, None, BLOCK, d * LANES),
                            lambda b, hp, c: (b, hp, jnp.maximum(per * c - 1, 0), 0))
        return [cur, cur, prev, cur, prev]

    return pl.pallas_call(
        _attn_kernel,
        grid=(B, N_PAIRS, nsup),
        in_specs=specs(1) + specs(4) + specs(16) + [
            pl.BlockSpec((2, 2 * BLOCK, 2 * BLOCK), lambda b, hp, c: (0, 0, 0))],
        out_specs=pl.BlockSpec((None, SUPER, LANES), lambda b, hp, c: (b, c, hp)),
        out_shape=jax.ShapeDtypeStruct((B, S, ATTN_WIDTH), BF16),
        scratch_shapes=[pltpu.VMEM((3, SUPER, LANES), F32)] * 2,
        compiler_params=_cparams(("parallel", "parallel", "parallel")),
        name="dilated_attn",
    )(q1, k1, k1, v1, v1, q4, k4, k4, v4, v4, q16, k16, k16, v16, v16, _band_bias())


def _memkv_kernel(mem_ref, mn_ref, wkv_ref, gk_ref, o256_ref, km_ref, vm_ref):
    m = mem_ref[...]
    ms = jnp.mean(m * m, axis=-1, keepdims=True)
    mn = (m * lax.rsqrt(ms + EPS) * mn_ref[...]).astype(BF16)
    kv = jnp.dot(mn, wkv_ref[...], preferred_element_type=F32)
    km_ref[...] = _head_norm(kv[:, :MEM_WIDTH], o256_ref, gk_ref).astype(BF16)
    vm_ref[...] = kv[:, MEM_WIDTH:].astype(BF16)


def _memkv(mem, mem_norm, w_mem_kv, gmk, B):
    const = lambda b: (0, 0)
    return pl.pallas_call(
        _memkv_kernel,
        grid=(B,),
        in_specs=[pl.BlockSpec((None, N_MEM, D_MODEL), lambda b: (b, 0, 0)),
                  pl.BlockSpec((1, D_MODEL), const),
                  pl.BlockSpec((D_MODEL, 2 * MEM_WIDTH), const),
                  pl.BlockSpec((1, MEM_WIDTH), const),
                  pl.BlockSpec((MEM_WIDTH, MEM_WIDTH), const)],
        out_specs=[pl.BlockSpec((None, N_MEM, MEM_WIDTH), lambda b: (b, 0, 0))] * 2,
        out_shape=[jax.ShapeDtypeStruct((B, N_MEM, MEM_WIDTH), BF16)] * 2,
        compiler_params=_cparams(("parallel",)),
        name="memkv",
    )(mem, mem_norm, w_mem_kv, gmk, _group_ones(MEM_WIDTH))


def _mix_kernel(tiles_per_seq, ya_ref, u_ref, uh_ref, qm_ref, km_ref, vm_ref, x_ref, pp_ref, ps_ref,
                wo_ref, fn_ref, wr_ref, br_ref, ltri_ref,
                x2_ref, h2p_ref, rinfo_ref, rinfot_ref, cnt_ref, base_ref):
    i = pl.program_id(0)
    tm = x_ref.shape[0]
    seq_tile = i % tiles_per_seq

    @pl.when(i == 0)
    def _():
        base_ref[...] = jnp.zeros_like(base_ref)

    u = u_ref[...]
    halo = jnp.where(seq_tile == 0, 0.0, uh_ref[...])
    uu = jnp.concatenate([halo, u], axis=0)
    a1 = uu[1:] + uu[:-1]
    a2 = a1[2:] + a1[:-2]
    a3 = a2[4:] + a2[:-4]
    a4 = a3[8:] + a3[:-8]
    lane_p = lax.broadcasted_iota(I32, (tm, POOL_WIDTH), 1)
    g0, g1, g2 = lane_p < 64, lane_p < 128, lane_p < 192
    wsum = jnp.where(g0, a1[15:], jnp.where(g1, a2[13:], jnp.where(g2, a3[9:], a4[1:])))
    wlen = jnp.where(g0, 2.0, jnp.where(g1, 4.0, jnp.where(g2, 8.0, 16.0)))
    tpos = seq_tile * tm + lax.broadcasted_iota(I32, (tm, POOL_WIDTH), 0) + 1
    cnt = jnp.minimum(tpos.astype(F32), wlen)
    pooled = wsum / cnt - u
    y_pool = jnp.dot(pooled.astype(BF16), pp_ref[...], preferred_element_type=F32) * ps_ref[...]

    lane = lax.broadcasted_iota(I32, (tm, LANES), 1)
    is_a = lane < HEAD_DIM
    lane_row = lax.broadcasted_iota(I32, (1, LANES), 1)
    mask_a = jnp.where(lane_row < HEAD_DIM, 1.0, 0.0).astype(BF16)
    mask_b = jnp.where(lane_row < HEAD_DIM, 0.0, 1.0).astype(BF16)
    y_mem = []
    for pr in range(MEM_WIDTH // LANES):
        cols = slice(pr * LANES, (pr + 1) * LANES)
        qp, kp, vp = qm_ref[:, cols], km_ref[:, cols], vm_ref[:, cols]
        outs = []
        for msk in (mask_a, mask_b):
            s = lax.dot_general(qp * msk, kp, (((1,), (1,)), ((), ())), preferred_element_type=F32)
            m = jnp.max(s, axis=-1, keepdims=True)
            p = jnp.exp(s - m)
            p = p / jnp.sum(p, axis=-1, keepdims=True)
            outs.append(jnp.dot(p.astype(BF16), vp, preferred_element_type=F32))
        y_mem.append(jnp.where(is_a, outs[0], outs[1]))

    a0, a1w = ATTN_WIDTH, ATTN_WIDTH + POOL_WIDTH
    proj = jnp.dot(ya_ref[...], wo_ref[:a0, :], preferred_element_type=F32)
    proj += jnp.dot(y_pool.astype(BF16), wo_ref[a0:a1w, :], preferred_element_type=F32)
    for pr in range(MEM_WIDTH // LANES):
        lo = a1w + pr * LANES
        proj += jnp.dot(y_mem[pr].astype(BF16), wo_ref[lo:lo + LANES, :], preferred_element_type=F32)
    x2 = x_ref[...] + proj
    x2_ref[...] = x2

    ms = jnp.mean(x2 * x2, axis=-1, keepdims=True)
    h2 = (x2 * lax.rsqrt(ms + EPS) * fn_ref[...]).astype(BF16)
    bits = lax.bitcast_convert_type(h2.astype(F32), U32)
    h2p_ref[...] = bits[:, :PACKED] | (bits[:, PACKED:] >> 16)

    logits = jnp.dot(h2, wr_ref[...], preferred_element_type=F32) + br_ref[...]
    ninf = -jnp.inf
    is_g = lane < N_GROUPS
    lg = jnp.where(is_g, logits, ninf)
    mg = jnp.max(lg, axis=-1, keepdims=True)
    g_sel = jnp.min(jnp.where(lg == mg, lane, LANES), axis=-1, keepdims=True)
    w_g = 1.0 / jnp.sum(jnp.where(is_g, jnp.exp(logits - mg), 0.0), axis=-1, keepdims=True)
    in_grp = (lane >= EXPERT_LANE0) & (((lane - EXPERT_LANE0) >> 3) == g_sel)
    le = jnp.where(in_grp, logits, ninf)
    v1 = jnp.max(le, axis=-1, keepdims=True)
    i1 = jnp.min(jnp.where(le == v1, lane, 2 * LANES), axis=-1, keepdims=True)
    le2 = jnp.where(lane == i1, ninf, le)
    v2 = jnp.max(le2, axis=-1, keepdims=True)
    i2 = jnp.min(jnp.where(le2 == v2, lane, 2 * LANES), axis=-1, keepdims=True)
    e21 = jnp.exp(v2 - v1)
    gate1 = w_g / (1.0 + e21)
    gate2 = w_g * e21 / (1.0 + e21)

    hot1 = lane == i1
    hot2 = lane == i2
    onehot = jnp.where(hot1 | hot2, 1.0, 0.0)
    prefix = jnp.dot(ltri_ref[...], onehot.astype(BF16), preferred_element_type=F32)
    tot = prefix + base_ref[...]
    rank1 = jnp.sum(jnp.where(hot1, tot, 0.0), axis=-1, keepdims=True)
    rank2 = jnp.sum(jnp.where(hot2, tot, 0.0), axis=-1, keepdims=True)
    base_new = base_ref[...] + jnp.sum(onehot, axis=0, keepdims=True)
    base_ref[...] = base_new
    cnt_ref[...] = base_new

    e1 = (i1 - EXPERT_LANE0).astype(F32)
    e2 = (i2 - EXPERT_LANE0).astype(F32)
    rinfo = jnp.where(lane == 0, e1, jnp.where(lane == 1, e2, jnp.where(lane == 2, rank1, jnp.where(
        lane == 3, rank2, jnp.where(lane == 4, gate1, jnp.where(lane == 5, gate2, 0.0))))))
    rinfo_ref[...] = rinfo
    rinfot_ref[...] = jnp.transpose(rinfo)[:8, :]


def _mix(ya, u, qm, km, vm, x2d, pool_bd, pool_scale, w_out, ffn_norm, w_r, b_r, B, S):
    T = B * S
    tm = TM_MIX
    tps = S // tm
    hb = tm // HALO
    const = lambda i: (0, 0)
    ltri = (jnp.arange(tm)[:, None] > jnp.arange(tm)[None, :]).astype(BF16)
    return pl.pallas_call(
        functools.partial(_mix_kernel, tps),
        grid=(T // tm,),
        in_specs=[pl.BlockSpec((tm, ATTN_WIDTH), lambda i: (i, 0)),
                  pl.BlockSpec((tm, POOL_WIDTH), lambda i: (i, 0)),
                  pl.BlockSpec((HALO, POOL_WIDTH), lambda i: (jnp.maximum(i * hb - 1, 0), 0)),
                  pl.BlockSpec((tm, MEM_WIDTH), lambda i: (i, 0)),
                  pl.BlockSpec((None, N_MEM, MEM_WIDTH), lambda i: (i // tps, 0, 0)),
                  pl.BlockSpec((None, N_MEM, MEM_WIDTH), lambda i: (i // tps, 0, 0)),
                  pl.BlockSpec((tm, D_MODEL), lambda i: (i, 0)),
                  pl.BlockSpec((POOL_WIDTH, POOL_WIDTH), const),
                  pl.BlockSpec((1, POOL_WIDTH), const),
                  pl.BlockSpec((D_MODEL, D_MODEL), const),
                  pl.BlockSpec((1, D_MODEL), const),
                  pl.BlockSpec((D_MODEL, LANES), const),
                  pl.BlockSpec((1, LANES), const),
                  pl.BlockSpec((tm, tm), const)],
        out_specs=[pl.BlockSpec((tm, D_MODEL), lambda i: (i, 0)),
                   pl.BlockSpec((tm, PACKED), lambda i: (i, 0)),
                   pl.BlockSpec((tm, LANES), lambda i: (i, 0)),
                   pl.BlockSpec((8, tm), lambda i: (0, i)),
                   pl.BlockSpec((1, LANES), const)],
        out_shape=[jax.ShapeDtypeStruct((T, D_MODEL), F32),
                   jax.ShapeDtypeStruct((T, PACKED), U32),
                   jax.ShapeDtypeStruct((T, LANES), F32),
                   jax.ShapeDtypeStruct((8, T), F32),
                   jax.ShapeDtypeStruct((1, LANES), F32)],
        scratch_shapes=[pltpu.VMEM((1, LANES), F32)],
        compiler_params=_cparams(("arbitrary",)),
        name="mix_router",
    )(ya, u, u, qm, km, vm, x2d, pool_bd, pool_scale, w_out, ffn_norm, w_r, b_r, ltri)


def _dest_kernel(rt_ref, ps_ref, d_ref):
    tn = rt_ref.shape[1]
    sub = lax.broadcasted_iota(I32, (N_EXPERTS, tn), 0).astype(F32)
    ps = ps_ref[...]
    rows = []
    for k in range(2):
        e = rt_ref[k:k + 1, :]
        start = jnp.sum(jnp.where(sub == e, ps, 0.0), axis=0, keepdims=True)
        rows.append(start + rt_ref[2 + k:3 + k, :])
    word = rows[0].astype(I32) | (rows[1].astype(I32) << 16)
    d_ref[...] = jnp.concatenate([word, jnp.zeros((7, tn), I32)], axis=0)


def _dest(rinfot, pstart_col, T):
    tn = 2048
    return pl.pallas_call(
        _dest_kernel,
        grid=(T // tn,),
        in_specs=[pl.BlockSpec((8, tn), lambda i: (0, i)),
                  pl.BlockSpec((N_EXPERTS, 1), lambda i: (0, 0))],
        out_specs=pl.BlockSpec((8, tn), lambda i: (0, i)),
        out_shape=jax.ShapeDtypeStruct((8, T), I32),
        compiler_params=_cparams(("parallel",)),
        name="dest_rows",
    )(rinfot, pstart_col)


def _rowmap_kernel(d_ref, fill_hbm, inv_ref, sem):
    T = d_ref.shape[0]
    cp = pltpu.make_async_copy(fill_hbm, inv_ref, sem)
    cp.start()
    cp.wait()
    unroll = 8

    def body(g, carry):
        t0 = g * unroll
        words = [d_ref[t0 + u] for u in range(unroll)]
        for u, w in enumerate(words):
            inv_ref[w & 0xFFFF] = t0 + u
            inv_ref[lax.shift_right_logical(w, 16)] = T + t0 + u
        return carry

    lax.fori_loop(0, T // unroll, body, 0)


def _rowmap(dword, R):
    assert R <= 1 << 16
    return pl.pallas_call(
        _rowmap_kernel,
        grid_spec=pltpu.PrefetchScalarGridSpec(
            num_scalar_prefetch=1,
            grid=(1,),
            in_specs=[pl.BlockSpec(memory_space=pl.ANY)],
            out_specs=pl.BlockSpec(memory_space=pltpu.SMEM),
            scratch_shapes=[pltpu.SemaphoreType.DMA(())]),
        out_shape=jax.ShapeDtypeStruct((R,), I32),
        compiler_params=_cparams(("arbitrary",)),
        name="rowmap",
    )(dword, jnp.zeros((R,), I32))


def _row_loop(n, body):
    shift = ROW_UNROLL.bit_length() - 1

    def group(g, carry):
        for u in range(ROW_UNROLL):
            body(g * ROW_UNROLL + u)
        return carry

    def rest(i, carry):
        body(((n >> shift) << shift) + i)
        return carry

    lax.fori_loop(0, n >> shift, group, 0)
    lax.fori_loop(0, n & (ROW_UNROLL - 1), rest, 0)


def _moe_kernel(be_ref, nu_ref, first_ref, nxt_ref, ws_ref, valid_ref, inv_ref,
                h_hbm, w1_hbm, w3_hbm, w2_hbm, o_hbm,
                tab, xbuf, ybuf, w1s, w3s, w2s, w1b, w3b, w2b, tsem, ssem, wsem):
    j = pl.program_id(0)
    nu = nu_ref[0]
    T = h_hbm.shape[0]
    tr = TR_MOE
    n_tiles = inv_ref.shape[0] // tr

    def tile_rows(ref, row, n=1):
        return ref.at[pl.ds(pl.multiple_of(row * ROW_SUB, ROW_SUB), n * ROW_SUB)]

    def gather_tile(tile, slot):
        for i in range(tr):
            v = inv_ref[tile * tr + i]
            tok = (v & (T - 1)) if T & (T - 1) == 0 else jnp.where(v >= T, v - T, v)
            xbuf[pl.ds(slot * tr + i, 1), :] = tab[pl.ds(tok, 1), :]

    def wait_rows(n, done):
        @pl.when(n == tr)
        def _():
            done(tr).wait()

        @pl.when(n != tr)
        def _():
            lax.fori_loop(0, n >> 3, lambda g, c: (done(8).wait(), c)[1], 0)
            lax.fori_loop(0, n & 7, lambda g, c: (done(1).wait(), c)[1], 0)

    def scatter_row(tile, slot, i):
        v = inv_ref[tile * tr + i]
        return pltpu.make_async_copy(tile_rows(ybuf, slot * tr + i), tile_rows(o_hbm, v), ssem.at[slot])

    def scatter_wait(slot, n):
        wait_rows(n, lambda rows: pltpu.make_async_copy(
            tile_rows(ybuf, slot * tr, rows), tile_rows(o_hbm, 0, rows), ssem.at[slot]))

    def weight_copies(e, slot):
        return [pltpu.make_async_copy(src.at[e], dst.at[slot], wsem.at[slot])
                for src, dst in ((w1_hbm, w1s), (w3_hbm, w3s), (w2_hbm, w2s))]

    @pl.when(j < nu)
    def _():
        s = j & 1
        e = be_ref[j]

        @pl.when(j == 0)
        def _():
            table = pltpu.make_async_copy(h_hbm, tab, tsem)
            table.start()
            for cp in weight_copies(e, 0):
                cp.start()
            table.wait()
            gather_tile(0, 0)

        @pl.when(first_ref[j] == 1)
        def _():
            wslot = ws_ref[j]
            for cp in weight_copies(e, wslot):
                cp.wait()
            e_next = nxt_ref[j]

            @pl.when(e_next >= 0)
            def _():
                for cp in weight_copies(e_next, 1 - wslot):
                    cp.start()

            w1b[...] = w1s[wslot].astype(BF16)
            w3b[...] = w3s[wslot].astype(BF16)
            w2b[...] = w2s[wslot].astype(BF16)

        @pl.when(j >= 2)
        def _():
            scatter_wait(s, valid_ref[j - 2])

        w = xbuf[pl.ds(pl.multiple_of(s * tr, tr), tr), :]
        lo = lax.bitcast_convert_type(w & jnp.uint32(0xFFFF0000), F32).astype(BF16)
        hi = lax.bitcast_convert_type(w << 16, F32).astype(BF16)
        gather_tile(jnp.minimum(j + 1, n_tiles - 1), 1 - s)
        halves = [slice(k * (EXPERT_HIDDEN // 2), (k + 1) * (EXPERT_HIDDEN // 2)) for k in range(2)]

        def up(wb, cols):
            return (jnp.dot(lo, wb[:PACKED, cols], preferred_element_type=F32)
                    + jnp.dot(hi, wb[PACKED:, cols], preferred_element_type=F32))

        ab = [(up(w1b, cols), up(w3b, cols)) for cols in halves]
        y = None
        for cols, (a, b) in zip(halves, ab):
            hmid = (a / (1.0 + jnp.exp(-a)) * b).astype(BF16)
            part = jnp.dot(hmid, w2b[cols, :], preferred_element_type=F32)
            y = part if y is None else y + part
        base = pl.multiple_of(s * (tr * ROW_SUB), tr * ROW_SUB)
        for c in range(ROW_SUB):
            ybuf[pl.ds(base + c, tr, stride=ROW_SUB), :] = y[:, c * LANES:(c + 1) * LANES]

        _row_loop(valid_ref[j], lambda i: scatter_row(j, s, i).start())

        @pl.when(j == nu - 1)
        def _():
            scatter_wait(s, valid_ref[j])

            @pl.when(j >= 1)
            def _():
                scatter_wait(1 - s, valid_ref[j - 1])


def _moe(blk_exp, n_used, first, nxt, wslot, valid, inv, h2p, w1, w3, w2):
    T = h2p.shape[0]
    tr = TR_MOE
    n_tiles = blk_exp.shape[0]
    any_spec = pl.BlockSpec(memory_space=pl.ANY)
    return pl.pallas_call(
        _moe_kernel,
        grid_spec=pltpu.PrefetchScalarGridSpec(
            num_scalar_prefetch=7,
            grid=(n_tiles,),
            in_specs=[any_spec] * 4,
            out_specs=any_spec,
            scratch_shapes=[pltpu.VMEM((T, PACKED), U32),
                            pltpu.VMEM((2 * tr, PACKED), U32),
                            pltpu.VMEM((2 * tr * ROW_SUB, LANES), F32),
                            pltpu.VMEM((2, D_MODEL, EXPERT_HIDDEN), F32),
                            pltpu.VMEM((2, D_MODEL, EXPERT_HIDDEN), F32),
                            pltpu.VMEM((2, EXPERT_HIDDEN, D_MODEL), F32),
                            pltpu.VMEM((D_MODEL, EXPERT_HIDDEN), BF16),
                            pltpu.VMEM((D_MODEL, EXPERT_HIDDEN), BF16),
                            pltpu.VMEM((EXPERT_HIDDEN, D_MODEL), BF16),
                            pltpu.SemaphoreType.DMA(()),
                            pltpu.SemaphoreType.DMA((2,)),
                            pltpu.SemaphoreType.DMA((2,))]),
        out_shape=jax.ShapeDtypeStruct((2 * T * ROW_SUB, LANES), F32),
        compiler_params=pltpu.CompilerParams(dimension_semantics=("arbitrary",), vmem_limit_bytes=MOE_VMEM_LIMIT),
        name="moe_experts",
    )(blk_exp, n_used, first, nxt, wslot, valid, inv, h2p, w1, w3, w2)


def _combine_kernel(x2_ref, rinfo_ref, y1_ref, y2_ref, o_ref):
    g1 = rinfo_ref[:, 4:5]
    g2 = rinfo_ref[:, 5:6]
    tc = x2_ref.shape[0]
    for c in range(ROW_SUB):
        cols = slice(c * LANES, (c + 1) * LANES)
        rows = pl.ds(c, tc, stride=ROW_SUB)
        o_ref[:, cols] = x2_ref[:, cols] + (g1 * y1_ref[rows, :] + g2 * y2_ref[rows, :])


def _combine(x2, rinfo, y2slot):
    T = x2.shape[0]
    tc = TC_COMBINE
    return pl.pallas_call(
        _combine_kernel,
        grid=(T // tc,),
        in_specs=[pl.BlockSpec((tc, D_MODEL), lambda i: (i, 0)),
                  pl.BlockSpec((tc, LANES), lambda i: (i, 0)),
                  pl.BlockSpec((tc * ROW_SUB, LANES), lambda i: (i, 0)),
                  pl.BlockSpec((tc * ROW_SUB, LANES), lambda i: (T // tc + i, 0))],
        out_specs=pl.BlockSpec((tc, D_MODEL), lambda i: (i, 0)),
        out_shape=jax.ShapeDtypeStruct((T, D_MODEL), F32),
        compiler_params=_cparams(("parallel",)),
        name="combine",
    )(x2, rinfo, y2slot, y2slot)


def _layer(x, mem, attn_norm, w_in, q_norm, k_norm, pool_proj, pool_scale, mem_norm, w_mem_kv,
           mq_norm, mk_norm, w_out, ffn_norm, w_group, b_group, w_router, b_router, w1, w3, w2):
    B, S, D = x.shape
    T = B * S
    assert D == D_MODEL and S % SUPER == 0 and T % TM_PROJ == 0
    x2d = x.reshape(T, D)
    row = lambda v: v.reshape(1, -1).astype(F32)
    scale = HEAD_DIM ** -0.5
    gq = row(jnp.tile(q_norm, ATTN_WIDTH // HEAD_DIM) * scale)
    gk = row(jnp.tile(k_norm, ATTN_WIDTH // HEAD_DIM))
    gmq = row(jnp.tile(mq_norm, MEM_WIDTH // HEAD_DIM) * scale)
    gmk = row(jnp.tile(mk_norm, MEM_WIDTH // HEAD_DIM))

    (q1, k1, v1, q4, k4, v4, q16, k16, v16, u, qm) = _proj(
        x2d, row(attn_norm), w_in.astype(BF16), gq, gk, gmq, B, S)
    ya = _attn(q1, k1, v1, q4, k4, v4, q16, k16, v16, B, S).reshape(T, ATTN_WIDTH)
    km, vm = _memkv(mem, row(mem_norm), w_mem_kv.astype(BF16), gmk, B)

    pool_bd = jax.scipy.linalg.block_diag(*[pool_proj[g] for g in range(pool_proj.shape[0])]).astype(BF16)
    w_r = jnp.zeros((D, LANES), F32)
    w_r = w_r.at[:, :N_GROUPS].set(w_group)
    w_r = w_r.at[:, EXPERT_LANE0:].set(jnp.transpose(w_router, (1, 0, 2)).reshape(D, N_EXPERTS))
    b_r = jnp.zeros((1, LANES), F32).at[0, :N_GROUPS].set(b_group).at[0, EXPERT_LANE0:].set(b_router.reshape(-1))
    x2, h2p, rinfo, rinfot, counts = _mix(ya, u, qm, km, vm, x2d, pool_bd, row(pool_scale),
                                          w_out.astype(BF16), row(ffn_norm), w_r.astype(BF16), b_r, B, S)

    R, pstart, sched = _tile_schedule(counts, T)
    dest = _dest(rinfot, pstart.astype(F32).reshape(N_EXPERTS, 1), T)
    inv = _rowmap(dest[0], R)
    y2slot = _moe(*sched, inv, h2p, w1, w3, w2)
    out = _combine(x2, rinfo, y2slot)
    return out.reshape(B, S, D)


def _tile_schedule(counts, T):
    tr = TR_MOE
    R = 2 * T + N_EXPERTS * tr
    n_tiles = R // tr
    cnt = counts[0, EXPERT_LANE0:].astype(I32)
    padded = ((cnt + tr - 1) // tr) * tr
    pend = jnp.cumsum(padded)
    pstart = pend - padded
    n_used = (pend[-1] // tr).astype(I32).reshape(1)
    tiles = jnp.arange(n_tiles, dtype=I32)
    tile_row = jnp.minimum(tiles, n_used[0] - 1) * tr
    blk_exp = jnp.minimum(jnp.sum(tile_row[:, None] >= pend[None, :], axis=1), N_EXPERTS - 1).astype(I32)
    in_use = tiles < n_used[0]
    first = in_use & ((tiles == 0) | (blk_exp != jnp.roll(blk_exp, 1)))
    run_idx = jnp.cumsum(first.astype(I32)) - 1
    used = cnt > 0
    exp_of_run = jnp.argsort(jnp.logical_not(used), stable=True).astype(I32)
    nxt = jnp.where(run_idx + 1 < jnp.sum(used), exp_of_run[jnp.clip(run_idx + 1, 0, N_EXPERTS - 1)], -1)
    valid = jnp.where(in_use, jnp.clip(cnt[blk_exp] - (tiles * tr - pstart[blk_exp]), 0, tr), 0)
    sched = (blk_exp, n_used, first.astype(I32), nxt.astype(I32), (run_idx & 1).astype(I32), valid.astype(I32))
    return R, pstart, sched


def kernel(x, mem, attn_norm, w_in, q_norm, k_norm, pool_proj, pool_scale, mem_norm, w_mem_kv, mq_norm, mk_norm,
           w_out, ffn_norm, w_group, b_group, w_router, b_router, w1, w3, w2):
    for l in range(attn_norm.shape[0]):
        x = _layer(x, mem, attn_norm[l], w_in[l], q_norm[l], k_norm[l], pool_proj[l], pool_scale[l],
                   mem_norm[l], w_mem_kv[l], mq_norm[l], mk_norm[l], w_out[l], ffn_norm[l],
                   w_group[l], b_group[l], w_router[l], b_router[l], w1[l], w3[l], w2[l])
    return x
```

```python
import functools

import jax
import jax.numpy as jnp
from jax import lax
from jax.experimental import pallas as pl
from jax.experimental.pallas import tpu as pltpu

F32 = jnp.float32
BF16 = jnp.bfloat16
I32 = jnp.int32
U32 = jnp.uint32

D_MODEL = 1024
HEAD_DIM = 64
ATTN_WIDTH = 512
POOL_WIDTH = 256
MEM_WIDTH = 256
N_MEM = 256
IN_WIDTH = 3 * ATTN_WIDTH + POOL_WIDTH + MEM_WIDTH
N_GROUPS = 8
EXPERTS_PER_GROUP = 8
N_EXPERTS = 64
EXPERT_HIDDEN = 512
EPS = 1e-6
NEG_INF = -1e30

LANES = 128
N_PAIRS = ATTN_WIDTH // LANES
BLOCK = 128
SUPER = 16 * BLOCK
HALO = 16

ROW_SUB = D_MODEL // LANES
PACKED = D_MODEL // 2

TM_PROJ = 1024
TM_MIX = 1024
TR_MOE = 256
ROW_UNROLL = 16
TC_COMBINE = 256
EXPERT_LANE0 = 64

VMEM_LIMIT = 48 * 1024 * 1024
MOE_VMEM_LIMIT = 56 * 1024 * 1024


def _cparams(sem):
    return pltpu.CompilerParams(dimension_semantics=sem, vmem_limit_bytes=VMEM_LIMIT)


def _group_ones(n):
    g = jnp.arange(n) // HEAD_DIM
    return (g[:, None] == g[None, :]).astype(BF16)


def _head_norm(z, ones_ref, gain_ref):
    ss = jnp.dot((z * z).astype(BF16), ones_ref[...], preferred_element_type=F32)
    return z * lax.rsqrt(ss * (1.0 / HEAD_DIM) + EPS) * gain_ref[...]


def _proj_kernel(x_ref, an_ref, win_ref, gq_ref, gk_ref, gm_ref, o512_ref, o256_ref,
                 q1_ref, k1_ref, v1_ref, q4_ref, k4_ref, v4_ref, q16_ref, k16_ref, v16_ref,
                 u_ref, qm_ref, zs_ref):
    tm = x_ref.shape[0]
    x = x_ref[...]
    ms = jnp.mean(x * x, axis=-1, keepdims=True)
    h = (x * lax.rsqrt(ms + EPS) * an_ref[...]).astype(BF16)
    a0, a1, a2, a3 = ATTN_WIDTH, 2 * ATTN_WIDTH, 3 * ATTN_WIDTH, 3 * ATTN_WIDTH + POOL_WIDTH

    def cols(lo, hi):
        return jnp.dot(h, win_ref[:, lo:hi], preferred_element_type=F32)

    groups = ((lambda: _head_norm(cols(0, a0), o512_ref, gq_ref), q1_ref, q4_ref, q16_ref),
              (lambda: _head_norm(cols(a0, a1), o512_ref, gk_ref), k1_ref, k4_ref, k16_ref),
              (lambda: cols(a1, a2), v1_ref, v4_ref, v16_ref))
    for make, o1, o4, o16 in groups:
        val = make()
        for hp in range(N_PAIRS):
            pair = val[:, hp * LANES:(hp + 1) * LANES]
            zs_ref[hp] = pair
            o1[hp] = pair.astype(BF16)
        for d, o in ((4, o4), (16, o16)):
            for r in range(d):
                for hp in range(N_PAIRS):
                    rows = zs_ref[hp, pl.ds(r, tm // d, stride=d), :]
                    o[hp, :, r * LANES:(r + 1) * LANES] = rows.astype(BF16)
    u_ref[...] = cols(a2, a3)
    qm_ref[...] = _head_norm(cols(a3, IN_WIDTH), o256_ref, gm_ref).astype(BF16)


def _proj(x2d, attn_norm, w_in, gq, gk, gm, B, S):
    T = B * S
    tm = TM_PROJ
    nj = S // tm
    const = lambda i: (0, 0)

    def lay(d):
        return jax.ShapeDtypeStruct((B, N_PAIRS, S // d, d * LANES), BF16)

    def lay_spec(d):
        return pl.BlockSpec((None, N_PAIRS, tm // d, d * LANES), lambda i: (i // nj, 0, i % nj, 0))

    out_shape = [lay(1)] * 3 + [lay(4)] * 3 + [lay(16)] * 3 + [
        jax.ShapeDtypeStruct((T, POOL_WIDTH), F32), jax.ShapeDtypeStruct((T, MEM_WIDTH), BF16)]
    out_specs = [lay_spec(1)] * 3 + [lay_spec(4)] * 3 + [lay_spec(16)] * 3 + [
        pl.BlockSpec((tm, POOL_WIDTH), lambda i: (i, 0)), pl.BlockSpec((tm, MEM_WIDTH), lambda i: (i, 0))]
    return pl.pallas_call(
        _proj_kernel,
        grid=(T // tm,),
        in_specs=[pl.BlockSpec((tm, D_MODEL), lambda i: (i, 0)),
                  pl.BlockSpec((1, D_MODEL), const),
                  pl.BlockSpec((D_MODEL, IN_WIDTH), const),
                  pl.BlockSpec((1, ATTN_WIDTH), const),
                  pl.BlockSpec((1, ATTN_WIDTH), const),
                  pl.BlockSpec((1, MEM_WIDTH), const),
                  pl.BlockSpec((ATTN_WIDTH, ATTN_WIDTH), const),
                  pl.BlockSpec((MEM_WIDTH, MEM_WIDTH), const)],
        out_specs=out_specs,
        out_shape=out_shape,
        scratch_shapes=[pltpu.VMEM((N_PAIRS, tm, LANES), F32)],
        compiler_params=_cparams(("parallel",)),
        name="proj",
    )(x2d, attn_norm, w_in, gq, gk, gm, _group_ones(ATTN_WIDTH), _group_ones(MEM_WIDTH))


def _attn_kernel(q1, k1c, k1p, v1c, v1p, q4, k4c, k4p, v4c, v4p, q16, k16c, k16p, v16c, v16p,
                 bias_ref, o_ref, obuf, lbuf):
    c = pl.program_id(2)
    lane = lax.broadcasted_iota(I32, (BLOCK, LANES), 1)
    is_a = lane < HEAD_DIM
    lane_row = lax.broadcasted_iota(I32, (1, LANES), 1)
    mask_a = jnp.where(lane_row < HEAD_DIM, 1.0, 0.0).astype(BF16)
    mask_b = jnp.where(lane_row < HEAD_DIM, 0.0, 1.0).astype(BF16)
    bias_full = bias_ref[0]
    bias_first = jnp.where(c > 0, bias_full, bias_ref[1])

    def tile(q_t, kp_t, kc_t, vp_t, vc_t, bias):
        lhs = jnp.concatenate([q_t * mask_a, q_t * mask_b], axis=0)
        keys = jnp.concatenate([kp_t, kc_t], axis=0)
        s = lax.dot_general(lhs, keys, (((1,), (1,)), ((), ())), preferred_element_type=F32) + bias
        m = jnp.max(s, axis=-1, keepdims=True)
        p = jnp.exp(s - m)
        l = jnp.sum(p, axis=-1, keepdims=True)
        vals = jnp.concatenate([vp_t, vc_t], axis=0)
        pv = jnp.dot(p.astype(BF16), vals, preferred_element_type=F32) * (1.0 / l)
        lse = m + jnp.log(l)
        return jnp.where(is_a, pv[:BLOCK], pv[BLOCK:]), jnp.where(is_a, lse[:BLOCK], lse[BLOCK:])

    for jb in range(SUPER // BLOCK):
        cur = pl.ds(jb * BLOCK, BLOCK)
        if jb == 0:
            kp_t, vp_t, bias = k1p[...], v1p[...], bias_first
        else:
            prev = pl.ds((jb - 1) * BLOCK, BLOCK)
            kp_t, vp_t, bias = k1c[prev, :], v1c[prev, :], bias_full
        o_t, lse_t = tile(q1[cur, :], kp_t, k1c[cur, :], vp_t, v1c[cur, :], bias)
        obuf[0, cur, :] = o_t
        lbuf[0, cur, :] = lse_t
    for pat, (d, q, kc, kp, vc, vp) in enumerate(((4, q4, k4c, k4p, v4c, v4p), (16, q16, k16c, k16p, v16c, v16p)), 1):
        nblk = SUPER // (BLOCK * d)
        for r in range(d):
            cols = pl.ds(r * LANES, LANES)
            for jb in range(nblk):
                cur = pl.ds(jb * BLOCK, BLOCK)
                if jb == 0:
                    kp_t, vp_t, bias = kp[:, cols], vp[:, cols], bias_first
                else:
                    prev = pl.ds((jb - 1) * BLOCK, BLOCK)
                    kp_t, vp_t, bias = kc[prev, cols], vc[prev, cols], bias_full
                o_t, lse_t = tile(q[cur, cols], kp_t, kc[cur, cols], vp_t, vc[cur, cols], bias)
                rows = pl.ds(jb * BLOCK * d + r, BLOCK, stride=d)
                obuf[pat, rows, :] = o_t
                lbuf[pat, rows, :] = lse_t
    for jb in range(SUPER // BLOCK):
        cur = pl.ds(jb * BLOCK, BLOCK)
        l0, l1, l2 = lbuf[0, cur, :], lbuf[1, cur, :], lbuf[2, cur, :]
        top = jnp.maximum(jnp.maximum(l0, l1), l2)
        w0, w1, w2 = jnp.exp(l0 - top), jnp.exp(l1 - top), jnp.exp(l2 - top)
        mixed = (w0 * obuf[0, cur, :] + w1 * obuf[1, cur, :] + w2 * obuf[2, cur, :]) * (1.0 / (w0 + w1 + w2))
        o_ref[cur, :] = mixed.astype(BF16)


def _band_bias():
    qi = jnp.arange(BLOCK)[:, None]
    kj = jnp.arange(2 * BLOCK)[None, :]
    dist = qi + BLOCK - kj
    in_band = (dist >= 0) & (dist <= BLOCK)
    full = jnp.where(in_band, 0.0, NEG_INF).astype(F32)
    first = jnp.where(in_band & (kj >= BLOCK), 0.0, NEG_INF).astype(F32)
    return jnp.stack([jnp.tile(full, (2, 1)), jnp.tile(first, (2, 1))])


def _attn(q1, k1, v1, q4, k4, v4, q16, k16, v16, B, S):
    nsup = S // SUPER

    def specs(d):
        rows = SUPER // d
        per = rows // BLOCK
        cur = pl.BlockSpec((None, None, rows, d * LANES), lambda b, hp, c: (b, hp, c, 0))
        prev = pl.BlockSpec((None, None, BLOCK, d * LANES),
                            lambda b, hp, c: (b, hp, jnp.maximum(per * c - 1, 0), 0))
        return [cur, cur, prev, cur, prev]

    return pl.pallas_call(
        _attn_kernel,
        grid=(B, N_PAIRS, nsup),
        in_specs=specs(1) + specs(4) + specs(16) + [
            pl.BlockSpec((2, 2 * BLOCK, 2 * BLOCK), lambda b, hp, c: (0, 0, 0))],
        out_specs=pl.BlockSpec((None, SUPER, LANES), lambda b, hp, c: (b, c, hp)),
        out_shape=jax.ShapeDtypeStruct((B, S, ATTN_WIDTH), BF16),
        scratch_shapes=[pltpu.VMEM((3, SUPER, LANES), F32)] * 2,
        compiler_params=_cparams(("parallel", "parallel", "parallel")),
        name="dilated_attn",
    )(q1, k1, k1, v1, v1, q4, k4, k4, v4, v4, q16, k16, k16, v16, v16, _band_bias())


def _memkv_kernel(mem_ref, mn_ref, wkv_ref, gk_ref, o256_ref, km_ref, vm_ref):
    m = mem_ref[...]
    ms = jnp.mean(m * m, axis=-1, keepdims=True)
    mn = (m * lax.rsqrt(ms + EPS) * mn_ref[...]).astype(BF16)
    kv = jnp.dot(mn, wkv_ref[...], preferred_element_type=F32)
    km_ref[...] = _head_norm(kv[:, :MEM_WIDTH], o256_ref, gk_ref).astype(BF16)
    vm_ref[...] = kv[:, MEM_WIDTH:].astype(BF16)


def _memkv(mem, mem_norm, w_mem_kv, gmk, B):
    const = lambda b: (0, 0)
    return pl.pallas_call(
        _memkv_kernel,
        grid=(B,),
        in_specs=[pl.BlockSpec((None, N_MEM, D_MODEL), lambda b: (b, 0, 0)),
                  pl.BlockSpec((1, D_MODEL), const),
                  pl.BlockSpec((D_MODEL, 2 * MEM_WIDTH), const),
                  pl.BlockSpec((1, MEM_WIDTH), const),
                  pl.BlockSpec((MEM_WIDTH, MEM_WIDTH), const)],
        out_specs=[pl.BlockSpec((None, N_MEM, MEM_WIDTH), lambda b: (b, 0, 0))] * 2,
        out_shape=[jax.ShapeDtypeStruct((B, N_MEM, MEM_WIDTH), BF16)] * 2,
        compiler_params=_cparams(("parallel",)),
        name="memkv",
    )(mem, mem_norm, w_mem_kv, gmk, _group_ones(MEM_WIDTH))


def _mix_kernel(tiles_per_seq, ya_ref, u_ref, uh_ref, qm_ref, km_ref, vm_ref, x_ref, pp_ref, ps_ref,
                wo_ref, fn_ref, wr_ref, br_ref, ltri_ref,
                x2_ref, h2p_ref, rinfo_ref, rinfot_ref, cnt_ref, base_ref):
    i = pl.program_id(0)
    tm = x_ref.shape[0]
    seq_tile = i % tiles_per_seq

    @pl.when(i == 0)
    def _():
        base_ref[...] = jnp.zeros_like(base_ref)

    u = u_ref[...]
    halo = jnp.where(seq_tile == 0, 0.0, uh_ref[...])
    uu = jnp.concatenate([halo, u], axis=0)
    a1 = uu[1:] + uu[:-1]
    a2 = a1[2:] + a1[:-2]
    a3 = a2[4:] + a2[:-4]
    a4 = a3[8:] + a3[:-8]
    lane_p = lax.broadcasted_iota(I32, (tm, POOL_WIDTH), 1)
    g0, g1, g2 = lane_p < 64, lane_p < 128, lane_p < 192
    wsum = jnp.where(g0, a1[15:], jnp.where(g1, a2[13:], jnp.where(g2, a3[9:], a4[1:])))
    wlen = jnp.where(g0, 2.0, jnp.where(g1, 4.0, jnp.where(g2, 8.0, 16.0)))
    tpos = seq_tile * tm + lax.broadcasted_iota(I32, (tm, POOL_WIDTH), 0) + 1
    cnt = jnp.minimum(tpos.astype(F32), wlen)
    pooled = wsum / cnt - u
    y_pool = jnp.dot(pooled.astype(BF16), pp_ref[...], preferred_element_type=F32) * ps_ref[...]

    lane = lax.broadcasted_iota(I32, (tm, LANES), 1)
    is_a = lane < HEAD_DIM
    lane_row = lax.broadcasted_iota(I32, (1, LANES), 1)
    mask_a = jnp.where(lane_row < HEAD_DIM, 1.0, 0.0).astype(BF16)
    mask_b = jnp.where(lane_row < HEAD_DIM, 0.0, 1.0).astype(BF16)
    y_mem = []
    for pr in range(MEM_WIDTH // LANES):
        cols = slice(pr * LANES, (pr + 1) * LANES)
        qp, kp, vp = qm_ref[:, cols], km_ref[:, cols], vm_ref[:, cols]
        outs = []
        for msk in (mask_a, mask_b):
            s = lax.dot_general(qp * msk, kp, (((1,), (1,)), ((), ())), preferred_element_type=F32)
            m = jnp.max(s, axis=-1, keepdims=True)
            p = jnp.exp(s - m)
            p = p / jnp.sum(p, axis=-1, keepdims=True)
            outs.append(jnp.dot(p.astype(BF16), vp, preferred_element_type=F32))
        y_mem.append(jnp.where(is_a, outs[0], outs[1]))

    rest = jnp.concatenate([y_pool] + y_mem, axis=1).astype(BF16)
    proj = jnp.dot(ya_ref[...], wo_ref[:ATTN_WIDTH, :], preferred_element_type=F32)
    proj += jnp.dot(rest, wo_ref[ATTN_WIDTH:, :], preferred_element_type=F32)
    x2 = x_ref[...] + proj
    x2_ref[...] = x2

    ms = jnp.mean(x2 * x2, axis=-1, keepdims=True)
    h2 = (x2 * lax.rsqrt(ms + EPS) * fn_ref[...]).astype(BF16)
    bits = lax.bitcast_convert_type(h2.astype(F32), U32)
    h2p_ref[...] = bits[:, :PACKED] | (bits[:, PACKED:] >> 16)

    logits = jnp.dot(h2, wr_ref[...], preferred_element_type=F32) + br_ref[...]
    ninf = -jnp.inf
    lane_f = lane.astype(F32)
    grp_f = ((lane - EXPERT_LANE0) >> 3).astype(F32)
    is_g = lane < N_GROUPS
    lg = jnp.where(is_g, logits, ninf)
    mg = jnp.max(lg, axis=-1, keepdims=True)
    g_sel = jnp.min(jnp.where(lg == mg, lane_f, float(LANES)), axis=-1, keepdims=True)
    w_g = 1.0 / jnp.sum(jnp.where(is_g, jnp.exp(logits - mg), 0.0), axis=-1, keepdims=True)
    in_grp = (lane >= EXPERT_LANE0) & (grp_f == g_sel)
    le = jnp.where(in_grp, logits, ninf)
    v1 = jnp.max(le, axis=-1, keepdims=True)
    i1 = jnp.min(jnp.where(le == v1, lane_f, float(2 * LANES)), axis=-1, keepdims=True)
    le2 = jnp.where(lane_f == i1, ninf, le)
    v2 = jnp.max(le2, axis=-1, keepdims=True)
    i2 = jnp.min(jnp.where(le2 == v2, lane_f, float(2 * LANES)), axis=-1, keepdims=True)
    e21 = jnp.exp(v2 - v1)
    gate1 = w_g / (1.0 + e21)
    gate2 = w_g * e21 / (1.0 + e21)

    hot1 = lane_f == i1
    hot2 = lane_f == i2
    onehot = jnp.where(hot1 | hot2, 1.0, 0.0)
    prefix = jnp.dot(ltri_ref[...], onehot.astype(BF16), preferred_element_type=F32)
    tot = prefix + base_ref[...]
    rank1 = jnp.sum(jnp.where(hot1, tot, 0.0), axis=-1, keepdims=True)
    rank2 = jnp.sum(jnp.where(hot2, tot, 0.0), axis=-1, keepdims=True)
    base_new = base_ref[...] + jnp.sum(onehot, axis=0, keepdims=True)
    base_ref[...] = base_new
    cnt_ref[...] = base_new

    e1 = i1 - float(EXPERT_LANE0)
    e2 = i2 - float(EXPERT_LANE0)
    rinfo = jnp.where(lane == 0, e1, jnp.where(lane == 1, e2, jnp.where(lane == 2, rank1, jnp.where(
        lane == 3, rank2, jnp.where(lane == 4, gate1, jnp.where(lane == 5, gate2, 0.0))))))
    rinfo_ref[...] = rinfo
    rinfot_ref[...] = jnp.transpose(rinfo)[:8, :]


def _mix(ya, u, qm, km, vm, x2d, pool_bd, pool_scale, w_out, ffn_norm, w_r, b_r, B, S):
    T = B * S
    tm = TM_MIX
    tps = S // tm
    hb = tm // HALO
    const = lambda i: (0, 0)
    ltri = (jnp.arange(tm)[:, None] > jnp.arange(tm)[None, :]).astype(BF16)
    return pl.pallas_call(
        functools.partial(_mix_kernel, tps),
        grid=(T // tm,),
        in_specs=[pl.BlockSpec((tm, ATTN_WIDTH), lambda i: (i, 0)),
                  pl.BlockSpec((tm, POOL_WIDTH), lambda i: (i, 0)),
                  pl.BlockSpec((HALO, POOL_WIDTH), lambda i: (jnp.maximum(i * hb - 1, 0), 0)),
                  pl.BlockSpec((tm, MEM_WIDTH), lambda i: (i, 0)),
                  pl.BlockSpec((None, N_MEM, MEM_WIDTH), lambda i: (i // tps, 0, 0)),
                  pl.BlockSpec((None, N_MEM, MEM_WIDTH), lambda i: (i // tps, 0, 0)),
                  pl.BlockSpec((tm, D_MODEL), lambda i: (i, 0)),
                  pl.BlockSpec((POOL_WIDTH, POOL_WIDTH), const),
                  pl.BlockSpec((1, POOL_WIDTH), const),
                  pl.BlockSpec((D_MODEL, D_MODEL), const),
                  pl.BlockSpec((1, D_MODEL), const),
                  pl.BlockSpec((D_MODEL, LANES), const),
                  pl.BlockSpec((1, LANES), const),
                  pl.BlockSpec((tm, tm), const)],
        out_specs=[pl.BlockSpec((tm, D_MODEL), lambda i: (i, 0)),
                   pl.BlockSpec((tm, PACKED), lambda i: (i, 0)),
                   pl.BlockSpec((tm, LANES), lambda i: (i, 0)),
                   pl.BlockSpec((8, tm), lambda i: (0, i)),
                   pl.BlockSpec((1, LANES), const)],
        out_shape=[jax.ShapeDtypeStruct((T, D_MODEL), F32),
                   jax.ShapeDtypeStruct((T, PACKED), U32),
                   jax.ShapeDtypeStruct((T, LANES), F32),
                   jax.ShapeDtypeStruct((8, T), F32),
                   jax.ShapeDtypeStruct((1, LANES), F32)],
        scratch_shapes=[pltpu.VMEM((1, LANES), F32)],
        compiler_params=_cparams(("arbitrary",)),
        name="mix_router",
    )(ya, u, u, qm, km, vm, x2d, pool_bd, pool_scale, w_out, ffn_norm, w_r, b_r, ltri)


def _dest_kernel(rt_ref, ps_ref, d_ref):
    tn = rt_ref.shape[1]
    sub = lax.broadcasted_iota(I32, (N_EXPERTS, tn), 0).astype(F32)
    ps = ps_ref[...]
    rows = []
    for k in range(2):
        e = rt_ref[k:k + 1, :]
        start = jnp.sum(jnp.where(sub == e, ps, 0.0), axis=0, keepdims=True)
        rows.append(start + rt_ref[2 + k:3 + k, :])
    word = rows[0].astype(I32) | (rows[1].astype(I32) << 16)
    d_ref[...] = jnp.concatenate([word, jnp.zeros((7, tn), I32)], axis=0)


def _dest(rinfot, pstart_col, T):
    tn = 2048
    return pl.pallas_call(
        _dest_kernel,
        grid=(T // tn,),
        in_specs=[pl.BlockSpec((8, tn), lambda i: (0, i)),
                  pl.BlockSpec((N_EXPERTS, 1), lambda i: (0, 0))],
        out_specs=pl.BlockSpec((8, tn), lambda i: (0, i)),
        out_shape=jax.ShapeDtypeStruct((8, T), I32),
        compiler_params=_cparams(("parallel",)),
        name="dest_rows",
    )(rinfot, pstart_col)


def _rowmap_kernel(d_ref, fill_hbm, inv_ref, sem):
    T = d_ref.shape[0]
    cp = pltpu.make_async_copy(fill_hbm, inv_ref, sem)
    cp.start()
    cp.wait()
    unroll = 8

    def body(g, carry):
        t0 = g * unroll
        words = [d_ref[t0 + u] for u in range(unroll)]
        for u, w in enumerate(words):
            inv_ref[w & 0xFFFF] = t0 + u
            inv_ref[lax.shift_right_logical(w, 16)] = T + t0 + u
        return carry

    lax.fori_loop(0, T // unroll, body, 0)


def _rowmap(dword, R):
    assert R <= 1 << 16
    return pl.pallas_call(
        _rowmap_kernel,
        grid_spec=pltpu.PrefetchScalarGridSpec(
            num_scalar_prefetch=1,
            grid=(1,),
            in_specs=[pl.BlockSpec(memory_space=pl.ANY)],
            out_specs=pl.BlockSpec(memory_space=pltpu.SMEM),
            scratch_shapes=[pltpu.SemaphoreType.DMA(())]),
        out_shape=jax.ShapeDtypeStruct((R,), I32),
        compiler_params=_cparams(("arbitrary",)),
        name="rowmap",
    )(dword, jnp.zeros((R,), I32))


def _row_loop(n, body):
    shift = ROW_UNROLL.bit_length() - 1

    def group(g, carry):
        for u in range(ROW_UNROLL):
            body(g * ROW_UNROLL + u)
        return carry

    def rest(i, carry):
        body(((n >> shift) << shift) + i)
        return carry

    lax.fori_loop(0, n >> shift, group, 0)
    lax.fori_loop(0, n & (ROW_UNROLL - 1), rest, 0)


def _moe_kernel(be_ref, nu_ref, first_ref, nxt_ref, ws_ref, valid_ref, inv_ref,
                h_hbm, w1_hbm, w3_hbm, w2_hbm, o_hbm,
                tab, xbuf, ybuf, w1s, w3s, w2s, w1b, w3b, w2b, tsem, ssem, wsem):
    j = pl.program_id(0)
    nu = nu_ref[0]
    T = h_hbm.shape[0]
    tr = TR_MOE
    n_tiles = inv_ref.shape[0] // tr

    def tile_rows(ref, row, n=1):
        return ref.at[pl.ds(pl.multiple_of(row * ROW_SUB, ROW_SUB), n * ROW_SUB)]

    def gather_tile(tile, slot):
        for i in range(tr):
            v = inv_ref[tile * tr + i]
            tok = (v & (T - 1)) if T & (T - 1) == 0 else jnp.where(v >= T, v - T, v)
            xbuf[pl.ds(slot * tr + i, 1), :] = tab[pl.ds(tok, 1), :]

    def wait_rows(n, done):
        @pl.when(n == tr)
        def _():
            done(tr).wait()

        @pl.when(n != tr)
        def _():
            lax.fori_loop(0, n >> 3, lambda g, c: (done(8).wait(), c)[1], 0)
            lax.fori_loop(0, n & 7, lambda g, c: (done(1).wait(), c)[1], 0)

    def scatter_row(tile, slot, i):
        v = inv_ref[tile * tr + i]
        return pltpu.make_async_copy(tile_rows(ybuf, slot * tr + i), tile_rows(o_hbm, v), ssem.at[slot])

    def scatter_wait(slot, n):
        wait_rows(n, lambda rows: pltpu.make_async_copy(
            tile_rows(ybuf, slot * tr, rows), tile_rows(o_hbm, 0, rows), ssem.at[slot]))

    def weight_copies(e, slot):
        return [pltpu.make_async_copy(src.at[e], dst.at[slot], wsem.at[slot])
                for src, dst in ((w1_hbm, w1s), (w3_hbm, w3s), (w2_hbm, w2s))]

    @pl.when(j < nu)
    def _():
        s = j & 1
        e = be_ref[j]

        @pl.when(j == 0)
        def _():
            table = pltpu.make_async_copy(h_hbm, tab, tsem)
            table.start()
            for cp in weight_copies(e, 0):
                cp.start()
            table.wait()
            gather_tile(0, 0)

        @pl.when(first_ref[j] == 1)
        def _():
            wslot = ws_ref[j]
            for cp in weight_copies(e, wslot):
                cp.wait()
            e_next = nxt_ref[j]

            @pl.when(e_next >= 0)
            def _():
                for cp in weight_copies(e_next, 1 - wslot):
                    cp.start()

            w1b[...] = w1s[wslot].astype(BF16)
            w3b[...] = w3s[wslot].astype(BF16)
            w2b[...] = w2s[wslot].astype(BF16)

        @pl.when(j >= 2)
        def _():
            scatter_wait(s, valid_ref[j - 2])

        w = xbuf[pl.ds(pl.multiple_of(s * tr, tr), tr), :]
        lo = lax.bitcast_convert_type(w & jnp.uint32(0xFFFF0000), F32).astype(BF16)
        hi = lax.bitcast_convert_type(w << 16, F32).astype(BF16)
        gather_tile(jnp.minimum(j + 1, n_tiles - 1), 1 - s)
        halves = [slice(k * (EXPERT_HIDDEN // 2), (k + 1) * (EXPERT_HIDDEN // 2)) for k in range(2)]

        def up(wb, cols):
            return (jnp.dot(lo, wb[:PACKED, cols], preferred_element_type=F32)
                    + jnp.dot(hi, wb[PACKED:, cols], preferred_element_type=F32))

        ab = [(up(w1b, cols), up(w3b, cols)) for cols in halves]
        y = None
        for cols, (a, b) in zip(halves, ab):
            hmid = (a / (1.0 + jnp.exp(-a)) * b).astype(BF16)
            part = jnp.dot(hmid, w2b[cols, :], preferred_element_type=F32)
            y = part if y is None else y + part
        base = pl.multiple_of(s * (tr * ROW_SUB), tr * ROW_SUB)
        for c in range(ROW_SUB):
            ybuf[pl.ds(base + c, tr, stride=ROW_SUB), :] = y[:, c * LANES:(c + 1) * LANES]

        _row_loop(valid_ref[j], lambda i: scatter_row(j, s, i).start())

        @pl.when(j == nu - 1)
        def _():
            scatter_wait(s, valid_ref[j])

            @pl.when(j >= 1)
            def _():
                scatter_wait(1 - s, valid_ref[j - 1])


def _moe(blk_exp, n_used, first, nxt, wslot, valid, inv, h2p, w1, w3, w2):
    T = h2p.shape[0]
    tr = TR_MOE
    n_tiles = blk_exp.shape[0]
    any_spec = pl.BlockSpec(memory_space=pl.ANY)
    return pl.pallas_call(
        _moe_kernel,
        grid_spec=pltpu.PrefetchScalarGridSpec(
            num_scalar_prefetch=7,
            grid=(n_tiles,),
            in_specs=[any_spec] * 4,
            out_specs=any_spec,
            scratch_shapes=[pltpu.VMEM((T, PACKED), U32),
                            pltpu.VMEM((2 * tr, PACKED), U32),
                            pltpu.VMEM((2 * tr * ROW_SUB, LANES), F32),
                            pltpu.VMEM((2, D_MODEL, EXPERT_HIDDEN), F32),
                            pltpu.VMEM((2, D_MODEL, EXPERT_HIDDEN), F32),
                            pltpu.VMEM((2, EXPERT_HIDDEN, D_MODEL), F32),
                            pltpu.VMEM((D_MODEL, EXPERT_HIDDEN), BF16),
                            pltpu.VMEM((D_MODEL, EXPERT_HIDDEN), BF16),
                            pltpu.VMEM((EXPERT_HIDDEN, D_MODEL), BF16),
                            pltpu.SemaphoreType.DMA(()),
                            pltpu.SemaphoreType.DMA((2,)),
                            pltpu.SemaphoreType.DMA((2,))]),
        out_shape=jax.ShapeDtypeStruct((2 * T * ROW_SUB, LANES), F32),
        compiler_params=pltpu.CompilerParams(dimension_semantics=("arbitrary",), vmem_limit_bytes=MOE_VMEM_LIMIT),
        name="moe_experts",
    )(blk_exp, n_used, first, nxt, wslot, valid, inv, h2p, w1, w3, w2)


def _combine_kernel(x2_ref, rinfo_ref, y1_ref, y2_ref, o_ref):
    g1 = rinfo_ref[:, 4:5]
    g2 = rinfo_ref[:, 5:6]
    tc = x2_ref.shape[0]
    for c in range(ROW_SUB):
        cols = slice(c * LANES, (c + 1) * LANES)
        rows = pl.ds(c, tc, stride=ROW_SUB)
        o_ref[:, cols] = x2_ref[:, cols] + (g1 * y1_ref[rows, :] + g2 * y2_ref[rows, :])


def _combine(x2, rinfo, y2slot):
    T = x2.shape[0]
    tc = TC_COMBINE
    return pl.pallas_call(
        _combine_kernel,
        grid=(T // tc,),
        in_specs=[pl.BlockSpec((tc, D_MODEL), lambda i: (i, 0)),
                  pl.BlockSpec((tc, LANES), lambda i: (i, 0)),
                  pl.BlockSpec((tc * ROW_SUB, LANES), lambda i: (i, 0)),
                  pl.BlockSpec((tc * ROW_SUB, LANES), lambda i: (T // tc + i, 0))],
        out_specs=pl.BlockSpec((tc, D_MODEL), lambda i: (i, 0)),
        out_shape=jax.ShapeDtypeStruct((T, D_MODEL), F32),
        compiler_params=_cparams(("parallel",)),
        name="combine",
    )(x2, rinfo, y2slot, y2slot)


def _layer(x, mem, attn_norm, w_in, q_norm, k_norm, pool_proj, pool_scale, mem_norm, w_mem_kv,
           mq_norm, mk_norm, w_out, ffn_norm, w_group, b_group, w_router, b_router, w1, w3, w2):
    B, S, D = x.shape
    T = B * S
    assert D == D_MODEL and S % SUPER == 0 and T % TM_PROJ == 0
    x2d = x.reshape(T, D)
    row = lambda v: v.reshape(1, -1).astype(F32)
    scale = HEAD_DIM ** -0.5
    gq = row(jnp.tile(q_norm, ATTN_WIDTH // HEAD_DIM) * scale)
    gk = row(jnp.tile(k_norm, ATTN_WIDTH // HEAD_DIM))
    gmq = row(jnp.tile(mq_norm, MEM_WIDTH // HEAD_DIM) * scale)
    gmk = row(jnp.tile(mk_norm, MEM_WIDTH // HEAD_DIM))

    (q1, k1, v1, q4, k4, v4, q16, k16, v16, u, qm) = _proj(
        x2d, row(attn_norm), w_in.astype(BF16), gq, gk, gmq, B, S)
    ya = _attn(q1, k1, v1, q4, k4, v4, q16, k16, v16, B, S).reshape(T, ATTN_WIDTH)
    km, vm = _memkv(mem, row(mem_norm), w_mem_kv.astype(BF16), gmk, B)

    pool_bd = jax.scipy.linalg.block_diag(*[pool_proj[g] for g in range(pool_proj.shape[0])]).astype(BF16)
    w_r = jnp.zeros((D, LANES), F32)
    w_r = w_r.at[:, :N_GROUPS].set(w_group)
    w_r = w_r.at[:, EXPERT_LANE0:].set(jnp.transpose(w_router, (1, 0, 2)).reshape(D, N_EXPERTS))
    b_r = jnp.zeros((1, LANES), F32).at[0, :N_GROUPS].set(b_group).at[0, EXPERT_LANE0:].set(b_router.reshape(-1))
    x2, h2p, rinfo, rinfot, counts = _mix(ya, u, qm, km, vm, x2d, pool_bd, row(pool_scale),
                                          w_out.astype(BF16), row(ffn_norm), w_r.astype(BF16), b_r, B, S)

    R, pstart, sched = _tile_schedule(counts, T)
    dest = _dest(rinfot, pstart.astype(F32).reshape(N_EXPERTS, 1), T)
    inv = _rowmap(dest[0], R)
    y2slot = _moe(*sched, inv, h2p, w1, w3, w2)
    out = _combine(x2, rinfo, y2slot)
    return out.reshape(B, S, D)


def _tile_schedule(counts, T):
    tr = TR_MOE
    R = 2 * T + N_EXPERTS * tr
    n_tiles = R // tr
    cnt = counts[0, EXPERT_LANE0:].astype(I32)
    padded = ((cnt + tr - 1) // tr) * tr
    pend = jnp.cumsum(padded)
    pstart = pend - padded
    n_used = (pend[-1] // tr).astype(I32).reshape(1)
    tiles = jnp.arange(n_tiles, dtype=I32)
    tile_row = jnp.minimum(tiles, n_used[0] - 1) * tr
    blk_exp = jnp.minimum(jnp.sum(tile_row[:, None] >= pend[None, :], axis=1), N_EXPERTS - 1).astype(I32)
    in_use = tiles < n_used[0]
    first = in_use & ((tiles == 0) | (blk_exp != jnp.roll(blk_exp, 1)))
    run_idx = jnp.cumsum(first.astype(I32)) - 1
    used = cnt > 0
    exp_of_run = jnp.argsort(jnp.logical_not(used), stable=True).astype(I32)
    nxt = jnp.where(run_idx + 1 < jnp.sum(used), exp_of_run[jnp.clip(run_idx + 1, 0, N_EXPERTS - 1)], -1)
    valid = jnp.where(in_use, jnp.clip(cnt[blk_exp] - (tiles * tr - pstart[blk_exp]), 0, tr), 0)
    sched = (blk_exp, n_used, first.astype(I32), nxt.astype(I32), (run_idx & 1).astype(I32), valid.astype(I32))
    return R, pstart, sched


def kernel(x, mem, attn_norm, w_in, q_norm, k_norm, pool_proj, pool_scale, mem_norm, w_mem_kv, mq_norm, mk_norm,
           w_out, ffn_norm, w_group, b_group, w_router, b_router, w1, w3, w2):
    for l in range(attn_norm.shape[0]):
        x = _layer(x, mem, attn_norm[l], w_in[l], q_norm[l], k_norm[l], pool_proj[l], pool_scale[l],
                   mem_norm[l], w_mem_kv[l], mq_norm[l], mk_norm[l], w_out[l], ffn_norm[l],
                   w_group[l], b_group[l], w_router[l], b_router[l], w1[l], w3[l], w2[l])
    return x
```

```python
import functools

import jax
import jax.numpy as jnp
from jax import lax
from jax.experimental import pallas as pl
from jax.experimental.pallas import tpu as pltpu

F32 = jnp.float32
BF16 = jnp.bfloat16
I32 = jnp.int32
U32 = jnp.uint32

D_MODEL = 1024
HEAD_DIM = 64
ATTN_WIDTH = 512
POOL_WIDTH = 256
MEM_WIDTH = 256
N_MEM = 256
IN_WIDTH = 3 * ATTN_WIDTH + POOL_WIDTH + MEM_WIDTH
N_GROUPS = 8
EXPERTS_PER_GROUP = 8
N_EXPERTS = 64
EXPERT_HIDDEN = 512
EPS = 1e-6
NEG_INF = -1e30

LANES = 128
N_PAIRS = ATTN_WIDTH // LANES
BLOCK = 128
SUPER = 16 * BLOCK
HALO = 16

ROW_SUB = D_MODEL // LANES
PACKED = D_MODEL // 2

TM_PROJ = 1024
TM_MIX = 1024
TR_MOE = 256
ROW_UNROLL = 16
TC_COMBINE = 256
EXPERT_LANE0 = 64

VMEM_LIMIT = 48 * 1024 * 1024
MOE_VMEM_LIMIT = 56 * 1024 * 1024


def _cparams(sem):
    return pltpu.CompilerParams(dimension_semantics=sem, vmem_limit_bytes=VMEM_LIMIT)


def _group_ones(n):
    g = jnp.arange(n) // HEAD_DIM
    return (g[:, None] == g[None, :]).astype(BF16)


def _head_norm(z, ones_ref, gain_ref):
    ss = jnp.dot((z * z).astype(BF16), ones_ref[...], preferred_element_type=F32)
    return z * lax.rsqrt(ss * (1.0 / HEAD_DIM) + EPS) * gain_ref[...]


def _proj_kernel(x_ref, an_ref, win_ref, gq_ref, gk_ref, gm_ref, o512_ref, o256_ref,
                 q1_ref, k1_ref, v1_ref, q4_ref, k4_ref, v4_ref, q16_ref, k16_ref, v16_ref,
                 u_ref, qm_ref, zs_ref, z4_ref):
    tm = x_ref.shape[0]
    x = x_ref[...]
    ms = jnp.mean(x * x, axis=-1, keepdims=True)
    h = (x * lax.rsqrt(ms + EPS) * an_ref[...]).astype(BF16)
    a0, a1, a2, a3 = ATTN_WIDTH, 2 * ATTN_WIDTH, 3 * ATTN_WIDTH, 3 * ATTN_WIDTH + POOL_WIDTH

    def cols(lo, hi):
        return jnp.dot(h, win_ref[:, lo:hi], preferred_element_type=F32)

    groups = ((lambda: _head_norm(cols(0, a0), o512_ref, gq_ref), q1_ref, q4_ref, q16_ref),
              (lambda: _head_norm(cols(a0, a1), o512_ref, gk_ref), k1_ref, k4_ref, k16_ref),
              (lambda: cols(a1, a2), v1_ref, v4_ref, v16_ref))
    for make, o1, o4, o16 in groups:
        val = make()
        for hp in range(N_PAIRS):
            pair = val[:, hp * LANES:(hp + 1) * LANES]
            zs_ref[hp] = pair
            o1[hp] = pair.astype(BF16)
        q4n = tm // 4
        for hp in range(N_PAIRS):
            for r4 in range(4):
                rows = zs_ref[hp, pl.ds(r4, q4n, stride=4), :]
                z4_ref[hp, pl.ds(r4 * q4n, q4n), :] = rows
                o4[hp, :, r4 * LANES:(r4 + 1) * LANES] = rows.astype(BF16)
            for r16 in range(16):
                r4, a = r16 % 4, r16 // 4
                rows = z4_ref[hp, pl.ds(r4 * q4n + a, tm // 16, stride=4), :]
                o16[hp, :, r16 * LANES:(r16 + 1) * LANES] = rows.astype(BF16)
    u_ref[...] = cols(a2, a3)
    qm_ref[...] = _head_norm(cols(a3, IN_WIDTH), o256_ref, gm_ref).astype(BF16)


def _proj(x2d, attn_norm, w_in, gq, gk, gm, B, S):
    T = B * S
    tm = TM_PROJ
    nj = S // tm
    const = lambda i: (0, 0)

    def lay(d):
        return jax.ShapeDtypeStruct((B, N_PAIRS, S // d, d * LANES), BF16)

    def lay_spec(d):
        return pl.BlockSpec((None, N_PAIRS, tm // d, d * LANES), lambda i: (i // nj, 0, i % nj, 0))

    out_shape = [lay(1)] * 3 + [lay(4)] * 3 + [lay(16)] * 3 + [
        jax.ShapeDtypeStruct((T, POOL_WIDTH), F32), jax.ShapeDtypeStruct((T, MEM_WIDTH), BF16)]
    out_specs = [lay_spec(1)] * 3 + [lay_spec(4)] * 3 + [lay_spec(16)] * 3 + [
        pl.BlockSpec((tm, POOL_WIDTH), lambda i: (i, 0)), pl.BlockSpec((tm, MEM_WIDTH), lambda i: (i, 0))]
    return pl.pallas_call(
        _proj_kernel,
        grid=(T // tm,),
        in_specs=[pl.BlockSpec((tm, D_MODEL), lambda i: (i, 0)),
                  pl.BlockSpec((1, D_MODEL), const),
                  pl.BlockSpec((D_MODEL, IN_WIDTH), const),
                  pl.BlockSpec((1, ATTN_WIDTH), const),
                  pl.BlockSpec((1, ATTN_WIDTH), const),
                  pl.BlockSpec((1, MEM_WIDTH), const),
                  pl.BlockSpec((ATTN_WIDTH, ATTN_WIDTH), const),
                  pl.BlockSpec((MEM_WIDTH, MEM_WIDTH), const)],
        out_specs=out_specs,
        out_shape=out_shape,
        scratch_shapes=[pltpu.VMEM((N_PAIRS, tm, LANES), F32)] * 2,
        compiler_params=_cparams(("parallel",)),
        name="proj",
    )(x2d, attn_norm, w_in, gq, gk, gm, _group_ones(ATTN_WIDTH), _group_ones(MEM_WIDTH))


def _attn_kernel(q1, k1c, k1p, v1c, v1p, q4, k4c, k4p, v4c, v4p, q16, k16c, k16p, v16c, v16p,
                 bias_ref, o_ref, obuf, lbuf):
    c = pl.program_id(2)
    lane = lax.broadcasted_iota(I32, (BLOCK, LANES), 1)
    is_a = lane < HEAD_DIM
    lane_row = lax.broadcasted_iota(I32, (1, LANES), 1)
    mask_a = jnp.where(lane_row < HEAD_DIM, 1.0, 0.0).astype(BF16)
    mask_b = jnp.where(lane_row < HEAD_DIM, 0.0, 1.0).astype(BF16)
    bias_full = bias_ref[0]
    bias_first = jnp.where(c > 0, bias_full, bias_ref[1])

    def tile(q_t, kp_t, kc_t, vp_t, vc_t, bias):
        lhs = jnp.concatenate([q_t * mask_a, q_t * mask_b], axis=0)
        keys = jnp.concatenate([kp_t, kc_t], axis=0)
        s = lax.dot_general(lhs, keys, (((1,), (1,)), ((), ())), preferred_element_type=F32) + bias
        m = jnp.max(s, axis=-1, keepdims=True)
        p = jnp.exp(s - m)
        l = jnp.sum(p, axis=-1, keepdims=True)
        vals = jnp.concatenate([vp_t, vc_t], axis=0)
        pv = jnp.dot(p.astype(BF16), vals, preferred_element_type=F32) * (1.0 / l)
        lse = m + jnp.log(l)
        return jnp.where(is_a, pv[:BLOCK], pv[BLOCK:]), jnp.where(is_a, lse[:BLOCK], lse[BLOCK:])

    for jb in range(SUPER // BLOCK):
        cur = pl.ds(jb * BLOCK, BLOCK)
        if jb == 0:
            kp_t, vp_t, bias = k1p[...], v1p[...], bias_first
        else:
            prev = pl.ds((jb - 1) * BLOCK, BLOCK)
            kp_t, vp_t, bias = k1c[prev, :], v1c[prev, :], bias_full
        o_t, lse_t = tile(q1[cur, :], kp_t, k1c[cur, :], vp_t, v1c[cur, :], bias)
        obuf[0, cur, :] = o_t
        lbuf[0, cur, :] = lse_t
    for pat, (d, q, kc, kp, vc, vp) in enumerate(((4, q4, k4c, k4p, v4c, v4p), (16, q16, k16c, k16p, v16c, v16p)), 1):
        nblk = SUPER // (BLOCK * d)
        for r in range(d):
            cols = pl.ds(r * LANES, LANES)
            for jb in range(nblk):
                cur = pl.ds(jb * BLOCK, BLOCK)
                if jb == 0:
                    kp_t, vp_t, bias = kp[:, cols], vp[:, cols], bias_first
                else:
                    prev = pl.ds((jb - 1) * BLOCK, BLOCK)
                    kp_t, vp_t, bias = kc[prev, cols], vc[prev, cols], bias_full
                o_t, lse_t = tile(q[cur, cols], kp_t, kc[cur, cols], vp_t, vc[cur, cols], bias)
                rows = pl.ds(jb * BLOCK * d + r, BLOCK, stride=d)
                obuf[pat, rows, :] = o_t
                lbuf[pat, rows, :] = lse_t
    for jb in range(SUPER // BLOCK):
        cur = pl.ds(jb * BLOCK, BLOCK)
        l0, l1, l2 = lbuf[0, cur, :], lbuf[1, cur, :], lbuf[2, cur, :]
        top = jnp.maximum(jnp.maximum(l0, l1), l2)
        w0, w1, w2 = jnp.exp(l0 - top), jnp.exp(l1 - top), jnp.exp(l2 - top)
        mixed = (w0 * obuf[0, cur, :] + w1 * obuf[1, cur, :] + w2 * obuf[2, cur, :]) * (1.0 / (w0 + w1 + w2))
        o_ref[cur, :] = mixed.astype(BF16)


def _band_bias():
    qi = jnp.arange(BLOCK)[:, None]
    kj = jnp.arange(2 * BLOCK)[None, :]
    dist = qi + BLOCK - kj
    in_band = (dist >= 0) & (dist <= BLOCK)
    full = jnp.where(in_band, 0.0, NEG_INF).astype(F32)
    first = jnp.where(in_band & (kj >= BLOCK), 0.0, NEG_INF).astype(F32)
    return jnp.stack([jnp.tile(full, (2, 1)), jnp.tile(first, (2, 1))])


def _attn(q1, k1, v1, q4, k4, v4, q16, k16, v16, B, S):
    nsup = S // SUPER

    def specs(d):
        rows = SUPER // d
        per = rows // BLOCK
        cur = pl.BlockSpec((None, None, rows, d * LANES), lambda b, hp, c: (b, hp, c, 0))
        prev = pl.BlockSpec((None, None, BLOCK, d * LANES),
                            lambda b, hp, c: (b, hp, jnp.maximum(per * c - 1, 0), 0))
        return [cur, cur, prev, cur, prev]

    return pl.pallas_call(
        _attn_kernel,
        grid=(B, N_PAIRS, nsup),
        in_specs=specs(1) + specs(4) + specs(16) + [
            pl.BlockSpec((2, 2 * BLOCK, 2 * BLOCK), lambda b, hp, c: (0, 0, 0))],
        out_specs=pl.BlockSpec((None, SUPER, LANES), lambda b, hp, c: (b, c, hp)),
        out_shape=jax.ShapeDtypeStruct((B, S, ATTN_WIDTH), BF16),
        scratch_shapes=[pltpu.VMEM((3, SUPER, LANES), F32)] * 2,
        compiler_params=_cparams(("parallel", "parallel", "parallel")),
        name="dilated_attn",
    )(q1, k1, k1, v1, v1, q4, k4, k4, v4, v4, q16, k16, k16, v16, v16, _band_bias())


def _memkv_kernel(mem_ref, mn_ref, wkv_ref, gk_ref, o256_ref, km_ref, vm_ref):
    m = mem_ref[...]
    ms = jnp.mean(m * m, axis=-1, keepdims=True)
    mn = (m * lax.rsqrt(ms + EPS) * mn_ref[...]).astype(BF16)
    kv = jnp.dot(mn, wkv_ref[...], preferred_element_type=F32)
    km_ref[...] = _head_norm(kv[:, :MEM_WIDTH], o256_ref, gk_ref).astype(BF16)
    vm_ref[...] = kv[:, MEM_WIDTH:].astype(BF16)


def _memkv(mem, mem_norm, w_mem_kv, gmk, B):
    const = lambda b: (0, 0)
    return pl.pallas_call(
        _memkv_kernel,
        grid=(B,),
        in_specs=[pl.BlockSpec((None, N_MEM, D_MODEL), lambda b: (b, 0, 0)),
                  pl.BlockSpec((1, D_MODEL), const),
                  pl.BlockSpec((D_MODEL, 2 * MEM_WIDTH), const),
                  pl.BlockSpec((1, MEM_WIDTH), const),
                  pl.BlockSpec((MEM_WIDTH, MEM_WIDTH), const)],
        out_specs=[pl.BlockSpec((None, N_MEM, MEM_WIDTH), lambda b: (b, 0, 0))] * 2,
        out_shape=[jax.ShapeDtypeStruct((B, N_MEM, MEM_WIDTH), BF16)] * 2,
        compiler_params=_cparams(("parallel",)),
        name="memkv",
    )(mem, mem_norm, w_mem_kv, gmk, _group_ones(MEM_WIDTH))


def _mix_kernel(tiles_per_seq, ya_ref, u_ref, uh_ref, qm_ref, km_ref, vm_ref, x_ref, pp_ref, ps_ref,
                wo_ref, fn_ref, wr_ref, br_ref, ltri_ref,
                x2_ref, h2p_ref, rinfo_ref, rinfot_ref, cnt_ref, base_ref):
    i = pl.program_id(0)
    tm = x_ref.shape[0]
    seq_tile = i % tiles_per_seq

    @pl.when(i == 0)
    def _():
        base_ref[...] = jnp.zeros_like(base_ref)

    u = u_ref[...]
    halo = jnp.where(seq_tile == 0, 0.0, uh_ref[...])
    uu = jnp.concatenate([halo, u], axis=0)
    a1 = uu[1:] + uu[:-1]
    a2 = a1[2:] + a1[:-2]
    a3 = a2[4:] + a2[:-4]
    a4 = a3[8:] + a3[:-8]
    lane_p = lax.broadcasted_iota(I32, (tm, POOL_WIDTH), 1)
    g0, g1, g2 = lane_p < 64, lane_p < 128, lane_p < 192
    wsum = jnp.where(g0, a1[15:], jnp.where(g1, a2[13:], jnp.where(g2, a3[9:], a4[1:])))
    wlen = jnp.where(g0, 2.0, jnp.where(g1, 4.0, jnp.where(g2, 8.0, 16.0)))
    tpos = seq_tile * tm + lax.broadcasted_iota(I32, (tm, POOL_WIDTH), 0) + 1
    cnt = jnp.minimum(tpos.astype(F32), wlen)
    pooled = wsum / cnt - u
    y_pool = jnp.dot(pooled.astype(BF16), pp_ref[...], preferred_element_type=F32) * ps_ref[...]

    lane = lax.broadcasted_iota(I32, (tm, LANES), 1)
    is_a = lane < HEAD_DIM
    lane_row = lax.broadcasted_iota(I32, (1, LANES), 1)
    mask_a = jnp.where(lane_row < HEAD_DIM, 1.0, 0.0).astype(BF16)
    mask_b = jnp.where(lane_row < HEAD_DIM, 0.0, 1.0).astype(BF16)
    y_mem = []
    for pr in range(MEM_WIDTH // LANES):
        cols = slice(pr * LANES, (pr + 1) * LANES)
        qp, kp, vp = qm_ref[:, cols], km_ref[:, cols], vm_ref[:, cols]
        outs = []
        for msk in (mask_a, mask_b):
            s = lax.dot_general(qp * msk, kp, (((1,), (1,)), ((), ())), preferred_element_type=F32)
            m = jnp.max(s, axis=-1, keepdims=True)
            p = jnp.exp(s - m)
            p = p / jnp.sum(p, axis=-1, keepdims=True)
            outs.append(jnp.dot(p.astype(BF16), vp, preferred_element_type=F32))
        y_mem.append(jnp.where(is_a, outs[0], outs[1]))

    rest = jnp.concatenate([y_pool] + y_mem, axis=1).astype(BF16)
    proj = jnp.dot(ya_ref[...], wo_ref[:ATTN_WIDTH, :], preferred_element_type=F32)
    proj += jnp.dot(rest, wo_ref[ATTN_WIDTH:, :], preferred_element_type=F32)
    x2 = x_ref[...] + proj
    x2_ref[...] = x2

    ms = jnp.mean(x2 * x2, axis=-1, keepdims=True)
    h2 = (x2 * lax.rsqrt(ms + EPS) * fn_ref[...]).astype(BF16)
    bits = lax.bitcast_convert_type(h2.astype(F32), U32)
    h2p_ref[...] = bits[:, :PACKED] | (bits[:, PACKED:] >> 16)

    logits = jnp.dot(h2, wr_ref[...], preferred_element_type=F32) + br_ref[...]
    ninf = -jnp.inf
    lane_f = lane.astype(F32)
    grp_f = ((lane - EXPERT_LANE0) >> 3).astype(F32)
    is_g = lane < N_GROUPS
    lg = jnp.where(is_g, logits, ninf)
    mg = jnp.max(lg, axis=-1, keepdims=True)
    g_sel = jnp.min(jnp.where(lg == mg, lane_f, float(LANES)), axis=-1, keepdims=True)
    w_g = 1.0 / jnp.sum(jnp.where(is_g, jnp.exp(logits - mg), 0.0), axis=-1, keepdims=True)
    in_grp = (lane >= EXPERT_LANE0) & (grp_f == g_sel)
    le = jnp.where(in_grp, logits, ninf)
    v1 = jnp.max(le, axis=-1, keepdims=True)
    i1 = jnp.min(jnp.where(le == v1, lane_f, float(2 * LANES)), axis=-1, keepdims=True)
    le2 = jnp.where(lane_f == i1, ninf, le)
    v2 = jnp.max(le2, axis=-1, keepdims=True)
    i2 = jnp.min(jnp.where(le2 == v2, lane_f, float(2 * LANES)), axis=-1, keepdims=True)
    e21 = jnp.exp(v2 - v1)
    gate1 = w_g / (1.0 + e21)
    gate2 = w_g * e21 / (1.0 + e21)

    hot1 = lane_f == i1
    hot2 = lane_f == i2
    onehot = jnp.where(hot1 | hot2, 1.0, 0.0)
    prefix = jnp.dot(ltri_ref[...], onehot.astype(BF16), preferred_element_type=F32)
    tot = prefix + base_ref[...]
    rank1 = jnp.sum(jnp.where(hot1, tot, 0.0), axis=-1, keepdims=True)
    rank2 = jnp.sum(jnp.where(hot2, tot, 0.0), axis=-1, keepdims=True)
    base_new = base_ref[...] + jnp.sum(onehot, axis=0, keepdims=True)
    base_ref[...] = base_new
    cnt_ref[...] = base_new

    e1 = i1 - float(EXPERT_LANE0)
    e2 = i2 - float(EXPERT_LANE0)
    rinfo = jnp.where(lane == 0, e1, jnp.where(lane == 1, e2, jnp.where(lane == 2, rank1, jnp.where(
        lane == 3, rank2, jnp.where(lane == 4, gate1, jnp.where(lane == 5, gate2, 0.0))))))
    rinfo_ref[...] = rinfo
    rinfot_ref[...] = jnp.transpose(rinfo)[:8, :]


def _mix(ya, u, qm, km, vm, x2d, pool_bd, pool_scale, w_out, ffn_norm, w_r, b_r, B, S):
    T = B * S
    tm = TM_MIX
    tps = S // tm
    hb = tm // HALO
    const = lambda i: (0, 0)
    ltri = (jnp.arange(tm)[:, None] > jnp.arange(tm)[None, :]).astype(BF16)
    return pl.pallas_call(
        functools.partial(_mix_kernel, tps),
        grid=(T // tm,),
        in_specs=[pl.BlockSpec((tm, ATTN_WIDTH), lambda i: (i, 0)),
                  pl.BlockSpec((tm, POOL_WIDTH), lambda i: (i, 0)),
                  pl.BlockSpec((HALO, POOL_WIDTH), lambda i: (jnp.maximum(i * hb - 1, 0), 0)),
                  pl.BlockSpec((tm, MEM_WIDTH), lambda i: (i, 0)),
                  pl.BlockSpec((None, N_MEM, MEM_WIDTH), lambda i: (i // tps, 0, 0)),
                  pl.BlockSpec((None, N_MEM, MEM_WIDTH), lambda i: (i // tps, 0, 0)),
                  pl.BlockSpec((tm, D_MODEL), lambda i: (i, 0)),
                  pl.BlockSpec((POOL_WIDTH, POOL_WIDTH), const),
                  pl.BlockSpec((1, POOL_WIDTH), const),
                  pl.BlockSpec((D_MODEL, D_MODEL), const),
                  pl.BlockSpec((1, D_MODEL), const),
                  pl.BlockSpec((D_MODEL, LANES), const),
                  pl.BlockSpec((1, LANES), const),
                  pl.BlockSpec((tm, tm), const)],
        out_specs=[pl.BlockSpec((tm, D_MODEL), lambda i: (i, 0)),
                   pl.BlockSpec((tm, PACKED), lambda i: (i, 0)),
                   pl.BlockSpec((tm, LANES), lambda i: (i, 0)),
                   pl.BlockSpec((8, tm), lambda i: (0, i)),
                   pl.BlockSpec((1, LANES), const)],
        out_shape=[jax.ShapeDtypeStruct((T, D_MODEL), F32),
                   jax.ShapeDtypeStruct((T, PACKED), U32),
                   jax.ShapeDtypeStruct((T, LANES), F32),
                   jax.ShapeDtypeStruct((8, T), F32),
                   jax.ShapeDtypeStruct((1, LANES), F32)],
        scratch_shapes=[pltpu.VMEM((1, LANES), F32)],
        compiler_params=_cparams(("arbitrary",)),
        name="mix_router",
    )(ya, u, u, qm, km, vm, x2d, pool_bd, pool_scale, w_out, ffn_norm, w_r, b_r, ltri)


def _dest_kernel(rt_ref, ps_ref, d_ref):
    tn = rt_ref.shape[1]
    sub = lax.broadcasted_iota(I32, (N_EXPERTS, tn), 0).astype(F32)
    ps = ps_ref[...]
    rows = []
    for k in range(2):
        e = rt_ref[k:k + 1, :]
        start = jnp.sum(jnp.where(sub == e, ps, 0.0), axis=0, keepdims=True)
        rows.append(start + rt_ref[2 + k:3 + k, :])
    word = rows[0].astype(I32) | (rows[1].astype(I32) << 16)
    d_ref[...] = jnp.concatenate([word, jnp.zeros((7, tn), I32)], axis=0)


def _dest(rinfot, pstart_col, T):
    tn = 2048
    return pl.pallas_call(
        _dest_kernel,
        grid=(T // tn,),
        in_specs=[pl.BlockSpec((8, tn), lambda i: (0, i)),
                  pl.BlockSpec((N_EXPERTS, 1), lambda i: (0, 0))],
        out_specs=pl.BlockSpec((8, tn), lambda i: (0, i)),
        out_shape=jax.ShapeDtypeStruct((8, T), I32),
        compiler_params=_cparams(("parallel",)),
        name="dest_rows",
    )(rinfot, pstart_col)


def _rowmap_kernel(d_ref, fill_hbm, inv_ref, sem):
    T = d_ref.shape[0]
    cp = pltpu.make_async_copy(fill_hbm, inv_ref, sem)
    cp.start()
    cp.wait()
    unroll = 8

    def body(g, carry):
        t0 = g * unroll
        words = [d_ref[t0 + u] for u in range(unroll)]
        for u, w in enumerate(words):
            inv_ref[w & 0xFFFF] = t0 + u
            inv_ref[lax.shift_right_logical(w, 16)] = T + t0 + u
        return carry

    lax.fori_loop(0, T // unroll, body, 0)


def _rowmap(dword, R):
    assert R <= 1 << 16
    return pl.pallas_call(
        _rowmap_kernel,
        grid_spec=pltpu.PrefetchScalarGridSpec(
            num_scalar_prefetch=1,
            grid=(1,),
            in_specs=[pl.BlockSpec(memory_space=pl.ANY)],
            out_specs=pl.BlockSpec(memory_space=pltpu.SMEM),
            scratch_shapes=[pltpu.SemaphoreType.DMA(())]),
        out_shape=jax.ShapeDtypeStruct((R,), I32),
        compiler_params=_cparams(("arbitrary",)),
        name="rowmap",
    )(dword, jnp.zeros((R,), I32))


def _row_loop(n, body):
    shift = ROW_UNROLL.bit_length() - 1

    def group(g, carry):
        for u in range(ROW_UNROLL):
            body(g * ROW_UNROLL + u)
        return carry

    def rest(i, carry):
        body(((n >> shift) << shift) + i)
        return carry

    lax.fori_loop(0, n >> shift, group, 0)
    lax.fori_loop(0, n & (ROW_UNROLL - 1), rest, 0)


def _moe_kernel(be_ref, nu_ref, first_ref, nxt_ref, ws_ref, valid_ref, inv_ref,
                h_hbm, w1_hbm, w3_hbm, w2_hbm, o_hbm,
                tab, xbuf, ybuf, w1s, w3s, w2s, w1b, w3b, w2b, tsem, ssem, wsem):
    j = pl.program_id(0)
    nu = nu_ref[0]
    T = h_hbm.shape[0]
    tr = TR_MOE
    n_tiles = inv_ref.shape[0] // tr

    def tile_rows(ref, row, n=1):
        return ref.at[pl.ds(pl.multiple_of(row * ROW_SUB, ROW_SUB), n * ROW_SUB)]

    def gather_tile(tile, slot):
        for i in range(tr):
            v = inv_ref[tile * tr + i]
            tok = (v & (T - 1)) if T & (T - 1) == 0 else jnp.where(v >= T, v - T, v)
            xbuf[pl.ds(slot * tr + i, 1), :] = tab[pl.ds(tok, 1), :]

    def wait_rows(n, done):
        @pl.when(n == tr)
        def _():
            done(tr).wait()

        @pl.when(n != tr)
        def _():
            lax.fori_loop(0, n >> 3, lambda g, c: (done(8).wait(), c)[1], 0)
            lax.fori_loop(0, n & 7, lambda g, c: (done(1).wait(), c)[1], 0)

    def scatter_row(tile, slot, i):
        v = inv_ref[tile * tr + i]
        return pltpu.make_async_copy(tile_rows(ybuf, slot * tr + i), tile_rows(o_hbm, v), ssem.at[slot])

    def scatter_wait(slot, n):
        wait_rows(n, lambda rows: pltpu.make_async_copy(
            tile_rows(ybuf, slot * tr, rows), tile_rows(o_hbm, 0, rows), ssem.at[slot]))

    def weight_copies(e, slot):
        return [pltpu.make_async_copy(src.at[e], dst.at[slot], wsem.at[slot])
                for src, dst in ((w1_hbm, w1s), (w3_hbm, w3s), (w2_hbm, w2s))]

    @pl.when(j < nu)
    def _():
        s = j & 1
        e = be_ref[j]

        @pl.when(j == 0)
        def _():
            table = pltpu.make_async_copy(h_hbm, tab, tsem)
            table.start()
            for cp in weight_copies(e, 0):
                cp.start()
            table.wait()
            gather_tile(0, 0)

        @pl.when(first_ref[j] == 1)
        def _():
            wslot = ws_ref[j]
            for cp in weight_copies(e, wslot):
                cp.wait()
            e_next = nxt_ref[j]

            @pl.when(e_next >= 0)
            def _():
                for cp in weight_copies(e_next, 1 - wslot):
                    cp.start()

            w1b[...] = w1s[wslot].astype(BF16)
            w3b[...] = w3s[wslot].astype(BF16)
            w2b[...] = w2s[wslot].astype(BF16)

        @pl.when(j >= 2)
        def _():
            scatter_wait(s, valid_ref[j - 2])

        w = xbuf[pl.ds(pl.multiple_of(s * tr, tr), tr), :]
        lo = lax.bitcast_convert_type(w & jnp.uint32(0xFFFF0000), F32).astype(BF16)
        hi = lax.bitcast_convert_type(w << 16, F32).astype(BF16)
        gather_tile(jnp.minimum(j + 1, n_tiles - 1), 1 - s)
        halves = [slice(k * (EXPERT_HIDDEN // 2), (k + 1) * (EXPERT_HIDDEN // 2)) for k in range(2)]

        def up(wb, cols):
            return (jnp.dot(lo, wb[:PACKED, cols], preferred_element_type=F32)
                    + jnp.dot(hi, wb[PACKED:, cols], preferred_element_type=F32))

        ab = [(up(w1b, cols), up(w3b, cols)) for cols in halves]
        y = None
        for cols, (a, b) in zip(halves, ab):
            hmid = (a / (1.0 + jnp.exp(-a)) * b).astype(BF16)
            part = jnp.dot(hmid, w2b[cols, :], preferred_element_type=F32)
            y = part if y is None else y + part
        base = pl.multiple_of(s * (tr * ROW_SUB), tr * ROW_SUB)
        for c in range(ROW_SUB):
            ybuf[pl.ds(base + c, tr, stride=ROW_SUB), :] = y[:, c * LANES:(c + 1) * LANES]

        _row_loop(valid_ref[j], lambda i: scatter_row(j, s, i).start())

        @pl.when(j == nu - 1)
        def _():
            scatter_wait(s, valid_ref[j])

            @pl.when(j >= 1)
            def _():
                scatter_wait(1 - s, valid_ref[j - 1])


def _moe(blk_exp, n_used, first, nxt, wslot, valid, inv, h2p, w1, w3, w2):
    T = h2p.shape[0]
    tr = TR_MOE
    n_tiles = blk_exp.shape[0]
    any_spec = pl.BlockSpec(memory_space=pl.ANY)
    return pl.pallas_call(
        _moe_kernel,
        grid_spec=pltpu.PrefetchScalarGridSpec(
            num_scalar_prefetch=7,
            grid=(n_tiles,),
            in_specs=[any_spec] * 4,
            out_specs=any_spec,
            scratch_shapes=[pltpu.VMEM((T, PACKED), U32),
                            pltpu.VMEM((2 * tr, PACKED), U32),
                            pltpu.VMEM((2 * tr * ROW_SUB, LANES), F32),
                            pltpu.VMEM((2, D_MODEL, EXPERT_HIDDEN), F32),
                            pltpu.VMEM((2, D_MODEL, EXPERT_HIDDEN), F32),
                            pltpu.VMEM((2, EXPERT_HIDDEN, D_MODEL), F32),
                            pltpu.VMEM((D_MODEL, EXPERT_HIDDEN), BF16),
                            pltpu.VMEM((D_MODEL, EXPERT_HIDDEN), BF16),
                            pltpu.VMEM((EXPERT_HIDDEN, D_MODEL), BF16),
                            pltpu.SemaphoreType.DMA(()),
                            pltpu.SemaphoreType.DMA((2,)),
                            pltpu.SemaphoreType.DMA((2,))]),
        out_shape=jax.ShapeDtypeStruct((2 * T * ROW_SUB, LANES), F32),
        compiler_params=pltpu.CompilerParams(dimension_semantics=("arbitrary",), vmem_limit_bytes=MOE_VMEM_LIMIT),
        name="moe_experts",
    )(blk_exp, n_used, first, nxt, wslot, valid, inv, h2p, w1, w3, w2)


def _combine_kernel(x2_ref, rinfo_ref, y1_ref, y2_ref, o_ref):
    g1 = rinfo_ref[:, 4:5]
    g2 = rinfo_ref[:, 5:6]
    tc = x2_ref.shape[0]
    for c in range(ROW_SUB):
        cols = slice(c * LANES, (c + 1) * LANES)
        rows = pl.ds(c, tc, stride=ROW_SUB)
        o_ref[:, cols] = x2_ref[:, cols] + (g1 * y1_ref[rows, :] + g2 * y2_ref[rows, :])


def _combine(x2, rinfo, y2slot):
    T = x2.shape[0]
    tc = TC_COMBINE
    return pl.pallas_call(
        _combine_kernel,
        grid=(T // tc,),
        in_specs=[pl.BlockSpec((tc, D_MODEL), lambda i: (i, 0)),
                  pl.BlockSpec((tc, LANES), lambda i: (i, 0)),
                  pl.BlockSpec((tc * ROW_SUB, LANES), lambda i: (i, 0)),
                  pl.BlockSpec((tc * ROW_SUB, LANES), lambda i: (T // tc + i, 0))],
        out_specs=pl.BlockSpec((tc, D_MODEL), lambda i: (i, 0)),
        out_shape=jax.ShapeDtypeStruct((T, D_MODEL), F32),
        compiler_params=_cparams(("parallel",)),
        name="combine",
    )(x2, rinfo, y2slot, y2slot)


def _layer(x, mem, attn_norm, w_in, q_norm, k_norm, pool_proj, pool_scale, mem_norm, w_mem_kv,
           mq_norm, mk_norm, w_out, ffn_norm, w_group, b_group, w_router, b_router, w1, w3, w2):
    B, S, D = x.shape
    T = B * S
    assert D == D_MODEL and S % SUPER == 0 and T % TM_PROJ == 0
    x2d = x.reshape(T, D)
    row = lambda v: v.reshape(1, -1).astype(F32)
    scale = HEAD_DIM ** -0.5
    gq = row(jnp.tile(q_norm, ATTN_WIDTH // HEAD_DIM) * scale)
    gk = row(jnp.tile(k_norm, ATTN_WIDTH // HEAD_DIM))
    gmq = row(jnp.tile(mq_norm, MEM_WIDTH // HEAD_DIM) * scale)
    gmk = row(jnp.tile(mk_norm, MEM_WIDTH // HEAD_DIM))

    (q1, k1, v1, q4, k4, v4, q16, k16, v16, u, qm) = _proj(
        x2d, row(attn_norm), w_in.astype(BF16), gq, gk, gmq, B, S)
    ya = _attn(q1, k1, v1, q4, k4, v4, q16, k16, v16, B, S).reshape(T, ATTN_WIDTH)
    km, vm = _memkv(mem, row(mem_norm), w_mem_kv.astype(BF16), gmk, B)

    pool_bd = jax.scipy.linalg.block_diag(*[pool_proj[g] for g in range(pool_proj.shape[0])]).astype(BF16)
    w_r = jnp.zeros((D, LANES), F32)
    w_r = w_r.at[:, :N_GROUPS].set(w_group)
    w_r = w_r.at[:, EXPERT_LANE0:].set(jnp.transpose(w_router, (1, 0, 2)).reshape(D, N_EXPERTS))
    b_r = jnp.zeros((1, LANES), F32).at[0, :N_GROUPS].set(b_group).at[0, EXPERT_LANE0:].set(b_router.reshape(-1))
    x2, h2p, rinfo, rinfot, counts = _mix(ya, u, qm, km, vm, x2d, pool_bd, row(pool_scale),
                                          w_out.astype(BF16), row(ffn_norm), w_r.astype(BF16), b_r, B, S)

    R, pstart, sched = _tile_schedule(counts, T)
    dest = _dest(rinfot, pstart.astype(F32).reshape(N_EXPERTS, 1), T)
    inv = _rowmap(dest[0], R)
    y2slot = _moe(*sched, inv, h2p, w1, w3, w2)
    out = _combine(x2, rinfo, y2slot)
    return out.reshape(B, S, D)


def _tile_schedule(counts, T):
    tr = TR_MOE
    R = 2 * T + N_EXPERTS * tr
    n_tiles = R // tr
    cnt = counts[0, EXPERT_LANE0:].astype(I32)
    padded = ((cnt + tr - 1) // tr) * tr
    pend = jnp.cumsum(padded)
    pstart = pend - padded
    n_used = (pend[-1] // tr).astype(I32).reshape(1)
    tiles = jnp.arange(n_tiles, dtype=I32)
    tile_row = jnp.minimum(tiles, n_used[0] - 1) * tr
    blk_exp = jnp.minimum(jnp.sum(tile_row[:, None] >= pend[None, :], axis=1), N_EXPERTS - 1).astype(I32)
    in_use = tiles < n_used[0]
    first = in_use & ((tiles == 0) | (blk_exp != jnp.roll(blk_exp, 1)))
    run_idx = jnp.cumsum(first.astype(I32)) - 1
    used = cnt > 0
    exp_of_run = jnp.argsort(jnp.logical_not(used), stable=True).astype(I32)
    nxt = jnp.where(run_idx + 1 < jnp.sum(used), exp_of_run[jnp.clip(run_idx + 1, 0, N_EXPERTS - 1)], -1)
    valid = jnp.where(in_use, jnp.clip(cnt[blk_exp] - (tiles * tr - pstart[blk_exp]), 0, tr), 0)
    sched = (blk_exp, n_used, first.astype(I32), nxt.astype(I32), (run_idx & 1).astype(I32), valid.astype(I32))
    return R, pstart, sched


def kernel(x, mem, attn_norm, w_in, q_norm, k_norm, pool_proj, pool_scale, mem_norm, w_mem_kv, mq_norm, mk_norm,
           w_out, ffn_norm, w_group, b_group, w_router, b_router, w1, w3, w2):
    for l in range(attn_norm.shape[0]):
        x = _layer(x, mem, attn_norm[l], w_in[l], q_norm[l], k_norm[l], pool_proj[l], pool_scale[l],
                   mem_norm[l], w_mem_kv[l], mq_norm[l], mk_norm[l], w_out[l], ffn_norm[l],
                   w_group[l], b_group[l], w_router[l], b_router[l], w1[l], w3[l], w2[l])
    return x
```

```python
import functools

import jax
import jax.numpy as jnp
from jax import lax
from jax.experimental import pallas as pl
from jax.experimental.pallas import tpu as pltpu

F32 = jnp.float32
BF16 = jnp.bfloat16
I32 = jnp.int32
U32 = jnp.uint32

D_MODEL = 1024
HEAD_DIM = 64
ATTN_WIDTH = 512
POOL_WIDTH = 256
MEM_WIDTH = 256
N_MEM = 256
IN_WIDTH = 3 * ATTN_WIDTH + POOL_WIDTH + MEM_WIDTH
N_GROUPS = 8
EXPERTS_PER_GROUP = 8
N_EXPERTS = 64
EXPERT_HIDDEN = 512
EPS = 1e-6
NEG_INF = -1e30

LANES = 128
N_PAIRS = ATTN_WIDTH // LANES
BLOCK = 128
SUPER = 16 * BLOCK
HALO = 16
ROW_SUB = D_MODEL // LANES
PACKED = D_MODEL // 2

TM_PROJ = 1024
TM_MIX = 1024
TR_MOE = 256
ROW_UNROLL = 16
TC_COMBINE = 1024
EXPERT_LANE0 = 64

VMEM_LIMIT = 48 * 1024 * 1024
MOE_VMEM_LIMIT = 56 * 1024 * 1024


def _cparams(sem):
    return pltpu.CompilerParams(dimension_semantics=sem, vmem_limit_bytes=VMEM_LIMIT)


def _group_ones(n):
    g = jnp.arange(n) // HEAD_DIM
    return (g[:, None] == g[None, :]).astype(BF16)


def _head_norm(z, ones_ref, gain_ref):
    ss = jnp.dot((z * z).astype(BF16), ones_ref[...], preferred_element_type=F32)
    return z * lax.rsqrt(ss * (1.0 / HEAD_DIM) + EPS) * gain_ref[...]


def _proj_kernel(x_ref, an_ref, win_ref, gq_ref, gk_ref, gm_ref, o512_ref, o256_ref,
                 q1_ref, k1_ref, v1_ref, q4_ref, k4_ref, v4_ref, q16_ref, k16_ref, v16_ref,
                 u_ref, qm_ref, zs_ref, z4_ref):
    tm = x_ref.shape[0]
    x = x_ref[...]
    ms = jnp.mean(x * x, axis=-1, keepdims=True)
    h = (x * lax.rsqrt(ms + EPS) * an_ref[...]).astype(BF16)
    a0, a1, a2, a3 = ATTN_WIDTH, 2 * ATTN_WIDTH, 3 * ATTN_WIDTH, 3 * ATTN_WIDTH + POOL_WIDTH

    def cols(lo, hi):
        return jnp.dot(h, win_ref[:, lo:hi], preferred_element_type=F32)

    groups = ((lambda: _head_norm(cols(0, a0), o512_ref, gq_ref), q1_ref, q4_ref, q16_ref),
              (lambda: _head_norm(cols(a0, a1), o512_ref, gk_ref), k1_ref, k4_ref, k16_ref),
              (lambda: cols(a1, a2), v1_ref, v4_ref, v16_ref))
    for make, o1, o4, o16 in groups:
        val = make()
        for hp in range(N_PAIRS):
            pair = val[:, hp * LANES:(hp + 1) * LANES]
            zs_ref[hp] = pair
            o1[hp] = pair.astype(BF16)
        q4n = tm // 4
        for hp in range(N_PAIRS):
            for r4 in range(4):
                rows = zs_ref[hp, pl.ds(r4, q4n, stride=4), :]
                z4_ref[hp, pl.ds(r4 * q4n, q4n), :] = rows
                o4[hp, :, r4 * LANES:(r4 + 1) * LANES] = rows.astype(BF16)
            for r16 in range(16):
                r4, a = r16 % 4, r16 // 4
                rows = z4_ref[hp, pl.ds(r4 * q4n + a, tm // 16, stride=4), :]
                o16[hp, :, r16 * LANES:(r16 + 1) * LANES] = rows.astype(BF16)
    u_ref[...] = cols(a2, a3)
    qm_ref[...] = _head_norm(cols(a3, IN_WIDTH), o256_ref, gm_ref).astype(BF16)


def _proj(x2d, attn_norm, w_in, gq, gk, gm, B, S):
    T = B * S
    tm = TM_PROJ
    nj = S // tm
    const = lambda i: (0, 0)

    def lay(d):
        return jax.ShapeDtypeStruct((B, N_PAIRS, S // d, d * LANES), BF16)

    def lay_spec(d):
        return pl.BlockSpec((None, N_PAIRS, tm // d, d * LANES), lambda i: (i // nj, 0, i % nj, 0))

    out_shape = [lay(1)] * 3 + [lay(4)] * 3 + [lay(16)] * 3 + [
        jax.ShapeDtypeStruct((T, POOL_WIDTH), F32), jax.ShapeDtypeStruct((T, MEM_WIDTH), BF16)]
    out_specs = [lay_spec(1)] * 3 + [lay_spec(4)] * 3 + [lay_spec(16)] * 3 + [
        pl.BlockSpec((tm, POOL_WIDTH), lambda i: (i, 0)), pl.BlockSpec((tm, MEM_WIDTH), lambda i: (i, 0))]
    return pl.pallas_call(
        _proj_kernel,
        grid=(T // tm,),
        in_specs=[pl.BlockSpec((tm, D_MODEL), lambda i: (i, 0)),
                  pl.BlockSpec((1, D_MODEL), const),
                  pl.BlockSpec((D_MODEL, IN_WIDTH), const),
                  pl.BlockSpec((1, ATTN_WIDTH), const),
                  pl.BlockSpec((1, ATTN_WIDTH), const),
                  pl.BlockSpec((1, MEM_WIDTH), const),
                  pl.BlockSpec((ATTN_WIDTH, ATTN_WIDTH), const),
                  pl.BlockSpec((MEM_WIDTH, MEM_WIDTH), const)],
        out_specs=out_specs,
        out_shape=out_shape,
        scratch_shapes=[pltpu.VMEM((N_PAIRS, tm, LANES), F32)] * 2,
        compiler_params=_cparams(("parallel",)),
        name="proj",
    )(x2d, attn_norm, w_in, gq, gk, gm, _group_ones(ATTN_WIDTH), _group_ones(MEM_WIDTH))


def _attn_kernel(q1, k1c, k1p, v1c, v1p, q4, k4c, k4p, v4c, v4p, q16, k16c, k16p, v16c, v16p,
                 bias_ref, o_ref, obuf, lbuf):
    c = pl.program_id(2)
    lane = lax.broadcasted_iota(I32, (BLOCK, LANES), 1)
    is_a = lane < HEAD_DIM
    lane_row = lax.broadcasted_iota(I32, (1, LANES), 1)
    mask_a = jnp.where(lane_row < HEAD_DIM, 1.0, 0.0).astype(BF16)
    mask_b = jnp.where(lane_row < HEAD_DIM, 0.0, 1.0).astype(BF16)
    bias_full = bias_ref[0]
    bias_first = jnp.where(c > 0, bias_full, bias_ref[1])

    def tile(q_t, kp_t, kc_t, vp_t, vc_t, bias):
        lhs = jnp.concatenate([q_t * mask_a, q_t * mask_b], axis=0)
        keys = jnp.concatenate([kp_t, kc_t], axis=0)
        s = lax.dot_general(lhs, keys, (((1,), (1,)), ((), ())), preferred_element_type=F32) + bias
        m = jnp.max(s, axis=-1, keepdims=True)
        p = jnp.exp(s - m)
        l = jnp.sum(p, axis=-1, keepdims=True)
        vals = jnp.concatenate([vp_t, vc_t], axis=0)
        pv = jnp.dot(p.astype(BF16), vals, preferred_element_type=F32) * (1.0 / l)
        lse = m + jnp.log(l)
        return jnp.where(is_a, pv[:BLOCK], pv[BLOCK:]), jnp.where(is_a, lse[:BLOCK], lse[BLOCK:])

    for jb in range(SUPER // BLOCK):
        cur = pl.ds(jb * BLOCK, BLOCK)
        if jb == 0:
            kp_t, vp_t, bias = k1p[...], v1p[...], bias_first
        else:
            prev = pl.ds((jb - 1) * BLOCK, BLOCK)
            kp_t, vp_t, bias = k1c[prev, :], v1c[prev, :], bias_full
        o_t, lse_t = tile(q1[cur, :], kp_t, k1c[cur, :], vp_t, v1c[cur, :], bias)
        obuf[0, cur, :] = o_t
        lbuf[0, cur, :] = lse_t
    for pat, (d, q, kc, kp, vc, vp) in enumerate(((4, q4, k4c, k4p, v4c, v4p), (16, q16, k16c, k16p, v16c, v16p)), 1):
        nblk = SUPER // (BLOCK * d)
        for r in range(d):
            cols = pl.ds(r * LANES, LANES)
            for jb in range(nblk):
                cur = pl.ds(jb * BLOCK, BLOCK)
                if jb == 0:
                    kp_t, vp_t, bias = kp[:, cols], vp[:, cols], bias_first
                else:
                    prev = pl.ds((jb - 1) * BLOCK, BLOCK)
                    kp_t, vp_t, bias = kc[prev, cols], vc[prev, cols], bias_full
                o_t, lse_t = tile(q[cur, cols], kp_t, kc[cur, cols], vp_t, vc[cur, cols], bias)
                rows = pl.ds(jb * BLOCK * d + r, BLOCK, stride=d)
                obuf[pat, rows, :] = o_t
                lbuf[pat, rows, :] = lse_t
    for jb in range(SUPER // BLOCK):
        cur = pl.ds(jb * BLOCK, BLOCK)
        l0, l1, l2 = lbuf[0, cur, :], lbuf[1, cur, :], lbuf[2, cur, :]
        top = jnp.maximum(jnp.maximum(l0, l1), l2)
        w0, w1, w2 = jnp.exp(l0 - top), jnp.exp(l1 - top), jnp.exp(l2 - top)
        mixed = (w0 * obuf[0, cur, :] + w1 * obuf[1, cur, :] + w2 * obuf[2, cur, :]) * (1.0 / (w0 + w1 + w2))
        o_ref[cur, :] = mixed.astype(BF16)


def _band_bias():
    qi = jnp.arange(BLOCK)[:, None]
    kj = jnp.arange(2 * BLOCK)[None, :]
    dist = qi + BLOCK - kj
    in_band = (dist >= 0) & (dist <= BLOCK)
    full = jnp.where(in_band, 0.0, NEG_INF).astype(F32)
    first = jnp.where(in_band & (kj >= BLOCK), 0.0, NEG_INF).astype(F32)
    return jnp.stack([jnp.tile(full, (2, 1)), jnp.tile(first, (2, 1))])


def _attn(q1, k1, v1, q4, k4, v4, q16, k16, v16, B, S):
    nsup = S // SUPER

    def specs(d):
        rows = SUPER // d
        per = rows // BLOCK
        cur = pl.BlockSpec((None, None, rows, d * LANES), lambda b, hp, c: (b, hp, c, 0))
        prev = pl.BlockSpec((None, None, BLOCK, d * LANES),
                            lambda b, hp, c: (b, hp, jnp.maximum(per * c - 1, 0), 0))
        return [cur, cur, prev, cur, prev]

    return pl.pallas_call(
        _attn_kernel,
        grid=(B, N_PAIRS, nsup),
        in_specs=specs(1) + specs(4) + specs(16) + [
            pl.BlockSpec((2, 2 * BLOCK, 2 * BLOCK), lambda b, hp, c: (0, 0, 0))],
        out_specs=pl.BlockSpec((None, SUPER, LANES), lambda b, hp, c: (b, c, hp)),
        out_shape=jax.ShapeDtypeStruct((B, S, ATTN_WIDTH), BF16),
        scratch_shapes=[pltpu.VMEM((3, SUPER, LANES), F32)] * 2,
        compiler_params=_cparams(("parallel", "parallel", "parallel")),
        name="dilated_attn",
    )(q1, k1, k1, v1, v1, q4, k4, k4, v4, v4, q16, k16, k16, v16, v16, _band_bias())


def _memkv_kernel(mem_ref, mn_ref, wkv_ref, gk_ref, o256_ref, km_ref, vm_ref):
    m = mem_ref[...]
    ms = jnp.mean(m * m, axis=-1, keepdims=True)
    mn = (m * lax.rsqrt(ms + EPS) * mn_ref[...]).astype(BF16)
    kv = jnp.dot(mn, wkv_ref[...], preferred_element_type=F32)
    km_ref[...] = _head_norm(kv[:, :MEM_WIDTH], o256_ref, gk_ref).astype(BF16)
    vm_ref[...] = kv[:, MEM_WIDTH:].astype(BF16)


def _memkv(mem, mem_norm, w_mem_kv, gmk, B):
    const = lambda b: (0, 0)
    return pl.pallas_call(
        _memkv_kernel,
        grid=(B,),
        in_specs=[pl.BlockSpec((None, N_MEM, D_MODEL), lambda b: (b, 0, 0)),
                  pl.BlockSpec((1, D_MODEL), const),
                  pl.BlockSpec((D_MODEL, 2 * MEM_WIDTH), const),
                  pl.BlockSpec((1, MEM_WIDTH), const),
                  pl.BlockSpec((MEM_WIDTH, MEM_WIDTH), const)],
        out_specs=[pl.BlockSpec((None, N_MEM, MEM_WIDTH), lambda b: (b, 0, 0))] * 2,
        out_shape=[jax.ShapeDtypeStruct((B, N_MEM, MEM_WIDTH), BF16)] * 2,
        compiler_params=_cparams(("parallel",)),
        name="memkv",
    )(mem, mem_norm, w_mem_kv, gmk, _group_ones(MEM_WIDTH))


def _mix_kernel(tiles_per_seq, ya_ref, u_ref, uh_ref, qm_ref, km_ref, vm_ref, x_ref, pp_ref, ps_ref,
                wo_ref, fn_ref, wr_ref, br_ref, ltri_ref,
                x2_ref, h2p_ref, rinfo_ref, rinfot_ref, cnt_ref, base_ref):
    i = pl.program_id(0)
    tm = x_ref.shape[0]
    seq_tile = i % tiles_per_seq

    @pl.when(i == 0)
    def _():
        base_ref[...] = jnp.zeros_like(base_ref)

    u = u_ref[...]
    halo = jnp.where(seq_tile == 0, 0.0, uh_ref[...])
    uu = jnp.concatenate([halo, u], axis=0)
    a1 = uu[1:] + uu[:-1]
    a2 = a1[2:] + a1[:-2]
    a3 = a2[4:] + a2[:-4]
    a4 = a3[8:] + a3[:-8]
    lane_p = lax.broadcasted_iota(I32, (tm, POOL_WIDTH), 1)
    g0, g1, g2 = lane_p < 64, lane_p < 128, lane_p < 192
    wsum = jnp.where(g0, a1[15:], jnp.where(g1, a2[13:], jnp.where(g2, a3[9:], a4[1:])))
    wlen = jnp.where(g0, 2.0, jnp.where(g1, 4.0, jnp.where(g2, 8.0, 16.0)))
    tpos = seq_tile * tm + lax.broadcasted_iota(I32, (tm, POOL_WIDTH), 0) + 1
    cnt = jnp.minimum(tpos.astype(F32), wlen)
    pooled = wsum / cnt - u
    y_pool = jnp.dot(pooled.astype(BF16), pp_ref[...], preferred_element_type=F32) * ps_ref[...]

    lane = lax.broadcasted_iota(I32, (tm, LANES), 1)
    is_a = lane < HEAD_DIM
    lane_row = lax.broadcasted_iota(I32, (1, LANES), 1)
    mask_a = jnp.where(lane_row < HEAD_DIM, 1.0, 0.0).astype(BF16)
    mask_b = jnp.where(lane_row < HEAD_DIM, 0.0, 1.0).astype(BF16)
    y_mem = []
    for pr in range(MEM_WIDTH // LANES):
        cols = slice(pr * LANES, (pr + 1) * LANES)
        qp, kp, vp = qm_ref[:, cols], km_ref[:, cols], vm_ref[:, cols]
        outs = []
        for msk in (mask_a, mask_b):
            s = lax.dot_general(qp * msk, kp, (((1,), (1,)), ((), ())), preferred_element_type=F32)
            m = jnp.max(s, axis=-1, keepdims=True)
            p = jnp.exp(s - m)
            p = p / jnp.sum(p, axis=-1, keepdims=True)
            outs.append(jnp.dot(p.astype(BF16), vp, preferred_element_type=F32))
        y_mem.append(jnp.where(is_a, outs[0], outs[1]))

    rest = jnp.concatenate([y_pool] + y_mem, axis=1).astype(BF16)
    proj = jnp.dot(ya_ref[...], wo_ref[:ATTN_WIDTH, :], preferred_element_type=F32)
    proj += jnp.dot(rest, wo_ref[ATTN_WIDTH:, :], preferred_element_type=F32)
    x2 = x_ref[...] + proj
    x2_ref[...] = x2

    ms = jnp.mean(x2 * x2, axis=-1, keepdims=True)
    h2 = (x2 * lax.rsqrt(ms + EPS) * fn_ref[...]).astype(BF16)
    bits = lax.bitcast_convert_type(h2.astype(F32), U32)
    h2p_ref[...] = bits[:, :PACKED] | (bits[:, PACKED:] >> 16)

    logits = jnp.dot(h2, wr_ref[...], preferred_element_type=F32) + br_ref[...]
    ninf = -jnp.inf
    lane_f = lane.astype(F32)
    grp_f = ((lane - EXPERT_LANE0) >> 3).astype(F32)
    is_g = lane < N_GROUPS
    lg = jnp.where(is_g, logits, ninf)
    mg = jnp.max(lg, axis=-1, keepdims=True)
    g_sel = jnp.min(jnp.where(lg == mg, lane_f, float(LANES)), axis=-1, keepdims=True)
    w_g = 1.0 / jnp.sum(jnp.where(is_g, jnp.exp(logits - mg), 0.0), axis=-1, keepdims=True)
    in_grp = (lane >= EXPERT_LANE0) & (grp_f == g_sel)
    le = jnp.where(in_grp, logits, ninf)
    v1 = jnp.max(le, axis=-1, keepdims=True)
    i1 = jnp.min(jnp.where(le == v1, lane_f, float(2 * LANES)), axis=-1, keepdims=True)
    le2 = jnp.where(lane_f == i1, ninf, le)
    v2 = jnp.max(le2, axis=-1, keepdims=True)
    i2 = jnp.min(jnp.where(le2 == v2, lane_f, float(2 * LANES)), axis=-1, keepdims=True)
    e21 = jnp.exp(v2 - v1)
    gate1 = w_g / (1.0 + e21)
    gate2 = w_g * e21 / (1.0 + e21)

    hot1 = lane_f == i1
    hot2 = lane_f == i2
    onehot = jnp.where(hot1 | hot2, 1.0, 0.0)
    prefix = jnp.dot(ltri_ref[...], onehot.astype(BF16), preferred_element_type=F32)
    tot = prefix + base_ref[...]
    rank1 = jnp.sum(jnp.where(hot1, tot, 0.0), axis=-1, keepdims=True)
    rank2 = jnp.sum(jnp.where(hot2, tot, 0.0), axis=-1, keepdims=True)
    base_new = base_ref[...] + jnp.sum(onehot, axis=0, keepdims=True)
    base_ref[...] = base_new
    cnt_ref[...] = base_new

    e1 = i1 - float(EXPERT_LANE0)
    e2 = i2 - float(EXPERT_LANE0)
    rinfo = jnp.where(lane == 0, e1, jnp.where(lane == 1, e2, jnp.where(lane == 2, rank1, jnp.where(
        lane == 3, rank2, jnp.where(lane == 4, gate1, jnp.where(lane == 5, gate2, 0.0))))))
    rinfo_ref[...] = rinfo
    rinfot_ref[...] = jnp.transpose(rinfo)[:8, :]


def _mix(ya, u, qm, km, vm, x2d, pool_bd, pool_scale, w_out, ffn_norm, w_r, b_r, B, S):
    T = B * S
    tm = TM_MIX
    tps = S // tm
    hb = tm // HALO
    const = lambda i: (0, 0)
    ltri = (jnp.arange(tm)[:, None] > jnp.arange(tm)[None, :]).astype(BF16)
    return pl.pallas_call(
        functools.partial(_mix_kernel, tps),
        grid=(T // tm,),
        in_specs=[pl.BlockSpec((tm, ATTN_WIDTH), lambda i: (i, 0)),
                  pl.BlockSpec((tm, POOL_WIDTH), lambda i: (i, 0)),
                  pl.BlockSpec((HALO, POOL_WIDTH), lambda i: (jnp.maximum(i * hb - 1, 0), 0)),
                  pl.BlockSpec((tm, MEM_WIDTH), lambda i: (i, 0)),
                  pl.BlockSpec((None, N_MEM, MEM_WIDTH), lambda i: (i // tps, 0, 0)),
                  pl.BlockSpec((None, N_MEM, MEM_WIDTH), lambda i: (i // tps, 0, 0)),
                  pl.BlockSpec((tm, D_MODEL), lambda i: (i, 0)),
                  pl.BlockSpec((POOL_WIDTH, POOL_WIDTH), const),
                  pl.BlockSpec((1, POOL_WIDTH), const),
                  pl.BlockSpec((D_MODEL, D_MODEL), const),
                  pl.BlockSpec((1, D_MODEL), const),
                  pl.BlockSpec((D_MODEL, LANES), const),
                  pl.BlockSpec((1, LANES), const),
                  pl.BlockSpec((tm, tm), const)],
        out_specs=[pl.BlockSpec((tm, D_MODEL), lambda i: (i, 0)),
                   pl.BlockSpec((tm, PACKED), lambda i: (i, 0)),
                   pl.BlockSpec((tm, LANES), lambda i: (i, 0)),
                   pl.BlockSpec((8, tm), lambda i: (0, i)),
                   pl.BlockSpec((1, LANES), const)],
        out_shape=[jax.ShapeDtypeStruct((T, D_MODEL), F32),
                   jax.ShapeDtypeStruct((T, PACKED), U32),
                   jax.ShapeDtypeStruct((T, LANES), F32),
                   jax.ShapeDtypeStruct((8, T), F32),
                   jax.ShapeDtypeStruct((1, LANES), F32)],
        scratch_shapes=[pltpu.VMEM((1, LANES), F32)],
        compiler_params=_cparams(("arbitrary",)),
        name="mix_router",
    )(ya, u, u, qm, km, vm, x2d, pool_bd, pool_scale, w_out, ffn_norm, w_r, b_r, ltri)


def _dest_kernel(rt_ref, ps_ref, d_ref):
    tn = rt_ref.shape[1]
    sub = lax.broadcasted_iota(I32, (N_EXPERTS, tn), 0).astype(F32)
    ps = ps_ref[...]
    rows = []
    for k in range(2):
        e = rt_ref[k:k + 1, :]
        start = jnp.sum(jnp.where(sub == e, ps, 0.0), axis=0, keepdims=True)
        rows.append(start + rt_ref[2 + k:3 + k, :])
    word = rows[0].astype(I32) | (rows[1].astype(I32) << 16)
    d_ref[...] = jnp.concatenate([word, jnp.zeros((7, tn), I32)], axis=0)


def _dest(rinfot, pstart_col, T):
    tn = 2048
    return pl.pallas_call(
        _dest_kernel,
        grid=(T // tn,),
        in_specs=[pl.BlockSpec((8, tn), lambda i: (0, i)),
                  pl.BlockSpec((N_EXPERTS, 1), lambda i: (0, 0))],
        out_specs=pl.BlockSpec((8, tn), lambda i: (0, i)),
        out_shape=jax.ShapeDtypeStruct((8, T), I32),
        compiler_params=_cparams(("parallel",)),
        name="dest_rows",
    )(rinfot, pstart_col)


def _rowmap_kernel(d_ref, fill_hbm, inv_ref, sem):
    T = d_ref.shape[0]
    cp = pltpu.make_async_copy(fill_hbm, inv_ref, sem)
    cp.start()
    cp.wait()
    unroll = 8

    def body(g, carry):
        t0 = g * unroll
        words = [d_ref[t0 + u] for u in range(unroll)]
        for u, w in enumerate(words):
            inv_ref[w & 0xFFFF] = t0 + u
            inv_ref[lax.shift_right_logical(w, 16)] = T + t0 + u
        return carry

    lax.fori_loop(0, T // unroll, body, 0)


def _rowmap(dword, R):
    assert R <= 1 << 16
    return pl.pallas_call(
        _rowmap_kernel,
        grid_spec=pltpu.PrefetchScalarGridSpec(
            num_scalar_prefetch=1,
            grid=(1,),
            in_specs=[pl.BlockSpec(memory_space=pl.ANY)],
            out_specs=pl.BlockSpec(memory_space=pltpu.SMEM),
            scratch_shapes=[pltpu.SemaphoreType.DMA(())]),
        out_shape=jax.ShapeDtypeStruct((R,), I32),
        compiler_params=_cparams(("arbitrary",)),
        name="rowmap",
    )(dword, jnp.zeros((R,), I32))


def _row_loop(n, body):
    shift = ROW_UNROLL.bit_length() - 1

    def group(g, carry):
        for u in range(ROW_UNROLL):
            body(g * ROW_UNROLL + u)
        return carry

    def rest(i, carry):
        body(((n >> shift) << shift) + i)
        return carry

    lax.fori_loop(0, n >> shift, group, 0)
    lax.fori_loop(0, n & (ROW_UNROLL - 1), rest, 0)


def _moe_kernel(be_ref, nu_ref, first_ref, nxt_ref, ws_ref, valid_ref, inv_ref,
                h_hbm, w1_hbm, w3_hbm, w2_hbm, o_hbm,
                tab, xbuf, ybuf, w1s, w3s, w2s, w1b, w3b, w2b, tsem, ssem, wsem):
    j = pl.program_id(0)
    nu = nu_ref[0]
    T = h_hbm.shape[0]
    tr = TR_MOE
    n_tiles = inv_ref.shape[0] // tr

    def tile_rows(ref, row, n=1):
        return ref.at[pl.ds(pl.multiple_of(row * ROW_SUB, ROW_SUB), n * ROW_SUB)]

    def gather_tile(tile, slot):
        for i in range(tr):
            v = inv_ref[tile * tr + i]
            tok = (v & (T - 1)) if T & (T - 1) == 0 else jnp.where(v >= T, v - T, v)
            xbuf[pl.ds(slot * tr + i, 1), :] = tab[pl.ds(tok, 1), :]

    def wait_rows(n, done):
        @pl.when(n == tr)
        def _():
            done(tr).wait()

        @pl.when(n != tr)
        def _():
            lax.fori_loop(0, n >> 3, lambda g, c: (done(8).wait(), c)[1], 0)
            lax.fori_loop(0, n & 7, lambda g, c: (done(1).wait(), c)[1], 0)

    def scatter_row(tile, slot, i):
        v = inv_ref[tile * tr + i]
        return pltpu.make_async_copy(tile_rows(ybuf, slot * tr + i), tile_rows(o_hbm, v), ssem.at[slot])

    def scatter_wait(slot, n):
        wait_rows(n, lambda rows: pltpu.make_async_copy(
            tile_rows(ybuf, slot * tr, rows), tile_rows(o_hbm, 0, rows), ssem.at[slot]))

    def weight_copies(e, slot):
        return [pltpu.make_async_copy(src.at[e], dst.at[slot], wsem.at[slot])
                for src, dst in ((w1_hbm, w1s), (w3_hbm, w3s), (w2_hbm, w2s))]

    @pl.when(j < nu)
    def _():
        s = j & 1
        e = be_ref[j]

        @pl.when(j == 0)
        def _():
            table = pltpu.make_async_copy(h_hbm, tab, tsem)
            table.start()
            for cp in weight_copies(e, 0):
                cp.start()
            table.wait()
            gather_tile(0, 0)

        @pl.when(first_ref[j] == 1)
        def _():
            wslot = ws_ref[j]
            for cp in weight_copies(e, wslot):
                cp.wait()
            e_next = nxt_ref[j]

            @pl.when(e_next >= 0)
            def _():
                for cp in weight_copies(e_next, 1 - wslot):
                    cp.start()

            w1b[...] = w1s[wslot].astype(BF16)
            w3b[...] = w3s[wslot].astype(BF16)
            w2b[...] = w2s[wslot].astype(BF16)

        @pl.when(j >= 2)
        def _():
            scatter_wait(s, valid_ref[j - 2])

        w = xbuf[pl.ds(pl.multiple_of(s * tr, tr), tr), :]
        lo = lax.bitcast_convert_type(w & jnp.uint32(0xFFFF0000), F32).astype(BF16)
        hi = lax.bitcast_convert_type(w << 16, F32).astype(BF16)
        gather_tile(jnp.minimum(j + 1, n_tiles - 1), 1 - s)
        halves = [slice(k * (EXPERT_HIDDEN // 2), (k + 1) * (EXPERT_HIDDEN // 2)) for k in range(2)]

        def up(wb, cols):
            return (jnp.dot(lo, wb[:PACKED, cols], preferred_element_type=F32)
                    + jnp.dot(hi, wb[PACKED:, cols], preferred_element_type=F32))

        ab = [(up(w1b, cols), up(w3b, cols)) for cols in halves]
        y = None
        for cols, (a, b) in zip(halves, ab):
            hmid = (a / (1.0 + jnp.exp(-a)) * b).astype(BF16)
            part = jnp.dot(hmid, w2b[cols, :], preferred_element_type=F32)
            y = part if y is None else y + part
        base = pl.multiple_of(s * (tr * ROW_SUB), tr * ROW_SUB)
        for c in range(ROW_SUB):
            ybuf[pl.ds(base + c, tr, stride=ROW_SUB), :] = y[:, c * LANES:(c + 1) * LANES]

        @pl.when(valid_ref[j] == tr)
        def _():
            for i in range(tr):
                scatter_row(j, s, i).start()

        @pl.when(valid_ref[j] != tr)
        def _():
            _row_loop(valid_ref[j], lambda i: scatter_row(j, s, i).start())

        @pl.when(j == nu - 1)
        def _():
            scatter_wait(s, valid_ref[j])

            @pl.when(j >= 1)
            def _():
                scatter_wait(1 - s, valid_ref[j - 1])


def _moe(blk_exp, n_used, first, nxt, wslot, valid, inv, h2p, w1, w3, w2):
    T = h2p.shape[0]
    tr = TR_MOE
    n_tiles = blk_exp.shape[0]
    any_spec = pl.BlockSpec(memory_space=pl.ANY)
    return pl.pallas_call(
        _moe_kernel,
        grid_spec=pltpu.PrefetchScalarGridSpec(
            num_scalar_prefetch=7,
            grid=(n_tiles,),
            in_specs=[any_spec] * 4,
            out_specs=any_spec,
            scratch_shapes=[pltpu.VMEM((T, PACKED), U32),
                            pltpu.VMEM((2 * tr, PACKED), U32),
                            pltpu.VMEM((2 * tr * ROW_SUB, LANES), F32),
                            pltpu.VMEM((2, D_MODEL, EXPERT_HIDDEN), F32),
                            pltpu.VMEM((2, D_MODEL, EXPERT_HIDDEN), F32),
                            pltpu.VMEM((2, EXPERT_HIDDEN, D_MODEL), F32),
                            pltpu.VMEM((D_MODEL, EXPERT_HIDDEN), BF16),
                            pltpu.VMEM((D_MODEL, EXPERT_HIDDEN), BF16),
                            pltpu.VMEM((EXPERT_HIDDEN, D_MODEL), BF16),
                            pltpu.SemaphoreType.DMA(()),
                            pltpu.SemaphoreType.DMA((2,)),
                            pltpu.SemaphoreType.DMA((2,))]),
        out_shape=jax.ShapeDtypeStruct((2 * T * ROW_SUB, LANES), F32),
        compiler_params=pltpu.CompilerParams(dimension_semantics=("arbitrary",), vmem_limit_bytes=MOE_VMEM_LIMIT),
        name="moe_experts",
    )(blk_exp, n_used, first, nxt, wslot, valid, inv, h2p, w1, w3, w2)


def _combine_kernel(x2_ref, rinfo_ref, y1_ref, y2_ref, o_ref):
    g1 = rinfo_ref[:, 4:5]
    g2 = rinfo_ref[:, 5:6]
    tc = x2_ref.shape[0]
    for c in range(ROW_SUB):
        cols = slice(c * LANES, (c + 1) * LANES)
        rows = pl.ds(c, tc, stride=ROW_SUB)
        o_ref[:, cols] = x2_ref[:, cols] + (g1 * y1_ref[rows, :] + g2 * y2_ref[rows, :])


def _combine(x2, rinfo, y2slot):
    T = x2.shape[0]
    tc = TC_COMBINE
    return pl.pallas_call(
        _combine_kernel,
        grid=(T // tc,),
        in_specs=[pl.BlockSpec((tc, D_MODEL), lambda i: (i, 0)),
                  pl.BlockSpec((tc, LANES), lambda i: (i, 0)),
                  pl.BlockSpec((tc * ROW_SUB, LANES), lambda i: (i, 0)),
                  pl.BlockSpec((tc * ROW_SUB, LANES), lambda i: (T // tc + i, 0))],
        out_specs=pl.BlockSpec((tc, D_MODEL), lambda i: (i, 0)),
        out_shape=jax.ShapeDtypeStruct((T, D_MODEL), F32),
        compiler_params=_cparams(("parallel",)),
        name="combine",
    )(x2, rinfo, y2slot, y2slot)


def _layer(x, mem, attn_norm, w_in, q_norm, k_norm, pool_proj, pool_scale, mem_norm, w_mem_kv,
           mq_norm, mk_norm, w_out, ffn_norm, w_group, b_group, w_router, b_router, w1, w3, w2):
    B, S, D = x.shape
    T = B * S
    assert D == D_MODEL and S % SUPER == 0 and T % TM_PROJ == 0
    x2d = x.reshape(T, D)
    row = lambda v: v.reshape(1, -1).astype(F32)
    scale = HEAD_DIM ** -0.5
    gq = row(jnp.tile(q_norm, ATTN_WIDTH // HEAD_DIM) * scale)
    gk = row(jnp.tile(k_norm, ATTN_WIDTH // HEAD_DIM))
    gmq = row(jnp.tile(mq_norm, MEM_WIDTH // HEAD_DIM) * scale)
    gmk = row(jnp.tile(mk_norm, MEM_WIDTH // HEAD_DIM))

    (q1, k1, v1, q4, k4, v4, q16, k16, v16, u, qm) = _proj(
        x2d, row(attn_norm), w_in.astype(BF16), gq, gk, gmq, B, S)
    ya = _attn(q1, k1, v1, q4, k4, v4, q16, k16, v16, B, S).reshape(T, ATTN_WIDTH)
    km, vm = _memkv(mem, row(mem_norm), w_mem_kv.astype(BF16), gmk, B)

    pool_bd = jax.scipy.linalg.block_diag(*[pool_proj[g] for g in range(pool_proj.shape[0])]).astype(BF16)
    w_r = jnp.zeros((D, LANES), F32)
    w_r = w_r.at[:, :N_GROUPS].set(w_group)
    w_r = w_r.at[:, EXPERT_LANE0:].set(jnp.transpose(w_router, (1, 0, 2)).reshape(D, N_EXPERTS))
    b_r = jnp.zeros((1, LANES), F32).at[0, :N_GROUPS].set(b_group).at[0, EXPERT_LANE0:].set(b_router.reshape(-1))
    x2, h2p, rinfo, rinfot, counts = _mix(ya, u, qm, km, vm, x2d, pool_bd, row(pool_scale),
                                          w_out.astype(BF16), row(ffn_norm), w_r.astype(BF16), b_r, B, S)

    R, pstart, sched = _tile_schedule(counts, T)
    dest = _dest(rinfot, pstart.astype(F32).reshape(N_EXPERTS, 1), T)
    inv = _rowmap(dest[0], R)
    y2slot = _moe(*sched, inv, h2p, w1, w3, w2)
    out = _combine(x2, rinfo, y2slot)
    return out.reshape(B, S, D)


def _tile_schedule(counts, T):
    tr = TR_MOE
    R = 2 * T + N_EXPERTS * tr
    n_tiles = R // tr
    cnt = counts[0, EXPERT_LANE0:].astype(I32)
    padded = ((cnt + tr - 1) // tr) * tr
    pend = jnp.cumsum(padded)
    pstart = pend - padded
    n_used = (pend[-1] // tr).astype(I32).reshape(1)
    tiles = jnp.arange(n_tiles, dtype=I32)
    tile_row = jnp.minimum(tiles, n_used[0] - 1) * tr
    blk_exp = jnp.minimum(jnp.sum(tile_row[:, None] >= pend[None, :], axis=1), N_EXPERTS - 1).astype(I32)
    in_use = tiles < n_used[0]
    first = in_use & ((tiles == 0) | (blk_exp != jnp.roll(blk_exp, 1)))
    run_idx = jnp.cumsum(first.astype(I32)) - 1
    used = cnt > 0
    exp_of_run = jnp.argsort(jnp.logical_not(used), stable=True).astype(I32)
    nxt = jnp.where(run_idx + 1 < jnp.sum(used), exp_of_run[jnp.clip(run_idx + 1, 0, N_EXPERTS - 1)], -1)
    valid = jnp.where(in_use, jnp.clip(cnt[blk_exp] - (tiles * tr - pstart[blk_exp]), 0, tr), 0)
    sched = (blk_exp, n_used, first.astype(I32), nxt.astype(I32), (run_idx & 1).astype(I32), valid.astype(I32))
    return R, pstart, sched


def kernel(x, mem, attn_norm, w_in, q_norm, k_norm, pool_proj, pool_scale, mem_norm, w_mem_kv, mq_norm, mk_norm,
           w_out, ffn_norm, w_group, b_group, w_router, b_router, w1, w3, w2):
    for l in range(attn_norm.shape[0]):
        x = _layer(x, mem, attn_norm[l], w_in[l], q_norm[l], k_norm[l], pool_proj[l], pool_scale[l],
                   mem_norm[l], w_mem_kv[l], mq_norm[l], mk_norm[l], w_out[l], ffn_norm[l],
                   w_group[l], b_group[l], w_router[l], b_router[l], w1[l], w3[l], w2[l])
    return x
```

```python
import functools

import jax
import jax.numpy as jnp
from jax import lax
from jax.experimental import pallas as pl
from jax.experimental.pallas import tpu as pltpu

F32 = jnp.float32
BF16 = jnp.bfloat16
I32 = jnp.int32
U32 = jnp.uint32

D_MODEL = 1024
HEAD_DIM = 64
ATTN_WIDTH = 512
POOL_WIDTH = 256
MEM_WIDTH = 256
N_MEM = 256
IN_WIDTH = 3 * ATTN_WIDTH + POOL_WIDTH + MEM_WIDTH
N_GROUPS = 8
EXPERTS_PER_GROUP = 8
N_EXPERTS = 64
EXPERT_HIDDEN = 512
EPS = 1e-6
NEG_INF = -1e30

LANES = 128
N_PAIRS = ATTN_WIDTH // LANES
BLOCK = 128
SUPER = 16 * BLOCK
HALO = 16
ROW_SUB = D_MODEL // LANES
PACKED = D_MODEL // 2
PACK_SUB = PACKED // LANES

TM_PROJ = 1024
TM_MIX = 1024
TR_MOE = 256
ROW_UNROLL = 16
TC_COMBINE = 1024
EXPERT_LANE0 = 64

VMEM_LIMIT = 48 * 1024 * 1024
MOE_VMEM_LIMIT = 56 * 1024 * 1024


def _cparams(sem):
    return pltpu.CompilerParams(dimension_semantics=sem, vmem_limit_bytes=VMEM_LIMIT)


def _group_ones(n):
    g = jnp.arange(n) // HEAD_DIM
    return (g[:, None] == g[None, :]).astype(BF16)


def _head_norm(z, ones_ref, gain_ref):
    ss = jnp.dot((z * z).astype(BF16), ones_ref[...], preferred_element_type=F32)
    return z * lax.rsqrt(ss * (1.0 / HEAD_DIM) + EPS) * gain_ref[...]


def _proj_kernel(x_ref, an_ref, win_ref, gq_ref, gk_ref, gm_ref, o512_ref, o256_ref,
                 q1_ref, k1_ref, v1_ref, q4_ref, k4_ref, v4_ref, q16_ref, k16_ref, v16_ref,
                 u_ref, qm_ref, zs_ref, z4_ref):
    tm = x_ref.shape[0]
    x = x_ref[...]
    ms = jnp.mean(x * x, axis=-1, keepdims=True)
    h = (x * lax.rsqrt(ms + EPS) * an_ref[...]).astype(BF16)
    a0, a1, a2, a3 = ATTN_WIDTH, 2 * ATTN_WIDTH, 3 * ATTN_WIDTH, 3 * ATTN_WIDTH + POOL_WIDTH

    def cols(lo, hi):
        return jnp.dot(h, win_ref[:, lo:hi], preferred_element_type=F32)

    groups = ((lambda: _head_norm(cols(0, a0), o512_ref, gq_ref), q1_ref, q4_ref, q16_ref),
              (lambda: _head_norm(cols(a0, a1), o512_ref, gk_ref), k1_ref, k4_ref, k16_ref),
              (lambda: cols(a1, a2), v1_ref, v4_ref, v16_ref))
    for make, o1, o4, o16 in groups:
        val = make()
        for hp in range(N_PAIRS):
            pair = val[:, hp * LANES:(hp + 1) * LANES]
            zs_ref[hp] = pair
            o1[hp] = pair.astype(BF16)
        q4n = tm // 4
        for hp in range(N_PAIRS):
            for r4 in range(4):
                rows = zs_ref[hp, pl.ds(r4, q4n, stride=4), :]
                z4_ref[hp, pl.ds(r4 * q4n, q4n), :] = rows
                o4[hp, :, r4 * LANES:(r4 + 1) * LANES] = rows.astype(BF16)
            for r16 in range(16):
                r4, a = r16 % 4, r16 // 4
                rows = z4_ref[hp, pl.ds(r4 * q4n + a, tm // 16, stride=4), :]
                o16[hp, :, r16 * LANES:(r16 + 1) * LANES] = rows.astype(BF16)
    u_ref[...] = cols(a2, a3)
    qm_ref[...] = _head_norm(cols(a3, IN_WIDTH), o256_ref, gm_ref).astype(BF16)


def _proj(x2d, attn_norm, w_in, gq, gk, gm, B, S):
    T = B * S
    tm = TM_PROJ
    nj = S // tm
    const = lambda i: (0, 0)

    def lay(d):
        return jax.ShapeDtypeStruct((B, N_PAIRS, S // d, d * LANES), BF16)

    def lay_spec(d):
        return pl.BlockSpec((None, N_PAIRS, tm // d, d * LANES), lambda i: (i // nj, 0, i % nj, 0))

    out_shape = [lay(1)] * 3 + [lay(4)] * 3 + [lay(16)] * 3 + [
        jax.ShapeDtypeStruct((T, POOL_WIDTH), F32), jax.ShapeDtypeStruct((T, MEM_WIDTH), BF16)]
    out_specs = [lay_spec(1)] * 3 + [lay_spec(4)] * 3 + [lay_spec(16)] * 3 + [
        pl.BlockSpec((tm, POOL_WIDTH), lambda i: (i, 0)), pl.BlockSpec((tm, MEM_WIDTH), lambda i: (i, 0))]
    return pl.pallas_call(
        _proj_kernel,
        grid=(T // tm,),
        in_specs=[pl.BlockSpec((tm, D_MODEL), lambda i: (i, 0)),
                  pl.BlockSpec((1, D_MODEL), const),
                  pl.BlockSpec((D_MODEL, IN_WIDTH), const),
                  pl.BlockSpec((1, ATTN_WIDTH), const),
                  pl.BlockSpec((1, ATTN_WIDTH), const),
                  pl.BlockSpec((1, MEM_WIDTH), const),
                  pl.BlockSpec((ATTN_WIDTH, ATTN_WIDTH), const),
                  pl.BlockSpec((MEM_WIDTH, MEM_WIDTH), const)],
        out_specs=out_specs,
        out_shape=out_shape,
        scratch_shapes=[pltpu.VMEM((N_PAIRS, tm, LANES), F32)] * 2,
        compiler_params=_cparams(("parallel",)),
        name="proj",
    )(x2d, attn_norm, w_in, gq, gk, gm, _group_ones(ATTN_WIDTH), _group_ones(MEM_WIDTH))


def _attn_kernel(q1, k1c, k1p, v1c, v1p, q4, k4c, k4p, v4c, v4p, q16, k16c, k16p, v16c, v16p,
                 bias_ref, o_ref, obuf, lbuf):
    c = pl.program_id(2)
    lane = lax.broadcasted_iota(I32, (BLOCK, LANES), 1)
    is_a = lane < HEAD_DIM
    lane_row = lax.broadcasted_iota(I32, (1, LANES), 1)
    mask_a = jnp.where(lane_row < HEAD_DIM, 1.0, 0.0).astype(BF16)
    mask_b = jnp.where(lane_row < HEAD_DIM, 0.0, 1.0).astype(BF16)
    bias_full = bias_ref[0]
    bias_first = jnp.where(c > 0, bias_full, bias_ref[1])

    def tile(q_t, kp_t, kc_t, vp_t, vc_t, bias):
        lhs = jnp.concatenate([q_t * mask_a, q_t * mask_b], axis=0)
        keys = jnp.concatenate([kp_t, kc_t], axis=0)
        s = lax.dot_general(lhs, keys, (((1,), (1,)), ((), ())), preferred_element_type=F32) + bias
        m = jnp.max(s, axis=-1, keepdims=True)
        p = jnp.exp(s - m)
        l = jnp.sum(p, axis=-1, keepdims=True)
        vals = jnp.concatenate([vp_t, vc_t], axis=0)
        pv = jnp.dot(p.astype(BF16), vals, preferred_element_type=F32) * (1.0 / l)
        lse = m + jnp.log(l)
        return jnp.where(is_a, pv[:BLOCK], pv[BLOCK:]), jnp.where(is_a, lse[:BLOCK], lse[BLOCK:])

    for jb in range(SUPER // BLOCK):
        cur = pl.ds(jb * BLOCK, BLOCK)
        if jb == 0:
            kp_t, vp_t, bias = k1p[...], v1p[...], bias_first
        else:
            prev = pl.ds((jb - 1) * BLOCK, BLOCK)
            kp_t, vp_t, bias = k1c[prev, :], v1c[prev, :], bias_full
        o_t, lse_t = tile(q1[cur, :], kp_t, k1c[cur, :], vp_t, v1c[cur, :], bias)
        obuf[0, cur, :] = o_t
        lbuf[0, cur, :] = lse_t
    for pat, (d, q, kc, kp, vc, vp) in enumerate(((4, q4, k4c, k4p, v4c, v4p), (16, q16, k16c, k16p, v16c, v16p)), 1):
        nblk = SUPER // (BLOCK * d)
        for r in range(d):
            cols = pl.ds(r * LANES, LANES)
            for jb in range(nblk):
                cur = pl.ds(jb * BLOCK, BLOCK)
                if jb == 0:
                    kp_t, vp_t, bias = kp[:, cols], vp[:, cols], bias_first
                else:
                    prev = pl.ds((jb - 1) * BLOCK, BLOCK)
                    kp_t, vp_t, bias = kc[prev, cols], vc[prev, cols], bias_full
                o_t, lse_t = tile(q[cur, cols], kp_t, kc[cur, cols], vp_t, vc[cur, cols], bias)
                rows = pl.ds(jb * BLOCK * d + r, BLOCK, stride=d)
                obuf[pat, rows, :] = o_t
                lbuf[pat, rows, :] = lse_t
    for jb in range(SUPER // BLOCK):
        cur = pl.ds(jb * BLOCK, BLOCK)
        l0, l1, l2 = lbuf[0, cur, :], lbuf[1, cur, :], lbuf[2, cur, :]
        top = jnp.maximum(jnp.maximum(l0, l1), l2)
        w0, w1, w2 = jnp.exp(l0 - top), jnp.exp(l1 - top), jnp.exp(l2 - top)
        mixed = (w0 * obuf[0, cur, :] + w1 * obuf[1, cur, :] + w2 * obuf[2, cur, :]) * (1.0 / (w0 + w1 + w2))
        o_ref[cur, :] = mixed.astype(BF16)


def _band_bias():
    qi = jnp.arange(BLOCK)[:, None]
    kj = jnp.arange(2 * BLOCK)[None, :]
    dist = qi + BLOCK - kj
    in_band = (dist >= 0) & (dist <= BLOCK)
    full = jnp.where(in_band, 0.0, NEG_INF).astype(F32)
    first = jnp.where(in_band & (kj >= BLOCK), 0.0, NEG_INF).astype(F32)
    return jnp.stack([jnp.tile(full, (2, 1)), jnp.tile(first, (2, 1))])


def _attn(q1, k1, v1, q4, k4, v4, q16, k16, v16, B, S):
    nsup = S // SUPER

    def specs(d):
        rows = SUPER // d
        per = rows // BLOCK
        cur = pl.BlockSpec((None, None, rows, d * LANES), lambda b, hp, c: (b, hp, c, 0))
        prev = pl.BlockSpec((None, None, BLOCK, d * LANES),
                            lambda b, hp, c: (b, hp, jnp.maximum(per * c - 1, 0), 0))
        return [cur, cur, prev, cur, prev]

    return pl.pallas_call(
        _attn_kernel,
        grid=(B, N_PAIRS, nsup),
        in_specs=specs(1) + specs(4) + specs(16) + [
            pl.BlockSpec((2, 2 * BLOCK, 2 * BLOCK), lambda b, hp, c: (0, 0, 0))],
        out_specs=pl.BlockSpec((None, SUPER, LANES), lambda b, hp, c: (b, c, hp)),
        out_shape=jax.ShapeDtypeStruct((B, S, ATTN_WIDTH), BF16),
        scratch_shapes=[pltpu.VMEM((3, SUPER, LANES), F32)] * 2,
        compiler_params=_cparams(("parallel", "parallel", "parallel")),
        name="dilated_attn",
    )(q1, k1, k1, v1, v1, q4, k4, k4, v4, v4, q16, k16, k16, v16, v16, _band_bias())


def _memkv_kernel(mem_ref, mn_ref, wkv_ref, gk_ref, o256_ref, km_ref, vm_ref):
    m = mem_ref[...]
    ms = jnp.mean(m * m, axis=-1, keepdims=True)
    mn = (m * lax.rsqrt(ms + EPS) * mn_ref[...]).astype(BF16)
    kv = jnp.dot(mn, wkv_ref[...], preferred_element_type=F32)
    km_ref[...] = _head_norm(kv[:, :MEM_WIDTH], o256_ref, gk_ref).astype(BF16)
    vm_ref[...] = kv[:, MEM_WIDTH:].astype(BF16)


def _memkv(mem, mem_norm, w_mem_kv, gmk, B):
    const = lambda b: (0, 0)
    return pl.pallas_call(
        _memkv_kernel,
        grid=(B,),
        in_specs=[pl.BlockSpec((None, N_MEM, D_MODEL), lambda b: (b, 0, 0)),
                  pl.BlockSpec((1, D_MODEL), const),
                  pl.BlockSpec((D_MODEL, 2 * MEM_WIDTH), const),
                  pl.BlockSpec((1, MEM_WIDTH), const),
                  pl.BlockSpec((MEM_WIDTH, MEM_WIDTH), const)],
        out_specs=[pl.BlockSpec((None, N_MEM, MEM_WIDTH), lambda b: (b, 0, 0))] * 2,
        out_shape=[jax.ShapeDtypeStruct((B, N_MEM, MEM_WIDTH), BF16)] * 2,
        compiler_params=_cparams(("parallel",)),
        name="memkv",
    )(mem, mem_norm, w_mem_kv, gmk, _group_ones(MEM_WIDTH))


def _mix_kernel(tiles_per_seq, ya_ref, u_ref, uh_ref, qm_ref, km_ref, vm_ref, x_ref, pp_ref, ps_ref,
                wo_ref, fn_ref, wr_ref, br_ref, ltri_ref,
                x2_ref, h2p_ref, rinfo_ref, rinfot_ref, cnt_ref, base_ref):
    i = pl.program_id(0)
    tm = x_ref.shape[0]
    seq_tile = i % tiles_per_seq

    @pl.when(i == 0)
    def _():
        base_ref[...] = jnp.zeros_like(base_ref)

    u = u_ref[...]
    halo = jnp.where(seq_tile == 0, 0.0, uh_ref[...])
    uu = jnp.concatenate([halo, u], axis=0)
    a1 = uu[1:] + uu[:-1]
    a2 = a1[2:] + a1[:-2]
    a3 = a2[4:] + a2[:-4]
    a4 = a3[8:] + a3[:-8]
    lane_p = lax.broadcasted_iota(I32, (tm, POOL_WIDTH), 1)
    g0, g1, g2 = lane_p < 64, lane_p < 128, lane_p < 192
    wsum = jnp.where(g0, a1[15:], jnp.where(g1, a2[13:], jnp.where(g2, a3[9:], a4[1:])))
    wlen = jnp.where(g0, 2.0, jnp.where(g1, 4.0, jnp.where(g2, 8.0, 16.0)))
    tpos = seq_tile * tm + lax.broadcasted_iota(I32, (tm, POOL_WIDTH), 0) + 1
    cnt = jnp.minimum(tpos.astype(F32), wlen)
    pooled = wsum / cnt - u
    y_pool = jnp.dot(pooled.astype(BF16), pp_ref[...], preferred_element_type=F32) * ps_ref[...]

    lane = lax.broadcasted_iota(I32, (tm, LANES), 1)
    is_a = lane < HEAD_DIM
    lane_row = lax.broadcasted_iota(I32, (1, LANES), 1)
    mask_a = jnp.where(lane_row < HEAD_DIM, 1.0, 0.0).astype(BF16)
    mask_b = jnp.where(lane_row < HEAD_DIM, 0.0, 1.0).astype(BF16)
    y_mem = []
    for pr in range(MEM_WIDTH // LANES):
        cols = slice(pr * LANES, (pr + 1) * LANES)
        qp, kp, vp = qm_ref[:, cols], km_ref[:, cols], vm_ref[:, cols]
        outs = []
        for msk in (mask_a, mask_b):
            s = lax.dot_general(qp * msk, kp, (((1,), (1,)), ((), ())), preferred_element_type=F32)
            m = jnp.max(s, axis=-1, keepdims=True)
            p = jnp.exp(s - m)
            p = p / jnp.sum(p, axis=-1, keepdims=True)
            outs.append(jnp.dot(p.astype(BF16), vp, preferred_element_type=F32))
        y_mem.append(jnp.where(is_a, outs[0], outs[1]))

    rest = jnp.concatenate([y_pool] + y_mem, axis=1).astype(BF16)
    proj = jnp.dot(ya_ref[...], wo_ref[:ATTN_WIDTH, :], preferred_element_type=F32)
    proj += jnp.dot(rest, wo_ref[ATTN_WIDTH:, :], preferred_element_type=F32)
    x2 = x_ref[...] + proj
    x2_ref[...] = x2

    ms = jnp.mean(x2 * x2, axis=-1, keepdims=True)
    h2 = (x2 * lax.rsqrt(ms + EPS) * fn_ref[...]).astype(BF16)
    bits = lax.bitcast_convert_type(h2.astype(F32), U32)
    word = bits[:, :PACKED] | (bits[:, PACKED:] >> 16)
    for c in range(PACK_SUB):
        h2p_ref[pl.ds(c, tm, stride=PACK_SUB), :] = word[:, c * LANES:(c + 1) * LANES]

    logits = jnp.dot(h2, wr_ref[...], preferred_element_type=F32) + br_ref[...]
    ninf = -jnp.inf
    lane_f = lane.astype(F32)
    grp_f = ((lane - EXPERT_LANE0) >> 3).astype(F32)
    is_g = lane < N_GROUPS
    lg = jnp.where(is_g, logits, ninf)
    mg = jnp.max(lg, axis=-1, keepdims=True)
    g_sel = jnp.min(jnp.where(lg == mg, lane_f, float(LANES)), axis=-1, keepdims=True)
    w_g = 1.0 / jnp.sum(jnp.where(is_g, jnp.exp(logits - mg), 0.0), axis=-1, keepdims=True)
    in_grp = (lane >= EXPERT_LANE0) & (grp_f == g_sel)
    le = jnp.where(in_grp, logits, ninf)
    v1 = jnp.max(le, axis=-1, keepdims=True)
    i1 = jnp.min(jnp.where(le == v1, lane_f, float(2 * LANES)), axis=-1, keepdims=True)
    le2 = jnp.where(lane_f == i1, ninf, le)
    v2 = jnp.max(le2, axis=-1, keepdims=True)
    i2 = jnp.min(jnp.where(le2 == v2, lane_f, float(2 * LANES)), axis=-1, keepdims=True)
    e21 = jnp.exp(v2 - v1)
    gate1 = w_g / (1.0 + e21)
    gate2 = w_g * e21 / (1.0 + e21)

    hot1 = lane_f == i1
    hot2 = lane_f == i2
    onehot = jnp.where(hot1 | hot2, 1.0, 0.0)
    prefix = jnp.dot(ltri_ref[...], onehot.astype(BF16), preferred_element_type=F32)
    tot = prefix + base_ref[...]
    rank1 = jnp.sum(jnp.where(hot1, tot, 0.0), axis=-1, keepdims=True)
    rank2 = jnp.sum(jnp.where(hot2, tot, 0.0), axis=-1, keepdims=True)
    base_new = base_ref[...] + jnp.sum(onehot, axis=0, keepdims=True)
    base_ref[...] = base_new
    cnt_ref[...] = base_new

    e1 = i1 - float(EXPERT_LANE0)
    e2 = i2 - float(EXPERT_LANE0)
    rinfo = jnp.where(lane == 0, e1, jnp.where(lane == 1, e2, jnp.where(lane == 2, rank1, jnp.where(
        lane == 3, rank2, jnp.where(lane == 4, gate1, jnp.where(lane == 5, gate2, 0.0))))))
    rinfo_ref[...] = rinfo
    rinfot_ref[...] = jnp.transpose(rinfo)[:8, :]


def _mix(ya, u, qm, km, vm, x2d, pool_bd, pool_scale, w_out, ffn_norm, w_r, b_r, B, S):
    T = B * S
    tm = TM_MIX
    tps = S // tm
    hb = tm // HALO
    const = lambda i: (0, 0)
    ltri = (jnp.arange(tm)[:, None] > jnp.arange(tm)[None, :]).astype(BF16)
    return pl.pallas_call(
        functools.partial(_mix_kernel, tps),
        grid=(T // tm,),
        in_specs=[pl.BlockSpec((tm, ATTN_WIDTH), lambda i: (i, 0)),
                  pl.BlockSpec((tm, POOL_WIDTH), lambda i: (i, 0)),
                  pl.BlockSpec((HALO, POOL_WIDTH), lambda i: (jnp.maximum(i * hb - 1, 0), 0)),
                  pl.BlockSpec((tm, MEM_WIDTH), lambda i: (i, 0)),
                  pl.BlockSpec((None, N_MEM, MEM_WIDTH), lambda i: (i // tps, 0, 0)),
                  pl.BlockSpec((None, N_MEM, MEM_WIDTH), lambda i: (i // tps, 0, 0)),
                  pl.BlockSpec((tm, D_MODEL), lambda i: (i, 0)),
                  pl.BlockSpec((POOL_WIDTH, POOL_WIDTH), const),
                  pl.BlockSpec((1, POOL_WIDTH), const),
                  pl.BlockSpec((D_MODEL, D_MODEL), const),
                  pl.BlockSpec((1, D_MODEL), const),
                  pl.BlockSpec((D_MODEL, LANES), const),
                  pl.BlockSpec((1, LANES), const),
                  pl.BlockSpec((tm, tm), const)],
        out_specs=[pl.BlockSpec((tm, D_MODEL), lambda i: (i, 0)),
                   pl.BlockSpec((tm * PACK_SUB, LANES), lambda i: (i, 0)),
                   pl.BlockSpec((tm, LANES), lambda i: (i, 0)),
                   pl.BlockSpec((8, tm), lambda i: (0, i)),
                   pl.BlockSpec((1, LANES), const)],
        out_shape=[jax.ShapeDtypeStruct((T, D_MODEL), F32),
                   jax.ShapeDtypeStruct((T * PACK_SUB, LANES), U32),
                   jax.ShapeDtypeStruct((T, LANES), F32),
                   jax.ShapeDtypeStruct((8, T), F32),
                   jax.ShapeDtypeStruct((1, LANES), F32)],
        scratch_shapes=[pltpu.VMEM((1, LANES), F32)],
        compiler_params=_cparams(("arbitrary",)),
        name="mix_router",
    )(ya, u, u, qm, km, vm, x2d, pool_bd, pool_scale, w_out, ffn_norm, w_r, b_r, ltri)


def _dest_kernel(rt_ref, ps_ref, d_ref):
    tn = rt_ref.shape[1]
    sub = lax.broadcasted_iota(I32, (N_EXPERTS, tn), 0).astype(F32)
    ps = ps_ref[...]
    rows = []
    for k in range(2):
        e = rt_ref[k:k + 1, :]
        start = jnp.sum(jnp.where(sub == e, ps, 0.0), axis=0, keepdims=True)
        rows.append(start + rt_ref[2 + k:3 + k, :])
    word = rows[0].astype(I32) | (rows[1].astype(I32) << 16)
    d_ref[...] = jnp.concatenate([word, jnp.zeros((7, tn), I32)], axis=0)


def _dest(rinfot, pstart_col, T):
    tn = 2048
    return pl.pallas_call(
        _dest_kernel,
        grid=(T // tn,),
        in_specs=[pl.BlockSpec((8, tn), lambda i: (0, i)),
                  pl.BlockSpec((N_EXPERTS, 1), lambda i: (0, 0))],
        out_specs=pl.BlockSpec((8, tn), lambda i: (0, i)),
        out_shape=jax.ShapeDtypeStruct((8, T), I32),
        compiler_params=_cparams(("parallel",)),
        name="dest_rows",
    )(rinfot, pstart_col)


def _rowmap_kernel(d_ref, fill_hbm, inv_ref, sem):
    T = d_ref.shape[0]
    cp = pltpu.make_async_copy(fill_hbm, inv_ref, sem)
    cp.start()
    cp.wait()
    unroll = 8

    def body(g, carry):
        t0 = g * unroll
        words = [d_ref[t0 + u] for u in range(unroll)]
        for u, w in enumerate(words):
            inv_ref[w & 0xFFFF] = t0 + u
            inv_ref[lax.shift_right_logical(w, 16)] = T + t0 + u
        return carry

    lax.fori_loop(0, T // unroll, body, 0)


def _rowmap(dword, R):
    assert R <= 1 << 16
    return pl.pallas_call(
        _rowmap_kernel,
        grid_spec=pltpu.PrefetchScalarGridSpec(
            num_scalar_prefetch=1,
            grid=(1,),
            in_specs=[pl.BlockSpec(memory_space=pl.ANY)],
            out_specs=pl.BlockSpec(memory_space=pltpu.SMEM),
            scratch_shapes=[pltpu.SemaphoreType.DMA(())]),
        out_shape=jax.ShapeDtypeStruct((R,), I32),
        compiler_params=_cparams(("arbitrary",)),
        name="rowmap",
    )(dword, jnp.zeros((R,), I32))


def _row_loop(n, body):
    shift = ROW_UNROLL.bit_length() - 1

    def group(g, carry):
        for u in range(ROW_UNROLL):
            body(g * ROW_UNROLL + u)
        return carry

    def rest(i, carry):
        body(((n >> shift) << shift) + i)
        return carry

    lax.fori_loop(0, n >> shift, group, 0)
    lax.fori_loop(0, n & (ROW_UNROLL - 1), rest, 0)


def _moe_kernel(be_ref, nu_ref, first_ref, nxt_ref, ws_ref, valid_ref, inv_ref,
                h_hbm, w1_hbm, w3_hbm, w2_hbm, o_hbm,
                tab, xbuf, ybuf, w1s, w3s, w2s, w1b, w3b, w2b, tsem, ssem, wsem):
    j = pl.program_id(0)
    nu = nu_ref[0]
    T = h_hbm.shape[0] // PACK_SUB
    tr = TR_MOE
    n_tiles = inv_ref.shape[0] // tr

    def tile_rows(ref, row, n=1):
        return ref.at[pl.ds(pl.multiple_of(row * ROW_SUB, ROW_SUB), n * ROW_SUB)]

    def gather_tile(tile, slot):
        for i in range(tr):
            v = inv_ref[tile * tr + i]
            tok = (v & (T - 1)) if T & (T - 1) == 0 else jnp.where(v >= T, v - T, v)
            dst = pl.multiple_of((slot * tr + i) * PACK_SUB, PACK_SUB)
            xbuf[pl.ds(dst, PACK_SUB), :] = tab[pl.ds(pl.multiple_of(tok * PACK_SUB, PACK_SUB), PACK_SUB), :]

    def wait_rows(n, done):
        @pl.when(n == tr)
        def _():
            done(tr).wait()

        @pl.when(n != tr)
        def _():
            lax.fori_loop(0, n >> 3, lambda g, c: (done(8).wait(), c)[1], 0)
            lax.fori_loop(0, n & 7, lambda g, c: (done(1).wait(), c)[1], 0)

    def scatter_row(tile, slot, i):
        v = inv_ref[tile * tr + i]
        return pltpu.make_async_copy(tile_rows(ybuf, slot * tr + i), tile_rows(o_hbm, v), ssem.at[slot])

    def scatter_wait(slot, n):
        wait_rows(n, lambda rows: pltpu.make_async_copy(
            tile_rows(ybuf, slot * tr, rows), tile_rows(o_hbm, 0, rows), ssem.at[slot]))

    def weight_copies(e, slot):
        return [pltpu.make_async_copy(src.at[e], dst.at[slot], wsem.at[slot])
                for src, dst in ((w1_hbm, w1s), (w3_hbm, w3s), (w2_hbm, w2s))]

    @pl.when(j < nu)
    def _():
        s = j & 1
        e = be_ref[j]

        @pl.when(j == 0)
        def _():
            table = pltpu.make_async_copy(h_hbm, tab, tsem)
            table.start()
            for cp in weight_copies(e, 0):
                cp.start()
            table.wait()
            gather_tile(0, 0)

        @pl.when(first_ref[j] == 1)
        def _():
            wslot = ws_ref[j]
            for cp in weight_copies(e, wslot):
                cp.wait()
            e_next = nxt_ref[j]

            @pl.when(e_next >= 0)
            def _():
                for cp in weight_copies(e_next, 1 - wslot):
                    cp.start()

            w1b[...] = w1s[wslot].astype(BF16)
            w3b[...] = w3s[wslot].astype(BF16)
            w2b[...] = w2s[wslot].astype(BF16)

        @pl.when(j >= 2)
        def _():
            scatter_wait(s, valid_ref[j - 2])

        xrow0 = pl.multiple_of(s * (tr * PACK_SUB), tr * PACK_SUB)
        w = jnp.concatenate([xbuf[pl.ds(xrow0 + c, tr, stride=PACK_SUB), :] for c in range(PACK_SUB)],
                            axis=1)
        lo = lax.bitcast_convert_type(w & jnp.uint32(0xFFFF0000), F32).astype(BF16)
        hi = lax.bitcast_convert_type(w << 16, F32).astype(BF16)
        gather_tile(jnp.minimum(j + 1, n_tiles - 1), 1 - s)
        halves = [slice(k * (EXPERT_HIDDEN // 2), (k + 1) * (EXPERT_HIDDEN // 2)) for k in range(2)]

        def up(wb, cols):
            return (jnp.dot(lo, wb[:PACKED, cols], preferred_element_type=F32)
                    + jnp.dot(hi, wb[PACKED:, cols], preferred_element_type=F32))

        ab = [(up(w1b, cols), up(w3b, cols)) for cols in halves]
        y = None
        for cols, (a, b) in zip(halves, ab):
            hmid = (a / (1.0 + jnp.exp(-a)) * b).astype(BF16)
            part = jnp.dot(hmid, w2b[cols, :], preferred_element_type=F32)
            y = part if y is None else y + part
        base = pl.multiple_of(s * (tr * ROW_SUB), tr * ROW_SUB)
        for c in range(ROW_SUB):
            ybuf[pl.ds(base + c, tr, stride=ROW_SUB), :] = y[:, c * LANES:(c + 1) * LANES]

        @pl.when(valid_ref[j] == tr)
        def _():
            for i in range(tr):
                scatter_row(j, s, i).start()

        @pl.when(valid_ref[j] != tr)
        def _():
            _row_loop(valid_ref[j], lambda i: scatter_row(j, s, i).start())

        @pl.when(j == nu - 1)
        def _():
            scatter_wait(s, valid_ref[j])

            @pl.when(j >= 1)
            def _():
                scatter_wait(1 - s, valid_ref[j - 1])


def _moe(blk_exp, n_used, first, nxt, wslot, valid, inv, h2p, w1, w3, w2):
    T = h2p.shape[0] // PACK_SUB
    tr = TR_MOE
    n_tiles = blk_exp.shape[0]
    any_spec = pl.BlockSpec(memory_space=pl.ANY)
    return pl.pallas_call(
        _moe_kernel,
        grid_spec=pltpu.PrefetchScalarGridSpec(
            num_scalar_prefetch=7,
            grid=(n_tiles,),
            in_specs=[any_spec] * 4,
            out_specs=any_spec,
            scratch_shapes=[pltpu.VMEM((T * PACK_SUB, LANES), U32),
                            pltpu.VMEM((2 * tr * PACK_SUB, LANES), U32),
                            pltpu.VMEM((2 * tr * ROW_SUB, LANES), F32),
                            pltpu.VMEM((2, D_MODEL, EXPERT_HIDDEN), F32),
                            pltpu.VMEM((2, D_MODEL, EXPERT_HIDDEN), F32),
                            pltpu.VMEM((2, EXPERT_HIDDEN, D_MODEL), F32),
                            pltpu.VMEM((D_MODEL, EXPERT_HIDDEN), BF16),
                            pltpu.VMEM((D_MODEL, EXPERT_HIDDEN), BF16),
                            pltpu.VMEM((EXPERT_HIDDEN, D_MODEL), BF16),
                            pltpu.SemaphoreType.DMA(()),
                            pltpu.SemaphoreType.DMA((2,)),
                            pltpu.SemaphoreType.DMA((2,))]),
        out_shape=jax.ShapeDtypeStruct((2 * T * ROW_SUB, LANES), F32),
        compiler_params=pltpu.CompilerParams(dimension_semantics=("arbitrary",), vmem_limit_bytes=MOE_VMEM_LIMIT),
        name="moe_experts",
    )(blk_exp, n_used, first, nxt, wslot, valid, inv, h2p, w1, w3, w2)


def _combine_kernel(x2_ref, rinfo_ref, y1_ref, y2_ref, o_ref):
    g1 = rinfo_ref[:, 4:5]
    g2 = rinfo_ref[:, 5:6]
    tc = x2_ref.shape[0]
    for c in range(ROW_SUB):
        cols = slice(c * LANES, (c + 1) * LANES)
        rows = pl.ds(c, tc, stride=ROW_SUB)
        o_ref[:, cols] = x2_ref[:, cols] + (g1 * y1_ref[rows, :] + g2 * y2_ref[rows, :])


def _combine(x2, rinfo, y2slot):
    T = x2.shape[0]
    tc = TC_COMBINE
    return pl.pallas_call(
        _combine_kernel,
        grid=(T // tc,),
        in_specs=[pl.BlockSpec((tc, D_MODEL), lambda i: (i, 0)),
                  pl.BlockSpec((tc, LANES), lambda i: (i, 0)),
                  pl.BlockSpec((tc * ROW_SUB, LANES), lambda i: (i, 0)),
                  pl.BlockSpec((tc * ROW_SUB, LANES), lambda i: (T // tc + i, 0))],
        out_specs=pl.BlockSpec((tc, D_MODEL), lambda i: (i, 0)),
        out_shape=jax.ShapeDtypeStruct((T, D_MODEL), F32),
        compiler_params=_cparams(("parallel",)),
        name="combine",
    )(x2, rinfo, y2slot, y2slot)


def _layer(x, mem, attn_norm, w_in, q_norm, k_norm, pool_proj, pool_scale, mem_norm, w_mem_kv,
           mq_norm, mk_norm, w_out, ffn_norm, w_group, b_group, w_router, b_router, w1, w3, w2):
    B, S, D = x.shape
    T = B * S
    assert D == D_MODEL and S % SUPER == 0 and T % TM_PROJ == 0
    x2d = x.reshape(T, D)
    row = lambda v: v.reshape(1, -1).astype(F32)
    scale = HEAD_DIM ** -0.5
    gq = row(jnp.tile(q_norm, ATTN_WIDTH // HEAD_DIM) * scale)
    gk = row(jnp.tile(k_norm, ATTN_WIDTH // HEAD_DIM))
    gmq = row(jnp.tile(mq_norm, MEM_WIDTH // HEAD_DIM) * scale)
    gmk = row(jnp.tile(mk_norm, MEM_WIDTH // HEAD_DIM))

    (q1, k1, v1, q4, k4, v4, q16, k16, v16, u, qm) = _proj(
        x2d, row(attn_norm), w_in.astype(BF16), gq, gk, gmq, B, S)
    ya = _attn(q1, k1, v1, q4, k4, v4, q16, k16, v16, B, S).reshape(T, ATTN_WIDTH)
    km, vm = _memkv(mem, row(mem_norm), w_mem_kv.astype(BF16), gmk, B)

    pool_bd = jax.scipy.linalg.block_diag(*[pool_proj[g] for g in range(pool_proj.shape[0])]).astype(BF16)
    w_r = jnp.zeros((D, LANES), F32)
    w_r = w_r.at[:, :N_GROUPS].set(w_group)
    w_r = w_r.at[:, EXPERT_LANE0:].set(jnp.transpose(w_router, (1, 0, 2)).reshape(D, N_EXPERTS))
    b_r = jnp.zeros((1, LANES), F32).at[0, :N_GROUPS].set(b_group).at[0, EXPERT_LANE0:].set(b_router.reshape(-1))
    x2, h2p, rinfo, rinfot, counts = _mix(ya, u, qm, km, vm, x2d, pool_bd, row(pool_scale),
                                          w_out.astype(BF16), row(ffn_norm), w_r.astype(BF16), b_r, B, S)

    R, pstart, sched = _tile_schedule(counts, T)
    dest = _dest(rinfot, pstart.astype(F32).reshape(N_EXPERTS, 1), T)
    inv = _rowmap(dest[0], R)
    y2slot = _moe(*sched, inv, h2p, w1, w3, w2)
    out = _combine(x2, rinfo, y2slot)
    return out.reshape(B, S, D)


def _tile_schedule(counts, T):
    tr = TR_MOE
    R = 2 * T + N_EXPERTS * tr
    n_tiles = R // tr
    cnt = counts[0, EXPERT_LANE0:].astype(I32)
    padded = ((cnt + tr - 1) // tr) * tr
    pend = jnp.cumsum(padded)
    pstart = pend - padded
    n_used = (pend[-1] // tr).astype(I32).reshape(1)
    tiles = jnp.arange(n_tiles, dtype=I32)
    tile_row = jnp.minimum(tiles, n_used[0] - 1) * tr
    blk_exp = jnp.minimum(jnp.sum(tile_row[:, None] >= pend[None, :], axis=1), N_EXPERTS - 1).astype(I32)
    in_use = tiles < n_used[0]
    first = in_use & ((tiles == 0) | (blk_exp != jnp.roll(blk_exp, 1)))
    run_idx = jnp.cumsum(first.astype(I32)) - 1
    used = cnt > 0
    exp_of_run = jnp.argsort(jnp.logical_not(used), stable=True).astype(I32)
    nxt = jnp.where(run_idx + 1 < jnp.sum(used), exp_of_run[jnp.clip(run_idx + 1, 0, N_EXPERTS - 1)], -1)
    valid = jnp.where(in_use, jnp.clip(cnt[blk_exp] - (tiles * tr - pstart[blk_exp]), 0, tr), 0)
    sched = (blk_exp, n_used, first.astype(I32), nxt.astype(I32), (run_idx & 1).astype(I32), valid.astype(I32))
    return R, pstart, sched


def kernel(x, mem, attn_norm, w_in, q_norm, k_norm, pool_proj, pool_scale, mem_norm, w_mem_kv, mq_norm, mk_norm,
           w_out, ffn_norm, w_group, b_group, w_router, b_router, w1, w3, w2):
    for l in range(attn_norm.shape[0]):
        x = _layer(x, mem, attn_norm[l], w_in[l], q_norm[l], k_norm[l], pool_proj[l], pool_scale[l],
                   mem_norm[l], w_mem_kv[l], mq_norm[l], mk_norm[l], w_out[l], ffn_norm[l],
                   w_group[l], b_group[l], w_router[l], b_router[l], w1[l], w3[l], w2[l])
    return x
```

```python
import functools

import jax
import jax.numpy as jnp
from jax import lax
from jax.experimental import pallas as pl
from jax.experimental.pallas import tpu as pltpu

F32 = jnp.float32
BF16 = jnp.bfloat16
I32 = jnp.int32
U32 = jnp.uint32

D_MODEL = 1024
HEAD_DIM = 64
ATTN_WIDTH = 512
POOL_WIDTH = 256
MEM_WIDTH = 256
N_MEM = 256
IN_WIDTH = 3 * ATTN_WIDTH + POOL_WIDTH + MEM_WIDTH
N_GROUPS = 8
EXPERTS_PER_GROUP = 8
N_EXPERTS = 64
EXPERT_HIDDEN = 512
EPS = 1e-6
NEG_INF = -1e30

LANES = 128
N_PAIRS = ATTN_WIDTH // LANES
BLOCK = 128
SUPER = 16 * BLOCK
HALO = 16
ROW_SUB = D_MODEL // LANES
PACKED = D_MODEL // 2

TM_PROJ = 1024
TM_MIX = 1024
TR_MOE = 256
ROW_UNROLL = 16
TC_COMBINE = 1024
EXPERT_LANE0 = 64

VMEM_LIMIT = 48 * 1024 * 1024
MOE_VMEM_LIMIT = 56 * 1024 * 1024


def _cparams(sem):
    return pltpu.CompilerParams(dimension_semantics=sem, vmem_limit_bytes=VMEM_LIMIT)


def _group_ones(n):
    g = jnp.arange(n) // HEAD_DIM
    return (g[:, None] == g[None, :]).astype(BF16)


def _head_norm(z, ones_ref, gain_ref):
    ss = jnp.dot((z * z).astype(BF16), ones_ref[...], preferred_element_type=F32)
    return z * lax.rsqrt(ss * (1.0 / HEAD_DIM) + EPS) * gain_ref[...]


def _proj_kernel(x_ref, an_ref, win_ref, gq_ref, gk_ref, gm_ref, o512_ref, o256_ref,
                 q1_ref, k1_ref, v1_ref, q4_ref, k4_ref, v4_ref, q16_ref, k16_ref, v16_ref,
                 u_ref, qm_ref, zs_ref, z4_ref):
    tm = x_ref.shape[0]
    x = x_ref[...]
    ms = jnp.mean(x * x, axis=-1, keepdims=True)
    h = (x * lax.rsqrt(ms + EPS) * an_ref[...]).astype(BF16)
    a0, a1, a2, a3 = ATTN_WIDTH, 2 * ATTN_WIDTH, 3 * ATTN_WIDTH, 3 * ATTN_WIDTH + POOL_WIDTH

    def cols(lo, hi):
        return jnp.dot(h, win_ref[:, lo:hi], preferred_element_type=F32)

    groups = ((lambda: _head_norm(cols(0, a0), o512_ref, gq_ref), q1_ref, q4_ref, q16_ref),
              (lambda: _head_norm(cols(a0, a1), o512_ref, gk_ref), k1_ref, k4_ref, k16_ref),
              (lambda: cols(a1, a2), v1_ref, v4_ref, v16_ref))
    for make, o1, o4, o16 in groups:
        val = make()
        for hp in range(N_PAIRS):
            pair = val[:, hp * LANES:(hp + 1) * LANES]
            zs_ref[hp] = pair
            o1[hp] = pair.astype(BF16)
        q4n = tm // 4
        for hp in range(N_PAIRS):
            for r4 in range(4):
                rows = zs_ref[hp, pl.ds(r4, q4n, stride=4), :]
                z4_ref[hp, pl.ds(r4 * q4n, q4n), :] = rows
                o4[hp, :, r4 * LANES:(r4 + 1) * LANES] = rows.astype(BF16)
            for r16 in range(16):
                r4, a = r16 % 4, r16 // 4
                rows = z4_ref[hp, pl.ds(r4 * q4n + a, tm // 16, stride=4), :]
                o16[hp, :, r16 * LANES:(r16 + 1) * LANES] = rows.astype(BF16)
    u_ref[...] = cols(a2, a3)
    qm_ref[...] = _head_norm(cols(a3, IN_WIDTH), o256_ref, gm_ref).astype(BF16)


def _proj(x2d, attn_norm, w_in, gq, gk, gm, B, S):
    T = B * S
    tm = TM_PROJ
    nj = S // tm
    const = lambda i: (0, 0)

    def lay(d):
        return jax.ShapeDtypeStruct((B, N_PAIRS, S // d, d * LANES), BF16)

    def lay_spec(d):
        return pl.BlockSpec((None, N_PAIRS, tm // d, d * LANES), lambda i: (i // nj, 0, i % nj, 0))

    out_shape = [lay(1)] * 3 + [lay(4)] * 3 + [lay(16)] * 3 + [
        jax.ShapeDtypeStruct((T, POOL_WIDTH), F32), jax.ShapeDtypeStruct((T, MEM_WIDTH), BF16)]
    out_specs = [lay_spec(1)] * 3 + [lay_spec(4)] * 3 + [lay_spec(16)] * 3 + [
        pl.BlockSpec((tm, POOL_WIDTH), lambda i: (i, 0)), pl.BlockSpec((tm, MEM_WIDTH), lambda i: (i, 0))]
    return pl.pallas_call(
        _proj_kernel,
        grid=(T // tm,),
        in_specs=[pl.BlockSpec((tm, D_MODEL), lambda i: (i, 0)),
                  pl.BlockSpec((1, D_MODEL), const),
                  pl.BlockSpec((D_MODEL, IN_WIDTH), const),
                  pl.BlockSpec((1, ATTN_WIDTH), const),
                  pl.BlockSpec((1, ATTN_WIDTH), const),
                  pl.BlockSpec((1, MEM_WIDTH), const),
                  pl.BlockSpec((ATTN_WIDTH, ATTN_WIDTH), const),
                  pl.BlockSpec((MEM_WIDTH, MEM_WIDTH), const)],
        out_specs=out_specs,
        out_shape=out_shape,
        scratch_shapes=[pltpu.VMEM((N_PAIRS, tm, LANES), F32)] * 2,
        compiler_params=_cparams(("parallel",)),
        name="proj",
    )(x2d, attn_norm, w_in, gq, gk, gm, _group_ones(ATTN_WIDTH), _group_ones(MEM_WIDTH))


def _attn_kernel(q1, k1c, k1p, v1c, v1p, q4, k4c, k4p, v4c, v4p, q16, k16c, k16p, v16c, v16p,
                 bias_ref, o_ref, obuf, lbuf):
    c = pl.program_id(2)
    lane = lax.broadcasted_iota(I32, (BLOCK, LANES), 1)
    is_a = lane < HEAD_DIM
    lane_row = lax.broadcasted_iota(I32, (1, LANES), 1)
    mask_a = jnp.where(lane_row < HEAD_DIM, 1.0, 0.0).astype(BF16)
    mask_b = jnp.where(lane_row < HEAD_DIM, 0.0, 1.0).astype(BF16)
    bias_full = bias_ref[0]
    bias_first = jnp.where(c > 0, bias_full, bias_ref[1])

    def tile(q_t, kp_t, kc_t, vp_t, vc_t, bias):
        lhs = jnp.concatenate([q_t * mask_a, q_t * mask_b], axis=0)
        keys = jnp.concatenate([kp_t, kc_t], axis=0)
        s = lax.dot_general(lhs, keys, (((1,), (1,)), ((), ())), preferred_element_type=F32) + bias
        m = jnp.max(s, axis=-1, keepdims=True)
        p = jnp.exp(s - m)
        l = jnp.sum(p, axis=-1, keepdims=True)
        vals = jnp.concatenate([vp_t, vc_t], axis=0)
        pv = jnp.dot(p.astype(BF16), vals, preferred_element_type=F32) * (1.0 / l)
        lse = m + jnp.log(l)
        return jnp.where(is_a, pv[:BLOCK], pv[BLOCK:]), jnp.where(is_a, lse[:BLOCK], lse[BLOCK:])

    for jb in range(SUPER // BLOCK):
        cur = pl.ds(jb * BLOCK, BLOCK)
        if jb == 0:
            kp_t, vp_t, bias = k1p[...], v1p[...], bias_first
        else:
            prev = pl.ds((jb - 1) * BLOCK, BLOCK)
            kp_t, vp_t, bias = k1c[prev, :], v1c[prev, :], bias_full
        o_t, lse_t = tile(q1[cur, :], kp_t, k1c[cur, :], vp_t, v1c[cur, :], bias)
        obuf[0, cur, :] = o_t
        lbuf[0, cur, :] = lse_t
    for pat, (d, q, kc, kp, vc, vp) in enumerate(((4, q4, k4c, k4p, v4c, v4p), (16, q16, k16c, k16p, v16c, v16p)), 1):
        nblk = SUPER // (BLOCK * d)
        for r in range(d):
            cols = pl.ds(r * LANES, LANES)
            for jb in range(nblk):
                cur = pl.ds(jb * BLOCK, BLOCK)
                if jb == 0:
                    kp_t, vp_t, bias = kp[:, cols], vp[:, cols], bias_first
                else:
                    prev = pl.ds((jb - 1) * BLOCK, BLOCK)
                    kp_t, vp_t, bias = kc[prev, cols], vc[prev, cols], bias_full
                o_t, lse_t = tile(q[cur, cols], kp_t, kc[cur, cols], vp_t, vc[cur, cols], bias)
                rows = pl.ds(jb * BLOCK * d + r, BLOCK, stride=d)
                obuf[pat, rows, :] = o_t
                lbuf[pat, rows, :] = lse_t
    for jb in range(SUPER // BLOCK):
        cur = pl.ds(jb * BLOCK, BLOCK)
        l0, l1, l2 = lbuf[0, cur, :], lbuf[1, cur, :], lbuf[2, cur, :]
        top = jnp.maximum(jnp.maximum(l0, l1), l2)
        w0, w1, w2 = jnp.exp(l0 - top), jnp.exp(l1 - top), jnp.exp(l2 - top)
        mixed = (w0 * obuf[0, cur, :] + w1 * obuf[1, cur, :] + w2 * obuf[2, cur, :]) * (1.0 / (w0 + w1 + w2))
        o_ref[cur, :] = mixed.astype(BF16)


def _band_bias():
    qi = jnp.arange(BLOCK)[:, None]
    kj = jnp.arange(2 * BLOCK)[None, :]
    dist = qi + BLOCK - kj
    in_band = (dist >= 0) & (dist <= BLOCK)
    full = jnp.where(in_band, 0.0, NEG_INF).astype(F32)
    first = jnp.where(in_band & (kj >= BLOCK), 0.0, NEG_INF).astype(F32)
    return jnp.stack([jnp.tile(full, (2, 1)), jnp.tile(first, (2, 1))])


def _attn(q1, k1, v1, q4, k4, v4, q16, k16, v16, B, S):
    nsup = S // SUPER

    def specs(d):
        rows = SUPER // d
        per = rows // BLOCK
        cur = pl.BlockSpec((None, None, rows, d * LANES), lambda b, hp, c: (b, hp, c, 0))
        prev = pl.BlockSpec((None, None, BLOCK, d * LANES),
                            lambda b, hp, c: (b, hp, jnp.maximum(per * c - 1, 0), 0))
        return [cur, cur, prev, cur, prev]

    return pl.pallas_call(
        _attn_kernel,
        grid=(B, N_PAIRS, nsup),
        in_specs=specs(1) + specs(4) + specs(16) + [
            pl.BlockSpec((2, 2 * BLOCK, 2 * BLOCK), lambda b, hp, c: (0, 0, 0))],
        out_specs=pl.BlockSpec((None, SUPER, LANES), lambda b, hp, c: (b, c, hp)),
        out_shape=jax.ShapeDtypeStruct((B, S, ATTN_WIDTH), BF16),
        scratch_shapes=[pltpu.VMEM((3, SUPER, LANES), F32)] * 2,
        compiler_params=_cparams(("parallel", "parallel", "parallel")),
        name="dilated_attn",
    )(q1, k1, k1, v1, v1, q4, k4, k4, v4, v4, q16, k16, k16, v16, v16, _band_bias())


def _memkv_kernel(mem_ref, mn_ref, wkv_ref, gk_ref, o256_ref, km_ref, vm_ref):
    m = mem_ref[...]
    ms = jnp.mean(m * m, axis=-1, keepdims=True)
    mn = (m * lax.rsqrt(ms + EPS) * mn_ref[...]).astype(BF16)
    kv = jnp.dot(mn, wkv_ref[...], preferred_element_type=F32)
    km_ref[...] = _head_norm(kv[:, :MEM_WIDTH], o256_ref, gk_ref).astype(BF16)
    vm_ref[...] = kv[:, MEM_WIDTH:].astype(BF16)


def _memkv(mem, mem_norm, w_mem_kv, gmk, B):
    const = lambda b: (0, 0)
    return pl.pallas_call(
        _memkv_kernel,
        grid=(B,),
        in_specs=[pl.BlockSpec((None, N_MEM, D_MODEL), lambda b: (b, 0, 0)),
                  pl.BlockSpec((1, D_MODEL), const),
                  pl.BlockSpec((D_MODEL, 2 * MEM_WIDTH), const),
                  pl.BlockSpec((1, MEM_WIDTH), const),
                  pl.BlockSpec((MEM_WIDTH, MEM_WIDTH), const)],
        out_specs=[pl.BlockSpec((None, N_MEM, MEM_WIDTH), lambda b: (b, 0, 0))] * 2,
        out_shape=[jax.ShapeDtypeStruct((B, N_MEM, MEM_WIDTH), BF16)] * 2,
        compiler_params=_cparams(("parallel",)),
        name="memkv",
    )(mem, mem_norm, w_mem_kv, gmk, _group_ones(MEM_WIDTH))


def _mix_kernel(tiles_per_seq, ya_ref, u_ref, uh_ref, qm_ref, km_ref, vm_ref, x_ref, pp_ref, ps_ref,
                wo_ref, fn_ref, wr_ref, br_ref, ltri_ref,
                x2_ref, h2p_ref, rinfo_ref, rinfot_ref, cnt_ref, base_ref):
    i = pl.program_id(0)
    tm = x_ref.shape[0]
    seq_tile = i % tiles_per_seq

    @pl.when(i == 0)
    def _():
        base_ref[...] = jnp.zeros_like(base_ref)

    u = u_ref[...]
    halo = jnp.where(seq_tile == 0, 0.0, uh_ref[...])
    uu = jnp.concatenate([halo, u], axis=0)
    a1 = uu[1:] + uu[:-1]
    a2 = a1[2:] + a1[:-2]
    a3 = a2[4:] + a2[:-4]
    a4 = a3[8:] + a3[:-8]
    lane_p = lax.broadcasted_iota(I32, (tm, POOL_WIDTH), 1)
    g0, g1, g2 = lane_p < 64, lane_p < 128, lane_p < 192
    wsum = jnp.where(g0, a1[15:], jnp.where(g1, a2[13:], jnp.where(g2, a3[9:], a4[1:])))
    wlen = jnp.where(g0, 2.0, jnp.where(g1, 4.0, jnp.where(g2, 8.0, 16.0)))
    tpos = seq_tile * tm + lax.broadcasted_iota(I32, (tm, POOL_WIDTH), 0) + 1
    cnt = jnp.minimum(tpos.astype(F32), wlen)
    pooled = wsum / cnt - u
    y_pool = jnp.dot(pooled.astype(BF16), pp_ref[...], preferred_element_type=F32) * ps_ref[...]

    lane = lax.broadcasted_iota(I32, (tm, LANES), 1)
    is_a = lane < HEAD_DIM
    lane_row = lax.broadcasted_iota(I32, (1, LANES), 1)
    mask_a = jnp.where(lane_row < HEAD_DIM, 1.0, 0.0).astype(BF16)
    mask_b = jnp.where(lane_row < HEAD_DIM, 0.0, 1.0).astype(BF16)
    y_mem = []
    for pr in range(MEM_WIDTH // LANES):
        cols = slice(pr * LANES, (pr + 1) * LANES)
        qp, kp, vp = qm_ref[:, cols], km_ref[:, cols], vm_ref[:, cols]
        outs = []
        for msk in (mask_a, mask_b):
            s = lax.dot_general(qp * msk, kp, (((1,), (1,)), ((), ())), preferred_element_type=F32)
            m = jnp.max(s, axis=-1, keepdims=True)
            p = jnp.exp(s - m)
            p = p / jnp.sum(p, axis=-1, keepdims=True)
            outs.append(jnp.dot(p.astype(BF16), vp, preferred_element_type=F32))
        y_mem.append(jnp.where(is_a, outs[0], outs[1]))

    rest = jnp.concatenate([y_pool] + y_mem, axis=1).astype(BF16)
    proj = jnp.dot(ya_ref[...], wo_ref[:ATTN_WIDTH, :], preferred_element_type=F32)
    proj += jnp.dot(rest, wo_ref[ATTN_WIDTH:, :], preferred_element_type=F32)
    x2 = x_ref[...] + proj
    x2_ref[...] = x2

    ms = jnp.mean(x2 * x2, axis=-1, keepdims=True)
    h2 = (x2 * lax.rsqrt(ms + EPS) * fn_ref[...]).astype(BF16)
    bits = lax.bitcast_convert_type(h2.astype(F32), U32)
    h2p_ref[...] = bits[:, :PACKED] | (bits[:, PACKED:] >> 16)

    logits = jnp.dot(h2, wr_ref[...], preferred_element_type=F32) + br_ref[...]
    ninf = -jnp.inf
    lane_f = lane.astype(F32)
    grp_f = ((lane - EXPERT_LANE0) >> 3).astype(F32)
    is_g = lane < N_GROUPS
    lg = jnp.where(is_g, logits, ninf)
    mg = jnp.max(lg, axis=-1, keepdims=True)
    g_sel = jnp.min(jnp.where(lg == mg, lane_f, float(LANES)), axis=-1, keepdims=True)
    w_g = 1.0 / jnp.sum(jnp.where(is_g, jnp.exp(logits - mg), 0.0), axis=-1, keepdims=True)
    in_grp = (lane >= EXPERT_LANE0) & (grp_f == g_sel)
    le = jnp.where(in_grp, logits, ninf)
    v1 = jnp.max(le, axis=-1, keepdims=True)
    i1 = jnp.min(jnp.where(le == v1, lane_f, float(2 * LANES)), axis=-1, keepdims=True)
    le2 = jnp.where(lane_f == i1, ninf, le)
    v2 = jnp.max(le2, axis=-1, keepdims=True)
    i2 = jnp.min(jnp.where(le2 == v2, lane_f, float(2 * LANES)), axis=-1, keepdims=True)
    e21 = jnp.exp(v2 - v1)
    gate1 = w_g / (1.0 + e21)
    gate2 = w_g * e21 / (1.0 + e21)

    hot1 = lane_f == i1
    hot2 = lane_f == i2
    onehot = jnp.where(hot1 | hot2, 1.0, 0.0)
    prefix = jnp.dot(ltri_ref[...], onehot.astype(BF16), preferred_element_type=F32)
    tot = prefix + base_ref[...]
    rank1 = jnp.sum(jnp.where(hot1, tot, 0.0), axis=-1, keepdims=True)
    rank2 = jnp.sum(jnp.where(hot2, tot, 0.0), axis=-1, keepdims=True)
    base_new = base_ref[...] + jnp.sum(onehot, axis=0, keepdims=True)
    base_ref[...] = base_new
    cnt_ref[...] = base_new

    e1 = i1 - float(EXPERT_LANE0)
    e2 = i2 - float(EXPERT_LANE0)
    rinfo = jnp.where(lane == 0, e1, jnp.where(lane == 1, e2, jnp.where(lane == 2, rank1, jnp.where(
        lane == 3, rank2, jnp.where(lane == 4, gate1, jnp.where(lane == 5, gate2, 0.0))))))
    rinfo_ref[...] = rinfo
    rinfot_ref[...] = jnp.transpose(rinfo)[:8, :]


def _mix(ya, u, qm, km, vm, x2d, pool_bd, pool_scale, w_out, ffn_norm, w_r, b_r, B, S):
    T = B * S
    tm = TM_MIX
    tps = S // tm
    hb = tm // HALO
    const = lambda i: (0, 0)
    ltri = (jnp.arange(tm)[:, None] > jnp.arange(tm)[None, :]).astype(BF16)
    return pl.pallas_call(
        functools.partial(_mix_kernel, tps),
        grid=(T // tm,),
        in_specs=[pl.BlockSpec((tm, ATTN_WIDTH), lambda i: (i, 0)),
                  pl.BlockSpec((tm, POOL_WIDTH), lambda i: (i, 0)),
                  pl.BlockSpec((HALO, POOL_WIDTH), lambda i: (jnp.maximum(i * hb - 1, 0), 0)),
                  pl.BlockSpec((tm, MEM_WIDTH), lambda i: (i, 0)),
                  pl.BlockSpec((None, N_MEM, MEM_WIDTH), lambda i: (i // tps, 0, 0)),
                  pl.BlockSpec((None, N_MEM, MEM_WIDTH), lambda i: (i // tps, 0, 0)),
                  pl.BlockSpec((tm, D_MODEL), lambda i: (i, 0)),
                  pl.BlockSpec((POOL_WIDTH, POOL_WIDTH), const),
                  pl.BlockSpec((1, POOL_WIDTH), const),
                  pl.BlockSpec((D_MODEL, D_MODEL), const),
                  pl.BlockSpec((1, D_MODEL), const),
                  pl.BlockSpec((D_MODEL, LANES), const),
                  pl.BlockSpec((1, LANES), const),
                  pl.BlockSpec((tm, tm), const)],
        out_specs=[pl.BlockSpec((tm, D_MODEL), lambda i: (i, 0)),
                   pl.BlockSpec((tm, PACKED), lambda i: (i, 0)),
                   pl.BlockSpec((tm, LANES), lambda i: (i, 0)),
                   pl.BlockSpec((8, tm), lambda i: (0, i)),
                   pl.BlockSpec((1, LANES), const)],
        out_shape=[jax.ShapeDtypeStruct((T, D_MODEL), F32),
                   jax.ShapeDtypeStruct((T, PACKED), U32),
                   jax.ShapeDtypeStruct((T, LANES), F32),
                   jax.ShapeDtypeStruct((8, T), F32),
                   jax.ShapeDtypeStruct((1, LANES), F32)],
        scratch_shapes=[pltpu.VMEM((1, LANES), F32)],
        compiler_params=_cparams(("arbitrary",)),
        name="mix_router",
    )(ya, u, u, qm, km, vm, x2d, pool_bd, pool_scale, w_out, ffn_norm, w_r, b_r, ltri)


def _dest_kernel(rt_ref, ps_ref, d_ref):
    tn = rt_ref.shape[1]
    sub = lax.broadcasted_iota(I32, (N_EXPERTS, tn), 0).astype(F32)
    ps = ps_ref[...]
    rows = []
    for k in range(2):
        e = rt_ref[k:k + 1, :]
        start = jnp.sum(jnp.where(sub == e, ps, 0.0), axis=0, keepdims=True)
        rows.append(start + rt_ref[2 + k:3 + k, :])
    word = rows[0].astype(I32) | (rows[1].astype(I32) << 16)
    d_ref[...] = jnp.concatenate([word, jnp.zeros((7, tn), I32)], axis=0)


def _dest(rinfot, pstart_col, T):
    tn = 2048
    return pl.pallas_call(
        _dest_kernel,
        grid=(T // tn,),
        in_specs=[pl.BlockSpec((8, tn), lambda i: (0, i)),
                  pl.BlockSpec((N_EXPERTS, 1), lambda i: (0, 0))],
        out_specs=pl.BlockSpec((8, tn), lambda i: (0, i)),
        out_shape=jax.ShapeDtypeStruct((8, T), I32),
        compiler_params=_cparams(("parallel",)),
        name="dest_rows",
    )(rinfot, pstart_col)


def _rowmap_kernel(d_ref, fill_hbm, inv_ref, sem):
    T = d_ref.shape[0]
    cp = pltpu.make_async_copy(fill_hbm, inv_ref, sem)
    cp.start()
    cp.wait()
    unroll = 8

    def body(g, carry):
        t0 = g * unroll
        words = [d_ref[t0 + u] for u in range(unroll)]
        for u, w in enumerate(words):
            inv_ref[w & 0xFFFF] = t0 + u
            inv_ref[lax.shift_right_logical(w, 16)] = T + t0 + u
        return carry

    lax.fori_loop(0, T // unroll, body, 0)


def _rowmap(dword, R):
    assert R <= 1 << 16
    return pl.pallas_call(
        _rowmap_kernel,
        grid_spec=pltpu.PrefetchScalarGridSpec(
            num_scalar_prefetch=1,
            grid=(1,),
            in_specs=[pl.BlockSpec(memory_space=pl.ANY)],
            out_specs=pl.BlockSpec(memory_space=pltpu.SMEM),
            scratch_shapes=[pltpu.SemaphoreType.DMA(())]),
        out_shape=jax.ShapeDtypeStruct((R,), I32),
        compiler_params=_cparams(("arbitrary",)),
        name="rowmap",
    )(dword, jnp.zeros((R,), I32))


def _row_loop(n, body):
    shift = ROW_UNROLL.bit_length() - 1

    def group(g, carry):
        for u in range(ROW_UNROLL):
            body(g * ROW_UNROLL + u)
        return carry

    def rest(i, carry):
        body(((n >> shift) << shift) + i)
        return carry

    lax.fori_loop(0, n >> shift, group, 0)
    lax.fori_loop(0, n & (ROW_UNROLL - 1), rest, 0)


def _moe_kernel(be_ref, nu_ref, first_ref, nxt_ref, ws_ref, valid_ref, inv_ref,
                h_hbm, w1_hbm, w3_hbm, w2_hbm, o_hbm,
                tab, xbuf0, xbuf1, a0, b0, a1, b1, ybuf, w1s, w3s, w2s, w1b, w3b, w2b, tsem, ssem, wsem):
    j = pl.program_id(0)
    nu = nu_ref[0]
    T = h_hbm.shape[0]
    tr = TR_MOE
    n_tiles = inv_ref.shape[0] // tr
    xbuf = (xbuf0, xbuf1)
    abuf = ((a0, b0), (a1, b1))

    def tile_rows(ref, row, n=1):
        return ref.at[pl.ds(pl.multiple_of(row * ROW_SUB, ROW_SUB), n * ROW_SUB)]

    def gather_tile(tile, slot):
        for i in range(tr):
            v = inv_ref[tile * tr + i]
            tok = (v & (T - 1)) if T & (T - 1) == 0 else jnp.where(v >= T, v - T, v)
            xbuf[slot][pl.ds(i, 1), :] = tab[pl.ds(tok, 1), :]

    def wait_rows(n, done):
        @pl.when(n == tr)
        def _():
            done(tr).wait()

        @pl.when(n != tr)
        def _():
            lax.fori_loop(0, n >> 3, lambda g, c: (done(8).wait(), c)[1], 0)
            lax.fori_loop(0, n & 7, lambda g, c: (done(1).wait(), c)[1], 0)

    def scatter_row(tile, slot, i):
        v = inv_ref[tile * tr + i]
        return pltpu.make_async_copy(tile_rows(ybuf, slot * tr + i), tile_rows(o_hbm, v), ssem.at[slot])

    def scatter_wait(slot, n):
        wait_rows(n, lambda rows: pltpu.make_async_copy(
            tile_rows(ybuf, slot * tr, rows), tile_rows(o_hbm, 0, rows), ssem.at[slot]))

    def weight_copies(e, slot):
        return [pltpu.make_async_copy(src.at[e], dst.at[slot], wsem.at[slot])
                for src, dst in ((w1_hbm, w1s), (w3_hbm, w3s), (w2_hbm, w2s))]

    def up_project(slot):
        w = xbuf[slot][...]
        lo = lax.bitcast_convert_type(w & jnp.uint32(0xFFFF0000), F32).astype(BF16)
        hi = lax.bitcast_convert_type(w << 16, F32).astype(BF16)
        for dst, wb in zip(abuf[slot], (w1b, w3b)):
            dst[...] = (jnp.dot(lo, wb[:PACKED, :], preferred_element_type=F32)
                        + jnp.dot(hi, wb[PACKED:, :], preferred_element_type=F32))

    @pl.when(j < nu)
    def _():
        s = j & 1
        jn = jnp.minimum(j + 1, n_tiles - 1)

        @pl.when(j == 0)
        def _():
            table = pltpu.make_async_copy(h_hbm, tab, tsem)
            table.start()
            for cp in weight_copies(be_ref[0], 0):
                cp.start()
            table.wait()
            gather_tile(0, 0)
            gather_tile(jnp.minimum(1, n_tiles - 1), 1)
            for cp in weight_copies(be_ref[0], 0):
                cp.wait()

            @pl.when(nxt_ref[0] >= 0)
            def _():
                for cp in weight_copies(nxt_ref[0], 1):
                    cp.start()

            w1b[...] = w1s[0].astype(BF16)
            w3b[...] = w3s[0].astype(BF16)
            w2b[...] = w2s[0].astype(BF16)
            up_project(0)

        @pl.when((j >= 1) & (first_ref[j] == 1))
        def _():
            w2b[...] = w2s[ws_ref[j]].astype(BF16)

        @pl.when((j + 1 < nu) & (first_ref[jn] == 1))
        def _():
            wslot = ws_ref[jn]
            for cp in weight_copies(be_ref[jn], wslot):
                cp.wait()
            e_next = nxt_ref[jn]

            @pl.when(e_next >= 0)
            def _():
                for cp in weight_copies(e_next, 1 - wslot):
                    cp.start()

            w1b[...] = w1s[wslot].astype(BF16)
            w3b[...] = w3s[wslot].astype(BF16)

        @pl.when(j >= 2)
        def _():
            scatter_wait(s, valid_ref[j - 2])

        for slot in range(2):
            @pl.when(s == slot)
            def _():
                gather_tile(jnp.minimum(j + 2, n_tiles - 1), slot)
                a = abuf[slot][0][...]
                b = abuf[slot][1][...]
                up_project(1 - slot)
                hmid = (a / (1.0 + jnp.exp(-a)) * b).astype(BF16)
                y = jnp.dot(hmid, w2b[...], preferred_element_type=F32)
                for c in range(ROW_SUB):
                    ybuf[pl.ds(slot * (tr * ROW_SUB) + c, tr, stride=ROW_SUB), :] = y[:, c * LANES:(c + 1) * LANES]

        @pl.when(valid_ref[j] == tr)
        def _():
            for i in range(tr):
                scatter_row(j, s, i).start()

        @pl.when(valid_ref[j] != tr)
        def _():
            _row_loop(valid_ref[j], lambda i: scatter_row(j, s, i).start())

        @pl.when(j == nu - 1)
        def _():
            scatter_wait(s, valid_ref[j])

            @pl.when(j >= 1)
            def _():
                scatter_wait(1 - s, valid_ref[j - 1])


def _moe(blk_exp, n_used, first, nxt, wslot, valid, inv, h2p, w1, w3, w2):
    T = h2p.shape[0]
    tr = TR_MOE
    n_tiles = blk_exp.shape[0]
    any_spec = pl.BlockSpec(memory_space=pl.ANY)
    return pl.pallas_call(
        _moe_kernel,
        grid_spec=pltpu.PrefetchScalarGridSpec(
            num_scalar_prefetch=7,
            grid=(n_tiles,),
            in_specs=[any_spec] * 4,
            out_specs=any_spec,
            scratch_shapes=[pltpu.VMEM((T, PACKED), U32),
                            pltpu.VMEM((tr, PACKED), U32),
                            pltpu.VMEM((tr, PACKED), U32),
                            pltpu.VMEM((tr, EXPERT_HIDDEN), F32),
                            pltpu.VMEM((tr, EXPERT_HIDDEN), F32),
                            pltpu.VMEM((tr, EXPERT_HIDDEN), F32),
                            pltpu.VMEM((tr, EXPERT_HIDDEN), F32),
                            pltpu.VMEM((2 * tr * ROW_SUB, LANES), F32),
                            pltpu.VMEM((2, D_MODEL, EXPERT_HIDDEN), F32),
                            pltpu.VMEM((2, D_MODEL, EXPERT_HIDDEN), F32),
                            pltpu.VMEM((2, EXPERT_HIDDEN, D_MODEL), F32),
                            pltpu.VMEM((D_MODEL, EXPERT_HIDDEN), BF16),
                            pltpu.VMEM((D_MODEL, EXPERT_HIDDEN), BF16),
                            pltpu.VMEM((EXPERT_HIDDEN, D_MODEL), BF16),
                            pltpu.SemaphoreType.DMA(()),
                            pltpu.SemaphoreType.DMA((2,)),
                            pltpu.SemaphoreType.DMA((2,))]),
        out_shape=jax.ShapeDtypeStruct((2 * T * ROW_SUB, LANES), F32),
        compiler_params=pltpu.CompilerParams(dimension_semantics=("arbitrary",), vmem_limit_bytes=MOE_VMEM_LIMIT),
        name="moe_experts",
    )(blk_exp, n_used, first, nxt, wslot, valid, inv, h2p, w1, w3, w2)


def _combine_kernel(x2_ref, rinfo_ref, y1_ref, y2_ref, o_ref):
    g1 = rinfo_ref[:, 4:5]
    g2 = rinfo_ref[:, 5:6]
    tc = x2_ref.shape[0]
    for c in range(ROW_SUB):
        cols = slice(c * LANES, (c + 1) * LANES)
        rows = pl.ds(c, tc, stride=ROW_SUB)
        o_ref[:, cols] = x2_ref[:, cols] + (g1 * y1_ref[rows, :] + g2 * y2_ref[rows, :])


def _combine(x2, rinfo, y2slot):
    T = x2.shape[0]
    tc = TC_COMBINE
    return pl.pallas_call(
        _combine_kernel,
        grid=(T // tc,),
        in_specs=[pl.BlockSpec((tc, D_MODEL), lambda i: (i, 0)),
                  pl.BlockSpec((tc, LANES), lambda i: (i, 0)),
                  pl.BlockSpec((tc * ROW_SUB, LANES), lambda i: (i, 0)),
                  pl.BlockSpec((tc * ROW_SUB, LANES), lambda i: (T // tc + i, 0))],
        out_specs=pl.BlockSpec((tc, D_MODEL), lambda i: (i, 0)),
        out_shape=jax.ShapeDtypeStruct((T, D_MODEL), F32),
        compiler_params=_cparams(("parallel",)),
        name="combine",
    )(x2, rinfo, y2slot, y2slot)


def _layer(x, mem, attn_norm, w_in, q_norm, k_norm, pool_proj, pool_scale, mem_norm, w_mem_kv,
           mq_norm, mk_norm, w_out, ffn_norm, w_group, b_group, w_router, b_router, w1, w3, w2):
    B, S, D = x.shape
    T = B * S
    assert D == D_MODEL and S % SUPER == 0 and T % TM_PROJ == 0
    x2d = x.reshape(T, D)
    row = lambda v: v.reshape(1, -1).astype(F32)
    scale = HEAD_DIM ** -0.5
    gq = row(jnp.tile(q_norm, ATTN_WIDTH // HEAD_DIM) * scale)
    gk = row(jnp.tile(k_norm, ATTN_WIDTH // HEAD_DIM))
    gmq = row(jnp.tile(mq_norm, MEM_WIDTH // HEAD_DIM) * scale)
    gmk = row(jnp.tile(mk_norm, MEM_WIDTH // HEAD_DIM))

    (q1, k1, v1, q4, k4, v4, q16, k16, v16, u, qm) = _proj(
        x2d, row(attn_norm), w_in.astype(BF16), gq, gk, gmq, B, S)
    ya = _attn(q1, k1, v1, q4, k4, v4, q16, k16, v16, B, S).reshape(T, ATTN_WIDTH)
    km, vm = _memkv(mem, row(mem_norm), w_mem_kv.astype(BF16), gmk, B)

    pool_bd = jax.scipy.linalg.block_diag(*[pool_proj[g] for g in range(pool_proj.shape[0])]).astype(BF16)
    w_r = jnp.zeros((D, LANES), F32)
    w_r = w_r.at[:, :N_GROUPS].set(w_group)
    w_r = w_r.at[:, EXPERT_LANE0:].set(jnp.transpose(w_router, (1, 0, 2)).reshape(D, N_EXPERTS))
    b_r = jnp.zeros((1, LANES), F32).at[0, :N_GROUPS].set(b_group).at[0, EXPERT_LANE0:].set(b_router.reshape(-1))
    x2, h2p, rinfo, rinfot, counts = _mix(ya, u, qm, km, vm, x2d, pool_bd, row(pool_scale),
                                          w_out.astype(BF16), row(ffn_norm), w_r.astype(BF16), b_r, B, S)

    R, pstart, sched = _tile_schedule(counts, T)
    dest = _dest(rinfot, pstart.astype(F32).reshape(N_EXPERTS, 1), T)
    inv = _rowmap(dest[0], R)
    y2slot = _moe(*sched, inv, h2p, w1, w3, w2)
    out = _combine(x2, rinfo, y2slot)
    return out.reshape(B, S, D)


def _tile_schedule(counts, T):
    tr = TR_MOE
    R = 2 * T + N_EXPERTS * tr
    n_tiles = R // tr
    cnt = counts[0, EXPERT_LANE0:].astype(I32)
    padded = ((cnt + tr - 1) // tr) * tr
    pend = jnp.cumsum(padded)
    pstart = pend - padded
    n_used = (pend[-1] // tr).astype(I32).reshape(1)
    tiles = jnp.arange(n_tiles, dtype=I32)
    tile_row = jnp.minimum(tiles, n_used[0] - 1) * tr
    blk_exp = jnp.minimum(jnp.sum(tile_row[:, None] >= pend[None, :], axis=1), N_EXPERTS - 1).astype(I32)
    in_use = tiles < n_used[0]
    first = in_use & ((tiles == 0) | (blk_exp != jnp.roll(blk_exp, 1)))
    run_idx = jnp.cumsum(first.astype(I32)) - 1
    used = cnt > 0
    exp_of_run = jnp.argsort(jnp.logical_not(used), stable=True).astype(I32)
    nxt = jnp.where(run_idx + 1 < jnp.sum(used), exp_of_run[jnp.clip(run_idx + 1, 0, N_EXPERTS - 1)], -1)
    valid = jnp.where(in_use, jnp.clip(cnt[blk_exp] - (tiles * tr - pstart[blk_exp]), 0, tr), 0)
    sched = (blk_exp, n_used, first.astype(I32), nxt.astype(I32), (run_idx & 1).astype(I32), valid.astype(I32))
    return R, pstart, sched


def kernel(x, mem, attn_norm, w_in, q_norm, k_norm, pool_proj, pool_scale, mem_norm, w_mem_kv, mq_norm, mk_norm,
           w_out, ffn_norm, w_group, b_group, w_router, b_router, w1, w3, w2):
    for l in range(attn_norm.shape[0]):
        x = _layer(x, mem, attn_norm[l], w_in[l], q_norm[l], k_norm[l], pool_proj[l], pool_scale[l],
                   mem_norm[l], w_mem_kv[l], mq_norm[l], mk_norm[l], w_out[l], ffn_norm[l],
                   w_group[l], b_group[l], w_router[l], b_router[l], w1[l], w3[l], w2[l])
    return x
```

```python
import functools

import jax
import jax.numpy as jnp
from jax import lax
from jax.experimental import pallas as pl
from jax.experimental.pallas import tpu as pltpu

F32 = jnp.float32
BF16 = jnp.bfloat16
I32 = jnp.int32
U32 = jnp.uint32

D_MODEL = 1024
HEAD_DIM = 64
ATTN_WIDTH = 512
POOL_WIDTH = 256
MEM_WIDTH = 256
N_MEM = 256
IN_WIDTH = 3 * ATTN_WIDTH + POOL_WIDTH + MEM_WIDTH
N_GROUPS = 8
EXPERTS_PER_GROUP = 8
N_EXPERTS = 64
EXPERT_HIDDEN = 512
EPS = 1e-6
NEG_INF = -1e30

LANES = 128
N_PAIRS = ATTN_WIDTH // LANES
BLOCK = 128
SUPER = 16 * BLOCK
HALO = 16
ROW_SUB = D_MODEL // LANES
PACKED = D_MODEL // 2

TM_PROJ = 1024
TM_MIX = 1024
TR_MOE = 256
TC_COMBINE = 1024
EXPERT_LANE0 = 64

VMEM_LIMIT = 48 * 1024 * 1024
MOE_VMEM_LIMIT = 56 * 1024 * 1024


def _cparams(sem):
    return pltpu.CompilerParams(dimension_semantics=sem, vmem_limit_bytes=VMEM_LIMIT)


def _group_ones(n):
    g = jnp.arange(n) // HEAD_DIM
    return (g[:, None] == g[None, :]).astype(BF16)


def _head_norm(z, ones_ref, gain_ref):
    ss = jnp.dot((z * z).astype(BF16), ones_ref[...], preferred_element_type=F32)
    return z * lax.rsqrt(ss * (1.0 / HEAD_DIM) + EPS) * gain_ref[...]


def _proj_kernel(x_ref, an_ref, win_ref, gq_ref, gk_ref, gm_ref, o512_ref, o256_ref,
                 q1_ref, k1_ref, v1_ref, q4_ref, k4_ref, v4_ref, q16_ref, k16_ref, v16_ref,
                 u_ref, qm_ref, zs_ref, z4_ref):
    tm = x_ref.shape[0]
    x = x_ref[...]
    ms = jnp.mean(x * x, axis=-1, keepdims=True)
    h = (x * lax.rsqrt(ms + EPS) * an_ref[...]).astype(BF16)
    a0, a1, a2, a3 = ATTN_WIDTH, 2 * ATTN_WIDTH, 3 * ATTN_WIDTH, 3 * ATTN_WIDTH + POOL_WIDTH

    def cols(lo, hi):
        return jnp.dot(h, win_ref[:, lo:hi], preferred_element_type=F32)

    groups = ((lambda: _head_norm(cols(0, a0), o512_ref, gq_ref), q1_ref, q4_ref, q16_ref),
              (lambda: _head_norm(cols(a0, a1), o512_ref, gk_ref), k1_ref, k4_ref, k16_ref),
              (lambda: cols(a1, a2), v1_ref, v4_ref, v16_ref))
    for make, o1, o4, o16 in groups:
        val = make()
        for hp in range(N_PAIRS):
            pair = val[:, hp * LANES:(hp + 1) * LANES]
            zs_ref[hp] = pair
            o1[hp] = pair.astype(BF16)
        q4n = tm // 4
        for hp in range(N_PAIRS):
            for r4 in range(4):
                rows = zs_ref[hp, pl.ds(r4, q4n, stride=4), :]
                z4_ref[hp, pl.ds(r4 * q4n, q4n), :] = rows
                o4[hp, :, r4 * LANES:(r4 + 1) * LANES] = rows.astype(BF16)
            for r16 in range(16):
                r4, a = r16 % 4, r16 // 4
                rows = z4_ref[hp, pl.ds(r4 * q4n + a, tm // 16, stride=4), :]
                o16[hp, :, r16 * LANES:(r16 + 1) * LANES] = rows.astype(BF16)
    u_ref[...] = cols(a2, a3)
    qm_ref[...] = _head_norm(cols(a3, IN_WIDTH), o256_ref, gm_ref).astype(BF16)


def _proj(x2d, attn_norm, w_in, gq, gk, gm, B, S):
    T = B * S
    tm = TM_PROJ
    nj = S // tm
    const = lambda i: (0, 0)

    def lay(d):
        return jax.ShapeDtypeStruct((B, N_PAIRS, S // d, d * LANES), BF16)

    def lay_spec(d):
        return pl.BlockSpec((None, N_PAIRS, tm // d, d * LANES), lambda i: (i // nj, 0, i % nj, 0))

    out_shape = [lay(1)] * 3 + [lay(4)] * 3 + [lay(16)] * 3 + [
        jax.ShapeDtypeStruct((T, POOL_WIDTH), F32), jax.ShapeDtypeStruct((T, MEM_WIDTH), BF16)]
    out_specs = [lay_spec(1)] * 3 + [lay_spec(4)] * 3 + [lay_spec(16)] * 3 + [
        pl.BlockSpec((tm, POOL_WIDTH), lambda i: (i, 0)), pl.BlockSpec((tm, MEM_WIDTH), lambda i: (i, 0))]
    return pl.pallas_call(
        _proj_kernel,
        grid=(T // tm,),
        in_specs=[pl.BlockSpec((tm, D_MODEL), lambda i: (i, 0)),
                  pl.BlockSpec((1, D_MODEL), const),
                  pl.BlockSpec((D_MODEL, IN_WIDTH), const),
                  pl.BlockSpec((1, ATTN_WIDTH), const),
                  pl.BlockSpec((1, ATTN_WIDTH), const),
                  pl.BlockSpec((1, MEM_WIDTH), const),
                  pl.BlockSpec((ATTN_WIDTH, ATTN_WIDTH), const),
                  pl.BlockSpec((MEM_WIDTH, MEM_WIDTH), const)],
        out_specs=out_specs,
        out_shape=out_shape,
        scratch_shapes=[pltpu.VMEM((N_PAIRS, tm, LANES), F32)] * 2,
        compiler_params=_cparams(("parallel",)),
        name="proj",
    )(x2d, attn_norm, w_in, gq, gk, gm, _group_ones(ATTN_WIDTH), _group_ones(MEM_WIDTH))


def _attn_kernel(q1, k1c, k1p, v1c, v1p, q4, k4c, k4p, v4c, v4p, q16, k16c, k16p, v16c, v16p,
                 bias_ref, o_ref, obuf, lbuf):
    c = pl.program_id(2)
    lane = lax.broadcasted_iota(I32, (BLOCK, LANES), 1)
    is_a = lane < HEAD_DIM
    lane_row = lax.broadcasted_iota(I32, (1, LANES), 1)
    mask_a = jnp.where(lane_row < HEAD_DIM, 1.0, 0.0).astype(BF16)
    mask_b = jnp.where(lane_row < HEAD_DIM, 0.0, 1.0).astype(BF16)
    bias_full = bias_ref[0]
    bias_first = jnp.where(c > 0, bias_full, bias_ref[1])

    def tile(q_t, kp_t, kc_t, vp_t, vc_t, bias):
        lhs = jnp.concatenate([q_t * mask_a, q_t * mask_b], axis=0)
        keys = jnp.concatenate([kp_t, kc_t], axis=0)
        s = lax.dot_general(lhs, keys, (((1,), (1,)), ((), ())), preferred_element_type=F32) + bias
        m = jnp.max(s, axis=-1, keepdims=True)
        p = jnp.exp(s - m)
        l = jnp.sum(p, axis=-1, keepdims=True)
        vals = jnp.concatenate([vp_t, vc_t], axis=0)
        pv = jnp.dot(p.astype(BF16), vals, preferred_element_type=F32) * (1.0 / l)
        lse = m + jnp.log(l)
        return jnp.where(is_a, pv[:BLOCK], pv[BLOCK:]), jnp.where(is_a, lse[:BLOCK], lse[BLOCK:])

    for jb in range(SUPER // BLOCK):
        cur = pl.ds(jb * BLOCK, BLOCK)
        if jb == 0:
            kp_t, vp_t, bias = k1p[...], v1p[...], bias_first
        else:
            prev = pl.ds((jb - 1) * BLOCK, BLOCK)
            kp_t, vp_t, bias = k1c[prev, :], v1c[prev, :], bias_full
        o_t, lse_t = tile(q1[cur, :], kp_t, k1c[cur, :], vp_t, v1c[cur, :], bias)
        obuf[0, cur, :] = o_t
        lbuf[0, cur, :] = lse_t
    for pat, (d, q, kc, kp, vc, vp) in enumerate(((4, q4, k4c, k4p, v4c, v4p), (16, q16, k16c, k16p, v16c, v16p)), 1):
        nblk = SUPER // (BLOCK * d)
        for r in range(d):
            cols = pl.ds(r * LANES, LANES)
            for jb in range(nblk):
                cur = pl.ds(jb * BLOCK, BLOCK)
                if jb == 0:
                    kp_t, vp_t, bias = kp[:, cols], vp[:, cols], bias_first
                else:
                    prev = pl.ds((jb - 1) * BLOCK, BLOCK)
                    kp_t, vp_t, bias = kc[prev, cols], vc[prev, cols], bias_full
                o_t, lse_t = tile(q[cur, cols], kp_t, kc[cur, cols], vp_t, vc[cur, cols], bias)
                rows = pl.ds(jb * BLOCK * d + r, BLOCK, stride=d)
                obuf[pat, rows, :] = o_t
                lbuf[pat, rows, :] = lse_t
    for jb in range(SUPER // BLOCK):
        cur = pl.ds(jb * BLOCK, BLOCK)
        l0, l1, l2 = lbuf[0, cur, :], lbuf[1, cur, :], lbuf[2, cur, :]
        top = jnp.maximum(jnp.maximum(l0, l1), l2)
        w0, w1, w2 = jnp.exp(l0 - top), jnp.exp(l1 - top), jnp.exp(l2 - top)
        mixed = (w0 * obuf[0, cur, :] + w1 * obuf[1, cur, :] + w2 * obuf[2, cur, :]) * (1.0 / (w0 + w1 + w2))
        o_ref[cur, :] = mixed.astype(BF16)


def _band_bias():
    qi = jnp.arange(BLOCK)[:, None]
    kj = jnp.arange(2 * BLOCK)[None, :]
    dist = qi + BLOCK - kj
    in_band = (dist >= 0) & (dist <= BLOCK)
    full = jnp.where(in_band, 0.0, NEG_INF).astype(F32)
    first = jnp.where(in_band & (kj >= BLOCK), 0.0, NEG_INF).astype(F32)
    return jnp.stack([jnp.tile(full, (2, 1)), jnp.tile(first, (2, 1))])


def _attn(q1, k1, v1, q4, k4, v4, q16, k16, v16, B, S):
    nsup = S // SUPER

    def specs(d):
        rows = SUPER // d
        per = rows // BLOCK
        cur = pl.BlockSpec((None, None, rows, d * LANES), lambda b, hp, c: (b, hp, c, 0))
        prev = pl.BlockSpec((None, None, BLOCK, d * LANES),
                            lambda b, hp, c: (b, hp, jnp.maximum(per * c - 1, 0), 0))
        return [cur, cur, prev, cur, prev]

    return pl.pallas_call(
        _attn_kernel,
        grid=(B, N_PAIRS, nsup),
        in_specs=specs(1) + specs(4) + specs(16) + [
            pl.BlockSpec((2, 2 * BLOCK, 2 * BLOCK), lambda b, hp, c: (0, 0, 0))],
        out_specs=pl.BlockSpec((None, SUPER, LANES), lambda b, hp, c: (b, c, hp)),
        out_shape=jax.ShapeDtypeStruct((B, S, ATTN_WIDTH), BF16),
        scratch_shapes=[pltpu.VMEM((3, SUPER, LANES), F32)] * 2,
        compiler_params=_cparams(("parallel", "parallel", "parallel")),
        name="dilated_attn",
    )(q1, k1, k1, v1, v1, q4, k4, k4, v4, v4, q16, k16, k16, v16, v16, _band_bias())


def _memkv_kernel(mem_ref, mn_ref, wkv_ref, gk_ref, o256_ref, km_ref, vm_ref):
    m = mem_ref[...]
    ms = jnp.mean(m * m, axis=-1, keepdims=True)
    mn = (m * lax.rsqrt(ms + EPS) * mn_ref[...]).astype(BF16)
    kv = jnp.dot(mn, wkv_ref[...], preferred_element_type=F32)
    km_ref[...] = _head_norm(kv[:, :MEM_WIDTH], o256_ref, gk_ref).astype(BF16)
    vm_ref[...] = kv[:, MEM_WIDTH:].astype(BF16)


def _memkv(mem, mem_norm, w_mem_kv, gmk, B):
    const = lambda b: (0, 0)
    return pl.pallas_call(
        _memkv_kernel,
        grid=(B,),
        in_specs=[pl.BlockSpec((None, N_MEM, D_MODEL), lambda b: (b, 0, 0)),
                  pl.BlockSpec((1, D_MODEL), const),
                  pl.BlockSpec((D_MODEL, 2 * MEM_WIDTH), const),
                  pl.BlockSpec((1, MEM_WIDTH), const),
                  pl.BlockSpec((MEM_WIDTH, MEM_WIDTH), const)],
        out_specs=[pl.BlockSpec((None, N_MEM, MEM_WIDTH), lambda b: (b, 0, 0))] * 2,
        out_shape=[jax.ShapeDtypeStruct((B, N_MEM, MEM_WIDTH), BF16)] * 2,
        compiler_params=_cparams(("parallel",)),
        name="memkv",
    )(mem, mem_norm, w_mem_kv, gmk, _group_ones(MEM_WIDTH))


def _mix_kernel(tiles_per_seq, ya_ref, u_ref, uh_ref, qm_ref, km_ref, vm_ref, x_ref, pp_ref, ps_ref,
                wo_ref, fn_ref, wr_ref, br_ref, ltri_ref,
                x2_ref, h2p_ref, rinfo_ref, rinfot_ref, cnt_ref, base_ref):
    i = pl.program_id(0)
    tm = x_ref.shape[0]
    seq_tile = i % tiles_per_seq

    @pl.when(i == 0)
    def _():
        base_ref[...] = jnp.zeros_like(base_ref)

    u = u_ref[...]
    halo = jnp.where(seq_tile == 0, 0.0, uh_ref[...])
    uu = jnp.concatenate([halo, u], axis=0)
    a1 = uu[1:] + uu[:-1]
    a2 = a1[2:] + a1[:-2]
    a3 = a2[4:] + a2[:-4]
    a4 = a3[8:] + a3[:-8]
    lane_p = lax.broadcasted_iota(I32, (tm, POOL_WIDTH), 1)
    g0, g1, g2 = lane_p < 64, lane_p < 128, lane_p < 192
    wsum = jnp.where(g0, a1[15:], jnp.where(g1, a2[13:], jnp.where(g2, a3[9:], a4[1:])))
    wlen = jnp.where(g0, 2.0, jnp.where(g1, 4.0, jnp.where(g2, 8.0, 16.0)))
    tpos = seq_tile * tm + lax.broadcasted_iota(I32, (tm, POOL_WIDTH), 0) + 1
    cnt = jnp.minimum(tpos.astype(F32), wlen)
    pooled = wsum / cnt - u
    y_pool = jnp.dot(pooled.astype(BF16), pp_ref[...], preferred_element_type=F32) * ps_ref[...]

    lane = lax.broadcasted_iota(I32, (tm, LANES), 1)
    is_a = lane < HEAD_DIM
    lane_row = lax.broadcasted_iota(I32, (1, LANES), 1)
    mask_a = jnp.where(lane_row < HEAD_DIM, 1.0, 0.0).astype(BF16)
    mask_b = jnp.where(lane_row < HEAD_DIM, 0.0, 1.0).astype(BF16)
    y_mem = []
    for pr in range(MEM_WIDTH // LANES):
        cols = slice(pr * LANES, (pr + 1) * LANES)
        qp, kp, vp = qm_ref[:, cols], km_ref[:, cols], vm_ref[:, cols]
        outs = []
        for msk in (mask_a, mask_b):
            s = lax.dot_general(qp * msk, kp, (((1,), (1,)), ((), ())), preferred_element_type=F32)
            m = jnp.max(s, axis=-1, keepdims=True)
            p = jnp.exp(s - m)
            p = p / jnp.sum(p, axis=-1, keepdims=True)
            outs.append(jnp.dot(p.astype(BF16), vp, preferred_element_type=F32))
        y_mem.append(jnp.where(is_a, outs[0], outs[1]))

    rest = jnp.concatenate([y_pool] + y_mem, axis=1).astype(BF16)
    proj = jnp.dot(ya_ref[...], wo_ref[:ATTN_WIDTH, :], preferred_element_type=F32)
    proj += jnp.dot(rest, wo_ref[ATTN_WIDTH:, :], preferred_element_type=F32)
    x2 = x_ref[...] + proj
    x2_ref[...] = x2

    ms = jnp.mean(x2 * x2, axis=-1, keepdims=True)
    h2 = (x2 * lax.rsqrt(ms + EPS) * fn_ref[...]).astype(BF16)
    bits = lax.bitcast_convert_type(h2.astype(F32), U32)
    h2p_ref[...] = bits[:, :PACKED] | (bits[:, PACKED:] >> 16)

    logits = jnp.dot(h2, wr_ref[...], preferred_element_type=F32) + br_ref[...]
    ninf = -jnp.inf
    lane_f = lane.astype(F32)
    grp_f = ((lane - EXPERT_LANE0) >> 3).astype(F32)
    is_g = lane < N_GROUPS
    lg = jnp.where(is_g, logits, ninf)
    mg = jnp.max(lg, axis=-1, keepdims=True)
    g_sel = jnp.min(jnp.where(lg == mg, lane_f, float(LANES)), axis=-1, keepdims=True)
    w_g = 1.0 / jnp.sum(jnp.where(is_g, jnp.exp(logits - mg), 0.0), axis=-1, keepdims=True)
    in_grp = (lane >= EXPERT_LANE0) & (grp_f == g_sel)
    le = jnp.where(in_grp, logits, ninf)
    v1 = jnp.max(le, axis=-1, keepdims=True)
    i1 = jnp.min(jnp.where(le == v1, lane_f, float(2 * LANES)), axis=-1, keepdims=True)
    le2 = jnp.where(lane_f == i1, ninf, le)
    v2 = jnp.max(le2, axis=-1, keepdims=True)
    i2 = jnp.min(jnp.where(le2 == v2, lane_f, float(2 * LANES)), axis=-1, keepdims=True)
    e21 = jnp.exp(v2 - v1)
    gate1 = w_g / (1.0 + e21)
    gate2 = w_g * e21 / (1.0 + e21)

    hot1 = lane_f == i1
    hot2 = lane_f == i2
    onehot = jnp.where(hot1 | hot2, 1.0, 0.0)
    prefix = jnp.dot(ltri_ref[...], onehot.astype(BF16), preferred_element_type=F32)
    tot = prefix + base_ref[...]
    rank1 = jnp.sum(jnp.where(hot1, tot, 0.0), axis=-1, keepdims=True)
    rank2 = jnp.sum(jnp.where(hot2, tot, 0.0), axis=-1, keepdims=True)
    base_new = base_ref[...] + jnp.sum(onehot, axis=0, keepdims=True)
    base_ref[...] = base_new
    cnt_ref[...] = base_new

    e1 = i1 - float(EXPERT_LANE0)
    e2 = i2 - float(EXPERT_LANE0)
    rinfo = jnp.where(lane == 0, e1, jnp.where(lane == 1, e2, jnp.where(lane == 2, rank1, jnp.where(
        lane == 3, rank2, jnp.where(lane == 4, gate1, jnp.where(lane == 5, gate2, 0.0))))))
    rinfo_ref[...] = rinfo
    rinfot_ref[...] = jnp.transpose(rinfo)[:8, :]


def _mix(ya, u, qm, km, vm, x2d, pool_bd, pool_scale, w_out, ffn_norm, w_r, b_r, B, S):
    T = B * S
    tm = TM_MIX
    tps = S // tm
    hb = tm // HALO
    const = lambda i: (0, 0)
    ltri = (jnp.arange(tm)[:, None] > jnp.arange(tm)[None, :]).astype(BF16)
    return pl.pallas_call(
        functools.partial(_mix_kernel, tps),
        grid=(T // tm,),
        in_specs=[pl.BlockSpec((tm, ATTN_WIDTH), lambda i: (i, 0)),
                  pl.BlockSpec((tm, POOL_WIDTH), lambda i: (i, 0)),
                  pl.BlockSpec((HALO, POOL_WIDTH), lambda i: (jnp.maximum(i * hb - 1, 0), 0)),
                  pl.BlockSpec((tm, MEM_WIDTH), lambda i: (i, 0)),
                  pl.BlockSpec((None, N_MEM, MEM_WIDTH), lambda i: (i // tps, 0, 0)),
                  pl.BlockSpec((None, N_MEM, MEM_WIDTH), lambda i: (i // tps, 0, 0)),
                  pl.BlockSpec((tm, D_MODEL), lambda i: (i, 0)),
                  pl.BlockSpec((POOL_WIDTH, POOL_WIDTH), const),
                  pl.BlockSpec((1, POOL_WIDTH), const),
                  pl.BlockSpec((D_MODEL, D_MODEL), const),
                  pl.BlockSpec((1, D_MODEL), const),
                  pl.BlockSpec((D_MODEL, LANES), const),
                  pl.BlockSpec((1, LANES), const),
                  pl.BlockSpec((tm, tm), const)],
        out_specs=[pl.BlockSpec((tm, D_MODEL), lambda i: (i, 0)),
                   pl.BlockSpec((tm, PACKED), lambda i: (i, 0)),
                   pl.BlockSpec((tm, LANES), lambda i: (i, 0)),
                   pl.BlockSpec((8, tm), lambda i: (0, i)),
                   pl.BlockSpec((1, LANES), const)],
        out_shape=[jax.ShapeDtypeStruct((T, D_MODEL), F32),
                   jax.ShapeDtypeStruct((T, PACKED), U32),
                   jax.ShapeDtypeStruct((T, LANES), F32),
                   jax.ShapeDtypeStruct((8, T), F32),
                   jax.ShapeDtypeStruct((1, LANES), F32)],
        scratch_shapes=[pltpu.VMEM((1, LANES), F32)],
        compiler_params=_cparams(("arbitrary",)),
        name="mix_router",
    )(ya, u, u, qm, km, vm, x2d, pool_bd, pool_scale, w_out, ffn_norm, w_r, b_r, ltri)


def _dest_kernel(rt_ref, ps_ref, d_ref):
    tn = rt_ref.shape[1]
    sub = lax.broadcasted_iota(I32, (N_EXPERTS, tn), 0).astype(F32)
    ps = ps_ref[...]
    rows = []
    for k in range(2):
        e = rt_ref[k:k + 1, :]
        start = jnp.sum(jnp.where(sub == e, ps, 0.0), axis=0, keepdims=True)
        rows.append(start + rt_ref[2 + k:3 + k, :])
    word = rows[0].astype(I32) | (rows[1].astype(I32) << 16)
    d_ref[...] = jnp.concatenate([word, jnp.zeros((7, tn), I32)], axis=0)


def _dest(rinfot, pstart_col, T):
    tn = 2048
    return pl.pallas_call(
        _dest_kernel,
        grid=(T // tn,),
        in_specs=[pl.BlockSpec((8, tn), lambda i: (0, i)),
                  pl.BlockSpec((N_EXPERTS, 1), lambda i: (0, 0))],
        out_specs=pl.BlockSpec((8, tn), lambda i: (0, i)),
        out_shape=jax.ShapeDtypeStruct((8, T), I32),
        compiler_params=_cparams(("parallel",)),
        name="dest_rows",
    )(rinfot, pstart_col)


def _rowmap_kernel(d_ref, fill_hbm, inv_ref, sem):
    T = d_ref.shape[0]
    cp = pltpu.make_async_copy(fill_hbm, inv_ref, sem)
    cp.start()
    cp.wait()
    unroll = 8

    def body(g, carry):
        t0 = g * unroll
        words = [d_ref[t0 + u] for u in range(unroll)]
        for u, w in enumerate(words):
            inv_ref[w & 0xFFFF] = t0 + u
            inv_ref[lax.shift_right_logical(w, 16)] = T + t0 + u
        return carry

    lax.fori_loop(0, T // unroll, body, 0)


def _rowmap(dword, R):
    assert R <= 1 << 16
    T = dword.shape[0]
    fill = 2 * T + jnp.arange(R, dtype=I32) % (2 * TR_MOE)
    return pl.pallas_call(
        _rowmap_kernel,
        grid_spec=pltpu.PrefetchScalarGridSpec(
            num_scalar_prefetch=1,
            grid=(1,),
            in_specs=[pl.BlockSpec(memory_space=pl.ANY)],
            out_specs=pl.BlockSpec(memory_space=pltpu.SMEM),
            scratch_shapes=[pltpu.SemaphoreType.DMA(())]),
        out_shape=jax.ShapeDtypeStruct((R,), I32),
        compiler_params=_cparams(("arbitrary",)),
        name="rowmap",
    )(dword, fill)


def _moe_kernel(be_ref, nu_ref, first_ref, nxt_ref, ws_ref, inv_ref,
                h_hbm, w1_hbm, w3_hbm, w2_hbm, o_hbm,
                tab, xbuf0, xbuf1, a0, b0, a1, b1, ybuf, w1s, w3s, w2s, w1b, w3b, w2b, tsem, ssem, wsem):
    j = pl.program_id(0)
    nu = nu_ref[0]
    T = h_hbm.shape[0]
    tr = TR_MOE
    n_tiles = inv_ref.shape[0] // tr
    xbuf = (xbuf0, xbuf1)
    abuf = ((a0, b0), (a1, b1))

    def tile_rows(ref, row, n=1):
        return ref.at[pl.ds(pl.multiple_of(row * ROW_SUB, ROW_SUB), n * ROW_SUB)]

    def gather_tile(tile, slot):
        for i in range(tr):
            v = inv_ref[tile * tr + i]
            if T & (T - 1) == 0:
                tok = v & (T - 1)
            else:
                tok = jnp.where(v >= 2 * T, 0, jnp.where(v >= T, v - T, v))
            xbuf[slot][pl.ds(i, 1), :] = tab[pl.ds(tok, 1), :]

    def scatter_row(tile, slot, i):
        v = inv_ref[tile * tr + i]
        return pltpu.make_async_copy(tile_rows(ybuf, slot * tr + i), tile_rows(o_hbm, v), ssem.at[slot])

    def scatter_wait(slot):
        pltpu.make_async_copy(tile_rows(ybuf, slot * tr, tr), tile_rows(o_hbm, 0, tr), ssem.at[slot]).wait()

    def weight_copies(e, slot):
        return [pltpu.make_async_copy(src.at[e], dst.at[slot], wsem.at[slot])
                for src, dst in ((w1_hbm, w1s), (w3_hbm, w3s), (w2_hbm, w2s))]

    def up_project(slot):
        w = xbuf[slot][...]
        lo = lax.bitcast_convert_type(w & jnp.uint32(0xFFFF0000), F32).astype(BF16)
        hi = lax.bitcast_convert_type(w << 16, F32).astype(BF16)
        for dst, wb in zip(abuf[slot], (w1b, w3b)):
            dst[...] = (jnp.dot(lo, wb[:PACKED, :], preferred_element_type=F32)
                        + jnp.dot(hi, wb[PACKED:, :], preferred_element_type=F32))

    @pl.when(j < nu)
    def _():
        s = j & 1
        jn = jnp.minimum(j + 1, n_tiles - 1)

        @pl.when(j == 0)
        def _():
            ybuf[...] = jnp.zeros_like(ybuf)
            scratch_rows = pltpu.make_async_copy(ybuf, tile_rows(o_hbm, 2 * T, 2 * tr), tsem)
            scratch_rows.start()
            scratch_rows.wait()
            table = pltpu.make_async_copy(h_hbm, tab, tsem)
            table.start()
            for cp in weight_copies(be_ref[0], 0):
                cp.start()
            table.wait()
            gather_tile(0, 0)
            gather_tile(jnp.minimum(1, n_tiles - 1), 1)
            for cp in weight_copies(be_ref[0], 0):
                cp.wait()

            @pl.when(nxt_ref[0] >= 0)
            def _():
                for cp in weight_copies(nxt_ref[0], 1):
                    cp.start()

            w1b[...] = w1s[0].astype(BF16)
            w3b[...] = w3s[0].astype(BF16)
            w2b[...] = w2s[0].astype(BF16)
            up_project(0)

        @pl.when((j >= 1) & (first_ref[j] == 1))
        def _():
            w2b[...] = w2s[ws_ref[j]].astype(BF16)

        @pl.when((j + 1 < nu) & (first_ref[jn] == 1))
        def _():
            wslot = ws_ref[jn]
            for cp in weight_copies(be_ref[jn], wslot):
                cp.wait()
            e_next = nxt_ref[jn]

            @pl.when(e_next >= 0)
            def _():
                for cp in weight_copies(e_next, 1 - wslot):
                    cp.start()

            w1b[...] = w1s[wslot].astype(BF16)
            w3b[...] = w3s[wslot].astype(BF16)

        @pl.when(j >= 2)
        def _():
            scatter_wait(s)

        for slot in range(2):
            @pl.when(s == slot)
            def _():
                gather_tile(jnp.minimum(j + 2, n_tiles - 1), slot)
                a = abuf[slot][0][...]
                b = abuf[slot][1][...]
                up_project(1 - slot)
                hmid = (a / (1.0 + jnp.exp(-a)) * b).astype(BF16)
                y = jnp.dot(hmid, w2b[...], preferred_element_type=F32)
                for c in range(ROW_SUB):
                    ybuf[pl.ds(slot * (tr * ROW_SUB) + c, tr, stride=ROW_SUB), :] = y[:, c * LANES:(c + 1) * LANES]
                for i in range(tr):
                    scatter_row(j, slot, i).start()

        @pl.when(j == nu - 1)
        def _():
            scatter_wait(s)

            @pl.when(j >= 1)
            def _():
                scatter_wait(1 - s)


def _moe(blk_exp, n_used, first, nxt, wslot, inv, h2p, w1, w3, w2):
    T = h2p.shape[0]
    tr = TR_MOE
    n_tiles = blk_exp.shape[0]
    any_spec = pl.BlockSpec(memory_space=pl.ANY)
    return pl.pallas_call(
        _moe_kernel,
        grid_spec=pltpu.PrefetchScalarGridSpec(
            num_scalar_prefetch=6,
            grid=(n_tiles,),
            in_specs=[any_spec] * 4,
            out_specs=any_spec,
            scratch_shapes=[pltpu.VMEM((T, PACKED), U32),
                            pltpu.VMEM((tr, PACKED), U32),
                            pltpu.VMEM((tr, PACKED), U32),
                            pltpu.VMEM((tr, EXPERT_HIDDEN), F32),
                            pltpu.VMEM((tr, EXPERT_HIDDEN), F32),
                            pltpu.VMEM((tr, EXPERT_HIDDEN), F32),
                            pltpu.VMEM((tr, EXPERT_HIDDEN), F32),
                            pltpu.VMEM((2 * tr * ROW_SUB, LANES), F32),
                            pltpu.VMEM((2, D_MODEL, EXPERT_HIDDEN), F32),
                            pltpu.VMEM((2, D_MODEL, EXPERT_HIDDEN), F32),
                            pltpu.VMEM((2, EXPERT_HIDDEN, D_MODEL), F32),
                            pltpu.VMEM((D_MODEL, EXPERT_HIDDEN), BF16),
                            pltpu.VMEM((D_MODEL, EXPERT_HIDDEN), BF16),
                            pltpu.VMEM((EXPERT_HIDDEN, D_MODEL), BF16),
                            pltpu.SemaphoreType.DMA(()),
                            pltpu.SemaphoreType.DMA((2,)),
                            pltpu.SemaphoreType.DMA((2,))]),
        out_shape=jax.ShapeDtypeStruct(((2 * T + 2 * tr) * ROW_SUB, LANES), F32),
        compiler_params=pltpu.CompilerParams(dimension_semantics=("arbitrary",), vmem_limit_bytes=MOE_VMEM_LIMIT),
        name="moe_experts",
    )(blk_exp, n_used, first, nxt, wslot, inv, h2p, w1, w3, w2)


def _combine_kernel(x2_ref, rinfo_ref, y1_ref, y2_ref, o_ref):
    g1 = rinfo_ref[:, 4:5]
    g2 = rinfo_ref[:, 5:6]
    tc = x2_ref.shape[0]
    for c in range(ROW_SUB):
        cols = slice(c * LANES, (c + 1) * LANES)
        rows = pl.ds(c, tc, stride=ROW_SUB)
        o_ref[:, cols] = x2_ref[:, cols] + (g1 * y1_ref[rows, :] + g2 * y2_ref[rows, :])


def _combine(x2, rinfo, y2slot):
    T = x2.shape[0]
    tc = TC_COMBINE
    return pl.pallas_call(
        _combine_kernel,
        grid=(T // tc,),
        in_specs=[pl.BlockSpec((tc, D_MODEL), lambda i: (i, 0)),
                  pl.BlockSpec((tc, LANES), lambda i: (i, 0)),
                  pl.BlockSpec((tc * ROW_SUB, LANES), lambda i: (i, 0)),
                  pl.BlockSpec((tc * ROW_SUB, LANES), lambda i: (T // tc + i, 0))],
        out_specs=pl.BlockSpec((tc, D_MODEL), lambda i: (i, 0)),
        out_shape=jax.ShapeDtypeStruct((T, D_MODEL), F32),
        compiler_params=_cparams(("parallel",)),
        name="combine",
    )(x2, rinfo, y2slot, y2slot)


def _layer(x, mem, attn_norm, w_in, q_norm, k_norm, pool_proj, pool_scale, mem_norm, w_mem_kv,
           mq_norm, mk_norm, w_out, ffn_norm, w_group, b_group, w_router, b_router, w1, w3, w2):
    B, S, D = x.shape
    T = B * S
    assert D == D_MODEL and S % SUPER == 0 and T % TM_PROJ == 0
    x2d = x.reshape(T, D)
    row = lambda v: v.reshape(1, -1).astype(F32)
    scale = HEAD_DIM ** -0.5
    gq = row(jnp.tile(q_norm, ATTN_WIDTH // HEAD_DIM) * scale)
    gk = row(jnp.tile(k_norm, ATTN_WIDTH // HEAD_DIM))
    gmq = row(jnp.tile(mq_norm, MEM_WIDTH // HEAD_DIM) * scale)
    gmk = row(jnp.tile(mk_norm, MEM_WIDTH // HEAD_DIM))

    (q1, k1, v1, q4, k4, v4, q16, k16, v16, u, qm) = _proj(
        x2d, row(attn_norm), w_in.astype(BF16), gq, gk, gmq, B, S)
    ya = _attn(q1, k1, v1, q4, k4, v4, q16, k16, v16, B, S).reshape(T, ATTN_WIDTH)
    km, vm = _memkv(mem, row(mem_norm), w_mem_kv.astype(BF16), gmk, B)

    pool_bd = jax.scipy.linalg.block_diag(*[pool_proj[g] for g in range(pool_proj.shape[0])]).astype(BF16)
    w_r = jnp.zeros((D, LANES), F32)
    w_r = w_r.at[:, :N_GROUPS].set(w_group)
    w_r = w_r.at[:, EXPERT_LANE0:].set(jnp.transpose(w_router, (1, 0, 2)).reshape(D, N_EXPERTS))
    b_r = jnp.zeros((1, LANES), F32).at[0, :N_GROUPS].set(b_group).at[0, EXPERT_LANE0:].set(b_router.reshape(-1))
    x2, h2p, rinfo, rinfot, counts = _mix(ya, u, qm, km, vm, x2d, pool_bd, row(pool_scale),
                                          w_out.astype(BF16), row(ffn_norm), w_r.astype(BF16), b_r, B, S)

    R, pstart, sched = _tile_schedule(counts, T)
    dest = _dest(rinfot, pstart.astype(F32).reshape(N_EXPERTS, 1), T)
    inv = _rowmap(dest[0], R)
    y2slot = _moe(*sched, inv, h2p, w1, w3, w2)
    out = _combine(x2, rinfo, y2slot)
    return out.reshape(B, S, D)


def _tile_schedule(counts, T):
    tr = TR_MOE
    R = 2 * T + N_EXPERTS * tr
    n_tiles = R // tr
    cnt = counts[0, EXPERT_LANE0:].astype(I32)
    padded = ((cnt + tr - 1) // tr) * tr
    pend = jnp.cumsum(padded)
    pstart = pend - padded
    n_used = (pend[-1] // tr).astype(I32).reshape(1)
    tiles = jnp.arange(n_tiles, dtype=I32)
    tile_row = jnp.minimum(tiles, n_used[0] - 1) * tr
    blk_exp = jnp.minimum(jnp.sum(tile_row[:, None] >= pend[None, :], axis=1), N_EXPERTS - 1).astype(I32)
    in_use = tiles < n_used[0]
    first = in_use & ((tiles == 0) | (blk_exp != jnp.roll(blk_exp, 1)))
    run_idx = jnp.cumsum(first.astype(I32)) - 1
    used = cnt > 0
    exp_of_run = jnp.argsort(jnp.logical_not(used), stable=True).astype(I32)
    nxt = jnp.where(run_idx + 1 < jnp.sum(used), exp_of_run[jnp.clip(run_idx + 1, 0, N_EXPERTS - 1)], -1)
    sched = (blk_exp, n_used, first.astype(I32), nxt.astype(I32), (run_idx & 1).astype(I32))
    return R, pstart, sched


def kernel(x, mem, attn_norm, w_in, q_norm, k_norm, pool_proj, pool_scale, mem_norm, w_mem_kv, mq_norm, mk_norm,
           w_out, ffn_norm, w_group, b_group, w_router, b_router, w1, w3, w2):
    for l in range(attn_norm.shape[0]):
        x = _layer(x, mem, attn_norm[l], w_in[l], q_norm[l], k_norm[l], pool_proj[l], pool_scale[l],
                   mem_norm[l], w_mem_kv[l], mq_norm[l], mk_norm[l], w_out[l], ffn_norm[l],
                   w_group[l], b_group[l], w_router[l], b_router[l], w1[l], w3[l], w2[l])
    return x
```

```python
import functools

import jax
import jax.numpy as jnp
from jax import lax
from jax.experimental import pallas as pl
from jax.experimental.pallas import tpu as pltpu

F32 = jnp.float32
BF16 = jnp.bfloat16
I32 = jnp.int32
U32 = jnp.uint32

D_MODEL = 1024
HEAD_DIM = 64
ATTN_WIDTH = 512
POOL_WIDTH = 256
MEM_WIDTH = 256
N_MEM = 256
IN_WIDTH = 3 * ATTN_WIDTH + POOL_WIDTH + MEM_WIDTH
N_GROUPS = 8
EXPERTS_PER_GROUP = 8
N_EXPERTS = 64
EXPERT_HIDDEN = 512
EPS = 1e-6
NEG_INF = -1e30

LANES = 128
N_PAIRS = ATTN_WIDTH // LANES
BLOCK = 128
SUPER = 16 * BLOCK
HALO = 16
ROW_SUB = D_MODEL // LANES
PACKED = D_MODEL // 2

TM_PROJ = 1024
TM_MIX = 1024
TR_MOE = 256
TC_COMBINE = 1024
EXPERT_LANE0 = 64

VMEM_LIMIT = 48 * 1024 * 1024
MOE_VMEM_LIMIT = 56 * 1024 * 1024


def _cparams(sem):
    return pltpu.CompilerParams(dimension_semantics=sem, vmem_limit_bytes=VMEM_LIMIT)


def _group_ones(n):
    g = jnp.arange(n) // HEAD_DIM
    return (g[:, None] == g[None, :]).astype(BF16)


def _head_norm(z, ones_ref, gain_ref):
    ss = jnp.dot((z * z).astype(BF16), ones_ref[...], preferred_element_type=F32)
    return z * lax.rsqrt(ss * (1.0 / HEAD_DIM) + EPS) * gain_ref[...]


def _proj_kernel(x_ref, an_ref, win_ref, gq_ref, gk_ref, gm_ref, o512_ref, o256_ref,
                 q1_ref, k1_ref, v1_ref, q4_ref, k4_ref, v4_ref, q16_ref, k16_ref, v16_ref,
                 u_ref, qm_ref, zs_ref, z4_ref):
    tm = x_ref.shape[0]
    x = x_ref[...]
    ms = jnp.mean(x * x, axis=-1, keepdims=True)
    h = (x * lax.rsqrt(ms + EPS) * an_ref[...]).astype(BF16)
    a0, a1, a2, a3 = ATTN_WIDTH, 2 * ATTN_WIDTH, 3 * ATTN_WIDTH, 3 * ATTN_WIDTH + POOL_WIDTH

    def cols(lo, hi):
        return jnp.dot(h, win_ref[:, lo:hi], preferred_element_type=F32)

    groups = ((lambda: _head_norm(cols(0, a0), o512_ref, gq_ref), q1_ref, q4_ref, q16_ref),
              (lambda: _head_norm(cols(a0, a1), o512_ref, gk_ref), k1_ref, k4_ref, k16_ref),
              (lambda: cols(a1, a2), v1_ref, v4_ref, v16_ref))
    for make, o1, o4, o16 in groups:
        val = make()
        for hp in range(N_PAIRS):
            pair = val[:, hp * LANES:(hp + 1) * LANES]
            zs_ref[hp] = pair
            o1[hp] = pair.astype(BF16)
        q4n = tm // 4
        for hp in range(N_PAIRS):
            for r4 in range(4):
                rows = zs_ref[hp, pl.ds(r4, q4n, stride=4), :]
                z4_ref[hp, pl.ds(r4 * q4n, q4n), :] = rows
                o4[hp, :, r4 * LANES:(r4 + 1) * LANES] = rows.astype(BF16)
            for r16 in range(16):
                r4, a = r16 % 4, r16 // 4
                rows = z4_ref[hp, pl.ds(r4 * q4n + a, tm // 16, stride=4), :]
                o16[hp, :, r16 * LANES:(r16 + 1) * LANES] = rows.astype(BF16)
    u_ref[...] = cols(a2, a3)
    qm_ref[...] = _head_norm(cols(a3, IN_WIDTH), o256_ref, gm_ref).astype(BF16)


def _proj(x2d, attn_norm, w_in, gq, gk, gm, B, S):
    T = B * S
    tm = TM_PROJ
    nj = S // tm
    const = lambda i: (0, 0)

    def lay(d):
        return jax.ShapeDtypeStruct((B, N_PAIRS, S // d, d * LANES), BF16)

    def lay_spec(d):
        return pl.BlockSpec((None, N_PAIRS, tm // d, d * LANES), lambda i: (i // nj, 0, i % nj, 0))

    out_shape = [lay(1)] * 3 + [lay(4)] * 3 + [lay(16)] * 3 + [
        jax.ShapeDtypeStruct((T, POOL_WIDTH), F32), jax.ShapeDtypeStruct((T, MEM_WIDTH), BF16)]
    out_specs = [lay_spec(1)] * 3 + [lay_spec(4)] * 3 + [lay_spec(16)] * 3 + [
        pl.BlockSpec((tm, POOL_WIDTH), lambda i: (i, 0)), pl.BlockSpec((tm, MEM_WIDTH), lambda i: (i, 0))]
    return pl.pallas_call(
        _proj_kernel,
        grid=(T // tm,),
        in_specs=[pl.BlockSpec((tm, D_MODEL), lambda i: (i, 0)),
                  pl.BlockSpec((1, D_MODEL), const),
                  pl.BlockSpec((D_MODEL, IN_WIDTH), const),
                  pl.BlockSpec((1, ATTN_WIDTH), const),
                  pl.BlockSpec((1, ATTN_WIDTH), const),
                  pl.BlockSpec((1, MEM_WIDTH), const),
                  pl.BlockSpec((ATTN_WIDTH, ATTN_WIDTH), const),
                  pl.BlockSpec((MEM_WIDTH, MEM_WIDTH), const)],
        out_specs=out_specs,
        out_shape=out_shape,
        scratch_shapes=[pltpu.VMEM((N_PAIRS, tm, LANES), F32)] * 2,
        compiler_params=_cparams(("parallel",)),
        name="proj",
    )(x2d, attn_norm, w_in, gq, gk, gm, _group_ones(ATTN_WIDTH), _group_ones(MEM_WIDTH))


def _attn_kernel(q1, k1c, k1p, v1c, v1p, q4, k4c, k4p, v4c, v4p, q16, k16c, k16p, v16c, v16p,
                 bias_ref, o_ref, obuf, lbuf):
    c = pl.program_id(2)
    lane = lax.broadcasted_iota(I32, (BLOCK, LANES), 1)
    is_a = lane < HEAD_DIM
    lane_row = lax.broadcasted_iota(I32, (1, LANES), 1)
    mask_a = jnp.where(lane_row < HEAD_DIM, 1.0, 0.0).astype(BF16)
    mask_b = jnp.where(lane_row < HEAD_DIM, 0.0, 1.0).astype(BF16)
    bias_full = bias_ref[0]
    bias_first = jnp.where(c > 0, bias_full, bias_ref[1])

    def tile(q_t, kp_t, kc_t, vp_t, vc_t, bias):
        lhs = jnp.concatenate([q_t * mask_a, q_t * mask_b], axis=0)
        keys = jnp.concatenate([kp_t, kc_t], axis=0)
        s = lax.dot_general(lhs, keys, (((1,), (1,)), ((), ())), preferred_element_type=F32) + bias
        m = jnp.max(s, axis=-1, keepdims=True)
        p = jnp.exp(s - m)
        l = jnp.sum(p, axis=-1, keepdims=True)
        vals = jnp.concatenate([vp_t, vc_t], axis=0)
        pv = jnp.dot(p.astype(BF16), vals, preferred_element_type=F32) * (1.0 / l)
        lse = m + jnp.log(l)
        return jnp.where(is_a, pv[:BLOCK], pv[BLOCK:]), jnp.where(is_a, lse[:BLOCK], lse[BLOCK:])

    for pat, (d, q, kc, kp, vc, vp) in enumerate(((16, q16, k16c, k16p, v16c, v16p), (4, q4, k4c, k4p, v4c, v4p))):
        nblk = SUPER // (BLOCK * d)
        for r in range(d):
            cols = pl.ds(r * LANES, LANES)
            for jb in range(nblk):
                cur = pl.ds(jb * BLOCK, BLOCK)
                if jb == 0:
                    kp_t, vp_t, bias = kp[:, cols], vp[:, cols], bias_first
                else:
                    prev = pl.ds((jb - 1) * BLOCK, BLOCK)
                    kp_t, vp_t, bias = kc[prev, cols], vc[prev, cols], bias_full
                o_t, lse_t = tile(q[cur, cols], kp_t, kc[cur, cols], vp_t, vc[cur, cols], bias)
                rows = pl.ds(jb * BLOCK * d + r, BLOCK, stride=d)
                obuf[pat, rows, :] = o_t
                lbuf[pat, rows, :] = lse_t
    for jb in range(SUPER // BLOCK):
        cur = pl.ds(jb * BLOCK, BLOCK)
        if jb == 0:
            kp_t, vp_t, bias = k1p[...], v1p[...], bias_first
        else:
            prev = pl.ds((jb - 1) * BLOCK, BLOCK)
            kp_t, vp_t, bias = k1c[prev, :], v1c[prev, :], bias_full
        o0, l0 = tile(q1[cur, :], kp_t, k1c[cur, :], vp_t, v1c[cur, :], bias)
        l1, l2 = lbuf[0, cur, :], lbuf[1, cur, :]
        top = jnp.maximum(jnp.maximum(l0, l1), l2)
        w0, w1, w2 = jnp.exp(l0 - top), jnp.exp(l1 - top), jnp.exp(l2 - top)
        mixed = (w0 * o0 + w1 * obuf[0, cur, :] + w2 * obuf[1, cur, :]) * (1.0 / (w0 + w1 + w2))
        o_ref[cur, :] = mixed.astype(BF16)


def _band_bias():
    qi = jnp.arange(BLOCK)[:, None]
    kj = jnp.arange(2 * BLOCK)[None, :]
    dist = qi + BLOCK - kj
    in_band = (dist >= 0) & (dist <= BLOCK)
    full = jnp.where(in_band, 0.0, NEG_INF).astype(F32)
    first = jnp.where(in_band & (kj >= BLOCK), 0.0, NEG_INF).astype(F32)
    return jnp.stack([jnp.tile(full, (2, 1)), jnp.tile(first, (2, 1))])


def _attn(q1, k1, v1, q4, k4, v4, q16, k16, v16, B, S):
    nsup = S // SUPER

    def specs(d):
        rows = SUPER // d
        per = rows // BLOCK
        cur = pl.BlockSpec((None, None, rows, d * LANES), lambda b, hp, c: (b, hp, c, 0))
        prev = pl.BlockSpec((None, None, BLOCK, d * LANES),
                            lambda b, hp, c: (b, hp, jnp.maximum(per * c - 1, 0), 0))
        return [cur, cur, prev, cur, prev]

    return pl.pallas_call(
        _attn_kernel,
        grid=(B, N_PAIRS, nsup),
        in_specs=specs(1) + specs(4) + specs(16) + [
            pl.BlockSpec((2, 2 * BLOCK, 2 * BLOCK), lambda b, hp, c: (0, 0, 0))],
        out_specs=pl.BlockSpec((None, SUPER, LANES), lambda b, hp, c: (b, c, hp)),
        out_shape=jax.ShapeDtypeStruct((B, S, ATTN_WIDTH), BF16),
        scratch_shapes=[pltpu.VMEM((2, SUPER, LANES), F32)] * 2,
        compiler_params=_cparams(("parallel", "parallel", "parallel")),
        name="dilated_attn",
    )(q1, k1, k1, v1, v1, q4, k4, k4, v4, v4, q16, k16, k16, v16, v16, _band_bias())


def _memkv_kernel(mem_ref, mn_ref, wkv_ref, gk_ref, o256_ref, km_ref, vm_ref):
    m = mem_ref[...]
    ms = jnp.mean(m * m, axis=-1, keepdims=True)
    mn = (m * lax.rsqrt(ms + EPS) * mn_ref[...]).astype(BF16)
    kv = jnp.dot(mn, wkv_ref[...], preferred_element_type=F32)
    km_ref[...] = _head_norm(kv[:, :MEM_WIDTH], o256_ref, gk_ref).astype(BF16)
    vm_ref[...] = kv[:, MEM_WIDTH:].astype(BF16)


def _memkv(mem, mem_norm, w_mem_kv, gmk, B):
    const = lambda b: (0, 0)
    return pl.pallas_call(
        _memkv_kernel,
        grid=(B,),
        in_specs=[pl.BlockSpec((None, N_MEM, D_MODEL), lambda b: (b, 0, 0)),
                  pl.BlockSpec((1, D_MODEL), const),
                  pl.BlockSpec((D_MODEL, 2 * MEM_WIDTH), const),
                  pl.BlockSpec((1, MEM_WIDTH), const),
                  pl.BlockSpec((MEM_WIDTH, MEM_WIDTH), const)],
        out_specs=[pl.BlockSpec((None, N_MEM, MEM_WIDTH), lambda b: (b, 0, 0))] * 2,
        out_shape=[jax.ShapeDtypeStruct((B, N_MEM, MEM_WIDTH), BF16)] * 2,
        compiler_params=_cparams(("parallel",)),
        name="memkv",
    )(mem, mem_norm, w_mem_kv, gmk, _group_ones(MEM_WIDTH))


def _mix_kernel(tiles_per_seq, ya_ref, u_ref, uh_ref, qm_ref, km_ref, vm_ref, x_ref, pp_ref, ps_ref,
                wo_ref, fn_ref, wr_ref, br_ref, ltri_ref,
                x2_ref, h2p_ref, rinfo_ref, rinfot_ref, cnt_ref, base_ref):
    i = pl.program_id(0)
    tm = x_ref.shape[0]
    seq_tile = i % tiles_per_seq

    @pl.when(i == 0)
    def _():
        base_ref[...] = jnp.zeros_like(base_ref)

    u = u_ref[...]
    halo = jnp.where(seq_tile == 0, 0.0, uh_ref[...])
    uu = jnp.concatenate([halo, u], axis=0)
    a1 = uu[1:] + uu[:-1]
    a2 = a1[2:] + a1[:-2]
    a3 = a2[4:] + a2[:-4]
    a4 = a3[8:] + a3[:-8]
    lane_p = lax.broadcasted_iota(I32, (tm, POOL_WIDTH), 1)
    g0, g1, g2 = lane_p < 64, lane_p < 128, lane_p < 192
    wsum = jnp.where(g0, a1[15:], jnp.where(g1, a2[13:], jnp.where(g2, a3[9:], a4[1:])))
    wlen = jnp.where(g0, 2.0, jnp.where(g1, 4.0, jnp.where(g2, 8.0, 16.0)))
    tpos = seq_tile * tm + lax.broadcasted_iota(I32, (tm, POOL_WIDTH), 0) + 1
    cnt = jnp.minimum(tpos.astype(F32), wlen)
    pooled = wsum / cnt - u
    y_pool = jnp.dot(pooled.astype(BF16), pp_ref[...], preferred_element_type=F32) * ps_ref[...]

    lane = lax.broadcasted_iota(I32, (tm, LANES), 1)
    is_a = lane < HEAD_DIM
    lane_row = lax.broadcasted_iota(I32, (1, LANES), 1)
    mask_a = jnp.where(lane_row < HEAD_DIM, 1.0, 0.0).astype(BF16)
    mask_b = jnp.where(lane_row < HEAD_DIM, 0.0, 1.0).astype(BF16)
    y_mem = []
    for pr in range(MEM_WIDTH // LANES):
        cols = slice(pr * LANES, (pr + 1) * LANES)
        qp, kp, vp = qm_ref[:, cols], km_ref[:, cols], vm_ref[:, cols]
        outs = []
        for msk in (mask_a, mask_b):
            s = lax.dot_general(qp * msk, kp, (((1,), (1,)), ((), ())), preferred_element_type=F32)
            m = jnp.max(s, axis=-1, keepdims=True)
            p = jnp.exp(s - m)
            p = p / jnp.sum(p, axis=-1, keepdims=True)
            outs.append(jnp.dot(p.astype(BF16), vp, preferred_element_type=F32))
        y_mem.append(jnp.where(is_a, outs[0], outs[1]))

    rest = jnp.concatenate([y_pool] + y_mem, axis=1).astype(BF16)
    proj = jnp.dot(ya_ref[...], wo_ref[:ATTN_WIDTH, :], preferred_element_type=F32)
    proj += jnp.dot(rest, wo_ref[ATTN_WIDTH:, :], preferred_element_type=F32)
    x2 = x_ref[...] + proj
    x2_ref[...] = x2

    ms = jnp.mean(x2 * x2, axis=-1, keepdims=True)
    h2 = (x2 * lax.rsqrt(ms + EPS) * fn_ref[...]).astype(BF16)
    bits = lax.bitcast_convert_type(h2.astype(F32), U32)
    h2p_ref[...] = bits[:, :PACKED] | (bits[:, PACKED:] >> 16)

    logits = jnp.dot(h2, wr_ref[...], preferred_element_type=F32) + br_ref[...]
    ninf = -jnp.inf
    lane_f = lane.astype(F32)
    grp_f = ((lane - EXPERT_LANE0) >> 3).astype(F32)
    is_g = lane < N_GROUPS
    lg = jnp.where(is_g, logits, ninf)
    mg = jnp.max(lg, axis=-1, keepdims=True)
    g_sel = jnp.min(jnp.where(lg == mg, lane_f, float(LANES)), axis=-1, keepdims=True)
    w_g = 1.0 / jnp.sum(jnp.where(is_g, jnp.exp(logits - mg), 0.0), axis=-1, keepdims=True)
    in_grp = (lane >= EXPERT_LANE0) & (grp_f == g_sel)
    le = jnp.where(in_grp, logits, ninf)
    v1 = jnp.max(le, axis=-1, keepdims=True)
    i1 = jnp.min(jnp.where(le == v1, lane_f, float(2 * LANES)), axis=-1, keepdims=True)
    le2 = jnp.where(lane_f == i1, ninf, le)
    v2 = jnp.max(le2, axis=-1, keepdims=True)
    i2 = jnp.min(jnp.where(le2 == v2, lane_f, float(2 * LANES)), axis=-1, keepdims=True)
    e21 = jnp.exp(v2 - v1)
    gate1 = w_g / (1.0 + e21)
    gate2 = w_g * e21 / (1.0 + e21)

    hot1 = lane_f == i1
    hot2 = lane_f == i2
    onehot = jnp.where(hot1 | hot2, 1.0, 0.0)
    prefix = jnp.dot(ltri_ref[...], onehot.astype(BF16), preferred_element_type=F32)
    tot = prefix + base_ref[...]
    rank1 = jnp.sum(jnp.where(hot1, tot, 0.0), axis=-1, keepdims=True)
    rank2 = jnp.sum(jnp.where(hot2, tot, 0.0), axis=-1, keepdims=True)
    base_new = base_ref[...] + jnp.sum(onehot, axis=0, keepdims=True)
    base_ref[...] = base_new
    cnt_ref[...] = base_new

    e1 = i1 - float(EXPERT_LANE0)
    e2 = i2 - float(EXPERT_LANE0)
    rinfo = jnp.where(lane == 0, e1, jnp.where(lane == 1, e2, jnp.where(lane == 2, rank1, jnp.where(
        lane == 3, rank2, jnp.where(lane == 4, gate1, jnp.where(lane == 5, gate2, 0.0))))))
    rinfo_ref[...] = rinfo
    rinfot_ref[...] = jnp.transpose(rinfo)[:8, :]


def _mix(ya, u, qm, km, vm, x2d, pool_bd, pool_scale, w_out, ffn_norm, w_r, b_r, B, S):
    T = B * S
    tm = TM_MIX
    tps = S // tm
    hb = tm // HALO
    const = lambda i: (0, 0)
    ltri = (jnp.arange(tm)[:, None] > jnp.arange(tm)[None, :]).astype(BF16)
    return pl.pallas_call(
        functools.partial(_mix_kernel, tps),
        grid=(T // tm,),
        in_specs=[pl.BlockSpec((tm, ATTN_WIDTH), lambda i: (i, 0)),
                  pl.BlockSpec((tm, POOL_WIDTH), lambda i: (i, 0)),
                  pl.BlockSpec((HALO, POOL_WIDTH), lambda i: (jnp.maximum(i * hb - 1, 0), 0)),
                  pl.BlockSpec((tm, MEM_WIDTH), lambda i: (i, 0)),
                  pl.BlockSpec((None, N_MEM, MEM_WIDTH), lambda i: (i // tps, 0, 0)),
                  pl.BlockSpec((None, N_MEM, MEM_WIDTH), lambda i: (i // tps, 0, 0)),
                  pl.BlockSpec((tm, D_MODEL), lambda i: (i, 0)),
                  pl.BlockSpec((POOL_WIDTH, POOL_WIDTH), const),
                  pl.BlockSpec((1, POOL_WIDTH), const),
                  pl.BlockSpec((D_MODEL, D_MODEL), const),
                  pl.BlockSpec((1, D_MODEL), const),
                  pl.BlockSpec((D_MODEL, LANES), const),
                  pl.BlockSpec((1, LANES), const),
                  pl.BlockSpec((tm, tm), const)],
        out_specs=[pl.BlockSpec((tm, D_MODEL), lambda i: (i, 0)),
                   pl.BlockSpec((tm, PACKED), lambda i: (i, 0)),
                   pl.BlockSpec((tm, LANES), lambda i: (i, 0)),
                   pl.BlockSpec((8, tm), lambda i: (0, i)),
                   pl.BlockSpec((1, LANES), const)],
        out_shape=[jax.ShapeDtypeStruct((T, D_MODEL), F32),
                   jax.ShapeDtypeStruct((T, PACKED), U32),
                   jax.ShapeDtypeStruct((T, LANES), F32),
                   jax.ShapeDtypeStruct((8, T), F32),
                   jax.ShapeDtypeStruct((1, LANES), F32)],
        scratch_shapes=[pltpu.VMEM((1, LANES), F32)],
        compiler_params=_cparams(("arbitrary",)),
        name="mix_router",
    )(ya, u, u, qm, km, vm, x2d, pool_bd, pool_scale, w_out, ffn_norm, w_r, b_r, ltri)


def _dest_kernel(rt_ref, ps_ref, d_ref):
    tn = rt_ref.shape[1]
    sub = lax.broadcasted_iota(I32, (N_EXPERTS, tn), 0).astype(F32)
    ps = ps_ref[...]
    rows = []
    for k in range(2):
        e = rt_ref[k:k + 1, :]
        start = jnp.sum(jnp.where(sub == e, ps, 0.0), axis=0, keepdims=True)
        rows.append(start + rt_ref[2 + k:3 + k, :])
    word = rows[0].astype(I32) | (rows[1].astype(I32) << 16)
    d_ref[...] = jnp.concatenate([word, jnp.zeros((7, tn), I32)], axis=0)


def _dest(rinfot, pstart_col, T):
    tn = 2048
    return pl.pallas_call(
        _dest_kernel,
        grid=(T // tn,),
        in_specs=[pl.BlockSpec((8, tn), lambda i: (0, i)),
                  pl.BlockSpec((N_EXPERTS, 1), lambda i: (0, 0))],
        out_specs=pl.BlockSpec((8, tn), lambda i: (0, i)),
        out_shape=jax.ShapeDtypeStruct((8, T), I32),
        compiler_params=_cparams(("parallel",)),
        name="dest_rows",
    )(rinfot, pstart_col)


def _rowmap_kernel(d_ref, fill_hbm, inv_ref, sem):
    T = d_ref.shape[0]
    cp = pltpu.make_async_copy(fill_hbm, inv_ref, sem)
    cp.start()
    cp.wait()
    unroll = 8

    def body(g, carry):
        t0 = g * unroll
        words = [d_ref[t0 + u] for u in range(unroll)]
        for u, w in enumerate(words):
            inv_ref[w & 0xFFFF] = t0 + u
            inv_ref[lax.shift_right_logical(w, 16)] = T + t0 + u
        return carry

    lax.fori_loop(0, T // unroll, body, 0)


def _rowmap(dword, R):
    assert R <= 1 << 16
    T = dword.shape[0]
    fill = 2 * T + jnp.arange(R, dtype=I32) % (2 * TR_MOE)
    return pl.pallas_call(
        _rowmap_kernel,
        grid_spec=pltpu.PrefetchScalarGridSpec(
            num_scalar_prefetch=1,
            grid=(1,),
            in_specs=[pl.BlockSpec(memory_space=pl.ANY)],
            out_specs=pl.BlockSpec(memory_space=pltpu.SMEM),
            scratch_shapes=[pltpu.SemaphoreType.DMA(())]),
        out_shape=jax.ShapeDtypeStruct((R,), I32),
        compiler_params=_cparams(("arbitrary",)),
        name="rowmap",
    )(dword, fill)


def _moe_kernel(be_ref, nu_ref, first_ref, nxt_ref, ws_ref, inv_ref,
                h_hbm, w1_hbm, w3_hbm, w2_hbm, o_hbm,
                tab, xbuf0, xbuf1, a0, b0, a1, b1, ybuf, w1s, w3s, w2s, w1b, w3b, w2b, tsem, ssem, wsem):
    j = pl.program_id(0)
    nu = nu_ref[0]
    T = h_hbm.shape[0]
    tr = TR_MOE
    n_tiles = inv_ref.shape[0] // tr
    xbuf = (xbuf0, xbuf1)
    abuf = ((a0, b0), (a1, b1))

    def tile_rows(ref, row, n=1):
        return ref.at[pl.ds(pl.multiple_of(row * ROW_SUB, ROW_SUB), n * ROW_SUB)]

    def gather_tile(tile, slot):
        for i in range(tr):
            v = inv_ref[tile * tr + i]
            if T & (T - 1) == 0:
                tok = v & (T - 1)
            else:
                tok = jnp.where(v >= 2 * T, 0, jnp.where(v >= T, v - T, v))
            xbuf[slot][pl.ds(i, 1), :] = tab[pl.ds(tok, 1), :]

    def scatter_row(tile, slot, i):
        v = inv_ref[tile * tr + i]
        return pltpu.make_async_copy(tile_rows(ybuf, slot * tr + i), tile_rows(o_hbm, v), ssem.at[slot])

    def scatter_wait(slot):
        pltpu.make_async_copy(tile_rows(ybuf, slot * tr, tr), tile_rows(o_hbm, 0, tr), ssem.at[slot]).wait()

    def weight_copies(e, slot):
        return [pltpu.make_async_copy(src.at[e], dst.at[slot], wsem.at[slot])
                for src, dst in ((w1_hbm, w1s), (w3_hbm, w3s), (w2_hbm, w2s))]

    def up_project(slot):
        w = xbuf[slot][...]
        lo = lax.bitcast_convert_type(w & jnp.uint32(0xFFFF0000), F32).astype(BF16)
        hi = lax.bitcast_convert_type(w << 16, F32).astype(BF16)
        for dst, wb in zip(abuf[slot], (w1b, w3b)):
            dst[...] = (jnp.dot(lo, wb[:PACKED, :], preferred_element_type=F32)
                        + jnp.dot(hi, wb[PACKED:, :], preferred_element_type=F32))

    @pl.when(j < nu)
    def _():
        s = j & 1
        jn = jnp.minimum(j + 1, n_tiles - 1)

        @pl.when(j == 0)
        def _():
            ybuf[...] = jnp.zeros_like(ybuf)
            scratch_rows = pltpu.make_async_copy(ybuf, tile_rows(o_hbm, 2 * T, 2 * tr), tsem)
            scratch_rows.start()
            scratch_rows.wait()
            table = pltpu.make_async_copy(h_hbm, tab, tsem)
            table.start()
            for cp in weight_copies(be_ref[0], 0):
                cp.start()
            table.wait()
            gather_tile(0, 0)
            gather_tile(jnp.minimum(1, n_tiles - 1), 1)
            for cp in weight_copies(be_ref[0], 0):
                cp.wait()

            @pl.when(nxt_ref[0] >= 0)
            def _():
                for cp in weight_copies(nxt_ref[0], 1):
                    cp.start()

            w1b[...] = w1s[0].astype(BF16)
            w3b[...] = w3s[0].astype(BF16)
            w2b[...] = w2s[0].astype(BF16)
            up_project(0)

        @pl.when((j >= 1) & (first_ref[j] == 1))
        def _():
            w2b[...] = w2s[ws_ref[j]].astype(BF16)

        @pl.when((j + 1 < nu) & (first_ref[jn] == 1))
        def _():
            wslot = ws_ref[jn]
            for cp in weight_copies(be_ref[jn], wslot):
                cp.wait()
            e_next = nxt_ref[jn]

            @pl.when(e_next >= 0)
            def _():
                for cp in weight_copies(e_next, 1 - wslot):
                    cp.start()

            w1b[...] = w1s[wslot].astype(BF16)
            w3b[...] = w3s[wslot].astype(BF16)

        @pl.when(j >= 2)
        def _():
            scatter_wait(s)

        for slot in range(2):
            @pl.when(s == slot)
            def _():
                gather_tile(jnp.minimum(j + 2, n_tiles - 1), slot)
                a = abuf[slot][0][...]
                b = abuf[slot][1][...]
                up_project(1 - slot)
                hmid = (a / (1.0 + jnp.exp(-a)) * b).astype(BF16)
                y = jnp.dot(hmid, w2b[...], preferred_element_type=F32)
                for c in range(ROW_SUB):
                    ybuf[pl.ds(slot * (tr * ROW_SUB) + c, tr, stride=ROW_SUB), :] = y[:, c * LANES:(c + 1) * LANES]
                for i in range(tr):
                    scatter_row(j, slot, i).start()

        @pl.when(j == nu - 1)
        def _():
            scatter_wait(s)

            @pl.when(j >= 1)
            def _():
                scatter_wait(1 - s)


def _moe(blk_exp, n_used, first, nxt, wslot, inv, h2p, w1, w3, w2):
    T = h2p.shape[0]
    tr = TR_MOE
    n_tiles = blk_exp.shape[0]
    any_spec = pl.BlockSpec(memory_space=pl.ANY)
    return pl.pallas_call(
        _moe_kernel,
        grid_spec=pltpu.PrefetchScalarGridSpec(
            num_scalar_prefetch=6,
            grid=(n_tiles,),
            in_specs=[any_spec] * 4,
            out_specs=any_spec,
            scratch_shapes=[pltpu.VMEM((T, PACKED), U32),
                            pltpu.VMEM((tr, PACKED), U32),
                            pltpu.VMEM((tr, PACKED), U32),
                            pltpu.VMEM((tr, EXPERT_HIDDEN), F32),
                            pltpu.VMEM((tr, EXPERT_HIDDEN), F32),
                            pltpu.VMEM((tr, EXPERT_HIDDEN), F32),
                            pltpu.VMEM((tr, EXPERT_HIDDEN), F32),
                            pltpu.VMEM((2 * tr * ROW_SUB, LANES), F32),
                            pltpu.VMEM((2, D_MODEL, EXPERT_HIDDEN), F32),
                            pltpu.VMEM((2, D_MODEL, EXPERT_HIDDEN), F32),
                            pltpu.VMEM((2, EXPERT_HIDDEN, D_MODEL), F32),
                            pltpu.VMEM((D_MODEL, EXPERT_HIDDEN), BF16),
                            pltpu.VMEM((D_MODEL, EXPERT_HIDDEN), BF16),
                            pltpu.VMEM((EXPERT_HIDDEN, D_MODEL), BF16),
                            pltpu.SemaphoreType.DMA(()),
                            pltpu.SemaphoreType.DMA((2,)),
                            pltpu.SemaphoreType.DMA((2,))]),
        out_shape=jax.ShapeDtypeStruct(((2 * T + 2 * tr) * ROW_SUB, LANES), F32),
        compiler_params=pltpu.CompilerParams(dimension_semantics=("arbitrary",), vmem_limit_bytes=MOE_VMEM_LIMIT),
        name="moe_experts",
    )(blk_exp, n_used, first, nxt, wslot, inv, h2p, w1, w3, w2)


def _combine_kernel(x2_ref, rinfo_ref, y1_ref, y2_ref, o_ref):
    g1 = rinfo_ref[:, 4:5]
    g2 = rinfo_ref[:, 5:6]
    tc = x2_ref.shape[0]
    for c in range(ROW_SUB):
        cols = slice(c * LANES, (c + 1) * LANES)
        rows = pl.ds(c, tc, stride=ROW_SUB)
        o_ref[:, cols] = x2_ref[:, cols] + (g1 * y1_ref[rows, :] + g2 * y2_ref[rows, :])


def _combine(x2, rinfo, y2slot):
    T = x2.shape[0]
    tc = TC_COMBINE
    return pl.pallas_call(
        _combine_kernel,
        grid=(T // tc,),
        in_specs=[pl.BlockSpec((tc, D_MODEL), lambda i: (i, 0)),
                  pl.BlockSpec((tc, LANES), lambda i: (i, 0)),
                  pl.BlockSpec((tc * ROW_SUB, LANES), lambda i: (i, 0)),
                  pl.BlockSpec((tc * ROW_SUB, LANES), lambda i: (T // tc + i, 0))],
        out_specs=pl.BlockSpec((tc, D_MODEL), lambda i: (i, 0)),
        out_shape=jax.ShapeDtypeStruct((T, D_MODEL), F32),
        compiler_params=_cparams(("parallel",)),
        name="combine",
    )(x2, rinfo, y2slot, y2slot)


def _layer(x, mem, attn_norm, w_in, q_norm, k_norm, pool_proj, pool_scale, mem_norm, w_mem_kv,
           mq_norm, mk_norm, w_out, ffn_norm, w_group, b_group, w_router, b_router, w1, w3, w2):
    B, S, D = x.shape
    T = B * S
    assert D == D_MODEL and S % SUPER == 0 and T % TM_PROJ == 0
    x2d = x.reshape(T, D)
    row = lambda v: v.reshape(1, -1).astype(F32)
    scale = HEAD_DIM ** -0.5
    gq = row(jnp.tile(q_norm, ATTN_WIDTH // HEAD_DIM) * scale)
    gk = row(jnp.tile(k_norm, ATTN_WIDTH // HEAD_DIM))
    gmq = row(jnp.tile(mq_norm, MEM_WIDTH // HEAD_DIM) * scale)
    gmk = row(jnp.tile(mk_norm, MEM_WIDTH // HEAD_DIM))

    (q1, k1, v1, q4, k4, v4, q16, k16, v16, u, qm) = _proj(
        x2d, row(attn_norm), w_in.astype(BF16), gq, gk, gmq, B, S)
    ya = _attn(q1, k1, v1, q4, k4, v4, q16, k16, v16, B, S).reshape(T, ATTN_WIDTH)
    km, vm = _memkv(mem, row(mem_norm), w_mem_kv.astype(BF16), gmk, B)

    pool_bd = jax.scipy.linalg.block_diag(*[pool_proj[g] for g in range(pool_proj.shape[0])]).astype(BF16)
    w_r = jnp.zeros((D, LANES), F32)
    w_r = w_r.at[:, :N_GROUPS].set(w_group)
    w_r = w_r.at[:, EXPERT_LANE0:].set(jnp.transpose(w_router, (1, 0, 2)).reshape(D, N_EXPERTS))
    b_r = jnp.zeros((1, LANES), F32).at[0, :N_GROUPS].set(b_group).at[0, EXPERT_LANE0:].set(b_router.reshape(-1))
    x2, h2p, rinfo, rinfot, counts = _mix(ya, u, qm, km, vm, x2d, pool_bd, row(pool_scale),
                                          w_out.astype(BF16), row(ffn_norm), w_r.astype(BF16), b_r, B, S)

    R, pstart, sched = _tile_schedule(counts, T)
    dest = _dest(rinfot, pstart.astype(F32).reshape(N_EXPERTS, 1), T)
    inv = _rowmap(dest[0], R)
    y2slot = _moe(*sched, inv, h2p, w1, w3, w2)
    out = _combine(x2, rinfo, y2slot)
    return out.reshape(B, S, D)


def _tile_schedule(counts, T):
    tr = TR_MOE
    R = 2 * T + N_EXPERTS * tr
    n_tiles = R // tr
    cnt = counts[0, EXPERT_LANE0:].astype(I32)
    padded = ((cnt + tr - 1) // tr) * tr
    pend = jnp.cumsum(padded)
    pstart = pend - padded
    n_used = (pend[-1] // tr).astype(I32).reshape(1)
    tiles = jnp.arange(n_tiles, dtype=I32)
    tile_row = jnp.minimum(tiles, n_used[0] - 1) * tr
    blk_exp = jnp.minimum(jnp.sum(tile_row[:, None] >= pend[None, :], axis=1), N_EXPERTS - 1).astype(I32)
    in_use = tiles < n_used[0]
    first = in_use & ((tiles == 0) | (blk_exp != jnp.roll(blk_exp, 1)))
    run_idx = jnp.cumsum(first.astype(I32)) - 1
    used = cnt > 0
    exp_of_run = jnp.argsort(jnp.logical_not(used), stable=True).astype(I32)
    nxt = jnp.where(run_idx + 1 < jnp.sum(used), exp_of_run[jnp.clip(run_idx + 1, 0, N_EXPERTS - 1)], -1)
    sched = (blk_exp, n_used, first.astype(I32), nxt.astype(I32), (run_idx & 1).astype(I32))
    return R, pstart, sched


def kernel(x, mem, attn_norm, w_in, q_norm, k_norm, pool_proj, pool_scale, mem_norm, w_mem_kv, mq_norm, mk_norm,
           w_out, ffn_norm, w_group, b_group, w_router, b_router, w1, w3, w2):
    for l in range(attn_norm.shape[0]):
        x = _layer(x, mem, attn_norm[l], w_in[l], q_norm[l], k_norm[l], pool_proj[l], pool_scale[l],
                   mem_norm[l], w_mem_kv[l], mq_norm[l], mk_norm[l], w_out[l], ffn_norm[l],
                   w_group[l], b_group[l], w_router[l], b_router[l], w1[l], w3[l], w2[l])
    return x
```

```python
import functools

import jax
import jax.numpy as jnp
from jax import lax
from jax.experimental import pallas as pl
from jax.experimental.pallas import tpu as pltpu

F32 = jnp.float32
BF16 = jnp.bfloat16
I32 = jnp.int32
U32 = jnp.uint32

D_MODEL = 1024
HEAD_DIM = 64
ATTN_WIDTH = 512
POOL_WIDTH = 256
MEM_WIDTH = 256
N_MEM = 256
IN_WIDTH = 3 * ATTN_WIDTH + POOL_WIDTH + MEM_WIDTH
N_GROUPS = 8
EXPERTS_PER_GROUP = 8
N_EXPERTS = 64
EXPERT_HIDDEN = 512
EPS = 1e-6
NEG_INF = -1e30

LANES = 128
N_PAIRS = ATTN_WIDTH // LANES
BLOCK = 128
SUPER = 16 * BLOCK
HALO = 16
ROW_SUB = D_MODEL // LANES
PACKED = D_MODEL // 2

TM_PROJ = 1024
TM_MIX = 1024
TR_MOE = 256
TC_COMBINE = 1024
EXPERT_LANE0 = 64

VMEM_LIMIT = 48 * 1024 * 1024
MOE_VMEM_LIMIT = 56 * 1024 * 1024


def _cparams(sem):
    return pltpu.CompilerParams(dimension_semantics=sem, vmem_limit_bytes=VMEM_LIMIT)


def _group_ones(n):
    g = jnp.arange(n) // HEAD_DIM
    return (g[:, None] == g[None, :]).astype(BF16)


def _head_norm(z, ones_ref, gain_ref):
    ss = jnp.dot((z * z).astype(BF16), ones_ref[...], preferred_element_type=F32)
    return z * lax.rsqrt(ss * (1.0 / HEAD_DIM) + EPS) * gain_ref[...]


def _proj_kernel(x_ref, an_ref, win_ref, gq_ref, gk_ref, gm_ref, o512_ref, o256_ref,
                 q1_ref, k1_ref, v1_ref, q4_ref, k4_ref, v4_ref, q16_ref, k16_ref, v16_ref,
                 u_ref, qm_ref, zs_ref, z4_ref):
    tm = x_ref.shape[0]
    x = x_ref[...]
    ms = jnp.mean(x * x, axis=-1, keepdims=True)
    h = (x * lax.rsqrt(ms + EPS) * an_ref[...]).astype(BF16)
    a0, a1, a2, a3 = ATTN_WIDTH, 2 * ATTN_WIDTH, 3 * ATTN_WIDTH, 3 * ATTN_WIDTH + POOL_WIDTH

    def cols(lo, hi):
        return jnp.dot(h, win_ref[:, lo:hi], preferred_element_type=F32)

    groups = ((lambda: _head_norm(cols(0, a0), o512_ref, gq_ref), q1_ref, q4_ref, q16_ref),
              (lambda: _head_norm(cols(a0, a1), o512_ref, gk_ref), k1_ref, k4_ref, k16_ref),
              (lambda: cols(a1, a2), v1_ref, v4_ref, v16_ref))
    for make, o1, o4, o16 in groups:
        val = make()
        for hp in range(N_PAIRS):
            pair = val[:, hp * LANES:(hp + 1) * LANES]
            zs_ref[hp] = pair
            o1[hp] = pair.astype(BF16)
        q4n = tm // 4
        for hp in range(N_PAIRS):
            for r4 in range(4):
                rows = zs_ref[hp, pl.ds(r4, q4n, stride=4), :]
                z4_ref[hp, pl.ds(r4 * q4n, q4n), :] = rows
                o4[hp, :, r4 * LANES:(r4 + 1) * LANES] = rows.astype(BF16)
            for r16 in range(16):
                r4, a = r16 % 4, r16 // 4
                rows = z4_ref[hp, pl.ds(r4 * q4n + a, tm // 16, stride=4), :]
                o16[hp, :, r16 * LANES:(r16 + 1) * LANES] = rows.astype(BF16)
    u_ref[...] = cols(a2, a3)
    qm_ref[...] = _head_norm(cols(a3, IN_WIDTH), o256_ref, gm_ref).astype(BF16)


def _proj(x2d, attn_norm, w_in, gq, gk, gm, B, S):
    T = B * S
    tm = TM_PROJ
    nj = S // tm
    const = lambda i: (0, 0)

    def lay(d):
        return jax.ShapeDtypeStruct((B, N_PAIRS, S // d, d * LANES), BF16)

    def lay_spec(d):
        return pl.BlockSpec((None, N_PAIRS, tm // d, d * LANES), lambda i: (i // nj, 0, i % nj, 0))

    out_shape = [lay(1)] * 3 + [lay(4)] * 3 + [lay(16)] * 3 + [
        jax.ShapeDtypeStruct((T, POOL_WIDTH), F32), jax.ShapeDtypeStruct((T, MEM_WIDTH), BF16)]
    out_specs = [lay_spec(1)] * 3 + [lay_spec(4)] * 3 + [lay_spec(16)] * 3 + [
        pl.BlockSpec((tm, POOL_WIDTH), lambda i: (i, 0)), pl.BlockSpec((tm, MEM_WIDTH), lambda i: (i, 0))]
    return pl.pallas_call(
        _proj_kernel,
        grid=(T // tm,),
        in_specs=[pl.BlockSpec((tm, D_MODEL), lambda i: (i, 0)),
                  pl.BlockSpec((1, D_MODEL), const),
                  pl.BlockSpec((D_MODEL, IN_WIDTH), const),
                  pl.BlockSpec((1, ATTN_WIDTH), const),
                  pl.BlockSpec((1, ATTN_WIDTH), const),
                  pl.BlockSpec((1, MEM_WIDTH), const),
                  pl.BlockSpec((ATTN_WIDTH, ATTN_WIDTH), const),
                  pl.BlockSpec((MEM_WIDTH, MEM_WIDTH), const)],
        out_specs=out_specs,
        out_shape=out_shape,
        scratch_shapes=[pltpu.VMEM((N_PAIRS, tm, LANES), F32)] * 2,
        compiler_params=_cparams(("parallel",)),
        name="proj",
    )(x2d, attn_norm, w_in, gq, gk, gm, _group_ones(ATTN_WIDTH), _group_ones(MEM_WIDTH))


def _attn_kernel(q1, k1c, k1p, v1c, v1p, q4, k4c, k4p, v4c, v4p, q16, k16c, k16p, v16c, v16p,
                 bias_ref, o_ref, obuf, lbuf):
    c = pl.program_id(2)
    lane = lax.broadcasted_iota(I32, (BLOCK, LANES), 1)
    is_a = lane < HEAD_DIM
    lane_row = lax.broadcasted_iota(I32, (1, LANES), 1)
    mask_a = jnp.where(lane_row < HEAD_DIM, 1.0, 0.0).astype(BF16)
    mask_b = jnp.where(lane_row < HEAD_DIM, 0.0, 1.0).astype(BF16)
    bias_full = bias_ref[0]
    bias_first = jnp.where(c > 0, bias_full, bias_ref[1])

    def tile(q_t, kp_t, kc_t, vp_t, vc_t, bias):
        lhs = jnp.concatenate([q_t * mask_a, q_t * mask_b], axis=0)
        keys = jnp.concatenate([kp_t, kc_t], axis=0)
        s = lax.dot_general(lhs, keys, (((1,), (1,)), ((), ())), preferred_element_type=F32) + bias
        m = jnp.max(s, axis=-1, keepdims=True)
        p = jnp.exp(s - m)
        l = jnp.sum(p, axis=-1, keepdims=True)
        vals = jnp.concatenate([vp_t, vc_t], axis=0)
        pv = jnp.dot(p.astype(BF16), vals, preferred_element_type=F32) * (1.0 / l)
        lse = m + jnp.log(l)
        return jnp.where(is_a, pv[:BLOCK], pv[BLOCK:]), jnp.where(is_a, lse[:BLOCK], lse[BLOCK:])

    for pat, (d, q, kc, kp, vc, vp) in enumerate(((16, q16, k16c, k16p, v16c, v16p), (4, q4, k4c, k4p, v4c, v4p))):
        nblk = SUPER // (BLOCK * d)
        for r in range(d):
            cols = pl.ds(r * LANES, LANES)
            for jb in range(nblk):
                cur = pl.ds(jb * BLOCK, BLOCK)
                if jb == 0:
                    kp_t, vp_t, bias = kp[:, cols], vp[:, cols], bias_first
                else:
                    prev = pl.ds((jb - 1) * BLOCK, BLOCK)
                    kp_t, vp_t, bias = kc[prev, cols], vc[prev, cols], bias_full
                o_t, lse_t = tile(q[cur, cols], kp_t, kc[cur, cols], vp_t, vc[cur, cols], bias)
                rows = pl.ds(jb * BLOCK * d + r, BLOCK, stride=d)
                obuf[pat, rows, :] = o_t
                lbuf[pat, rows, :] = lse_t
    for jb in range(SUPER // BLOCK):
        cur = pl.ds(jb * BLOCK, BLOCK)
        if jb == 0:
            kp_t, vp_t, bias = k1p[...], v1p[...], bias_first
        else:
            prev = pl.ds((jb - 1) * BLOCK, BLOCK)
            kp_t, vp_t, bias = k1c[prev, :], v1c[prev, :], bias_full
        o0, l0 = tile(q1[cur, :], kp_t, k1c[cur, :], vp_t, v1c[cur, :], bias)
        l1, l2 = lbuf[0, cur, :], lbuf[1, cur, :]
        top = jnp.maximum(jnp.maximum(l0, l1), l2)
        w0, w1, w2 = jnp.exp(l0 - top), jnp.exp(l1 - top), jnp.exp(l2 - top)
        mixed = (w0 * o0 + w1 * obuf[0, cur, :] + w2 * obuf[1, cur, :]) * (1.0 / (w0 + w1 + w2))
        o_ref[cur, :] = mixed.astype(BF16)


def _band_bias():
    qi = jnp.arange(BLOCK)[:, None]
    kj = jnp.arange(2 * BLOCK)[None, :]
    dist = qi + BLOCK - kj
    in_band = (dist >= 0) & (dist <= BLOCK)
    full = jnp.where(in_band, 0.0, NEG_INF).astype(F32)
    first = jnp.where(in_band & (kj >= BLOCK), 0.0, NEG_INF).astype(F32)
    return jnp.stack([jnp.tile(full, (2, 1)), jnp.tile(first, (2, 1))])


def _attn(q1, k1, v1, q4, k4, v4, q16, k16, v16, B, S):
    nsup = S // SUPER

    def specs(d):
        rows = SUPER // d
        per = rows // BLOCK
        cur = pl.BlockSpec((None, None, rows, d * LANES), lambda b, hp, c: (b, hp, c, 0))
        prev = pl.BlockSpec((None, None, BLOCK, d * LANES),
                            lambda b, hp, c: (b, hp, jnp.maximum(per * c - 1, 0), 0))
        return [cur, cur, prev, cur, prev]

    return pl.pallas_call(
        _attn_kernel,
        grid=(B, N_PAIRS, nsup),
        in_specs=specs(1) + specs(4) + specs(16) + [
            pl.BlockSpec((2, 2 * BLOCK, 2 * BLOCK), lambda b, hp, c: (0, 0, 0))],
        out_specs=pl.BlockSpec((None, SUPER, LANES), lambda b, hp, c: (b, c, hp)),
        out_shape=jax.ShapeDtypeStruct((B, S, ATTN_WIDTH), BF16),
        scratch_shapes=[pltpu.VMEM((2, SUPER, LANES), F32)] * 2,
        compiler_params=_cparams(("parallel", "parallel", "parallel")),
        name="dilated_attn",
    )(q1, k1, k1, v1, v1, q4, k4, k4, v4, v4, q16, k16, k16, v16, v16, _band_bias())


def _memkv_kernel(mem_ref, mn_ref, wkv_ref, gk_ref, o256_ref, km_ref, vm_ref):
    m = mem_ref[...]
    ms = jnp.mean(m * m, axis=-1, keepdims=True)
    mn = (m * lax.rsqrt(ms + EPS) * mn_ref[...]).astype(BF16)
    kv = jnp.dot(mn, wkv_ref[...], preferred_element_type=F32)
    km_ref[...] = _head_norm(kv[:, :MEM_WIDTH], o256_ref, gk_ref).astype(BF16)
    vm_ref[...] = kv[:, MEM_WIDTH:].astype(BF16)


def _memkv(mem, mem_norm, w_mem_kv, gmk, B):
    const = lambda b: (0, 0)
    return pl.pallas_call(
        _memkv_kernel,
        grid=(B,),
        in_specs=[pl.BlockSpec((None, N_MEM, D_MODEL), lambda b: (b, 0, 0)),
                  pl.BlockSpec((1, D_MODEL), const),
                  pl.BlockSpec((D_MODEL, 2 * MEM_WIDTH), const),
                  pl.BlockSpec((1, MEM_WIDTH), const),
                  pl.BlockSpec((MEM_WIDTH, MEM_WIDTH), const)],
        out_specs=[pl.BlockSpec((None, N_MEM, MEM_WIDTH), lambda b: (b, 0, 0))] * 2,
        out_shape=[jax.ShapeDtypeStruct((B, N_MEM, MEM_WIDTH), BF16)] * 2,
        compiler_params=_cparams(("parallel",)),
        name="memkv",
    )(mem, mem_norm, w_mem_kv, gmk, _group_ones(MEM_WIDTH))


def _mix_kernel(tiles_per_seq, ya_ref, u_ref, uh_ref, qm_ref, km_ref, vm_ref, x_ref, pp_ref, ps_ref,
                wo_ref, fn_ref, wr_ref, br_ref, utri_ref,
                x2_ref, h2p_ref, rinfo_ref, rinfot_ref, cnt_ref, base_ref):
    i = pl.program_id(0)
    tm = x_ref.shape[0]
    seq_tile = i % tiles_per_seq

    @pl.when(i == 0)
    def _():
        base_ref[...] = jnp.zeros_like(base_ref)

    u = u_ref[...]
    halo = jnp.where(seq_tile == 0, 0.0, uh_ref[...])
    uu = jnp.concatenate([halo, u], axis=0)
    a1 = uu[1:] + uu[:-1]
    a2 = a1[2:] + a1[:-2]
    a3 = a2[4:] + a2[:-4]
    a4 = a3[8:] + a3[:-8]
    lane_p = lax.broadcasted_iota(I32, (tm, POOL_WIDTH), 1)
    g0, g1, g2 = lane_p < 64, lane_p < 128, lane_p < 192
    wsum = jnp.where(g0, a1[15:], jnp.where(g1, a2[13:], jnp.where(g2, a3[9:], a4[1:])))
    wlen = jnp.where(g0, 2.0, jnp.where(g1, 4.0, jnp.where(g2, 8.0, 16.0)))
    tpos = seq_tile * tm + lax.broadcasted_iota(I32, (tm, POOL_WIDTH), 0) + 1
    cnt = jnp.minimum(tpos.astype(F32), wlen)
    pooled = wsum / cnt - u
    y_pool = jnp.dot(pooled.astype(BF16), pp_ref[...], preferred_element_type=F32) * ps_ref[...]

    lane = lax.broadcasted_iota(I32, (tm, LANES), 1)
    is_a = lane < HEAD_DIM
    lane_row = lax.broadcasted_iota(I32, (1, LANES), 1)
    mask_a = jnp.where(lane_row < HEAD_DIM, 1.0, 0.0).astype(BF16)
    mask_b = jnp.where(lane_row < HEAD_DIM, 0.0, 1.0).astype(BF16)
    y_mem = []
    for pr in range(MEM_WIDTH // LANES):
        cols = slice(pr * LANES, (pr + 1) * LANES)
        qp, kp, vp = qm_ref[:, cols], km_ref[:, cols], vm_ref[:, cols]
        outs = []
        for msk in (mask_a, mask_b):
            s = lax.dot_general(qp * msk, kp, (((1,), (1,)), ((), ())), preferred_element_type=F32)
            m = jnp.max(s, axis=-1, keepdims=True)
            p = jnp.exp(s - m)
            p = p / jnp.sum(p, axis=-1, keepdims=True)
            outs.append(jnp.dot(p.astype(BF16), vp, preferred_element_type=F32))
        y_mem.append(jnp.where(is_a, outs[0], outs[1]))

    rest = jnp.concatenate([y_pool] + y_mem, axis=1).astype(BF16)
    proj = jnp.dot(ya_ref[...], wo_ref[:ATTN_WIDTH, :], preferred_element_type=F32)
    proj += jnp.dot(rest, wo_ref[ATTN_WIDTH:, :], preferred_element_type=F32)
    x2 = x_ref[...] + proj
    x2_ref[...] = x2

    ms = jnp.mean(x2 * x2, axis=-1, keepdims=True)
    h2 = (x2 * lax.rsqrt(ms + EPS) * fn_ref[...]).astype(BF16)
    bits = lax.bitcast_convert_type(h2.astype(F32), U32)
    h2p_ref[...] = bits[:, :PACKED] | (bits[:, PACKED:] >> 16)

    logits = jnp.dot(h2, wr_ref[...], preferred_element_type=F32) + br_ref[...]
    lt = jnp.transpose(logits)
    ninf = -jnp.inf
    sub8 = lax.broadcasted_iota(I32, (EXPERTS_PER_GROUP, tm), 0).astype(F32)
    lg = lt[:N_GROUPS]
    mg = jnp.max(lg, axis=0, keepdims=True)
    g_sel = jnp.min(jnp.where(lg == mg, sub8, float(N_GROUPS)), axis=0, keepdims=True)
    w_g = 1.0 / jnp.sum(jnp.exp(lg - mg), axis=0, keepdims=True)
    le = lt[EXPERT_LANE0:EXPERT_LANE0 + EXPERTS_PER_GROUP]
    for g in range(1, N_GROUPS):
        lo_g = EXPERT_LANE0 + g * EXPERTS_PER_GROUP
        le = jnp.where(g_sel == float(g), lt[lo_g:lo_g + EXPERTS_PER_GROUP], le)
    v1 = jnp.max(le, axis=0, keepdims=True)
    i1 = jnp.min(jnp.where(le == v1, sub8, float(EXPERTS_PER_GROUP)), axis=0, keepdims=True)
    le2 = jnp.where(sub8 == i1, ninf, le)
    v2 = jnp.max(le2, axis=0, keepdims=True)
    i2 = jnp.min(jnp.where(le2 == v2, sub8, float(EXPERTS_PER_GROUP)), axis=0, keepdims=True)
    e21 = jnp.exp(v2 - v1)
    gate1 = w_g / (1.0 + e21)
    gate2 = w_g * e21 / (1.0 + e21)
    e1 = g_sel * float(EXPERTS_PER_GROUP) + i1
    e2 = g_sel * float(EXPERTS_PER_GROUP) + i2

    sub64 = lax.broadcasted_iota(I32, (N_EXPERTS, tm), 0).astype(F32)
    hot1 = sub64 == e1
    hot2 = sub64 == e2
    onehot = jnp.where(hot1 | hot2, 1.0, 0.0)
    prefix = jnp.dot(onehot.astype(BF16), utri_ref[...], preferred_element_type=F32)
    tot = prefix + base_ref[...]
    rank1 = jnp.sum(jnp.where(hot1, tot, 0.0), axis=0, keepdims=True)
    rank2 = jnp.sum(jnp.where(hot2, tot, 0.0), axis=0, keepdims=True)
    base_new = base_ref[...] + jnp.sum(onehot, axis=1, keepdims=True)
    base_ref[...] = base_new
    cnt_ref[...] = base_new

    rt = jnp.where(sub8 == 0.0, e1, jnp.where(sub8 == 1.0, e2, jnp.where(sub8 == 2.0, rank1, jnp.where(
        sub8 == 3.0, rank2, jnp.where(sub8 == 4.0, gate1, jnp.where(sub8 == 5.0, gate2, 0.0))))))
    rinfot_ref[...] = rt
    rinfo_ref[...] = jnp.transpose(jnp.concatenate([rt, jnp.zeros((LANES - 8, tm), F32)], axis=0))


def _mix(ya, u, qm, km, vm, x2d, pool_bd, pool_scale, w_out, ffn_norm, w_r, b_r, B, S):
    T = B * S
    tm = TM_MIX
    tps = S // tm
    hb = tm // HALO
    const = lambda i: (0, 0)
    utri = (jnp.arange(tm)[:, None] < jnp.arange(tm)[None, :]).astype(BF16)
    return pl.pallas_call(
        functools.partial(_mix_kernel, tps),
        grid=(T // tm,),
        in_specs=[pl.BlockSpec((tm, ATTN_WIDTH), lambda i: (i, 0)),
                  pl.BlockSpec((tm, POOL_WIDTH), lambda i: (i, 0)),
                  pl.BlockSpec((HALO, POOL_WIDTH), lambda i: (jnp.maximum(i * hb - 1, 0), 0)),
                  pl.BlockSpec((tm, MEM_WIDTH), lambda i: (i, 0)),
                  pl.BlockSpec((None, N_MEM, MEM_WIDTH), lambda i: (i // tps, 0, 0)),
                  pl.BlockSpec((None, N_MEM, MEM_WIDTH), lambda i: (i // tps, 0, 0)),
                  pl.BlockSpec((tm, D_MODEL), lambda i: (i, 0)),
                  pl.BlockSpec((POOL_WIDTH, POOL_WIDTH), const),
                  pl.BlockSpec((1, POOL_WIDTH), const),
                  pl.BlockSpec((D_MODEL, D_MODEL), const),
                  pl.BlockSpec((1, D_MODEL), const),
                  pl.BlockSpec((D_MODEL, LANES), const),
                  pl.BlockSpec((1, LANES), const),
                  pl.BlockSpec((tm, tm), const)],
        out_specs=[pl.BlockSpec((tm, D_MODEL), lambda i: (i, 0)),
                   pl.BlockSpec((tm, PACKED), lambda i: (i, 0)),
                   pl.BlockSpec((tm, LANES), lambda i: (i, 0)),
                   pl.BlockSpec((8, tm), lambda i: (0, i)),
                   pl.BlockSpec((N_EXPERTS, 1), const)],
        out_shape=[jax.ShapeDtypeStruct((T, D_MODEL), F32),
                   jax.ShapeDtypeStruct((T, PACKED), U32),
                   jax.ShapeDtypeStruct((T, LANES), F32),
                   jax.ShapeDtypeStruct((8, T), F32),
                   jax.ShapeDtypeStruct((N_EXPERTS, 1), F32)],
        scratch_shapes=[pltpu.VMEM((N_EXPERTS, 1), F32)],
        compiler_params=_cparams(("arbitrary",)),
        name="mix_router",
    )(ya, u, u, qm, km, vm, x2d, pool_bd, pool_scale, w_out, ffn_norm, w_r, b_r, utri)


def _dest_kernel(rt_ref, ps_ref, d_ref):
    tn = rt_ref.shape[1]
    sub = lax.broadcasted_iota(I32, (N_EXPERTS, tn), 0).astype(F32)
    ps = ps_ref[...]
    rows = []
    for k in range(2):
        e = rt_ref[k:k + 1, :]
        start = jnp.sum(jnp.where(sub == e, ps, 0.0), axis=0, keepdims=True)
        rows.append(start + rt_ref[2 + k:3 + k, :])
    word = rows[0].astype(I32) | (rows[1].astype(I32) << 16)
    d_ref[...] = jnp.concatenate([word, jnp.zeros((7, tn), I32)], axis=0)


def _dest(rinfot, pstart_col, T):
    tn = 2048
    return pl.pallas_call(
        _dest_kernel,
        grid=(T // tn,),
        in_specs=[pl.BlockSpec((8, tn), lambda i: (0, i)),
                  pl.BlockSpec((N_EXPERTS, 1), lambda i: (0, 0))],
        out_specs=pl.BlockSpec((8, tn), lambda i: (0, i)),
        out_shape=jax.ShapeDtypeStruct((8, T), I32),
        compiler_params=_cparams(("parallel",)),
        name="dest_rows",
    )(rinfot, pstart_col)


def _rowmap_kernel(d_ref, fill_hbm, inv_ref, sem):
    T = d_ref.shape[0]
    cp = pltpu.make_async_copy(fill_hbm, inv_ref, sem)
    cp.start()
    cp.wait()
    unroll = 8

    def body(g, carry):
        t0 = g * unroll
        words = [d_ref[t0 + u] for u in range(unroll)]
        for u, w in enumerate(words):
            inv_ref[w & 0xFFFF] = t0 + u
            inv_ref[lax.shift_right_logical(w, 16)] = T + t0 + u
        return carry

    lax.fori_loop(0, T // unroll, body, 0)


def _rowmap(dword, R):
    assert R <= 1 << 16
    T = dword.shape[0]
    fill = 2 * T + jnp.arange(R, dtype=I32) % (2 * TR_MOE)
    return pl.pallas_call(
        _rowmap_kernel,
        grid_spec=pltpu.PrefetchScalarGridSpec(
            num_scalar_prefetch=1,
            grid=(1,),
            in_specs=[pl.BlockSpec(memory_space=pl.ANY)],
            out_specs=pl.BlockSpec(memory_space=pltpu.SMEM),
            scratch_shapes=[pltpu.SemaphoreType.DMA(())]),
        out_shape=jax.ShapeDtypeStruct((R,), I32),
        compiler_params=_cparams(("arbitrary",)),
        name="rowmap",
    )(dword, fill)


def _moe_kernel(be_ref, nu_ref, first_ref, nxt_ref, ws_ref, inv_ref,
                h_hbm, w1_hbm, w3_hbm, w2_hbm, o_hbm,
                tab, xbuf0, xbuf1, a0, b0, a1, b1, ybuf, w1s, w3s, w2s, w1b, w3b, w2b, tsem, ssem, wsem):
    j = pl.program_id(0)
    nu = nu_ref[0]
    T = h_hbm.shape[0]
    tr = TR_MOE
    n_tiles = inv_ref.shape[0] // tr
    xbuf = (xbuf0, xbuf1)
    abuf = ((a0, b0), (a1, b1))

    def tile_rows(ref, row, n=1):
        return ref.at[pl.ds(pl.multiple_of(row * ROW_SUB, ROW_SUB), n * ROW_SUB)]

    def gather_tile(tile, slot):
        for i in range(tr):
            v = inv_ref[tile * tr + i]
            if T & (T - 1) == 0:
                tok = v & (T - 1)
            else:
                tok = jnp.where(v >= 2 * T, 0, jnp.where(v >= T, v - T, v))
            xbuf[slot][pl.ds(i, 1), :] = tab[pl.ds(tok, 1), :]

    def scatter_row(tile, slot, i):
        v = inv_ref[tile * tr + i]
        return pltpu.make_async_copy(tile_rows(ybuf, slot * tr + i), tile_rows(o_hbm, v), ssem.at[slot])

    def scatter_wait(slot):
        pltpu.make_async_copy(tile_rows(ybuf, slot * tr, tr), tile_rows(o_hbm, 0, tr), ssem.at[slot]).wait()

    def weight_copies(e, slot):
        return [pltpu.make_async_copy(src.at[e], dst.at[slot], wsem.at[slot])
                for src, dst in ((w1_hbm, w1s), (w3_hbm, w3s), (w2_hbm, w2s))]

    def up_project(slot):
        w = xbuf[slot][...]
        lo = lax.bitcast_convert_type(w & jnp.uint32(0xFFFF0000), F32).astype(BF16)
        hi = lax.bitcast_convert_type(w << 16, F32).astype(BF16)
        for dst, wb in zip(abuf[slot], (w1b, w3b)):
            dst[...] = (jnp.dot(lo, wb[:PACKED, :], preferred_element_type=F32)
                        + jnp.dot(hi, wb[PACKED:, :], preferred_element_type=F32))

    @pl.when(j < nu)
    def _():
        s = j & 1
        jn = jnp.minimum(j + 1, n_tiles - 1)

        @pl.when(j == 0)
        def _():
            ybuf[...] = jnp.zeros_like(ybuf)
            scratch_rows = pltpu.make_async_copy(ybuf, tile_rows(o_hbm, 2 * T, 2 * tr), tsem)
            scratch_rows.start()
            scratch_rows.wait()
            table = pltpu.make_async_copy(h_hbm, tab, tsem)
            table.start()
            for cp in weight_copies(be_ref[0], 0):
                cp.start()
            table.wait()
            gather_tile(0, 0)
            gather_tile(jnp.minimum(1, n_tiles - 1), 1)
            for cp in weight_copies(be_ref[0], 0):
                cp.wait()

            @pl.when(nxt_ref[0] >= 0)
            def _():
                for cp in weight_copies(nxt_ref[0], 1):
                    cp.start()

            w1b[...] = w1s[0].astype(BF16)
            w3b[...] = w3s[0].astype(BF16)
            w2b[...] = w2s[0].astype(BF16)
            up_project(0)

        @pl.when((j >= 1) & (first_ref[j] == 1))
        def _():
            w2b[...] = w2s[ws_ref[j]].astype(BF16)

        @pl.when((j + 1 < nu) & (first_ref[jn] == 1))
        def _():
            wslot = ws_ref[jn]
            for cp in weight_copies(be_ref[jn], wslot):
                cp.wait()
            e_next = nxt_ref[jn]

            @pl.when(e_next >= 0)
            def _():
                for cp in weight_copies(e_next, 1 - wslot):
                    cp.start()

            w1b[...] = w1s[wslot].astype(BF16)
            w3b[...] = w3s[wslot].astype(BF16)

        @pl.when(j >= 2)
        def _():
            scatter_wait(s)

        for slot in range(2):
            @pl.when(s == slot)
            def _():
                gather_tile(jnp.minimum(j + 2, n_tiles - 1), slot)
                a = abuf[slot][0][...]
                b = abuf[slot][1][...]
                up_project(1 - slot)
                hmid = (a / (1.0 + jnp.exp(-a)) * b).astype(BF16)
                y = jnp.dot(hmid, w2b[...], preferred_element_type=F32)
                for c in range(ROW_SUB):
                    ybuf[pl.ds(slot * (tr * ROW_SUB) + c, tr, stride=ROW_SUB), :] = y[:, c * LANES:(c + 1) * LANES]
                for i in range(tr):
                    scatter_row(j, slot, i).start()

        @pl.when(j == nu - 1)
        def _():
            scatter_wait(s)

            @pl.when(j >= 1)
            def _():
                scatter_wait(1 - s)


def _moe(blk_exp, n_used, first, nxt, wslot, inv, h2p, w1, w3, w2):
    T = h2p.shape[0]
    tr = TR_MOE
    n_tiles = blk_exp.shape[0]
    any_spec = pl.BlockSpec(memory_space=pl.ANY)
    return pl.pallas_call(
        _moe_kernel,
        grid_spec=pltpu.PrefetchScalarGridSpec(
            num_scalar_prefetch=6,
            grid=(n_tiles,),
            in_specs=[any_spec] * 4,
            out_specs=any_spec,
            scratch_shapes=[pltpu.VMEM((T, PACKED), U32),
                            pltpu.VMEM((tr, PACKED), U32),
                            pltpu.VMEM((tr, PACKED), U32),
                            pltpu.VMEM((tr, EXPERT_HIDDEN), F32),
                            pltpu.VMEM((tr, EXPERT_HIDDEN), F32),
                            pltpu.VMEM((tr, EXPERT_HIDDEN), F32),
                            pltpu.VMEM((tr, EXPERT_HIDDEN), F32),
                            pltpu.VMEM((2 * tr * ROW_SUB, LANES), F32),
                            pltpu.VMEM((2, D_MODEL, EXPERT_HIDDEN), F32),
                            pltpu.VMEM((2, D_MODEL, EXPERT_HIDDEN), F32),
                            pltpu.VMEM((2, EXPERT_HIDDEN, D_MODEL), F32),
                            pltpu.VMEM((D_MODEL, EXPERT_HIDDEN), BF16),
                            pltpu.VMEM((D_MODEL, EXPERT_HIDDEN), BF16),
                            pltpu.VMEM((EXPERT_HIDDEN, D_MODEL), BF16),
                            pltpu.SemaphoreType.DMA(()),
                            pltpu.SemaphoreType.DMA((2,)),
                            pltpu.SemaphoreType.DMA((2,))]),
        out_shape=jax.ShapeDtypeStruct(((2 * T + 2 * tr) * ROW_SUB, LANES), F32),
        compiler_params=pltpu.CompilerParams(dimension_semantics=("arbitrary",), vmem_limit_bytes=MOE_VMEM_LIMIT),
        name="moe_experts",
    )(blk_exp, n_used, first, nxt, wslot, inv, h2p, w1, w3, w2)


def _combine_kernel(x2_ref, rinfo_ref, y1_ref, y2_ref, o_ref):
    g1 = rinfo_ref[:, 4:5]
    g2 = rinfo_ref[:, 5:6]
    tc = x2_ref.shape[0]
    for c in range(ROW_SUB):
        cols = slice(c * LANES, (c + 1) * LANES)
        rows = pl.ds(c, tc, stride=ROW_SUB)
        o_ref[:, cols] = x2_ref[:, cols] + (g1 * y1_ref[rows, :] + g2 * y2_ref[rows, :])


def _combine(x2, rinfo, y2slot):
    T = x2.shape[0]
    tc = TC_COMBINE
    return pl.pallas_call(
        _combine_kernel,
        grid=(T // tc,),
        in_specs=[pl.BlockSpec((tc, D_MODEL), lambda i: (i, 0)),
                  pl.BlockSpec((tc, LANES), lambda i: (i, 0)),
                  pl.BlockSpec((tc * ROW_SUB, LANES), lambda i: (i, 0)),
                  pl.BlockSpec((tc * ROW_SUB, LANES), lambda i: (T // tc + i, 0))],
        out_specs=pl.BlockSpec((tc, D_MODEL), lambda i: (i, 0)),
        out_shape=jax.ShapeDtypeStruct((T, D_MODEL), F32),
        compiler_params=_cparams(("parallel",)),
        name="combine",
    )(x2, rinfo, y2slot, y2slot)


def _layer(x, mem, attn_norm, w_in, q_norm, k_norm, pool_proj, pool_scale, mem_norm, w_mem_kv,
           mq_norm, mk_norm, w_out, ffn_norm, w_group, b_group, w_router, b_router, w1, w3, w2):
    B, S, D = x.shape
    T = B * S
    assert D == D_MODEL and S % SUPER == 0 and T % TM_PROJ == 0
    x2d = x.reshape(T, D)
    row = lambda v: v.reshape(1, -1).astype(F32)
    scale = HEAD_DIM ** -0.5
    gq = row(jnp.tile(q_norm, ATTN_WIDTH // HEAD_DIM) * scale)
    gk = row(jnp.tile(k_norm, ATTN_WIDTH // HEAD_DIM))
    gmq = row(jnp.tile(mq_norm, MEM_WIDTH // HEAD_DIM) * scale)
    gmk = row(jnp.tile(mk_norm, MEM_WIDTH // HEAD_DIM))

    (q1, k1, v1, q4, k4, v4, q16, k16, v16, u, qm) = _proj(
        x2d, row(attn_norm), w_in.astype(BF16), gq, gk, gmq, B, S)
    ya = _attn(q1, k1, v1, q4, k4, v4, q16, k16, v16, B, S).reshape(T, ATTN_WIDTH)
    km, vm = _memkv(mem, row(mem_norm), w_mem_kv.astype(BF16), gmk, B)

    pool_bd = jax.scipy.linalg.block_diag(*[pool_proj[g] for g in range(pool_proj.shape[0])]).astype(BF16)
    w_r = jnp.zeros((D, LANES), F32)
    w_r = w_r.at[:, :N_GROUPS].set(w_group)
    w_r = w_r.at[:, EXPERT_LANE0:].set(jnp.transpose(w_router, (1, 0, 2)).reshape(D, N_EXPERTS))
    b_r = jnp.zeros((1, LANES), F32).at[0, :N_GROUPS].set(b_group).at[0, EXPERT_LANE0:].set(b_router.reshape(-1))
    x2, h2p, rinfo, rinfot, counts = _mix(ya, u, qm, km, vm, x2d, pool_bd, row(pool_scale),
                                          w_out.astype(BF16), row(ffn_norm), w_r.astype(BF16), b_r, B, S)

    R, pstart, sched = _tile_schedule(counts, T)
    dest = _dest(rinfot, pstart.astype(F32).reshape(N_EXPERTS, 1), T)
    inv = _rowmap(dest[0], R)
    y2slot = _moe(*sched, inv, h2p, w1, w3, w2)
    out = _combine(x2, rinfo, y2slot)
    return out.reshape(B, S, D)


def _tile_schedule(counts, T):
    tr = TR_MOE
    R = 2 * T + N_EXPERTS * tr
    n_tiles = R // tr
    cnt = counts[:, 0].astype(I32)
    padded = ((cnt + tr - 1) // tr) * tr
    pend = jnp.cumsum(padded)
    pstart = pend - padded
    n_used = (pend[-1] // tr).astype(I32).reshape(1)
    tiles = jnp.arange(n_tiles, dtype=I32)
    tile_row = jnp.minimum(tiles, n_used[0] - 1) * tr
    blk_exp = jnp.minimum(jnp.sum(tile_row[:, None] >= pend[None, :], axis=1), N_EXPERTS - 1).astype(I32)
    in_use = tiles < n_used[0]
    first = in_use & ((tiles == 0) | (blk_exp != jnp.roll(blk_exp, 1)))
    run_idx = jnp.cumsum(first.astype(I32)) - 1
    used = cnt > 0
    exp_of_run = jnp.argsort(jnp.logical_not(used), stable=True).astype(I32)
    nxt = jnp.where(run_idx + 1 < jnp.sum(used), exp_of_run[jnp.clip(run_idx + 1, 0, N_EXPERTS - 1)], -1)
    sched = (blk_exp, n_used, first.astype(I32), nxt.astype(I32), (run_idx & 1).astype(I32))
    return R, pstart, sched


def kernel(x, mem, attn_norm, w_in, q_norm, k_norm, pool_proj, pool_scale, mem_norm, w_mem_kv, mq_norm, mk_norm,
           w_out, ffn_norm, w_group, b_group, w_router, b_router, w1, w3, w2):
    for l in range(attn_norm.shape[0]):
        x = _layer(x, mem, attn_norm[l], w_in[l], q_norm[l], k_norm[l], pool_proj[l], pool_scale[l],
                   mem_norm[l], w_mem_kv[l], mq_norm[l], mk_norm[l], w_out[l], ffn_norm[l],
                   w_group[l], b_group[l], w_router[l], b_router[l], w1[l], w3[l], w2[l])
    return x
```

```python
import functools

import jax
import jax.numpy as jnp
from jax import lax
from jax.experimental import pallas as pl
from jax.experimental.pallas import tpu as pltpu

F32 = jnp.float32
BF16 = jnp.bfloat16
I32 = jnp.int32
U32 = jnp.uint32

D_MODEL = 1024
HEAD_DIM = 64
ATTN_WIDTH = 512
POOL_WIDTH = 256
MEM_WIDTH = 256
N_MEM = 256
IN_WIDTH = 3 * ATTN_WIDTH + POOL_WIDTH + MEM_WIDTH
N_GROUPS = 8
EXPERTS_PER_GROUP = 8
N_EXPERTS = 64
EXPERT_HIDDEN = 512
EPS = 1e-6
NEG_INF = -1e30

LANES = 128
N_PAIRS = ATTN_WIDTH // LANES
BLOCK = 128
SUPER = 16 * BLOCK
HALO = 16
ROW_SUB = D_MODEL // LANES
PACKED = D_MODEL // 2

TM_PROJ = 1024
TM_MIX = 1024
TR_MOE = 256
TC_COMBINE = 1024
EXPERT_LANE0 = 64

VMEM_LIMIT = 48 * 1024 * 1024
MOE_VMEM_LIMIT = 56 * 1024 * 1024


def _cparams(sem):
    return pltpu.CompilerParams(dimension_semantics=sem, vmem_limit_bytes=VMEM_LIMIT)


def _group_ones(n):
    g = jnp.arange(n) // HEAD_DIM
    return (g[:, None] == g[None, :]).astype(BF16)


def _head_norm(z, ones_ref, gain_ref):
    ss = jnp.dot((z * z).astype(BF16), ones_ref[...], preferred_element_type=F32)
    return z * lax.rsqrt(ss * (1.0 / HEAD_DIM) + EPS) * gain_ref[...]


def _proj_kernel(x_ref, an_ref, win_ref, gq_ref, gk_ref, gm_ref, o512_ref, o256_ref,
                 q1_ref, k1_ref, v1_ref, q4_ref, k4_ref, v4_ref, q16_ref, k16_ref, v16_ref,
                 u_ref, qm_ref, zs_ref, z4_ref):
    tm = x_ref.shape[0]
    x = x_ref[...]
    ms = jnp.mean(x * x, axis=-1, keepdims=True)
    h = (x * lax.rsqrt(ms + EPS) * an_ref[...]).astype(BF16)
    a0, a1, a2, a3 = ATTN_WIDTH, 2 * ATTN_WIDTH, 3 * ATTN_WIDTH, 3 * ATTN_WIDTH + POOL_WIDTH

    def cols(lo, hi):
        return jnp.dot(h, win_ref[:, lo:hi], preferred_element_type=F32)

    groups = ((lambda: _head_norm(cols(0, a0), o512_ref, gq_ref), q1_ref, q4_ref, q16_ref),
              (lambda: _head_norm(cols(a0, a1), o512_ref, gk_ref), k1_ref, k4_ref, k16_ref),
              (lambda: cols(a1, a2), v1_ref, v4_ref, v16_ref))
    for make, o1, o4, o16 in groups:
        val = make()
        for hp in range(N_PAIRS):
            pair = val[:, hp * LANES:(hp + 1) * LANES]
            zs_ref[hp] = pair
            o1[hp] = pair.astype(BF16)
        q4n = tm // 4
        for hp in range(N_PAIRS):
            for r4 in range(4):
                rows = zs_ref[hp, pl.ds(r4, q4n, stride=4), :]
                z4_ref[hp, pl.ds(r4 * q4n, q4n), :] = rows
                o4[hp, :, r4 * LANES:(r4 + 1) * LANES] = rows.astype(BF16)
            for r16 in range(16):
                r4, a = r16 % 4, r16 // 4
                rows = z4_ref[hp, pl.ds(r4 * q4n + a, tm // 16, stride=4), :]
                o16[hp, :, r16 * LANES:(r16 + 1) * LANES] = rows.astype(BF16)
    u_ref[...] = cols(a2, a3)
    qm_ref[...] = _head_norm(cols(a3, IN_WIDTH), o256_ref, gm_ref).astype(BF16)


def _proj(x2d, attn_norm, w_in, gq, gk, gm, B, S):
    T = B * S
    tm = TM_PROJ
    nj = S // tm
    const = lambda i: (0, 0)

    def lay(d):
        return jax.ShapeDtypeStruct((B, N_PAIRS, S // d, d * LANES), BF16)

    def lay_spec(d):
        return pl.BlockSpec((None, N_PAIRS, tm // d, d * LANES), lambda i: (i // nj, 0, i % nj, 0))

    out_shape = [lay(1)] * 3 + [lay(4)] * 3 + [lay(16)] * 3 + [
        jax.ShapeDtypeStruct((T, POOL_WIDTH), F32), jax.ShapeDtypeStruct((T, MEM_WIDTH), BF16)]
    out_specs = [lay_spec(1)] * 3 + [lay_spec(4)] * 3 + [lay_spec(16)] * 3 + [
        pl.BlockSpec((tm, POOL_WIDTH), lambda i: (i, 0)), pl.BlockSpec((tm, MEM_WIDTH), lambda i: (i, 0))]
    return pl.pallas_call(
        _proj_kernel,
        grid=(T // tm,),
        in_specs=[pl.BlockSpec((tm, D_MODEL), lambda i: (i, 0)),
                  pl.BlockSpec((1, D_MODEL), const),
                  pl.BlockSpec((D_MODEL, IN_WIDTH), const),
                  pl.BlockSpec((1, ATTN_WIDTH), const),
                  pl.BlockSpec((1, ATTN_WIDTH), const),
                  pl.BlockSpec((1, MEM_WIDTH), const),
                  pl.BlockSpec((ATTN_WIDTH, ATTN_WIDTH), const),
                  pl.BlockSpec((MEM_WIDTH, MEM_WIDTH), const)],
        out_specs=out_specs,
        out_shape=out_shape,
        scratch_shapes=[pltpu.VMEM((N_PAIRS, tm, LANES), F32)] * 2,
        compiler_params=_cparams(("parallel",)),
        name="proj",
    )(x2d, attn_norm, w_in, gq, gk, gm, _group_ones(ATTN_WIDTH), _group_ones(MEM_WIDTH))


def _attn_kernel(q1, k1c, k1p, v1c, v1p, q4, k4c, k4p, v4c, v4p, q16, k16c, k16p, v16c, v16p,
                 bias_ref, o_ref, obuf, mbuf, lbuf):
    c = pl.program_id(2)
    lane = lax.broadcasted_iota(I32, (BLOCK, LANES), 1)
    is_a = lane < HEAD_DIM
    lane_row = lax.broadcasted_iota(I32, (1, LANES), 1)
    mask_a = jnp.where(lane_row < HEAD_DIM, 1.0, 0.0).astype(BF16)
    mask_b = jnp.where(lane_row < HEAD_DIM, 0.0, 1.0).astype(BF16)
    bias_full = bias_ref[0]
    bias_first = jnp.where(c > 0, bias_full, bias_ref[1])

    def tile(q_t, kp_t, kc_t, vp_t, vc_t, bias):
        lhs = jnp.concatenate([q_t * mask_a, q_t * mask_b], axis=0)
        keys = jnp.concatenate([kp_t, kc_t], axis=0)
        s = lax.dot_general(lhs, keys, (((1,), (1,)), ((), ())), preferred_element_type=F32) + bias
        m = jnp.max(s, axis=-1, keepdims=True)
        p = jnp.exp(s - m)
        l = jnp.sum(p, axis=-1, keepdims=True)
        vals = jnp.concatenate([vp_t, vc_t], axis=0)
        pv = jnp.dot(p.astype(BF16), vals, preferred_element_type=F32)
        return (jnp.where(is_a, pv[:BLOCK], pv[BLOCK:]), jnp.where(is_a, m[:BLOCK], m[BLOCK:]),
                jnp.where(is_a, l[:BLOCK], l[BLOCK:]))

    for pat, (d, q, kc, kp, vc, vp) in enumerate(((16, q16, k16c, k16p, v16c, v16p), (4, q4, k4c, k4p, v4c, v4p))):
        nblk = SUPER // (BLOCK * d)
        for r in range(d):
            cols = pl.ds(r * LANES, LANES)
            for jb in range(nblk):
                cur = pl.ds(jb * BLOCK, BLOCK)
                if jb == 0:
                    kp_t, vp_t, bias = kp[:, cols], vp[:, cols], bias_first
                else:
                    prev = pl.ds((jb - 1) * BLOCK, BLOCK)
                    kp_t, vp_t, bias = kc[prev, cols], vc[prev, cols], bias_full
                o_t, m_t, l_t = tile(q[cur, cols], kp_t, kc[cur, cols], vp_t, vc[cur, cols], bias)
                rows = pl.ds(jb * BLOCK * d + r, BLOCK, stride=d)
                obuf[pat, rows, :] = o_t
                mbuf[pat, rows, :] = m_t
                lbuf[pat, rows, :] = l_t
    for jb in range(SUPER // BLOCK):
        cur = pl.ds(jb * BLOCK, BLOCK)
        if jb == 0:
            kp_t, vp_t, bias = k1p[...], v1p[...], bias_first
        else:
            prev = pl.ds((jb - 1) * BLOCK, BLOCK)
            kp_t, vp_t, bias = k1c[prev, :], v1c[prev, :], bias_full
        o0, m0, l0 = tile(q1[cur, :], kp_t, k1c[cur, :], vp_t, v1c[cur, :], bias)
        m1, m2 = mbuf[0, cur, :], mbuf[1, cur, :]
        top = jnp.maximum(jnp.maximum(m0, m1), m2)
        w0, w1, w2 = jnp.exp(m0 - top), jnp.exp(m1 - top), jnp.exp(m2 - top)
        num = w0 * o0 + w1 * obuf[0, cur, :] + w2 * obuf[1, cur, :]
        den = w0 * l0 + w1 * lbuf[0, cur, :] + w2 * lbuf[1, cur, :]
        o_ref[cur, :] = (num * (1.0 / den)).astype(BF16)


def _band_bias():
    qi = jnp.arange(BLOCK)[:, None]
    kj = jnp.arange(2 * BLOCK)[None, :]
    dist = qi + BLOCK - kj
    in_band = (dist >= 0) & (dist <= BLOCK)
    full = jnp.where(in_band, 0.0, NEG_INF).astype(F32)
    first = jnp.where(in_band & (kj >= BLOCK), 0.0, NEG_INF).astype(F32)
    return jnp.stack([jnp.tile(full, (2, 1)), jnp.tile(first, (2, 1))])


def _attn(q1, k1, v1, q4, k4, v4, q16, k16, v16, B, S):
    nsup = S // SUPER

    def specs(d):
        rows = SUPER // d
        per = rows // BLOCK
        cur = pl.BlockSpec((None, None, rows, d * LANES), lambda b, hp, c: (b, hp, c, 0))
        prev = pl.BlockSpec((None, None, BLOCK, d * LANES),
                            lambda b, hp, c: (b, hp, jnp.maximum(per * c - 1, 0), 0))
        return [cur, cur, prev, cur, prev]

    return pl.pallas_call(
        _attn_kernel,
        grid=(B, N_PAIRS, nsup),
        in_specs=specs(1) + specs(4) + specs(16) + [
            pl.BlockSpec((2, 2 * BLOCK, 2 * BLOCK), lambda b, hp, c: (0, 0, 0))],
        out_specs=pl.BlockSpec((None, SUPER, LANES), lambda b, hp, c: (b, c, hp)),
        out_shape=jax.ShapeDtypeStruct((B, S, ATTN_WIDTH), BF16),
        scratch_shapes=[pltpu.VMEM((2, SUPER, LANES), F32)] * 3,
        compiler_params=_cparams(("parallel", "parallel", "parallel")),
        name="dilated_attn",
    )(q1, k1, k1, v1, v1, q4, k4, k4, v4, v4, q16, k16, k16, v16, v16, _band_bias())


def _memkv_kernel(mem_ref, mn_ref, wkv_ref, gk_ref, o256_ref, km_ref, vm_ref):
    m = mem_ref[...]
    ms = jnp.mean(m * m, axis=-1, keepdims=True)
    mn = (m * lax.rsqrt(ms + EPS) * mn_ref[...]).astype(BF16)
    kv = jnp.dot(mn, wkv_ref[...], preferred_element_type=F32)
    km_ref[...] = _head_norm(kv[:, :MEM_WIDTH], o256_ref, gk_ref).astype(BF16)
    vm_ref[...] = kv[:, MEM_WIDTH:].astype(BF16)


def _memkv(mem, mem_norm, w_mem_kv, gmk, B):
    const = lambda b: (0, 0)
    return pl.pallas_call(
        _memkv_kernel,
        grid=(B,),
        in_specs=[pl.BlockSpec((None, N_MEM, D_MODEL), lambda b: (b, 0, 0)),
                  pl.BlockSpec((1, D_MODEL), const),
                  pl.BlockSpec((D_MODEL, 2 * MEM_WIDTH), const),
                  pl.BlockSpec((1, MEM_WIDTH), const),
                  pl.BlockSpec((MEM_WIDTH, MEM_WIDTH), const)],
        out_specs=[pl.BlockSpec((None, N_MEM, MEM_WIDTH), lambda b: (b, 0, 0))] * 2,
        out_shape=[jax.ShapeDtypeStruct((B, N_MEM, MEM_WIDTH), BF16)] * 2,
        compiler_params=_cparams(("parallel",)),
        name="memkv",
    )(mem, mem_norm, w_mem_kv, gmk, _group_ones(MEM_WIDTH))


def _mix_kernel(tiles_per_seq, ya_ref, u_ref, uh_ref, qm_ref, km_ref, vm_ref, x_ref, pp_ref, ps_ref,
                wo_ref, fn_ref, wr_ref, br_ref, utri_ref,
                x2_ref, h2p_ref, rinfo_ref, rinfot_ref, cnt_ref, base_ref):
    i = pl.program_id(0)
    tm = x_ref.shape[0]
    seq_tile = i % tiles_per_seq

    @pl.when(i == 0)
    def _():
        base_ref[...] = jnp.zeros_like(base_ref)

    u = u_ref[...]
    halo = jnp.where(seq_tile == 0, 0.0, uh_ref[...])
    uu = jnp.concatenate([halo, u], axis=0)
    a1 = uu[1:] + uu[:-1]
    a2 = a1[2:] + a1[:-2]
    a3 = a2[4:] + a2[:-4]
    a4 = a3[8:] + a3[:-8]
    lane_p = lax.broadcasted_iota(I32, (tm, POOL_WIDTH), 1)
    g0, g1, g2 = lane_p < 64, lane_p < 128, lane_p < 192
    wsum = jnp.where(g0, a1[15:], jnp.where(g1, a2[13:], jnp.where(g2, a3[9:], a4[1:])))
    wlen = jnp.where(g0, 2.0, jnp.where(g1, 4.0, jnp.where(g2, 8.0, 16.0)))
    tpos = seq_tile * tm + lax.broadcasted_iota(I32, (tm, POOL_WIDTH), 0) + 1
    cnt = jnp.minimum(tpos.astype(F32), wlen)
    pooled = wsum / cnt - u
    y_pool = jnp.dot(pooled.astype(BF16), pp_ref[...], preferred_element_type=F32) * ps_ref[...]

    lane = lax.broadcasted_iota(I32, (tm, LANES), 1)
    is_a = lane < HEAD_DIM
    lane_row = lax.broadcasted_iota(I32, (1, LANES), 1)
    mask_a = jnp.where(lane_row < HEAD_DIM, 1.0, 0.0).astype(BF16)
    mask_b = jnp.where(lane_row < HEAD_DIM, 0.0, 1.0).astype(BF16)
    y_mem = []
    for pr in range(MEM_WIDTH // LANES):
        cols = slice(pr * LANES, (pr + 1) * LANES)
        qp, kp, vp = qm_ref[:, cols], km_ref[:, cols], vm_ref[:, cols]
        outs = []
        for msk in (mask_a, mask_b):
            s = lax.dot_general(qp * msk, kp, (((1,), (1,)), ((), ())), preferred_element_type=F32)
            m = jnp.max(s, axis=-1, keepdims=True)
            p = jnp.exp(s - m)
            p = p / jnp.sum(p, axis=-1, keepdims=True)
            outs.append(jnp.dot(p.astype(BF16), vp, preferred_element_type=F32))
        y_mem.append(jnp.where(is_a, outs[0], outs[1]))

    rest = jnp.concatenate([y_pool] + y_mem, axis=1).astype(BF16)
    proj = jnp.dot(ya_ref[...], wo_ref[:ATTN_WIDTH, :], preferred_element_type=F32)
    proj += jnp.dot(rest, wo_ref[ATTN_WIDTH:, :], preferred_element_type=F32)
    x2 = x_ref[...] + proj
    x2_ref[...] = x2

    ms = jnp.mean(x2 * x2, axis=-1, keepdims=True)
    h2 = (x2 * lax.rsqrt(ms + EPS) * fn_ref[...]).astype(BF16)
    bits = lax.bitcast_convert_type(h2.astype(F32), U32)
    h2p_ref[...] = bits[:, :PACKED] | (bits[:, PACKED:] >> 16)

    logits = jnp.dot(h2, wr_ref[...], preferred_element_type=F32) + br_ref[...]
    lt = jnp.transpose(logits)
    ninf = -jnp.inf
    sub8 = lax.broadcasted_iota(I32, (EXPERTS_PER_GROUP, tm), 0).astype(F32)
    lg = lt[:N_GROUPS]
    mg = jnp.max(lg, axis=0, keepdims=True)
    g_sel = jnp.min(jnp.where(lg == mg, sub8, float(N_GROUPS)), axis=0, keepdims=True)
    w_g = 1.0 / jnp.sum(jnp.exp(lg - mg), axis=0, keepdims=True)
    le = lt[EXPERT_LANE0:EXPERT_LANE0 + EXPERTS_PER_GROUP]
    for g in range(1, N_GROUPS):
        lo_g = EXPERT_LANE0 + g * EXPERTS_PER_GROUP
        le = jnp.where(g_sel == float(g), lt[lo_g:lo_g + EXPERTS_PER_GROUP], le)
    v1 = jnp.max(le, axis=0, keepdims=True)
    i1 = jnp.min(jnp.where(le == v1, sub8, float(EXPERTS_PER_GROUP)), axis=0, keepdims=True)
    le2 = jnp.where(sub8 == i1, ninf, le)
    v2 = jnp.max(le2, axis=0, keepdims=True)
    i2 = jnp.min(jnp.where(le2 == v2, sub8, float(EXPERTS_PER_GROUP)), axis=0, keepdims=True)
    e21 = jnp.exp(v2 - v1)
    gate1 = w_g / (1.0 + e21)
    gate2 = w_g * e21 / (1.0 + e21)
    e1 = g_sel * float(EXPERTS_PER_GROUP) + i1
    e2 = g_sel * float(EXPERTS_PER_GROUP) + i2

    sub64 = lax.broadcasted_iota(I32, (N_EXPERTS, tm), 0).astype(F32)
    hot1 = sub64 == e1
    hot2 = sub64 == e2
    onehot = jnp.where(hot1 | hot2, 1.0, 0.0)
    prefix = jnp.dot(onehot.astype(BF16), utri_ref[...], preferred_element_type=F32)
    tot = prefix + base_ref[...]
    rank1 = jnp.sum(jnp.where(hot1, tot, 0.0), axis=0, keepdims=True)
    rank2 = jnp.sum(jnp.where(hot2, tot, 0.0), axis=0, keepdims=True)
    base_new = base_ref[...] + jnp.sum(onehot, axis=1, keepdims=True)
    base_ref[...] = base_new
    cnt_ref[...] = base_new

    rt = jnp.where(sub8 == 0.0, e1, jnp.where(sub8 == 1.0, e2, jnp.where(sub8 == 2.0, rank1, jnp.where(
        sub8 == 3.0, rank2, jnp.where(sub8 == 4.0, gate1, jnp.where(sub8 == 5.0, gate2, 0.0))))))
    rinfot_ref[...] = rt
    rinfo_ref[...] = jnp.transpose(jnp.concatenate([rt, jnp.zeros((LANES - 8, tm), F32)], axis=0))


def _mix(ya, u, qm, km, vm, x2d, pool_bd, pool_scale, w_out, ffn_norm, w_r, b_r, B, S):
    T = B * S
    tm = TM_MIX
    tps = S // tm
    hb = tm // HALO
    const = lambda i: (0, 0)
    utri = (jnp.arange(tm)[:, None] < jnp.arange(tm)[None, :]).astype(BF16)
    return pl.pallas_call(
        functools.partial(_mix_kernel, tps),
        grid=(T // tm,),
        in_specs=[pl.BlockSpec((tm, ATTN_WIDTH), lambda i: (i, 0)),
                  pl.BlockSpec((tm, POOL_WIDTH), lambda i: (i, 0)),
                  pl.BlockSpec((HALO, POOL_WIDTH), lambda i: (jnp.maximum(i * hb - 1, 0), 0)),
                  pl.BlockSpec((tm, MEM_WIDTH), lambda i: (i, 0)),
                  pl.BlockSpec((None, N_MEM, MEM_WIDTH), lambda i: (i // tps, 0, 0)),
                  pl.BlockSpec((None, N_MEM, MEM_WIDTH), lambda i: (i // tps, 0, 0)),
                  pl.BlockSpec((tm, D_MODEL), lambda i: (i, 0)),
                  pl.BlockSpec((POOL_WIDTH, POOL_WIDTH), const),
                  pl.BlockSpec((1, POOL_WIDTH), const),
                  pl.BlockSpec((D_MODEL, D_MODEL), const),
                  pl.BlockSpec((1, D_MODEL), const),
                  pl.BlockSpec((D_MODEL, LANES), const),
                  pl.BlockSpec((1, LANES), const),
                  pl.BlockSpec((tm, tm), const)],
        out_specs=[pl.BlockSpec((tm, D_MODEL), lambda i: (i, 0)),
                   pl.BlockSpec((tm, PACKED), lambda i: (i, 0)),
                   pl.BlockSpec((tm, LANES), lambda i: (i, 0)),
                   pl.BlockSpec((8, tm), lambda i: (0, i)),
                   pl.BlockSpec((N_EXPERTS, 1), const)],
        out_shape=[jax.ShapeDtypeStruct((T, D_MODEL), F32),
                   jax.ShapeDtypeStruct((T, PACKED), U32),
                   jax.ShapeDtypeStruct((T, LANES), F32),
                   jax.ShapeDtypeStruct((8, T), F32),
                   jax.ShapeDtypeStruct((N_EXPERTS, 1), F32)],
        scratch_shapes=[pltpu.VMEM((N_EXPERTS, 1), F32)],
        compiler_params=_cparams(("arbitrary",)),
        name="mix_router",
    )(ya, u, u, qm, km, vm, x2d, pool_bd, pool_scale, w_out, ffn_norm, w_r, b_r, utri)


def _dest_kernel(rt_ref, ps_ref, d_ref):
    tn = rt_ref.shape[1]
    sub = lax.broadcasted_iota(I32, (N_EXPERTS, tn), 0).astype(F32)
    ps = ps_ref[...]
    rows = []
    for k in range(2):
        e = rt_ref[k:k + 1, :]
        start = jnp.sum(jnp.where(sub == e, ps, 0.0), axis=0, keepdims=True)
        rows.append(start + rt_ref[2 + k:3 + k, :])
    word = rows[0].astype(I32) | (rows[1].astype(I32) << 16)
    d_ref[...] = jnp.concatenate([word, jnp.zeros((7, tn), I32)], axis=0)


def _dest(rinfot, pstart_col, T):
    tn = 2048
    return pl.pallas_call(
        _dest_kernel,
        grid=(T // tn,),
        in_specs=[pl.BlockSpec((8, tn), lambda i: (0, i)),
                  pl.BlockSpec((N_EXPERTS, 1), lambda i: (0, 0))],
        out_specs=pl.BlockSpec((8, tn), lambda i: (0, i)),
        out_shape=jax.ShapeDtypeStruct((8, T), I32),
        compiler_params=_cparams(("parallel",)),
        name="dest_rows",
    )(rinfot, pstart_col)


def _rowmap_kernel(d_ref, fill_hbm, inv_ref, sem):
    T = d_ref.shape[0]
    cp = pltpu.make_async_copy(fill_hbm, inv_ref, sem)
    cp.start()
    cp.wait()
    unroll = 8

    def body(g, carry):
        t0 = g * unroll
        words = [d_ref[t0 + u] for u in range(unroll)]
        for u, w in enumerate(words):
            inv_ref[w & 0xFFFF] = t0 + u
            inv_ref[lax.shift_right_logical(w, 16)] = T + t0 + u
        return carry

    lax.fori_loop(0, T // unroll, body, 0)


def _rowmap(dword, R):
    assert R <= 1 << 16
    T = dword.shape[0]
    fill = 2 * T + jnp.arange(R, dtype=I32) % (2 * TR_MOE)
    return pl.pallas_call(
        _rowmap_kernel,
        grid_spec=pltpu.PrefetchScalarGridSpec(
            num_scalar_prefetch=1,
            grid=(1,),
            in_specs=[pl.BlockSpec(memory_space=pl.ANY)],
            out_specs=pl.BlockSpec(memory_space=pltpu.SMEM),
            scratch_shapes=[pltpu.SemaphoreType.DMA(())]),
        out_shape=jax.ShapeDtypeStruct((R,), I32),
        compiler_params=_cparams(("arbitrary",)),
        name="rowmap",
    )(dword, fill)


def _moe_kernel(be_ref, nu_ref, first_ref, nxt_ref, ws_ref, inv_ref,
                h_hbm, w1_hbm, w3_hbm, w2_hbm, o_hbm,
                tab, xbuf0, xbuf1, a0, b0, a1, b1, ybuf, w1s, w3s, w2s, w1b, w3b, w2b, tsem, ssem, wsem):
    j = pl.program_id(0)
    nu = nu_ref[0]
    T = h_hbm.shape[0]
    tr = TR_MOE
    n_tiles = inv_ref.shape[0] // tr
    xbuf = (xbuf0, xbuf1)
    abuf = ((a0, b0), (a1, b1))

    def tile_rows(ref, row, n=1):
        return ref.at[pl.ds(pl.multiple_of(row * ROW_SUB, ROW_SUB), n * ROW_SUB)]

    def gather_tile(tile, slot):
        for i in range(tr):
            v = inv_ref[tile * tr + i]
            if T & (T - 1) == 0:
                tok = v & (T - 1)
            else:
                tok = jnp.where(v >= 2 * T, 0, jnp.where(v >= T, v - T, v))
            xbuf[slot][pl.ds(i, 1), :] = tab[pl.ds(tok, 1), :]

    def scatter_row(tile, slot, i):
        v = inv_ref[tile * tr + i]
        return pltpu.make_async_copy(tile_rows(ybuf, slot * tr + i), tile_rows(o_hbm, v), ssem.at[slot])

    def scatter_wait(slot):
        pltpu.make_async_copy(tile_rows(ybuf, slot * tr, tr), tile_rows(o_hbm, 0, tr), ssem.at[slot]).wait()

    def weight_copies(e, slot):
        return [pltpu.make_async_copy(src.at[e], dst.at[slot], wsem.at[slot])
                for src, dst in ((w1_hbm, w1s), (w3_hbm, w3s), (w2_hbm, w2s))]

    def up_project(slot):
        w = xbuf[slot][...]
        lo = lax.bitcast_convert_type(w & jnp.uint32(0xFFFF0000), F32).astype(BF16)
        hi = lax.bitcast_convert_type(w << 16, F32).astype(BF16)
        for dst, wb in zip(abuf[slot], (w1b, w3b)):
            dst[...] = (jnp.dot(lo, wb[:PACKED, :], preferred_element_type=F32)
                        + jnp.dot(hi, wb[PACKED:, :], preferred_element_type=F32))

    @pl.when(j < nu)
    def _():
        s = j & 1
        jn = jnp.minimum(j + 1, n_tiles - 1)

        @pl.when(j == 0)
        def _():
            ybuf[...] = jnp.zeros_like(ybuf)
            scratch_rows = pltpu.make_async_copy(ybuf, tile_rows(o_hbm, 2 * T, 2 * tr), tsem)
            scratch_rows.start()
            scratch_rows.wait()
            table = pltpu.make_async_copy(h_hbm, tab, tsem)
            table.start()
            for cp in weight_copies(be_ref[0], 0):
                cp.start()
            table.wait()
            gather_tile(0, 0)
            gather_tile(jnp.minimum(1, n_tiles - 1), 1)
            for cp in weight_copies(be_ref[0], 0):
                cp.wait()

            @pl.when(nxt_ref[0] >= 0)
            def _():
                for cp in weight_copies(nxt_ref[0], 1):
                    cp.start()

            w1b[...] = w1s[0].astype(BF16)
            w3b[...] = w3s[0].astype(BF16)
            w2b[...] = w2s[0].astype(BF16)
            up_project(0)

        @pl.when((j >= 1) & (first_ref[j] == 1))
        def _():
            w2b[...] = w2s[ws_ref[j]].astype(BF16)

        @pl.when((j + 1 < nu) & (first_ref[jn] == 1))
        def _():
            wslot = ws_ref[jn]
            for cp in weight_copies(be_ref[jn], wslot):
                cp.wait()
            e_next = nxt_ref[jn]

            @pl.when(e_next >= 0)
            def _():
                for cp in weight_copies(e_next, 1 - wslot):
                    cp.start()

            w1b[...] = w1s[wslot].astype(BF16)
            w3b[...] = w3s[wslot].astype(BF16)

        @pl.when(j >= 2)
        def _():
            scatter_wait(s)

        for slot in range(2):
            @pl.when(s == slot)
            def _():
                gather_tile(jnp.minimum(j + 2, n_tiles - 1), slot)
                a = abuf[slot][0][...]
                b = abuf[slot][1][...]
                up_project(1 - slot)
                hmid = (a / (1.0 + jnp.exp(-a)) * b).astype(BF16)
                y = jnp.dot(hmid, w2b[...], preferred_element_type=F32)
                for c in range(ROW_SUB):
                    ybuf[pl.ds(slot * (tr * ROW_SUB) + c, tr, stride=ROW_SUB), :] = y[:, c * LANES:(c + 1) * LANES]
                for i in range(tr):
                    scatter_row(j, slot, i).start()

        @pl.when(j == nu - 1)
        def _():
            scatter_wait(s)

            @pl.when(j >= 1)
            def _():
                scatter_wait(1 - s)


def _moe(blk_exp, n_used, first, nxt, wslot, inv, h2p, w1, w3, w2):
    T = h2p.shape[0]
    tr = TR_MOE
    n_tiles = blk_exp.shape[0]
    any_spec = pl.BlockSpec(memory_space=pl.ANY)
    return pl.pallas_call(
        _moe_kernel,
        grid_spec=pltpu.PrefetchScalarGridSpec(
            num_scalar_prefetch=6,
            grid=(n_tiles,),
            in_specs=[any_spec] * 4,
            out_specs=any_spec,
            scratch_shapes=[pltpu.VMEM((T, PACKED), U32),
                            pltpu.VMEM((tr, PACKED), U32),
                            pltpu.VMEM((tr, PACKED), U32),
                            pltpu.VMEM((tr, EXPERT_HIDDEN), F32),
                            pltpu.VMEM((tr, EXPERT_HIDDEN), F32),
                            pltpu.VMEM((tr, EXPERT_HIDDEN), F32),
                            pltpu.VMEM((tr, EXPERT_HIDDEN), F32),
                            pltpu.VMEM((2 * tr * ROW_SUB, LANES), F32),
                            pltpu.VMEM((2, D_MODEL, EXPERT_HIDDEN), F32),
                            pltpu.VMEM((2, D_MODEL, EXPERT_HIDDEN), F32),
                            pltpu.VMEM((2, EXPERT_HIDDEN, D_MODEL), F32),
                            pltpu.VMEM((D_MODEL, EXPERT_HIDDEN), BF16),
                            pltpu.VMEM((D_MODEL, EXPERT_HIDDEN), BF16),
                            pltpu.VMEM((EXPERT_HIDDEN, D_MODEL), BF16),
                            pltpu.SemaphoreType.DMA(()),
                            pltpu.SemaphoreType.DMA((2,)),
                            pltpu.SemaphoreType.DMA((2,))]),
        out_shape=jax.ShapeDtypeStruct(((2 * T + 2 * tr) * ROW_SUB, LANES), F32),
        compiler_params=pltpu.CompilerParams(dimension_semantics=("arbitrary",), vmem_limit_bytes=MOE_VMEM_LIMIT),
        name="moe_experts",
    )(blk_exp, n_used, first, nxt, wslot, inv, h2p, w1, w3, w2)


def _combine_kernel(x2_ref, rinfo_ref, y1_ref, y2_ref, o_ref):
    g1 = rinfo_ref[:, 4:5]
    g2 = rinfo_ref[:, 5:6]
    tc = x2_ref.shape[0]
    for c in range(ROW_SUB):
        cols = slice(c * LANES, (c + 1) * LANES)
        rows = pl.ds(c, tc, stride=ROW_SUB)
        o_ref[:, cols] = x2_ref[:, cols] + (g1 * y1_ref[rows, :] + g2 * y2_ref[rows, :])


def _combine(x2, rinfo, y2slot):
    T = x2.shape[0]
    tc = TC_COMBINE
    return pl.pallas_call(
        _combine_kernel,
        grid=(T // tc,),
        in_specs=[pl.BlockSpec((tc, D_MODEL), lambda i: (i, 0)),
                  pl.BlockSpec((tc, LANES), lambda i: (i, 0)),
                  pl.BlockSpec((tc * ROW_SUB, LANES), lambda i: (i, 0)),
                  pl.BlockSpec((tc * ROW_SUB, LANES), lambda i: (T // tc + i, 0))],
        out_specs=pl.BlockSpec((tc, D_MODEL), lambda i: (i, 0)),
        out_shape=jax.ShapeDtypeStruct((T, D_MODEL), F32),
        compiler_params=_cparams(("parallel",)),
        name="combine",
    )(x2, rinfo, y2slot, y2slot)


def _layer(x, mem, attn_norm, w_in, q_norm, k_norm, pool_proj, pool_scale, mem_norm, w_mem_kv,
           mq_norm, mk_norm, w_out, ffn_norm, w_group, b_group, w_router, b_router, w1, w3, w2):
    B, S, D = x.shape
    T = B * S
    assert D == D_MODEL and S % SUPER == 0 and T % TM_PROJ == 0
    x2d = x.reshape(T, D)
    row = lambda v: v.reshape(1, -1).astype(F32)
    scale = HEAD_DIM ** -0.5
    gq = row(jnp.tile(q_norm, ATTN_WIDTH // HEAD_DIM) * scale)
    gk = row(jnp.tile(k_norm, ATTN_WIDTH // HEAD_DIM))
    gmq = row(jnp.tile(mq_norm, MEM_WIDTH // HEAD_DIM) * scale)
    gmk = row(jnp.tile(mk_norm, MEM_WIDTH // HEAD_DIM))

    (q1, k1, v1, q4, k4, v4, q16, k16, v16, u, qm) = _proj(
        x2d, row(attn_norm), w_in.astype(BF16), gq, gk, gmq, B, S)
    ya = _attn(q1, k1, v1, q4, k4, v4, q16, k16, v16, B, S).reshape(T, ATTN_WIDTH)
    km, vm = _memkv(mem, row(mem_norm), w_mem_kv.astype(BF16), gmk, B)

    pool_bd = jax.scipy.linalg.block_diag(*[pool_proj[g] for g in range(pool_proj.shape[0])]).astype(BF16)
    w_r = jnp.zeros((D, LANES), F32)
    w_r = w_r.at[:, :N_GROUPS].set(w_group)
    w_r = w_r.at[:, EXPERT_LANE0:].set(jnp.transpose(w_router, (1, 0, 2)).reshape(D, N_EXPERTS))
    b_r = jnp.zeros((1, LANES), F32).at[0, :N_GROUPS].set(b_group).at[0, EXPERT_LANE0:].set(b_router.reshape(-1))
    x2, h2p, rinfo, rinfot, counts = _mix(ya, u, qm, km, vm, x2d, pool_bd, row(pool_scale),
                                          w_out.astype(BF16), row(ffn_norm), w_r.astype(BF16), b_r, B, S)

    R, pstart, sched = _tile_schedule(counts, T)
    dest = _dest(rinfot, pstart.astype(F32).reshape(N_EXPERTS, 1), T)
    inv = _rowmap(dest[0], R)
    y2slot = _moe(*sched, inv, h2p, w1, w3, w2)
    out = _combine(x2, rinfo, y2slot)
    return out.reshape(B, S, D)


def _tile_schedule(counts, T):
    tr = TR_MOE
    R = 2 * T + N_EXPERTS * tr
    n_tiles = R // tr
    cnt = counts[:, 0].astype(I32)
    padded = ((cnt + tr - 1) // tr) * tr
    pend = jnp.cumsum(padded)
    pstart = pend - padded
    n_used = (pend[-1] // tr).astype(I32).reshape(1)
    tiles = jnp.arange(n_tiles, dtype=I32)
    tile_row = jnp.minimum(tiles, n_used[0] - 1) * tr
    blk_exp = jnp.minimum(jnp.sum(tile_row[:, None] >= pend[None, :], axis=1), N_EXPERTS - 1).astype(I32)
    in_use = tiles < n_used[0]
    first = in_use & ((tiles == 0) | (blk_exp != jnp.roll(blk_exp, 1)))
    run_idx = jnp.cumsum(first.astype(I32)) - 1
    used = cnt > 0
    exp_of_run = jnp.argsort(jnp.logical_not(used), stable=True).astype(I32)
    nxt = jnp.where(run_idx + 1 < jnp.sum(used), exp_of_run[jnp.clip(run_idx + 1, 0, N_EXPERTS - 1)], -1)
    sched = (blk_exp, n_used, first.astype(I32), nxt.astype(I32), (run_idx & 1).astype(I32))
    return R, pstart, sched


def kernel(x, mem, attn_norm, w_in, q_norm, k_norm, pool_proj, pool_scale, mem_norm, w_mem_kv, mq_norm, mk_norm,
           w_out, ffn_norm, w_group, b_group, w_router, b_router, w1, w3, w2):
    for l in range(attn_norm.shape[0]):
        x = _layer(x, mem, attn_norm[l], w_in[l], q_norm[l], k_norm[l], pool_proj[l], pool_scale[l],
                   mem_norm[l], w_mem_kv[l], mq_norm[l], mk_norm[l], w_out[l], ffn_norm[l],
                   w_group[l], b_group[l], w_router[l], b_router[l], w1[l], w3[l], w2[l])
    return x
```

```python
import functools

import jax
import jax.numpy as jnp
from jax import lax
from jax.experimental import pallas as pl
from jax.experimental.pallas import tpu as pltpu

F32 = jnp.float32
BF16 = jnp.bfloat16
I32 = jnp.int32
U32 = jnp.uint32

D_MODEL = 1024
HEAD_DIM = 64
ATTN_WIDTH = 512
POOL_WIDTH = 256
MEM_WIDTH = 256
N_MEM = 256
IN_WIDTH = 3 * ATTN_WIDTH + POOL_WIDTH + MEM_WIDTH
N_GROUPS = 8
EXPERTS_PER_GROUP = 8
N_EXPERTS = 64
EXPERT_HIDDEN = 512
EPS = 1e-6
NEG_INF = -1e30

LANES = 128
N_PAIRS = ATTN_WIDTH // LANES
BLOCK = 128
SUPER = 16 * BLOCK
HALO = 16
ROW_SUB = D_MODEL // LANES
PACKED = D_MODEL // 2

TM_PROJ = 1024
TM_MIX = 1024
TR_MOE = 256
TC_COMBINE = 1024
EXPERT_LANE0 = 64

VMEM_LIMIT = 48 * 1024 * 1024
MOE_VMEM_LIMIT = 56 * 1024 * 1024


def _cparams(sem):
    return pltpu.CompilerParams(dimension_semantics=sem, vmem_limit_bytes=VMEM_LIMIT)


def _group_ones(n):
    g = jnp.arange(n) // HEAD_DIM
    return (g[:, None] == g[None, :]).astype(BF16)


def _head_norm(z, ones_ref, gain_ref):
    ss = jnp.dot((z * z).astype(BF16), ones_ref[...], preferred_element_type=F32)
    return z * lax.rsqrt(ss * (1.0 / HEAD_DIM) + EPS) * gain_ref[...]


def _proj_kernel(x_ref, an_ref, win_ref, gq_ref, gk_ref, gm_ref, o512_ref, o256_ref,
                 q1_ref, k1_ref, v1_ref, q4_ref, k4_ref, v4_ref, q16_ref, k16_ref, v16_ref,
                 u_ref, qm_ref, zs_ref, z4_ref):
    tm = x_ref.shape[0]
    x = x_ref[...]
    ms = jnp.mean(x * x, axis=-1, keepdims=True)
    h = (x * lax.rsqrt(ms + EPS) * an_ref[...]).astype(BF16)
    a0, a1, a2, a3 = ATTN_WIDTH, 2 * ATTN_WIDTH, 3 * ATTN_WIDTH, 3 * ATTN_WIDTH + POOL_WIDTH

    def cols(lo, hi):
        return jnp.dot(h, win_ref[:, lo:hi], preferred_element_type=F32)

    groups = ((lambda: _head_norm(cols(0, a0), o512_ref, gq_ref), q1_ref, q4_ref, q16_ref),
              (lambda: _head_norm(cols(a0, a1), o512_ref, gk_ref), k1_ref, k4_ref, k16_ref),
              (lambda: cols(a1, a2), v1_ref, v4_ref, v16_ref))
    for make, o1, o4, o16 in groups:
        val = make()
        for hp in range(N_PAIRS):
            pair = val[:, hp * LANES:(hp + 1) * LANES]
            zs_ref[hp] = pair
            o1[hp] = pair.astype(BF16)
        q4n = tm // 4
        for hp in range(N_PAIRS):
            for r4 in range(4):
                rows = zs_ref[hp, pl.ds(r4, q4n, stride=4), :]
                z4_ref[hp, pl.ds(r4 * q4n, q4n), :] = rows
                o4[hp, :, r4 * LANES:(r4 + 1) * LANES] = rows.astype(BF16)
            for r16 in range(16):
                r4, a = r16 % 4, r16 // 4
                rows = z4_ref[hp, pl.ds(r4 * q4n + a, tm // 16, stride=4), :]
                o16[hp, :, r16 * LANES:(r16 + 1) * LANES] = rows.astype(BF16)
    u_ref[...] = cols(a2, a3)
    qm_ref[...] = _head_norm(cols(a3, IN_WIDTH), o256_ref, gm_ref).astype(BF16)


def _proj(x2d, attn_norm, w_in, gq, gk, gm, B, S):
    T = B * S
    tm = TM_PROJ
    nj = S // tm
    const = lambda i: (0, 0)

    def lay(d):
        return jax.ShapeDtypeStruct((B, N_PAIRS, S // d, d * LANES), BF16)

    def lay_spec(d):
        return pl.BlockSpec((None, N_PAIRS, tm // d, d * LANES), lambda i: (i // nj, 0, i % nj, 0))

    out_shape = [lay(1)] * 3 + [lay(4)] * 3 + [lay(16)] * 3 + [
        jax.ShapeDtypeStruct((T, POOL_WIDTH), F32), jax.ShapeDtypeStruct((T, MEM_WIDTH), BF16)]
    out_specs = [lay_spec(1)] * 3 + [lay_spec(4)] * 3 + [lay_spec(16)] * 3 + [
        pl.BlockSpec((tm, POOL_WIDTH), lambda i: (i, 0)), pl.BlockSpec((tm, MEM_WIDTH), lambda i: (i, 0))]
    return pl.pallas_call(
        _proj_kernel,
        grid=(T // tm,),
        in_specs=[pl.BlockSpec((tm, D_MODEL), lambda i: (i, 0)),
                  pl.BlockSpec((1, D_MODEL), const),
                  pl.BlockSpec((D_MODEL, IN_WIDTH), const),
                  pl.BlockSpec((1, ATTN_WIDTH), const),
                  pl.BlockSpec((1, ATTN_WIDTH), const),
                  pl.BlockSpec((1, MEM_WIDTH), const),
                  pl.BlockSpec((ATTN_WIDTH, ATTN_WIDTH), const),
                  pl.BlockSpec((MEM_WIDTH, MEM_WIDTH), const)],
        out_specs=out_specs,
        out_shape=out_shape,
        scratch_shapes=[pltpu.VMEM((N_PAIRS, tm, LANES), F32)] * 2,
        compiler_params=_cparams(("parallel",)),
        name="proj",
    )(x2d, attn_norm, w_in, gq, gk, gm, _group_ones(ATTN_WIDTH), _group_ones(MEM_WIDTH))


def _attn_kernel(q1, k1c, k1p, v1c, v1p, q4, k4c, k4p, v4c, v4p, q16, k16c, k16p, v16c, v16p,
                 bias_ref, o_ref, obuf, mbuf, lbuf):
    c = pl.program_id(2)
    lane = lax.broadcasted_iota(I32, (BLOCK, LANES), 1)
    is_a = lane < HEAD_DIM
    lane_row = lax.broadcasted_iota(I32, (1, LANES), 1)
    mask_a = jnp.where(lane_row < HEAD_DIM, 1.0, 0.0).astype(BF16)
    mask_b = jnp.where(lane_row < HEAD_DIM, 0.0, 1.0).astype(BF16)
    bias_full = bias_ref[0]
    bias_first = jnp.where(c > 0, bias_full, bias_ref[1])

    def tile(q_t, kp_t, kc_t, vp_t, vc_t, bias):
        lhs = jnp.concatenate([q_t * mask_a, q_t * mask_b], axis=0)
        keys = jnp.concatenate([kp_t, kc_t], axis=0)
        s = lax.dot_general(lhs, keys, (((1,), (1,)), ((), ())), preferred_element_type=F32) + bias
        m = jnp.max(s, axis=-1, keepdims=True)
        p = jnp.exp(s - m)
        l = jnp.sum(p, axis=-1, keepdims=True)
        vals = jnp.concatenate([vp_t, vc_t], axis=0)
        pv = jnp.dot(p.astype(BF16), vals, preferred_element_type=F32)
        return (jnp.where(is_a, pv[:BLOCK], pv[BLOCK:]), jnp.where(is_a, m[:BLOCK], m[BLOCK:]),
                jnp.where(is_a, l[:BLOCK], l[BLOCK:]))

    for pat, (d, q, kc, kp, vc, vp) in enumerate(((16, q16, k16c, k16p, v16c, v16p), (4, q4, k4c, k4p, v4c, v4p))):
        nblk = SUPER // (BLOCK * d)
        for r in range(d):
            cols = pl.ds(r * LANES, LANES)
            for jb in range(nblk):
                cur = pl.ds(jb * BLOCK, BLOCK)
                if jb == 0:
                    kp_t, vp_t, bias = kp[:, cols], vp[:, cols], bias_first
                else:
                    prev = pl.ds((jb - 1) * BLOCK, BLOCK)
                    kp_t, vp_t, bias = kc[prev, cols], vc[prev, cols], bias_full
                o_t, m_t, l_t = tile(q[cur, cols], kp_t, kc[cur, cols], vp_t, vc[cur, cols], bias)
                rows = pl.ds(jb * BLOCK * d + r, BLOCK, stride=d)
                obuf[pat, rows, :] = o_t
                mbuf[pat, rows, :] = m_t
                lbuf[pat, rows, :] = l_t
    for jb in range(SUPER // BLOCK):
        cur = pl.ds(jb * BLOCK, BLOCK)
        if jb == 0:
            kp_t, vp_t, bias = k1p[...], v1p[...], bias_first
        else:
            prev = pl.ds((jb - 1) * BLOCK, BLOCK)
            kp_t, vp_t, bias = k1c[prev, :], v1c[prev, :], bias_full
        o0, m0, l0 = tile(q1[cur, :], kp_t, k1c[cur, :], vp_t, v1c[cur, :], bias)
        m1, m2 = mbuf[0, cur, :], mbuf[1, cur, :]
        top = jnp.maximum(jnp.maximum(m0, m1), m2)
        w0, w1, w2 = jnp.exp(m0 - top), jnp.exp(m1 - top), jnp.exp(m2 - top)
        num = w0 * o0 + w1 * obuf[0, cur, :] + w2 * obuf[1, cur, :]
        den = w0 * l0 + w1 * lbuf[0, cur, :] + w2 * lbuf[1, cur, :]
        o_ref[cur, :] = (num * (1.0 / den)).astype(BF16)


def _band_bias():
    qi = jnp.arange(BLOCK)[:, None]
    kj = jnp.arange(2 * BLOCK)[None, :]
    dist = qi + BLOCK - kj
    in_band = (dist >= 0) & (dist <= BLOCK)
    full = jnp.where(in_band, 0.0, NEG_INF).astype(F32)
    first = jnp.where(in_band & (kj >= BLOCK), 0.0, NEG_INF).astype(F32)
    return jnp.stack([jnp.tile(full, (2, 1)), jnp.tile(first, (2, 1))])


def _attn(q1, k1, v1, q4, k4, v4, q16, k16, v16, B, S):
    nsup = S // SUPER

    def specs(d):
        rows = SUPER // d
        per = rows // BLOCK
        cur = pl.BlockSpec((None, None, rows, d * LANES), lambda b, hp, c: (b, hp, c, 0))
        prev = pl.BlockSpec((None, None, BLOCK, d * LANES),
                            lambda b, hp, c: (b, hp, jnp.maximum(per * c - 1, 0), 0))
        return [cur, cur, prev, cur, prev]

    return pl.pallas_call(
        _attn_kernel,
        grid=(B, N_PAIRS, nsup),
        in_specs=specs(1) + specs(4) + specs(16) + [
            pl.BlockSpec((2, 2 * BLOCK, 2 * BLOCK), lambda b, hp, c: (0, 0, 0))],
        out_specs=pl.BlockSpec((None, SUPER, LANES), lambda b, hp, c: (b, c, hp)),
        out_shape=jax.ShapeDtypeStruct((B, S, ATTN_WIDTH), BF16),
        scratch_shapes=[pltpu.VMEM((2, SUPER, LANES), F32)] * 3,
        compiler_params=_cparams(("parallel", "parallel", "parallel")),
        name="dilated_attn",
    )(q1, k1, k1, v1, v1, q4, k4, k4, v4, v4, q16, k16, k16, v16, v16, _band_bias())


def _memkv_kernel(mem_ref, mn_ref, wkv_ref, gk_ref, o256_ref, km_ref, vm_ref):
    m = mem_ref[...]
    ms = jnp.mean(m * m, axis=-1, keepdims=True)
    mn = (m * lax.rsqrt(ms + EPS) * mn_ref[...]).astype(BF16)
    kv = jnp.dot(mn, wkv_ref[...], preferred_element_type=F32)
    km_ref[...] = _head_norm(kv[:, :MEM_WIDTH], o256_ref, gk_ref).astype(BF16)
    vm_ref[...] = kv[:, MEM_WIDTH:].astype(BF16)


def _memkv(mem, mem_norm, w_mem_kv, gmk, B):
    const = lambda b: (0, 0)
    return pl.pallas_call(
        _memkv_kernel,
        grid=(B,),
        in_specs=[pl.BlockSpec((None, N_MEM, D_MODEL), lambda b: (b, 0, 0)),
                  pl.BlockSpec((1, D_MODEL), const),
                  pl.BlockSpec((D_MODEL, 2 * MEM_WIDTH), const),
                  pl.BlockSpec((1, MEM_WIDTH), const),
                  pl.BlockSpec((MEM_WIDTH, MEM_WIDTH), const)],
        out_specs=[pl.BlockSpec((None, N_MEM, MEM_WIDTH), lambda b: (b, 0, 0))] * 2,
        out_shape=[jax.ShapeDtypeStruct((B, N_MEM, MEM_WIDTH), BF16)] * 2,
        compiler_params=_cparams(("parallel",)),
        name="memkv",
    )(mem, mem_norm, w_mem_kv, gmk, _group_ones(MEM_WIDTH))


def _mix_kernel(tiles_per_seq, ya_ref, u_ref, uh_ref, qm_ref, km_ref, vm_ref, x_ref, pp_ref, ps_ref,
                wo_ref, fn_ref, wr_ref, br_ref, utri_ref,
                x2_ref, h2p_ref, rinfo_ref, rinfot_ref, cnt_ref, base_ref):
    i = pl.program_id(0)
    tm = x_ref.shape[0]
    seq_tile = i % tiles_per_seq

    @pl.when(i == 0)
    def _():
        base_ref[...] = jnp.zeros_like(base_ref)

    u = u_ref[...]
    halo = jnp.where(seq_tile == 0, 0.0, uh_ref[...])
    uu = jnp.concatenate([halo, u], axis=0)
    a1 = uu[1:] + uu[:-1]
    a2 = a1[2:] + a1[:-2]
    a3 = a2[4:] + a2[:-4]
    a4 = a3[8:] + a3[:-8]
    lane_p = lax.broadcasted_iota(I32, (tm, POOL_WIDTH), 1)
    g0, g1, g2 = lane_p < 64, lane_p < 128, lane_p < 192
    wsum = jnp.where(g0, a1[15:], jnp.where(g1, a2[13:], jnp.where(g2, a3[9:], a4[1:])))
    wlen = jnp.where(g0, 2.0, jnp.where(g1, 4.0, jnp.where(g2, 8.0, 16.0)))
    tpos = seq_tile * tm + lax.broadcasted_iota(I32, (tm, POOL_WIDTH), 0) + 1
    cnt = jnp.minimum(tpos.astype(F32), wlen)
    pooled = wsum / cnt - u
    y_pool = jnp.dot(pooled.astype(BF16), pp_ref[...], preferred_element_type=F32) * ps_ref[...]

    lane = lax.broadcasted_iota(I32, (tm, LANES), 1)
    is_a = lane < HEAD_DIM
    lane_row = lax.broadcasted_iota(I32, (1, LANES), 1)
    mask_a = jnp.where(lane_row < HEAD_DIM, 1.0, 0.0).astype(BF16)
    mask_b = jnp.where(lane_row < HEAD_DIM, 0.0, 1.0).astype(BF16)
    y_mem = []
    for pr in range(MEM_WIDTH // LANES):
        cols = slice(pr * LANES, (pr + 1) * LANES)
        qp, kp, vp = qm_ref[:, cols], km_ref[:, cols], vm_ref[:, cols]
        outs = []
        for msk in (mask_a, mask_b):
            s = lax.dot_general(qp * msk, kp, (((1,), (1,)), ((), ())), preferred_element_type=F32)
            m = jnp.max(s, axis=-1, keepdims=True)
            p = jnp.exp(s - m)
            p = p / jnp.sum(p, axis=-1, keepdims=True)
            outs.append(jnp.dot(p.astype(BF16), vp, preferred_element_type=F32))
        y_mem.append(jnp.where(is_a, outs[0], outs[1]))

    rest = jnp.concatenate([y_pool] + y_mem, axis=1).astype(BF16)
    proj = jnp.dot(ya_ref[...], wo_ref[:ATTN_WIDTH, :], preferred_element_type=F32)
    proj += jnp.dot(rest, wo_ref[ATTN_WIDTH:, :], preferred_element_type=F32)
    x2 = x_ref[...] + proj
    x2_ref[...] = x2

    ms = jnp.mean(x2 * x2, axis=-1, keepdims=True)
    h2 = (x2 * lax.rsqrt(ms + EPS) * fn_ref[...]).astype(BF16)
    bits = lax.bitcast_convert_type(h2.astype(F32), U32)
    h2p_ref[...] = bits[:, :PACKED] | (bits[:, PACKED:] >> 16)

    logits = jnp.dot(h2, wr_ref[...], preferred_element_type=F32) + br_ref[...]
    lt = jnp.transpose(logits)
    ninf = -jnp.inf
    sub8 = lax.broadcasted_iota(I32, (EXPERTS_PER_GROUP, tm), 0).astype(F32)
    lg = lt[:N_GROUPS]
    mg = jnp.max(lg, axis=0, keepdims=True)
    g_sel = jnp.min(jnp.where(lg == mg, sub8, float(N_GROUPS)), axis=0, keepdims=True)
    w_g = 1.0 / jnp.sum(jnp.exp(lg - mg), axis=0, keepdims=True)
    le = lt[EXPERT_LANE0:EXPERT_LANE0 + EXPERTS_PER_GROUP]
    for g in range(1, N_GROUPS):
        lo_g = EXPERT_LANE0 + g * EXPERTS_PER_GROUP
        le = jnp.where(g_sel == float(g), lt[lo_g:lo_g + EXPERTS_PER_GROUP], le)
    v1 = jnp.max(le, axis=0, keepdims=True)
    i1 = jnp.min(jnp.where(le == v1, sub8, float(EXPERTS_PER_GROUP)), axis=0, keepdims=True)
    le2 = jnp.where(sub8 == i1, ninf, le)
    v2 = jnp.max(le2, axis=0, keepdims=True)
    i2 = jnp.min(jnp.where(le2 == v2, sub8, float(EXPERTS_PER_GROUP)), axis=0, keepdims=True)
    e21 = jnp.exp(v2 - v1)
    gate1 = w_g / (1.0 + e21)
    gate2 = w_g * e21 / (1.0 + e21)
    e1 = g_sel * float(EXPERTS_PER_GROUP) + i1
    e2 = g_sel * float(EXPERTS_PER_GROUP) + i2

    sub64 = lax.broadcasted_iota(I32, (N_EXPERTS, tm), 0).astype(F32)
    hot1 = sub64 == e1
    hot2 = sub64 == e2
    onehot = jnp.where(hot1 | hot2, 1.0, 0.0)
    prefix = jnp.dot(onehot.astype(BF16), utri_ref[...], preferred_element_type=F32)
    tot = prefix + base_ref[...]
    rank1 = jnp.sum(jnp.where(hot1, tot, 0.0), axis=0, keepdims=True)
    rank2 = jnp.sum(jnp.where(hot2, tot, 0.0), axis=0, keepdims=True)
    base_new = base_ref[...] + jnp.sum(onehot, axis=1, keepdims=True)
    base_ref[...] = base_new
    cnt_ref[...] = base_new

    rt = jnp.where(sub8 == 0.0, e1, jnp.where(sub8 == 1.0, e2, jnp.where(sub8 == 2.0, rank1, jnp.where(
        sub8 == 3.0, rank2, jnp.where(sub8 == 4.0, gate1, jnp.where(sub8 == 5.0, gate2, 0.0))))))
    rinfot_ref[...] = rt
    rinfo_ref[...] = jnp.transpose(jnp.concatenate([rt, jnp.zeros((LANES - 8, tm), F32)], axis=0))


def _mix(ya, u, qm, km, vm, x2d, pool_bd, pool_scale, w_out, ffn_norm, w_r, b_r, B, S):
    T = B * S
    tm = TM_MIX
    tps = S // tm
    hb = tm // HALO
    const = lambda i: (0, 0)
    utri = (jnp.arange(tm)[:, None] < jnp.arange(tm)[None, :]).astype(BF16)
    return pl.pallas_call(
        functools.partial(_mix_kernel, tps),
        grid=(T // tm,),
        in_specs=[pl.BlockSpec((tm, ATTN_WIDTH), lambda i: (i, 0)),
                  pl.BlockSpec((tm, POOL_WIDTH), lambda i: (i, 0)),
                  pl.BlockSpec((HALO, POOL_WIDTH), lambda i: (jnp.maximum(i * hb - 1, 0), 0)),
                  pl.BlockSpec((tm, MEM_WIDTH), lambda i: (i, 0)),
                  pl.BlockSpec((None, N_MEM, MEM_WIDTH), lambda i: (i // tps, 0, 0)),
                  pl.BlockSpec((None, N_MEM, MEM_WIDTH), lambda i: (i // tps, 0, 0)),
                  pl.BlockSpec((tm, D_MODEL), lambda i: (i, 0)),
                  pl.BlockSpec((POOL_WIDTH, POOL_WIDTH), const),
                  pl.BlockSpec((1, POOL_WIDTH), const),
                  pl.BlockSpec((D_MODEL, D_MODEL), const),
                  pl.BlockSpec((1, D_MODEL), const),
                  pl.BlockSpec((D_MODEL, LANES), const),
                  pl.BlockSpec((1, LANES), const),
                  pl.BlockSpec((tm, tm), const)],
        out_specs=[pl.BlockSpec((tm, D_MODEL), lambda i: (i, 0)),
                   pl.BlockSpec((tm, PACKED), lambda i: (i, 0)),
                   pl.BlockSpec((tm, LANES), lambda i: (i, 0)),
                   pl.BlockSpec((8, tm), lambda i: (0, i)),
                   pl.BlockSpec((N_EXPERTS, 1), const)],
        out_shape=[jax.ShapeDtypeStruct((T, D_MODEL), F32),
                   jax.ShapeDtypeStruct((T, PACKED), U32),
                   jax.ShapeDtypeStruct((T, LANES), F32),
                   jax.ShapeDtypeStruct((8, T), F32),
                   jax.ShapeDtypeStruct((N_EXPERTS, 1), F32)],
        scratch_shapes=[pltpu.VMEM((N_EXPERTS, 1), F32)],
        compiler_params=_cparams(("arbitrary",)),
        name="mix_router",
    )(ya, u, u, qm, km, vm, x2d, pool_bd, pool_scale, w_out, ffn_norm, w_r, b_r, utri)


def _dest_kernel(rt_ref, ps_ref, d_ref):
    tn = rt_ref.shape[1]
    sub = lax.broadcasted_iota(I32, (N_EXPERTS, tn), 0).astype(F32)
    ps = ps_ref[...]
    rows = []
    for k in range(2):
        e = rt_ref[k:k + 1, :]
        start = jnp.sum(jnp.where(sub == e, ps, 0.0), axis=0, keepdims=True)
        rows.append(start + rt_ref[2 + k:3 + k, :])
    word = rows[0].astype(I32) | (rows[1].astype(I32) << 16)
    d_ref[...] = jnp.concatenate([word, jnp.zeros((7, tn), I32)], axis=0)


def _dest(rinfot, pstart_col, T):
    tn = 2048
    return pl.pallas_call(
        _dest_kernel,
        grid=(T // tn,),
        in_specs=[pl.BlockSpec((8, tn), lambda i: (0, i)),
                  pl.BlockSpec((N_EXPERTS, 1), lambda i: (0, 0))],
        out_specs=pl.BlockSpec((8, tn), lambda i: (0, i)),
        out_shape=jax.ShapeDtypeStruct((8, T), I32),
        compiler_params=_cparams(("parallel",)),
        name="dest_rows",
    )(rinfot, pstart_col)


def _row_map(d_ref, inv_ref):
    T = d_ref.shape[0]
    unroll = 8

    def body(g, carry):
        t0 = g * unroll
        words = [d_ref[t0 + u] for u in range(unroll)]
        for u, w in enumerate(words):
            inv_ref[w & 0xFFFF] = t0 + u
            inv_ref[lax.shift_right_logical(w, 16)] = T + t0 + u
        return carry

    lax.fori_loop(0, T // unroll, body, 0)


def _moe_kernel(be_ref, nu_ref, first_ref, nxt_ref, ws_ref, d_ref,
                h_hbm, w1_hbm, w3_hbm, w2_hbm, fill_hbm, o_hbm,
                tab, xbuf0, xbuf1, a0, b0, a1, b1, ybuf, w1s, w3s, w2s, w1b, w3b, w2b, inv_ref, tsem, ssem, wsem):
    j = pl.program_id(0)
    nu = nu_ref[0]
    T = h_hbm.shape[0]
    tr = TR_MOE
    n_tiles = be_ref.shape[0]
    xbuf = (xbuf0, xbuf1)
    abuf = ((a0, b0), (a1, b1))

    def tile_rows(ref, row, n=1):
        return ref.at[pl.ds(pl.multiple_of(row * ROW_SUB, ROW_SUB), n * ROW_SUB)]

    def gather_tile(tile, slot):
        for i in range(tr):
            v = inv_ref[tile * tr + i]
            if T & (T - 1) == 0:
                tok = v & (T - 1)
            else:
                tok = jnp.where(v >= 2 * T, 0, jnp.where(v >= T, v - T, v))
            xbuf[slot][pl.ds(i, 1), :] = tab[pl.ds(tok, 1), :]

    def scatter_row(tile, slot, i):
        v = inv_ref[tile * tr + i]
        return pltpu.make_async_copy(tile_rows(ybuf, slot * tr + i), tile_rows(o_hbm, v), ssem.at[slot])

    def scatter_wait(slot):
        pltpu.make_async_copy(tile_rows(ybuf, slot * tr, tr), tile_rows(o_hbm, 0, tr), ssem.at[slot]).wait()

    def weight_copies(e, slot):
        return [pltpu.make_async_copy(src.at[e], dst.at[slot], wsem.at[slot])
                for src, dst in ((w1_hbm, w1s), (w3_hbm, w3s), (w2_hbm, w2s))]

    def up_project(slot):
        w = xbuf[slot][...]
        lo = lax.bitcast_convert_type(w & jnp.uint32(0xFFFF0000), F32).astype(BF16)
        hi = lax.bitcast_convert_type(w << 16, F32).astype(BF16)
        for dst, wb in zip(abuf[slot], (w1b, w3b)):
            dst[...] = (jnp.dot(lo, wb[:PACKED, :], preferred_element_type=F32)
                        + jnp.dot(hi, wb[PACKED:, :], preferred_element_type=F32))

    @pl.when(j < nu)
    def _():
        s = j & 1
        jn = jnp.minimum(j + 1, n_tiles - 1)

        @pl.when(j == 0)
        def _():
            fill = pltpu.make_async_copy(fill_hbm, inv_ref, tsem)
            fill.start()
            fill.wait()
            ybuf[...] = jnp.zeros_like(ybuf)
            scratch_rows = pltpu.make_async_copy(ybuf, tile_rows(o_hbm, 2 * T, 2 * tr), tsem)
            scratch_rows.start()
            scratch_rows.wait()
            table = pltpu.make_async_copy(h_hbm, tab, tsem)
            table.start()
            for cp in weight_copies(be_ref[0], 0):
                cp.start()
            _row_map(d_ref, inv_ref)
            table.wait()
            gather_tile(0, 0)
            gather_tile(jnp.minimum(1, n_tiles - 1), 1)
            for cp in weight_copies(be_ref[0], 0):
                cp.wait()

            @pl.when(nxt_ref[0] >= 0)
            def _():
                for cp in weight_copies(nxt_ref[0], 1):
                    cp.start()

            w1b[...] = w1s[0].astype(BF16)
            w3b[...] = w3s[0].astype(BF16)
            w2b[...] = w2s[0].astype(BF16)
            up_project(0)

        @pl.when((j >= 1) & (first_ref[j] == 1))
        def _():
            w2b[...] = w2s[ws_ref[j]].astype(BF16)

        @pl.when((j + 1 < nu) & (first_ref[jn] == 1))
        def _():
            wslot = ws_ref[jn]
            for cp in weight_copies(be_ref[jn], wslot):
                cp.wait()
            e_next = nxt_ref[jn]

            @pl.when(e_next >= 0)
            def _():
                for cp in weight_copies(e_next, 1 - wslot):
                    cp.start()

            w1b[...] = w1s[wslot].astype(BF16)
            w3b[...] = w3s[wslot].astype(BF16)

        @pl.when(j >= 2)
        def _():
            scatter_wait(s)

        for slot in range(2):
            @pl.when(s == slot)
            def _():
                gather_tile(jnp.minimum(j + 2, n_tiles - 1), slot)
                a = abuf[slot][0][...]
                b = abuf[slot][1][...]
                up_project(1 - slot)
                hmid = (a / (1.0 + jnp.exp(-a)) * b).astype(BF16)
                y = jnp.dot(hmid, w2b[...], preferred_element_type=F32)
                for c in range(ROW_SUB):
                    ybuf[pl.ds(slot * (tr * ROW_SUB) + c, tr, stride=ROW_SUB), :] = y[:, c * LANES:(c + 1) * LANES]
                for i in range(tr):
                    scatter_row(j, slot, i).start()

        @pl.when(j == nu - 1)
        def _():
            scatter_wait(s)

            @pl.when(j >= 1)
            def _():
                scatter_wait(1 - s)


def _moe(blk_exp, n_used, first, nxt, wslot, dword, h2p, w1, w3, w2):
    T = h2p.shape[0]
    tr = TR_MOE
    n_tiles = blk_exp.shape[0]
    R = n_tiles * tr
    assert R <= 1 << 16
    fill = 2 * T + jnp.arange(R, dtype=I32) % (2 * tr)
    any_spec = pl.BlockSpec(memory_space=pl.ANY)
    return pl.pallas_call(
        _moe_kernel,
        grid_spec=pltpu.PrefetchScalarGridSpec(
            num_scalar_prefetch=6,
            grid=(n_tiles,),
            in_specs=[any_spec] * 5,
            out_specs=any_spec,
            scratch_shapes=[pltpu.VMEM((T, PACKED), U32),
                            pltpu.VMEM((tr, PACKED), U32),
                            pltpu.VMEM((tr, PACKED), U32),
                            pltpu.VMEM((tr, EXPERT_HIDDEN), F32),
                            pltpu.VMEM((tr, EXPERT_HIDDEN), F32),
                            pltpu.VMEM((tr, EXPERT_HIDDEN), F32),
                            pltpu.VMEM((tr, EXPERT_HIDDEN), F32),
                            pltpu.VMEM((2 * tr * ROW_SUB, LANES), F32),
                            pltpu.VMEM((2, D_MODEL, EXPERT_HIDDEN), F32),
                            pltpu.VMEM((2, D_MODEL, EXPERT_HIDDEN), F32),
                            pltpu.VMEM((2, EXPERT_HIDDEN, D_MODEL), F32),
                            pltpu.VMEM((D_MODEL, EXPERT_HIDDEN), BF16),
                            pltpu.VMEM((D_MODEL, EXPERT_HIDDEN), BF16),
                            pltpu.VMEM((EXPERT_HIDDEN, D_MODEL), BF16),
                            pltpu.SMEM((R,), I32),
                            pltpu.SemaphoreType.DMA(()),
                            pltpu.SemaphoreType.DMA((2,)),
                            pltpu.SemaphoreType.DMA((2,))]),
        out_shape=jax.ShapeDtypeStruct(((2 * T + 2 * tr) * ROW_SUB, LANES), F32),
        compiler_params=pltpu.CompilerParams(dimension_semantics=("arbitrary",), vmem_limit_bytes=MOE_VMEM_LIMIT),
        name="moe_experts",
    )(blk_exp, n_used, first, nxt, wslot, dword, h2p, w1, w3, w2, fill)


def _combine_kernel(x2_ref, rinfo_ref, y1_ref, y2_ref, o_ref):
    g1 = rinfo_ref[:, 4:5]
    g2 = rinfo_ref[:, 5:6]
    tc = x2_ref.shape[0]
    for c in range(ROW_SUB):
        cols = slice(c * LANES, (c + 1) * LANES)
        rows = pl.ds(c, tc, stride=ROW_SUB)
        o_ref[:, cols] = x2_ref[:, cols] + (g1 * y1_ref[rows, :] + g2 * y2_ref[rows, :])


def _combine(x2, rinfo, y2slot):
    T = x2.shape[0]
    tc = TC_COMBINE
    return pl.pallas_call(
        _combine_kernel,
        grid=(T // tc,),
        in_specs=[pl.BlockSpec((tc, D_MODEL), lambda i: (i, 0)),
                  pl.BlockSpec((tc, LANES), lambda i: (i, 0)),
                  pl.BlockSpec((tc * ROW_SUB, LANES), lambda i: (i, 0)),
                  pl.BlockSpec((tc * ROW_SUB, LANES), lambda i: (T // tc + i, 0))],
        out_specs=pl.BlockSpec((tc, D_MODEL), lambda i: (i, 0)),
        out_shape=jax.ShapeDtypeStruct((T, D_MODEL), F32),
        compiler_params=_cparams(("parallel",)),
        name="combine",
    )(x2, rinfo, y2slot, y2slot)


def _layer(x, mem, attn_norm, w_in, q_norm, k_norm, pool_proj, pool_scale, mem_norm, w_mem_kv,
           mq_norm, mk_norm, w_out, ffn_norm, w_group, b_group, w_router, b_router, w1, w3, w2):
    B, S, D = x.shape
    T = B * S
    assert D == D_MODEL and S % SUPER == 0 and T % TM_PROJ == 0
    x2d = x.reshape(T, D)
    row = lambda v: v.reshape(1, -1).astype(F32)
    scale = HEAD_DIM ** -0.5
    gq = row(jnp.tile(q_norm, ATTN_WIDTH // HEAD_DIM) * scale)
    gk = row(jnp.tile(k_norm, ATTN_WIDTH // HEAD_DIM))
    gmq = row(jnp.tile(mq_norm, MEM_WIDTH // HEAD_DIM) * scale)
    gmk = row(jnp.tile(mk_norm, MEM_WIDTH // HEAD_DIM))

    (q1, k1, v1, q4, k4, v4, q16, k16, v16, u, qm) = _proj(
        x2d, row(attn_norm), w_in.astype(BF16), gq, gk, gmq, B, S)
    ya = _attn(q1, k1, v1, q4, k4, v4, q16, k16, v16, B, S).reshape(T, ATTN_WIDTH)
    km, vm = _memkv(mem, row(mem_norm), w_mem_kv.astype(BF16), gmk, B)

    pool_bd = jax.scipy.linalg.block_diag(*[pool_proj[g] for g in range(pool_proj.shape[0])]).astype(BF16)
    w_r = jnp.zeros((D, LANES), F32)
    w_r = w_r.at[:, :N_GROUPS].set(w_group)
    w_r = w_r.at[:, EXPERT_LANE0:].set(jnp.transpose(w_router, (1, 0, 2)).reshape(D, N_EXPERTS))
    b_r = jnp.zeros((1, LANES), F32).at[0, :N_GROUPS].set(b_group).at[0, EXPERT_LANE0:].set(b_router.reshape(-1))
    x2, h2p, rinfo, rinfot, counts = _mix(ya, u, qm, km, vm, x2d, pool_bd, row(pool_scale),
                                          w_out.astype(BF16), row(ffn_norm), w_r.astype(BF16), b_r, B, S)

    pstart, sched = _tile_schedule(counts, T)
    dest = _dest(rinfot, pstart.astype(F32).reshape(N_EXPERTS, 1), T)
    y2slot = _moe(*sched, dest[0], h2p, w1, w3, w2)
    out = _combine(x2, rinfo, y2slot)
    return out.reshape(B, S, D)


def _tile_schedule(counts, T):
    tr = TR_MOE
    n_tiles = (2 * T + N_EXPERTS * tr) // tr
    cnt = counts[:, 0].astype(I32)
    padded = ((cnt + tr - 1) // tr) * tr
    pend = jnp.cumsum(padded)
    pstart = pend - padded
    n_used = (pend[-1] // tr).astype(I32).reshape(1)
    tiles = jnp.arange(n_tiles, dtype=I32)
    tile_row = jnp.minimum(tiles, n_used[0] - 1) * tr
    blk_exp = jnp.minimum(jnp.sum(tile_row[:, None] >= pend[None, :], axis=1), N_EXPERTS - 1).astype(I32)
    in_use = tiles < n_used[0]
    first = in_use & ((tiles == 0) | (blk_exp != jnp.roll(blk_exp, 1)))
    run_idx = jnp.cumsum(first.astype(I32)) - 1
    used = cnt > 0
    exp_of_run = jnp.argsort(jnp.logical_not(used), stable=True).astype(I32)
    nxt = jnp.where(run_idx + 1 < jnp.sum(used), exp_of_run[jnp.clip(run_idx + 1, 0, N_EXPERTS - 1)], -1)
    sched = (blk_exp, n_used, first.astype(I32), nxt.astype(I32), (run_idx & 1).astype(I32))
    return pstart, sched


def kernel(x, mem, attn_norm, w_in, q_norm, k_norm, pool_proj, pool_scale, mem_norm, w_mem_kv, mq_norm, mk_norm,
           w_out, ffn_norm, w_group, b_group, w_router, b_router, w1, w3, w2):
    for l in range(attn_norm.shape[0]):
        x = _layer(x, mem, attn_norm[l], w_in[l], q_norm[l], k_norm[l], pool_proj[l], pool_scale[l],
                   mem_norm[l], w_mem_kv[l], mq_norm[l], mk_norm[l], w_out[l], ffn_norm[l],
                   w_group[l], b_group[l], w_router[l], b_router[l], w1[l], w3[l], w2[l])
    return x
```

```python
import functools

import jax
import jax.numpy as jnp
from jax import lax
from jax.experimental import pallas as pl
from jax.experimental.pallas import tpu as pltpu

F32 = jnp.float32
BF16 = jnp.bfloat16
I32 = jnp.int32
U32 = jnp.uint32

D_MODEL = 1024
HEAD_DIM = 64
ATTN_WIDTH = 512
POOL_WIDTH = 256
MEM_WIDTH = 256
N_MEM = 256
IN_WIDTH = 3 * ATTN_WIDTH + POOL_WIDTH + MEM_WIDTH
N_GROUPS = 8
EXPERTS_PER_GROUP = 8
N_EXPERTS = 64
EXPERT_HIDDEN = 512
EPS = 1e-6
NEG_INF = -1e30

LANES = 128
N_PAIRS = ATTN_WIDTH // LANES
BLOCK = 128
SUPER = 16 * BLOCK
HALO = 16
ROW_SUB = D_MODEL // LANES
PACKED = D_MODEL // 2

TM_PROJ = 1024
TM_MIX = 1024
TR_MOE = 256
TC_COMBINE = 1024
EXPERT_LANE0 = 64

VMEM_LIMIT = 48 * 1024 * 1024
MOE_VMEM_LIMIT = 56 * 1024 * 1024


def _cparams(sem):
    return pltpu.CompilerParams(dimension_semantics=sem, vmem_limit_bytes=VMEM_LIMIT)


def _group_ones(n):
    g = jnp.arange(n) // HEAD_DIM
    return (g[:, None] == g[None, :]).astype(BF16)


def _head_norm(z, ones_ref, gain_ref):
    ss = jnp.dot((z * z).astype(BF16), ones_ref[...], preferred_element_type=F32)
    return z * lax.rsqrt(ss * (1.0 / HEAD_DIM) + EPS) * gain_ref[...]


def _proj_kernel(x_ref, an_ref, win_ref, gq_ref, gk_ref, gm_ref, o512_ref, o256_ref,
                 q1_ref, k1_ref, v1_ref, q4_ref, k4_ref, v4_ref, q16_ref, k16_ref, v16_ref,
                 u_ref, qm_ref, zs_ref, z4_ref):
    tm = x_ref.shape[0]
    x = x_ref[...]
    ms = jnp.mean(x * x, axis=-1, keepdims=True)
    h = (x * lax.rsqrt(ms + EPS) * an_ref[...]).astype(BF16)
    a0, a1, a2, a3 = ATTN_WIDTH, 2 * ATTN_WIDTH, 3 * ATTN_WIDTH, 3 * ATTN_WIDTH + POOL_WIDTH

    def cols(lo, hi):
        return jnp.dot(h, win_ref[:, lo:hi], preferred_element_type=F32)

    groups = ((lambda: _head_norm(cols(0, a0), o512_ref, gq_ref), q1_ref, q4_ref, q16_ref),
              (lambda: _head_norm(cols(a0, a1), o512_ref, gk_ref), k1_ref, k4_ref, k16_ref),
              (lambda: cols(a1, a2), v1_ref, v4_ref, v16_ref))
    for make, o1, o4, o16 in groups:
        val = make()
        for hp in range(N_PAIRS):
            pair = val[:, hp * LANES:(hp + 1) * LANES]
            zs_ref[hp] = pair
            o1[hp] = pair.astype(BF16)
        q4n = tm // 4
        for hp in range(N_PAIRS):
            for r4 in range(4):
                rows = zs_ref[hp, pl.ds(r4, q4n, stride=4), :]
                z4_ref[hp, pl.ds(r4 * q4n, q4n), :] = rows
                o4[hp, :, r4 * LANES:(r4 + 1) * LANES] = rows.astype(BF16)
            for r16 in range(16):
                r4, a = r16 % 4, r16 // 4
                rows = z4_ref[hp, pl.ds(r4 * q4n + a, tm // 16, stride=4), :]
                o16[hp, :, r16 * LANES:(r16 + 1) * LANES] = rows.astype(BF16)
    u_ref[...] = cols(a2, a3)
    qm_ref[...] = _head_norm(cols(a3, IN_WIDTH), o256_ref, gm_ref).astype(BF16)


def _proj(x2d, attn_norm, w_in, gq, gk, gm, B, S):
    T = B * S
    tm = TM_PROJ
    nj = S // tm
    const = lambda i: (0, 0)

    def lay(d):
        return jax.ShapeDtypeStruct((B, N_PAIRS, S // d, d * LANES), BF16)

    def lay_spec(d):
        return pl.BlockSpec((None, N_PAIRS, tm // d, d * LANES), lambda i: (i // nj, 0, i % nj, 0))

    out_shape = [lay(1)] * 3 + [lay(4)] * 3 + [lay(16)] * 3 + [
        jax.ShapeDtypeStruct((T, POOL_WIDTH), F32), jax.ShapeDtypeStruct((T, MEM_WIDTH), BF16)]
    out_specs = [lay_spec(1)] * 3 + [lay_spec(4)] * 3 + [lay_spec(16)] * 3 + [
        pl.BlockSpec((tm, POOL_WIDTH), lambda i: (i, 0)), pl.BlockSpec((tm, MEM_WIDTH), lambda i: (i, 0))]
    return pl.pallas_call(
        _proj_kernel,
        grid=(T // tm,),
        in_specs=[pl.BlockSpec((tm, D_MODEL), lambda i: (i, 0)),
                  pl.BlockSpec((1, D_MODEL), const),
                  pl.BlockSpec((D_MODEL, IN_WIDTH), const),
                  pl.BlockSpec((1, ATTN_WIDTH), const),
                  pl.BlockSpec((1, ATTN_WIDTH), const),
                  pl.BlockSpec((1, MEM_WIDTH), const),
                  pl.BlockSpec((ATTN_WIDTH, ATTN_WIDTH), const),
                  pl.BlockSpec((MEM_WIDTH, MEM_WIDTH), const)],
        out_specs=out_specs,
        out_shape=out_shape,
        scratch_shapes=[pltpu.VMEM((N_PAIRS, tm, LANES), F32)] * 2,
        compiler_params=_cparams(("parallel",)),
        name="proj",
    )(x2d, attn_norm, w_in, gq, gk, gm, _group_ones(ATTN_WIDTH), _group_ones(MEM_WIDTH))


def _attn_kernel(q1, k1c, k1p, v1c, v1p, q4, k4c, k4p, v4c, v4p, q16, k16c, k16p, v16c, v16p,
                 bias_ref, o_ref, obuf, mbuf, lbuf):
    c = pl.program_id(2)
    lane = lax.broadcasted_iota(I32, (BLOCK, LANES), 1)
    is_a = lane < HEAD_DIM
    lane_row = lax.broadcasted_iota(I32, (1, LANES), 1)
    mask_a = jnp.where(lane_row < HEAD_DIM, 1.0, 0.0).astype(BF16)
    mask_b = jnp.where(lane_row < HEAD_DIM, 0.0, 1.0).astype(BF16)
    bias_full = bias_ref[0]
    bias_first = jnp.where(c > 0, bias_full, bias_ref[1])

    def tile(q_t, kp_t, kc_t, vp_t, vc_t, bias):
        lhs = jnp.concatenate([q_t * mask_a, q_t * mask_b], axis=0)
        keys = jnp.concatenate([kp_t, kc_t], axis=0)
        s = lax.dot_general(lhs, keys, (((1,), (1,)), ((), ())), preferred_element_type=F32) + bias
        m = jnp.max(s, axis=-1, keepdims=True)
        p = jnp.exp(s - m)
        l = jnp.sum(p, axis=-1, keepdims=True)
        vals = jnp.concatenate([vp_t, vc_t], axis=0)
        pv = jnp.dot(p.astype(BF16), vals, preferred_element_type=F32)
        return (jnp.where(is_a, pv[:BLOCK], pv[BLOCK:]), jnp.where(is_a, m[:BLOCK], m[BLOCK:]),
                jnp.where(is_a, l[:BLOCK], l[BLOCK:]))

    for pat, (d, q, kc, kp, vc, vp) in enumerate(((16, q16, k16c, k16p, v16c, v16p), (4, q4, k4c, k4p, v4c, v4p))):
        nblk = SUPER // (BLOCK * d)
        for r in range(d):
            cols = pl.ds(r * LANES, LANES)
            for jb in range(nblk):
                cur = pl.ds(jb * BLOCK, BLOCK)
                if jb == 0:
                    kp_t, vp_t, bias = kp[:, cols], vp[:, cols], bias_first
                else:
                    prev = pl.ds((jb - 1) * BLOCK, BLOCK)
                    kp_t, vp_t, bias = kc[prev, cols], vc[prev, cols], bias_full
                o_t, m_t, l_t = tile(q[cur, cols], kp_t, kc[cur, cols], vp_t, vc[cur, cols], bias)
                rows = pl.ds(jb * BLOCK * d + r, BLOCK, stride=d)
                obuf[pat, rows, :] = o_t
                mbuf[pat, rows, :] = m_t
                lbuf[pat, rows, :] = l_t
    for jb in range(SUPER // BLOCK):
        cur = pl.ds(jb * BLOCK, BLOCK)
        if jb == 0:
            kp_t, vp_t, bias = k1p[...], v1p[...], bias_first
        else:
            prev = pl.ds((jb - 1) * BLOCK, BLOCK)
            kp_t, vp_t, bias = k1c[prev, :], v1c[prev, :], bias_full
        o0, m0, l0 = tile(q1[cur, :], kp_t, k1c[cur, :], vp_t, v1c[cur, :], bias)
        m1, m2 = mbuf[0, cur, :], mbuf[1, cur, :]
        top = jnp.maximum(jnp.maximum(m0, m1), m2)
        w0, w1, w2 = jnp.exp(m0 - top), jnp.exp(m1 - top), jnp.exp(m2 - top)
        num = w0 * o0 + w1 * obuf[0, cur, :] + w2 * obuf[1, cur, :]
        den = w0 * l0 + w1 * lbuf[0, cur, :] + w2 * lbuf[1, cur, :]
        o_ref[cur, :] = (num * (1.0 / den)).astype(BF16)


def _band_bias():
    qi = jnp.arange(BLOCK)[:, None]
    kj = jnp.arange(2 * BLOCK)[None, :]
    dist = qi + BLOCK - kj
    in_band = (dist >= 0) & (dist <= BLOCK)
    full = jnp.where(in_band, 0.0, NEG_INF).astype(F32)
    first = jnp.where(in_band & (kj >= BLOCK), 0.0, NEG_INF).astype(F32)
    return jnp.stack([jnp.tile(full, (2, 1)), jnp.tile(first, (2, 1))])


def _attn(q1, k1, v1, q4, k4, v4, q16, k16, v16, B, S):
    nsup = S // SUPER

    def specs(d):
        rows = SUPER // d
        per = rows // BLOCK
        cur = pl.BlockSpec((None, None, rows, d * LANES), lambda b, hp, c: (b, hp, c, 0))
        prev = pl.BlockSpec((None, None, BLOCK, d * LANES),
                            lambda b, hp, c: (b, hp, jnp.maximum(per * c - 1, 0), 0))
        return [cur, cur, prev, cur, prev]

    return pl.pallas_call(
        _attn_kernel,
        grid=(B, N_PAIRS, nsup),
        in_specs=specs(1) + specs(4) + specs(16) + [
            pl.BlockSpec((2, 2 * BLOCK, 2 * BLOCK), lambda b, hp, c: (0, 0, 0))],
        out_specs=pl.BlockSpec((None, SUPER, LANES), lambda b, hp, c: (b, c, hp)),
        out_shape=jax.ShapeDtypeStruct((B, S, ATTN_WIDTH), BF16),
        scratch_shapes=[pltpu.VMEM((2, SUPER, LANES), F32)] * 3,
        compiler_params=_cparams(("parallel", "parallel", "parallel")),
        name="dilated_attn",
    )(q1, k1, k1, v1, v1, q4, k4, k4, v4, v4, q16, k16, k16, v16, v16, _band_bias())


def _memkv_kernel(mem_ref, mn_ref, wkv_ref, gk_ref, o256_ref, km_ref, vm_ref):
    m = mem_ref[...]
    ms = jnp.mean(m * m, axis=-1, keepdims=True)
    mn = (m * lax.rsqrt(ms + EPS) * mn_ref[...]).astype(BF16)
    kv = jnp.dot(mn, wkv_ref[...], preferred_element_type=F32)
    km_ref[...] = _head_norm(kv[:, :MEM_WIDTH], o256_ref, gk_ref).astype(BF16)
    vm_ref[...] = kv[:, MEM_WIDTH:].astype(BF16)


def _memkv(mem, mem_norm, w_mem_kv, gmk, B):
    const = lambda b: (0, 0)
    return pl.pallas_call(
        _memkv_kernel,
        grid=(B,),
        in_specs=[pl.BlockSpec((None, N_MEM, D_MODEL), lambda b: (b, 0, 0)),
                  pl.BlockSpec((1, D_MODEL), const),
                  pl.BlockSpec((D_MODEL, 2 * MEM_WIDTH), const),
                  pl.BlockSpec((1, MEM_WIDTH), const),
                  pl.BlockSpec((MEM_WIDTH, MEM_WIDTH), const)],
        out_specs=[pl.BlockSpec((None, N_MEM, MEM_WIDTH), lambda b: (b, 0, 0))] * 2,
        out_shape=[jax.ShapeDtypeStruct((B, N_MEM, MEM_WIDTH), BF16)] * 2,
        compiler_params=_cparams(("parallel",)),
        name="memkv",
    )(mem, mem_norm, w_mem_kv, gmk, _group_ones(MEM_WIDTH))


def _mix_kernel(tiles_per_seq, ya_ref, u_ref, uh_ref, qm_ref, km_ref, vm_ref, x_ref, pp_ref, ps_ref,
                wo_ref, fn_ref, wr_ref, br_ref, utri_ref,
                x2_ref, h2p_ref, rinfo_ref, rinfot_ref, cnt_ref, base_ref):
    i = pl.program_id(0)
    tm = x_ref.shape[0]
    seq_tile = i % tiles_per_seq

    @pl.when(i == 0)
    def _():
        base_ref[...] = jnp.zeros_like(base_ref)

    u = u_ref[...]
    halo = jnp.where(seq_tile == 0, 0.0, uh_ref[...])
    uu = jnp.concatenate([halo, u], axis=0)
    a1 = uu[1:] + uu[:-1]
    a2 = a1[2:] + a1[:-2]
    a3 = a2[4:] + a2[:-4]
    a4 = a3[8:] + a3[:-8]
    lane_p = lax.broadcasted_iota(I32, (tm, POOL_WIDTH), 1)
    g0, g1, g2 = lane_p < 64, lane_p < 128, lane_p < 192
    wsum = jnp.where(g0, a1[15:], jnp.where(g1, a2[13:], jnp.where(g2, a3[9:], a4[1:])))
    wlen = jnp.where(g0, 2.0, jnp.where(g1, 4.0, jnp.where(g2, 8.0, 16.0)))
    tpos = seq_tile * tm + lax.broadcasted_iota(I32, (tm, POOL_WIDTH), 0) + 1
    cnt = jnp.minimum(tpos.astype(F32), wlen)
    pooled = wsum / cnt - u
    y_pool = jnp.dot(pooled.astype(BF16), pp_ref[...], preferred_element_type=F32) * ps_ref[...]

    lane = lax.broadcasted_iota(I32, (tm, LANES), 1)
    is_a = lane < HEAD_DIM
    lane_row = lax.broadcasted_iota(I32, (1, LANES), 1)
    mask_a = jnp.where(lane_row < HEAD_DIM, 1.0, 0.0).astype(BF16)
    mask_b = jnp.where(lane_row < HEAD_DIM, 0.0, 1.0).astype(BF16)
    y_mem = []
    for pr in range(MEM_WIDTH // LANES):
        cols = slice(pr * LANES, (pr + 1) * LANES)
        qp, kp, vp = qm_ref[:, cols], km_ref[:, cols], vm_ref[:, cols]
        outs = []
        for msk in (mask_a, mask_b):
            s = lax.dot_general(qp * msk, kp, (((1,), (1,)), ((), ())), preferred_element_type=F32)
            m = jnp.max(s, axis=-1, keepdims=True)
            p = jnp.exp(s - m)
            p = p / jnp.sum(p, axis=-1, keepdims=True)
            outs.append(jnp.dot(p.astype(BF16), vp, preferred_element_type=F32))
        y_mem.append(jnp.where(is_a, outs[0], outs[1]))

    rest = jnp.concatenate([y_pool] + y_mem, axis=1).astype(BF16)
    proj = jnp.dot(ya_ref[...], wo_ref[:ATTN_WIDTH, :], preferred_element_type=F32)
    proj += jnp.dot(rest, wo_ref[ATTN_WIDTH:, :], preferred_element_type=F32)
    x2 = x_ref[...] + proj
    x2_ref[...] = x2

    ms = jnp.mean(x2 * x2, axis=-1, keepdims=True)
    h2 = (x2 * lax.rsqrt(ms + EPS) * fn_ref[...]).astype(BF16)
    bits = lax.bitcast_convert_type(h2.astype(F32), U32)
    h2p_ref[...] = bits[:, :PACKED] | (bits[:, PACKED:] >> 16)

    logits = jnp.dot(h2, wr_ref[...], preferred_element_type=F32) + br_ref[...]
    lt = jnp.transpose(logits)
    ninf = -jnp.inf
    sub8 = lax.broadcasted_iota(I32, (EXPERTS_PER_GROUP, tm), 0).astype(F32)
    lg = lt[:N_GROUPS]
    mg = jnp.max(lg, axis=0, keepdims=True)
    g_sel = jnp.min(jnp.where(lg == mg, sub8, float(N_GROUPS)), axis=0, keepdims=True)
    w_g = 1.0 / jnp.sum(jnp.exp(lg - mg), axis=0, keepdims=True)
    le = lt[EXPERT_LANE0:EXPERT_LANE0 + EXPERTS_PER_GROUP]
    for g in range(1, N_GROUPS):
        lo_g = EXPERT_LANE0 + g * EXPERTS_PER_GROUP
        le = jnp.where(g_sel == float(g), lt[lo_g:lo_g + EXPERTS_PER_GROUP], le)
    v1 = jnp.max(le, axis=0, keepdims=True)
    i1 = jnp.min(jnp.where(le == v1, sub8, float(EXPERTS_PER_GROUP)), axis=0, keepdims=True)
    le2 = jnp.where(sub8 == i1, ninf, le)
    v2 = jnp.max(le2, axis=0, keepdims=True)
    i2 = jnp.min(jnp.where(le2 == v2, sub8, float(EXPERTS_PER_GROUP)), axis=0, keepdims=True)
    e21 = jnp.exp(v2 - v1)
    gate1 = w_g / (1.0 + e21)
    gate2 = w_g * e21 / (1.0 + e21)
    e1 = g_sel * float(EXPERTS_PER_GROUP) + i1
    e2 = g_sel * float(EXPERTS_PER_GROUP) + i2

    sub64 = lax.broadcasted_iota(I32, (N_EXPERTS, tm), 0).astype(F32)
    hot1 = sub64 == e1
    hot2 = sub64 == e2
    onehot = jnp.where(hot1 | hot2, 1.0, 0.0)
    prefix = jnp.dot(onehot.astype(BF16), utri_ref[...], preferred_element_type=F32)
    tot = prefix + base_ref[...]
    rank1 = jnp.sum(jnp.where(hot1, tot, 0.0), axis=0, keepdims=True)
    rank2 = jnp.sum(jnp.where(hot2, tot, 0.0), axis=0, keepdims=True)
    base_new = base_ref[...] + jnp.sum(onehot, axis=1, keepdims=True)
    base_ref[...] = base_new
    cnt_ref[...] = base_new

    rt = jnp.where(sub8 == 0.0, e1, jnp.where(sub8 == 1.0, e2, jnp.where(sub8 == 2.0, rank1, jnp.where(
        sub8 == 3.0, rank2, jnp.where(sub8 == 4.0, gate1, jnp.where(sub8 == 5.0, gate2, 0.0))))))
    rinfot_ref[...] = rt
    rinfo_ref[...] = jnp.transpose(jnp.concatenate([rt, jnp.zeros((LANES - 8, tm), F32)], axis=0))


def _mix(ya, u, qm, km, vm, x2d, pool_bd, pool_scale, w_out, ffn_norm, w_r, b_r, B, S):
    T = B * S
    tm = TM_MIX
    tps = S // tm
    hb = tm // HALO
    const = lambda i: (0, 0)
    utri = (jnp.arange(tm)[:, None] < jnp.arange(tm)[None, :]).astype(BF16)
    return pl.pallas_call(
        functools.partial(_mix_kernel, tps),
        grid=(T // tm,),
        in_specs=[pl.BlockSpec((tm, ATTN_WIDTH), lambda i: (i, 0)),
                  pl.BlockSpec((tm, POOL_WIDTH), lambda i: (i, 0)),
                  pl.BlockSpec((HALO, POOL_WIDTH), lambda i: (jnp.maximum(i * hb - 1, 0), 0)),
                  pl.BlockSpec((tm, MEM_WIDTH), lambda i: (i, 0)),
                  pl.BlockSpec((None, N_MEM, MEM_WIDTH), lambda i: (i // tps, 0, 0)),
                  pl.BlockSpec((None, N_MEM, MEM_WIDTH), lambda i: (i // tps, 0, 0)),
                  pl.BlockSpec((tm, D_MODEL), lambda i: (i, 0)),
                  pl.BlockSpec((POOL_WIDTH, POOL_WIDTH), const),
                  pl.BlockSpec((1, POOL_WIDTH), const),
                  pl.BlockSpec((D_MODEL, D_MODEL), const),
                  pl.BlockSpec((1, D_MODEL), const),
                  pl.BlockSpec((D_MODEL, LANES), const),
                  pl.BlockSpec((1, LANES), const),
                  pl.BlockSpec((tm, tm), const)],
        out_specs=[pl.BlockSpec((tm, D_MODEL), lambda i: (i, 0)),
                   pl.BlockSpec((tm, PACKED), lambda i: (i, 0)),
                   pl.BlockSpec((tm, LANES), lambda i: (i, 0)),
                   pl.BlockSpec((8, tm), lambda i: (0, i)),
                   pl.BlockSpec((N_EXPERTS, 1), const)],
        out_shape=[jax.ShapeDtypeStruct((T, D_MODEL), F32),
                   jax.ShapeDtypeStruct((T, PACKED), U32),
                   jax.ShapeDtypeStruct((T, LANES), F32),
                   jax.ShapeDtypeStruct((8, T), F32),
                   jax.ShapeDtypeStruct((N_EXPERTS, 1), F32)],
        scratch_shapes=[pltpu.VMEM((N_EXPERTS, 1), F32)],
        compiler_params=_cparams(("arbitrary",)),
        name="mix_router",
    )(ya, u, u, qm, km, vm, x2d, pool_bd, pool_scale, w_out, ffn_norm, w_r, b_r, utri)


def _dest_kernel(rt_ref, ps_ref, d_ref):
    tn = rt_ref.shape[1]
    sub = lax.broadcasted_iota(I32, (N_EXPERTS, tn), 0).astype(F32)
    ps = ps_ref[...]
    rows = []
    for k in range(2):
        e = rt_ref[k:k + 1, :]
        start = jnp.sum(jnp.where(sub == e, ps, 0.0), axis=0, keepdims=True)
        rows.append(start + rt_ref[2 + k:3 + k, :])
    word = rows[0].astype(I32) | (rows[1].astype(I32) << 16)
    d_ref[...] = jnp.concatenate([word, jnp.zeros((7, tn), I32)], axis=0)


def _dest(rinfot, pstart_col, T):
    tn = 2048
    return pl.pallas_call(
        _dest_kernel,
        grid=(T // tn,),
        in_specs=[pl.BlockSpec((8, tn), lambda i: (0, i)),
                  pl.BlockSpec((N_EXPERTS, 1), lambda i: (0, 0))],
        out_specs=pl.BlockSpec((8, tn), lambda i: (0, i)),
        out_shape=jax.ShapeDtypeStruct((8, T), I32),
        compiler_params=_cparams(("parallel",)),
        name="dest_rows",
    )(rinfot, pstart_col)


def _row_map(d_ref, inv_ref):
    T = d_ref.shape[0]
    unroll = 8

    def body(g, carry):
        t0 = g * unroll
        words = [d_ref[t0 + u] for u in range(unroll)]
        for u, w in enumerate(words):
            inv_ref[w & 0xFFFF] = t0 + u
            inv_ref[lax.shift_right_logical(w, 16)] = T + t0 + u
        return carry

    lax.fori_loop(0, T // unroll, body, 0)


def _moe_kernel(be_ref, nu_ref, first_ref, nxt_ref, ws_ref, d_ref,
                h_hbm, w1_hbm, w3_hbm, w2_hbm, fill_hbm, o_hbm,
                tab, xbuf0, xbuf1, a0, b0, a1, b1, ybuf, w1s, w3s, w2s, w1b, w3b, w2b, inv_ref, dst_ref, tsem, ssem, wsem):
    j = pl.program_id(0)
    nu = nu_ref[0]
    T = h_hbm.shape[0]
    tr = TR_MOE
    n_tiles = be_ref.shape[0]
    xbuf = (xbuf0, xbuf1)
    abuf = ((a0, b0), (a1, b1))

    def tile_rows(ref, row, n=1):
        return ref.at[pl.ds(pl.multiple_of(row * ROW_SUB, ROW_SUB), n * ROW_SUB)]

    def gather_tile(tile, slot):
        for i in range(tr):
            v = inv_ref[tile * tr + i]
            if T & (T - 1) == 0:
                tok = v & (T - 1)
            else:
                tok = jnp.where(v >= 2 * T, 0, jnp.where(v >= T, v - T, v))
            xbuf[slot][pl.ds(i, 1), :] = tab[pl.ds(tok, 1), :]

    def stage_rows(tile, slot):
        for i in range(tr):
            dst_ref[slot * tr + i] = inv_ref[tile * tr + i]

    def scatter_row(slot, i):
        v = dst_ref[slot * tr + i]
        return pltpu.make_async_copy(tile_rows(ybuf, slot * tr + i), tile_rows(o_hbm, v), ssem.at[slot])

    def scatter_wait(slot):
        pltpu.make_async_copy(tile_rows(ybuf, slot * tr, tr), tile_rows(o_hbm, 0, tr), ssem.at[slot]).wait()

    def weight_copies(e, slot):
        return [pltpu.make_async_copy(src.at[e], dst.at[slot], wsem.at[slot])
                for src, dst in ((w1_hbm, w1s), (w3_hbm, w3s), (w2_hbm, w2s))]

    def up_project(slot):
        w = xbuf[slot][...]
        lo = lax.bitcast_convert_type(w & jnp.uint32(0xFFFF0000), F32).astype(BF16)
        hi = lax.bitcast_convert_type(w << 16, F32).astype(BF16)
        for dst, wb in zip(abuf[slot], (w1b, w3b)):
            dst[...] = (jnp.dot(lo, wb[:PACKED, :], preferred_element_type=F32)
                        + jnp.dot(hi, wb[PACKED:, :], preferred_element_type=F32))

    @pl.when(j < nu)
    def _():
        s = j & 1
        jn = jnp.minimum(j + 1, n_tiles - 1)

        @pl.when(j == 0)
        def _():
            fill = pltpu.make_async_copy(fill_hbm, inv_ref, tsem)
            fill.start()
            fill.wait()
            ybuf[...] = jnp.zeros_like(ybuf)
            scratch_rows = pltpu.make_async_copy(ybuf, tile_rows(o_hbm, 2 * T, 2 * tr), tsem)
            scratch_rows.start()
            scratch_rows.wait()
            table = pltpu.make_async_copy(h_hbm, tab, tsem)
            table.start()
            for cp in weight_copies(be_ref[0], 0):
                cp.start()
            _row_map(d_ref, inv_ref)
            table.wait()
            gather_tile(0, 0)
            gather_tile(jnp.minimum(1, n_tiles - 1), 1)
            stage_rows(0, 0)
            for cp in weight_copies(be_ref[0], 0):
                cp.wait()

            @pl.when(nxt_ref[0] >= 0)
            def _():
                for cp in weight_copies(nxt_ref[0], 1):
                    cp.start()

            w1b[...] = w1s[0].astype(BF16)
            w3b[...] = w3s[0].astype(BF16)
            w2b[...] = w2s[0].astype(BF16)
            up_project(0)

        @pl.when((j >= 1) & (first_ref[j] == 1))
        def _():
            w2b[...] = w2s[ws_ref[j]].astype(BF16)

        @pl.when((j + 1 < nu) & (first_ref[jn] == 1))
        def _():
            wslot = ws_ref[jn]
            for cp in weight_copies(be_ref[jn], wslot):
                cp.wait()
            e_next = nxt_ref[jn]

            @pl.when(e_next >= 0)
            def _():
                for cp in weight_copies(e_next, 1 - wslot):
                    cp.start()

            w1b[...] = w1s[wslot].astype(BF16)
            w3b[...] = w3s[wslot].astype(BF16)

        @pl.when(j >= 2)
        def _():
            scatter_wait(s)

        for slot in range(2):
            @pl.when(s == slot)
            def _():
                gather_tile(jnp.minimum(j + 2, n_tiles - 1), slot)
                a = abuf[slot][0][...]
                b = abuf[slot][1][...]
                stage_rows(jn, 1 - slot)
                up_project(1 - slot)
                hmid = (a / (1.0 + jnp.exp(-a)) * b).astype(BF16)
                y = jnp.dot(hmid, w2b[...], preferred_element_type=F32)
                for c in range(ROW_SUB):
                    ybuf[pl.ds(slot * (tr * ROW_SUB) + c, tr, stride=ROW_SUB), :] = y[:, c * LANES:(c + 1) * LANES]
                for i in range(tr):
                    scatter_row(slot, i).start()

        @pl.when(j == nu - 1)
        def _():
            scatter_wait(s)

            @pl.when(j >= 1)
            def _():
                scatter_wait(1 - s)


def _moe(blk_exp, n_used, first, nxt, wslot, dword, h2p, w1, w3, w2):
    T = h2p.shape[0]
    tr = TR_MOE
    n_tiles = blk_exp.shape[0]
    R = n_tiles * tr
    assert R <= 1 << 16
    fill = 2 * T + jnp.arange(R, dtype=I32) % (2 * tr)
    any_spec = pl.BlockSpec(memory_space=pl.ANY)
    return pl.pallas_call(
        _moe_kernel,
        grid_spec=pltpu.PrefetchScalarGridSpec(
            num_scalar_prefetch=6,
            grid=(n_tiles,),
            in_specs=[any_spec] * 5,
            out_specs=any_spec,
            scratch_shapes=[pltpu.VMEM((T, PACKED), U32),
                            pltpu.VMEM((tr, PACKED), U32),
                            pltpu.VMEM((tr, PACKED), U32),
                            pltpu.VMEM((tr, EXPERT_HIDDEN), F32),
                            pltpu.VMEM((tr, EXPERT_HIDDEN), F32),
                            pltpu.VMEM((tr, EXPERT_HIDDEN), F32),
                            pltpu.VMEM((tr, EXPERT_HIDDEN), F32),
                            pltpu.VMEM((2 * tr * ROW_SUB, LANES), F32),
                            pltpu.VMEM((2, D_MODEL, EXPERT_HIDDEN), F32),
                            pltpu.VMEM((2, D_MODEL, EXPERT_HIDDEN), F32),
                            pltpu.VMEM((2, EXPERT_HIDDEN, D_MODEL), F32),
                            pltpu.VMEM((D_MODEL, EXPERT_HIDDEN), BF16),
                            pltpu.VMEM((D_MODEL, EXPERT_HIDDEN), BF16),
                            pltpu.VMEM((EXPERT_HIDDEN, D_MODEL), BF16),
                            pltpu.SMEM((R,), I32),
                            pltpu.SMEM((2 * tr,), I32),
                            pltpu.SemaphoreType.DMA(()),
                            pltpu.SemaphoreType.DMA((2,)),
                            pltpu.SemaphoreType.DMA((2,))]),
        out_shape=jax.ShapeDtypeStruct(((2 * T + 2 * tr) * ROW_SUB, LANES), F32),
        compiler_params=pltpu.CompilerParams(dimension_semantics=("arbitrary",), vmem_limit_bytes=MOE_VMEM_LIMIT),
        name="moe_experts",
    )(blk_exp, n_used, first, nxt, wslot, dword, h2p, w1, w3, w2, fill)


def _combine_kernel(x2_ref, rinfo_ref, y1_ref, y2_ref, o_ref):
    g1 = rinfo_ref[:, 4:5]
    g2 = rinfo_ref[:, 5:6]
    tc = x2_ref.shape[0]
    for c in range(ROW_SUB):
        cols = slice(c * LANES, (c + 1) * LANES)
        rows = pl.ds(c, tc, stride=ROW_SUB)
        o_ref[:, cols] = x2_ref[:, cols] + (g1 * y1_ref[rows, :] + g2 * y2_ref[rows, :])


def _combine(x2, rinfo, y2slot):
    T = x2.shape[0]
    tc = TC_COMBINE
    return pl.pallas_call(
        _combine_kernel,
        grid=(T // tc,),
        in_specs=[pl.BlockSpec((tc, D_MODEL), lambda i: (i, 0)),
                  pl.BlockSpec((tc, LANES), lambda i: (i, 0)),
                  pl.BlockSpec((tc * ROW_SUB, LANES), lambda i: (i, 0)),
                  pl.BlockSpec((tc * ROW_SUB, LANES), lambda i: (T // tc + i, 0))],
        out_specs=pl.BlockSpec((tc, D_MODEL), lambda i: (i, 0)),
        out_shape=jax.ShapeDtypeStruct((T, D_MODEL), F32),
        compiler_params=_cparams(("parallel",)),
        name="combine",
    )(x2, rinfo, y2slot, y2slot)


def _layer(x, mem, attn_norm, w_in, q_norm, k_norm, pool_proj, pool_scale, mem_norm, w_mem_kv,
           mq_norm, mk_norm, w_out, ffn_norm, w_group, b_group, w_router, b_router, w1, w3, w2):
    B, S, D = x.shape
    T = B * S
    assert D == D_MODEL and S % SUPER == 0 and T % TM_PROJ == 0
    x2d = x.reshape(T, D)
    row = lambda v: v.reshape(1, -1).astype(F32)
    scale = HEAD_DIM ** -0.5
    gq = row(jnp.tile(q_norm, ATTN_WIDTH // HEAD_DIM) * scale)
    gk = row(jnp.tile(k_norm, ATTN_WIDTH // HEAD_DIM))
    gmq = row(jnp.tile(mq_norm, MEM_WIDTH // HEAD_DIM) * scale)
    gmk = row(jnp.tile(mk_norm, MEM_WIDTH // HEAD_DIM))

    (q1, k1, v1, q4, k4, v4, q16, k16, v16, u, qm) = _proj(
        x2d, row(attn_norm), w_in.astype(BF16), gq, gk, gmq, B, S)
    ya = _attn(q1, k1, v1, q4, k4, v4, q16, k16, v16, B, S).reshape(T, ATTN_WIDTH)
    km, vm = _memkv(mem, row(mem_norm), w_mem_kv.astype(BF16), gmk, B)

    pool_bd = jax.scipy.linalg.block_diag(*[pool_proj[g] for g in range(pool_proj.shape[0])]).astype(BF16)
    w_r = jnp.zeros((D, LANES), F32)
    w_r = w_r.at[:, :N_GROUPS].set(w_group)
    w_r = w_r.at[:, EXPERT_LANE0:].set(jnp.transpose(w_router, (1, 0, 2)).reshape(D, N_EXPERTS))
    b_r = jnp.zeros((1, LANES), F32).at[0, :N_GROUPS].set(b_group).at[0, EXPERT_LANE0:].set(b_router.reshape(-1))
    x2, h2p, rinfo, rinfot, counts = _mix(ya, u, qm, km, vm, x2d, pool_bd, row(pool_scale),
                                          w_out.astype(BF16), row(ffn_norm), w_r.astype(BF16), b_r, B, S)

    pstart, sched = _tile_schedule(counts, T)
    dest = _dest(rinfot, pstart.astype(F32).reshape(N_EXPERTS, 1), T)
    y2slot = _moe(*sched, dest[0], h2p, w1, w3, w2)
    out = _combine(x2, rinfo, y2slot)
    return out.reshape(B, S, D)


def _tile_schedule(counts, T):
    tr = TR_MOE
    n_tiles = (2 * T + N_EXPERTS * tr) // tr
    cnt = counts[:, 0].astype(I32)
    padded = ((cnt + tr - 1) // tr) * tr
    pend = jnp.cumsum(padded)
    pstart = pend - padded
    n_used = (pend[-1] // tr).astype(I32).reshape(1)
    tiles = jnp.arange(n_tiles, dtype=I32)
    tile_row = jnp.minimum(tiles, n_used[0] - 1) * tr
    blk_exp = jnp.minimum(jnp.sum(tile_row[:, None] >= pend[None, :], axis=1), N_EXPERTS - 1).astype(I32)
    in_use = tiles < n_used[0]
    first = in_use & ((tiles == 0) | (blk_exp != jnp.roll(blk_exp, 1)))
    run_idx = jnp.cumsum(first.astype(I32)) - 1
    used = cnt > 0
    exp_of_run = jnp.argsort(jnp.logical_not(used), stable=True).astype(I32)
    nxt = jnp.where(run_idx + 1 < jnp.sum(used), exp_of_run[jnp.clip(run_idx + 1, 0, N_EXPERTS - 1)], -1)
    sched = (blk_exp, n_used, first.astype(I32), nxt.astype(I32), (run_idx & 1).astype(I32))
    return pstart, sched


def kernel(x, mem, attn_norm, w_in, q_norm, k_norm, pool_proj, pool_scale, mem_norm, w_mem_kv, mq_norm, mk_norm,
           w_out, ffn_norm, w_group, b_group, w_router, b_router, w1, w3, w2):
    for l in range(attn_norm.shape[0]):
        x = _layer(x, mem, attn_norm[l], w_in[l], q_norm[l], k_norm[l], pool_proj[l], pool_scale[l],
                   mem_norm[l], w_mem_kv[l], mq_norm[l], mk_norm[l], w_out[l], ffn_norm[l],
                   w_group[l], b_group[l], w_router[l], b_router[l], w1[l], w3[l], w2[l])
    return x
```

```python
import functools

import jax
import jax.numpy as jnp
from jax import lax
from jax.experimental import pallas as pl
from jax.experimental.pallas import tpu as pltpu

F32 = jnp.float32
BF16 = jnp.bfloat16
I32 = jnp.int32
U32 = jnp.uint32

D_MODEL = 1024
HEAD_DIM = 64
ATTN_WIDTH = 512
POOL_WIDTH = 256
MEM_WIDTH = 256
N_MEM = 256
IN_WIDTH = 3 * ATTN_WIDTH + POOL_WIDTH + MEM_WIDTH
N_GROUPS = 8
EXPERTS_PER_GROUP = 8
N_EXPERTS = 64
EXPERT_HIDDEN = 512
EPS = 1e-6
NEG_INF = -1e30

LANES = 128
N_PAIRS = ATTN_WIDTH // LANES
BLOCK = 128
SUPER = 16 * BLOCK
HALO = 16
ROW_SUB = D_MODEL // LANES
PACKED = D_MODEL // 2

TM_PROJ = 1024
TM_MIX = 1024
TR_MOE = 256
TC_COMBINE = 1024
EXPERT_LANE0 = 64

VMEM_LIMIT = 48 * 1024 * 1024
MOE_VMEM_LIMIT = 56 * 1024 * 1024


def _cparams(sem):
    return pltpu.CompilerParams(dimension_semantics=sem, vmem_limit_bytes=VMEM_LIMIT)


def _group_ones(n):
    g = jnp.arange(n) // HEAD_DIM
    return (g[:, None] == g[None, :]).astype(BF16)


def _head_norm(z, ones_ref, gain_ref):
    ss = jnp.dot((z * z).astype(BF16), ones_ref[...], preferred_element_type=F32)
    return z * lax.rsqrt(ss * (1.0 / HEAD_DIM) + EPS) * gain_ref[...]


def _proj_kernel(x_ref, an_ref, win_ref, gq_ref, gk_ref, gm_ref, o512_ref, o256_ref,
                 q1_ref, k1_ref, v1_ref, q4_ref, k4_ref, v4_ref, q16_ref, k16_ref, v16_ref,
                 u_ref, qm_ref, zs_ref, z4_ref):
    tm = x_ref.shape[0]
    x = x_ref[...]
    ms = jnp.mean(x * x, axis=-1, keepdims=True)
    h = (x * lax.rsqrt(ms + EPS) * an_ref[...]).astype(BF16)
    a0, a1, a2, a3 = ATTN_WIDTH, 2 * ATTN_WIDTH, 3 * ATTN_WIDTH, 3 * ATTN_WIDTH + POOL_WIDTH

    def cols(lo, hi):
        return jnp.dot(h, win_ref[:, lo:hi], preferred_element_type=F32)

    groups = ((lambda: _head_norm(cols(0, a0), o512_ref, gq_ref), q1_ref, q4_ref, q16_ref),
              (lambda: _head_norm(cols(a0, a1), o512_ref, gk_ref), k1_ref, k4_ref, k16_ref),
              (lambda: cols(a1, a2), v1_ref, v4_ref, v16_ref))
    for make, o1, o4, o16 in groups:
        val = make()
        for hp in range(N_PAIRS):
            pair = val[:, hp * LANES:(hp + 1) * LANES]
            zs_ref[hp] = pair
            o1[hp] = pair.astype(BF16)
        q4n = tm // 4
        for hp in range(N_PAIRS):
            for r4 in range(4):
                rows = zs_ref[hp, pl.ds(r4, q4n, stride=4), :]
                z4_ref[hp, pl.ds(r4 * q4n, q4n), :] = rows
                o4[hp, :, r4 * LANES:(r4 + 1) * LANES] = rows.astype(BF16)
            for r16 in range(16):
                r4, a = r16 % 4, r16 // 4
                rows = z4_ref[hp, pl.ds(r4 * q4n + a, tm // 16, stride=4), :]
                o16[hp, :, r16 * LANES:(r16 + 1) * LANES] = rows.astype(BF16)
    u_ref[...] = cols(a2, a3)
    qm_ref[...] = _head_norm(cols(a3, IN_WIDTH), o256_ref, gm_ref).astype(BF16)


def _proj(x2d, attn_norm, w_in, gq, gk, gm, B, S):
    T = B * S
    tm = TM_PROJ
    nj = S // tm
    const = lambda i: (0, 0)

    def lay(d):
        return jax.ShapeDtypeStruct((B, N_PAIRS, S // d, d * LANES), BF16)

    def lay_spec(d):
        return pl.BlockSpec((None, N_PAIRS, tm // d, d * LANES), lambda i: (i // nj, 0, i % nj, 0))

    out_shape = [lay(1)] * 3 + [lay(4)] * 3 + [lay(16)] * 3 + [
        jax.ShapeDtypeStruct((T, POOL_WIDTH), F32), jax.ShapeDtypeStruct((T, MEM_WIDTH), BF16)]
    out_specs = [lay_spec(1)] * 3 + [lay_spec(4)] * 3 + [lay_spec(16)] * 3 + [
        pl.BlockSpec((tm, POOL_WIDTH), lambda i: (i, 0)), pl.BlockSpec((tm, MEM_WIDTH), lambda i: (i, 0))]
    return pl.pallas_call(
        _proj_kernel,
        grid=(T // tm,),
        in_specs=[pl.BlockSpec((tm, D_MODEL), lambda i: (i, 0)),
                  pl.BlockSpec((1, D_MODEL), const),
                  pl.BlockSpec((D_MODEL, IN_WIDTH), const),
                  pl.BlockSpec((1, ATTN_WIDTH), const),
                  pl.BlockSpec((1, ATTN_WIDTH), const),
                  pl.BlockSpec((1, MEM_WIDTH), const),
                  pl.BlockSpec((ATTN_WIDTH, ATTN_WIDTH), const),
                  pl.BlockSpec((MEM_WIDTH, MEM_WIDTH), const)],
        out_specs=out_specs,
        out_shape=out_shape,
        scratch_shapes=[pltpu.VMEM((N_PAIRS, tm, LANES), F32)] * 2,
        compiler_params=_cparams(("parallel",)),
        name="proj",
    )(x2d, attn_norm, w_in, gq, gk, gm, _group_ones(ATTN_WIDTH), _group_ones(MEM_WIDTH))


def _attn_kernel(q1, k1c, k1p, v1c, v1p, q4, k4c, k4p, v4c, v4p, q16, k16c, k16p, v16c, v16p,
                 bias_ref, o_ref, obuf, mbuf, lbuf):
    c = pl.program_id(2)
    lane = lax.broadcasted_iota(I32, (BLOCK, LANES), 1)
    is_a = lane < HEAD_DIM
    lane_row = lax.broadcasted_iota(I32, (1, LANES), 1)
    mask_a = jnp.where(lane_row < HEAD_DIM, 1.0, 0.0).astype(BF16)
    mask_b = jnp.where(lane_row < HEAD_DIM, 0.0, 1.0).astype(BF16)
    bias_full = bias_ref[0]
    bias_first = jnp.where(c > 0, bias_full, bias_ref[1])

    def tile(q_t, kp_t, kc_t, vp_t, vc_t, bias):
        lhs = jnp.concatenate([q_t * mask_a, q_t * mask_b], axis=0)
        keys = jnp.concatenate([kp_t, kc_t], axis=0)
        s = lax.dot_general(lhs, keys, (((1,), (1,)), ((), ())), preferred_element_type=F32) + bias
        m = jnp.max(s, axis=-1, keepdims=True)
        p = jnp.exp(s - m)
        l = jnp.sum(p, axis=-1, keepdims=True)
        vals = jnp.concatenate([vp_t, vc_t], axis=0)
        pv = jnp.dot(p.astype(BF16), vals, preferred_element_type=F32)
        return (jnp.where(is_a, pv[:BLOCK], pv[BLOCK:]), jnp.where(is_a, m[:BLOCK], m[BLOCK:]),
                jnp.where(is_a, l[:BLOCK], l[BLOCK:]))

    for pat, (d, q, kc, kp, vc, vp) in enumerate(((16, q16, k16c, k16p, v16c, v16p), (4, q4, k4c, k4p, v4c, v4p))):
        nblk = SUPER // (BLOCK * d)
        for r in range(d):
            cols = pl.ds(r * LANES, LANES)
            for jb in range(nblk):
                cur = pl.ds(jb * BLOCK, BLOCK)
                if jb == 0:
                    kp_t, vp_t, bias = kp[:, cols], vp[:, cols], bias_first
                else:
                    prev = pl.ds((jb - 1) * BLOCK, BLOCK)
                    kp_t, vp_t, bias = kc[prev, cols], vc[prev, cols], bias_full
                o_t, m_t, l_t = tile(q[cur, cols], kp_t, kc[cur, cols], vp_t, vc[cur, cols], bias)
                rows = pl.ds(jb * BLOCK * d + r, BLOCK, stride=d)
                obuf[pat, rows, :] = o_t
                mbuf[pat, rows, :] = m_t
                lbuf[pat, rows, :] = l_t
    for jb in range(SUPER // BLOCK):
        cur = pl.ds(jb * BLOCK, BLOCK)
        if jb == 0:
            kp_t, vp_t, bias = k1p[...], v1p[...], bias_first
        else:
            prev = pl.ds((jb - 1) * BLOCK, BLOCK)
            kp_t, vp_t, bias = k1c[prev, :], v1c[prev, :], bias_full
        o0, m0, l0 = tile(q1[cur, :], kp_t, k1c[cur, :], vp_t, v1c[cur, :], bias)
        m1, m2 = mbuf[0, cur, :], mbuf[1, cur, :]
        top = jnp.maximum(jnp.maximum(m0, m1), m2)
        w0, w1, w2 = jnp.exp(m0 - top), jnp.exp(m1 - top), jnp.exp(m2 - top)
        num = w0 * o0 + w1 * obuf[0, cur, :] + w2 * obuf[1, cur, :]
        den = w0 * l0 + w1 * lbuf[0, cur, :] + w2 * lbuf[1, cur, :]
        o_ref[cur, :] = (num * (1.0 / den)).astype(BF16)


def _band_bias():
    qi = jnp.arange(BLOCK)[:, None]
    kj = jnp.arange(2 * BLOCK)[None, :]
    dist = qi + BLOCK - kj
    in_band = (dist >= 0) & (dist <= BLOCK)
    full = jnp.where(in_band, 0.0, NEG_INF).astype(F32)
    first = jnp.where(in_band & (kj >= BLOCK), 0.0, NEG_INF).astype(F32)
    return jnp.stack([jnp.tile(full, (2, 1)), jnp.tile(first, (2, 1))])


def _attn(q1, k1, v1, q4, k4, v4, q16, k16, v16, B, S):
    nsup = S // SUPER

    def specs(d):
        rows = SUPER // d
        per = rows // BLOCK
        cur = pl.BlockSpec((None, None, rows, d * LANES), lambda b, hp, c: (b, hp, c, 0))
        prev = pl.BlockSpec((None, None, BLOCK, d * LANES),
                            lambda b, hp, c: (b, hp, jnp.maximum(per * c - 1, 0), 0))
        return [cur, cur, prev, cur, prev]

    return pl.pallas_call(
        _attn_kernel,
        grid=(B, N_PAIRS, nsup),
        in_specs=specs(1) + specs(4) + specs(16) + [
            pl.BlockSpec((2, 2 * BLOCK, 2 * BLOCK), lambda b, hp, c: (0, 0, 0))],
        out_specs=pl.BlockSpec((None, SUPER, LANES), lambda b, hp, c: (b, c, hp)),
        out_shape=jax.ShapeDtypeStruct((B, S, ATTN_WIDTH), BF16),
        scratch_shapes=[pltpu.VMEM((2, SUPER, LANES), F32)] * 3,
        compiler_params=_cparams(("parallel", "parallel", "parallel")),
        name="dilated_attn",
    )(q1, k1, k1, v1, v1, q4, k4, k4, v4, v4, q16, k16, k16, v16, v16, _band_bias())


def _memkv_kernel(mem_ref, mn_ref, wkv_ref, gk_ref, o256_ref, km_ref, vm_ref):
    m = mem_ref[...]
    ms = jnp.mean(m * m, axis=-1, keepdims=True)
    mn = (m * lax.rsqrt(ms + EPS) * mn_ref[...]).astype(BF16)
    kv = jnp.dot(mn, wkv_ref[...], preferred_element_type=F32)
    km_ref[...] = _head_norm(kv[:, :MEM_WIDTH], o256_ref, gk_ref).astype(BF16)
    vm_ref[...] = kv[:, MEM_WIDTH:].astype(BF16)


def _memkv(mem, mem_norm, w_mem_kv, gmk, B):
    const = lambda b: (0, 0)
    return pl.pallas_call(
        _memkv_kernel,
        grid=(B,),
        in_specs=[pl.BlockSpec((None, N_MEM, D_MODEL), lambda b: (b, 0, 0)),
                  pl.BlockSpec((1, D_MODEL), const),
                  pl.BlockSpec((D_MODEL, 2 * MEM_WIDTH), const),
                  pl.BlockSpec((1, MEM_WIDTH), const),
                  pl.BlockSpec((MEM_WIDTH, MEM_WIDTH), const)],
        out_specs=[pl.BlockSpec((None, N_MEM, MEM_WIDTH), lambda b: (b, 0, 0))] * 2,
        out_shape=[jax.ShapeDtypeStruct((B, N_MEM, MEM_WIDTH), BF16)] * 2,
        compiler_params=_cparams(("parallel",)),
        name="memkv",
    )(mem, mem_norm, w_mem_kv, gmk, _group_ones(MEM_WIDTH))


def _mix_kernel(tiles_per_seq, ya_ref, u_ref, uh_ref, qm_ref, km_ref, vm_ref, x_ref, pp_ref, ps_ref,
                wo_ref, fn_ref, wr_ref, br_ref, utri_ref,
                x2_ref, h2p_ref, rinfo_ref, rinfot_ref, cnt_ref, base_ref):
    i = pl.program_id(0)
    tm = x_ref.shape[0]
    seq_tile = i % tiles_per_seq

    @pl.when(i == 0)
    def _():
        base_ref[...] = jnp.zeros_like(base_ref)

    u = u_ref[...]
    halo = jnp.where(seq_tile == 0, 0.0, uh_ref[...])
    uu = jnp.concatenate([halo, u], axis=0)
    a1 = uu[1:] + uu[:-1]
    a2 = a1[2:] + a1[:-2]
    a3 = a2[4:] + a2[:-4]
    a4 = a3[8:] + a3[:-8]
    lane_p = lax.broadcasted_iota(I32, (tm, POOL_WIDTH), 1)
    g0, g1, g2 = lane_p < 64, lane_p < 128, lane_p < 192
    wsum = jnp.where(g0, a1[15:], jnp.where(g1, a2[13:], jnp.where(g2, a3[9:], a4[1:])))
    wlen = jnp.where(g0, 2.0, jnp.where(g1, 4.0, jnp.where(g2, 8.0, 16.0)))
    tpos = seq_tile * tm + lax.broadcasted_iota(I32, (tm, POOL_WIDTH), 0) + 1
    cnt = jnp.minimum(tpos.astype(F32), wlen)
    pooled = wsum / cnt - u
    y_pool = jnp.dot(pooled.astype(BF16), pp_ref[...], preferred_element_type=F32) * ps_ref[...]

    lane = lax.broadcasted_iota(I32, (tm, LANES), 1)
    is_a = lane < HEAD_DIM
    lane_row = lax.broadcasted_iota(I32, (1, LANES), 1)
    mask_a = jnp.where(lane_row < HEAD_DIM, 1.0, 0.0).astype(BF16)
    mask_b = jnp.where(lane_row < HEAD_DIM, 0.0, 1.0).astype(BF16)
    y_mem = []
    for pr in range(MEM_WIDTH // LANES):
        cols = slice(pr * LANES, (pr + 1) * LANES)
        qp, kp, vp = qm_ref[:, cols], km_ref[:, cols], vm_ref[:, cols]
        outs = []
        for msk in (mask_a, mask_b):
            s = lax.dot_general(qp * msk, kp, (((1,), (1,)), ((), ())), preferred_element_type=F32)
            m = jnp.max(s, axis=-1, keepdims=True)
            p = jnp.exp(s - m)
            p = p / jnp.sum(p, axis=-1, keepdims=True)
            outs.append(jnp.dot(p.astype(BF16), vp, preferred_element_type=F32))
        y_mem.append(jnp.where(is_a, outs[0], outs[1]))

    rest = jnp.concatenate([y_pool] + y_mem, axis=1).astype(BF16)
    proj = jnp.dot(ya_ref[...], wo_ref[:ATTN_WIDTH, :], preferred_element_type=F32)
    proj += jnp.dot(rest, wo_ref[ATTN_WIDTH:, :], preferred_element_type=F32)
    x2 = x_ref[...] + proj
    x2_ref[...] = x2

    ms = jnp.mean(x2 * x2, axis=-1, keepdims=True)
    h2 = (x2 * lax.rsqrt(ms + EPS) * fn_ref[...]).astype(BF16)
    bits = lax.bitcast_convert_type(h2.astype(F32), U32)
    h2p_ref[...] = bits[:, :PACKED] | (bits[:, PACKED:] >> 16)

    logits = jnp.dot(h2, wr_ref[...], preferred_element_type=F32) + br_ref[...]
    lt = jnp.transpose(logits)
    ninf = -jnp.inf
    sub8 = lax.broadcasted_iota(I32, (EXPERTS_PER_GROUP, tm), 0).astype(F32)
    lg = lt[:N_GROUPS]
    mg = jnp.max(lg, axis=0, keepdims=True)
    g_sel = jnp.min(jnp.where(lg == mg, sub8, float(N_GROUPS)), axis=0, keepdims=True)
    w_g = 1.0 / jnp.sum(jnp.exp(lg - mg), axis=0, keepdims=True)
    le = lt[EXPERT_LANE0:EXPERT_LANE0 + EXPERTS_PER_GROUP]
    for g in range(1, N_GROUPS):
        lo_g = EXPERT_LANE0 + g * EXPERTS_PER_GROUP
        le = jnp.where(g_sel == float(g), lt[lo_g:lo_g + EXPERTS_PER_GROUP], le)
    v1 = jnp.max(le, axis=0, keepdims=True)
    i1 = jnp.min(jnp.where(le == v1, sub8, float(EXPERTS_PER_GROUP)), axis=0, keepdims=True)
    le2 = jnp.where(sub8 == i1, ninf, le)
    v2 = jnp.max(le2, axis=0, keepdims=True)
    i2 = jnp.min(jnp.where(le2 == v2, sub8, float(EXPERTS_PER_GROUP)), axis=0, keepdims=True)
    e21 = jnp.exp(v2 - v1)
    gate1 = w_g / (1.0 + e21)
    gate2 = w_g * e21 / (1.0 + e21)
    e1 = g_sel * float(EXPERTS_PER_GROUP) + i1
    e2 = g_sel * float(EXPERTS_PER_GROUP) + i2

    sub64 = lax.broadcasted_iota(I32, (N_EXPERTS, tm), 0).astype(F32)
    hot1 = sub64 == e1
    hot2 = sub64 == e2
    onehot = jnp.where(hot1 | hot2, 1.0, 0.0)
    prefix = jnp.dot(onehot.astype(BF16), utri_ref[...], preferred_element_type=F32)
    tot = prefix + base_ref[...]
    rank1 = jnp.sum(jnp.where(hot1, tot, 0.0), axis=0, keepdims=True)
    rank2 = jnp.sum(jnp.where(hot2, tot, 0.0), axis=0, keepdims=True)
    base_new = base_ref[...] + jnp.sum(onehot, axis=1, keepdims=True)
    base_ref[...] = base_new
    cnt_ref[...] = base_new

    rt = jnp.where(sub8 == 0.0, e1, jnp.where(sub8 == 1.0, e2, jnp.where(sub8 == 2.0, rank1, jnp.where(
        sub8 == 3.0, rank2, jnp.where(sub8 == 4.0, gate1, jnp.where(sub8 == 5.0, gate2, 0.0))))))
    rinfot_ref[...] = rt
    rinfo_ref[...] = jnp.transpose(jnp.concatenate([rt, jnp.zeros((LANES - 8, tm), F32)], axis=0))


def _mix(ya, u, qm, km, vm, x2d, pool_bd, pool_scale, w_out, ffn_norm, w_r, b_r, B, S):
    T = B * S
    tm = TM_MIX
    tps = S // tm
    hb = tm // HALO
    const = lambda i: (0, 0)
    utri = (jnp.arange(tm)[:, None] < jnp.arange(tm)[None, :]).astype(BF16)
    return pl.pallas_call(
        functools.partial(_mix_kernel, tps),
        grid=(T // tm,),
        in_specs=[pl.BlockSpec((tm, ATTN_WIDTH), lambda i: (i, 0)),
                  pl.BlockSpec((tm, POOL_WIDTH), lambda i: (i, 0)),
                  pl.BlockSpec((HALO, POOL_WIDTH), lambda i: (jnp.maximum(i * hb - 1, 0), 0)),
                  pl.BlockSpec((tm, MEM_WIDTH), lambda i: (i, 0)),
                  pl.BlockSpec((None, N_MEM, MEM_WIDTH), lambda i: (i // tps, 0, 0)),
                  pl.BlockSpec((None, N_MEM, MEM_WIDTH), lambda i: (i // tps, 0, 0)),
                  pl.BlockSpec((tm, D_MODEL), lambda i: (i, 0)),
                  pl.BlockSpec((POOL_WIDTH, POOL_WIDTH), const),
                  pl.BlockSpec((1, POOL_WIDTH), const),
                  pl.BlockSpec((D_MODEL, D_MODEL), const),
                  pl.BlockSpec((1, D_MODEL), const),
                  pl.BlockSpec((D_MODEL, LANES), const),
                  pl.BlockSpec((1, LANES), const),
                  pl.BlockSpec((tm, tm), const)],
        out_specs=[pl.BlockSpec((tm, D_MODEL), lambda i: (i, 0)),
                   pl.BlockSpec((tm, PACKED), lambda i: (i, 0)),
                   pl.BlockSpec((tm, LANES), lambda i: (i, 0)),
                   pl.BlockSpec((8, tm), lambda i: (0, i)),
                   pl.BlockSpec((N_EXPERTS, 1), const)],
        out_shape=[jax.ShapeDtypeStruct((T, D_MODEL), F32),
                   jax.ShapeDtypeStruct((T, PACKED), U32),
                   jax.ShapeDtypeStruct((T, LANES), F32),
                   jax.ShapeDtypeStruct((8, T), F32),
                   jax.ShapeDtypeStruct((N_EXPERTS, 1), F32)],
        scratch_shapes=[pltpu.VMEM((N_EXPERTS, 1), F32)],
        compiler_params=_cparams(("arbitrary",)),
        name="mix_router",
    )(ya, u, u, qm, km, vm, x2d, pool_bd, pool_scale, w_out, ffn_norm, w_r, b_r, utri)


def _dest_kernel(rt_ref, ps_ref, d_ref):
    tn = rt_ref.shape[1]
    sub = lax.broadcasted_iota(I32, (N_EXPERTS, tn), 0).astype(F32)
    ps = ps_ref[...]
    rows = []
    for k in range(2):
        e = rt_ref[k:k + 1, :]
        start = jnp.sum(jnp.where(sub == e, ps, 0.0), axis=0, keepdims=True)
        rows.append(start + rt_ref[2 + k:3 + k, :])
    word = rows[0].astype(I32) | (rows[1].astype(I32) << 16)
    d_ref[...] = jnp.concatenate([word, jnp.zeros((7, tn), I32)], axis=0)


def _dest(rinfot, pstart_col, T):
    tn = 2048
    return pl.pallas_call(
        _dest_kernel,
        grid=(T // tn,),
        in_specs=[pl.BlockSpec((8, tn), lambda i: (0, i)),
                  pl.BlockSpec((N_EXPERTS, 1), lambda i: (0, 0))],
        out_specs=pl.BlockSpec((8, tn), lambda i: (0, i)),
        out_shape=jax.ShapeDtypeStruct((8, T), I32),
        compiler_params=_cparams(("parallel",)),
        name="dest_rows",
    )(rinfot, pstart_col)


def _row_map(d_ref, inv_ref):
    T = d_ref.shape[0]
    unroll = 8

    def body(g, carry):
        t0 = g * unroll
        words = [d_ref[t0 + u] for u in range(unroll)]
        for u, w in enumerate(words):
            inv_ref[w & 0xFFFF] = t0 + u
            inv_ref[lax.shift_right_logical(w, 16)] = T + t0 + u
        return carry

    lax.fori_loop(0, T // unroll, body, 0)


def _moe_kernel(be_ref, nu_ref, first_ref, nxt_ref, ws_ref, d_ref,
                h_hbm, w1_hbm, w3_hbm, w2_hbm, fill_hbm, o_hbm,
                tab, xbuf0, xbuf1, a0, b0, a1, b1, ybuf, w1s, w3s, w2s, w1b, w3b, w2b, inv_ref, tsem, ssem, wsem):
    j = pl.program_id(0)
    nu = nu_ref[0]
    T = h_hbm.shape[0]
    tr = TR_MOE
    n_tiles = be_ref.shape[0]
    xbuf = (xbuf0, xbuf1)
    abuf = ((a0, b0), (a1, b1))

    def tile_rows(ref, row, n=1):
        return ref.at[pl.ds(pl.multiple_of(row * ROW_SUB, ROW_SUB), n * ROW_SUB)]

    def gather_tile(tile, slot):
        for i in range(tr):
            v = inv_ref[tile * tr + i]
            if T & (T - 1) == 0:
                tok = v & (T - 1)
            else:
                tok = jnp.where(v >= 2 * T, 0, jnp.where(v >= T, v - T, v))
            xbuf[slot][pl.ds(i, 1), :] = tab[pl.ds(tok, 1), :]

    def scatter_row(tile, slot, i):
        v = inv_ref[tile * tr + i]
        return pltpu.make_async_copy(tile_rows(ybuf, slot * tr + i), tile_rows(o_hbm, v), ssem.at[slot])

    def scatter_wait(slot):
        pltpu.make_async_copy(tile_rows(ybuf, slot * tr, tr), tile_rows(o_hbm, 0, tr), ssem.at[slot]).wait()

    def weight_copies(e, slot):
        return [pltpu.make_async_copy(src.at[e], dst.at[slot], wsem.at[slot])
                for src, dst in ((w1_hbm, w1s), (w3_hbm, w3s), (w2_hbm, w2s))]

    def up_project(slot):
        w = xbuf[slot][...]
        lo = lax.bitcast_convert_type(w & jnp.uint32(0xFFFF0000), F32).astype(BF16)
        hi = lax.bitcast_convert_type(w << 16, F32).astype(BF16)
        for dst, wb in zip(abuf[slot], (w1b, w3b)):
            dst[...] = (jnp.dot(lo, wb[:PACKED, :], preferred_element_type=F32)
                        + jnp.dot(hi, wb[PACKED:, :], preferred_element_type=F32))

    @pl.when(j < nu)
    def _():
        s = j & 1
        jn = jnp.minimum(j + 1, n_tiles - 1)

        @pl.when(j == 0)
        def _():
            fill = pltpu.make_async_copy(fill_hbm, inv_ref, tsem)
            fill.start()
            fill.wait()
            ybuf[...] = jnp.zeros_like(ybuf)
            scratch_rows = pltpu.make_async_copy(ybuf, tile_rows(o_hbm, 2 * T, 2 * tr), tsem)
            scratch_rows.start()
            scratch_rows.wait()
            table = pltpu.make_async_copy(h_hbm, tab, tsem)
            table.start()
            for cp in weight_copies(be_ref[0], 0):
                cp.start()
            _row_map(d_ref, inv_ref)
            table.wait()
            gather_tile(0, 0)
            gather_tile(jnp.minimum(1, n_tiles - 1), 1)
            for cp in weight_copies(be_ref[0], 0):
                cp.wait()

            @pl.when(nxt_ref[0] >= 0)
            def _():
                for cp in weight_copies(nxt_ref[0], 1):
                    cp.start()

            w1b[...] = w1s[0].astype(BF16)
            w3b[...] = w3s[0].astype(BF16)
            w2b[...] = w2s[0].astype(BF16)
            up_project(0)

        @pl.when((j >= 1) & (first_ref[j] == 1))
        def _():
            w2b[...] = w2s[ws_ref[j]].astype(BF16)

        @pl.when((j + 1 < nu) & (first_ref[jn] == 1))
        def _():
            wslot = ws_ref[jn]
            for cp in weight_copies(be_ref[jn], wslot):
                cp.wait()
            e_next = nxt_ref[jn]

            @pl.when(e_next >= 0)
            def _():
                for cp in weight_copies(e_next, 1 - wslot):
                    cp.start()

            w1b[...] = w1s[wslot].astype(BF16)
            w3b[...] = w3s[wslot].astype(BF16)

        @pl.when(j >= 2)
        def _():
            scatter_wait(s)

        for slot in range(2):
            @pl.when(s == slot)
            def _():
                gather_tile(jnp.minimum(j + 2, n_tiles - 1), slot)
                a = abuf[slot][0][...]
                b = abuf[slot][1][...]
                up_project(1 - slot)
                hmid = (a / (1.0 + jnp.exp(-a)) * b).astype(BF16)
                y = jnp.dot(hmid, w2b[...], preferred_element_type=F32)
                for c in range(ROW_SUB):
                    ybuf[pl.ds(slot * (tr * ROW_SUB) + c, tr, stride=ROW_SUB), :] = y[:, c * LANES:(c + 1) * LANES]
                for i in range(tr):
                    scatter_row(j, slot, i).start(priority=i % 2)

        @pl.when(j == nu - 1)
        def _():
            scatter_wait(s)

            @pl.when(j >= 1)
            def _():
                scatter_wait(1 - s)


def _moe(blk_exp, n_used, first, nxt, wslot, dword, h2p, w1, w3, w2):
    T = h2p.shape[0]
    tr = TR_MOE
    n_tiles = blk_exp.shape[0]
    R = n_tiles * tr
    assert R <= 1 << 16
    fill = 2 * T + jnp.arange(R, dtype=I32) % (2 * tr)
    any_spec = pl.BlockSpec(memory_space=pl.ANY)
    return pl.pallas_call(
        _moe_kernel,
        grid_spec=pltpu.PrefetchScalarGridSpec(
            num_scalar_prefetch=6,
            grid=(n_tiles,),
            in_specs=[any_spec] * 5,
            out_specs=any_spec,
            scratch_shapes=[pltpu.VMEM((T, PACKED), U32),
                            pltpu.VMEM((tr, PACKED), U32),
                            pltpu.VMEM((tr, PACKED), U32),
                            pltpu.VMEM((tr, EXPERT_HIDDEN), F32),
                            pltpu.VMEM((tr, EXPERT_HIDDEN), F32),
                            pltpu.VMEM((tr, EXPERT_HIDDEN), F32),
                            pltpu.VMEM((tr, EXPERT_HIDDEN), F32),
                            pltpu.VMEM((2 * tr * ROW_SUB, LANES), F32),
                            pltpu.VMEM((2, D_MODEL, EXPERT_HIDDEN), F32),
                            pltpu.VMEM((2, D_MODEL, EXPERT_HIDDEN), F32),
                            pltpu.VMEM((2, EXPERT_HIDDEN, D_MODEL), F32),
                            pltpu.VMEM((D_MODEL, EXPERT_HIDDEN), BF16),
                            pltpu.VMEM((D_MODEL, EXPERT_HIDDEN), BF16),
                            pltpu.VMEM((EXPERT_HIDDEN, D_MODEL), BF16),
                            pltpu.SMEM((R,), I32),
                            pltpu.SemaphoreType.DMA(()),
                            pltpu.SemaphoreType.DMA((2,)),
                            pltpu.SemaphoreType.DMA((2,))]),
        out_shape=jax.ShapeDtypeStruct(((2 * T + 2 * tr) * ROW_SUB, LANES), F32),
        compiler_params=pltpu.CompilerParams(dimension_semantics=("arbitrary",), vmem_limit_bytes=MOE_VMEM_LIMIT),
        name="moe_experts",
    )(blk_exp, n_used, first, nxt, wslot, dword, h2p, w1, w3, w2, fill)


def _combine_kernel(x2_ref, rinfo_ref, y1_ref, y2_ref, o_ref):
    g1 = rinfo_ref[:, 4:5]
    g2 = rinfo_ref[:, 5:6]
    tc = x2_ref.shape[0]
    for c in range(ROW_SUB):
        cols = slice(c * LANES, (c + 1) * LANES)
        rows = pl.ds(c, tc, stride=ROW_SUB)
        o_ref[:, cols] = x2_ref[:, cols] + (g1 * y1_ref[rows, :] + g2 * y2_ref[rows, :])


def _combine(x2, rinfo, y2slot):
    T = x2.shape[0]
    tc = TC_COMBINE
    return pl.pallas_call(
        _combine_kernel,
        grid=(T // tc,),
        in_specs=[pl.BlockSpec((tc, D_MODEL), lambda i: (i, 0)),
                  pl.BlockSpec((tc, LANES), lambda i: (i, 0)),
                  pl.BlockSpec((tc * ROW_SUB, LANES), lambda i: (i, 0)),
                  pl.BlockSpec((tc * ROW_SUB, LANES), lambda i: (T // tc + i, 0))],
        out_specs=pl.BlockSpec((tc, D_MODEL), lambda i: (i, 0)),
        out_shape=jax.ShapeDtypeStruct((T, D_MODEL), F32),
        compiler_params=_cparams(("parallel",)),
        name="combine",
    )(x2, rinfo, y2slot, y2slot)


def _layer(x, mem, attn_norm, w_in, q_norm, k_norm, pool_proj, pool_scale, mem_norm, w_mem_kv,
           mq_norm, mk_norm, w_out, ffn_norm, w_group, b_group, w_router, b_router, w1, w3, w2):
    B, S, D = x.shape
    T = B * S
    assert D == D_MODEL and S % SUPER == 0 and T % TM_PROJ == 0
    x2d = x.reshape(T, D)
    row = lambda v: v.reshape(1, -1).astype(F32)
    scale = HEAD_DIM ** -0.5
    gq = row(jnp.tile(q_norm, ATTN_WIDTH // HEAD_DIM) * scale)
    gk = row(jnp.tile(k_norm, ATTN_WIDTH // HEAD_DIM))
    gmq = row(jnp.tile(mq_norm, MEM_WIDTH // HEAD_DIM) * scale)
    gmk = row(jnp.tile(mk_norm, MEM_WIDTH // HEAD_DIM))

    (q1, k1, v1, q4, k4, v4, q16, k16, v16, u, qm) = _proj(
        x2d, row(attn_norm), w_in.astype(BF16), gq, gk, gmq, B, S)
    ya = _attn(q1, k1, v1, q4, k4, v4, q16, k16, v16, B, S).reshape(T, ATTN_WIDTH)
    km, vm = _memkv(mem, row(mem_norm), w_mem_kv.astype(BF16), gmk, B)

    pool_bd = jax.scipy.linalg.block_diag(*[pool_proj[g] for g in range(pool_proj.shape[0])]).astype(BF16)
    w_r = jnp.zeros((D, LANES), F32)
    w_r = w_r.at[:, :N_GROUPS].set(w_group)
    w_r = w_r.at[:, EXPERT_LANE0:].set(jnp.transpose(w_router, (1, 0, 2)).reshape(D, N_EXPERTS))
    b_r = jnp.zeros((1, LANES), F32).at[0, :N_GROUPS].set(b_group).at[0, EXPERT_LANE0:].set(b_router.reshape(-1))
    x2, h2p, rinfo, rinfot, counts = _mix(ya, u, qm, km, vm, x2d, pool_bd, row(pool_scale),
                                          w_out.astype(BF16), row(ffn_norm), w_r.astype(BF16), b_r, B, S)

    pstart, sched = _tile_schedule(counts, T)
    dest = _dest(rinfot, pstart.astype(F32).reshape(N_EXPERTS, 1), T)
    y2slot = _moe(*sched, dest[0], h2p, w1, w3, w2)
    out = _combine(x2, rinfo, y2slot)
    return out.reshape(B, S, D)


def _tile_schedule(counts, T):
    tr = TR_MOE
    n_tiles = (2 * T + N_EXPERTS * tr) // tr
    cnt = counts[:, 0].astype(I32)
    padded = ((cnt + tr - 1) // tr) * tr
    pend = jnp.cumsum(padded)
    pstart = pend - padded
    n_used = (pend[-1] // tr).astype(I32).reshape(1)
    tiles = jnp.arange(n_tiles, dtype=I32)
    tile_row = jnp.minimum(tiles, n_used[0] - 1) * tr
    blk_exp = jnp.minimum(jnp.sum(tile_row[:, None] >= pend[None, :], axis=1), N_EXPERTS - 1).astype(I32)
    in_use = tiles < n_used[0]
    first = in_use & ((tiles == 0) | (blk_exp != jnp.roll(blk_exp, 1)))
    run_idx = jnp.cumsum(first.astype(I32)) - 1
    used = cnt > 0
    exp_of_run = jnp.argsort(jnp.logical_not(used), stable=True).astype(I32)
    nxt = jnp.where(run_idx + 1 < jnp.sum(used), exp_of_run[jnp.clip(run_idx + 1, 0, N_EXPERTS - 1)], -1)
    sched = (blk_exp, n_used, first.astype(I32), nxt.astype(I32), (run_idx & 1).astype(I32))
    return pstart, sched


def kernel(x, mem, attn_norm, w_in, q_norm, k_norm, pool_proj, pool_scale, mem_norm, w_mem_kv, mq_norm, mk_norm,
           w_out, ffn_norm, w_group, b_group, w_router, b_router, w1, w3, w2):
    for l in range(attn_norm.shape[0]):
        x = _layer(x, mem, attn_norm[l], w_in[l], q_norm[l], k_norm[l], pool_proj[l], pool_scale[l],
                   mem_norm[l], w_mem_kv[l], mq_norm[l], mk_norm[l], w_out[l], ffn_norm[l],
                   w_group[l], b_group[l], w_router[l], b_router[l], w1[l], w3[l], w2[l])
    return x
```

```python
import functools

import jax
import jax.numpy as jnp
from jax import lax
from jax.experimental import pallas as pl
from jax.experimental.pallas import tpu as pltpu

F32 = jnp.float32
BF16 = jnp.bfloat16
I32 = jnp.int32
U32 = jnp.uint32

D_MODEL = 1024
HEAD_DIM = 64
ATTN_WIDTH = 512
POOL_WIDTH = 256
MEM_WIDTH = 256
N_MEM = 256
IN_WIDTH = 3 * ATTN_WIDTH + POOL_WIDTH + MEM_WIDTH
N_GROUPS = 8
EXPERTS_PER_GROUP = 8
N_EXPERTS = 64
EXPERT_HIDDEN = 512
EPS = 1e-6
NEG_INF = -1e30

LANES = 128
N_PAIRS = ATTN_WIDTH // LANES
BLOCK = 128
SUPER = 16 * BLOCK
HALO = 16
ROW_SUB = D_MODEL // LANES
PACKED = D_MODEL // 2

TM_PROJ = 1024
TM_MIX = 1024
TR_MOE = 256
SCATTER_CHUNK = 32
TC_COMBINE = 1024
EXPERT_LANE0 = 64

VMEM_LIMIT = 48 * 1024 * 1024
MOE_VMEM_LIMIT = 56 * 1024 * 1024


def _cparams(sem):
    return pltpu.CompilerParams(dimension_semantics=sem, vmem_limit_bytes=VMEM_LIMIT)


def _group_ones(n):
    g = jnp.arange(n) // HEAD_DIM
    return (g[:, None] == g[None, :]).astype(BF16)


def _head_norm(z, ones_ref, gain_ref):
    ss = jnp.dot((z * z).astype(BF16), ones_ref[...], preferred_element_type=F32)
    return z * lax.rsqrt(ss * (1.0 / HEAD_DIM) + EPS) * gain_ref[...]


def _proj_kernel(x_ref, an_ref, win_ref, gq_ref, gk_ref, gm_ref, o512_ref, o256_ref,
                 q1_ref, k1_ref, v1_ref, q4_ref, k4_ref, v4_ref, q16_ref, k16_ref, v16_ref,
                 u_ref, qm_ref, zs_ref, z4_ref):
    tm = x_ref.shape[0]
    x = x_ref[...]
    ms = jnp.mean(x * x, axis=-1, keepdims=True)
    h = (x * lax.rsqrt(ms + EPS) * an_ref[...]).astype(BF16)
    a0, a1, a2, a3 = ATTN_WIDTH, 2 * ATTN_WIDTH, 3 * ATTN_WIDTH, 3 * ATTN_WIDTH + POOL_WIDTH

    def cols(lo, hi):
        return jnp.dot(h, win_ref[:, lo:hi], preferred_element_type=F32)

    groups = ((lambda: _head_norm(cols(0, a0), o512_ref, gq_ref), q1_ref, q4_ref, q16_ref),
              (lambda: _head_norm(cols(a0, a1), o512_ref, gk_ref), k1_ref, k4_ref, k16_ref),
              (lambda: cols(a1, a2), v1_ref, v4_ref, v16_ref))
    for make, o1, o4, o16 in groups:
        val = make()
        for hp in range(N_PAIRS):
            pair = val[:, hp * LANES:(hp + 1) * LANES]
            zs_ref[hp] = pair
            o1[hp] = pair.astype(BF16)
        q4n = tm // 4
        for hp in range(N_PAIRS):
            for r4 in range(4):
                rows = zs_ref[hp, pl.ds(r4, q4n, stride=4), :]
                z4_ref[hp, pl.ds(r4 * q4n, q4n), :] = rows
                o4[hp, :, r4 * LANES:(r4 + 1) * LANES] = rows.astype(BF16)
            for r16 in range(16):
                r4, a = r16 % 4, r16 // 4
                rows = z4_ref[hp, pl.ds(r4 * q4n + a, tm // 16, stride=4), :]
                o16[hp, :, r16 * LANES:(r16 + 1) * LANES] = rows.astype(BF16)
    u_ref[...] = cols(a2, a3)
    qm_ref[...] = _head_norm(cols(a3, IN_WIDTH), o256_ref, gm_ref).astype(BF16)


def _proj(x2d, attn_norm, w_in, gq, gk, gm, B, S):
    T = B * S
    tm = TM_PROJ
    nj = S // tm
    const = lambda i: (0, 0)

    def lay(d):
        return jax.ShapeDtypeStruct((B, N_PAIRS, S // d, d * LANES), BF16)

    def lay_spec(d):
        return pl.BlockSpec((None, N_PAIRS, tm // d, d * LANES), lambda i: (i // nj, 0, i % nj, 0))

    out_shape = [lay(1)] * 3 + [lay(4)] * 3 + [lay(16)] * 3 + [
        jax.ShapeDtypeStruct((T, POOL_WIDTH), F32), jax.ShapeDtypeStruct((T, MEM_WIDTH), BF16)]
    out_specs = [lay_spec(1)] * 3 + [lay_spec(4)] * 3 + [lay_spec(16)] * 3 + [
        pl.BlockSpec((tm, POOL_WIDTH), lambda i: (i, 0)), pl.BlockSpec((tm, MEM_WIDTH), lambda i: (i, 0))]
    return pl.pallas_call(
        _proj_kernel,
        grid=(T // tm,),
        in_specs=[pl.BlockSpec((tm, D_MODEL), lambda i: (i, 0)),
                  pl.BlockSpec((1, D_MODEL), const),
                  pl.BlockSpec((D_MODEL, IN_WIDTH), const),
                  pl.BlockSpec((1, ATTN_WIDTH), const),
                  pl.BlockSpec((1, ATTN_WIDTH), const),
                  pl.BlockSpec((1, MEM_WIDTH), const),
                  pl.BlockSpec((ATTN_WIDTH, ATTN_WIDTH), const),
                  pl.BlockSpec((MEM_WIDTH, MEM_WIDTH), const)],
        out_specs=out_specs,
        out_shape=out_shape,
        scratch_shapes=[pltpu.VMEM((N_PAIRS, tm, LANES), F32)] * 2,
        compiler_params=_cparams(("parallel",)),
        name="proj",
    )(x2d, attn_norm, w_in, gq, gk, gm, _group_ones(ATTN_WIDTH), _group_ones(MEM_WIDTH))


def _attn_kernel(q1, k1c, k1p, v1c, v1p, q4, k4c, k4p, v4c, v4p, q16, k16c, k16p, v16c, v16p,
                 bias_ref, o_ref, obuf, mbuf, lbuf):
    c = pl.program_id(2)
    lane = lax.broadcasted_iota(I32, (BLOCK, LANES), 1)
    is_a = lane < HEAD_DIM
    lane_row = lax.broadcasted_iota(I32, (1, LANES), 1)
    mask_a = jnp.where(lane_row < HEAD_DIM, 1.0, 0.0).astype(BF16)
    mask_b = jnp.where(lane_row < HEAD_DIM, 0.0, 1.0).astype(BF16)
    bias_full = bias_ref[0]
    bias_first = jnp.where(c > 0, bias_full, bias_ref[1])

    def tile(q_t, kp_t, kc_t, vp_t, vc_t, bias):
        lhs = jnp.concatenate([q_t * mask_a, q_t * mask_b], axis=0)
        keys = jnp.concatenate([kp_t, kc_t], axis=0)
        s = lax.dot_general(lhs, keys, (((1,), (1,)), ((), ())), preferred_element_type=F32) + bias
        m = jnp.max(s, axis=-1, keepdims=True)
        p = jnp.exp(s - m)
        l = jnp.sum(p, axis=-1, keepdims=True)
        vals = jnp.concatenate([vp_t, vc_t], axis=0)
        pv = jnp.dot(p.astype(BF16), vals, preferred_element_type=F32)
        return (jnp.where(is_a, pv[:BLOCK], pv[BLOCK:]), jnp.where(is_a, m[:BLOCK], m[BLOCK:]),
                jnp.where(is_a, l[:BLOCK], l[BLOCK:]))

    for pat, (d, q, kc, kp, vc, vp) in enumerate(((16, q16, k16c, k16p, v16c, v16p), (4, q4, k4c, k4p, v4c, v4p))):
        nblk = SUPER // (BLOCK * d)
        for r in range(d):
            cols = pl.ds(r * LANES, LANES)
            for jb in range(nblk):
                cur = pl.ds(jb * BLOCK, BLOCK)
                if jb == 0:
                    kp_t, vp_t, bias = kp[:, cols], vp[:, cols], bias_first
                else:
                    prev = pl.ds((jb - 1) * BLOCK, BLOCK)
                    kp_t, vp_t, bias = kc[prev, cols], vc[prev, cols], bias_full
                o_t, m_t, l_t = tile(q[cur, cols], kp_t, kc[cur, cols], vp_t, vc[cur, cols], bias)
                rows = pl.ds(jb * BLOCK * d + r, BLOCK, stride=d)
                obuf[pat, rows, :] = o_t
                mbuf[pat, rows, :] = m_t
                lbuf[pat, rows, :] = l_t
    for jb in range(SUPER // BLOCK):
        cur = pl.ds(jb * BLOCK, BLOCK)
        if jb == 0:
            kp_t, vp_t, bias = k1p[...], v1p[...], bias_first
        else:
            prev = pl.ds((jb - 1) * BLOCK, BLOCK)
            kp_t, vp_t, bias = k1c[prev, :], v1c[prev, :], bias_full
        o0, m0, l0 = tile(q1[cur, :], kp_t, k1c[cur, :], vp_t, v1c[cur, :], bias)
        m1, m2 = mbuf[0, cur, :], mbuf[1, cur, :]
        top = jnp.maximum(jnp.maximum(m0, m1), m2)
        w0, w1, w2 = jnp.exp(m0 - top), jnp.exp(m1 - top), jnp.exp(m2 - top)
        num = w0 * o0 + w1 * obuf[0, cur, :] + w2 * obuf[1, cur, :]
        den = w0 * l0 + w1 * lbuf[0, cur, :] + w2 * lbuf[1, cur, :]
        o_ref[cur, :] = (num * (1.0 / den)).astype(BF16)


def _band_bias():
    qi = jnp.arange(BLOCK)[:, None]
    kj = jnp.arange(2 * BLOCK)[None, :]
    dist = qi + BLOCK - kj
    in_band = (dist >= 0) & (dist <= BLOCK)
    full = jnp.where(in_band, 0.0, NEG_INF).astype(F32)
    first = jnp.where(in_band & (kj >= BLOCK), 0.0, NEG_INF).astype(F32)
    return jnp.stack([jnp.tile(full, (2, 1)), jnp.tile(first, (2, 1))])


def _attn(q1, k1, v1, q4, k4, v4, q16, k16, v16, B, S):
    nsup = S // SUPER

    def specs(d):
        rows = SUPER // d
        per = rows // BLOCK
        cur = pl.BlockSpec((None, None, rows, d * LANES), lambda b, hp, c: (b, hp, c, 0))
        prev = pl.BlockSpec((None, None, BLOCK, d * LANES),
                            lambda b, hp, c: (b, hp, jnp.maximum(per * c - 1, 0), 0))
        return [cur, cur, prev, cur, prev]

    return pl.pallas_call(
        _attn_kernel,
        grid=(B, N_PAIRS, nsup),
        in_specs=specs(1) + specs(4) + specs(16) + [
            pl.BlockSpec((2, 2 * BLOCK, 2 * BLOCK), lambda b, hp, c: (0, 0, 0))],
        out_specs=pl.BlockSpec((None, SUPER, LANES), lambda b, hp, c: (b, c, hp)),
        out_shape=jax.ShapeDtypeStruct((B, S, ATTN_WIDTH), BF16),
        scratch_shapes=[pltpu.VMEM((2, SUPER, LANES), F32)] * 3,
        compiler_params=_cparams(("parallel", "parallel", "parallel")),
        name="dilated_attn",
    )(q1, k1, k1, v1, v1, q4, k4, k4, v4, v4, q16, k16, k16, v16, v16, _band_bias())


def _memkv_kernel(mem_ref, mn_ref, wkv_ref, gk_ref, o256_ref, km_ref, vm_ref):
    m = mem_ref[...]
    ms = jnp.mean(m * m, axis=-1, keepdims=True)
    mn = (m * lax.rsqrt(ms + EPS) * mn_ref[...]).astype(BF16)
    kv = jnp.dot(mn, wkv_ref[...], preferred_element_type=F32)
    km_ref[...] = _head_norm(kv[:, :MEM_WIDTH], o256_ref, gk_ref).astype(BF16)
    vm_ref[...] = kv[:, MEM_WIDTH:].astype(BF16)


def _memkv(mem, mem_norm, w_mem_kv, gmk, B):
    const = lambda b: (0, 0)
    return pl.pallas_call(
        _memkv_kernel,
        grid=(B,),
        in_specs=[pl.BlockSpec((None, N_MEM, D_MODEL), lambda b: (b, 0, 0)),
                  pl.BlockSpec((1, D_MODEL), const),
                  pl.BlockSpec((D_MODEL, 2 * MEM_WIDTH), const),
                  pl.BlockSpec((1, MEM_WIDTH), const),
                  pl.BlockSpec((MEM_WIDTH, MEM_WIDTH), const)],
        out_specs=[pl.BlockSpec((None, N_MEM, MEM_WIDTH), lambda b: (b, 0, 0))] * 2,
        out_shape=[jax.ShapeDtypeStruct((B, N_MEM, MEM_WIDTH), BF16)] * 2,
        compiler_params=_cparams(("parallel",)),
        name="memkv",
    )(mem, mem_norm, w_mem_kv, gmk, _group_ones(MEM_WIDTH))


def _mix_kernel(tiles_per_seq, ya_ref, u_ref, uh_ref, qm_ref, km_ref, vm_ref, x_ref, pp_ref, ps_ref,
                wo_ref, fn_ref, wr_ref, br_ref, utri_ref,
                x2_ref, h2p_ref, rinfo_ref, rinfot_ref, cnt_ref, base_ref):
    i = pl.program_id(0)
    tm = x_ref.shape[0]
    seq_tile = i % tiles_per_seq

    @pl.when(i == 0)
    def _():
        base_ref[...] = jnp.zeros_like(base_ref)

    u = u_ref[...]
    halo = jnp.where(seq_tile == 0, 0.0, uh_ref[...])
    uu = jnp.concatenate([halo, u], axis=0)
    a1 = uu[1:] + uu[:-1]
    a2 = a1[2:] + a1[:-2]
    a3 = a2[4:] + a2[:-4]
    a4 = a3[8:] + a3[:-8]
    lane_p = lax.broadcasted_iota(I32, (tm, POOL_WIDTH), 1)
    g0, g1, g2 = lane_p < 64, lane_p < 128, lane_p < 192
    wsum = jnp.where(g0, a1[15:], jnp.where(g1, a2[13:], jnp.where(g2, a3[9:], a4[1:])))
    wlen = jnp.where(g0, 2.0, jnp.where(g1, 4.0, jnp.where(g2, 8.0, 16.0)))
    tpos = seq_tile * tm + lax.broadcasted_iota(I32, (tm, POOL_WIDTH), 0) + 1
    cnt = jnp.minimum(tpos.astype(F32), wlen)
    pooled = wsum / cnt - u
    y_pool = jnp.dot(pooled.astype(BF16), pp_ref[...], preferred_element_type=F32) * ps_ref[...]

    lane = lax.broadcasted_iota(I32, (tm, LANES), 1)
    is_a = lane < HEAD_DIM
    lane_row = lax.broadcasted_iota(I32, (1, LANES), 1)
    mask_a = jnp.where(lane_row < HEAD_DIM, 1.0, 0.0).astype(BF16)
    mask_b = jnp.where(lane_row < HEAD_DIM, 0.0, 1.0).astype(BF16)
    y_mem = []
    for pr in range(MEM_WIDTH // LANES):
        cols = slice(pr * LANES, (pr + 1) * LANES)
        qp, kp, vp = qm_ref[:, cols], km_ref[:, cols], vm_ref[:, cols]
        outs = []
        for msk in (mask_a, mask_b):
            s = lax.dot_general(qp * msk, kp, (((1,), (1,)), ((), ())), preferred_element_type=F32)
            m = jnp.max(s, axis=-1, keepdims=True)
            p = jnp.exp(s - m)
            p = p / jnp.sum(p, axis=-1, keepdims=True)
            outs.append(jnp.dot(p.astype(BF16), vp, preferred_element_type=F32))
        y_mem.append(jnp.where(is_a, outs[0], outs[1]))

    rest = jnp.concatenate([y_pool] + y_mem, axis=1).astype(BF16)
    proj = jnp.dot(ya_ref[...], wo_ref[:ATTN_WIDTH, :], preferred_element_type=F32)
    proj += jnp.dot(rest, wo_ref[ATTN_WIDTH:, :], preferred_element_type=F32)
    x2 = x_ref[...] + proj
    x2_ref[...] = x2

    ms = jnp.mean(x2 * x2, axis=-1, keepdims=True)
    h2 = (x2 * lax.rsqrt(ms + EPS) * fn_ref[...]).astype(BF16)
    bits = lax.bitcast_convert_type(h2.astype(F32), U32)
    h2p_ref[...] = bits[:, :PACKED] | (bits[:, PACKED:] >> 16)

    logits = jnp.dot(h2, wr_ref[...], preferred_element_type=F32) + br_ref[...]
    lt = jnp.transpose(logits)
    ninf = -jnp.inf
    sub8 = lax.broadcasted_iota(I32, (EXPERTS_PER_GROUP, tm), 0).astype(F32)
    lg = lt[:N_GROUPS]
    mg = jnp.max(lg, axis=0, keepdims=True)
    g_sel = jnp.min(jnp.where(lg == mg, sub8, float(N_GROUPS)), axis=0, keepdims=True)
    w_g = 1.0 / jnp.sum(jnp.exp(lg - mg), axis=0, keepdims=True)
    le = lt[EXPERT_LANE0:EXPERT_LANE0 + EXPERTS_PER_GROUP]
    for g in range(1, N_GROUPS):
        lo_g = EXPERT_LANE0 + g * EXPERTS_PER_GROUP
        le = jnp.where(g_sel == float(g), lt[lo_g:lo_g + EXPERTS_PER_GROUP], le)
    v1 = jnp.max(le, axis=0, keepdims=True)
    i1 = jnp.min(jnp.where(le == v1, sub8, float(EXPERTS_PER_GROUP)), axis=0, keepdims=True)
    le2 = jnp.where(sub8 == i1, ninf, le)
    v2 = jnp.max(le2, axis=0, keepdims=True)
    i2 = jnp.min(jnp.where(le2 == v2, sub8, float(EXPERTS_PER_GROUP)), axis=0, keepdims=True)
    e21 = jnp.exp(v2 - v1)
    gate1 = w_g / (1.0 + e21)
    gate2 = w_g * e21 / (1.0 + e21)
    e1 = g_sel * float(EXPERTS_PER_GROUP) + i1
    e2 = g_sel * float(EXPERTS_PER_GROUP) + i2

    sub64 = lax.broadcasted_iota(I32, (N_EXPERTS, tm), 0).astype(F32)
    hot1 = sub64 == e1
    hot2 = sub64 == e2
    onehot = jnp.where(hot1 | hot2, 1.0, 0.0)
    prefix = jnp.dot(onehot.astype(BF16), utri_ref[...], preferred_element_type=F32)
    tot = prefix + base_ref[...]
    rank1 = jnp.sum(jnp.where(hot1, tot, 0.0), axis=0, keepdims=True)
    rank2 = jnp.sum(jnp.where(hot2, tot, 0.0), axis=0, keepdims=True)
    base_new = base_ref[...] + jnp.sum(onehot, axis=1, keepdims=True)
    base_ref[...] = base_new
    cnt_ref[...] = base_new

    rt = jnp.where(sub8 == 0.0, e1, jnp.where(sub8 == 1.0, e2, jnp.where(sub8 == 2.0, rank1, jnp.where(
        sub8 == 3.0, rank2, jnp.where(sub8 == 4.0, gate1, jnp.where(sub8 == 5.0, gate2, 0.0))))))
    rinfot_ref[...] = rt
    rinfo_ref[...] = jnp.transpose(jnp.concatenate([rt, jnp.zeros((LANES - 8, tm), F32)], axis=0))


def _mix(ya, u, qm, km, vm, x2d, pool_bd, pool_scale, w_out, ffn_norm, w_r, b_r, B, S):
    T = B * S
    tm = TM_MIX
    tps = S // tm
    hb = tm // HALO
    const = lambda i: (0, 0)
    utri = (jnp.arange(tm)[:, None] < jnp.arange(tm)[None, :]).astype(BF16)
    return pl.pallas_call(
        functools.partial(_mix_kernel, tps),
        grid=(T // tm,),
        in_specs=[pl.BlockSpec((tm, ATTN_WIDTH), lambda i: (i, 0)),
                  pl.BlockSpec((tm, POOL_WIDTH), lambda i: (i, 0)),
                  pl.BlockSpec((HALO, POOL_WIDTH), lambda i: (jnp.maximum(i * hb - 1, 0), 0)),
                  pl.BlockSpec((tm, MEM_WIDTH), lambda i: (i, 0)),
                  pl.BlockSpec((None, N_MEM, MEM_WIDTH), lambda i: (i // tps, 0, 0)),
                  pl.BlockSpec((None, N_MEM, MEM_WIDTH), lambda i: (i // tps, 0, 0)),
                  pl.BlockSpec((tm, D_MODEL), lambda i: (i, 0)),
                  pl.BlockSpec((POOL_WIDTH, POOL_WIDTH), const),
                  pl.BlockSpec((1, POOL_WIDTH), const),
                  pl.BlockSpec((D_MODEL, D_MODEL), const),
                  pl.BlockSpec((1, D_MODEL), const),
                  pl.BlockSpec((D_MODEL, LANES), const),
                  pl.BlockSpec((1, LANES), const),
                  pl.BlockSpec((tm, tm), const)],
        out_specs=[pl.BlockSpec((tm, D_MODEL), lambda i: (i, 0)),
                   pl.BlockSpec((tm, PACKED), lambda i: (i, 0)),
                   pl.BlockSpec((tm, LANES), lambda i: (i, 0)),
                   pl.BlockSpec((8, tm), lambda i: (0, i)),
                   pl.BlockSpec((N_EXPERTS, 1), const)],
        out_shape=[jax.ShapeDtypeStruct((T, D_MODEL), F32),
                   jax.ShapeDtypeStruct((T, PACKED), U32),
                   jax.ShapeDtypeStruct((T, LANES), F32),
                   jax.ShapeDtypeStruct((8, T), F32),
                   jax.ShapeDtypeStruct((N_EXPERTS, 1), F32)],
        scratch_shapes=[pltpu.VMEM((N_EXPERTS, 1), F32)],
        compiler_params=_cparams(("arbitrary",)),
        name="mix_router",
    )(ya, u, u, qm, km, vm, x2d, pool_bd, pool_scale, w_out, ffn_norm, w_r, b_r, utri)


def _dest_kernel(rt_ref, ps_ref, d_ref):
    tn = rt_ref.shape[1]
    sub = lax.broadcasted_iota(I32, (N_EXPERTS, tn), 0).astype(F32)
    ps = ps_ref[...]
    rows = []
    for k in range(2):
        e = rt_ref[k:k + 1, :]
        start = jnp.sum(jnp.where(sub == e, ps, 0.0), axis=0, keepdims=True)
        rows.append(start + rt_ref[2 + k:3 + k, :])
    word = rows[0].astype(I32) | (rows[1].astype(I32) << 16)
    d_ref[...] = jnp.concatenate([word, jnp.zeros((7, tn), I32)], axis=0)


def _dest(rinfot, pstart_col, T):
    tn = 2048
    return pl.pallas_call(
        _dest_kernel,
        grid=(T // tn,),
        in_specs=[pl.BlockSpec((8, tn), lambda i: (0, i)),
                  pl.BlockSpec((N_EXPERTS, 1), lambda i: (0, 0))],
        out_specs=pl.BlockSpec((8, tn), lambda i: (0, i)),
        out_shape=jax.ShapeDtypeStruct((8, T), I32),
        compiler_params=_cparams(("parallel",)),
        name="dest_rows",
    )(rinfot, pstart_col)


def _row_map(d_ref, inv_ref):
    T = d_ref.shape[0]
    unroll = 8

    def body(g, carry):
        t0 = g * unroll
        words = [d_ref[t0 + u] for u in range(unroll)]
        for u, w in enumerate(words):
            inv_ref[w & 0xFFFF] = t0 + u
            inv_ref[lax.shift_right_logical(w, 16)] = T + t0 + u
        return carry

    lax.fori_loop(0, T // unroll, body, 0)


def _moe_kernel(be_ref, nu_ref, first_ref, nxt_ref, ws_ref, nc_ref, d_ref,
                h_hbm, w1_hbm, w3_hbm, w2_hbm, fill_hbm, o_hbm,
                tab, xbuf0, xbuf1, a0, b0, a1, b1, ybuf, w1s, w3s, w2s, w1b, w3b, w2b, inv_ref, tsem, ssem, wsem):
    j = pl.program_id(0)
    nu = nu_ref[0]
    T = h_hbm.shape[0]
    tr = TR_MOE
    n_tiles = be_ref.shape[0]
    xbuf = (xbuf0, xbuf1)
    abuf = ((a0, b0), (a1, b1))

    def tile_rows(ref, row, n=1):
        return ref.at[pl.ds(pl.multiple_of(row * ROW_SUB, ROW_SUB), n * ROW_SUB)]

    def gather_tile(tile, slot):
        for i in range(tr):
            v = inv_ref[tile * tr + i]
            if T & (T - 1) == 0:
                tok = v & (T - 1)
            else:
                tok = jnp.where(v >= 2 * T, 0, jnp.where(v >= T, v - T, v))
            xbuf[slot][pl.ds(i, 1), :] = tab[pl.ds(tok, 1), :]

    def scatter_row(tile, slot, i):
        v = inv_ref[tile * tr + i]
        return pltpu.make_async_copy(tile_rows(ybuf, slot * tr + i), tile_rows(o_hbm, v), ssem.at[slot])

    def scatter_wait(tile, slot):
        for c in range(tr // SCATTER_CHUNK):
            @pl.when(c < nc_ref[tile])
            def _():
                pltpu.make_async_copy(tile_rows(ybuf, slot * tr, SCATTER_CHUNK), tile_rows(o_hbm, 0, SCATTER_CHUNK),
                                      ssem.at[slot]).wait()

    def weight_copies(e, slot):
        return [pltpu.make_async_copy(src.at[e], dst.at[slot], wsem.at[slot])
                for src, dst in ((w1_hbm, w1s), (w3_hbm, w3s), (w2_hbm, w2s))]

    def up_project(slot):
        w = xbuf[slot][...]
        lo = lax.bitcast_convert_type(w & jnp.uint32(0xFFFF0000), F32).astype(BF16)
        hi = lax.bitcast_convert_type(w << 16, F32).astype(BF16)
        for dst, wb in zip(abuf[slot], (w1b, w3b)):
            dst[...] = (jnp.dot(lo, wb[:PACKED, :], preferred_element_type=F32)
                        + jnp.dot(hi, wb[PACKED:, :], preferred_element_type=F32))

    @pl.when(j < nu)
    def _():
        s = j & 1
        jn = jnp.minimum(j + 1, n_tiles - 1)

        @pl.when(j == 0)
        def _():
            fill = pltpu.make_async_copy(fill_hbm, inv_ref, tsem)
            fill.start()
            fill.wait()
            ybuf[...] = jnp.zeros_like(ybuf)
            scratch_rows = pltpu.make_async_copy(ybuf, tile_rows(o_hbm, 2 * T, 2 * tr), tsem)
            scratch_rows.start()
            scratch_rows.wait()
            table = pltpu.make_async_copy(h_hbm, tab, tsem)
            table.start()
            for cp in weight_copies(be_ref[0], 0):
                cp.start()
            _row_map(d_ref, inv_ref)
            table.wait()
            gather_tile(0, 0)
            gather_tile(jnp.minimum(1, n_tiles - 1), 1)
            for cp in weight_copies(be_ref[0], 0):
                cp.wait()

            @pl.when(nxt_ref[0] >= 0)
            def _():
                for cp in weight_copies(nxt_ref[0], 1):
                    cp.start()

            w1b[...] = w1s[0].astype(BF16)
            w3b[...] = w3s[0].astype(BF16)
            w2b[...] = w2s[0].astype(BF16)
            up_project(0)

        @pl.when((j >= 1) & (first_ref[j] == 1))
        def _():
            w2b[...] = w2s[ws_ref[j]].astype(BF16)

        @pl.when((j + 1 < nu) & (first_ref[jn] == 1))
        def _():
            wslot = ws_ref[jn]
            for cp in weight_copies(be_ref[jn], wslot):
                cp.wait()
            e_next = nxt_ref[jn]

            @pl.when(e_next >= 0)
            def _():
                for cp in weight_copies(e_next, 1 - wslot):
                    cp.start()

            w1b[...] = w1s[wslot].astype(BF16)
            w3b[...] = w3s[wslot].astype(BF16)

        @pl.when(j >= 2)
        def _():
            scatter_wait(j - 2, s)

        for slot in range(2):
            @pl.when(s == slot)
            def _():
                gather_tile(jnp.minimum(j + 2, n_tiles - 1), slot)
                a = abuf[slot][0][...]
                b = abuf[slot][1][...]
                up_project(1 - slot)
                hmid = (a / (1.0 + jnp.exp(-a)) * b).astype(BF16)
                y = jnp.dot(hmid, w2b[...], preferred_element_type=F32)
                for c in range(ROW_SUB):
                    ybuf[pl.ds(slot * (tr * ROW_SUB) + c, tr, stride=ROW_SUB), :] = y[:, c * LANES:(c + 1) * LANES]
                for c in range(tr // SCATTER_CHUNK):
                    @pl.when(c < nc_ref[j])
                    def _():
                        for i in range(c * SCATTER_CHUNK, (c + 1) * SCATTER_CHUNK):
                            scatter_row(j, slot, i).start()

        @pl.when(j == nu - 1)
        def _():
            scatter_wait(j, s)

            @pl.when(j >= 1)
            def _():
                scatter_wait(j - 1, 1 - s)


def _moe(blk_exp, n_used, first, nxt, wslot, n_chunks, dword, h2p, w1, w3, w2):
    T = h2p.shape[0]
    tr = TR_MOE
    n_tiles = blk_exp.shape[0]
    R = n_tiles * tr
    assert R <= 1 << 16
    fill = 2 * T + jnp.arange(R, dtype=I32) % (2 * tr)
    any_spec = pl.BlockSpec(memory_space=pl.ANY)
    return pl.pallas_call(
        _moe_kernel,
        grid_spec=pltpu.PrefetchScalarGridSpec(
            num_scalar_prefetch=7,
            grid=(n_tiles,),
            in_specs=[any_spec] * 5,
            out_specs=any_spec,
            scratch_shapes=[pltpu.VMEM((T, PACKED), U32),
                            pltpu.VMEM((tr, PACKED), U32),
                            pltpu.VMEM((tr, PACKED), U32),
                            pltpu.VMEM((tr, EXPERT_HIDDEN), F32),
                            pltpu.VMEM((tr, EXPERT_HIDDEN), F32),
                            pltpu.VMEM((tr, EXPERT_HIDDEN), F32),
                            pltpu.VMEM((tr, EXPERT_HIDDEN), F32),
                            pltpu.VMEM((2 * tr * ROW_SUB, LANES), F32),
                            pltpu.VMEM((2, D_MODEL, EXPERT_HIDDEN), F32),
                            pltpu.VMEM((2, D_MODEL, EXPERT_HIDDEN), F32),
                            pltpu.VMEM((2, EXPERT_HIDDEN, D_MODEL), F32),
                            pltpu.VMEM((D_MODEL, EXPERT_HIDDEN), BF16),
                            pltpu.VMEM((D_MODEL, EXPERT_HIDDEN), BF16),
                            pltpu.VMEM((EXPERT_HIDDEN, D_MODEL), BF16),
                            pltpu.SMEM((R,), I32),
                            pltpu.SemaphoreType.DMA(()),
                            pltpu.SemaphoreType.DMA((2,)),
                            pltpu.SemaphoreType.DMA((2,))]),
        out_shape=jax.ShapeDtypeStruct(((2 * T + 2 * tr) * ROW_SUB, LANES), F32),
        compiler_params=pltpu.CompilerParams(dimension_semantics=("arbitrary",), vmem_limit_bytes=MOE_VMEM_LIMIT),
        name="moe_experts",
    )(blk_exp, n_used, first, nxt, wslot, n_chunks, dword, h2p, w1, w3, w2, fill)


def _combine_kernel(x2_ref, rinfo_ref, y1_ref, y2_ref, o_ref):
    g1 = rinfo_ref[:, 4:5]
    g2 = rinfo_ref[:, 5:6]
    tc = x2_ref.shape[0]
    for c in range(ROW_SUB):
        cols = slice(c * LANES, (c + 1) * LANES)
        rows = pl.ds(c, tc, stride=ROW_SUB)
        o_ref[:, cols] = x2_ref[:, cols] + (g1 * y1_ref[rows, :] + g2 * y2_ref[rows, :])


def _combine(x2, rinfo, y2slot):
    T = x2.shape[0]
    tc = TC_COMBINE
    return pl.pallas_call(
        _combine_kernel,
        grid=(T // tc,),
        in_specs=[pl.BlockSpec((tc, D_MODEL), lambda i: (i, 0)),
                  pl.BlockSpec((tc, LANES), lambda i: (i, 0)),
                  pl.BlockSpec((tc * ROW_SUB, LANES), lambda i: (i, 0)),
                  pl.BlockSpec((tc * ROW_SUB, LANES), lambda i: (T // tc + i, 0))],
        out_specs=pl.BlockSpec((tc, D_MODEL), lambda i: (i, 0)),
        out_shape=jax.ShapeDtypeStruct((T, D_MODEL), F32),
        compiler_params=_cparams(("parallel",)),
        name="combine",
    )(x2, rinfo, y2slot, y2slot)


def _layer(x, mem, attn_norm, w_in, q_norm, k_norm, pool_proj, pool_scale, mem_norm, w_mem_kv,
           mq_norm, mk_norm, w_out, ffn_norm, w_group, b_group, w_router, b_router, w1, w3, w2):
    B, S, D = x.shape
    T = B * S
    assert D == D_MODEL and S % SUPER == 0 and T % TM_PROJ == 0
    x2d = x.reshape(T, D)
    row = lambda v: v.reshape(1, -1).astype(F32)
    scale = HEAD_DIM ** -0.5
    gq = row(jnp.tile(q_norm, ATTN_WIDTH // HEAD_DIM) * scale)
    gk = row(jnp.tile(k_norm, ATTN_WIDTH // HEAD_DIM))
    gmq = row(jnp.tile(mq_norm, MEM_WIDTH // HEAD_DIM) * scale)
    gmk = row(jnp.tile(mk_norm, MEM_WIDTH // HEAD_DIM))

    (q1, k1, v1, q4, k4, v4, q16, k16, v16, u, qm) = _proj(
        x2d, row(attn_norm), w_in.astype(BF16), gq, gk, gmq, B, S)
    ya = _attn(q1, k1, v1, q4, k4, v4, q16, k16, v16, B, S).reshape(T, ATTN_WIDTH)
    km, vm = _memkv(mem, row(mem_norm), w_mem_kv.astype(BF16), gmk, B)

    pool_bd = jax.scipy.linalg.block_diag(*[pool_proj[g] for g in range(pool_proj.shape[0])]).astype(BF16)
    w_r = jnp.zeros((D, LANES), F32)
    w_r = w_r.at[:, :N_GROUPS].set(w_group)
    w_r = w_r.at[:, EXPERT_LANE0:].set(jnp.transpose(w_router, (1, 0, 2)).reshape(D, N_EXPERTS))
    b_r = jnp.zeros((1, LANES), F32).at[0, :N_GROUPS].set(b_group).at[0, EXPERT_LANE0:].set(b_router.reshape(-1))
    x2, h2p, rinfo, rinfot, counts = _mix(ya, u, qm, km, vm, x2d, pool_bd, row(pool_scale),
                                          w_out.astype(BF16), row(ffn_norm), w_r.astype(BF16), b_r, B, S)

    pstart, sched = _tile_schedule(counts, T)
    dest = _dest(rinfot, pstart.astype(F32).reshape(N_EXPERTS, 1), T)
    y2slot = _moe(*sched, dest[0], h2p, w1, w3, w2)
    out = _combine(x2, rinfo, y2slot)
    return out.reshape(B, S, D)


def _tile_schedule(counts, T):
    tr = TR_MOE
    n_tiles = (2 * T + N_EXPERTS * tr) // tr
    cnt = counts[:, 0].astype(I32)
    padded = ((cnt + tr - 1) // tr) * tr
    pend = jnp.cumsum(padded)
    pstart = pend - padded
    n_used = (pend[-1] // tr).astype(I32).reshape(1)
    tiles = jnp.arange(n_tiles, dtype=I32)
    tile_row = jnp.minimum(tiles, n_used[0] - 1) * tr
    blk_exp = jnp.minimum(jnp.sum(tile_row[:, None] >= pend[None, :], axis=1), N_EXPERTS - 1).astype(I32)
    in_use = tiles < n_used[0]
    first = in_use & ((tiles == 0) | (blk_exp != jnp.roll(blk_exp, 1)))
    run_idx = jnp.cumsum(first.astype(I32)) - 1
    used = cnt > 0
    exp_of_run = jnp.argsort(jnp.logical_not(used), stable=True).astype(I32)
    nxt = jnp.where(run_idx + 1 < jnp.sum(used), exp_of_run[jnp.clip(run_idx + 1, 0, N_EXPERTS - 1)], -1)
    rows = jnp.clip(cnt[blk_exp] - (tile_row - pstart[blk_exp]), 0, tr)
    n_chunks = jnp.where(in_use, (rows + SCATTER_CHUNK - 1) // SCATTER_CHUNK, 0).astype(I32)
    sched = (blk_exp, n_used, first.astype(I32), nxt.astype(I32), (run_idx & 1).astype(I32), n_chunks)
    return pstart, sched


def kernel(x, mem, attn_norm, w_in, q_norm, k_norm, pool_proj, pool_scale, mem_norm, w_mem_kv, mq_norm, mk_norm,
           w_out, ffn_norm, w_group, b_group, w_router, b_router, w1, w3, w2):
    for l in range(attn_norm.shape[0]):
        x = _layer(x, mem, attn_norm[l], w_in[l], q_norm[l], k_norm[l], pool_proj[l], pool_scale[l],
                   mem_norm[l], w_mem_kv[l], mq_norm[l], mk_norm[l], w_out[l], ffn_norm[l],
                   w_group[l], b_group[l], w_router[l], b_router[l], w1[l], w3[l], w2[l])
    return x
```

```python
import functools

import jax
import jax.numpy as jnp
from jax import lax
from jax.experimental import pallas as pl
from jax.experimental.pallas import tpu as pltpu

F32 = jnp.float32
BF16 = jnp.bfloat16
I32 = jnp.int32
U32 = jnp.uint32

D_MODEL = 1024
HEAD_DIM = 64
ATTN_WIDTH = 512
POOL_WIDTH = 256
MEM_WIDTH = 256
N_MEM = 256
IN_WIDTH = 3 * ATTN_WIDTH + POOL_WIDTH + MEM_WIDTH
N_GROUPS = 8
EXPERTS_PER_GROUP = 8
N_EXPERTS = 64
EXPERT_HIDDEN = 512
EPS = 1e-6
NEG_INF = -1e30

LANES = 128
N_PAIRS = ATTN_WIDTH // LANES
BLOCK = 128
SUPER = 16 * BLOCK
HALO = 16
ROW_SUB = D_MODEL // LANES
PACKED = D_MODEL // 2

TM_PROJ = 1024
TM_MIX = 1024
TR_MOE = 256
SCATTER_CHUNK = 128
TC_COMBINE = 1024
EXPERT_LANE0 = 64

VMEM_LIMIT = 48 * 1024 * 1024
MOE_VMEM_LIMIT = 56 * 1024 * 1024


def _cparams(sem):
    return pltpu.CompilerParams(dimension_semantics=sem, vmem_limit_bytes=VMEM_LIMIT)


def _group_ones(n):
    g = jnp.arange(n) // HEAD_DIM
    return (g[:, None] == g[None, :]).astype(BF16)


def _head_norm(z, ones_ref, gain_ref):
    ss = jnp.dot((z * z).astype(BF16), ones_ref[...], preferred_element_type=F32)
    return z * lax.rsqrt(ss * (1.0 / HEAD_DIM) + EPS) * gain_ref[...]


def _proj_kernel(x_ref, an_ref, win_ref, gq_ref, gk_ref, gm_ref, o512_ref, o256_ref,
                 q1_ref, k1_ref, v1_ref, q4_ref, k4_ref, v4_ref, q16_ref, k16_ref, v16_ref,
                 u_ref, qm_ref, zs_ref, z4_ref):
    tm = x_ref.shape[0]
    x = x_ref[...]
    ms = jnp.mean(x * x, axis=-1, keepdims=True)
    h = (x * lax.rsqrt(ms + EPS) * an_ref[...]).astype(BF16)
    a0, a1, a2, a3 = ATTN_WIDTH, 2 * ATTN_WIDTH, 3 * ATTN_WIDTH, 3 * ATTN_WIDTH + POOL_WIDTH

    def cols(lo, hi):
        return jnp.dot(h, win_ref[:, lo:hi], preferred_element_type=F32)

    groups = ((lambda: _head_norm(cols(0, a0), o512_ref, gq_ref), q1_ref, q4_ref, q16_ref),
              (lambda: _head_norm(cols(a0, a1), o512_ref, gk_ref), k1_ref, k4_ref, k16_ref),
              (lambda: cols(a1, a2), v1_ref, v4_ref, v16_ref))
    for make, o1, o4, o16 in groups:
        val = make()
        for hp in range(N_PAIRS):
            pair = val[:, hp * LANES:(hp + 1) * LANES]
            zs_ref[hp] = pair
            o1[hp] = pair.astype(BF16)
        q4n = tm // 4
        for hp in range(N_PAIRS):
            for r4 in range(4):
                rows = zs_ref[hp, pl.ds(r4, q4n, stride=4), :]
                z4_ref[hp, pl.ds(r4 * q4n, q4n), :] = rows
                o4[hp, :, r4 * LANES:(r4 + 1) * LANES] = rows.astype(BF16)
            for r16 in range(16):
                r4, a = r16 % 4, r16 // 4
                rows = z4_ref[hp, pl.ds(r4 * q4n + a, tm // 16, stride=4), :]
                o16[hp, :, r16 * LANES:(r16 + 1) * LANES] = rows.astype(BF16)
    u_ref[...] = cols(a2, a3)
    qm_ref[...] = _head_norm(cols(a3, IN_WIDTH), o256_ref, gm_ref).astype(BF16)


def _proj(x2d, attn_norm, w_in, gq, gk, gm, B, S):
    T = B * S
    tm = TM_PROJ
    nj = S // tm
    const = lambda i: (0, 0)

    def lay(d):
        return jax.ShapeDtypeStruct((B, N_PAIRS, S // d, d * LANES), BF16)

    def lay_spec(d):
        return pl.BlockSpec((None, N_PAIRS, tm // d, d * LANES), lambda i: (i // nj, 0, i % nj, 0))

    out_shape = [lay(1)] * 3 + [lay(4)] * 3 + [lay(16)] * 3 + [
        jax.ShapeDtypeStruct((T, POOL_WIDTH), F32), jax.ShapeDtypeStruct((T, MEM_WIDTH), BF16)]
    out_specs = [lay_spec(1)] * 3 + [lay_spec(4)] * 3 + [lay_spec(16)] * 3 + [
        pl.BlockSpec((tm, POOL_WIDTH), lambda i: (i, 0)), pl.BlockSpec((tm, MEM_WIDTH), lambda i: (i, 0))]
    return pl.pallas_call(
        _proj_kernel,
        grid=(T // tm,),
        in_specs=[pl.BlockSpec((tm, D_MODEL), lambda i: (i, 0)),
                  pl.BlockSpec((1, D_MODEL), const),
                  pl.BlockSpec((D_MODEL, IN_WIDTH), const),
                  pl.BlockSpec((1, ATTN_WIDTH), const),
                  pl.BlockSpec((1, ATTN_WIDTH), const),
                  pl.BlockSpec((1, MEM_WIDTH), const),
                  pl.BlockSpec((ATTN_WIDTH, ATTN_WIDTH), const),
                  pl.BlockSpec((MEM_WIDTH, MEM_WIDTH), const)],
        out_specs=out_specs,
        out_shape=out_shape,
        scratch_shapes=[pltpu.VMEM((N_PAIRS, tm, LANES), F32)] * 2,
        compiler_params=_cparams(("parallel",)),
        name="proj",
    )(x2d, attn_norm, w_in, gq, gk, gm, _group_ones(ATTN_WIDTH), _group_ones(MEM_WIDTH))


def _attn_kernel(q1, k1c, k1p, v1c, v1p, q4, k4c, k4p, v4c, v4p, q16, k16c, k16p, v16c, v16p,
                 bias_ref, o_ref, obuf, mbuf, lbuf):
    c = pl.program_id(2)
    lane = lax.broadcasted_iota(I32, (BLOCK, LANES), 1)
    is_a = lane < HEAD_DIM
    lane_row = lax.broadcasted_iota(I32, (1, LANES), 1)
    mask_a = jnp.where(lane_row < HEAD_DIM, 1.0, 0.0).astype(BF16)
    mask_b = jnp.where(lane_row < HEAD_DIM, 0.0, 1.0).astype(BF16)
    bias_full = bias_ref[0]
    bias_first = jnp.where(c > 0, bias_full, bias_ref[1])

    def tile(q_t, kp_t, kc_t, vp_t, vc_t, bias):
        lhs = jnp.concatenate([q_t * mask_a, q_t * mask_b], axis=0)
        keys = jnp.concatenate([kp_t, kc_t], axis=0)
        s = lax.dot_general(lhs, keys, (((1,), (1,)), ((), ())), preferred_element_type=F32) + bias
        m = jnp.max(s, axis=-1, keepdims=True)
        p = jnp.exp(s - m)
        l = jnp.sum(p, axis=-1, keepdims=True)
        vals = jnp.concatenate([vp_t, vc_t], axis=0)
        pv = jnp.dot(p.astype(BF16), vals, preferred_element_type=F32)
        return (jnp.where(is_a, pv[:BLOCK], pv[BLOCK:]), jnp.where(is_a, m[:BLOCK], m[BLOCK:]),
                jnp.where(is_a, l[:BLOCK], l[BLOCK:]))

    for pat, (d, q, kc, kp, vc, vp) in enumerate(((16, q16, k16c, k16p, v16c, v16p), (4, q4, k4c, k4p, v4c, v4p))):
        nblk = SUPER // (BLOCK * d)
        for r in range(d):
            cols = pl.ds(r * LANES, LANES)
            for jb in range(nblk):
                cur = pl.ds(jb * BLOCK, BLOCK)
                if jb == 0:
                    kp_t, vp_t, bias = kp[:, cols], vp[:, cols], bias_first
                else:
                    prev = pl.ds((jb - 1) * BLOCK, BLOCK)
                    kp_t, vp_t, bias = kc[prev, cols], vc[prev, cols], bias_full
                o_t, m_t, l_t = tile(q[cur, cols], kp_t, kc[cur, cols], vp_t, vc[cur, cols], bias)
                rows = pl.ds(jb * BLOCK * d + r, BLOCK, stride=d)
                obuf[pat, rows, :] = o_t
                mbuf[pat, rows, :] = m_t
                lbuf[pat, rows, :] = l_t
    for jb in range(SUPER // BLOCK):
        cur = pl.ds(jb * BLOCK, BLOCK)
        if jb == 0:
            kp_t, vp_t, bias = k1p[...], v1p[...], bias_first
        else:
            prev = pl.ds((jb - 1) * BLOCK, BLOCK)
            kp_t, vp_t, bias = k1c[prev, :], v1c[prev, :], bias_full
        o0, m0, l0 = tile(q1[cur, :], kp_t, k1c[cur, :], vp_t, v1c[cur, :], bias)
        m1, m2 = mbuf[0, cur, :], mbuf[1, cur, :]
        top = jnp.maximum(jnp.maximum(m0, m1), m2)
        w0, w1, w2 = jnp.exp(m0 - top), jnp.exp(m1 - top), jnp.exp(m2 - top)
        num = w0 * o0 + w1 * obuf[0, cur, :] + w2 * obuf[1, cur, :]
        den = w0 * l0 + w1 * lbuf[0, cur, :] + w2 * lbuf[1, cur, :]
        o_ref[cur, :] = (num * (1.0 / den)).astype(BF16)


def _band_bias():
    qi = jnp.arange(BLOCK)[:, None]
    kj = jnp.arange(2 * BLOCK)[None, :]
    dist = qi + BLOCK - kj
    in_band = (dist >= 0) & (dist <= BLOCK)
    full = jnp.where(in_band, 0.0, NEG_INF).astype(F32)
    first = jnp.where(in_band & (kj >= BLOCK), 0.0, NEG_INF).astype(F32)
    return jnp.stack([jnp.tile(full, (2, 1)), jnp.tile(first, (2, 1))])


def _attn(q1, k1, v1, q4, k4, v4, q16, k16, v16, B, S):
    nsup = S // SUPER

    def specs(d):
        rows = SUPER // d
        per = rows // BLOCK
        cur = pl.BlockSpec((None, None, rows, d * LANES), lambda b, hp, c: (b, hp, c, 0))
        prev = pl.BlockSpec((None, None, BLOCK, d * LANES),
                            lambda b, hp, c: (b, hp, jnp.maximum(per * c - 1, 0), 0))
        return [cur, cur, prev, cur, prev]

    return pl.pallas_call(
        _attn_kernel,
        grid=(B, N_PAIRS, nsup),
        in_specs=specs(1) + specs(4) + specs(16) + [
            pl.BlockSpec((2, 2 * BLOCK, 2 * BLOCK), lambda b, hp, c: (0, 0, 0))],
        out_specs=pl.BlockSpec((None, SUPER, LANES), lambda b, hp, c: (b, c, hp)),
        out_shape=jax.ShapeDtypeStruct((B, S, ATTN_WIDTH), BF16),
        scratch_shapes=[pltpu.VMEM((2, SUPER, LANES), F32)] * 3,
        compiler_params=_cparams(("parallel", "parallel", "parallel")),
        name="dilated_attn",
    )(q1, k1, k1, v1, v1, q4, k4, k4, v4, v4, q16, k16, k16, v16, v16, _band_bias())


def _memkv_kernel(mem_ref, mn_ref, wkv_ref, gk_ref, o256_ref, km_ref, vm_ref):
    m = mem_ref[...]
    ms = jnp.mean(m * m, axis=-1, keepdims=True)
    mn = (m * lax.rsqrt(ms + EPS) * mn_ref[...]).astype(BF16)
    kv = jnp.dot(mn, wkv_ref[...], preferred_element_type=F32)
    km_ref[...] = _head_norm(kv[:, :MEM_WIDTH], o256_ref, gk_ref).astype(BF16)
    vm_ref[...] = kv[:, MEM_WIDTH:].astype(BF16)


def _memkv(mem, mem_norm, w_mem_kv, gmk, B):
    const = lambda b: (0, 0)
    return pl.pallas_call(
        _memkv_kernel,
        grid=(B,),
        in_specs=[pl.BlockSpec((None, N_MEM, D_MODEL), lambda b: (b, 0, 0)),
                  pl.BlockSpec((1, D_MODEL), const),
                  pl.BlockSpec((D_MODEL, 2 * MEM_WIDTH), const),
                  pl.BlockSpec((1, MEM_WIDTH), const),
                  pl.BlockSpec((MEM_WIDTH, MEM_WIDTH), const)],
        out_specs=[pl.BlockSpec((None, N_MEM, MEM_WIDTH), lambda b: (b, 0, 0))] * 2,
        out_shape=[jax.ShapeDtypeStruct((B, N_MEM, MEM_WIDTH), BF16)] * 2,
        compiler_params=_cparams(("parallel",)),
        name="memkv",
    )(mem, mem_norm, w_mem_kv, gmk, _group_ones(MEM_WIDTH))


def _mix_kernel(tiles_per_seq, ya_ref, u_ref, uh_ref, qm_ref, km_ref, vm_ref, x_ref, pp_ref, ps_ref,
                wo_ref, fn_ref, wr_ref, br_ref, utri_ref,
                x2_ref, h2p_ref, rinfo_ref, rinfot_ref, cnt_ref, base_ref):
    i = pl.program_id(0)
    tm = x_ref.shape[0]
    seq_tile = i % tiles_per_seq

    @pl.when(i == 0)
    def _():
        base_ref[...] = jnp.zeros_like(base_ref)

    u = u_ref[...]
    halo = jnp.where(seq_tile == 0, 0.0, uh_ref[...])
    uu = jnp.concatenate([halo, u], axis=0)
    a1 = uu[1:] + uu[:-1]
    a2 = a1[2:] + a1[:-2]
    a3 = a2[4:] + a2[:-4]
    a4 = a3[8:] + a3[:-8]
    lane_p = lax.broadcasted_iota(I32, (tm, POOL_WIDTH), 1)
    g0, g1, g2 = lane_p < 64, lane_p < 128, lane_p < 192
    wsum = jnp.where(g0, a1[15:], jnp.where(g1, a2[13:], jnp.where(g2, a3[9:], a4[1:])))
    wlen = jnp.where(g0, 2.0, jnp.where(g1, 4.0, jnp.where(g2, 8.0, 16.0)))
    tpos = seq_tile * tm + lax.broadcasted_iota(I32, (tm, POOL_WIDTH), 0) + 1
    cnt = jnp.minimum(tpos.astype(F32), wlen)
    pooled = wsum / cnt - u
    y_pool = jnp.dot(pooled.astype(BF16), pp_ref[...], preferred_element_type=F32) * ps_ref[...]

    lane = lax.broadcasted_iota(I32, (tm, LANES), 1)
    is_a = lane < HEAD_DIM
    lane_row = lax.broadcasted_iota(I32, (1, LANES), 1)
    mask_a = jnp.where(lane_row < HEAD_DIM, 1.0, 0.0).astype(BF16)
    mask_b = jnp.where(lane_row < HEAD_DIM, 0.0, 1.0).astype(BF16)
    y_mem = []
    for pr in range(MEM_WIDTH // LANES):
        cols = slice(pr * LANES, (pr + 1) * LANES)
        qp, kp, vp = qm_ref[:, cols], km_ref[:, cols], vm_ref[:, cols]
        outs = []
        for msk in (mask_a, mask_b):
            s = lax.dot_general(qp * msk, kp, (((1,), (1,)), ((), ())), preferred_element_type=F32)
            m = jnp.max(s, axis=-1, keepdims=True)
            p = jnp.exp(s - m)
            p = p / jnp.sum(p, axis=-1, keepdims=True)
            outs.append(jnp.dot(p.astype(BF16), vp, preferred_element_type=F32))
        y_mem.append(jnp.where(is_a, outs[0], outs[1]))

    rest = jnp.concatenate([y_pool] + y_mem, axis=1).astype(BF16)
    proj = jnp.dot(ya_ref[...], wo_ref[:ATTN_WIDTH, :], preferred_element_type=F32)
    proj += jnp.dot(rest, wo_ref[ATTN_WIDTH:, :], preferred_element_type=F32)
    x2 = x_ref[...] + proj
    x2_ref[...] = x2

    ms = jnp.mean(x2 * x2, axis=-1, keepdims=True)
    h2 = (x2 * lax.rsqrt(ms + EPS) * fn_ref[...]).astype(BF16)
    bits = lax.bitcast_convert_type(h2.astype(F32), U32)
    h2p_ref[...] = bits[:, :PACKED] | (bits[:, PACKED:] >> 16)

    logits = jnp.dot(h2, wr_ref[...], preferred_element_type=F32) + br_ref[...]
    lt = jnp.transpose(logits)
    ninf = -jnp.inf
    sub8 = lax.broadcasted_iota(I32, (EXPERTS_PER_GROUP, tm), 0).astype(F32)
    lg = lt[:N_GROUPS]
    mg = jnp.max(lg, axis=0, keepdims=True)
    g_sel = jnp.min(jnp.where(lg == mg, sub8, float(N_GROUPS)), axis=0, keepdims=True)
    w_g = 1.0 / jnp.sum(jnp.exp(lg - mg), axis=0, keepdims=True)
    le = lt[EXPERT_LANE0:EXPERT_LANE0 + EXPERTS_PER_GROUP]
    for g in range(1, N_GROUPS):
        lo_g = EXPERT_LANE0 + g * EXPERTS_PER_GROUP
        le = jnp.where(g_sel == float(g), lt[lo_g:lo_g + EXPERTS_PER_GROUP], le)
    v1 = jnp.max(le, axis=0, keepdims=True)
    i1 = jnp.min(jnp.where(le == v1, sub8, float(EXPERTS_PER_GROUP)), axis=0, keepdims=True)
    le2 = jnp.where(sub8 == i1, ninf, le)
    v2 = jnp.max(le2, axis=0, keepdims=True)
    i2 = jnp.min(jnp.where(le2 == v2, sub8, float(EXPERTS_PER_GROUP)), axis=0, keepdims=True)
    e21 = jnp.exp(v2 - v1)
    gate1 = w_g / (1.0 + e21)
    gate2 = w_g * e21 / (1.0 + e21)
    e1 = g_sel * float(EXPERTS_PER_GROUP) + i1
    e2 = g_sel * float(EXPERTS_PER_GROUP) + i2

    sub64 = lax.broadcasted_iota(I32, (N_EXPERTS, tm), 0).astype(F32)
    hot1 = sub64 == e1
    hot2 = sub64 == e2
    onehot = jnp.where(hot1 | hot2, 1.0, 0.0)
    prefix = jnp.dot(onehot.astype(BF16), utri_ref[...], preferred_element_type=F32)
    tot = prefix + base_ref[...]
    rank1 = jnp.sum(jnp.where(hot1, tot, 0.0), axis=0, keepdims=True)
    rank2 = jnp.sum(jnp.where(hot2, tot, 0.0), axis=0, keepdims=True)
    base_new = base_ref[...] + jnp.sum(onehot, axis=1, keepdims=True)
    base_ref[...] = base_new
    cnt_ref[...] = base_new

    rt = jnp.where(sub8 == 0.0, e1, jnp.where(sub8 == 1.0, e2, jnp.where(sub8 == 2.0, rank1, jnp.where(
        sub8 == 3.0, rank2, jnp.where(sub8 == 4.0, gate1, jnp.where(sub8 == 5.0, gate2, 0.0))))))
    rinfot_ref[...] = rt
    rinfo_ref[...] = jnp.transpose(jnp.concatenate([rt, jnp.zeros((LANES - 8, tm), F32)], axis=0))


def _mix(ya, u, qm, km, vm, x2d, pool_bd, pool_scale, w_out, ffn_norm, w_r, b_r, B, S):
    T = B * S
    tm = TM_MIX
    tps = S // tm
    hb = tm // HALO
    const = lambda i: (0, 0)
    utri = (jnp.arange(tm)[:, None] < jnp.arange(tm)[None, :]).astype(BF16)
    return pl.pallas_call(
        functools.partial(_mix_kernel, tps),
        grid=(T // tm,),
        in_specs=[pl.BlockSpec((tm, ATTN_WIDTH), lambda i: (i, 0)),
                  pl.BlockSpec((tm, POOL_WIDTH), lambda i: (i, 0)),
                  pl.BlockSpec((HALO, POOL_WIDTH), lambda i: (jnp.maximum(i * hb - 1, 0), 0)),
                  pl.BlockSpec((tm, MEM_WIDTH), lambda i: (i, 0)),
                  pl.BlockSpec((None, N_MEM, MEM_WIDTH), lambda i: (i // tps, 0, 0)),
                  pl.BlockSpec((None, N_MEM, MEM_WIDTH), lambda i: (i // tps, 0, 0)),
                  pl.BlockSpec((tm, D_MODEL), lambda i: (i, 0)),
                  pl.BlockSpec((POOL_WIDTH, POOL_WIDTH), const),
                  pl.BlockSpec((1, POOL_WIDTH), const),
                  pl.BlockSpec((D_MODEL, D_MODEL), const),
                  pl.BlockSpec((1, D_MODEL), const),
                  pl.BlockSpec((D_MODEL, LANES), const),
                  pl.BlockSpec((1, LANES), const),
                  pl.BlockSpec((tm, tm), const)],
        out_specs=[pl.BlockSpec((tm, D_MODEL), lambda i: (i, 0)),
                   pl.BlockSpec((tm, PACKED), lambda i: (i, 0)),
                   pl.BlockSpec((tm, LANES), lambda i: (i, 0)),
                   pl.BlockSpec((8, tm), lambda i: (0, i)),
                   pl.BlockSpec((N_EXPERTS, 1), const)],
        out_shape=[jax.ShapeDtypeStruct((T, D_MODEL), F32),
                   jax.ShapeDtypeStruct((T, PACKED), U32),
                   jax.ShapeDtypeStruct((T, LANES), F32),
                   jax.ShapeDtypeStruct((8, T), F32),
                   jax.ShapeDtypeStruct((N_EXPERTS, 1), F32)],
        scratch_shapes=[pltpu.VMEM((N_EXPERTS, 1), F32)],
        compiler_params=_cparams(("arbitrary",)),
        name="mix_router",
    )(ya, u, u, qm, km, vm, x2d, pool_bd, pool_scale, w_out, ffn_norm, w_r, b_r, utri)


def _dest_kernel(rt_ref, ps_ref, d_ref):
    tn = rt_ref.shape[1]
    sub = lax.broadcasted_iota(I32, (N_EXPERTS, tn), 0).astype(F32)
    ps = ps_ref[...]
    rows = []
    for k in range(2):
        e = rt_ref[k:k + 1, :]
        start = jnp.sum(jnp.where(sub == e, ps, 0.0), axis=0, keepdims=True)
        rows.append(start + rt_ref[2 + k:3 + k, :])
    word = rows[0].astype(I32) | (rows[1].astype(I32) << 16)
    d_ref[...] = jnp.concatenate([word, jnp.zeros((7, tn), I32)], axis=0)


def _dest(rinfot, pstart_col, T):
    tn = 2048
    return pl.pallas_call(
        _dest_kernel,
        grid=(T // tn,),
        in_specs=[pl.BlockSpec((8, tn), lambda i: (0, i)),
                  pl.BlockSpec((N_EXPERTS, 1), lambda i: (0, 0))],
        out_specs=pl.BlockSpec((8, tn), lambda i: (0, i)),
        out_shape=jax.ShapeDtypeStruct((8, T), I32),
        compiler_params=_cparams(("parallel",)),
        name="dest_rows",
    )(rinfot, pstart_col)


def _row_map(d_ref, inv_ref):
    T = d_ref.shape[0]
    unroll = 8

    def body(g, carry):
        t0 = g * unroll
        words = [d_ref[t0 + u] for u in range(unroll)]
        for u, w in enumerate(words):
            inv_ref[w & 0xFFFF] = t0 + u
            inv_ref[lax.shift_right_logical(w, 16)] = T + t0 + u
        return carry

    lax.fori_loop(0, T // unroll, body, 0)


def _moe_kernel(be_ref, nu_ref, first_ref, nxt_ref, ws_ref, nc_ref, d_ref,
                h_hbm, w1_hbm, w3_hbm, w2_hbm, fill_hbm, o_hbm,
                tab, xbuf0, xbuf1, a0, b0, a1, b1, ybuf, w1s, w3s, w2s, w1b, w3b, w2b, inv_ref, tsem, ssem, wsem):
    j = pl.program_id(0)
    nu = nu_ref[0]
    T = h_hbm.shape[0]
    tr = TR_MOE
    n_tiles = be_ref.shape[0]
    xbuf = (xbuf0, xbuf1)
    abuf = ((a0, b0), (a1, b1))

    def tile_rows(ref, row, n=1):
        return ref.at[pl.ds(pl.multiple_of(row * ROW_SUB, ROW_SUB), n * ROW_SUB)]

    def gather_tile(tile, slot):
        for i in range(tr):
            v = inv_ref[tile * tr + i]
            if T & (T - 1) == 0:
                tok = v & (T - 1)
            else:
                tok = jnp.where(v >= 2 * T, 0, jnp.where(v >= T, v - T, v))
            xbuf[slot][pl.ds(i, 1), :] = tab[pl.ds(tok, 1), :]

    def scatter_row(tile, slot, i):
        v = inv_ref[tile * tr + i]
        return pltpu.make_async_copy(tile_rows(ybuf, slot * tr + i), tile_rows(o_hbm, v), ssem.at[slot])

    def chunked(tile, fn):
        fn(0)
        for c in range(1, tr // SCATTER_CHUNK):
            pl.when(c < nc_ref[tile])(functools.partial(fn, c))

    def scatter_wait(tile, slot):
        chunked(tile, lambda c: pltpu.make_async_copy(
            tile_rows(ybuf, slot * tr, SCATTER_CHUNK), tile_rows(o_hbm, 0, SCATTER_CHUNK), ssem.at[slot]).wait())

    def weight_copies(e, slot):
        return [pltpu.make_async_copy(src.at[e], dst.at[slot], wsem.at[slot])
                for src, dst in ((w1_hbm, w1s), (w3_hbm, w3s), (w2_hbm, w2s))]

    def up_project(slot):
        w = xbuf[slot][...]
        lo = lax.bitcast_convert_type(w & jnp.uint32(0xFFFF0000), F32).astype(BF16)
        hi = lax.bitcast_convert_type(w << 16, F32).astype(BF16)
        for dst, wb in zip(abuf[slot], (w1b, w3b)):
            dst[...] = (jnp.dot(lo, wb[:PACKED, :], preferred_element_type=F32)
                        + jnp.dot(hi, wb[PACKED:, :], preferred_element_type=F32))

    @pl.when(j < nu)
    def _():
        s = j & 1
        jn = jnp.minimum(j + 1, n_tiles - 1)

        @pl.when(j == 0)
        def _():
            fill = pltpu.make_async_copy(fill_hbm, inv_ref, tsem)
            fill.start()
            fill.wait()
            ybuf[...] = jnp.zeros_like(ybuf)
            scratch_rows = pltpu.make_async_copy(ybuf, tile_rows(o_hbm, 2 * T, 2 * tr), tsem)
            scratch_rows.start()
            scratch_rows.wait()
            table = pltpu.make_async_copy(h_hbm, tab, tsem)
            table.start()
            for cp in weight_copies(be_ref[0], 0):
                cp.start()
            _row_map(d_ref, inv_ref)
            table.wait()
            gather_tile(0, 0)
            gather_tile(jnp.minimum(1, n_tiles - 1), 1)
            for cp in weight_copies(be_ref[0], 0):
                cp.wait()

            @pl.when(nxt_ref[0] >= 0)
            def _():
                for cp in weight_copies(nxt_ref[0], 1):
                    cp.start()

            w1b[...] = w1s[0].astype(BF16)
            w3b[...] = w3s[0].astype(BF16)
            w2b[...] = w2s[0].astype(BF16)
            up_project(0)

        @pl.when((j >= 1) & (first_ref[j] == 1))
        def _():
            w2b[...] = w2s[ws_ref[j]].astype(BF16)

        @pl.when((j + 1 < nu) & (first_ref[jn] == 1))
        def _():
            wslot = ws_ref[jn]
            for cp in weight_copies(be_ref[jn], wslot):
                cp.wait()
            e_next = nxt_ref[jn]

            @pl.when(e_next >= 0)
            def _():
                for cp in weight_copies(e_next, 1 - wslot):
                    cp.start()

            w1b[...] = w1s[wslot].astype(BF16)
            w3b[...] = w3s[wslot].astype(BF16)

        @pl.when(j >= 2)
        def _():
            scatter_wait(j - 2, s)

        for slot in range(2):
            @pl.when(s == slot)
            def _():
                gather_tile(jnp.minimum(j + 2, n_tiles - 1), slot)
                a = abuf[slot][0][...]
                b = abuf[slot][1][...]
                up_project(1 - slot)
                hmid = (a / (1.0 + jnp.exp(-a)) * b).astype(BF16)
                y = jnp.dot(hmid, w2b[...], preferred_element_type=F32)
                for c in range(ROW_SUB):
                    ybuf[pl.ds(slot * (tr * ROW_SUB) + c, tr, stride=ROW_SUB), :] = y[:, c * LANES:(c + 1) * LANES]
                def start_chunk(c, slot=slot):
                    for i in range(c * SCATTER_CHUNK, (c + 1) * SCATTER_CHUNK):
                        scatter_row(j, slot, i).start()

                chunked(j, start_chunk)

        @pl.when(j == nu - 1)
        def _():
            scatter_wait(j, s)

            @pl.when(j >= 1)
            def _():
                scatter_wait(j - 1, 1 - s)


def _moe(blk_exp, n_used, first, nxt, wslot, n_chunks, dword, h2p, w1, w3, w2):
    T = h2p.shape[0]
    tr = TR_MOE
    n_tiles = blk_exp.shape[0]
    R = n_tiles * tr
    assert R <= 1 << 16
    fill = 2 * T + jnp.arange(R, dtype=I32) % (2 * tr)
    any_spec = pl.BlockSpec(memory_space=pl.ANY)
    return pl.pallas_call(
        _moe_kernel,
        grid_spec=pltpu.PrefetchScalarGridSpec(
            num_scalar_prefetch=7,
            grid=(n_tiles,),
            in_specs=[any_spec] * 5,
            out_specs=any_spec,
            scratch_shapes=[pltpu.VMEM((T, PACKED), U32),
                            pltpu.VMEM((tr, PACKED), U32),
                            pltpu.VMEM((tr, PACKED), U32),
                            pltpu.VMEM((tr, EXPERT_HIDDEN), F32),
                            pltpu.VMEM((tr, EXPERT_HIDDEN), F32),
                            pltpu.VMEM((tr, EXPERT_HIDDEN), F32),
                            pltpu.VMEM((tr, EXPERT_HIDDEN), F32),
                            pltpu.VMEM((2 * tr * ROW_SUB, LANES), F32),
                            pltpu.VMEM((2, D_MODEL, EXPERT_HIDDEN), F32),
                            pltpu.VMEM((2, D_MODEL, EXPERT_HIDDEN), F32),
                            pltpu.VMEM((2, EXPERT_HIDDEN, D_MODEL), F32),
                            pltpu.VMEM((D_MODEL, EXPERT_HIDDEN), BF16),
                            pltpu.VMEM((D_MODEL, EXPERT_HIDDEN), BF16),
                            pltpu.VMEM((EXPERT_HIDDEN, D_MODEL), BF16),
                            pltpu.SMEM((R,), I32),
                            pltpu.SemaphoreType.DMA(()),
                            pltpu.SemaphoreType.DMA((2,)),
                            pltpu.SemaphoreType.DMA((2,))]),
        out_shape=jax.ShapeDtypeStruct(((2 * T + 2 * tr) * ROW_SUB, LANES), F32),
        compiler_params=pltpu.CompilerParams(dimension_semantics=("arbitrary",), vmem_limit_bytes=MOE_VMEM_LIMIT),
        name="moe_experts",
    )(blk_exp, n_used, first, nxt, wslot, n_chunks, dword, h2p, w1, w3, w2, fill)


def _combine_kernel(x2_ref, rinfo_ref, y1_ref, y2_ref, o_ref):
    g1 = rinfo_ref[:, 4:5]
    g2 = rinfo_ref[:, 5:6]
    tc = x2_ref.shape[0]
    for c in range(ROW_SUB):
        cols = slice(c * LANES, (c + 1) * LANES)
        rows = pl.ds(c, tc, stride=ROW_SUB)
        o_ref[:, cols] = x2_ref[:, cols] + (g1 * y1_ref[rows, :] + g2 * y2_ref[rows, :])


def _combine(x2, rinfo, y2slot):
    T = x2.shape[0]
    tc = TC_COMBINE
    return pl.pallas_call(
        _combine_kernel,
        grid=(T // tc,),
        in_specs=[pl.BlockSpec((tc, D_MODEL), lambda i: (i, 0)),
                  pl.BlockSpec((tc, LANES), lambda i: (i, 0)),
                  pl.BlockSpec((tc * ROW_SUB, LANES), lambda i: (i, 0)),
                  pl.BlockSpec((tc * ROW_SUB, LANES), lambda i: (T // tc + i, 0))],
        out_specs=pl.BlockSpec((tc, D_MODEL), lambda i: (i, 0)),
        out_shape=jax.ShapeDtypeStruct((T, D_MODEL), F32),
        compiler_params=_cparams(("parallel",)),
        name="combine",
    )(x2, rinfo, y2slot, y2slot)


def _layer(x, mem, attn_norm, w_in, q_norm, k_norm, pool_proj, pool_scale, mem_norm, w_mem_kv,
           mq_norm, mk_norm, w_out, ffn_norm, w_group, b_group, w_router, b_router, w1, w3, w2):
    B, S, D = x.shape
    T = B * S
    assert D == D_MODEL and S % SUPER == 0 and T % TM_PROJ == 0
    x2d = x.reshape(T, D)
    row = lambda v: v.reshape(1, -1).astype(F32)
    scale = HEAD_DIM ** -0.5
    gq = row(jnp.tile(q_norm, ATTN_WIDTH // HEAD_DIM) * scale)
    gk = row(jnp.tile(k_norm, ATTN_WIDTH // HEAD_DIM))
    gmq = row(jnp.tile(mq_norm, MEM_WIDTH // HEAD_DIM) * scale)
    gmk = row(jnp.tile(mk_norm, MEM_WIDTH // HEAD_DIM))

    (q1, k1, v1, q4, k4, v4, q16, k16, v16, u, qm) = _proj(
        x2d, row(attn_norm), w_in.astype(BF16), gq, gk, gmq, B, S)
    ya = _attn(q1, k1, v1, q4, k4, v4, q16, k16, v16, B, S).reshape(T, ATTN_WIDTH)
    km, vm = _memkv(mem, row(mem_norm), w_mem_kv.astype(BF16), gmk, B)

    pool_bd = jax.scipy.linalg.block_diag(*[pool_proj[g] for g in range(pool_proj.shape[0])]).astype(BF16)
    w_r = jnp.zeros((D, LANES), F32)
    w_r = w_r.at[:, :N_GROUPS].set(w_group)
    w_r = w_r.at[:, EXPERT_LANE0:].set(jnp.transpose(w_router, (1, 0, 2)).reshape(D, N_EXPERTS))
    b_r = jnp.zeros((1, LANES), F32).at[0, :N_GROUPS].set(b_group).at[0, EXPERT_LANE0:].set(b_router.reshape(-1))
    x2, h2p, rinfo, rinfot, counts = _mix(ya, u, qm, km, vm, x2d, pool_bd, row(pool_scale),
                                          w_out.astype(BF16), row(ffn_norm), w_r.astype(BF16), b_r, B, S)

    pstart, sched = _tile_schedule(counts, T)
    dest = _dest(rinfot, pstart.astype(F32).reshape(N_EXPERTS, 1), T)
    y2slot = _moe(*sched, dest[0], h2p, w1, w3, w2)
    out = _combine(x2, rinfo, y2slot)
    return out.reshape(B, S, D)


def _tile_schedule(counts, T):
    tr = TR_MOE
    n_tiles = (2 * T + N_EXPERTS * tr) // tr
    cnt = counts[:, 0].astype(I32)
    padded = ((cnt + tr - 1) // tr) * tr
    pend = jnp.cumsum(padded)
    pstart = pend - padded
    n_used = (pend[-1] // tr).astype(I32).reshape(1)
    tiles = jnp.arange(n_tiles, dtype=I32)
    tile_row = jnp.minimum(tiles, n_used[0] - 1) * tr
    blk_exp = jnp.minimum(jnp.sum(tile_row[:, None] >= pend[None, :], axis=1), N_EXPERTS - 1).astype(I32)
    in_use = tiles < n_used[0]
    first = in_use & ((tiles == 0) | (blk_exp != jnp.roll(blk_exp, 1)))
    run_idx = jnp.cumsum(first.astype(I32)) - 1
    used = cnt > 0
    exp_of_run = jnp.argsort(jnp.logical_not(used), stable=True).astype(I32)
    nxt = jnp.where(run_idx + 1 < jnp.sum(used), exp_of_run[jnp.clip(run_idx + 1, 0, N_EXPERTS - 1)], -1)
    rows = jnp.clip(cnt[blk_exp] - (tile_row - pstart[blk_exp]), 0, tr)
    n_chunks = jnp.where(in_use, (rows + SCATTER_CHUNK - 1) // SCATTER_CHUNK, 0).astype(I32)
    sched = (blk_exp, n_used, first.astype(I32), nxt.astype(I32), (run_idx & 1).astype(I32), n_chunks)
    return pstart, sched


def kernel(x, mem, attn_norm, w_in, q_norm, k_norm, pool_proj, pool_scale, mem_norm, w_mem_kv, mq_norm, mk_norm,
           w_out, ffn_norm, w_group, b_group, w_router, b_router, w1, w3, w2):
    for l in range(attn_norm.shape[0]):
        x = _layer(x, mem, attn_norm[l], w_in[l], q_norm[l], k_norm[l], pool_proj[l], pool_scale[l],
                   mem_norm[l], w_mem_kv[l], mq_norm[l], mk_norm[l], w_out[l], ffn_norm[l],
                   w_group[l], b_group[l], w_router[l], b_router[l], w1[l], w3[l], w2[l])
    return x
```

```python
import functools

import jax
import jax.numpy as jnp
from jax import lax
from jax.experimental import pallas as pl
from jax.experimental.pallas import tpu as pltpu

F32 = jnp.float32
BF16 = jnp.bfloat16
I32 = jnp.int32
U32 = jnp.uint32

D_MODEL = 1024
HEAD_DIM = 64
ATTN_WIDTH = 512
POOL_WIDTH = 256
MEM_WIDTH = 256
N_MEM = 256
IN_WIDTH = 3 * ATTN_WIDTH + POOL_WIDTH + MEM_WIDTH
N_GROUPS = 8
EXPERTS_PER_GROUP = 8
N_EXPERTS = 64
EXPERT_HIDDEN = 512
EPS = 1e-6
NEG_INF = -1e30

LANES = 128
N_PAIRS = ATTN_WIDTH // LANES
BLOCK = 128
SUPER = 16 * BLOCK
HALO = 16
ROW_SUB = D_MODEL // LANES
PACKED = D_MODEL // 2

TM_PROJ = 1024
TM_MIX = 1024
TR_MOE = 256
WEIGHT_DMA_PRIORITY = 1
TC_COMBINE = 1024
EXPERT_LANE0 = 64

VMEM_LIMIT = 48 * 1024 * 1024
MOE_VMEM_LIMIT = 56 * 1024 * 1024


def _cparams(sem):
    return pltpu.CompilerParams(dimension_semantics=sem, vmem_limit_bytes=VMEM_LIMIT)


def _group_ones(n):
    g = jnp.arange(n) // HEAD_DIM
    return (g[:, None] == g[None, :]).astype(BF16)


def _head_norm(z, ones_ref, gain_ref):
    ss = jnp.dot((z * z).astype(BF16), ones_ref[...], preferred_element_type=F32)
    return z * lax.rsqrt(ss * (1.0 / HEAD_DIM) + EPS) * gain_ref[...]


def _proj_kernel(x_ref, an_ref, win_ref, gq_ref, gk_ref, gm_ref, o512_ref, o256_ref,
                 q1_ref, k1_ref, v1_ref, q4_ref, k4_ref, v4_ref, q16_ref, k16_ref, v16_ref,
                 u_ref, qm_ref, zs_ref, z4_ref):
    tm = x_ref.shape[0]
    x = x_ref[...]
    ms = jnp.mean(x * x, axis=-1, keepdims=True)
    h = (x * lax.rsqrt(ms + EPS) * an_ref[...]).astype(BF16)
    a0, a1, a2, a3 = ATTN_WIDTH, 2 * ATTN_WIDTH, 3 * ATTN_WIDTH, 3 * ATTN_WIDTH + POOL_WIDTH

    def cols(lo, hi):
        return jnp.dot(h, win_ref[:, lo:hi], preferred_element_type=F32)

    groups = ((lambda: _head_norm(cols(0, a0), o512_ref, gq_ref), q1_ref, q4_ref, q16_ref),
              (lambda: _head_norm(cols(a0, a1), o512_ref, gk_ref), k1_ref, k4_ref, k16_ref),
              (lambda: cols(a1, a2), v1_ref, v4_ref, v16_ref))
    for make, o1, o4, o16 in groups:
        val = make()
        for hp in range(N_PAIRS):
            pair = val[:, hp * LANES:(hp + 1) * LANES]
            zs_ref[hp] = pair
            o1[hp] = pair.astype(BF16)
        q4n = tm // 4
        for hp in range(N_PAIRS):
            for r4 in range(4):
                rows = zs_ref[hp, pl.ds(r4, q4n, stride=4), :]
                z4_ref[hp, pl.ds(r4 * q4n, q4n), :] = rows
                o4[hp, :, r4 * LANES:(r4 + 1) * LANES] = rows.astype(BF16)
            for r16 in range(16):
                r4, a = r16 % 4, r16 // 4
                rows = z4_ref[hp, pl.ds(r4 * q4n + a, tm // 16, stride=4), :]
                o16[hp, :, r16 * LANES:(r16 + 1) * LANES] = rows.astype(BF16)
    u_ref[...] = cols(a2, a3)
    qm_ref[...] = _head_norm(cols(a3, IN_WIDTH), o256_ref, gm_ref).astype(BF16)


def _proj(x2d, attn_norm, w_in, gq, gk, gm, B, S):
    T = B * S
    tm = TM_PROJ
    nj = S // tm
    const = lambda i: (0, 0)

    def lay(d):
        return jax.ShapeDtypeStruct((B, N_PAIRS, S // d, d * LANES), BF16)

    def lay_spec(d):
        return pl.BlockSpec((None, N_PAIRS, tm // d, d * LANES), lambda i: (i // nj, 0, i % nj, 0))

    out_shape = [lay(1)] * 3 + [lay(4)] * 3 + [lay(16)] * 3 + [
        jax.ShapeDtypeStruct((T, POOL_WIDTH), F32), jax.ShapeDtypeStruct((T, MEM_WIDTH), BF16)]
    out_specs = [lay_spec(1)] * 3 + [lay_spec(4)] * 3 + [lay_spec(16)] * 3 + [
        pl.BlockSpec((tm, POOL_WIDTH), lambda i: (i, 0)), pl.BlockSpec((tm, MEM_WIDTH), lambda i: (i, 0))]
    return pl.pallas_call(
        _proj_kernel,
        grid=(T // tm,),
        in_specs=[pl.BlockSpec((tm, D_MODEL), lambda i: (i, 0)),
                  pl.BlockSpec((1, D_MODEL), const),
                  pl.BlockSpec((D_MODEL, IN_WIDTH), const),
                  pl.BlockSpec((1, ATTN_WIDTH), const),
                  pl.BlockSpec((1, ATTN_WIDTH), const),
                  pl.BlockSpec((1, MEM_WIDTH), const),
                  pl.BlockSpec((ATTN_WIDTH, ATTN_WIDTH), const),
                  pl.BlockSpec((MEM_WIDTH, MEM_WIDTH), const)],
        out_specs=out_specs,
        out_shape=out_shape,
        scratch_shapes=[pltpu.VMEM((N_PAIRS, tm, LANES), F32)] * 2,
        compiler_params=_cparams(("parallel",)),
        name="proj",
    )(x2d, attn_norm, w_in, gq, gk, gm, _group_ones(ATTN_WIDTH), _group_ones(MEM_WIDTH))


def _attn_kernel(q1, k1c, k1p, v1c, v1p, q4, k4c, k4p, v4c, v4p, q16, k16c, k16p, v16c, v16p,
                 bias_ref, o_ref, obuf, mbuf, lbuf):
    c = pl.program_id(2)
    lane = lax.broadcasted_iota(I32, (BLOCK, LANES), 1)
    is_a = lane < HEAD_DIM
    lane_row = lax.broadcasted_iota(I32, (1, LANES), 1)
    mask_a = jnp.where(lane_row < HEAD_DIM, 1.0, 0.0).astype(BF16)
    mask_b = jnp.where(lane_row < HEAD_DIM, 0.0, 1.0).astype(BF16)
    bias_full = bias_ref[0]
    bias_first = jnp.where(c > 0, bias_full, bias_ref[1])

    def tile(q_t, kp_t, kc_t, vp_t, vc_t, bias):
        lhs = jnp.concatenate([q_t * mask_a, q_t * mask_b], axis=0)
        keys = jnp.concatenate([kp_t, kc_t], axis=0)
        s = lax.dot_general(lhs, keys, (((1,), (1,)), ((), ())), preferred_element_type=F32) + bias
        m = jnp.max(s, axis=-1, keepdims=True)
        p = jnp.exp(s - m)
        l = jnp.sum(p, axis=-1, keepdims=True)
        vals = jnp.concatenate([vp_t, vc_t], axis=0)
        pv = jnp.dot(p.astype(BF16), vals, preferred_element_type=F32)
        return (jnp.where(is_a, pv[:BLOCK], pv[BLOCK:]), jnp.where(is_a, m[:BLOCK], m[BLOCK:]),
                jnp.where(is_a, l[:BLOCK], l[BLOCK:]))

    for pat, (d, q, kc, kp, vc, vp) in enumerate(((16, q16, k16c, k16p, v16c, v16p), (4, q4, k4c, k4p, v4c, v4p))):
        nblk = SUPER // (BLOCK * d)
        for r in range(d):
            cols = pl.ds(r * LANES, LANES)
            for jb in range(nblk):
                cur = pl.ds(jb * BLOCK, BLOCK)
                if jb == 0:
                    kp_t, vp_t, bias = kp[:, cols], vp[:, cols], bias_first
                else:
                    prev = pl.ds((jb - 1) * BLOCK, BLOCK)
                    kp_t, vp_t, bias = kc[prev, cols], vc[prev, cols], bias_full
                o_t, m_t, l_t = tile(q[cur, cols], kp_t, kc[cur, cols], vp_t, vc[cur, cols], bias)
                rows = pl.ds(jb * BLOCK * d + r, BLOCK, stride=d)
                obuf[pat, rows, :] = o_t
                mbuf[pat, rows, :] = m_t
                lbuf[pat, rows, :] = l_t
    for jb in range(SUPER // BLOCK):
        cur = pl.ds(jb * BLOCK, BLOCK)
        if jb == 0:
            kp_t, vp_t, bias = k1p[...], v1p[...], bias_first
        else:
            prev = pl.ds((jb - 1) * BLOCK, BLOCK)
            kp_t, vp_t, bias = k1c[prev, :], v1c[prev, :], bias_full
        o0, m0, l0 = tile(q1[cur, :], kp_t, k1c[cur, :], vp_t, v1c[cur, :], bias)
        m1, m2 = mbuf[0, cur, :], mbuf[1, cur, :]
        top = jnp.maximum(jnp.maximum(m0, m1), m2)
        w0, w1, w2 = jnp.exp(m0 - top), jnp.exp(m1 - top), jnp.exp(m2 - top)
        num = w0 * o0 + w1 * obuf[0, cur, :] + w2 * obuf[1, cur, :]
        den = w0 * l0 + w1 * lbuf[0, cur, :] + w2 * lbuf[1, cur, :]
        o_ref[cur, :] = (num * (1.0 / den)).astype(BF16)


def _band_bias():
    qi = jnp.arange(BLOCK)[:, None]
    kj = jnp.arange(2 * BLOCK)[None, :]
    dist = qi + BLOCK - kj
    in_band = (dist >= 0) & (dist <= BLOCK)
    full = jnp.where(in_band, 0.0, NEG_INF).astype(F32)
    first = jnp.where(in_band & (kj >= BLOCK), 0.0, NEG_INF).astype(F32)
    return jnp.stack([jnp.tile(full, (2, 1)), jnp.tile(first, (2, 1))])


def _attn(q1, k1, v1, q4, k4, v4, q16, k16, v16, B, S):
    nsup = S // SUPER

    def specs(d):
        rows = SUPER // d
        per = rows // BLOCK
        cur = pl.BlockSpec((None, None, rows, d * LANES), lambda b, hp, c: (b, hp, c, 0))
        prev = pl.BlockSpec((None, None, BLOCK, d * LANES),
                            lambda b, hp, c: (b, hp, jnp.maximum(per * c - 1, 0), 0))
        return [cur, cur, prev, cur, prev]

    return pl.pallas_call(
        _attn_kernel,
        grid=(B, N_PAIRS, nsup),
        in_specs=specs(1) + specs(4) + specs(16) + [
            pl.BlockSpec((2, 2 * BLOCK, 2 * BLOCK), lambda b, hp, c: (0, 0, 0))],
        out_specs=pl.BlockSpec((None, SUPER, LANES), lambda b, hp, c: (b, c, hp)),
        out_shape=jax.ShapeDtypeStruct((B, S, ATTN_WIDTH), BF16),
        scratch_shapes=[pltpu.VMEM((2, SUPER, LANES), F32)] * 3,
        compiler_params=_cparams(("parallel", "parallel", "parallel")),
        name="dilated_attn",
    )(q1, k1, k1, v1, v1, q4, k4, k4, v4, v4, q16, k16, k16, v16, v16, _band_bias())


def _memkv_kernel(mem_ref, mn_ref, wkv_ref, gk_ref, o256_ref, km_ref, vm_ref):
    m = mem_ref[...]
    ms = jnp.mean(m * m, axis=-1, keepdims=True)
    mn = (m * lax.rsqrt(ms + EPS) * mn_ref[...]).astype(BF16)
    kv = jnp.dot(mn, wkv_ref[...], preferred_element_type=F32)
    km_ref[...] = _head_norm(kv[:, :MEM_WIDTH], o256_ref, gk_ref).astype(BF16)
    vm_ref[...] = kv[:, MEM_WIDTH:].astype(BF16)


def _memkv(mem, mem_norm, w_mem_kv, gmk, B):
    const = lambda b: (0, 0)
    return pl.pallas_call(
        _memkv_kernel,
        grid=(B,),
        in_specs=[pl.BlockSpec((None, N_MEM, D_MODEL), lambda b: (b, 0, 0)),
                  pl.BlockSpec((1, D_MODEL), const),
                  pl.BlockSpec((D_MODEL, 2 * MEM_WIDTH), const),
                  pl.BlockSpec((1, MEM_WIDTH), const),
                  pl.BlockSpec((MEM_WIDTH, MEM_WIDTH), const)],
        out_specs=[pl.BlockSpec((None, N_MEM, MEM_WIDTH), lambda b: (b, 0, 0))] * 2,
        out_shape=[jax.ShapeDtypeStruct((B, N_MEM, MEM_WIDTH), BF16)] * 2,
        compiler_params=_cparams(("parallel",)),
        name="memkv",
    )(mem, mem_norm, w_mem_kv, gmk, _group_ones(MEM_WIDTH))


def _mix_kernel(tiles_per_seq, ya_ref, u_ref, uh_ref, qm_ref, km_ref, vm_ref, x_ref, pp_ref, ps_ref,
                wo_ref, fn_ref, wr_ref, br_ref, utri_ref,
                x2_ref, h2p_ref, rinfo_ref, rinfot_ref, cnt_ref, base_ref):
    i = pl.program_id(0)
    tm = x_ref.shape[0]
    seq_tile = i % tiles_per_seq

    @pl.when(i == 0)
    def _():
        base_ref[...] = jnp.zeros_like(base_ref)

    u = u_ref[...]
    halo = jnp.where(seq_tile == 0, 0.0, uh_ref[...])
    uu = jnp.concatenate([halo, u], axis=0)
    a1 = uu[1:] + uu[:-1]
    a2 = a1[2:] + a1[:-2]
    a3 = a2[4:] + a2[:-4]
    a4 = a3[8:] + a3[:-8]
    lane_p = lax.broadcasted_iota(I32, (tm, POOL_WIDTH), 1)
    g0, g1, g2 = lane_p < 64, lane_p < 128, lane_p < 192
    wsum = jnp.where(g0, a1[15:], jnp.where(g1, a2[13:], jnp.where(g2, a3[9:], a4[1:])))
    wlen = jnp.where(g0, 2.0, jnp.where(g1, 4.0, jnp.where(g2, 8.0, 16.0)))
    tpos = seq_tile * tm + lax.broadcasted_iota(I32, (tm, POOL_WIDTH), 0) + 1
    cnt = jnp.minimum(tpos.astype(F32), wlen)
    pooled = wsum / cnt - u
    y_pool = jnp.dot(pooled.astype(BF16), pp_ref[...], preferred_element_type=F32) * ps_ref[...]

    lane = lax.broadcasted_iota(I32, (tm, LANES), 1)
    is_a = lane < HEAD_DIM
    lane_row = lax.broadcasted_iota(I32, (1, LANES), 1)
    mask_a = jnp.where(lane_row < HEAD_DIM, 1.0, 0.0).astype(BF16)
    mask_b = jnp.where(lane_row < HEAD_DIM, 0.0, 1.0).astype(BF16)
    y_mem = []
    for pr in range(MEM_WIDTH // LANES):
        cols = slice(pr * LANES, (pr + 1) * LANES)
        qp, kp, vp = qm_ref[:, cols], km_ref[:, cols], vm_ref[:, cols]
        outs = []
        for msk in (mask_a, mask_b):
            s = lax.dot_general(qp * msk, kp, (((1,), (1,)), ((), ())), preferred_element_type=F32)
            m = jnp.max(s, axis=-1, keepdims=True)
            p = jnp.exp(s - m)
            p = p / jnp.sum(p, axis=-1, keepdims=True)
            outs.append(jnp.dot(p.astype(BF16), vp, preferred_element_type=F32))
        y_mem.append(jnp.where(is_a, outs[0], outs[1]))

    rest = jnp.concatenate([y_pool] + y_mem, axis=1).astype(BF16)
    proj = jnp.dot(ya_ref[...], wo_ref[:ATTN_WIDTH, :], preferred_element_type=F32)
    proj += jnp.dot(rest, wo_ref[ATTN_WIDTH:, :], preferred_element_type=F32)
    x2 = x_ref[...] + proj
    x2_ref[...] = x2

    ms = jnp.mean(x2 * x2, axis=-1, keepdims=True)
    h2 = (x2 * lax.rsqrt(ms + EPS) * fn_ref[...]).astype(BF16)
    bits = lax.bitcast_convert_type(h2.astype(F32), U32)
    h2p_ref[...] = bits[:, :PACKED] | (bits[:, PACKED:] >> 16)

    logits = jnp.dot(h2, wr_ref[...], preferred_element_type=F32) + br_ref[...]
    lt = jnp.transpose(logits)
    ninf = -jnp.inf
    sub8 = lax.broadcasted_iota(I32, (EXPERTS_PER_GROUP, tm), 0).astype(F32)
    lg = lt[:N_GROUPS]
    mg = jnp.max(lg, axis=0, keepdims=True)
    g_sel = jnp.min(jnp.where(lg == mg, sub8, float(N_GROUPS)), axis=0, keepdims=True)
    w_g = 1.0 / jnp.sum(jnp.exp(lg - mg), axis=0, keepdims=True)
    le = lt[EXPERT_LANE0:EXPERT_LANE0 + EXPERTS_PER_GROUP]
    for g in range(1, N_GROUPS):
        lo_g = EXPERT_LANE0 + g * EXPERTS_PER_GROUP
        le = jnp.where(g_sel == float(g), lt[lo_g:lo_g + EXPERTS_PER_GROUP], le)
    v1 = jnp.max(le, axis=0, keepdims=True)
    i1 = jnp.min(jnp.where(le == v1, sub8, float(EXPERTS_PER_GROUP)), axis=0, keepdims=True)
    le2 = jnp.where(sub8 == i1, ninf, le)
    v2 = jnp.max(le2, axis=0, keepdims=True)
    i2 = jnp.min(jnp.where(le2 == v2, sub8, float(EXPERTS_PER_GROUP)), axis=0, keepdims=True)
    e21 = jnp.exp(v2 - v1)
    gate1 = w_g / (1.0 + e21)
    gate2 = w_g * e21 / (1.0 + e21)
    e1 = g_sel * float(EXPERTS_PER_GROUP) + i1
    e2 = g_sel * float(EXPERTS_PER_GROUP) + i2

    sub64 = lax.broadcasted_iota(I32, (N_EXPERTS, tm), 0).astype(F32)
    hot1 = sub64 == e1
    hot2 = sub64 == e2
    onehot = jnp.where(hot1 | hot2, 1.0, 0.0)
    prefix = jnp.dot(onehot.astype(BF16), utri_ref[...], preferred_element_type=F32)
    tot = prefix + base_ref[...]
    rank1 = jnp.sum(jnp.where(hot1, tot, 0.0), axis=0, keepdims=True)
    rank2 = jnp.sum(jnp.where(hot2, tot, 0.0), axis=0, keepdims=True)
    base_new = base_ref[...] + jnp.sum(onehot, axis=1, keepdims=True)
    base_ref[...] = base_new
    cnt_ref[...] = base_new

    rt = jnp.where(sub8 == 0.0, e1, jnp.where(sub8 == 1.0, e2, jnp.where(sub8 == 2.0, rank1, jnp.where(
        sub8 == 3.0, rank2, jnp.where(sub8 == 4.0, gate1, jnp.where(sub8 == 5.0, gate2, 0.0))))))
    rinfot_ref[...] = rt
    rinfo_ref[...] = jnp.transpose(jnp.concatenate([rt, jnp.zeros((LANES - 8, tm), F32)], axis=0))


def _mix(ya, u, qm, km, vm, x2d, pool_bd, pool_scale, w_out, ffn_norm, w_r, b_r, B, S):
    T = B * S
    tm = TM_MIX
    tps = S // tm
    hb = tm // HALO
    const = lambda i: (0, 0)
    utri = (jnp.arange(tm)[:, None] < jnp.arange(tm)[None, :]).astype(BF16)
    return pl.pallas_call(
        functools.partial(_mix_kernel, tps),
        grid=(T // tm,),
        in_specs=[pl.BlockSpec((tm, ATTN_WIDTH), lambda i: (i, 0)),
                  pl.BlockSpec((tm, POOL_WIDTH), lambda i: (i, 0)),
                  pl.BlockSpec((HALO, POOL_WIDTH), lambda i: (jnp.maximum(i * hb - 1, 0), 0)),
                  pl.BlockSpec((tm, MEM_WIDTH), lambda i: (i, 0)),
                  pl.BlockSpec((None, N_MEM, MEM_WIDTH), lambda i: (i // tps, 0, 0)),
                  pl.BlockSpec((None, N_MEM, MEM_WIDTH), lambda i: (i // tps, 0, 0)),
                  pl.BlockSpec((tm, D_MODEL), lambda i: (i, 0)),
                  pl.BlockSpec((POOL_WIDTH, POOL_WIDTH), const),
                  pl.BlockSpec((1, POOL_WIDTH), const),
                  pl.BlockSpec((D_MODEL, D_MODEL), const),
                  pl.BlockSpec((1, D_MODEL), const),
                  pl.BlockSpec((D_MODEL, LANES), const),
                  pl.BlockSpec((1, LANES), const),
                  pl.BlockSpec((tm, tm), const)],
        out_specs=[pl.BlockSpec((tm, D_MODEL), lambda i: (i, 0)),
                   pl.BlockSpec((tm, PACKED), lambda i: (i, 0)),
                   pl.BlockSpec((tm, LANES), lambda i: (i, 0)),
                   pl.BlockSpec((8, tm), lambda i: (0, i)),
                   pl.BlockSpec((N_EXPERTS, 1), const)],
        out_shape=[jax.ShapeDtypeStruct((T, D_MODEL), F32),
                   jax.ShapeDtypeStruct((T, PACKED), U32),
                   jax.ShapeDtypeStruct((T, LANES), F32),
                   jax.ShapeDtypeStruct((8, T), F32),
                   jax.ShapeDtypeStruct((N_EXPERTS, 1), F32)],
        scratch_shapes=[pltpu.VMEM((N_EXPERTS, 1), F32)],
        compiler_params=_cparams(("arbitrary",)),
        name="mix_router",
    )(ya, u, u, qm, km, vm, x2d, pool_bd, pool_scale, w_out, ffn_norm, w_r, b_r, utri)


def _dest_kernel(rt_ref, ps_ref, d_ref):
    tn = rt_ref.shape[1]
    sub = lax.broadcasted_iota(I32, (N_EXPERTS, tn), 0).astype(F32)
    ps = ps_ref[...]
    rows = []
    for k in range(2):
        e = rt_ref[k:k + 1, :]
        start = jnp.sum(jnp.where(sub == e, ps, 0.0), axis=0, keepdims=True)
        rows.append(start + rt_ref[2 + k:3 + k, :])
    word = rows[0].astype(I32) | (rows[1].astype(I32) << 16)
    d_ref[...] = jnp.concatenate([word, jnp.zeros((7, tn), I32)], axis=0)


def _dest(rinfot, pstart_col, T):
    tn = 2048
    return pl.pallas_call(
        _dest_kernel,
        grid=(T // tn,),
        in_specs=[pl.BlockSpec((8, tn), lambda i: (0, i)),
                  pl.BlockSpec((N_EXPERTS, 1), lambda i: (0, 0))],
        out_specs=pl.BlockSpec((8, tn), lambda i: (0, i)),
        out_shape=jax.ShapeDtypeStruct((8, T), I32),
        compiler_params=_cparams(("parallel",)),
        name="dest_rows",
    )(rinfot, pstart_col)


def _row_map(d_ref, inv_ref):
    T = d_ref.shape[0]
    unroll = 8

    def body(g, carry):
        t0 = g * unroll
        words = [d_ref[t0 + u] for u in range(unroll)]
        for u, w in enumerate(words):
            inv_ref[w & 0xFFFF] = t0 + u
            inv_ref[lax.shift_right_logical(w, 16)] = T + t0 + u
        return carry

    lax.fori_loop(0, T // unroll, body, 0)


def _moe_kernel(be_ref, nu_ref, first_ref, nxt_ref, ws_ref, d_ref,
                h_hbm, w1_hbm, w3_hbm, w2_hbm, fill_hbm, o_hbm,
                tab, xbuf0, xbuf1, a0, b0, a1, b1, ybuf, w1s, w3s, w2s, w1b, w3b, w2b, inv_ref, tsem, ssem, wsem):
    j = pl.program_id(0)
    nu = nu_ref[0]
    T = h_hbm.shape[0]
    tr = TR_MOE
    n_tiles = be_ref.shape[0]
    xbuf = (xbuf0, xbuf1)
    abuf = ((a0, b0), (a1, b1))

    def tile_rows(ref, row, n=1):
        return ref.at[pl.ds(pl.multiple_of(row * ROW_SUB, ROW_SUB), n * ROW_SUB)]

    def gather_tile(tile, slot):
        for i in range(tr):
            v = inv_ref[tile * tr + i]
            if T & (T - 1) == 0:
                tok = v & (T - 1)
            else:
                tok = jnp.where(v >= 2 * T, 0, jnp.where(v >= T, v - T, v))
            xbuf[slot][pl.ds(i, 1), :] = tab[pl.ds(tok, 1), :]

    def scatter_row(tile, slot, i):
        v = inv_ref[tile * tr + i]
        return pltpu.make_async_copy(tile_rows(ybuf, slot * tr + i), tile_rows(o_hbm, v), ssem.at[slot])

    def scatter_wait(slot):
        pltpu.make_async_copy(tile_rows(ybuf, slot * tr, tr), tile_rows(o_hbm, 0, tr), ssem.at[slot]).wait()

    def weight_copies(e, slot):
        return [pltpu.make_async_copy(src.at[e], dst.at[slot], wsem.at[slot])
                for src, dst in ((w1_hbm, w1s), (w3_hbm, w3s), (w2_hbm, w2s))]

    def up_project(slot):
        w = xbuf[slot][...]
        lo = lax.bitcast_convert_type(w & jnp.uint32(0xFFFF0000), F32).astype(BF16)
        hi = lax.bitcast_convert_type(w << 16, F32).astype(BF16)
        for dst, wb in zip(abuf[slot], (w1b, w3b)):
            dst[...] = (jnp.dot(lo, wb[:PACKED, :], preferred_element_type=F32)
                        + jnp.dot(hi, wb[PACKED:, :], preferred_element_type=F32))

    @pl.when(j < nu)
    def _():
        s = j & 1
        jn = jnp.minimum(j + 1, n_tiles - 1)

        @pl.when(j == 0)
        def _():
            fill = pltpu.make_async_copy(fill_hbm, inv_ref, tsem)
            fill.start()
            fill.wait()
            ybuf[...] = jnp.zeros_like(ybuf)
            scratch_rows = pltpu.make_async_copy(ybuf, tile_rows(o_hbm, 2 * T, 2 * tr), tsem)
            scratch_rows.start()
            scratch_rows.wait()
            table = pltpu.make_async_copy(h_hbm, tab, tsem)
            table.start()
            for cp in weight_copies(be_ref[0], 0):
                cp.start(priority=WEIGHT_DMA_PRIORITY)
            _row_map(d_ref, inv_ref)
            table.wait()
            gather_tile(0, 0)
            gather_tile(jnp.minimum(1, n_tiles - 1), 1)
            for cp in weight_copies(be_ref[0], 0):
                cp.wait()

            @pl.when(nxt_ref[0] >= 0)
            def _():
                for cp in weight_copies(nxt_ref[0], 1):
                    cp.start(priority=WEIGHT_DMA_PRIORITY)

            w1b[...] = w1s[0].astype(BF16)
            w3b[...] = w3s[0].astype(BF16)
            w2b[...] = w2s[0].astype(BF16)
            up_project(0)

        @pl.when((j >= 1) & (first_ref[j] == 1))
        def _():
            w2b[...] = w2s[ws_ref[j]].astype(BF16)

        @pl.when((j + 1 < nu) & (first_ref[jn] == 1))
        def _():
            wslot = ws_ref[jn]
            for cp in weight_copies(be_ref[jn], wslot):
                cp.wait()
            e_next = nxt_ref[jn]

            @pl.when(e_next >= 0)
            def _():
                for cp in weight_copies(e_next, 1 - wslot):
                    cp.start(priority=WEIGHT_DMA_PRIORITY)

            w1b[...] = w1s[wslot].astype(BF16)
            w3b[...] = w3s[wslot].astype(BF16)

        @pl.when(j >= 2)
        def _():
            scatter_wait(s)

        for slot in range(2):
            @pl.when(s == slot)
            def _():
                gather_tile(jnp.minimum(j + 2, n_tiles - 1), slot)
                a = abuf[slot][0][...]
                b = abuf[slot][1][...]
                up_project(1 - slot)
                hmid = (a / (1.0 + jnp.exp(-a)) * b).astype(BF16)
                y = jnp.dot(hmid, w2b[...], preferred_element_type=F32)
                for c in range(ROW_SUB):
                    ybuf[pl.ds(slot * (tr * ROW_SUB) + c, tr, stride=ROW_SUB), :] = y[:, c * LANES:(c + 1) * LANES]
                for i in range(tr):
                    scatter_row(j, slot, i).start()

        @pl.when(j == nu - 1)
        def _():
            scatter_wait(s)

            @pl.when(j >= 1)
            def _():
                scatter_wait(1 - s)


def _moe(blk_exp, n_used, first, nxt, wslot, dword, h2p, w1, w3, w2):
    T = h2p.shape[0]
    tr = TR_MOE
    n_tiles = blk_exp.shape[0]
    R = n_tiles * tr
    assert R <= 1 << 16
    fill = 2 * T + jnp.arange(R, dtype=I32) % (2 * tr)
    any_spec = pl.BlockSpec(memory_space=pl.ANY)
    return pl.pallas_call(
        _moe_kernel,
        grid_spec=pltpu.PrefetchScalarGridSpec(
            num_scalar_prefetch=6,
            grid=(n_tiles,),
            in_specs=[any_spec] * 5,
            out_specs=any_spec,
            scratch_shapes=[pltpu.VMEM((T, PACKED), U32),
                            pltpu.VMEM((tr, PACKED), U32),
                            pltpu.VMEM((tr, PACKED), U32),
                            pltpu.VMEM((tr, EXPERT_HIDDEN), F32),
                            pltpu.VMEM((tr, EXPERT_HIDDEN), F32),
                            pltpu.VMEM((tr, EXPERT_HIDDEN), F32),
                            pltpu.VMEM((tr, EXPERT_HIDDEN), F32),
                            pltpu.VMEM((2 * tr * ROW_SUB, LANES), F32),
                            pltpu.VMEM((2, D_MODEL, EXPERT_HIDDEN), F32),
                            pltpu.VMEM((2, D_MODEL, EXPERT_HIDDEN), F32),
                            pltpu.VMEM((2, EXPERT_HIDDEN, D_MODEL), F32),
                            pltpu.VMEM((D_MODEL, EXPERT_HIDDEN), BF16),
                            pltpu.VMEM((D_MODEL, EXPERT_HIDDEN), BF16),
                            pltpu.VMEM((EXPERT_HIDDEN, D_MODEL), BF16),
                            pltpu.SMEM((R,), I32),
                            pltpu.SemaphoreType.DMA(()),
                            pltpu.SemaphoreType.DMA((2,)),
                            pltpu.SemaphoreType.DMA((2,))]),
        out_shape=jax.ShapeDtypeStruct(((2 * T + 2 * tr) * ROW_SUB, LANES), F32),
        compiler_params=pltpu.CompilerParams(dimension_semantics=("arbitrary",), vmem_limit_bytes=MOE_VMEM_LIMIT),
        name="moe_experts",
    )(blk_exp, n_used, first, nxt, wslot, dword, h2p, w1, w3, w2, fill)


def _combine_kernel(x2_ref, rinfo_ref, y1_ref, y2_ref, o_ref):
    g1 = rinfo_ref[:, 4:5]
    g2 = rinfo_ref[:, 5:6]
    tc = x2_ref.shape[0]
    for c in range(ROW_SUB):
        cols = slice(c * LANES, (c + 1) * LANES)
        rows = pl.ds(c, tc, stride=ROW_SUB)
        o_ref[:, cols] = x2_ref[:, cols] + (g1 * y1_ref[rows, :] + g2 * y2_ref[rows, :])


def _combine(x2, rinfo, y2slot):
    T = x2.shape[0]
    tc = TC_COMBINE
    return pl.pallas_call(
        _combine_kernel,
        grid=(T // tc,),
        in_specs=[pl.BlockSpec((tc, D_MODEL), lambda i: (i, 0)),
                  pl.BlockSpec((tc, LANES), lambda i: (i, 0)),
                  pl.BlockSpec((tc * ROW_SUB, LANES), lambda i: (i, 0)),
                  pl.BlockSpec((tc * ROW_SUB, LANES), lambda i: (T // tc + i, 0))],
        out_specs=pl.BlockSpec((tc, D_MODEL), lambda i: (i, 0)),
        out_shape=jax.ShapeDtypeStruct((T, D_MODEL), F32),
        compiler_params=_cparams(("parallel",)),
        name="combine",
    )(x2, rinfo, y2slot, y2slot)


def _layer(x, mem, attn_norm, w_in, q_norm, k_norm, pool_proj, pool_scale, mem_norm, w_mem_kv,
           mq_norm, mk_norm, w_out, ffn_norm, w_group, b_group, w_router, b_router, w1, w3, w2):
    B, S, D = x.shape
    T = B * S
    assert D == D_MODEL and S % SUPER == 0 and T % TM_PROJ == 0
    x2d = x.reshape(T, D)
    row = lambda v: v.reshape(1, -1).astype(F32)
    scale = HEAD_DIM ** -0.5
    gq = row(jnp.tile(q_norm, ATTN_WIDTH // HEAD_DIM) * scale)
    gk = row(jnp.tile(k_norm, ATTN_WIDTH // HEAD_DIM))
    gmq = row(jnp.tile(mq_norm, MEM_WIDTH // HEAD_DIM) * scale)
    gmk = row(jnp.tile(mk_norm, MEM_WIDTH // HEAD_DIM))

    (q1, k1, v1, q4, k4, v4, q16, k16, v16, u, qm) = _proj(
        x2d, row(attn_norm), w_in.astype(BF16), gq, gk, gmq, B, S)
    ya = _attn(q1, k1, v1, q4, k4, v4, q16, k16, v16, B, S).reshape(T, ATTN_WIDTH)
    km, vm = _memkv(mem, row(mem_norm), w_mem_kv.astype(BF16), gmk, B)

    pool_bd = jax.scipy.linalg.block_diag(*[pool_proj[g] for g in range(pool_proj.shape[0])]).astype(BF16)
    w_r = jnp.zeros((D, LANES), F32)
    w_r = w_r.at[:, :N_GROUPS].set(w_group)
    w_r = w_r.at[:, EXPERT_LANE0:].set(jnp.transpose(w_router, (1, 0, 2)).reshape(D, N_EXPERTS))
    b_r = jnp.zeros((1, LANES), F32).at[0, :N_GROUPS].set(b_group).at[0, EXPERT_LANE0:].set(b_router.reshape(-1))
    x2, h2p, rinfo, rinfot, counts = _mix(ya, u, qm, km, vm, x2d, pool_bd, row(pool_scale),
                                          w_out.astype(BF16), row(ffn_norm), w_r.astype(BF16), b_r, B, S)

    pstart, sched = _tile_schedule(counts, T)
    dest = _dest(rinfot, pstart.astype(F32).reshape(N_EXPERTS, 1), T)
    y2slot = _moe(*sched, dest[0], h2p, w1, w3, w2)
    out = _combine(x2, rinfo, y2slot)
    return out.reshape(B, S, D)


def _tile_schedule(counts, T):
    tr = TR_MOE
    n_tiles = (2 * T + N_EXPERTS * tr) // tr
    cnt = counts[:, 0].astype(I32)
    padded = ((cnt + tr - 1) // tr) * tr
    pend = jnp.cumsum(padded)
    pstart = pend - padded
    n_used = (pend[-1] // tr).astype(I32).reshape(1)
    tiles = jnp.arange(n_tiles, dtype=I32)
    tile_row = jnp.minimum(tiles, n_used[0] - 1) * tr
    blk_exp = jnp.minimum(jnp.sum(tile_row[:, None] >= pend[None, :], axis=1), N_EXPERTS - 1).astype(I32)
    in_use = tiles < n_used[0]
    first = in_use & ((tiles == 0) | (blk_exp != jnp.roll(blk_exp, 1)))
    run_idx = jnp.cumsum(first.astype(I32)) - 1
    used = cnt > 0
    exp_of_run = jnp.argsort(jnp.logical_not(used), stable=True).astype(I32)
    nxt = jnp.where(run_idx + 1 < jnp.sum(used), exp_of_run[jnp.clip(run_idx + 1, 0, N_EXPERTS - 1)], -1)
    sched = (blk_exp, n_used, first.astype(I32), nxt.astype(I32), (run_idx & 1).astype(I32))
    return pstart, sched


def kernel(x, mem, attn_norm, w_in, q_norm, k_norm, pool_proj, pool_scale, mem_norm, w_mem_kv, mq_norm, mk_norm,
           w_out, ffn_norm, w_group, b_group, w_router, b_router, w1, w3, w2):
    for l in range(attn_norm.shape[0]):
        x = _layer(x, mem, attn_norm[l], w_in[l], q_norm[l], k_norm[l], pool_proj[l], pool_scale[l],
                   mem_norm[l], w_mem_kv[l], mq_norm[l], mk_norm[l], w_out[l], ffn_norm[l],
                   w_group[l], b_group[l], w_router[l], b_router[l], w1[l], w3[l], w2[l])
    return x
```
